```python
import math
import jax, jax.numpy as jnp
from jax import lax
import numpy as np

D_MODEL = 1024
BATCH = 32
SEQ = 256
DEPTH = 1
DEC_BATCH = 2
DEC_SEQ = 2048
PAST_LEN = 512

GRID_W = 64
MLA_HEADS = 8
QK_NOPE = 128
QK_ROPE = 64
V_HEAD = 128
Q_LORA = 256
KV_LORA = 256
ROPE_BASE = 10000.0
ATTN_BLOCK = 128
SSD_HEADS = 16
SSD_HEADDIM = 64
SSD_INNER = SSD_HEADS * SSD_HEADDIM
SSD_GROUPS = 4
SSD_STATE = 64
SSD_CONV = 3
SSD_CHUNK = 128
SSD_CONV_DIM = SSD_INNER + 2 * SSD_GROUPS * SSD_STATE
D_FF = 2816
FFN_CONV = 3
EPS = 1e-6

IN_SPLITS = (Q_LORA, KV_LORA, QK_ROPE, SSD_INNER, SSD_INNER, SSD_GROUPS * SSD_STATE,
             SSD_GROUPS * SSD_STATE, 2 * SSD_HEADS, D_MODEL, D_MODEL)
IN_COLS = Q_LORA + KV_LORA + QK_ROPE + 2 * SSD_INNER + 2 * SSD_GROUPS * SSD_STATE + 2 * SSD_HEADS + 2 * D_MODEL

kernel_name = "hybrid_mla_ssd_convffn_diffusion_step"


def rmsnorm(x, g):
    x32 = x.astype(jnp.float32)
    y = x32 * lax.rsqrt(jnp.mean(x32 * x32, axis=-1, keepdims=True) + EPS)
    return (y * g.astype(jnp.float32)).astype(x.dtype)


def dwconv_centred(x, w, b):
    K = w.shape[0]
    pad = K // 2
    L = x.shape[1]
    xp = jnp.pad(x, ((0, 0), (pad, pad), (0, 0)))
    out = xp[:, 0:L] * w[0]
    for k in range(1, K):
        out = out + xp[:, k:k + L] * w[k]
    return out + b


def axial_angles(L):
    rows = L // GRID_W
    t = jnp.arange(rows * GRID_W)
    row = (t // GRID_W).astype(jnp.float32)
    col = (t % GRID_W).astype(jnp.float32)
    n = QK_ROPE // 4
    inv = ROPE_BASE ** (-jnp.arange(n, dtype=jnp.float32) / n)
    return jnp.stack([row[:, None] * inv, col[:, None] * inv], axis=1)


def rope_2d(x, ang):
    n = QK_ROPE // 4
    xr = x.reshape(x.shape[:-1] + (2, 2, n)).astype(jnp.float32)
    cos, sin = jnp.cos(ang), jnp.sin(ang)
    x0, x1 = xr[..., 0, :], xr[..., 1, :]
    out = jnp.stack([x0 * cos - x1 * sin, x0 * sin + x1 * cos], axis=-2)
    return out.reshape(x.shape).astype(x.dtype)


def block_attention(q, k, v):
    b, Lq, H, Dk = q.shape
    nb = Lq // ATTN_BLOCK
    scale = 1.0 / math.sqrt(QK_NOPE + QK_ROPE)
    qb = q.reshape(b, nb, ATTN_BLOCK, H, Dk).transpose(1, 0, 2, 3, 4)

    def one(qblk):
        s = jnp.einsum('bqhd,bkhd->bhqk', qblk, k).astype(jnp.float32) * scale
        p = jax.nn.softmax(s, axis=-1).astype(v.dtype)
        return jnp.einsum('bhqk,bkhd->bqhd', p, v)

    out = lax.map(one, qb)
    return out.transpose(1, 0, 2, 3, 4).reshape(b, Lq, H, v.shape[-1])


def mla_kv(ckv_n, kr, w_ukv):
    b, L, _ = ckv_n.shape
    kv = (ckv_n @ w_ukv).reshape(b, L, MLA_HEADS, QK_NOPE + V_HEAD)
    k_nope, v = kv[..., :QK_NOPE], kv[..., QK_NOPE:]
    k = jnp.concatenate([k_nope, jnp.broadcast_to(kr[:, :, None, :], (b, L, MLA_HEADS, QK_ROPE))], axis=-1)
    return k, v


def ssd_chunked_scan(x, dt, A, Bm, Cm, h0):
    b, L, H, P = x.shape
    Q = SSD_CHUNK
    nc = L // Q
    rep = H // SSD_GROUPS
    f32 = jnp.float32
    Bh = jnp.repeat(Bm.astype(f32), rep, axis=2).reshape(b, nc, Q, H, SSD_STATE)
    Ch = jnp.repeat(Cm.astype(f32), rep, axis=2).reshape(b, nc, Q, H, SSD_STATE)
    dtc = dt.astype(f32).reshape(b, nc, Q, H)
    xdt = x.astype(f32).reshape(b, nc, Q, H, P) * dtc[..., None]
    acum = jnp.cumsum(dtc * A.astype(f32), axis=2)
    seg = acum[:, :, :, None, :] - acum[:, :, None, :, :]
    causal = jnp.tril(jnp.ones((Q, Q), dtype=bool))[:, :, None]
    decay = jnp.where(causal, jnp.exp(jnp.where(causal, seg, 0.0)), 0.0)
    scores = jnp.einsum('bcihn,bcjhn->bcijh', Ch, Bh)
    y_diag = jnp.einsum('bcijh,bcjhp->bcihp', scores * decay, xdt)
    decay_end = jnp.exp(acum[:, :, -1:, :] - acum)
    chunk_states = jnp.einsum('bcjhn,bcjh,bcjhp->bchpn', Bh, decay_end, xdt)
    chunk_decay = jnp.exp(acum[:, :, -1, :])

    def step(h, inp):
        dec, st = inp
        return h * dec[:, :, None, None] + st, h

    hT, h_in = lax.scan(step, h0.astype(f32),
                        (chunk_decay.transpose(1, 0, 2), chunk_states.transpose(1, 0, 2, 3, 4)))
    h_in = h_in.transpose(1, 0, 2, 3, 4)
    y_off = jnp.einsum('bcihn,bchpn,bcih->bcihp', Ch, h_in, jnp.exp(acum))
    y = (y_diag + y_off).reshape(b, L, H, P)
    return y.astype(x.dtype), hT.astype(x.dtype)


def trunk_layer(x, c_vec, lp, latent, ctx):
    b, L, _ = x.shape
    mod = (jax.nn.silu(c_vec) @ lp['w_ada'] + lp['b_ada'])[:, None, :]
    sh1, sc1, g1, sh2, sc2, g2 = jnp.split(mod, 6, axis=-1)
    h = rmsnorm(x, lp['norm_attn_g']) * (1 + sc1) + sh1
    proj = h @ lp['w_in']
    cq, ckv, kr, z, xs, Bm, Cm, dt, gm, gs = jnp.split(proj, [int(s) for s in np.cumsum(IN_SPLITS)[:-1]], axis=-1)

    q = (rmsnorm(cq, lp['q_norm_g']) @ lp['w_uq']).reshape(b, L, MLA_HEADS, QK_NOPE + QK_ROPE)
    q_nope, q_rope = q[..., :QK_NOPE], q[..., QK_NOPE:]
    ckv_n = rmsnorm(ckv, lp['kv_norm_g'])
    if latent:
        ang = axial_angles(L)
        q_rope = rope_2d(q_rope, ang[:, None])
        kr = rope_2d(kr, ang)
    q = jnp.concatenate([q_nope, q_rope], axis=-1)
    k, v = mla_kv(ckv_n, kr, lp['w_ukv'])
    if ctx is not None:
        k_ctx, v_ctx = mla_kv(ctx[0], ctx[1], lp['w_ukv'])
        k = jnp.concatenate([k_ctx, k], axis=1)
        v = jnp.concatenate([v_ctx, v], axis=1)
    attn = block_attention(q, k, v)
    o_mla = attn.reshape(b, L, MLA_HEADS * V_HEAD) @ lp['w_o_mla']

    xbc = jax.nn.silu(dwconv_centred(jnp.concatenate([xs, Bm, Cm], axis=-1), lp['ssd_conv_w'], lp['ssd_conv_b']))
    xh = xbc[..., :SSD_INNER].reshape(b, L, SSD_HEADS, SSD_HEADDIM)
    Bg = xbc[..., SSD_INNER:SSD_INNER + SSD_GROUPS * SSD_STATE].reshape(b, L, SSD_GROUPS, SSD_STATE)
    Cg = xbc[..., SSD_INNER + SSD_GROUPS * SSD_STATE:].reshape(b, L, SSD_GROUPS, SSD_STATE)
    dt_all = jax.nn.softplus(dt.reshape(b, L, 2, SSD_HEADS).astype(jnp.float32) + lp['ssd_dt_bias'])
    A = -jnp.exp(lp['ssd_A_log'].astype(jnp.float32))
    if ctx is None:
        h0 = jnp.zeros((b, 2, SSD_HEADS, SSD_HEADDIM, SSD_STATE), jnp.float32)
    else:
        h0 = ctx[2]
    y_f, h_f = ssd_chunked_scan(xh, dt_all[:, :, 0], A[0], Bg, Cg, h0[:, 0])
    y_b, h_b = ssd_chunked_scan(jnp.flip(xh, 1), jnp.flip(dt_all[:, :, 1], 1), A[1],
                                jnp.flip(Bg, 1), jnp.flip(Cg, 1), h0[:, 1])
    y = y_f + jnp.flip(y_b, 1) + lp['ssd_D'][:, None] * xh
    y = rmsnorm(y.reshape(b, L, SSD_INNER) * jax.nn.silu(z), lp['ssd_norm_g'])
    o_ssd = y @ lp['w_o_ssd']

    merged = jax.nn.sigmoid(gm) * o_mla + jax.nn.sigmoid(gs) * o_ssd
    x = x + g1 * (merged @ lp['w_out'])

    h2 = rmsnorm(x, lp['norm_ffn_g']) * (1 + sc2) + sh2
    u = dwconv_centred(h2 @ lp['w_up'], lp['ffn_conv_w'], lp['ffn_conv_b'])
    x = x + g2 * ((jax.nn.silu(u[..., :D_FF]) * u[..., D_FF:]) @ lp['w_down'])
    return x, (ckv_n, kr, jnp.stack([h_f, h_b], axis=1))


def setup_inputs(seed: int = 0) -> dict:
    key = jax.random.key(seed)
    ks = iter(jax.random.split(key, 48))

    def nrm(shape, scale):
        return jax.random.normal(next(ks), shape, jnp.float32) * scale

    def gain(shape):
        return 1.0 + nrm(shape, 0.02)

    dt0 = jnp.exp(jax.random.uniform(next(ks), (DEPTH, 2, SSD_HEADS), jnp.float32,
                                     minval=math.log(1e-3), maxval=math.log(1e-1)))
    return {
        "x_prompt": nrm((BATCH, SEQ, D_MODEL), 1.0),
        "x_sample": nrm((DEC_BATCH, DEC_SEQ, D_MODEL), 1.0),
        "c": nrm((DEC_BATCH, D_MODEL), 1.0),
        "cache_ckv": nrm((DEC_BATCH, DEPTH, PAST_LEN, KV_LORA), 1.0),
        "cache_krope": nrm((DEC_BATCH, DEPTH, PAST_LEN, QK_ROPE), 1.0),
        "state_ssd": nrm((DEC_BATCH, DEPTH, 2, SSD_HEADS, SSD_HEADDIM, SSD_STATE), 0.1),
        "c_ctx": nrm((D_MODEL,), 1.0),
        "w_ada": nrm((DEPTH, D_MODEL, 6 * D_MODEL), D_MODEL ** -0.5),
        "b_ada": nrm((DEPTH, 6 * D_MODEL), 0.02),
        "norm_attn_g": gain((DEPTH, D_MODEL)),
        "w_in": nrm((DEPTH, D_MODEL, IN_COLS), D_MODEL ** -0.5),
        "q_norm_g": gain((DEPTH, Q_LORA)),
        "kv_norm_g": gain((DEPTH, KV_LORA)),
        "w_uq": nrm((DEPTH, Q_LORA, MLA_HEADS * (QK_NOPE + QK_ROPE)), Q_LORA ** -0.5),
        "w_ukv": nrm((DEPTH, KV_LORA, MLA_HEADS * (QK_NOPE + V_HEAD)), KV_LORA ** -0.5),
        "w_o_mla": nrm((DEPTH, MLA_HEADS * V_HEAD, D_MODEL), (MLA_HEADS * V_HEAD) ** -0.5),
        "ssd_conv_w": nrm((DEPTH, SSD_CONV, SSD_CONV_DIM), SSD_CONV ** -0.5),
        "ssd_conv_b": nrm((DEPTH, SSD_CONV_DIM), 0.02),
        "ssd_dt_bias": dt0 + jnp.log(-jnp.expm1(-dt0)),
        "ssd_A_log": jnp.log(jax.random.uniform(next(ks), (DEPTH, 2, SSD_HEADS), jnp.float32, minval=1.0, maxval=16.0)),
        "ssd_D": gain((DEPTH, SSD_HEADS)),
        "ssd_norm_g": gain((DEPTH, SSD_INNER)),
        "w_o_ssd": nrm((DEPTH, SSD_INNER, D_MODEL), SSD_INNER ** -0.5),
        "w_out": nrm((DEPTH, D_MODEL, D_MODEL), D_MODEL ** -0.5),
        "norm_ffn_g": gain((DEPTH, D_MODEL)),
        "w_up": nrm((DEPTH, D_MODEL, 2 * D_FF), D_MODEL ** -0.5),
        "ffn_conv_w": nrm((DEPTH, FFN_CONV, 2 * D_FF), FFN_CONV ** -0.5),
        "ffn_conv_b": nrm((DEPTH, 2 * D_FF), 0.02),
        "w_down": nrm((DEPTH, D_FF, D_MODEL), D_FF ** -0.5),
        "final_norm_g": gain((D_MODEL,)),
    }


def reference(x_prompt, x_sample, c, cache_ckv, cache_krope, state_ssd, c_ctx,
              w_ada, b_ada, norm_attn_g, w_in, q_norm_g, kv_norm_g, w_uq, w_ukv, w_o_mla,
              ssd_conv_w, ssd_conv_b, ssd_dt_bias, ssd_A_log, ssd_D, ssd_norm_g, w_o_ssd,
              w_out, norm_ffn_g, w_up, ffn_conv_w, ffn_conv_b, w_down, final_norm_g):
    xp = x_prompt
    xs = x_sample
    new_ckv, new_krope, new_ssd = [], [], []
    for l in range(DEPTH):
        lp = {
            'w_ada': w_ada[l], 'b_ada': b_ada[l], 'norm_attn_g': norm_attn_g[l], 'w_in': w_in[l],
            'q_norm_g': q_norm_g[l], 'kv_norm_g': kv_norm_g[l], 'w_uq': w_uq[l], 'w_ukv': w_ukv[l],
            'w_o_mla': w_o_mla[l], 'ssd_conv_w': ssd_conv_w[l], 'ssd_conv_b': ssd_conv_b[l],
            'ssd_dt_bias': ssd_dt_bias[l], 'ssd_A_log': ssd_A_log[l], 'ssd_D': ssd_D[l],
            'ssd_norm_g': ssd_norm_g[l], 'w_o_ssd': w_o_ssd[l], 'w_out': w_out[l],
            'norm_ffn_g': norm_ffn_g[l], 'w_up': w_up[l], 'ffn_conv_w': ffn_conv_w[l],
            'ffn_conv_b': ffn_conv_b[l], 'w_down': w_down[l],
        }
        xp, (ckv_l, kr_l, st_l) = trunk_layer(xp, c_ctx[None, :], lp, False, None)
        new_ckv.append(ckv_l)
        new_krope.append(kr_l)
        new_ssd.append(st_l)
        xs, _ = trunk_layer(xs, c, lp, True, (cache_ckv[:, l], cache_krope[:, l], state_ssd[:, l]))
    y_prompt = rmsnorm(xp, final_norm_g)
    y_sample = rmsnorm(xs, final_norm_g)
    return (y_prompt, y_sample, jnp.stack(new_ckv, axis=1), jnp.stack(new_krope, axis=1), jnp.stack(new_ssd, axis=1))
```

```python
import functools
import math

import jax
import jax.numpy as jnp
import numpy as np
from jax import lax
from jax.experimental import pallas as pl
from jax.experimental.pallas import tpu as pltpu

F32 = jnp.float32
BF16 = jnp.bfloat16

D_MODEL = 1024
GRID_W = 64
N_HEADS = 8
QK_NOPE = 128
QK_ROPE = 64
V_HEAD = 128
Q_LORA = 256
KV_LORA = 256
ROPE_BASE = 10000.0
SSD_HEADS = 16
SSD_HEADDIM = 64
SSD_INNER = SSD_HEADS * SSD_HEADDIM
SSD_GROUPS = 4
SSD_STATE = 64
SSD_CHUNK = 128
D_FF = 2816
EPS = 1e-6

ROW_GROUP = 2048
IN_TILE = 512
IN_COLS_PADDED = 11 * IN_TILE
FFN_TILE = 256
TOKEN_TILE = 512
ATTN_Q_TILE = 256
VMEM_LIMIT = 56 * 1024 * 1024
NEG_BIG = -1e30


def _sigmoid(x):
    return 1.0 / (1.0 + jnp.exp(-x))


def _silu(x):
    return x * _sigmoid(x)


def _softplus(x):
    return jnp.maximum(x, 0.0) + jnp.log1p(jnp.exp(-jnp.abs(x)))


def _rmsnorm(x, g):
    return x * lax.rsqrt(jnp.mean(x * x, axis=-1, keepdims=True) + EPS) * g


def _dot(a, b):
    return jnp.dot(a, b, preferred_element_type=F32)


def _dot_nt(a, b):
    return lax.dot_general(a, b, (((1,), (1,)), ((), ())), preferred_element_type=F32)


def _params(*sem):
    return pltpu.CompilerParams(dimension_semantics=sem, vmem_limit_bytes=VMEM_LIMIT)


def _dwconv3(u, w_ref, b_ref, seq_len):
    n = u.shape[0]
    pos = lax.broadcasted_iota(jnp.int32, u.shape, 0) & (seq_len - 1)
    prev = jnp.where(pos == 0, 0.0, pltpu.roll(u, 1, 0))
    nxt = jnp.where(pos == seq_len - 1, 0.0, pltpu.roll(u, n - 1, 0))
    return prev * w_ref[0:1, :] + u * w_ref[1:2, :] + nxt * w_ref[2:3, :] + b_ref[...]


def _ada_kernel(c_ref, w_ref, b_ref, o_ref):
    a = _silu(c_ref[...]).astype(BF16)
    o_ref[...] = _dot(a, w_ref[...].astype(BF16)) + b_ref[...]


def _ada(cvec, w_ada, b_ada):
    tn = 1536
    return pl.pallas_call(
        _ada_kernel,
        grid=(6 * D_MODEL // tn,),
        in_specs=[pl.BlockSpec((8, D_MODEL), lambda j: (0, 0)),
                  pl.BlockSpec((D_MODEL, tn), lambda j: (0, j)),
                  pl.BlockSpec((1, tn), lambda j: (0, j))],
        out_specs=pl.BlockSpec((8, tn), lambda j: (0, j)),
        out_shape=jax.ShapeDtypeStruct((8, 6 * D_MODEL), F32),
        compiler_params=_params("arbitrary"),
        name="ada_mod",
    )(cvec, w_ada, b_ada.reshape(1, -1))


Z_BLK, XS_BLK, GM_BLK, GS_BLK = 0, 1, 2, 3
BC_TILE, MLA_TILE, MISC_TILE = 8, 9, 10
DT_BLK = (MISC_TILE * IN_TILE + 2 * QK_ROPE) // 128


def _in_kernel(x_ref, sh_ref, sc_ref, g_ref, w_ref, cw_ref, cb_ref, o_ref, h_scr, *, seq_len):
    j = pl.program_id(1)

    @pl.when(j == 0)
    def _():
        def body(i, carry):
            rows = pl.ds(pl.multiple_of(i * 256, 256), 256)
            h = _rmsnorm(x_ref[rows, :], g_ref[...]) * (1.0 + sc_ref[...]) + sh_ref[...]
            h_scr[rows, :] = h.astype(BF16)
            return carry
        lax.fori_loop(0, ROW_GROUP // 256, body, 0)

    o_ref[...] = _dot(h_scr[...], w_ref[...])

    @pl.when(j <= 1)
    def _():
        o_ref[...] = _silu(o_ref[...])

    @pl.when((j == 2) | (j == 3) | (j == BC_TILE))
    def _():
        o_ref[...] = _silu(_dwconv3(o_ref[...], cw_ref, cb_ref, seq_len))

    @pl.when((j >= 4) & (j <= 7))
    def _():
        o_ref[...] = _sigmoid(o_ref[...])


def _in_proj(x2d, mod48, mod_row, norm_g, w_in_r, conv_w, conv_b, seq_len):
    t = x2d.shape[0]
    n_tiles = IN_COLS_PADDED // IN_TILE
    conv_idx = lambda r, j: (0, jnp.where(j == BC_TILE, 2, jnp.clip(j - 2, 0, 1)))
    return pl.pallas_call(
        functools.partial(_in_kernel, seq_len=seq_len),
        grid=(t // ROW_GROUP, n_tiles),
        in_specs=[pl.BlockSpec((ROW_GROUP, D_MODEL), lambda r, j: (r, 0)),
                  pl.BlockSpec((None, 1, D_MODEL), lambda r, j: (mod_row(r) * 6 + 0, 0, 0)),
                  pl.BlockSpec((None, 1, D_MODEL), lambda r, j: (mod_row(r) * 6 + 1, 0, 0)),
                  pl.BlockSpec((1, D_MODEL), lambda r, j: (0, 0)),
                  pl.BlockSpec((D_MODEL, IN_TILE), lambda r, j: (0, j)),
                  pl.BlockSpec((3, IN_TILE), conv_idx),
                  pl.BlockSpec((1, IN_TILE), conv_idx)],
        out_specs=pl.BlockSpec((ROW_GROUP, IN_TILE), lambda r, j: (r, j)),
        out_shape=jax.ShapeDtypeStruct((t, IN_COLS_PADDED), F32),
        scratch_shapes=[pltpu.VMEM((ROW_GROUP, D_MODEL), BF16)],
        compiler_params=_params("arbitrary", "arbitrary"),
        name="in_proj",
    )(x2d, mod48, mod48, norm_g, w_in_r, conv_w, conv_b)


def _mla_prep_kernel(*refs, latent):
    if latent:
        (p0_ref, p10_ref, qg_ref, kvg_ref, wuq_ref, wukv_ref, cos_ref, sin_ref,
         qn_ref, qr_ref, ckv_ref, kn_ref, v_ref, kr_ref) = refs
    else:
        (p0_ref, p10_ref, qg_ref, kvg_ref, wuq_ref, wukv_ref,
         qn_ref, qr_ref, ckv_ref, kn_ref, v_ref, kr_ref) = refs
    scale = 1.0 / math.sqrt(QK_NOPE + QK_ROPE)
    p0 = p0_ref[...]
    cqn = _rmsnorm(p0[:, :Q_LORA], qg_ref[...]).astype(BF16)
    q = _dot(cqn, wuq_ref[...])
    n_nope = N_HEADS * QK_NOPE
    n_rope = N_HEADS * QK_ROPE
    qn_ref[...] = (q[:, :n_nope] * scale).astype(BF16)
    q_rope = q[:, n_nope:n_nope + n_rope]
    misc = p10_ref[...]
    kr = misc[:, :QK_ROPE]
    if latent:
        cos2, sin2 = cos_ref[...], sin_ref[...]
        cos8 = jnp.concatenate([cos2] * (n_rope // 128), axis=1)
        sin8 = jnp.concatenate([sin2] * (n_rope // 128), axis=1)
        q_rope = q_rope * cos8 + q[:, n_nope + n_rope:] * sin8
        kr = kr * cos2[:, :QK_ROPE] + misc[:, QK_ROPE:2 * QK_ROPE] * sin2[:, :QK_ROPE]
    qr_ref[...] = (q_rope * scale).astype(BF16)
    kr_ref[...] = kr
    ckv_n = _rmsnorm(p0[:, Q_LORA:], kvg_ref[...])
    ckv_ref[...] = ckv_n
    kv = _dot(ckv_n.astype(BF16), wukv_ref[...])
    kn_ref[...] = kv[:, :N_HEADS * QK_NOPE].astype(BF16)
    v_ref[...] = kv[:, N_HEADS * QK_NOPE:].astype(BF16)


def _mla_prep(proj, q_norm_g, kv_norm_g, w_uq_r, w_ukv_r, rope_tables, seq_len):
    t = proj.shape[0]
    tm = TOKEN_TILE
    latent = rope_tables is not None
    in_specs = [pl.BlockSpec((tm, IN_TILE), lambda i: (i, MLA_TILE)),
                pl.BlockSpec((tm, IN_TILE), lambda i: (i, MISC_TILE)),
                pl.BlockSpec((1, Q_LORA), lambda i: (0, 0)),
                pl.BlockSpec((1, KV_LORA), lambda i: (0, 0)),
                pl.BlockSpec(w_uq_r.shape, lambda i: (0, 0)),
                pl.BlockSpec(w_ukv_r.shape, lambda i: (0, 0))]
    args = [proj, proj, q_norm_g, kv_norm_g, w_uq_r, w_ukv_r]
    if latent:
        per_seq = seq_len // tm
        in_specs += [pl.BlockSpec((tm, 128), lambda i: (i % per_seq, 0))] * 2
        args += list(rope_tables)
    widths = (N_HEADS * QK_NOPE, N_HEADS * QK_ROPE, KV_LORA, N_HEADS * QK_NOPE, N_HEADS * V_HEAD, QK_ROPE)
    dtypes = (BF16, BF16, F32, BF16, BF16, F32)
    return pl.pallas_call(
        functools.partial(_mla_prep_kernel, latent=latent),
        grid=(t // tm,),
        in_specs=in_specs,
        out_specs=[pl.BlockSpec((tm, w), lambda i: (i, 0)) for w in widths],
        out_shape=[jax.ShapeDtypeStruct((t, w), d) for w, d in zip(widths, dtypes)],
        compiler_params=_params("arbitrary"),
        name="mla_prep",
    )(*args)


def _kv_up_kernel(c_ref, w_ref, kn_ref, v_ref):
    kv = _dot(c_ref[...].astype(BF16), w_ref[...])
    kn_ref[...] = kv[:, :N_HEADS * QK_NOPE].astype(BF16)
    v_ref[...] = kv[:, N_HEADS * QK_NOPE:].astype(BF16)


def _kv_up(ckv2d, w_ukv_r):
    t = ckv2d.shape[0]
    tm = TOKEN_TILE
    w = N_HEADS * QK_NOPE
    return pl.pallas_call(
        _kv_up_kernel,
        grid=(t // tm,),
        in_specs=[pl.BlockSpec((tm, KV_LORA), lambda i: (i, 0)),
                  pl.BlockSpec(w_ukv_r.shape, lambda i: (0, 0))],
        out_specs=[pl.BlockSpec((tm, w), lambda i: (i, 0))] * 2,
        out_shape=[jax.ShapeDtypeStruct((t, w), BF16)] * 2,
        compiler_params=_params("arbitrary"),
        name="kv_up_ctx",
    )(ckv2d, w_ukv_r)


def _attn_kernel(qn_ref, qr_ref, kn_ref, kr_ref, v_ref, o_ref):
    kr = kr_ref[0].astype(BF16)
    for h in range(N_HEADS):
        nope = slice(h * QK_NOPE, (h + 1) * QK_NOPE)
        rope = slice(h * QK_ROPE, (h + 1) * QK_ROPE)
        s = _dot_nt(qn_ref[0, :, nope], kn_ref[0, :, nope]) + _dot_nt(qr_ref[0, :, rope], kr)
        p = jnp.exp(s - jnp.max(s, axis=-1, keepdims=True))
        l = jnp.sum(p, axis=-1, keepdims=True)
        o = _dot(p.astype(BF16), v_ref[0, :, h * V_HEAD:(h + 1) * V_HEAD])
        o_ref[0, :, h * V_HEAD:(h + 1) * V_HEAD] = (o / l).astype(BF16)


def _attention(qn, qr, kn, kr, v):
    b, lq, _ = qn.shape
    lk = kn.shape[1]
    tq = ATTN_Q_TILE
    return pl.pallas_call(
        _attn_kernel,
        grid=(b, lq // tq),
        in_specs=[pl.BlockSpec((1, tq, qn.shape[2]), lambda i, t: (i, t, 0)),
                  pl.BlockSpec((1, tq, qr.shape[2]), lambda i, t: (i, t, 0)),
                  pl.BlockSpec((1, lk, kn.shape[2]), lambda i, t: (i, 0, 0)),
                  pl.BlockSpec((1, lk, QK_ROPE), lambda i, t: (i, 0, 0)),
                  pl.BlockSpec((1, lk, v.shape[2]), lambda i, t: (i, 0, 0))],
        out_specs=pl.BlockSpec((1, tq, N_HEADS * V_HEAD), lambda i, t: (i, t, 0)),
        out_shape=jax.ShapeDtypeStruct((b, lq, N_HEADS * V_HEAD), BF16),
        compiler_params=_params("arbitrary", "arbitrary"),
        name="mla_attention",
    )(qn, qr, kn, kr, v)


def _split3(x):
    hi = x.astype(BF16)
    r = x - hi.astype(F32)
    mid = r.astype(BF16)
    lo = (r - mid.astype(F32)).astype(BF16)
    return hi, mid, lo


def _ssd_sweep(d, x, bc, dt_raw, dtb_ref, a_ref, h_scr):
    q = SSD_CHUNK
    dt = _softplus(dt_raw + dtb_ref[...])
    a = dt * a_ref[...]
    ii = lax.broadcasted_iota(jnp.int32, (q, q), 0)
    jj = lax.broadcasted_iota(jnp.int32, (q, q), 1)
    keep = (ii >= jj) if d == 0 else (ii <= jj)
    tri = jnp.where(keep, 1.0, 0.0).astype(BF16)
    hi, mid, lo = _split3(a)
    acum = _dot(tri, hi) + _dot(tri, mid) + _dot(tri, lo)
    acum_t = acum.T
    dt_t = dt.T
    end = q - 1 if d == 0 else 0
    x_t = [x[:, i * 128:(i + 1) * 128].T for i in range(SSD_INNER // 128)]
    ys = []
    for g in range(SSD_GROUPS):
        bg = bc[:, g * SSD_STATE:(g + 1) * SSD_STATE].astype(BF16)
        cg = bc[:, SSD_GROUPS * SSD_STATE + g * SSD_STATE:SSD_GROUPS * SSD_STATE + (g + 1) * SSD_STATE].astype(BF16)
        scores = _dot_nt(cg, bg)
        for hh in range(SSD_HEADS // SSD_GROUPS):
            h = g * (SSD_HEADS // SSD_GROUPS) + hh
            k = d * SSD_HEADS + h
            xh = x[:, h * SSD_HEADDIM:(h + 1) * SSD_HEADDIM]
            a_col = jnp.broadcast_to(acum[:, k:k + 1], (q, q))
            a_row = acum_t[k:k + 1, :]
            dt_row = dt_t[k:k + 1, :]
            decay = jnp.exp(jnp.where(keep, a_col - a_row, NEG_BIG))
            m = (scores * decay * dt_row).astype(BF16)
            hst = h_scr[d, h]
            y = _dot(m, xh.astype(BF16)) + _dot_nt(cg, hst.astype(BF16)) * jnp.exp(a_col[:, :SSD_HEADDIM])
            ys.append(y)
            total = a_row[:, end:end + 1]
            w_row = jnp.exp(total - a_row) * dt_row
            xh_t = x_t[h // 2][(h % 2) * SSD_HEADDIM:(h % 2 + 1) * SSD_HEADDIM, :]
            st = _dot((xh_t * w_row).astype(BF16), bg)
            h_scr[d, h] = hst * jnp.exp(total) + st
    return jnp.concatenate(ys, axis=1)


def _ssd_kernel(xs_ref, zs_ref, bc_ref, dt_ref, h0_ref, dtb_ref, a_ref, dx_ref, y_ref, ht_ref, yf_scr, h_scr, *, nc):
    s = pl.program_id(1)

    @pl.when(s == 0)
    def _():
        h_scr[...] = h0_ref[0]

    @pl.when(s < nc)
    def _():
        x = xs_ref[0]
        y = _ssd_sweep(0, x, bc_ref[0], dt_ref[0], dtb_ref, a_ref, h_scr)
        rows = pl.ds(pl.multiple_of(s * SSD_CHUNK, SSD_CHUNK), SSD_CHUNK)
        yf_scr[rows, :] = y + dx_ref[...] * x

    @pl.when(s >= nc)
    def _():
        c = 2 * nc - 1 - s
        y = _ssd_sweep(1, xs_ref[0], bc_ref[0], dt_ref[0], dtb_ref, a_ref, h_scr)
        rows = pl.ds(pl.multiple_of(c * SSD_CHUNK, SSD_CHUNK), SSD_CHUNK)
        y_ref[0] = (yf_scr[rows, :] + y) * zs_ref[0]

    @pl.when(s == 2 * nc - 1)
    def _():
        ht_ref[0] = h_scr[...]


def _ssd(proj3, h0, dt_bias128, a128, d_exp):
    b, l, _ = proj3.shape
    nc = l // SSD_CHUNK
    q = SSD_CHUNK
    chunk = lambda s: jnp.where(s < nc, s, 2 * nc - 1 - s)
    late = lambda s: jnp.where(s < nc, nc - 1, 2 * nc - 1 - s)
    st_shape = (1, 2, SSD_HEADS, SSD_HEADDIM, SSD_STATE)
    return pl.pallas_call(
        functools.partial(_ssd_kernel, nc=nc),
        grid=(b, 2 * nc),
        in_specs=[pl.BlockSpec((1, q, SSD_INNER), lambda i, s: (i, chunk(s), XS_BLK)),
                  pl.BlockSpec((1, q, SSD_INNER), lambda i, s: (i, late(s), Z_BLK)),
                  pl.BlockSpec((1, q, IN_TILE), lambda i, s: (i, chunk(s), BC_TILE)),
                  pl.BlockSpec((1, q, 128), lambda i, s: (i, chunk(s), DT_BLK)),
                  pl.BlockSpec(st_shape, lambda i, s: (i, 0, 0, 0, 0)),
                  pl.BlockSpec((1, 128), lambda i, s: (0, 0)),
                  pl.BlockSpec((1, 128), lambda i, s: (0, 0)),
                  pl.BlockSpec((1, SSD_INNER), lambda i, s: (0, 0))],
        out_specs=[pl.BlockSpec((1, q, SSD_INNER), lambda i, s: (i, late(s), 0)),
                   pl.BlockSpec(st_shape, lambda i, s: (i, 0, 0, 0, 0))],
        out_shape=[jax.ShapeDtypeStruct((b, l, SSD_INNER), F32),
                   jax.ShapeDtypeStruct((b,) + st_shape[1:], F32)],
        scratch_shapes=[pltpu.VMEM((l, SSD_INNER), F32),
                        pltpu.VMEM(st_shape[1:], F32)],
        compiler_params=_params("arbitrary", "arbitrary"),
        name="ssd_scan",
    )(proj3, proj3, proj3, proj3, h0, dt_bias128, a128, d_exp)


def _merge_kernel(attn_ref, yz_ref, gm_ref, gs_ref, x_ref, g1_ref, ng_ref, womla_ref, wossd_ref, wout_ref, o_ref):
    o_mla = _dot(attn_ref[...], womla_ref[...])
    o_ssd = _dot(_rmsnorm(yz_ref[...], ng_ref[...]).astype(BF16), wossd_ref[...])
    merged = gm_ref[...] * o_mla + gs_ref[...] * o_ssd
    o_ref[...] = x_ref[...] + g1_ref[...] * _dot(merged.astype(BF16), wout_ref[...])


def _merge(attn2d, yz2d, proj, x2d, mod48, mod_row, ssd_norm_g, w_o_mla, w_o_ssd, w_out):
    t = x2d.shape[0]
    tm = TOKEN_TILE
    row = lambda i: (i, 0)
    const = lambda i: (0, 0)
    wspec = pl.BlockSpec((D_MODEL, D_MODEL), const)
    return pl.pallas_call(
        _merge_kernel,
        grid=(t // tm,),
        in_specs=[pl.BlockSpec((tm, D_MODEL), row),
                  pl.BlockSpec((tm, D_MODEL), row),
                  pl.BlockSpec((tm, D_MODEL), lambda i: (i, GM_BLK)),
                  pl.BlockSpec((tm, D_MODEL), lambda i: (i, GS_BLK)),
                  pl.BlockSpec((tm, D_MODEL), row),
                  pl.BlockSpec((None, 1, D_MODEL), lambda i: (mod_row(i * tm // ROW_GROUP) * 6 + 2, 0, 0)),
                  pl.BlockSpec((1, D_MODEL), const),
                  wspec, wspec, wspec],
        out_specs=pl.BlockSpec((tm, D_MODEL), row),
        out_shape=jax.ShapeDtypeStruct((t, D_MODEL), F32),
        compiler_params=_params("arbitrary"),
        name="merge_out",
    )(attn2d, yz2d, proj, proj, x2d, mod48, ssd_norm_g, w_o_mla, w_o_ssd, w_out)


def _ffn_kernel(x_ref, sh_ref, sc_ref, g2_ref, ng_ref, wg_ref, wv_ref, cwg_ref, cwv_ref, cbg_ref, cbv_ref, wd_ref,
                fg_ref, o_ref, h_scr, *, seq_len):
    j = pl.program_id(1)

    @pl.when(j == 0)
    def _():
        def body(i, carry):
            rows = pl.ds(pl.multiple_of(i * 256, 256), 256)
            h = _rmsnorm(x_ref[rows, :], ng_ref[...]) * (1.0 + sc_ref[...]) + sh_ref[...]
            h_scr[rows, :] = h.astype(BF16)
            return carry
        lax.fori_loop(0, ROW_GROUP // 256, body, 0)
        o_ref[...] = jnp.zeros_like(o_ref)

    h = h_scr[...]
    ug = _dwconv3(_dot(h, wg_ref[...]), cwg_ref, cbg_ref, seq_len)
    uv = _dwconv3(_dot(h, wv_ref[...]), cwv_ref, cbv_ref, seq_len)
    o_ref[...] += _dot((_silu(ug) * uv).astype(BF16), wd_ref[...])

    @pl.when(j == pl.num_programs(1) - 1)
    def _():
        def body(i, carry):
            rows = pl.ds(pl.multiple_of(i * 256, 256), 256)
            o_ref[rows, :] = _rmsnorm(x_ref[rows, :] + g2_ref[...] * o_ref[rows, :], fg_ref[...])
            return carry
        lax.fori_loop(0, ROW_GROUP // 256, body, 0)


def _ffn(x2d, mod48, mod_row, norm_g, w_up, conv_w, conv_b, w_down, final_g, seq_len):
    t = x2d.shape[0]
    nj = D_FF // FFN_TILE
    gate = lambda r, j: (0, j)
    val = lambda r, j: (0, nj + j)
    const = lambda r, j: (0, 0)
    mod = lambda k: pl.BlockSpec((None, 1, D_MODEL), lambda r, j: (mod_row(r) * 6 + k, 0, 0))
    return pl.pallas_call(
        functools.partial(_ffn_kernel, seq_len=seq_len),
        grid=(t // ROW_GROUP, nj),
        in_specs=[pl.BlockSpec((ROW_GROUP, D_MODEL), lambda r, j: (r, 0)),
                  mod(3), mod(4), mod(5),
                  pl.BlockSpec((1, D_MODEL), const),
                  pl.BlockSpec((D_MODEL, FFN_TILE), gate),
                  pl.BlockSpec((D_MODEL, FFN_TILE), val),
                  pl.BlockSpec((3, FFN_TILE), gate),
                  pl.BlockSpec((3, FFN_TILE), val),
                  pl.BlockSpec((1, FFN_TILE), gate),
                  pl.BlockSpec((1, FFN_TILE), val),
                  pl.BlockSpec((FFN_TILE, D_MODEL), lambda r, j: (j, 0)),
                  pl.BlockSpec((1, D_MODEL), const)],
        out_specs=pl.BlockSpec((ROW_GROUP, D_MODEL), lambda r, j: (r, 0)),
        out_shape=jax.ShapeDtypeStruct((t, D_MODEL), F32),
        scratch_shapes=[pltpu.VMEM((ROW_GROUP, D_MODEL), BF16)],
        compiler_params=_params("arbitrary", "arbitrary"),
        name="conv_ffn",
    )(x2d, mod48, mod48, mod48, norm_g, w_up, w_up, conv_w, conv_w, conv_b, conv_b, w_down, final_g)


def _rope_tables(seq_len):
    t = jnp.arange(seq_len)
    row = (t // GRID_W).astype(F32)
    col = (t % GRID_W).astype(F32)
    n = QK_ROPE // 4
    inv = ROPE_BASE ** (-jnp.arange(n, dtype=F32) / n)
    ar, ac = row[:, None] * inv, col[:, None] * inv
    cos64 = jnp.concatenate([jnp.cos(ar), jnp.cos(ar), jnp.cos(ac), jnp.cos(ac)], axis=1)
    sin64 = jnp.concatenate([-jnp.sin(ar), jnp.sin(ar), -jnp.sin(ac), jnp.sin(ac)], axis=1)
    return jnp.concatenate([cos64, cos64], axis=1), jnp.concatenate([sin64, sin64], axis=1)


def _swap_rope_halves(w):
    lead = w.shape[:-1]
    return w.reshape(lead + (2, 2, QK_ROPE // 4))[..., ::-1, :].reshape(lead + (QK_ROPE,))


def _trunk_pass(x, mod48, mod_row, wts, ctx, latent):
    b, l, _ = x.shape
    x2d = x.reshape(b * l, D_MODEL)
    proj = _in_proj(x2d, mod48, mod_row, wts["norm_attn_g"], wts["w_in_r"], wts["ssd_conv_w"], wts["ssd_conv_b"], l)
    rope = _rope_tables(l) if latent else None
    w_uq_r = wts["w_uq_lat"] if latent else wts["w_uq_ctx"]
    qn, qr, ckv_n, kn, v, kr = _mla_prep(proj, wts["q_norm_g"], wts["kv_norm_g"], w_uq_r, wts["w_ukv_r"], rope, l)
    shape3 = lambda a: a.reshape(b, l, a.shape[-1])
    qn, qr, kn, v, kr3 = shape3(qn), shape3(qr), shape3(kn), shape3(v), shape3(kr)
    if ctx is None:
        h0 = jnp.zeros((b, 2, SSD_HEADS, SSD_HEADDIM, SSD_STATE), F32)
        kr_all = kr3
    else:
        cache_ckv, cache_krope, h0 = ctx
        past = cache_ckv.shape[1]
        kn_c, v_c = _kv_up(cache_ckv.reshape(b * past, KV_LORA), wts["w_ukv_r"])
        kn = jnp.concatenate([kn_c.reshape(b, past, -1), kn], axis=1)
        v = jnp.concatenate([v_c.reshape(b, past, -1), v], axis=1)
        kr_all = jnp.concatenate([cache_krope, kr3], axis=1)
    attn = _attention(qn, qr, kn, kr_all, v)
    yz, h_t = _ssd(proj.reshape(b, l, IN_COLS_PADDED), h0, wts["dt_bias128"], wts["a128"], wts["d_exp"])
    x1 = _merge(attn.reshape(b * l, -1), yz.reshape(b * l, -1), proj, x2d, mod48, mod_row, wts["ssd_norm_g"],
                wts["w_o_mla"], wts["w_o_ssd"], wts["w_out"])
    y = _ffn(x1, mod48, mod_row, wts["norm_ffn_g"], wts["w_up"], wts["ffn_conv_w"], wts["ffn_conv_b"], wts["w_down"],
             wts["final_norm_g"], l)
    return y.reshape(b, l, D_MODEL), ckv_n.reshape(b, l, KV_LORA), kr3, h_t


def kernel(x_prompt, x_sample, c, cache_ckv, cache_krope, state_ssd, c_ctx, w_ada, b_ada, norm_attn_g, w_in, q_norm_g,
           kv_norm_g, w_uq, w_ukv, w_o_mla, ssd_conv_w, ssd_conv_b, ssd_dt_bias, ssd_A_log, ssd_D, ssd_norm_g, w_o_ssd,
           w_out, norm_ffn_g, w_up, ffn_conv_w, ffn_conv_b, w_down, final_norm_g):
    depth = w_in.shape[0]
    assert depth == 1, "single trunk layer"
    dec_b = x_sample.shape[0]
    assert x_sample.shape[1] == ROW_GROUP and ROW_GROUP % x_prompt.shape[1] == 0
    lyr = 0

    cvec = jnp.zeros((8, D_MODEL), F32).at[0].set(c_ctx).at[1:1 + dec_b].set(c)
    mod48 = _ada(cvec, w_ada[lyr], b_ada[lyr]).reshape(8 * 6, 1, D_MODEL)

    wi = w_in[lyr]
    o = np.cumsum((0, Q_LORA, KV_LORA, QK_ROPE, SSD_INNER, SSD_INNER, SSD_GROUPS * SSD_STATE,
                   SSD_GROUPS * SSD_STATE, 2 * SSD_HEADS, D_MODEL, D_MODEL))
    piece = lambda i: wi[:, o[i]:o[i + 1]]
    w_kr = piece(2)
    misc_pad = IN_TILE - 2 * QK_ROPE - 2 * SSD_HEADS
    w_in_r = jnp.concatenate([piece(3), piece(4), piece(8), piece(9), piece(5), piece(6), piece(0), piece(1),
                              w_kr, _swap_rope_halves(w_kr), piece(7),
                              jnp.zeros((D_MODEL, misc_pad), F32)], axis=1).astype(BF16)
    wq = w_uq[lyr].reshape(Q_LORA, N_HEADS, QK_NOPE + QK_ROPE)
    wq_nope = wq[:, :, :QK_NOPE].reshape(Q_LORA, -1)
    wq_rope = wq[:, :, QK_NOPE:]
    w_uq_ctx = jnp.concatenate([wq_nope, wq_rope.reshape(Q_LORA, -1)], axis=1).astype(BF16)
    w_uq_lat = jnp.concatenate([wq_nope, wq_rope.reshape(Q_LORA, -1),
                                _swap_rope_halves(wq_rope).reshape(Q_LORA, -1)], axis=1).astype(BF16)
    wkv = w_ukv[lyr].reshape(KV_LORA, N_HEADS, QK_NOPE + V_HEAD)
    w_ukv_r = jnp.concatenate([wkv[:, :, :QK_NOPE].reshape(KV_LORA, -1),
                               wkv[:, :, QK_NOPE:].reshape(KV_LORA, -1)], axis=1).astype(BF16)
    pad128 = lambda a: jnp.pad(a.reshape(1, -1), ((0, 0), (0, 128 - a.size)))
    wts = {
        "norm_attn_g": norm_attn_g[lyr].reshape(1, -1), "w_in_r": w_in_r,
        "ssd_conv_w": ssd_conv_w[lyr], "ssd_conv_b": ssd_conv_b[lyr].reshape(1, -1),
        "q_norm_g": q_norm_g[lyr].reshape(1, -1), "kv_norm_g": kv_norm_g[lyr].reshape(1, -1),
        "w_uq_ctx": w_uq_ctx, "w_uq_lat": w_uq_lat, "w_ukv_r": w_ukv_r,
        "dt_bias128": pad128(ssd_dt_bias[lyr]), "a128": pad128(-jnp.exp(ssd_A_log[lyr])),
        "d_exp": jnp.repeat(ssd_D[lyr], SSD_HEADDIM).reshape(1, -1),
        "ssd_norm_g": ssd_norm_g[lyr].reshape(1, -1),
        "w_o_mla": w_o_mla[lyr].astype(BF16), "w_o_ssd": w_o_ssd[lyr].astype(BF16), "w_out": w_out[lyr].astype(BF16),
        "norm_ffn_g": norm_ffn_g[lyr].reshape(1, -1), "w_up": w_up[lyr].astype(BF16),
        "ffn_conv_w": ffn_conv_w[lyr], "ffn_conv_b": ffn_conv_b[lyr].reshape(1, -1),
        "w_down": w_down[lyr].astype(BF16), "final_norm_g": final_norm_g.reshape(1, -1),
    }

    y_p, ckv_p, kr_p, st_p = _trunk_pass(x_prompt, mod48, lambda r: 0, wts, None, False)
    ctx = (cache_ckv[:, lyr], cache_krope[:, lyr], state_ssd[:, lyr])
    y_s, _, _, _ = _trunk_pass(x_sample, mod48, lambda r: 1 + r, wts, ctx, True)
    return y_p, y_s, ckv_p[:, None], kr_p[:, None], st_p[:, None]
```

```python
import functools
import math

import jax
import jax.numpy as jnp
import numpy as np
from jax import lax
from jax.experimental import pallas as pl
from jax.experimental.pallas import tpu as pltpu

F32 = jnp.float32
BF16 = jnp.bfloat16

D_MODEL = 1024
GRID_W = 64
N_HEADS = 8
QK_NOPE = 128
QK_ROPE = 64
V_HEAD = 128
Q_LORA = 256
KV_LORA = 256
ROPE_BASE = 10000.0
SSD_HEADS = 16
SSD_HEADDIM = 64
SSD_INNER = SSD_HEADS * SSD_HEADDIM
SSD_GROUPS = 4
SSD_STATE = 64
SSD_CHUNK = 128
D_FF = 2816
EPS = 1e-6

ROW_GROUP = 2048
IN_TILE = 512
IN_COLS_PADDED = 11 * IN_TILE
FFN_TILE = 256
TOKEN_TILE = 512
ATTN_Q_TILE = 256
CONV_CHUNK = 256
CONV_PAD = 8
VMEM_LIMIT = 56 * 1024 * 1024
NEG_BIG = -1e30


def _sigmoid(x):
    return 1.0 / (1.0 + jnp.exp(-x))


def _silu(x):
    return x * _sigmoid(x)


def _softplus(x):
    return jnp.maximum(x, 0.0) + jnp.log1p(jnp.exp(-jnp.abs(x)))


def _rmsnorm(x, g):
    return x * lax.rsqrt(jnp.mean(x * x, axis=-1, keepdims=True) + EPS) * g


def _dot(a, b):
    return jnp.dot(a, b, preferred_element_type=F32)


def _dot_nt(a, b):
    return lax.dot_general(a, b, (((1,), (1,)), ((), ())), preferred_element_type=F32)


def _params(*sem):
    return pltpu.CompilerParams(dimension_semantics=sem, vmem_limit_bytes=VMEM_LIMIT)


def _norm_mod_rows(x_ref, g_ref, sc_ref, sh_ref, h_scr):
    def body(i, carry):
        rows = pl.ds(pl.multiple_of(i * CONV_CHUNK, CONV_CHUNK), CONV_CHUNK)
        h = _rmsnorm(x_ref[rows, :], g_ref[...]) * (1.0 + sc_ref[...]) + sh_ref[...]
        h_scr[rows, :] = h.astype(BF16)
        return carry
    lax.fori_loop(0, ROW_GROUP // CONV_CHUNK, body, 0)


def _zero_conv_pads(u_scr):
    zeros = jnp.zeros((CONV_PAD, u_scr.shape[1]), F32)
    u_scr[0:CONV_PAD, :] = zeros
    u_scr[CONV_PAD + ROW_GROUP:2 * CONV_PAD + ROW_GROUP, :] = zeros


def _stage_rows(c):
    return slice(CONV_PAD + c * CONV_CHUNK, CONV_PAD + (c + 1) * CONV_CHUNK)


def _dwconv3_rows(u_scr, c, cols, w_ref, b_ref, seq_len):
    r0 = c * CONV_CHUNK
    base = CONV_PAD + r0
    width = cols.stop - cols.start
    prev = u_scr[base - 1:base - 1 + CONV_CHUNK, cols]
    cur = u_scr[base:base + CONV_CHUNK, cols]
    nxt = u_scr[base + 1:base + 1 + CONV_CHUNK, cols]
    row = lax.broadcasted_iota(jnp.int32, (CONV_CHUNK, width), 0)
    if r0 % seq_len == 0:
        prev = jnp.where(row == 0, 0.0, prev)
    if (r0 + CONV_CHUNK) % seq_len == 0:
        nxt = jnp.where(row == CONV_CHUNK - 1, 0.0, nxt)
    return prev * w_ref[0:1, :] + cur * w_ref[1:2, :] + nxt * w_ref[2:3, :] + b_ref[...]


def _ada_kernel(c_ref, w_ref, b_ref, o_ref):
    a = _silu(c_ref[...]).astype(BF16)
    o_ref[...] = _dot(a, w_ref[...].astype(BF16)) + b_ref[...]


def _ada(cvec, w_ada, b_ada):
    tn = 1536
    return pl.pallas_call(
        _ada_kernel,
        grid=(6 * D_MODEL // tn,),
        in_specs=[pl.BlockSpec((8, D_MODEL), lambda j: (0, 0)),
                  pl.BlockSpec((D_MODEL, tn), lambda j: (0, j)),
                  pl.BlockSpec((1, tn), lambda j: (0, j))],
        out_specs=pl.BlockSpec((8, tn), lambda j: (0, j)),
        out_shape=jax.ShapeDtypeStruct((8, 6 * D_MODEL), F32),
        compiler_params=_params("arbitrary"),
        name="ada_mod",
    )(cvec, w_ada, b_ada.reshape(1, -1))


Z_BLK, XS_BLK, GM_BLK, GS_BLK = 0, 1, 2, 3
BC_TILE, MLA_TILE, MISC_TILE = 8, 9, 10
DT_BLK = (MISC_TILE * IN_TILE + 2 * QK_ROPE) // 128


def _in_kernel(x_ref, sh_ref, sc_ref, g_ref, w_ref, cw_ref, cb_ref, o_ref, h_scr, u_scr, *, seq_len):
    j = pl.program_id(1)
    n_chunks = ROW_GROUP // CONV_CHUNK
    rows = lambda c: slice(c * CONV_CHUNK, (c + 1) * CONV_CHUNK)

    @pl.when(j == 0)
    def _():
        _norm_mod_rows(x_ref, g_ref, sc_ref, sh_ref, h_scr)
        _zero_conv_pads(u_scr)

    def pointwise(fn):
        for c in range(n_chunks):
            o_ref[rows(c), :] = fn(_dot(h_scr[rows(c), :], w_ref[...]))

    @pl.when(j <= 1)
    def _():
        pointwise(_silu)

    @pl.when((j >= 4) & (j <= 7))
    def _():
        pointwise(_sigmoid)

    @pl.when(j >= MLA_TILE)
    def _():
        pointwise(lambda u: u)

    @pl.when((j == 2) | (j == 3) | (j == BC_TILE))
    def _():
        def conv_out(c):
            o_ref[rows(c), :] = _silu(_dwconv3_rows(u_scr, c, slice(0, IN_TILE), cw_ref, cb_ref, seq_len))
        for c in range(n_chunks):
            u_scr[_stage_rows(c), :] = _dot(h_scr[rows(c), :], w_ref[...])
            if c >= 1:
                conv_out(c - 1)
        conv_out(n_chunks - 1)


def _in_proj(x2d, mod48, mod_row, norm_g, w_in_r, conv_w, conv_b, seq_len):
    t = x2d.shape[0]
    n_tiles = IN_COLS_PADDED // IN_TILE
    conv_idx = lambda r, j: (0, jnp.where(j == BC_TILE, 2, jnp.clip(j - 2, 0, 1)))
    return pl.pallas_call(
        functools.partial(_in_kernel, seq_len=seq_len),
        grid=(t // ROW_GROUP, n_tiles),
        in_specs=[pl.BlockSpec((ROW_GROUP, D_MODEL), lambda r, j: (r, 0)),
                  pl.BlockSpec((None, 1, D_MODEL), lambda r, j: (mod_row(r) * 6 + 0, 0, 0)),
                  pl.BlockSpec((None, 1, D_MODEL), lambda r, j: (mod_row(r) * 6 + 1, 0, 0)),
                  pl.BlockSpec((1, D_MODEL), lambda r, j: (0, 0)),
                  pl.BlockSpec((D_MODEL, IN_TILE), lambda r, j: (0, j)),
                  pl.BlockSpec((3, IN_TILE), conv_idx),
                  pl.BlockSpec((1, IN_TILE), conv_idx)],
        out_specs=pl.BlockSpec((ROW_GROUP, IN_TILE), lambda r, j: (r, j)),
        out_shape=jax.ShapeDtypeStruct((t, IN_COLS_PADDED), F32),
        scratch_shapes=[pltpu.VMEM((ROW_GROUP, D_MODEL), BF16),
                        pltpu.VMEM((ROW_GROUP + 2 * CONV_PAD, IN_TILE), F32)],
        compiler_params=_params("arbitrary", "arbitrary"),
        name="in_proj",
    )(x2d, mod48, mod48, norm_g, w_in_r, conv_w, conv_b)


def _mla_prep_kernel(*refs, latent):
    if latent:
        (p0_ref, p10_ref, qg_ref, kvg_ref, wuq_ref, wukv_ref, cos_ref, sin_ref,
         qn_ref, qr_ref, ckv_ref, kn_ref, v_ref, kr_ref) = refs
    else:
        (p0_ref, p10_ref, qg_ref, kvg_ref, wuq_ref, wukv_ref,
         qn_ref, qr_ref, ckv_ref, kn_ref, v_ref, kr_ref) = refs
    scale = 1.0 / math.sqrt(QK_NOPE + QK_ROPE)
    p0 = p0_ref[...]
    cqn = _rmsnorm(p0[:, :Q_LORA], qg_ref[...]).astype(BF16)
    q = _dot(cqn, wuq_ref[...])
    n_nope = N_HEADS * QK_NOPE
    n_rope = N_HEADS * QK_ROPE
    qn_ref[...] = (q[:, :n_nope] * scale).astype(BF16)
    q_rope = q[:, n_nope:n_nope + n_rope]
    misc = p10_ref[...]
    kr = misc[:, :QK_ROPE]
    if latent:
        cos2, sin2 = cos_ref[...], sin_ref[...]
        cos8 = jnp.concatenate([cos2] * (n_rope // 128), axis=1)
        sin8 = jnp.concatenate([sin2] * (n_rope // 128), axis=1)
        q_rope = q_rope * cos8 + q[:, n_nope + n_rope:] * sin8
        kr = kr * cos2[:, :QK_ROPE] + misc[:, QK_ROPE:2 * QK_ROPE] * sin2[:, :QK_ROPE]
    qr_ref[...] = (q_rope * scale).astype(BF16)
    kr_ref[...] = kr
    ckv_n = _rmsnorm(p0[:, Q_LORA:], kvg_ref[...])
    ckv_ref[...] = ckv_n
    kv = _dot(ckv_n.astype(BF16), wukv_ref[...])
    kn_ref[...] = kv[:, :N_HEADS * QK_NOPE].astype(BF16)
    v_ref[...] = kv[:, N_HEADS * QK_NOPE:].astype(BF16)


def _mla_prep(proj, q_norm_g, kv_norm_g, w_uq_r, w_ukv_r, rope_tables, seq_len):
    t = proj.shape[0]
    tm = TOKEN_TILE
    latent = rope_tables is not None
    in_specs = [pl.BlockSpec((tm, IN_TILE), lambda i: (i, MLA_TILE)),
                pl.BlockSpec((tm, IN_TILE), lambda i: (i, MISC_TILE)),
                pl.BlockSpec((1, Q_LORA), lambda i: (0, 0)),
                pl.BlockSpec((1, KV_LORA), lambda i: (0, 0)),
                pl.BlockSpec(w_uq_r.shape, lambda i: (0, 0)),
                pl.BlockSpec(w_ukv_r.shape, lambda i: (0, 0))]
    args = [proj, proj, q_norm_g, kv_norm_g, w_uq_r, w_ukv_r]
    if latent:
        per_seq = seq_len // tm
        in_specs += [pl.BlockSpec((tm, 128), lambda i: (i % per_seq, 0))] * 2
        args += list(rope_tables)
    widths = (N_HEADS * QK_NOPE, N_HEADS * QK_ROPE, KV_LORA, N_HEADS * QK_NOPE, N_HEADS * V_HEAD, QK_ROPE)
    dtypes = (BF16, BF16, F32, BF16, BF16, F32)
    return pl.pallas_call(
        functools.partial(_mla_prep_kernel, latent=latent),
        grid=(t // tm,),
        in_specs=in_specs,
        out_specs=[pl.BlockSpec((tm, w), lambda i: (i, 0)) for w in widths],
        out_shape=[jax.ShapeDtypeStruct((t, w), d) for w, d in zip(widths, dtypes)],
        compiler_params=_params("arbitrary"),
        name="mla_prep",
    )(*args)


def _kv_up_kernel(c_ref, w_ref, kn_ref, v_ref):
    kv = _dot(c_ref[...].astype(BF16), w_ref[...])
    kn_ref[...] = kv[:, :N_HEADS * QK_NOPE].astype(BF16)
    v_ref[...] = kv[:, N_HEADS * QK_NOPE:].astype(BF16)


def _kv_up(ckv2d, w_ukv_r):
    t = ckv2d.shape[0]
    tm = TOKEN_TILE
    w = N_HEADS * QK_NOPE
    return pl.pallas_call(
        _kv_up_kernel,
        grid=(t // tm,),
        in_specs=[pl.BlockSpec((tm, KV_LORA), lambda i: (i, 0)),
                  pl.BlockSpec(w_ukv_r.shape, lambda i: (0, 0))],
        out_specs=[pl.BlockSpec((tm, w), lambda i: (i, 0))] * 2,
        out_shape=[jax.ShapeDtypeStruct((t, w), BF16)] * 2,
        compiler_params=_params("arbitrary"),
        name="kv_up_ctx",
    )(ckv2d, w_ukv_r)


def _attn_kernel(qn_ref, qr_ref, kn_ref, kr_ref, v_ref, o_ref):
    kr = kr_ref[0].astype(BF16)
    for h in range(N_HEADS):
        nope = slice(h * QK_NOPE, (h + 1) * QK_NOPE)
        rope = slice(h * QK_ROPE, (h + 1) * QK_ROPE)
        s = _dot_nt(qn_ref[0, :, nope], kn_ref[0, :, nope]) + _dot_nt(qr_ref[0, :, rope], kr)
        p = jnp.exp(s - jnp.max(s, axis=-1, keepdims=True))
        l = jnp.sum(p, axis=-1, keepdims=True)
        o = _dot(p.astype(BF16), v_ref[0, :, h * V_HEAD:(h + 1) * V_HEAD])
        o_ref[0, :, h * V_HEAD:(h + 1) * V_HEAD] = (o / l).astype(BF16)


def _attention(qn, qr, kn, kr, v):
    b, lq, _ = qn.shape
    lk = kn.shape[1]
    tq = ATTN_Q_TILE
    return pl.pallas_call(
        _attn_kernel,
        grid=(b, lq // tq),
        in_specs=[pl.BlockSpec((1, tq, qn.shape[2]), lambda i, t: (i, t, 0)),
                  pl.BlockSpec((1, tq, qr.shape[2]), lambda i, t: (i, t, 0)),
                  pl.BlockSpec((1, lk, kn.shape[2]), lambda i, t: (i, 0, 0)),
                  pl.BlockSpec((1, lk, QK_ROPE), lambda i, t: (i, 0, 0)),
                  pl.BlockSpec((1, lk, v.shape[2]), lambda i, t: (i, 0, 0))],
        out_specs=pl.BlockSpec((1, tq, N_HEADS * V_HEAD), lambda i, t: (i, t, 0)),
        out_shape=jax.ShapeDtypeStruct((b, lq, N_HEADS * V_HEAD), BF16),
        compiler_params=_params("arbitrary", "arbitrary"),
        name="mla_attention",
    )(qn, qr, kn, kr, v)


def _split3(x):
    hi = x.astype(BF16)
    r = x - hi.astype(F32)
    mid = r.astype(BF16)
    lo = (r - mid.astype(F32)).astype(BF16)
    return hi, mid, lo


def _ssd_sweep(d, x, bc, dt_raw, dtb_ref, a_ref, h_scr):
    q = SSD_CHUNK
    dt = _softplus(dt_raw + dtb_ref[...])
    a = dt * a_ref[...]
    ii = lax.broadcasted_iota(jnp.int32, (q, q), 0)
    jj = lax.broadcasted_iota(jnp.int32, (q, q), 1)
    keep = (ii >= jj) if d == 0 else (ii <= jj)
    tri = jnp.where(keep, 1.0, 0.0).astype(BF16)
    hi, mid, lo = _split3(a)
    acum = _dot(tri, hi) + _dot(tri, mid) + _dot(tri, lo)
    acum_t = acum.T
    dt_t = dt.T
    end = q - 1 if d == 0 else 0
    x_t = [x[:, i * 128:(i + 1) * 128].T for i in range(SSD_INNER // 128)]
    ys = []
    for g in range(SSD_GROUPS):
        bg = bc[:, g * SSD_STATE:(g + 1) * SSD_STATE].astype(BF16)
        cg = bc[:, SSD_GROUPS * SSD_STATE + g * SSD_STATE:SSD_GROUPS * SSD_STATE + (g + 1) * SSD_STATE].astype(BF16)
        scores = _dot_nt(cg, bg)
        for hh in range(SSD_HEADS // SSD_GROUPS):
            h = g * (SSD_HEADS // SSD_GROUPS) + hh
            k = d * SSD_HEADS + h
            xh = x[:, h * SSD_HEADDIM:(h + 1) * SSD_HEADDIM]
            a_col = jnp.broadcast_to(acum[:, k:k + 1], (q, q))
            a_row = acum_t[k:k + 1, :]
            dt_row = dt_t[k:k + 1, :]
            decay = jnp.exp(jnp.where(keep, a_col - a_row, NEG_BIG))
            m = (scores * decay * dt_row).astype(BF16)
            hst = h_scr[d, h]
            y = _dot(m, xh.astype(BF16)) + _dot_nt(cg, hst.astype(BF16)) * jnp.exp(a_col[:, :SSD_HEADDIM])
            ys.append(y)
            total = a_row[:, end:end + 1]
            w_row = jnp.exp(total - a_row) * dt_row
            xh_t = x_t[h // 2][(h % 2) * SSD_HEADDIM:(h % 2 + 1) * SSD_HEADDIM, :]
            st = _dot((xh_t * w_row).astype(BF16), bg)
            h_scr[d, h] = hst * jnp.exp(total) + st
    return jnp.concatenate(ys, axis=1)


def _ssd_kernel(*refs, nc, has_h0):
    if has_h0:
        xs_ref, zs_ref, bc_ref, dt_ref, dtb_ref, a_ref, dx_ref, h0_ref, y_ref, ht_ref, yf_scr, h_scr = refs
    else:
        xs_ref, zs_ref, bc_ref, dt_ref, dtb_ref, a_ref, dx_ref, y_ref, ht_ref, yf_scr, h_scr = refs
    s = pl.program_id(1)

    @pl.when(s == 0)
    def _():
        h_scr[...] = h0_ref[0] if has_h0 else jnp.zeros(h_scr.shape, F32)

    @pl.when(s < nc)
    def _():
        x = xs_ref[0]
        y = _ssd_sweep(0, x, bc_ref[0], dt_ref[0], dtb_ref, a_ref, h_scr)
        rows = pl.ds(pl.multiple_of(s * SSD_CHUNK, SSD_CHUNK), SSD_CHUNK)
        yf_scr[rows, :] = y + dx_ref[...] * x

    @pl.when(s >= nc)
    def _():
        c = 2 * nc - 1 - s
        y = _ssd_sweep(1, xs_ref[0], bc_ref[0], dt_ref[0], dtb_ref, a_ref, h_scr)
        rows = pl.ds(pl.multiple_of(c * SSD_CHUNK, SSD_CHUNK), SSD_CHUNK)
        y_ref[0] = (yf_scr[rows, :] + y) * zs_ref[0]

    @pl.when(s == 2 * nc - 1)
    def _():
        ht_ref[0] = h_scr[...]


def _ssd(proj3, h0, dt_bias128, a128, d_exp):
    b, l, _ = proj3.shape
    nc = l // SSD_CHUNK
    q = SSD_CHUNK
    chunk = lambda s: jnp.where(s < nc, s, 2 * nc - 1 - s)
    late = lambda s: jnp.where(s < nc, nc - 1, 2 * nc - 1 - s)
    st_shape = (1, 2, SSD_HEADS, SSD_HEADDIM, SSD_STATE)
    st_spec = pl.BlockSpec(st_shape, lambda i, s: (i, 0, 0, 0, 0))
    has_h0 = h0 is not None
    return pl.pallas_call(
        functools.partial(_ssd_kernel, nc=nc, has_h0=has_h0),
        grid=(b, 2 * nc),
        in_specs=[pl.BlockSpec((1, q, SSD_INNER), lambda i, s: (i, chunk(s), XS_BLK)),
                  pl.BlockSpec((1, q, SSD_INNER), lambda i, s: (i, late(s), Z_BLK)),
                  pl.BlockSpec((1, q, IN_TILE), lambda i, s: (i, chunk(s), BC_TILE)),
                  pl.BlockSpec((1, q, 128), lambda i, s: (i, chunk(s), DT_BLK)),
                  pl.BlockSpec((1, 128), lambda i, s: (0, 0)),
                  pl.BlockSpec((1, 128), lambda i, s: (0, 0)),
                  pl.BlockSpec((1, SSD_INNER), lambda i, s: (0, 0))] + ([st_spec] if has_h0 else []),
        out_specs=[pl.BlockSpec((1, q, SSD_INNER), lambda i, s: (i, late(s), 0)),
                   pl.BlockSpec(st_shape, lambda i, s: (i, 0, 0, 0, 0))],
        out_shape=[jax.ShapeDtypeStruct((b, l, SSD_INNER), F32),
                   jax.ShapeDtypeStruct((b,) + st_shape[1:], F32)],
        scratch_shapes=[pltpu.VMEM((l, SSD_INNER), F32),
                        pltpu.VMEM(st_shape[1:], F32)],
        compiler_params=_params("arbitrary", "arbitrary"),
        name="ssd_scan",
    )(proj3, proj3, proj3, proj3, dt_bias128, a128, d_exp, *([h0] if has_h0 else []))


def _merge_kernel(attn_ref, yz_ref, gm_ref, gs_ref, x_ref, g1_ref, ng_ref, womla_ref, wossd_ref, wout_ref, o_ref, w_scr):
    @pl.when(pl.program_id(0) == 0)
    def _():
        w_scr[0] = womla_ref[...].astype(BF16)
        w_scr[1] = wossd_ref[...].astype(BF16)
        w_scr[2] = wout_ref[...].astype(BF16)

    o_mla = _dot(attn_ref[...], w_scr[0])
    o_ssd = _dot(_rmsnorm(yz_ref[...], ng_ref[...]).astype(BF16), w_scr[1])
    merged = gm_ref[...] * o_mla + gs_ref[...] * o_ssd
    o_ref[...] = x_ref[...] + g1_ref[...] * _dot(merged.astype(BF16), w_scr[2])


def _merge(attn2d, yz2d, proj, x2d, mod48, mod_row, ssd_norm_g, w_o_mla, w_o_ssd, w_out):
    t = x2d.shape[0]
    tm = TOKEN_TILE
    row = lambda i: (i, 0)
    const = lambda i: (0, 0)
    wspec = pl.BlockSpec((D_MODEL, D_MODEL), const, pipeline_mode=pl.Buffered(1))
    return pl.pallas_call(
        _merge_kernel,
        grid=(t // tm,),
        in_specs=[pl.BlockSpec((tm, D_MODEL), row),
                  pl.BlockSpec((tm, D_MODEL), row),
                  pl.BlockSpec((tm, D_MODEL), lambda i: (i, GM_BLK)),
                  pl.BlockSpec((tm, D_MODEL), lambda i: (i, GS_BLK)),
                  pl.BlockSpec((tm, D_MODEL), row),
                  pl.BlockSpec((None, 1, D_MODEL), lambda i: (mod_row(i * tm // ROW_GROUP) * 6 + 2, 0, 0)),
                  pl.BlockSpec((1, D_MODEL), const),
                  wspec, wspec, wspec],
        out_specs=pl.BlockSpec((tm, D_MODEL), row),
        out_shape=jax.ShapeDtypeStruct((t, D_MODEL), F32),
        scratch_shapes=[pltpu.VMEM((3, D_MODEL, D_MODEL), BF16)],
        compiler_params=_params("arbitrary"),
        name="merge_out",
    )(attn2d, yz2d, proj, proj, x2d, mod48, ssd_norm_g, w_o_mla, w_o_ssd, w_out)


def _ffn_kernel(x_ref, sh_ref, sc_ref, g2_ref, ng_ref, wg_ref, wv_ref, cwg_ref, cwv_ref, cbg_ref, cbv_ref, wd_ref,
                fg_ref, o_ref, h_scr, wup_scr, wd_scr, u_scr, *, seq_len):
    j = pl.program_id(1)
    n_chunks = ROW_GROUP // CONV_CHUNK
    rows = lambda c: slice(c * CONV_CHUNK, (c + 1) * CONV_CHUNK)

    @pl.when(j == 0)
    def _():
        _norm_mod_rows(x_ref, ng_ref, sc_ref, sh_ref, h_scr)
        _zero_conv_pads(u_scr)
        o_ref[...] = jnp.zeros_like(o_ref)

    wup_scr[:, 0:FFN_TILE] = wg_ref[...].astype(BF16)
    wup_scr[:, FFN_TILE:2 * FFN_TILE] = wv_ref[...].astype(BF16)
    wd_scr[...] = wd_ref[...].astype(BF16)

    def down(c):
        ug = _dwconv3_rows(u_scr, c, slice(0, FFN_TILE), cwg_ref, cbg_ref, seq_len)
        uv = _dwconv3_rows(u_scr, c, slice(FFN_TILE, 2 * FFN_TILE), cwv_ref, cbv_ref, seq_len)
        o_ref[rows(c), :] += _dot((_silu(ug) * uv).astype(BF16), wd_scr[...])

    for c in range(n_chunks):
        u_scr[_stage_rows(c), :] = _dot(h_scr[rows(c), :], wup_scr[...])
        if c >= 1:
            down(c - 1)
    down(n_chunks - 1)

    @pl.when(j == pl.num_programs(1) - 1)
    def _():
        def body(i, carry):
            rows = pl.ds(pl.multiple_of(i * 256, 256), 256)
            o_ref[rows, :] = _rmsnorm(x_ref[rows, :] + g2_ref[...] * o_ref[rows, :], fg_ref[...])
            return carry
        lax.fori_loop(0, ROW_GROUP // 256, body, 0)


def _ffn(x2d, mod48, mod_row, norm_g, w_up, conv_w, conv_b, w_down, final_g, seq_len):
    t = x2d.shape[0]
    nj = D_FF // FFN_TILE
    gate = lambda r, j: (0, j)
    val = lambda r, j: (0, nj + j)
    const = lambda r, j: (0, 0)
    mod = lambda k: pl.BlockSpec((None, 1, D_MODEL), lambda r, j: (mod_row(r) * 6 + k, 0, 0))
    return pl.pallas_call(
        functools.partial(_ffn_kernel, seq_len=seq_len),
        grid=(t // ROW_GROUP, nj),
        in_specs=[pl.BlockSpec((ROW_GROUP, D_MODEL), lambda r, j: (r, 0)),
                  mod(3), mod(4), mod(5),
                  pl.BlockSpec((1, D_MODEL), const),
                  pl.BlockSpec((D_MODEL, FFN_TILE), gate),
                  pl.BlockSpec((D_MODEL, FFN_TILE), val),
                  pl.BlockSpec((3, FFN_TILE), gate),
                  pl.BlockSpec((3, FFN_TILE), val),
                  pl.BlockSpec((1, FFN_TILE), gate),
                  pl.BlockSpec((1, FFN_TILE), val),
                  pl.BlockSpec((FFN_TILE, D_MODEL), lambda r, j: (j, 0)),
                  pl.BlockSpec((1, D_MODEL), const)],
        out_specs=pl.BlockSpec((ROW_GROUP, D_MODEL), lambda r, j: (r, 0)),
        out_shape=jax.ShapeDtypeStruct((t, D_MODEL), F32),
        scratch_shapes=[pltpu.VMEM((ROW_GROUP, D_MODEL), BF16),
                        pltpu.VMEM((D_MODEL, 2 * FFN_TILE), BF16),
                        pltpu.VMEM((FFN_TILE, D_MODEL), BF16),
                        pltpu.VMEM((ROW_GROUP + 2 * CONV_PAD, 2 * FFN_TILE), F32)],
        compiler_params=_params("arbitrary", "arbitrary"),
        name="conv_ffn",
    )(x2d, mod48, mod48, mod48, norm_g, w_up, w_up, conv_w, conv_w, conv_b, conv_b, w_down, final_g)


def _rope_tables(seq_len):
    t = np.arange(seq_len)
    row = (t // GRID_W).astype(np.float32)
    col = (t % GRID_W).astype(np.float32)
    n = QK_ROPE // 4
    inv = (np.float32(ROPE_BASE) ** (-np.arange(n, dtype=np.float32) / np.float32(n))).astype(np.float32)
    ar, ac = row[:, None] * inv, col[:, None] * inv
    cos64 = np.concatenate([np.cos(ar), np.cos(ar), np.cos(ac), np.cos(ac)], axis=1)
    sin64 = np.concatenate([-np.sin(ar), np.sin(ar), -np.sin(ac), np.sin(ac)], axis=1)
    return (jnp.asarray(np.concatenate([cos64, cos64], axis=1), F32),
            jnp.asarray(np.concatenate([sin64, sin64], axis=1), F32))


def _swap_rope_halves(w):
    lead = w.shape[:-1]
    return w.reshape(lead + (2, 2, QK_ROPE // 4))[..., ::-1, :].reshape(lead + (QK_ROPE,))


def _trunk_pass(x, mod48, mod_row, wts, ctx, latent):
    b, l, _ = x.shape
    x2d = x.reshape(b * l, D_MODEL)
    proj = _in_proj(x2d, mod48, mod_row, wts["norm_attn_g"], wts["w_in_r"], wts["ssd_conv_w"], wts["ssd_conv_b"], l)
    rope = _rope_tables(l) if latent else None
    w_uq_r = wts["w_uq_lat"] if latent else wts["w_uq_ctx"]
    qn, qr, ckv_n, kn, v, kr = _mla_prep(proj, wts["q_norm_g"], wts["kv_norm_g"], w_uq_r, wts["w_ukv_r"], rope, l)
    shape3 = lambda a: a.reshape(b, l, a.shape[-1])
    qn, qr, kn, v, kr3 = shape3(qn), shape3(qr), shape3(kn), shape3(v), shape3(kr)
    if ctx is None:
        h0 = None
        kr_all = kr3
    else:
        cache_ckv, cache_krope, h0 = ctx
        past = cache_ckv.shape[1]
        kn_c, v_c = _kv_up(cache_ckv.reshape(b * past, KV_LORA), wts["w_ukv_r"])
        kn = jnp.concatenate([kn_c.reshape(b, past, -1), kn], axis=1)
        v = jnp.concatenate([v_c.reshape(b, past, -1), v], axis=1)
        kr_all = jnp.concatenate([cache_krope, kr3], axis=1)
    attn = _attention(qn, qr, kn, kr_all, v)
    yz, h_t = _ssd(proj.reshape(b, l, IN_COLS_PADDED), h0, wts["dt_bias128"], wts["a128"], wts["d_exp"])
    x1 = _merge(attn.reshape(b * l, -1), yz.reshape(b * l, -1), proj, x2d, mod48, mod_row, wts["ssd_norm_g"],
                wts["w_o_mla"], wts["w_o_ssd"], wts["w_out"])
    y = _ffn(x1, mod48, mod_row, wts["norm_ffn_g"], wts["w_up"], wts["ffn_conv_w"], wts["ffn_conv_b"], wts["w_down"],
             wts["final_norm_g"], l)
    return y.reshape(b, l, D_MODEL), ckv_n.reshape(b, l, KV_LORA), kr3, h_t


def kernel(x_prompt, x_sample, c, cache_ckv, cache_krope, state_ssd, c_ctx, w_ada, b_ada, norm_attn_g, w_in, q_norm_g,
           kv_norm_g, w_uq, w_ukv, w_o_mla, ssd_conv_w, ssd_conv_b, ssd_dt_bias, ssd_A_log, ssd_D, ssd_norm_g, w_o_ssd,
           w_out, norm_ffn_g, w_up, ffn_conv_w, ffn_conv_b, w_down, final_norm_g):
    depth = w_in.shape[0]
    assert depth == 1, "single trunk layer"
    dec_b = x_sample.shape[0]
    assert x_sample.shape[1] == ROW_GROUP and ROW_GROUP % x_prompt.shape[1] == 0
    lyr = 0

    cvec = jnp.zeros((8, D_MODEL), F32).at[0].set(c_ctx).at[1:1 + dec_b].set(c)
    mod48 = _ada(cvec, w_ada[lyr], b_ada[lyr]).reshape(8 * 6, 1, D_MODEL)

    wi = w_in[lyr]
    o = np.cumsum((0, Q_LORA, KV_LORA, QK_ROPE, SSD_INNER, SSD_INNER, SSD_GROUPS * SSD_STATE,
                   SSD_GROUPS * SSD_STATE, 2 * SSD_HEADS, D_MODEL, D_MODEL))
    piece = lambda i: wi[:, o[i]:o[i + 1]]
    w_kr = piece(2)
    misc_pad = IN_TILE - 2 * QK_ROPE - 2 * SSD_HEADS
    w_in_r = jnp.concatenate([piece(3), piece(4), piece(8), piece(9), piece(5), piece(6), piece(0), piece(1),
                              w_kr, _swap_rope_halves(w_kr), piece(7),
                              jnp.zeros((D_MODEL, misc_pad), F32)], axis=1).astype(BF16)
    wq = w_uq[lyr].reshape(Q_LORA, N_HEADS, QK_NOPE + QK_ROPE)
    wq_nope = wq[:, :, :QK_NOPE].reshape(Q_LORA, -1)
    wq_rope = wq[:, :, QK_NOPE:]
    w_uq_ctx = jnp.concatenate([wq_nope, wq_rope.reshape(Q_LORA, -1)], axis=1).astype(BF16)
    w_uq_lat = jnp.concatenate([wq_nope, wq_rope.reshape(Q_LORA, -1),
                                _swap_rope_halves(wq_rope).reshape(Q_LORA, -1)], axis=1).astype(BF16)
    wkv = w_ukv[lyr].reshape(KV_LORA, N_HEADS, QK_NOPE + V_HEAD)
    w_ukv_r = jnp.concatenate([wkv[:, :, :QK_NOPE].reshape(KV_LORA, -1),
                               wkv[:, :, QK_NOPE:].reshape(KV_LORA, -1)], axis=1).astype(BF16)
    pad128 = lambda a: jnp.pad(a.reshape(1, -1), ((0, 0), (0, 128 - a.size)))
    wts = {
        "norm_attn_g": norm_attn_g[lyr].reshape(1, -1), "w_in_r": w_in_r,
        "ssd_conv_w": ssd_conv_w[lyr], "ssd_conv_b": ssd_conv_b[lyr].reshape(1, -1),
        "q_norm_g": q_norm_g[lyr].reshape(1, -1), "kv_norm_g": kv_norm_g[lyr].reshape(1, -1),
        "w_uq_ctx": w_uq_ctx, "w_uq_lat": w_uq_lat, "w_ukv_r": w_ukv_r,
        "dt_bias128": pad128(ssd_dt_bias[lyr]), "a128": pad128(-jnp.exp(ssd_A_log[lyr])),
        "d_exp": jnp.repeat(ssd_D[lyr], SSD_HEADDIM).reshape(1, -1),
        "ssd_norm_g": ssd_norm_g[lyr].reshape(1, -1),
        "w_o_mla": w_o_mla[lyr], "w_o_ssd": w_o_ssd[lyr], "w_out": w_out[lyr],
        "norm_ffn_g": norm_ffn_g[lyr].reshape(1, -1), "w_up": w_up[lyr],
        "ffn_conv_w": ffn_conv_w[lyr], "ffn_conv_b": ffn_conv_b[lyr].reshape(1, -1),
        "w_down": w_down[lyr], "final_norm_g": final_norm_g.reshape(1, -1),
    }

    y_p, ckv_p, kr_p, st_p = _trunk_pass(x_prompt, mod48, lambda r: 0, wts, None, False)
    ctx = (cache_ckv[:, lyr], cache_krope[:, lyr], state_ssd[:, lyr])
    y_s, _, _, _ = _trunk_pass(x_sample, mod48, lambda r: 1 + r, wts, ctx, True)
    return y_p, y_s, ckv_p[:, None], kr_p[:, None], st_p[:, None]
```

```python
import functools
import math

import jax
import jax.numpy as jnp
import numpy as np
from jax import lax
from jax.experimental import pallas as pl
from jax.experimental.pallas import tpu as pltpu

F32 = jnp.float32
BF16 = jnp.bfloat16

D_MODEL = 1024
GRID_W = 64
N_HEADS = 8
QK_NOPE = 128
QK_ROPE = 64
V_HEAD = 128
Q_LORA = 256
KV_LORA = 256
ROPE_BASE = 10000.0
SSD_HEADS = 16
SSD_HEADDIM = 64
SSD_INNER = SSD_HEADS * SSD_HEADDIM
SSD_GROUPS = 4
SSD_STATE = 64
SSD_CHUNK = 128
D_FF = 2816
EPS = 1e-6

ROW_GROUP = 2048
IN_TILE = 512
IN_COLS_PADDED = 11 * IN_TILE
FFN_TILE = 256
TOKEN_TILE = 512
ATTN_Q_TILE = 256
CONV_CHUNK = 256
CONV_PAD = 8
VMEM_LIMIT = 56 * 1024 * 1024
NEG_BIG = -1e30


def _sigmoid(x):
    return 1.0 / (1.0 + jnp.exp(-x))


def _silu(x):
    return x * _sigmoid(x)


def _softplus(x):
    return jnp.maximum(x, 0.0) + jnp.log1p(jnp.exp(-jnp.abs(x)))


def _rmsnorm(x, g):
    return x * lax.rsqrt(jnp.mean(x * x, axis=-1, keepdims=True) + EPS) * g


def _dot(a, b):
    return jnp.dot(a, b, preferred_element_type=F32)


def _dot_nt(a, b):
    return lax.dot_general(a, b, (((1,), (1,)), ((), ())), preferred_element_type=F32)


def _params(*sem):
    return pltpu.CompilerParams(dimension_semantics=sem, vmem_limit_bytes=VMEM_LIMIT)


def _norm_mod_rows(x_ref, g_ref, sc_ref, sh_ref, h_scr):
    def body(i, carry):
        rows = pl.ds(pl.multiple_of(i * CONV_CHUNK, CONV_CHUNK), CONV_CHUNK)
        h = _rmsnorm(x_ref[rows, :], g_ref[...]) * (1.0 + sc_ref[...]) + sh_ref[...]
        h_scr[rows, :] = h.astype(BF16)
        return carry
    lax.fori_loop(0, ROW_GROUP // CONV_CHUNK, body, 0)


def _zero_conv_pads(u_scr):
    zeros = jnp.zeros((CONV_PAD, u_scr.shape[1]), F32)
    u_scr[0:CONV_PAD, :] = zeros
    u_scr[CONV_PAD + ROW_GROUP:2 * CONV_PAD + ROW_GROUP, :] = zeros


def _stage_rows(c):
    return slice(CONV_PAD + c * CONV_CHUNK, CONV_PAD + (c + 1) * CONV_CHUNK)


def _dwconv3_rows(u_scr, c, cols, w_ref, b_ref, seq_len):
    r0 = c * CONV_CHUNK
    base = CONV_PAD + r0
    width = cols.stop - cols.start
    prev = u_scr[base - 1:base - 1 + CONV_CHUNK, cols]
    cur = u_scr[base:base + CONV_CHUNK, cols]
    nxt = u_scr[base + 1:base + 1 + CONV_CHUNK, cols]
    row = lax.broadcasted_iota(jnp.int32, (CONV_CHUNK, width), 0)
    if r0 % seq_len == 0:
        prev = jnp.where(row == 0, 0.0, prev)
    if (r0 + CONV_CHUNK) % seq_len == 0:
        nxt = jnp.where(row == CONV_CHUNK - 1, 0.0, nxt)
    return prev * w_ref[0:1, :] + cur * w_ref[1:2, :] + nxt * w_ref[2:3, :] + b_ref[...]


def _ada_kernel(c_ref, w_ref, b_ref, o_ref):
    a = _silu(c_ref[...]).astype(BF16)
    o_ref[...] = _dot(a, w_ref[...].astype(BF16)) + b_ref[...]


def _ada(cvec, w_ada, b_ada):
    tn = 1536
    return pl.pallas_call(
        _ada_kernel,
        grid=(6 * D_MODEL // tn,),
        in_specs=[pl.BlockSpec((8, D_MODEL), lambda j: (0, 0)),
                  pl.BlockSpec((D_MODEL, tn), lambda j: (0, j)),
                  pl.BlockSpec((1, tn), lambda j: (0, j))],
        out_specs=pl.BlockSpec((8, tn), lambda j: (0, j)),
        out_shape=jax.ShapeDtypeStruct((8, 6 * D_MODEL), F32),
        compiler_params=_params("arbitrary"),
        name="ada_mod",
    )(cvec, w_ada, b_ada.reshape(1, -1))


Z_BLK, XS_BLK, GM_BLK, GS_BLK = 0, 1, 2, 3
BC_TILE, MLA_TILE, MISC_TILE = 8, 9, 10
DT_BLK = (MISC_TILE * IN_TILE + 256) // 128
ATTN_HEAD_COLS = 256


def _in_kernel(x_ref, sh_ref, sc_ref, g_ref, w_ref, cw_ref, cb_ref, o_ref, h_scr, u_scr, *, seq_len):
    j = pl.program_id(1)
    n_chunks = ROW_GROUP // CONV_CHUNK
    rows = lambda c: slice(c * CONV_CHUNK, (c + 1) * CONV_CHUNK)

    @pl.when(j == 0)
    def _():
        _norm_mod_rows(x_ref, g_ref, sc_ref, sh_ref, h_scr)
        _zero_conv_pads(u_scr)

    def pointwise(fn):
        for c in range(n_chunks):
            o_ref[rows(c), :] = fn(_dot(h_scr[rows(c), :], w_ref[...]))

    @pl.when(j <= 1)
    def _():
        pointwise(_silu)

    @pl.when((j >= 4) & (j <= 7))
    def _():
        pointwise(_sigmoid)

    @pl.when(j >= MLA_TILE)
    def _():
        pointwise(lambda u: u)

    @pl.when((j == 2) | (j == 3) | (j == BC_TILE))
    def _():
        def conv_out(c):
            o_ref[rows(c), :] = _silu(_dwconv3_rows(u_scr, c, slice(0, IN_TILE), cw_ref, cb_ref, seq_len))
        for c in range(n_chunks):
            u_scr[_stage_rows(c), :] = _dot(h_scr[rows(c), :], w_ref[...])
            if c >= 1:
                conv_out(c - 1)
        conv_out(n_chunks - 1)


def _in_proj(x2d, mod48, mod_row, norm_g, w_in_r, conv_w, conv_b, seq_len):
    t = x2d.shape[0]
    n_tiles = IN_COLS_PADDED // IN_TILE
    conv_idx = lambda r, j: (0, jnp.where(j == BC_TILE, 2, jnp.clip(j - 2, 0, 1)))
    return pl.pallas_call(
        functools.partial(_in_kernel, seq_len=seq_len),
        grid=(t // ROW_GROUP, n_tiles),
        in_specs=[pl.BlockSpec((ROW_GROUP, D_MODEL), lambda r, j: (r, 0)),
                  pl.BlockSpec((None, 1, D_MODEL), lambda r, j: (mod_row(r) * 6 + 0, 0, 0)),
                  pl.BlockSpec((None, 1, D_MODEL), lambda r, j: (mod_row(r) * 6 + 1, 0, 0)),
                  pl.BlockSpec((1, D_MODEL), lambda r, j: (0, 0)),
                  pl.BlockSpec((D_MODEL, IN_TILE), lambda r, j: (0, j)),
                  pl.BlockSpec((3, IN_TILE), conv_idx),
                  pl.BlockSpec((1, IN_TILE), conv_idx)],
        out_specs=pl.BlockSpec((ROW_GROUP, IN_TILE), lambda r, j: (r, j)),
        out_shape=jax.ShapeDtypeStruct((t, IN_COLS_PADDED), F32),
        scratch_shapes=[pltpu.VMEM((ROW_GROUP, D_MODEL), BF16),
                        pltpu.VMEM((ROW_GROUP + 2 * CONV_PAD, IN_TILE), F32)],
        compiler_params=_params("arbitrary", "arbitrary"),
        name="in_proj",
    )(x2d, mod48, mod48, norm_g, w_in_r, conv_w, conv_b)


def _mla_prep_kernel(*refs, latent):
    if latent:
        (p0_ref, p10_ref, qg_ref, kvg_ref, wuq_ref, wukv_ref, cos_ref, sin_ref,
         q_ref, k_ref, v_ref, ckv_ref, kr_ref) = refs
    else:
        (p0_ref, p10_ref, qg_ref, kvg_ref, wuq_ref, wukv_ref,
         q_ref, k_ref, v_ref, ckv_ref, kr_ref) = refs
    scale = 1.0 / math.sqrt(QK_NOPE + QK_ROPE)
    n_nope = N_HEADS * QK_NOPE
    p0 = p0_ref[...]
    cqn = _rmsnorm(p0[:, :Q_LORA], qg_ref[...]).astype(BF16)
    q = _dot(cqn, wuq_ref[...])
    q_rope = q[:, n_nope:2 * n_nope]
    kr = p10_ref[:, 0:128]
    if latent:
        cos, sin = cos_ref[...], sin_ref[...]
        q_rope = (q_rope * jnp.concatenate([cos] * N_HEADS, axis=1)
                  + q[:, 2 * n_nope:3 * n_nope] * jnp.concatenate([sin] * N_HEADS, axis=1))
        kr = kr * cos + p10_ref[:, 128:256] * sin
    kr_ref[...] = kr[:, :QK_ROPE]
    ckv_n = _rmsnorm(p0[:, Q_LORA:], kvg_ref[...])
    ckv_ref[...] = ckv_n
    kv = _dot(ckv_n.astype(BF16), wukv_ref[...])
    v_ref[...] = kv[:, n_nope:].astype(BF16)
    kr_bf = kr.astype(BF16)
    for h in range(N_HEADS):
        lo = slice(h * ATTN_HEAD_COLS, h * ATTN_HEAD_COLS + QK_NOPE)
        hi = slice(h * ATTN_HEAD_COLS + QK_NOPE, (h + 1) * ATTN_HEAD_COLS)
        head = slice(h * QK_NOPE, (h + 1) * QK_NOPE)
        q_ref[:, lo] = (q[:, head] * scale).astype(BF16)
        q_ref[:, hi] = (q_rope[:, head] * scale).astype(BF16)
        k_ref[:, lo] = kv[:, head].astype(BF16)
        k_ref[:, hi] = kr_bf


def _mla_prep(proj, q_norm_g, kv_norm_g, w_uq_r, w_ukv_r, rope_tables, seq_len):
    t = proj.shape[0]
    tm = TOKEN_TILE
    latent = rope_tables is not None
    in_specs = [pl.BlockSpec((tm, IN_TILE), lambda i: (i, MLA_TILE)),
                pl.BlockSpec((tm, IN_TILE), lambda i: (i, MISC_TILE)),
                pl.BlockSpec((1, Q_LORA), lambda i: (0, 0)),
                pl.BlockSpec((1, KV_LORA), lambda i: (0, 0)),
                pl.BlockSpec(w_uq_r.shape, lambda i: (0, 0)),
                pl.BlockSpec(w_ukv_r.shape, lambda i: (0, 0))]
    args = [proj, proj, q_norm_g, kv_norm_g, w_uq_r, w_ukv_r]
    if latent:
        per_seq = seq_len // tm
        in_specs += [pl.BlockSpec((tm, 128), lambda i: (i % per_seq, 0))] * 2
        args += list(rope_tables)
    widths = (N_HEADS * ATTN_HEAD_COLS, N_HEADS * ATTN_HEAD_COLS, N_HEADS * V_HEAD, KV_LORA, QK_ROPE)
    dtypes = (BF16, BF16, BF16, F32, F32)
    return pl.pallas_call(
        functools.partial(_mla_prep_kernel, latent=latent),
        grid=(t // tm,),
        in_specs=in_specs,
        out_specs=[pl.BlockSpec((tm, w), lambda i: (i, 0)) for w in widths],
        out_shape=[jax.ShapeDtypeStruct((t, w), d) for w, d in zip(widths, dtypes)],
        compiler_params=_params("arbitrary"),
        name="mla_prep",
    )(*args)


def _kv_up_kernel(c_ref, kr_ref, w_ref, k_ref, v_ref):
    kv = _dot(c_ref[...].astype(BF16), w_ref[...])
    v_ref[...] = kv[:, N_HEADS * QK_NOPE:].astype(BF16)
    kr_bf = kr_ref[...].astype(BF16)
    for h in range(N_HEADS):
        base = h * ATTN_HEAD_COLS
        k_ref[:, base:base + QK_NOPE] = kv[:, h * QK_NOPE:(h + 1) * QK_NOPE].astype(BF16)
        k_ref[:, base + QK_NOPE:base + ATTN_HEAD_COLS] = jnp.zeros((c_ref.shape[0], ATTN_HEAD_COLS - QK_NOPE), BF16)
        k_ref[:, base + QK_NOPE:base + QK_NOPE + QK_ROPE] = kr_bf


def _kv_up(ckv2d, kr2d, w_ukv_r):
    t = ckv2d.shape[0]
    tm = TOKEN_TILE
    widths = (N_HEADS * ATTN_HEAD_COLS, N_HEADS * V_HEAD)
    return pl.pallas_call(
        _kv_up_kernel,
        grid=(t // tm,),
        in_specs=[pl.BlockSpec((tm, KV_LORA), lambda i: (i, 0)),
                  pl.BlockSpec((tm, QK_ROPE), lambda i: (i, 0)),
                  pl.BlockSpec(w_ukv_r.shape, lambda i: (0, 0))],
        out_specs=[pl.BlockSpec((tm, w), lambda i: (i, 0)) for w in widths],
        out_shape=[jax.ShapeDtypeStruct((t, w), BF16) for w in widths],
        compiler_params=_params("arbitrary"),
        name="kv_up_ctx",
    )(ckv2d, kr2d, w_ukv_r)


def _attn_kernel(*refs):
    q_ref, o_ref = refs[0], refs[-1]
    segs = [(refs[i], refs[i + 1]) for i in range(1, len(refs) - 1, 2)]
    for h in range(N_HEADS):
        qk = slice(h * ATTN_HEAD_COLS, (h + 1) * ATTN_HEAD_COLS)
        vc = slice(h * V_HEAD, (h + 1) * V_HEAD)
        q = q_ref[0, :, qk]
        s = [_dot_nt(q, k_ref[0, :, qk]) for k_ref, _ in segs]
        m = functools.reduce(jnp.maximum, [jnp.max(si, axis=-1, keepdims=True) for si in s])
        p = [jnp.exp(si - m) for si in s]
        l = functools.reduce(jnp.add, [jnp.sum(pi, axis=-1, keepdims=True) for pi in p])
        o = functools.reduce(jnp.add, [_dot(pi.astype(BF16), v_ref[0, :, vc]) for pi, (_, v_ref) in zip(p, segs)])
        o_ref[0, :, vc] = (o / l).astype(BF16)


def _attention(q, segs):
    b, lq, _ = q.shape
    tq = ATTN_Q_TILE
    in_specs = [pl.BlockSpec((1, tq, q.shape[2]), lambda i, t: (i, t, 0))]
    args = [q]
    for k, v in segs:
        in_specs += [pl.BlockSpec((1,) + k.shape[1:], lambda i, t: (i, 0, 0)),
                     pl.BlockSpec((1,) + v.shape[1:], lambda i, t: (i, 0, 0))]
        args += [k, v]
    return pl.pallas_call(
        _attn_kernel,
        grid=(b, lq // tq),
        in_specs=in_specs,
        out_specs=pl.BlockSpec((1, tq, N_HEADS * V_HEAD), lambda i, t: (i, t, 0)),
        out_shape=jax.ShapeDtypeStruct((b, lq, N_HEADS * V_HEAD), BF16),
        compiler_params=_params("arbitrary", "arbitrary"),
        name="mla_attention",
    )(*args)


def _split3(x):
    hi = x.astype(BF16)
    r = x - hi.astype(F32)
    mid = r.astype(BF16)
    lo = (r - mid.astype(F32)).astype(BF16)
    return hi, mid, lo


def _ssd_sweep(d, x, bc, dt_raw, dtb_ref, a_ref, h_scr):
    q = SSD_CHUNK
    dt = _softplus(dt_raw + dtb_ref[...])
    a = dt * a_ref[...]
    ii = lax.broadcasted_iota(jnp.int32, (q, q), 0)
    jj = lax.broadcasted_iota(jnp.int32, (q, q), 1)
    keep = (ii >= jj) if d == 0 else (ii <= jj)
    tri = jnp.where(keep, 1.0, 0.0).astype(BF16)
    hi, mid, lo = _split3(a)
    acum = _dot(tri, hi) + _dot(tri, mid) + _dot(tri, lo)
    acum_t = acum.T
    dt_t = dt.T
    end = q - 1 if d == 0 else 0
    x_t = [x[:, i * 128:(i + 1) * 128].T for i in range(SSD_INNER // 128)]
    ys = []
    for g in range(SSD_GROUPS):
        bg = bc[:, g * SSD_STATE:(g + 1) * SSD_STATE].astype(BF16)
        cg = bc[:, SSD_GROUPS * SSD_STATE + g * SSD_STATE:SSD_GROUPS * SSD_STATE + (g + 1) * SSD_STATE].astype(BF16)
        scores = _dot_nt(cg, bg)
        for hh in range(SSD_HEADS // SSD_GROUPS):
            h = g * (SSD_HEADS // SSD_GROUPS) + hh
            k = d * SSD_HEADS + h
            xh = x[:, h * SSD_HEADDIM:(h + 1) * SSD_HEADDIM]
            a_col = jnp.broadcast_to(acum[:, k:k + 1], (q, q))
            a_row = acum_t[k:k + 1, :]
            dt_row = dt_t[k:k + 1, :]
            decay = jnp.exp(jnp.where(keep, a_col - a_row, NEG_BIG))
            m = (scores * decay * dt_row).astype(BF16)
            hst = h_scr[d, h]
            y = _dot(m, xh.astype(BF16)) + _dot_nt(cg, hst.astype(BF16)) * jnp.exp(a_col[:, :SSD_HEADDIM])
            ys.append(y)
            total = a_row[:, end:end + 1]
            w_row = jnp.exp(total - a_row) * dt_row
            xh_t = x_t[h // 2][(h % 2) * SSD_HEADDIM:(h % 2 + 1) * SSD_HEADDIM, :]
            st = _dot((xh_t * w_row).astype(BF16), bg)
            h_scr[d, h] = hst * jnp.exp(total) + st
    return jnp.concatenate(ys, axis=1)


def _ssd_kernel(*refs, nc, has_h0):
    if has_h0:
        xs_ref, zs_ref, bc_ref, dt_ref, dtb_ref, a_ref, dx_ref, h0_ref, y_ref, ht_ref, yf_scr, h_scr = refs
    else:
        xs_ref, zs_ref, bc_ref, dt_ref, dtb_ref, a_ref, dx_ref, y_ref, ht_ref, yf_scr, h_scr = refs
    s = pl.program_id(1)

    @pl.when(s == 0)
    def _():
        h_scr[...] = h0_ref[0] if has_h0 else jnp.zeros(h_scr.shape, F32)

    @pl.when(s < nc)
    def _():
        x = xs_ref[0]
        y = _ssd_sweep(0, x, bc_ref[0], dt_ref[0], dtb_ref, a_ref, h_scr)
        rows = pl.ds(pl.multiple_of(s * SSD_CHUNK, SSD_CHUNK), SSD_CHUNK)
        yf_scr[rows, :] = y + dx_ref[...] * x

    @pl.when(s >= nc)
    def _():
        c = 2 * nc - 1 - s
        y = _ssd_sweep(1, xs_ref[0], bc_ref[0], dt_ref[0], dtb_ref, a_ref, h_scr)
        rows = pl.ds(pl.multiple_of(c * SSD_CHUNK, SSD_CHUNK), SSD_CHUNK)
        y_ref[0] = (yf_scr[rows, :] + y) * zs_ref[0]

    @pl.when(s == 2 * nc - 1)
    def _():
        ht_ref[0] = h_scr[...]


def _ssd(proj3, h0, dt_bias128, a128, d_exp):
    b, l, _ = proj3.shape
    nc = l // SSD_CHUNK
    q = SSD_CHUNK
    chunk = lambda s: jnp.where(s < nc, s, 2 * nc - 1 - s)
    late = lambda s: jnp.where(s < nc, nc - 1, 2 * nc - 1 - s)
    st_shape = (1, 2, SSD_HEADS, SSD_HEADDIM, SSD_STATE)
    st_spec = pl.BlockSpec(st_shape, lambda i, s: (i, 0, 0, 0, 0))
    has_h0 = h0 is not None
    return pl.pallas_call(
        functools.partial(_ssd_kernel, nc=nc, has_h0=has_h0),
        grid=(b, 2 * nc),
        in_specs=[pl.BlockSpec((1, q, SSD_INNER), lambda i, s: (i, chunk(s), XS_BLK)),
                  pl.BlockSpec((1, q, SSD_INNER), lambda i, s: (i, late(s), Z_BLK)),
                  pl.BlockSpec((1, q, IN_TILE), lambda i, s: (i, chunk(s), BC_TILE)),
                  pl.BlockSpec((1, q, 128), lambda i, s: (i, chunk(s), DT_BLK)),
                  pl.BlockSpec((1, 128), lambda i, s: (0, 0)),
                  pl.BlockSpec((1, 128), lambda i, s: (0, 0)),
                  pl.BlockSpec((1, SSD_INNER), lambda i, s: (0, 0))] + ([st_spec] if has_h0 else []),
        out_specs=[pl.BlockSpec((1, q, SSD_INNER), lambda i, s: (i, late(s), 0)),
                   pl.BlockSpec(st_shape, lambda i, s: (i, 0, 0, 0, 0))],
        out_shape=[jax.ShapeDtypeStruct((b, l, SSD_INNER), F32),
                   jax.ShapeDtypeStruct((b,) + st_shape[1:], F32)],
        scratch_shapes=[pltpu.VMEM((l, SSD_INNER), F32),
                        pltpu.VMEM(st_shape[1:], F32)],
        compiler_params=_params("arbitrary", "arbitrary"),
        name="ssd_scan",
    )(proj3, proj3, proj3, proj3, dt_bias128, a128, d_exp, *([h0] if has_h0 else []))


def _merge_kernel(attn_ref, yz_ref, gm_ref, gs_ref, x_ref, g1_ref, ng_ref, womla_ref, wossd_ref, wout_ref, o_ref, w_scr):
    @pl.when(pl.program_id(0) == 0)
    def _():
        w_scr[0] = womla_ref[...].astype(BF16)
        w_scr[1] = wossd_ref[...].astype(BF16)
        w_scr[2] = wout_ref[...].astype(BF16)

    o_mla = _dot(attn_ref[...], w_scr[0])
    o_ssd = _dot(_rmsnorm(yz_ref[...], ng_ref[...]).astype(BF16), w_scr[1])
    merged = gm_ref[...] * o_mla + gs_ref[...] * o_ssd
    o_ref[...] = x_ref[...] + g1_ref[...] * _dot(merged.astype(BF16), w_scr[2])


def _merge(attn2d, yz2d, proj, x2d, mod48, mod_row, ssd_norm_g, w_o_mla, w_o_ssd, w_out):
    t = x2d.shape[0]
    tm = TOKEN_TILE
    row = lambda i: (i, 0)
    const = lambda i: (0, 0)
    wspec = pl.BlockSpec((D_MODEL, D_MODEL), const, pipeline_mode=pl.Buffered(1))
    return pl.pallas_call(
        _merge_kernel,
        grid=(t // tm,),
        in_specs=[pl.BlockSpec((tm, D_MODEL), row),
                  pl.BlockSpec((tm, D_MODEL), row),
                  pl.BlockSpec((tm, D_MODEL), lambda i: (i, GM_BLK)),
                  pl.BlockSpec((tm, D_MODEL), lambda i: (i, GS_BLK)),
                  pl.BlockSpec((tm, D_MODEL), row),
                  pl.BlockSpec((None, 1, D_MODEL), lambda i: (mod_row(i * tm // ROW_GROUP) * 6 + 2, 0, 0)),
                  pl.BlockSpec((1, D_MODEL), const),
                  wspec, wspec, wspec],
        out_specs=pl.BlockSpec((tm, D_MODEL), row),
        out_shape=jax.ShapeDtypeStruct((t, D_MODEL), F32),
        scratch_shapes=[pltpu.VMEM((3, D_MODEL, D_MODEL), BF16)],
        compiler_params=_params("arbitrary"),
        name="merge_out",
    )(attn2d, yz2d, proj, proj, x2d, mod48, ssd_norm_g, w_o_mla, w_o_ssd, w_out)


def _ffn_kernel(x_ref, sh_ref, sc_ref, g2_ref, ng_ref, wg_ref, wv_ref, cwg_ref, cwv_ref, cbg_ref, cbv_ref, wd_ref,
                fg_ref, o_ref, h_scr, wup_scr, wd_scr, u_scr, *, seq_len):
    j = pl.program_id(1)
    n_chunks = ROW_GROUP // CONV_CHUNK
    rows = lambda c: slice(c * CONV_CHUNK, (c + 1) * CONV_CHUNK)

    @pl.when(j == 0)
    def _():
        _norm_mod_rows(x_ref, ng_ref, sc_ref, sh_ref, h_scr)
        _zero_conv_pads(u_scr)
        o_ref[...] = jnp.zeros_like(o_ref)

    wup_scr[:, 0:FFN_TILE] = wg_ref[...].astype(BF16)
    wup_scr[:, FFN_TILE:2 * FFN_TILE] = wv_ref[...].astype(BF16)
    wd_scr[...] = wd_ref[...].astype(BF16)

    def gated(c):
        ug = _dwconv3_rows(u_scr, c, slice(0, FFN_TILE), cwg_ref, cbg_ref, seq_len)
        uv = _dwconv3_rows(u_scr, c, slice(FFN_TILE, 2 * FFN_TILE), cwv_ref, cbv_ref, seq_len)
        return (_silu(ug) * uv).astype(BF16)

    act = {}
    for c in range(n_chunks + 2):
        if c < n_chunks:
            u_scr[_stage_rows(c), :] = _dot(h_scr[rows(c), :], wup_scr[...])
        if c >= 2:
            o_ref[rows(c - 2), :] += _dot(act.pop(c - 2), wd_scr[...])
        if 1 <= c <= n_chunks:
            act[c - 1] = gated(c - 1)

    @pl.when(j == pl.num_programs(1) - 1)
    def _():
        def body(i, carry):
            rows = pl.ds(pl.multiple_of(i * 256, 256), 256)
            o_ref[rows, :] = _rmsnorm(x_ref[rows, :] + g2_ref[...] * o_ref[rows, :], fg_ref[...])
            return carry
        lax.fori_loop(0, ROW_GROUP // 256, body, 0)


def _ffn(x2d, mod48, mod_row, norm_g, w_up, conv_w, conv_b, w_down, final_g, seq_len):
    t = x2d.shape[0]
    nj = D_FF // FFN_TILE
    gate = lambda r, j: (0, j)
    val = lambda r, j: (0, nj + j)
    const = lambda r, j: (0, 0)
    mod = lambda k: pl.BlockSpec((None, 1, D_MODEL), lambda r, j: (mod_row(r) * 6 + k, 0, 0))
    return pl.pallas_call(
        functools.partial(_ffn_kernel, seq_len=seq_len),
        grid=(t // ROW_GROUP, nj),
        in_specs=[pl.BlockSpec((ROW_GROUP, D_MODEL), lambda r, j: (r, 0)),
                  mod(3), mod(4), mod(5),
                  pl.BlockSpec((1, D_MODEL), const),
                  pl.BlockSpec((D_MODEL, FFN_TILE), gate),
                  pl.BlockSpec((D_MODEL, FFN_TILE), val),
                  pl.BlockSpec((3, FFN_TILE), gate),
                  pl.BlockSpec((3, FFN_TILE), val),
                  pl.BlockSpec((1, FFN_TILE), gate),
                  pl.BlockSpec((1, FFN_TILE), val),
                  pl.BlockSpec((FFN_TILE, D_MODEL), lambda r, j: (j, 0)),
                  pl.BlockSpec((1, D_MODEL), const)],
        out_specs=pl.BlockSpec((ROW_GROUP, D_MODEL), lambda r, j: (r, 0)),
        out_shape=jax.ShapeDtypeStruct((t, D_MODEL), F32),
        scratch_shapes=[pltpu.VMEM((ROW_GROUP, D_MODEL), BF16),
                        pltpu.VMEM((D_MODEL, 2 * FFN_TILE), BF16),
                        pltpu.VMEM((FFN_TILE, D_MODEL), BF16),
                        pltpu.VMEM((ROW_GROUP + 2 * CONV_PAD, 2 * FFN_TILE), F32)],
        compiler_params=_params("arbitrary", "arbitrary"),
        name="conv_ffn",
    )(x2d, mod48, mod48, mod48, norm_g, w_up, w_up, conv_w, conv_w, conv_b, conv_b, w_down, final_g)


def _rope_tables(seq_len):
    t = np.arange(seq_len)
    row = (t // GRID_W).astype(np.float32)
    col = (t % GRID_W).astype(np.float32)
    n = QK_ROPE // 4
    inv = (np.float32(ROPE_BASE) ** (-np.arange(n, dtype=np.float32) / np.float32(n))).astype(np.float32)
    ar, ac = row[:, None] * inv, col[:, None] * inv
    cos64 = np.concatenate([np.cos(ar), np.cos(ar), np.cos(ac), np.cos(ac)], axis=1)
    sin64 = np.concatenate([-np.sin(ar), np.sin(ar), -np.sin(ac), np.sin(ac)], axis=1)
    zeros = np.zeros_like(cos64)
    return (jnp.asarray(np.concatenate([cos64, zeros], axis=1), F32),
            jnp.asarray(np.concatenate([sin64, zeros], axis=1), F32))


def _swap_rope_halves(w):
    lead = w.shape[:-1]
    return w.reshape(lead + (2, 2, QK_ROPE // 4))[..., ::-1, :].reshape(lead + (QK_ROPE,))


def _trunk_pass(x, mod48, mod_row, wts, ctx, latent):
    b, l, _ = x.shape
    x2d = x.reshape(b * l, D_MODEL)
    proj = _in_proj(x2d, mod48, mod_row, wts["norm_attn_g"], wts["w_in_r"], wts["ssd_conv_w"], wts["ssd_conv_b"], l)
    rope = _rope_tables(l) if latent else None
    w_uq_r = wts["w_uq_lat"] if latent else wts["w_uq_ctx"]
    q, k, v, ckv_n, kr = _mla_prep(proj, wts["q_norm_g"], wts["kv_norm_g"], w_uq_r, wts["w_ukv_r"], rope, l)
    shape3 = lambda a, n: a.reshape(b, n, a.shape[-1])
    segs = [(shape3(k, l), shape3(v, l))]
    h0 = None
    if ctx is not None:
        cache_ckv, cache_krope, h0 = ctx
        past = cache_ckv.shape[1]
        k_c, v_c = _kv_up(cache_ckv.reshape(b * past, KV_LORA), cache_krope.reshape(b * past, QK_ROPE), wts["w_ukv_r"])
        segs = [(shape3(k_c, past), shape3(v_c, past))] + segs
    attn = _attention(shape3(q, l), segs)
    kr3 = shape3(kr, l)
    yz, h_t = _ssd(proj.reshape(b, l, IN_COLS_PADDED), h0, wts["dt_bias128"], wts["a128"], wts["d_exp"])
    x1 = _merge(attn.reshape(b * l, -1), yz.reshape(b * l, -1), proj, x2d, mod48, mod_row, wts["ssd_norm_g"],
                wts["w_o_mla"], wts["w_o_ssd"], wts["w_out"])
    y = _ffn(x1, mod48, mod_row, wts["norm_ffn_g"], wts["w_up"], wts["ffn_conv_w"], wts["ffn_conv_b"], wts["w_down"],
             wts["final_norm_g"], l)
    return y.reshape(b, l, D_MODEL), ckv_n.reshape(b, l, KV_LORA), kr3, h_t


def kernel(x_prompt, x_sample, c, cache_ckv, cache_krope, state_ssd, c_ctx, w_ada, b_ada, norm_attn_g, w_in, q_norm_g,
           kv_norm_g, w_uq, w_ukv, w_o_mla, ssd_conv_w, ssd_conv_b, ssd_dt_bias, ssd_A_log, ssd_D, ssd_norm_g, w_o_ssd,
           w_out, norm_ffn_g, w_up, ffn_conv_w, ffn_conv_b, w_down, final_norm_g):
    depth = w_in.shape[0]
    assert depth == 1, "single trunk layer"
    dec_b = x_sample.shape[0]
    assert x_sample.shape[1] == ROW_GROUP and ROW_GROUP % x_prompt.shape[1] == 0
    lyr = 0

    cvec = jnp.zeros((8, D_MODEL), F32).at[0].set(c_ctx).at[1:1 + dec_b].set(c)
    mod48 = _ada(cvec, w_ada[lyr], b_ada[lyr]).reshape(8 * 6, 1, D_MODEL)

    wi = w_in[lyr]
    o = np.cumsum((0, Q_LORA, KV_LORA, QK_ROPE, SSD_INNER, SSD_INNER, SSD_GROUPS * SSD_STATE,
                   SSD_GROUPS * SSD_STATE, 2 * SSD_HEADS, D_MODEL, D_MODEL))
    piece = lambda i: wi[:, o[i]:o[i + 1]]
    w_kr = piece(2)
    zcols = lambda n: jnp.zeros((D_MODEL, n), F32)
    misc_pad = IN_TILE - 256 - 2 * SSD_HEADS
    w_in_r = jnp.concatenate([piece(3), piece(4), piece(8), piece(9), piece(5), piece(6), piece(0), piece(1),
                              w_kr, zcols(QK_ROPE), _swap_rope_halves(w_kr), zcols(QK_ROPE), piece(7),
                              zcols(misc_pad)], axis=1).astype(BF16)
    wq = w_uq[lyr].reshape(Q_LORA, N_HEADS, QK_NOPE + QK_ROPE)
    wq_nope = wq[:, :, :QK_NOPE].reshape(Q_LORA, -1)
    wq_rope = wq[:, :, QK_NOPE:]
    pad_rope = lambda w: jnp.pad(w, ((0, 0), (0, 0), (0, 128 - QK_ROPE))).reshape(Q_LORA, -1)
    w_uq_ctx = jnp.concatenate([wq_nope, pad_rope(wq_rope)], axis=1).astype(BF16)
    w_uq_lat = jnp.concatenate([wq_nope, pad_rope(wq_rope), pad_rope(_swap_rope_halves(wq_rope))], axis=1).astype(BF16)
    wkv = w_ukv[lyr].reshape(KV_LORA, N_HEADS, QK_NOPE + V_HEAD)
    w_ukv_r = jnp.concatenate([wkv[:, :, :QK_NOPE].reshape(KV_LORA, -1),
                               wkv[:, :, QK_NOPE:].reshape(KV_LORA, -1)], axis=1).astype(BF16)
    pad128 = lambda a: jnp.pad(a.reshape(1, -1), ((0, 0), (0, 128 - a.size)))
    wts = {
        "norm_attn_g": norm_attn_g[lyr].reshape(1, -1), "w_in_r": w_in_r,
        "ssd_conv_w": ssd_conv_w[lyr], "ssd_conv_b": ssd_conv_b[lyr].reshape(1, -1),
        "q_norm_g": q_norm_g[lyr].reshape(1, -1), "kv_norm_g": kv_norm_g[lyr].reshape(1, -1),
        "w_uq_ctx": w_uq_ctx, "w_uq_lat": w_uq_lat, "w_ukv_r": w_ukv_r,
        "dt_bias128": pad128(ssd_dt_bias[lyr]), "a128": pad128(-jnp.exp(ssd_A_log[lyr])),
        "d_exp": jnp.repeat(ssd_D[lyr], SSD_HEADDIM).reshape(1, -1),
        "ssd_norm_g": ssd_norm_g[lyr].reshape(1, -1),
        "w_o_mla": w_o_mla[lyr], "w_o_ssd": w_o_ssd[lyr], "w_out": w_out[lyr],
        "norm_ffn_g": norm_ffn_g[lyr].reshape(1, -1), "w_up": w_up[lyr],
        "ffn_conv_w": ffn_conv_w[lyr], "ffn_conv_b": ffn_conv_b[lyr].reshape(1, -1),
        "w_down": w_down[lyr], "final_norm_g": final_norm_g.reshape(1, -1),
    }

    y_p, ckv_p, kr_p, st_p = _trunk_pass(x_prompt, mod48, lambda r: 0, wts, None, False)
    ctx = (cache_ckv[:, lyr], cache_krope[:, lyr], state_ssd[:, lyr])
    y_s, _, _, _ = _trunk_pass(x_sample, mod48, lambda r: 1 + r, wts, ctx, True)
    return y_p, y_s, ckv_p[:, None], kr_p[:, None], st_p[:, None]
```

```python
import functools
import math

import jax
import jax.numpy as jnp
import numpy as np
from jax import lax
from jax.experimental import pallas as pl
from jax.experimental.pallas import tpu as pltpu

F32 = jnp.float32
BF16 = jnp.bfloat16

D_MODEL = 1024
GRID_W = 64
N_HEADS = 8
QK_NOPE = 128
QK_ROPE = 64
V_HEAD = 128
Q_LORA = 256
KV_LORA = 256
ROPE_BASE = 10000.0
SSD_HEADS = 16
SSD_HEADDIM = 64
SSD_INNER = SSD_HEADS * SSD_HEADDIM
SSD_GROUPS = 4
SSD_STATE = 64
SSD_CHUNK = 128
D_FF = 2816
EPS = 1e-6

ROW_GROUP = 2048
IN_TILE = 512
IN_COLS_PADDED = 11 * IN_TILE
FFN_TILE = 256
TOKEN_TILE = 512
ATTN_Q_TILE = 256
CONV_CHUNK = 256
CONV_PAD = 8
VMEM_LIMIT = 56 * 1024 * 1024
NEG_BIG = -1e30


def _sigmoid(x):
    return 1.0 / (1.0 + jnp.exp(-x))


def _silu(x):
    return x * _sigmoid(x)


def _softplus(x):
    e = jnp.exp(-jnp.abs(x))
    u = 1.0 + e
    log1p_e = jnp.where(u == 1.0, e, e * jnp.log(u) / jnp.where(u == 1.0, 1.0, u - 1.0))
    return jnp.maximum(x, 0.0) + log1p_e


def _rmsnorm(x, g):
    return x * lax.rsqrt(jnp.mean(x * x, axis=-1, keepdims=True) + EPS) * g


def _dot(a, b):
    return jnp.dot(a, b, preferred_element_type=F32)


def _dot_nt(a, b):
    return lax.dot_general(a, b, (((1,), (1,)), ((), ())), preferred_element_type=F32)


def _params(*sem):
    return pltpu.CompilerParams(dimension_semantics=sem, vmem_limit_bytes=VMEM_LIMIT)


def _norm_mod_rows(x_ref, g_ref, sc_ref, sh_ref, h_scr):
    def body(i, carry):
        rows = pl.ds(pl.multiple_of(i * CONV_CHUNK, CONV_CHUNK), CONV_CHUNK)
        h = _rmsnorm(x_ref[rows, :], g_ref[...]) * (1.0 + sc_ref[...]) + sh_ref[...]
        h_scr[rows, :] = h.astype(BF16)
        return carry
    lax.fori_loop(0, ROW_GROUP // CONV_CHUNK, body, 0)


def _zero_conv_pads(u_scr):
    zeros = jnp.zeros((CONV_PAD, u_scr.shape[1]), F32)
    u_scr[0:CONV_PAD, :] = zeros
    u_scr[CONV_PAD + ROW_GROUP:2 * CONV_PAD + ROW_GROUP, :] = zeros


def _stage_rows(c):
    return slice(CONV_PAD + c * CONV_CHUNK, CONV_PAD + (c + 1) * CONV_CHUNK)


def _dwconv3_rows(u_scr, c, cols, w_ref, b_ref, seq_len):
    r0 = c * CONV_CHUNK
    base = CONV_PAD + r0
    width = cols.stop - cols.start
    prev = u_scr[base - 1:base - 1 + CONV_CHUNK, cols]
    cur = u_scr[base:base + CONV_CHUNK, cols]
    nxt = u_scr[base + 1:base + 1 + CONV_CHUNK, cols]
    row = lax.broadcasted_iota(jnp.int32, (CONV_CHUNK, width), 0)
    if r0 % seq_len == 0:
        prev = jnp.where(row == 0, 0.0, prev)
    if (r0 + CONV_CHUNK) % seq_len == 0:
        nxt = jnp.where(row == CONV_CHUNK - 1, 0.0, nxt)
    return prev * w_ref[0:1, :] + cur * w_ref[1:2, :] + nxt * w_ref[2:3, :] + b_ref[...]


def _ada_kernel(c_ref, w_ref, b_ref, o_ref):
    a = _silu(c_ref[...]).astype(BF16)
    o_ref[...] = _dot(a, w_ref[...].astype(BF16)) + b_ref[...]


def _ada(cvec, w_ada, b_ada):
    tn = 1536
    return pl.pallas_call(
        _ada_kernel,
        grid=(6 * D_MODEL // tn,),
        in_specs=[pl.BlockSpec((8, D_MODEL), lambda j: (0, 0)),
                  pl.BlockSpec((D_MODEL, tn), lambda j: (0, j)),
                  pl.BlockSpec((1, tn), lambda j: (0, j))],
        out_specs=pl.BlockSpec((8, tn), lambda j: (0, j)),
        out_shape=jax.ShapeDtypeStruct((8, 6 * D_MODEL), F32),
        compiler_params=_params("arbitrary"),
        name="ada_mod",
    )(cvec, w_ada, b_ada.reshape(1, -1))


Z_BLK, XS_BLK, GM_BLK, GS_BLK = 0, 1, 2, 3
BC_TILE, MLA_TILE, MISC_TILE = 8, 9, 10
DT_BLK = (MISC_TILE * IN_TILE + 256) // 128
ATTN_HEAD_COLS = 256


def _in_kernel(x_ref, sh_ref, sc_ref, g_ref, w_ref, cw_ref, cb_ref, o_ref, h_scr, u_scr, *, seq_len):
    j = pl.program_id(1)
    n_chunks = ROW_GROUP // CONV_CHUNK
    rows = lambda c: slice(c * CONV_CHUNK, (c + 1) * CONV_CHUNK)

    @pl.when(j == 0)
    def _():
        _norm_mod_rows(x_ref, g_ref, sc_ref, sh_ref, h_scr)
        _zero_conv_pads(u_scr)

    def pointwise(fn):
        for c in range(n_chunks):
            o_ref[rows(c), :] = fn(_dot(h_scr[rows(c), :], w_ref[...]))

    @pl.when(j <= 1)
    def _():
        pointwise(_silu)

    @pl.when((j >= 4) & (j <= 7))
    def _():
        pointwise(_sigmoid)

    @pl.when(j >= MLA_TILE)
    def _():
        pointwise(lambda u: u)

    @pl.when((j == 2) | (j == 3) | (j == BC_TILE))
    def _():
        def conv_out(c):
            o_ref[rows(c), :] = _silu(_dwconv3_rows(u_scr, c, slice(0, IN_TILE), cw_ref, cb_ref, seq_len))
        for c in range(n_chunks):
            u_scr[_stage_rows(c), :] = _dot(h_scr[rows(c), :], w_ref[...])
            if c >= 1:
                conv_out(c - 1)
        conv_out(n_chunks - 1)


def _in_proj(x2d, mod48, mod_row, norm_g, w_in_r, conv_w, conv_b, seq_len):
    t = x2d.shape[0]
    n_tiles = IN_COLS_PADDED // IN_TILE
    conv_idx = lambda r, j: (0, jnp.where(j == BC_TILE, 2, jnp.clip(j - 2, 0, 1)))
    return pl.pallas_call(
        functools.partial(_in_kernel, seq_len=seq_len),
        grid=(t // ROW_GROUP, n_tiles),
        in_specs=[pl.BlockSpec((ROW_GROUP, D_MODEL), lambda r, j: (r, 0)),
                  pl.BlockSpec((None, 1, D_MODEL), lambda r, j: (mod_row(r) * 6 + 0, 0, 0)),
                  pl.BlockSpec((None, 1, D_MODEL), lambda r, j: (mod_row(r) * 6 + 1, 0, 0)),
                  pl.BlockSpec((1, D_MODEL), lambda r, j: (0, 0)),
                  pl.BlockSpec((D_MODEL, IN_TILE), lambda r, j: (0, j)),
                  pl.BlockSpec((3, IN_TILE), conv_idx),
                  pl.BlockSpec((1, IN_TILE), conv_idx)],
        out_specs=pl.BlockSpec((ROW_GROUP, IN_TILE), lambda r, j: (r, j)),
        out_shape=jax.ShapeDtypeStruct((t, IN_COLS_PADDED), F32),
        scratch_shapes=[pltpu.VMEM((ROW_GROUP, D_MODEL), BF16),
                        pltpu.VMEM((ROW_GROUP + 2 * CONV_PAD, IN_TILE), F32)],
        compiler_params=_params("arbitrary", "arbitrary"),
        name="in_proj",
    )(x2d, mod48, mod48, norm_g, w_in_r, conv_w, conv_b)


def _mla_prep_kernel(*refs, latent):
    if latent:
        (p0_ref, p10_ref, qg_ref, kvg_ref, wuq_ref, wukv_ref, cos_ref, sin_ref,
         q_ref, k_ref, v_ref, ckv_ref, kr_ref) = refs
    else:
        (p0_ref, p10_ref, qg_ref, kvg_ref, wuq_ref, wukv_ref,
         q_ref, k_ref, v_ref, ckv_ref, kr_ref) = refs
    scale = 1.0 / math.sqrt(QK_NOPE + QK_ROPE)
    n_nope = N_HEADS * QK_NOPE
    p0 = p0_ref[...]
    cqn = _rmsnorm(p0[:, :Q_LORA], qg_ref[...]).astype(BF16)
    q = _dot(cqn, wuq_ref[...])
    q_rope = q[:, n_nope:2 * n_nope]
    kr = p10_ref[:, 0:128]
    if latent:
        cos, sin = cos_ref[...], sin_ref[...]
        q_rope = (q_rope * jnp.concatenate([cos] * N_HEADS, axis=1)
                  + q[:, 2 * n_nope:3 * n_nope] * jnp.concatenate([sin] * N_HEADS, axis=1))
        kr = kr * cos + p10_ref[:, 128:256] * sin
    kr_ref[...] = kr[:, :QK_ROPE]
    ckv_n = _rmsnorm(p0[:, Q_LORA:], kvg_ref[...])
    ckv_ref[...] = ckv_n
    kv = _dot(ckv_n.astype(BF16), wukv_ref[...])
    v_ref[...] = kv[:, n_nope:].astype(BF16)
    kr_bf = kr.astype(BF16)
    for h in range(N_HEADS):
        lo = slice(h * ATTN_HEAD_COLS, h * ATTN_HEAD_COLS + QK_NOPE)
        hi = slice(h * ATTN_HEAD_COLS + QK_NOPE, (h + 1) * ATTN_HEAD_COLS)
        head = slice(h * QK_NOPE, (h + 1) * QK_NOPE)
        q_ref[:, lo] = (q[:, head] * scale).astype(BF16)
        q_ref[:, hi] = (q_rope[:, head] * scale).astype(BF16)
        k_ref[:, lo] = kv[:, head].astype(BF16)
        k_ref[:, hi] = kr_bf


def _mla_prep(proj, q_norm_g, kv_norm_g, w_uq_r, w_ukv_r, rope_tables, seq_len):
    t = proj.shape[0]
    tm = TOKEN_TILE
    latent = rope_tables is not None
    in_specs = [pl.BlockSpec((tm, IN_TILE), lambda i: (i, MLA_TILE)),
                pl.BlockSpec((tm, IN_TILE), lambda i: (i, MISC_TILE)),
                pl.BlockSpec((1, Q_LORA), lambda i: (0, 0)),
                pl.BlockSpec((1, KV_LORA), lambda i: (0, 0)),
                pl.BlockSpec(w_uq_r.shape, lambda i: (0, 0)),
                pl.BlockSpec(w_ukv_r.shape, lambda i: (0, 0))]
    args = [proj, proj, q_norm_g, kv_norm_g, w_uq_r, w_ukv_r]
    if latent:
        per_seq = seq_len // tm
        in_specs += [pl.BlockSpec((tm, 128), lambda i: (i % per_seq, 0))] * 2
        args += list(rope_tables)
    widths = (N_HEADS * ATTN_HEAD_COLS, N_HEADS * ATTN_HEAD_COLS, N_HEADS * V_HEAD, KV_LORA, QK_ROPE)
    dtypes = (BF16, BF16, BF16, F32, F32)
    return pl.pallas_call(
        functools.partial(_mla_prep_kernel, latent=latent),
        grid=(t // tm,),
        in_specs=in_specs,
        out_specs=[pl.BlockSpec((tm, w), lambda i: (i, 0)) for w in widths],
        out_shape=[jax.ShapeDtypeStruct((t, w), d) for w, d in zip(widths, dtypes)],
        compiler_params=_params("arbitrary"),
        name="mla_prep",
    )(*args)


def _kv_up_kernel(c_ref, kr_ref, w_ref, k_ref, v_ref):
    kv = _dot(c_ref[...].astype(BF16), w_ref[...])
    v_ref[...] = kv[:, N_HEADS * QK_NOPE:].astype(BF16)
    kr_bf = kr_ref[...].astype(BF16)
    for h in range(N_HEADS):
        base = h * ATTN_HEAD_COLS
        k_ref[:, base:base + QK_NOPE] = kv[:, h * QK_NOPE:(h + 1) * QK_NOPE].astype(BF16)
        k_ref[:, base + QK_NOPE:base + ATTN_HEAD_COLS] = jnp.zeros((c_ref.shape[0], ATTN_HEAD_COLS - QK_NOPE), BF16)
        k_ref[:, base + QK_NOPE:base + QK_NOPE + QK_ROPE] = kr_bf


def _kv_up(ckv2d, kr2d, w_ukv_r):
    t = ckv2d.shape[0]
    tm = TOKEN_TILE
    widths = (N_HEADS * ATTN_HEAD_COLS, N_HEADS * V_HEAD)
    return pl.pallas_call(
        _kv_up_kernel,
        grid=(t // tm,),
        in_specs=[pl.BlockSpec((tm, KV_LORA), lambda i: (i, 0)),
                  pl.BlockSpec((tm, QK_ROPE), lambda i: (i, 0)),
                  pl.BlockSpec(w_ukv_r.shape, lambda i: (0, 0))],
        out_specs=[pl.BlockSpec((tm, w), lambda i: (i, 0)) for w in widths],
        out_shape=[jax.ShapeDtypeStruct((t, w), BF16) for w in widths],
        compiler_params=_params("arbitrary"),
        name="kv_up_ctx",
    )(ckv2d, kr2d, w_ukv_r)


def _attn_kernel(*refs):
    q_ref, o_ref = refs[0], refs[-1]
    segs = [(refs[i], refs[i + 1]) for i in range(1, len(refs) - 1, 2)]
    for h in range(N_HEADS):
        qk = slice(h * ATTN_HEAD_COLS, (h + 1) * ATTN_HEAD_COLS)
        vc = slice(h * V_HEAD, (h + 1) * V_HEAD)
        q = q_ref[0, :, qk]
        s = [_dot_nt(q, k_ref[0, :, qk]) for k_ref, _ in segs]
        m = functools.reduce(jnp.maximum, [jnp.max(si, axis=-1, keepdims=True) for si in s])
        p = [jnp.exp(si - m) for si in s]
        l = functools.reduce(jnp.add, [jnp.sum(pi, axis=-1, keepdims=True) for pi in p])
        o = functools.reduce(jnp.add, [_dot(pi.astype(BF16), v_ref[0, :, vc]) for pi, (_, v_ref) in zip(p, segs)])
        o_ref[0, :, vc] = (o / l).astype(BF16)


def _attention(q, segs):
    b, lq, _ = q.shape
    tq = ATTN_Q_TILE
    in_specs = [pl.BlockSpec((1, tq, q.shape[2]), lambda i, t: (i, t, 0))]
    args = [q]
    for k, v in segs:
        in_specs += [pl.BlockSpec((1,) + k.shape[1:], lambda i, t: (i, 0, 0)),
                     pl.BlockSpec((1,) + v.shape[1:], lambda i, t: (i, 0, 0))]
        args += [k, v]
    return pl.pallas_call(
        _attn_kernel,
        grid=(b, lq // tq),
        in_specs=in_specs,
        out_specs=pl.BlockSpec((1, tq, N_HEADS * V_HEAD), lambda i, t: (i, t, 0)),
        out_shape=jax.ShapeDtypeStruct((b, lq, N_HEADS * V_HEAD), BF16),
        compiler_params=_params("arbitrary", "arbitrary"),
        name="mla_attention",
    )(*args)


def _split3(x):
    hi = x.astype(BF16)
    r = x - hi.astype(F32)
    mid = r.astype(BF16)
    lo = (r - mid.astype(F32)).astype(BF16)
    return hi, mid, lo


def _exact_dot(parts, sel):
    return functools.reduce(jnp.add, [_dot(p, sel) for p in parts])


def _exact_dot_rows(sel, parts):
    return functools.reduce(jnp.add, [_dot(sel, p) for p in parts])


def _exact_dot_nt(sel, parts):
    return functools.reduce(jnp.add, [_dot_nt(sel, p) for p in parts])


HEADS_PER_GROUP = SSD_HEADS // SSD_GROUPS
GROUP_COLS = HEADS_PER_GROUP * SSD_HEADDIM


def _ssd_kernel(*refs, nc, has_h0):
    if has_h0:
        (xs_ref, zs_ref, bc_ref, dt_ref, dtb_ref, a_ref, dx_ref, h0_ref, y_ref, ht_ref,
         yl_scr, acum_scr, acumt_scr, dtt_scr, tot_scr, sb_scr, h_scr, esel_scr) = refs
    else:
        (xs_ref, zs_ref, bc_ref, dt_ref, dtb_ref, a_ref, dx_ref, y_ref, ht_ref,
         yl_scr, acum_scr, acumt_scr, dtt_scr, tot_scr, sb_scr, h_scr, esel_scr) = refs
    s = pl.program_id(1)
    q = SSD_CHUNK
    n_bc = SSD_GROUPS * SSD_STATE
    lane = lax.broadcasted_iota(jnp.int32, (q, 128), 1)
    low_half = lane < SSD_HEADDIM
    ii = lax.broadcasted_iota(jnp.int32, (q, q), 0)
    jj = lax.broadcasted_iota(jnp.int32, (q, q), 1)
    lower, upper = ii >= jj, ii <= jj

    def lane_bcast(parts, d):
        return _exact_dot(parts, esel_scr[d])

    def stacked_states(d):
        return [jnp.concatenate([h_scr[d, 2 * i], h_scr[d, 2 * i + 1]], axis=0).astype(BF16)
                for i in range(SSD_GROUPS // 2)]

    def group_c(bc, g):
        i, r = divmod(g, 2)
        cpair = bc[:, n_bc + i * 128:n_bc + (i + 1) * 128]
        return jnp.where(low_half if r == 0 else ~low_half, cpair, 0.0).astype(BF16)

    @pl.when(s == 0)
    def _():
        k = lax.broadcasted_iota(jnp.int32, (128, SSD_INNER), 0)
        head = lax.broadcasted_iota(jnp.int32, (128, SSD_INNER), 1) // SSD_HEADDIM
        for d in range(2):
            esel_scr[d] = jnp.where(k == d * SSD_HEADS + head, 1.0, 0.0).astype(BF16)
        if has_h0:
            for d in range(2):
                for g in range(SSD_GROUPS):
                    hpn = h0_ref[0, d, g * HEADS_PER_GROUP:(g + 1) * HEADS_PER_GROUP].reshape(GROUP_COLS, SSD_STATE)
                    h_scr[d, g] = hpn.T
        else:
            h_scr[...] = jnp.zeros(h_scr.shape, F32)
        tri_f = jnp.where(lower, 1.0, 0.0).astype(BF16)
        tri_b = jnp.where(upper, 1.0, 0.0).astype(BF16)
        fwd_col = lane < SSD_HEADS
        for c in range(nc):
            crow = slice(c * q, (c + 1) * q)
            dt = _softplus(dt_ref[0, crow, :] + dtb_ref[...])
            parts = _split3(dt * a_ref[...])
            acum = jnp.where(fwd_col, _exact_dot_rows(tri_f, parts),
                             _exact_dot_rows(tri_b, parts))
            acum_scr[crow, :] = acum
            tot = jnp.where(fwd_col[0:1], acum[q - 1:q, :], acum[0:1, :])
            tot_scr[c] = jnp.broadcast_to(tot, (8, 128))
            acumt_scr[c] = acum.T
            dtt_scr[c] = dt.T

    @pl.when(s < nc)
    def _():
        rows = pl.ds(pl.multiple_of(s * q, q), q)
        x = xs_ref[0]
        bc = bc_ref[0]
        acum = acum_scr[rows, :]
        acum_t, dt_t = acumt_scr[s], dtt_scr[s]
        tot8 = tot_scr[s]
        e = jnp.exp(acum)
        e_hi = e.astype(BF16)
        eb_f = lane_bcast([e_hi, (e - e_hi.astype(F32)).astype(BF16)], 0)
        cd_f = jnp.exp(lane_bcast(_split3(tot8), 0))[0:1]
        b_t = [bc[:, i * 128:(i + 1) * 128].T for i in range(SSD_GROUPS // 2)]
        h_in = stacked_states(0)
        for g in range(SSD_GROUPS):
            i, r = divmod(g, 2)
            cm = group_c(bc, g)
            scores = _dot_nt(cm, bc[:, i * 128:(i + 1) * 128].astype(BF16))
            bg_t = b_t[i][r * SSD_STATE:(r + 1) * SSD_STATE, :]
            gcols = slice(g * GROUP_COLS, (g + 1) * GROUP_COLS)
            y_off = _dot(cm, h_in[i]) * eb_f[:, gcols]
            for t in range(HEADS_PER_GROUP // 2):
                pc = slice((2 * g + t) * 128, (2 * g + t + 1) * 128)
                xp = x[:, pc]
                x_half = [jnp.where(low_half, xp, 0.0).astype(BF16), jnp.where(low_half, 0.0, xp).astype(BF16)]
                m, sf, sb = [], [], []
                for u in range(2):
                    kf = g * HEADS_PER_GROUP + 2 * t + u
                    kb = SSD_HEADS + kf
                    af_col = jnp.broadcast_to(acum[:, kf:kf + 1], (q, q))
                    ab_col = jnp.broadcast_to(acum[:, kb:kb + 1], (q, q))
                    af_row, ab_row = acum_t[kf:kf + 1, :], acum_t[kb:kb + 1, :]
                    dtf_row, dtb_row = dt_t[kf:kf + 1, :], dt_t[kb:kb + 1, :]
                    decay = (jnp.exp(jnp.where(lower, af_col - af_row, NEG_BIG)) * dtf_row
                             + jnp.exp(jnp.where(upper, ab_col - ab_row, NEG_BIG)) * dtb_row)
                    m.append((scores * decay).astype(BF16))
                    wf = jnp.exp(af_row[:, q - 1:q] - af_row) * dtf_row
                    wb = jnp.exp(ab_row[:, 0:1] - ab_row) * dtb_row
                    sf.append((bg_t * wf).astype(BF16))
                    sb.append((bg_t * wb).astype(BF16))
                y_pair = _dot(m[0], x_half[0]) + _dot(m[1], x_half[1])
                tc = slice(t * 128, (t + 1) * 128)
                yl_scr[rows, pc] = y_pair + y_off[:, tc] + dx_ref[:, pc] * xp
                h_scr[0, g, :, tc] = (h_scr[0, g, :, tc] * cd_f[:, pc]
                                      + _dot(sf[0], x_half[0]) + _dot(sf[1], x_half[1]))
                sb_scr[s, g, :, tc] = _dot(sb[0], x_half[0]) + _dot(sb[1], x_half[1])

    @pl.when(s >= nc)
    def _():
        c = 2 * nc - 1 - s
        rows = pl.ds(pl.multiple_of(c * q, q), q)
        bc = bc_ref[0]
        e = jnp.exp(acum_scr[rows, :])
        e_hi = e.astype(BF16)
        eb_b = lane_bcast([e_hi, (e - e_hi.astype(F32)).astype(BF16)], 1)
        cd_b = jnp.exp(lane_bcast(_split3(tot_scr[c]), 1))[0:1]
        h_in = stacked_states(1)
        for g in range(SSD_GROUPS):
            gcols = slice(g * GROUP_COLS, (g + 1) * GROUP_COLS)
            y_off = _dot(group_c(bc, g), h_in[g // 2]) * eb_b[:, gcols]
            y_ref[0, :, gcols] = (yl_scr[rows, gcols] + y_off) * zs_ref[0, :, gcols]
            h_scr[1, g] = h_scr[1, g] * cd_b[:, gcols] + sb_scr[c, g]

    @pl.when(s == 2 * nc - 1)
    def _():
        for d in range(2):
            for g in range(SSD_GROUPS):
                ht_ref[0, d, g * HEADS_PER_GROUP:(g + 1) * HEADS_PER_GROUP] = h_scr[d, g].T.reshape(
                    HEADS_PER_GROUP, SSD_HEADDIM, SSD_STATE)


def _ssd(proj3, h0, dt_bias128, a128, d_exp):
    b, l, _ = proj3.shape
    nc = l // SSD_CHUNK
    q = SSD_CHUNK
    early = lambda s: jnp.minimum(s, nc - 1)
    chunk = lambda s: jnp.where(s < nc, s, 2 * nc - 1 - s)
    late = lambda s: jnp.where(s < nc, nc - 1, 2 * nc - 1 - s)
    st_shape = (1, 2, SSD_HEADS, SSD_HEADDIM, SSD_STATE)
    st_spec = pl.BlockSpec(st_shape, lambda i, s: (i, 0, 0, 0, 0))
    has_h0 = h0 is not None
    return pl.pallas_call(
        functools.partial(_ssd_kernel, nc=nc, has_h0=has_h0),
        grid=(b, 2 * nc),
        in_specs=[pl.BlockSpec((1, q, SSD_INNER), lambda i, s: (i, early(s), XS_BLK)),
                  pl.BlockSpec((1, q, SSD_INNER), lambda i, s: (i, late(s), Z_BLK)),
                  pl.BlockSpec((1, q, IN_TILE), lambda i, s: (i, chunk(s), BC_TILE)),
                  pl.BlockSpec((1, l, 128), lambda i, s: (i, 0, DT_BLK)),
                  pl.BlockSpec((1, 128), lambda i, s: (0, 0)),
                  pl.BlockSpec((1, 128), lambda i, s: (0, 0)),
                  pl.BlockSpec((1, SSD_INNER), lambda i, s: (0, 0))] + ([st_spec] if has_h0 else []),
        out_specs=[pl.BlockSpec((1, q, SSD_INNER), lambda i, s: (i, late(s), 0)),
                   pl.BlockSpec(st_shape, lambda i, s: (i, 0, 0, 0, 0))],
        out_shape=[jax.ShapeDtypeStruct((b, l, SSD_INNER), F32),
                   jax.ShapeDtypeStruct((b,) + st_shape[1:], F32)],
        scratch_shapes=[pltpu.VMEM((l, SSD_INNER), F32),
                        pltpu.VMEM((l, 128), F32),
                        pltpu.VMEM((nc, 128, q), F32),
                        pltpu.VMEM((nc, 128, q), F32),
                        pltpu.VMEM((nc, 8, 128), F32),
                        pltpu.VMEM((nc, SSD_GROUPS, SSD_STATE, GROUP_COLS), F32),
                        pltpu.VMEM((2, SSD_GROUPS, SSD_STATE, GROUP_COLS), F32),
                        pltpu.VMEM((2, 128, SSD_INNER), BF16)],
        compiler_params=_params("arbitrary", "arbitrary"),
        name="ssd_scan",
    )(proj3, proj3, proj3, proj3, dt_bias128, a128, d_exp, *([h0] if has_h0 else []))


def _merge_kernel(attn_ref, yz_ref, gm_ref, gs_ref, x_ref, g1_ref, ng_ref, womla_ref, wossd_ref, wout_ref, o_ref, w_scr):
    @pl.when(pl.program_id(0) == 0)
    def _():
        w_scr[0] = womla_ref[...].astype(BF16)
        w_scr[1] = wossd_ref[...].astype(BF16)
        w_scr[2] = wout_ref[...].astype(BF16)

    o_mla = _dot(attn_ref[...], w_scr[0])
    o_ssd = _dot(_rmsnorm(yz_ref[...], ng_ref[...]).astype(BF16), w_scr[1])
    merged = gm_ref[...] * o_mla + gs_ref[...] * o_ssd
    o_ref[...] = x_ref[...] + g1_ref[...] * _dot(merged.astype(BF16), w_scr[2])


def _merge(attn2d, yz2d, proj, x2d, mod48, mod_row, ssd_norm_g, w_o_mla, w_o_ssd, w_out):
    t = x2d.shape[0]
    tm = TOKEN_TILE
    row = lambda i: (i, 0)
    const = lambda i: (0, 0)
    wspec = pl.BlockSpec((D_MODEL, D_MODEL), const, pipeline_mode=pl.Buffered(1))
    return pl.pallas_call(
        _merge_kernel,
        grid=(t // tm,),
        in_specs=[pl.BlockSpec((tm, D_MODEL), row),
                  pl.BlockSpec((tm, D_MODEL), row),
                  pl.BlockSpec((tm, D_MODEL), lambda i: (i, GM_BLK)),
                  pl.BlockSpec((tm, D_MODEL), lambda i: (i, GS_BLK)),
                  pl.BlockSpec((tm, D_MODEL), row),
                  pl.BlockSpec((None, 1, D_MODEL), lambda i: (mod_row(i * tm // ROW_GROUP) * 6 + 2, 0, 0)),
                  pl.BlockSpec((1, D_MODEL), const),
                  wspec, wspec, wspec],
        out_specs=pl.BlockSpec((tm, D_MODEL), row),
        out_shape=jax.ShapeDtypeStruct((t, D_MODEL), F32),
        scratch_shapes=[pltpu.VMEM((3, D_MODEL, D_MODEL), BF16)],
        compiler_params=_params("arbitrary"),
        name="merge_out",
    )(attn2d, yz2d, proj, proj, x2d, mod48, ssd_norm_g, w_o_mla, w_o_ssd, w_out)


def _ffn_kernel(x_ref, sh_ref, sc_ref, g2_ref, ng_ref, wg_ref, wv_ref, cwg_ref, cwv_ref, cbg_ref, cbv_ref, wd_ref,
                fg_ref, o_ref, h_scr, wup_scr, wd_scr, u_scr, *, seq_len):
    j = pl.program_id(1)
    n_chunks = ROW_GROUP // CONV_CHUNK
    rows = lambda c: slice(c * CONV_CHUNK, (c + 1) * CONV_CHUNK)

    @pl.when(j == 0)
    def _():
        _norm_mod_rows(x_ref, ng_ref, sc_ref, sh_ref, h_scr)
        _zero_conv_pads(u_scr)
        o_ref[...] = jnp.zeros_like(o_ref)

    wup_scr[:, 0:FFN_TILE] = wg_ref[...].astype(BF16)
    wup_scr[:, FFN_TILE:2 * FFN_TILE] = wv_ref[...].astype(BF16)
    wd_scr[...] = wd_ref[...].astype(BF16)

    def gated(c):
        ug = _dwconv3_rows(u_scr, c, slice(0, FFN_TILE), cwg_ref, cbg_ref, seq_len)
        uv = _dwconv3_rows(u_scr, c, slice(FFN_TILE, 2 * FFN_TILE), cwv_ref, cbv_ref, seq_len)
        return (_silu(ug) * uv).astype(BF16)

    act = {}
    for c in range(n_chunks + 2):
        if c < n_chunks:
            u_scr[_stage_rows(c), :] = _dot(h_scr[rows(c), :], wup_scr[...])
        if c >= 2:
            o_ref[rows(c - 2), :] += _dot(act.pop(c - 2), wd_scr[...])
        if 1 <= c <= n_chunks:
            act[c - 1] = gated(c - 1)

    @pl.when(j == pl.num_programs(1) - 1)
    def _():
        def body(i, carry):
            rows = pl.ds(pl.multiple_of(i * 256, 256), 256)
            o_ref[rows, :] = _rmsnorm(x_ref[rows, :] + g2_ref[...] * o_ref[rows, :], fg_ref[...])
            return carry
        lax.fori_loop(0, ROW_GROUP // 256, body, 0)


def _ffn(x2d, mod48, mod_row, norm_g, w_up, conv_w, conv_b, w_down, final_g, seq_len):
    t = x2d.shape[0]
    nj = D_FF // FFN_TILE
    gate = lambda r, j: (0, j)
    val = lambda r, j: (0, nj + j)
    const = lambda r, j: (0, 0)
    mod = lambda k: pl.BlockSpec((None, 1, D_MODEL), lambda r, j: (mod_row(r) * 6 + k, 0, 0))
    return pl.pallas_call(
        functools.partial(_ffn_kernel, seq_len=seq_len),
        grid=(t // ROW_GROUP, nj),
        in_specs=[pl.BlockSpec((ROW_GROUP, D_MODEL), lambda r, j: (r, 0)),
                  mod(3), mod(4), mod(5),
                  pl.BlockSpec((1, D_MODEL), const),
                  pl.BlockSpec((D_MODEL, FFN_TILE), gate),
                  pl.BlockSpec((D_MODEL, FFN_TILE), val),
                  pl.BlockSpec((3, FFN_TILE), gate),
                  pl.BlockSpec((3, FFN_TILE), val),
                  pl.BlockSpec((1, FFN_TILE), gate),
                  pl.BlockSpec((1, FFN_TILE), val),
                  pl.BlockSpec((FFN_TILE, D_MODEL), lambda r, j: (j, 0)),
                  pl.BlockSpec((1, D_MODEL), const)],
        out_specs=pl.BlockSpec((ROW_GROUP, D_MODEL), lambda r, j: (r, 0)),
        out_shape=jax.ShapeDtypeStruct((t, D_MODEL), F32),
        scratch_shapes=[pltpu.VMEM((ROW_GROUP, D_MODEL), BF16),
                        pltpu.VMEM((D_MODEL, 2 * FFN_TILE), BF16),
                        pltpu.VMEM((FFN_TILE, D_MODEL), BF16),
                        pltpu.VMEM((ROW_GROUP + 2 * CONV_PAD, 2 * FFN_TILE), F32)],
        compiler_params=_params("arbitrary", "arbitrary"),
        name="conv_ffn",
    )(x2d, mod48, mod48, mod48, norm_g, w_up, w_up, conv_w, conv_w, conv_b, conv_b, w_down, final_g)


def _rope_tables(seq_len):
    t = np.arange(seq_len)
    row = (t // GRID_W).astype(np.float32)
    col = (t % GRID_W).astype(np.float32)
    n = QK_ROPE // 4
    inv = (np.float32(ROPE_BASE) ** (-np.arange(n, dtype=np.float32) / np.float32(n))).astype(np.float32)
    ar, ac = row[:, None] * inv, col[:, None] * inv
    cos64 = np.concatenate([np.cos(ar), np.cos(ar), np.cos(ac), np.cos(ac)], axis=1)
    sin64 = np.concatenate([-np.sin(ar), np.sin(ar), -np.sin(ac), np.sin(ac)], axis=1)
    zeros = np.zeros_like(cos64)
    return (jnp.asarray(np.concatenate([cos64, zeros], axis=1), F32),
            jnp.asarray(np.concatenate([sin64, zeros], axis=1), F32))


def _swap_rope_halves(w):
    lead = w.shape[:-1]
    return w.reshape(lead + (2, 2, QK_ROPE // 4))[..., ::-1, :].reshape(lead + (QK_ROPE,))


def _trunk_pass(x, mod48, mod_row, wts, ctx, latent):
    b, l, _ = x.shape
    x2d = x.reshape(b * l, D_MODEL)
    proj = _in_proj(x2d, mod48, mod_row, wts["norm_attn_g"], wts["w_in_r"], wts["ssd_conv_w"], wts["ssd_conv_b"], l)
    rope = _rope_tables(l) if latent else None
    w_uq_r = wts["w_uq_lat"] if latent else wts["w_uq_ctx"]
    q, k, v, ckv_n, kr = _mla_prep(proj, wts["q_norm_g"], wts["kv_norm_g"], w_uq_r, wts["w_ukv_r"], rope, l)
    shape3 = lambda a, n: a.reshape(b, n, a.shape[-1])
    segs = [(shape3(k, l), shape3(v, l))]
    h0 = None
    if ctx is not None:
        cache_ckv, cache_krope, h0 = ctx
        past = cache_ckv.shape[1]
        k_c, v_c = _kv_up(cache_ckv.reshape(b * past, KV_LORA), cache_krope.reshape(b * past, QK_ROPE), wts["w_ukv_r"])
        segs = [(shape3(k_c, past), shape3(v_c, past))] + segs
    attn = _attention(shape3(q, l), segs)
    kr3 = shape3(kr, l)
    yz, h_t = _ssd(proj.reshape(b, l, IN_COLS_PADDED), h0, wts["dt_bias128"], wts["a128"], wts["d_exp"])
    x1 = _merge(attn.reshape(b * l, -1), yz.reshape(b * l, -1), proj, x2d, mod48, mod_row, wts["ssd_norm_g"],
                wts["w_o_mla"], wts["w_o_ssd"], wts["w_out"])
    y = _ffn(x1, mod48, mod_row, wts["norm_ffn_g"], wts["w_up"], wts["ffn_conv_w"], wts["ffn_conv_b"], wts["w_down"],
             wts["final_norm_g"], l)
    return y.reshape(b, l, D_MODEL), ckv_n.reshape(b, l, KV_LORA), kr3, h_t


def kernel(x_prompt, x_sample, c, cache_ckv, cache_krope, state_ssd, c_ctx, w_ada, b_ada, norm_attn_g, w_in, q_norm_g,
           kv_norm_g, w_uq, w_ukv, w_o_mla, ssd_conv_w, ssd_conv_b, ssd_dt_bias, ssd_A_log, ssd_D, ssd_norm_g, w_o_ssd,
           w_out, norm_ffn_g, w_up, ffn_conv_w, ffn_conv_b, w_down, final_norm_g):
    depth = w_in.shape[0]
    assert depth == 1, "single trunk layer"
    dec_b = x_sample.shape[0]
    assert x_sample.shape[1] == ROW_GROUP and ROW_GROUP % x_prompt.shape[1] == 0
    lyr = 0

    cvec = jnp.zeros((8, D_MODEL), F32).at[0].set(c_ctx).at[1:1 + dec_b].set(c)
    mod48 = _ada(cvec, w_ada[lyr], b_ada[lyr]).reshape(8 * 6, 1, D_MODEL)

    wi = w_in[lyr]
    o = np.cumsum((0, Q_LORA, KV_LORA, QK_ROPE, SSD_INNER, SSD_INNER, SSD_GROUPS * SSD_STATE,
                   SSD_GROUPS * SSD_STATE, 2 * SSD_HEADS, D_MODEL, D_MODEL))
    piece = lambda i: wi[:, o[i]:o[i + 1]]
    w_kr = piece(2)
    zcols = lambda n: jnp.zeros((D_MODEL, n), F32)
    misc_pad = IN_TILE - 256 - 2 * SSD_HEADS
    w_in_r = jnp.concatenate([piece(3), piece(4), piece(8), piece(9), piece(5), piece(6), piece(0), piece(1),
                              w_kr, zcols(QK_ROPE), _swap_rope_halves(w_kr), zcols(QK_ROPE), piece(7),
                              zcols(misc_pad)], axis=1).astype(BF16)
    wq = w_uq[lyr].reshape(Q_LORA, N_HEADS, QK_NOPE + QK_ROPE)
    wq_nope = wq[:, :, :QK_NOPE].reshape(Q_LORA, -1)
    wq_rope = wq[:, :, QK_NOPE:]
    pad_rope = lambda w: jnp.pad(w, ((0, 0), (0, 0), (0, 128 - QK_ROPE))).reshape(Q_LORA, -1)
    w_uq_ctx = jnp.concatenate([wq_nope, pad_rope(wq_rope)], axis=1).astype(BF16)
    w_uq_lat = jnp.concatenate([wq_nope, pad_rope(wq_rope), pad_rope(_swap_rope_halves(wq_rope))], axis=1).astype(BF16)
    wkv = w_ukv[lyr].reshape(KV_LORA, N_HEADS, QK_NOPE + V_HEAD)
    w_ukv_r = jnp.concatenate([wkv[:, :, :QK_NOPE].reshape(KV_LORA, -1),
                               wkv[:, :, QK_NOPE:].reshape(KV_LORA, -1)], axis=1).astype(BF16)
    pad128 = lambda a: jnp.pad(a.reshape(1, -1), ((0, 0), (0, 128 - a.size)))
    wts = {
        "norm_attn_g": norm_attn_g[lyr].reshape(1, -1), "w_in_r": w_in_r,
        "ssd_conv_w": ssd_conv_w[lyr], "ssd_conv_b": ssd_conv_b[lyr].reshape(1, -1),
        "q_norm_g": q_norm_g[lyr].reshape(1, -1), "kv_norm_g": kv_norm_g[lyr].reshape(1, -1),
        "w_uq_ctx": w_uq_ctx, "w_uq_lat": w_uq_lat, "w_ukv_r": w_ukv_r,
        "dt_bias128": pad128(ssd_dt_bias[lyr]), "a128": pad128(-jnp.exp(ssd_A_log[lyr])),
        "d_exp": jnp.repeat(ssd_D[lyr], SSD_HEADDIM).reshape(1, -1),
        "ssd_norm_g": ssd_norm_g[lyr].reshape(1, -1),
        "w_o_mla": w_o_mla[lyr], "w_o_ssd": w_o_ssd[lyr], "w_out": w_out[lyr],
        "norm_ffn_g": norm_ffn_g[lyr].reshape(1, -1), "w_up": w_up[lyr],
        "ffn_conv_w": ffn_conv_w[lyr], "ffn_conv_b": ffn_conv_b[lyr].reshape(1, -1),
        "w_down": w_down[lyr], "final_norm_g": final_norm_g.reshape(1, -1),
    }

    y_p, ckv_p, kr_p, st_p = _trunk_pass(x_prompt, mod48, lambda r: 0, wts, None, False)
    ctx = (cache_ckv[:, lyr], cache_krope[:, lyr], state_ssd[:, lyr])
    y_s, _, _, _ = _trunk_pass(x_sample, mod48, lambda r: 1 + r, wts, ctx, True)
    return y_p, y_s, ckv_p[:, None], kr_p[:, None], st_p[:, None]
```

```python
import functools
import math

import jax
import jax.numpy as jnp
import numpy as np
from jax import lax
from jax.experimental import pallas as pl
from jax.experimental.pallas import tpu as pltpu

F32 = jnp.float32
BF16 = jnp.bfloat16

D_MODEL = 1024
GRID_W = 64
N_HEADS = 8
QK_NOPE = 128
QK_ROPE = 64
V_HEAD = 128
Q_LORA = 256
KV_LORA = 256
ROPE_BASE = 10000.0
SSD_HEADS = 16
SSD_HEADDIM = 64
SSD_INNER = SSD_HEADS * SSD_HEADDIM
SSD_GROUPS = 4
SSD_STATE = 64
SSD_CHUNK = 128
D_FF = 2816
EPS = 1e-6

ROW_GROUP = 2048
IN_TILE = 512
SSD_CHUNKS_PER_STEP = 2
FFN_TILE = 256
TOKEN_TILE = 512
ATTN_Q_TILE = 256
CONV_CHUNK = 256
CONV_PAD = 8
VMEM_LIMIT = 56 * 1024 * 1024
NEG_BIG = -1e30


def _sigmoid(x):
    return 1.0 / (1.0 + jnp.exp(-x))


def _silu(x):
    return x * _sigmoid(x)


def _softplus(x):
    e = jnp.exp(-jnp.abs(x))
    u = 1.0 + e
    log1p_e = jnp.where(u == 1.0, e, e * jnp.log(u) / jnp.where(u == 1.0, 1.0, u - 1.0))
    return jnp.maximum(x, 0.0) + log1p_e


def _rmsnorm(x, g):
    return x * lax.rsqrt(jnp.mean(x * x, axis=-1, keepdims=True) + EPS) * g


def _dot(a, b):
    return jnp.dot(a, b, preferred_element_type=F32)


def _dot_nt(a, b):
    return lax.dot_general(a, b, (((1,), (1,)), ((), ())), preferred_element_type=F32)


def _params(*sem):
    return pltpu.CompilerParams(dimension_semantics=sem, vmem_limit_bytes=VMEM_LIMIT)


def _norm_mod_rows(x_ref, g_ref, sc_ref, sh_ref, h_scr):
    def body(i, carry):
        rows = pl.ds(pl.multiple_of(i * CONV_CHUNK, CONV_CHUNK), CONV_CHUNK)
        h = _rmsnorm(x_ref[rows, :], g_ref[...]) * (1.0 + sc_ref[...]) + sh_ref[...]
        h_scr[rows, :] = h.astype(BF16)
        return carry
    lax.fori_loop(0, ROW_GROUP // CONV_CHUNK, body, 0)


def _zero_conv_pads(u_scr):
    zeros = jnp.zeros((CONV_PAD, u_scr.shape[1]), F32)
    u_scr[0:CONV_PAD, :] = zeros
    u_scr[CONV_PAD + ROW_GROUP:2 * CONV_PAD + ROW_GROUP, :] = zeros


def _stage_rows(c):
    return slice(CONV_PAD + c * CONV_CHUNK, CONV_PAD + (c + 1) * CONV_CHUNK)


def _dwconv3_rows(u_scr, c, cols, w_ref, b_ref, seq_len):
    r0 = c * CONV_CHUNK
    base = CONV_PAD + r0
    width = cols.stop - cols.start
    prev = u_scr[base - 1:base - 1 + CONV_CHUNK, cols]
    cur = u_scr[base:base + CONV_CHUNK, cols]
    nxt = u_scr[base + 1:base + 1 + CONV_CHUNK, cols]
    row = lax.broadcasted_iota(jnp.int32, (CONV_CHUNK, width), 0)
    if r0 % seq_len == 0:
        prev = jnp.where(row == 0, 0.0, prev)
    if (r0 + CONV_CHUNK) % seq_len == 0:
        nxt = jnp.where(row == CONV_CHUNK - 1, 0.0, nxt)
    return prev * w_ref[0:1, :] + cur * w_ref[1:2, :] + nxt * w_ref[2:3, :] + b_ref[...]


def _ada_kernel(c_ref, w_ref, b_ref, o_ref):
    a = _silu(c_ref[...]).astype(BF16)
    o_ref[...] = _dot(a, w_ref[...].astype(BF16)) + b_ref[...]


def _ada(cvec, w_ada, b_ada):
    tn = 1536
    return pl.pallas_call(
        _ada_kernel,
        grid=(6 * D_MODEL // tn,),
        in_specs=[pl.BlockSpec((8, D_MODEL), lambda j: (0, 0)),
                  pl.BlockSpec((D_MODEL, tn), lambda j: (0, j)),
                  pl.BlockSpec((1, tn), lambda j: (0, j))],
        out_specs=pl.BlockSpec((8, tn), lambda j: (0, j)),
        out_shape=jax.ShapeDtypeStruct((8, 6 * D_MODEL), F32),
        compiler_params=_params("arbitrary"),
        name="ada_mod",
    )(cvec, w_ada, b_ada.reshape(1, -1))


N_LOW_TILES, N_F32_TILES = 7, 4
Z_BLK, GM_BLK, GS_BLK = 0, 1, 2
BC_TILE = 6
XS_BLK = 0
MLA_TILE, MISC_TILE = 2, 3
DT_BLK = (MISC_TILE * IN_TILE + 256) // 128
ATTN_HEAD_COLS = 256


def _in_kernel(x_ref, sh_ref, sc_ref, g_ref, w_ref, cw_ref, cb_ref, lo_ref, hi_ref, h_scr, u_scr, *, seq_len):
    j = pl.program_id(1)
    n_chunks = ROW_GROUP // CONV_CHUNK
    rows = lambda c: slice(c * CONV_CHUNK, (c + 1) * CONV_CHUNK)

    @pl.when(j == 0)
    def _():
        _norm_mod_rows(x_ref, g_ref, sc_ref, sh_ref, h_scr)
        _zero_conv_pads(u_scr)

    def pointwise(fn, o_ref):
        for c in range(n_chunks):
            o_ref[rows(c), :] = fn(_dot(h_scr[rows(c), :], w_ref[...])).astype(o_ref.dtype)

    def conv(o_ref):
        def conv_out(c):
            v = _silu(_dwconv3_rows(u_scr, c, slice(0, IN_TILE), cw_ref, cb_ref, seq_len))
            o_ref[rows(c), :] = v.astype(o_ref.dtype)
        for c in range(n_chunks):
            u_scr[_stage_rows(c), :] = _dot(h_scr[rows(c), :], w_ref[...])
            if c >= 1:
                conv_out(c - 1)
        conv_out(n_chunks - 1)

    @pl.when(j <= 1)
    def _():
        pointwise(_silu, lo_ref)

    @pl.when((j >= 2) & (j <= 5))
    def _():
        pointwise(_sigmoid, lo_ref)

    @pl.when(j == BC_TILE)
    def _():
        conv(lo_ref)

    @pl.when((j == N_LOW_TILES) | (j == N_LOW_TILES + 1))
    def _():
        conv(hi_ref)

    @pl.when(j >= N_LOW_TILES + MLA_TILE)
    def _():
        pointwise(lambda u: u, hi_ref)


def _in_proj(x2d, mod48, mod_row, norm_g, w_in_r, conv_w, conv_b, seq_len):
    t = x2d.shape[0]
    n_tiles = N_LOW_TILES + N_F32_TILES
    conv_idx = lambda r, j: (0, jnp.where(j == BC_TILE, 2, jnp.clip(j - N_LOW_TILES, 0, 1)))
    return pl.pallas_call(
        functools.partial(_in_kernel, seq_len=seq_len),
        grid=(t // ROW_GROUP, n_tiles),
        in_specs=[pl.BlockSpec((ROW_GROUP, D_MODEL), lambda r, j: (r, 0)),
                  pl.BlockSpec((None, 1, D_MODEL), lambda r, j: (mod_row(r) * 6 + 0, 0, 0)),
                  pl.BlockSpec((None, 1, D_MODEL), lambda r, j: (mod_row(r) * 6 + 1, 0, 0)),
                  pl.BlockSpec((1, D_MODEL), lambda r, j: (0, 0)),
                  pl.BlockSpec((D_MODEL, IN_TILE), lambda r, j: (0, j)),
                  pl.BlockSpec((3, IN_TILE), conv_idx),
                  pl.BlockSpec((1, IN_TILE), conv_idx)],
        out_specs=[pl.BlockSpec((ROW_GROUP, IN_TILE), lambda r, j: (r, jnp.minimum(j, N_LOW_TILES - 1))),
                   pl.BlockSpec((ROW_GROUP, IN_TILE), lambda r, j: (r, jnp.maximum(j - N_LOW_TILES, 0)))],
        out_shape=[jax.ShapeDtypeStruct((t, N_LOW_TILES * IN_TILE), BF16),
                   jax.ShapeDtypeStruct((t, N_F32_TILES * IN_TILE), F32)],
        scratch_shapes=[pltpu.VMEM((ROW_GROUP, D_MODEL), BF16),
                        pltpu.VMEM((ROW_GROUP + 2 * CONV_PAD, IN_TILE), F32)],
        compiler_params=_params("arbitrary", "arbitrary"),
        name="in_proj",
    )(x2d, mod48, mod48, norm_g, w_in_r, conv_w, conv_b)


def _mla_prep_kernel(*refs, latent):
    if latent:
        (p0_ref, p10_ref, qg_ref, kvg_ref, wuq_ref, wukv_ref, cos_ref, sin_ref,
         q_ref, k_ref, v_ref, ckv_ref, kr_ref) = refs
    else:
        (p0_ref, p10_ref, qg_ref, kvg_ref, wuq_ref, wukv_ref,
         q_ref, k_ref, v_ref, ckv_ref, kr_ref) = refs
    scale = 1.0 / math.sqrt(QK_NOPE + QK_ROPE)
    n_nope = N_HEADS * QK_NOPE
    p0 = p0_ref[...]
    cqn = _rmsnorm(p0[:, :Q_LORA], qg_ref[...]).astype(BF16)
    q = _dot(cqn, wuq_ref[...])
    q_rope = q[:, n_nope:2 * n_nope]
    kr = p10_ref[:, 0:128]
    if latent:
        cos, sin = cos_ref[...], sin_ref[...]
        q_rope = (q_rope * jnp.concatenate([cos] * N_HEADS, axis=1)
                  + q[:, 2 * n_nope:3 * n_nope] * jnp.concatenate([sin] * N_HEADS, axis=1))
        kr = kr * cos + p10_ref[:, 128:256] * sin
    kr_ref[...] = kr[:, :QK_ROPE]
    ckv_n = _rmsnorm(p0[:, Q_LORA:], kvg_ref[...])
    ckv_ref[...] = ckv_n
    kv = _dot(ckv_n.astype(BF16), wukv_ref[...])
    v_ref[...] = kv[:, n_nope:].astype(BF16)
    kr_bf = kr.astype(BF16)
    for h in range(N_HEADS):
        lo = slice(h * ATTN_HEAD_COLS, h * ATTN_HEAD_COLS + QK_NOPE)
        hi = slice(h * ATTN_HEAD_COLS + QK_NOPE, (h + 1) * ATTN_HEAD_COLS)
        head = slice(h * QK_NOPE, (h + 1) * QK_NOPE)
        q_ref[:, lo] = (q[:, head] * scale).astype(BF16)
        q_ref[:, hi] = (q_rope[:, head] * scale).astype(BF16)
        k_ref[:, lo] = kv[:, head].astype(BF16)
        k_ref[:, hi] = kr_bf


def _mla_prep(proj, q_norm_g, kv_norm_g, w_uq_r, w_ukv_r, rope_tables, seq_len):
    t = proj.shape[0]
    tm = TOKEN_TILE
    latent = rope_tables is not None
    in_specs = [pl.BlockSpec((tm, IN_TILE), lambda i: (i, MLA_TILE)),
                pl.BlockSpec((tm, IN_TILE), lambda i: (i, MISC_TILE)),
                pl.BlockSpec((1, Q_LORA), lambda i: (0, 0)),
                pl.BlockSpec((1, KV_LORA), lambda i: (0, 0)),
                pl.BlockSpec(w_uq_r.shape, lambda i: (0, 0)),
                pl.BlockSpec(w_ukv_r.shape, lambda i: (0, 0))]
    args = [proj, proj, q_norm_g, kv_norm_g, w_uq_r, w_ukv_r]
    if latent:
        per_seq = seq_len // tm
        in_specs += [pl.BlockSpec((tm, 128), lambda i: (i % per_seq, 0))] * 2
        args += list(rope_tables)
    widths = (N_HEADS * ATTN_HEAD_COLS, N_HEADS * ATTN_HEAD_COLS, N_HEADS * V_HEAD, KV_LORA, QK_ROPE)
    dtypes = (BF16, BF16, BF16, F32, F32)
    return pl.pallas_call(
        functools.partial(_mla_prep_kernel, latent=latent),
        grid=(t // tm,),
        in_specs=in_specs,
        out_specs=[pl.BlockSpec((tm, w), lambda i: (i, 0)) for w in widths],
        out_shape=[jax.ShapeDtypeStruct((t, w), d) for w, d in zip(widths, dtypes)],
        compiler_params=_params("arbitrary"),
        name="mla_prep",
    )(*args)


def _kv_up_kernel(c_ref, kr_ref, w_ref, k_ref, v_ref):
    kv = _dot(c_ref[...].astype(BF16), w_ref[...])
    v_ref[...] = kv[:, N_HEADS * QK_NOPE:].astype(BF16)
    kr_bf = kr_ref[...].astype(BF16)
    for h in range(N_HEADS):
        base = h * ATTN_HEAD_COLS
        k_ref[:, base:base + QK_NOPE] = kv[:, h * QK_NOPE:(h + 1) * QK_NOPE].astype(BF16)
        k_ref[:, base + QK_NOPE:base + ATTN_HEAD_COLS] = jnp.zeros((c_ref.shape[0], ATTN_HEAD_COLS - QK_NOPE), BF16)
        k_ref[:, base + QK_NOPE:base + QK_NOPE + QK_ROPE] = kr_bf


def _kv_up(ckv2d, kr2d, w_ukv_r):
    t = ckv2d.shape[0]
    tm = TOKEN_TILE
    widths = (N_HEADS * ATTN_HEAD_COLS, N_HEADS * V_HEAD)
    return pl.pallas_call(
        _kv_up_kernel,
        grid=(t // tm,),
        in_specs=[pl.BlockSpec((tm, KV_LORA), lambda i: (i, 0)),
                  pl.BlockSpec((tm, QK_ROPE), lambda i: (i, 0)),
                  pl.BlockSpec(w_ukv_r.shape, lambda i: (0, 0))],
        out_specs=[pl.BlockSpec((tm, w), lambda i: (i, 0)) for w in widths],
        out_shape=[jax.ShapeDtypeStruct((t, w), BF16) for w in widths],
        compiler_params=_params("arbitrary"),
        name="kv_up_ctx",
    )(ckv2d, kr2d, w_ukv_r)


def _attn_kernel(*refs):
    q_ref, o_ref = refs[0], refs[-1]
    segs = [(refs[i], refs[i + 1]) for i in range(1, len(refs) - 1, 2)]
    for h in range(N_HEADS):
        qk = slice(h * ATTN_HEAD_COLS, (h + 1) * ATTN_HEAD_COLS)
        vc = slice(h * V_HEAD, (h + 1) * V_HEAD)
        q = q_ref[0, :, qk]
        s = [_dot_nt(q, k_ref[0, :, qk]) for k_ref, _ in segs]
        m = functools.reduce(jnp.maximum, [jnp.max(si, axis=-1, keepdims=True) for si in s])
        p = [jnp.exp(si - m) for si in s]
        l = functools.reduce(jnp.add, [jnp.sum(pi, axis=-1, keepdims=True) for pi in p])
        o = functools.reduce(jnp.add, [_dot(pi.astype(BF16), v_ref[0, :, vc]) for pi, (_, v_ref) in zip(p, segs)])
        o_ref[0, :, vc] = (o / l).astype(BF16)


def _attention(q, segs):
    b, lq, _ = q.shape
    tq = ATTN_Q_TILE
    in_specs = [pl.BlockSpec((1, tq, q.shape[2]), lambda i, t: (i, t, 0))]
    args = [q]
    for k, v in segs:
        in_specs += [pl.BlockSpec((1,) + k.shape[1:], lambda i, t: (i, 0, 0)),
                     pl.BlockSpec((1,) + v.shape[1:], lambda i, t: (i, 0, 0))]
        args += [k, v]
    return pl.pallas_call(
        _attn_kernel,
        grid=(b, lq // tq),
        in_specs=in_specs,
        out_specs=pl.BlockSpec((1, tq, N_HEADS * V_HEAD), lambda i, t: (i, t, 0)),
        out_shape=jax.ShapeDtypeStruct((b, lq, N_HEADS * V_HEAD), BF16),
        compiler_params=_params("arbitrary", "arbitrary"),
        name="mla_attention",
    )(*args)


def _split3(x):
    hi = x.astype(BF16)
    r = x - hi.astype(F32)
    mid = r.astype(BF16)
    lo = (r - mid.astype(F32)).astype(BF16)
    return hi, mid, lo


def _exact_dot(parts, sel):
    return functools.reduce(jnp.add, [_dot(p, sel) for p in parts])


def _exact_dot_rows(sel, parts):
    return functools.reduce(jnp.add, [_dot(sel, p) for p in parts])


def _exact_dot_nt(sel, parts):
    return functools.reduce(jnp.add, [_dot_nt(sel, p) for p in parts])


HEADS_PER_GROUP = SSD_HEADS // SSD_GROUPS
GROUP_COLS = HEADS_PER_GROUP * SSD_HEADDIM


def _ssd_kernel(*refs, nc, cps, has_h0):
    if has_h0:
        (xs_ref, zs_ref, bc_ref, dt_ref, dtb_ref, a_ref, dx_ref, h0_ref, y_ref, ht_ref,
         yl_scr, acum_scr, acumt_scr, dtt_scr, tot_scr, sb_scr, h_scr, esel_scr) = refs
    else:
        (xs_ref, zs_ref, bc_ref, dt_ref, dtb_ref, a_ref, dx_ref, y_ref, ht_ref,
         yl_scr, acum_scr, acumt_scr, dtt_scr, tot_scr, sb_scr, h_scr, esel_scr) = refs
    s = pl.program_id(1)
    q = SSD_CHUNK
    n_bc = SSD_GROUPS * SSD_STATE
    lane = lax.broadcasted_iota(jnp.int32, (q, 128), 1)
    low_half = lane < SSD_HEADDIM
    ii = lax.broadcasted_iota(jnp.int32, (q, q), 0)
    jj = lax.broadcasted_iota(jnp.int32, (q, q), 1)
    lower, upper = ii >= jj, ii <= jj

    def lane_bcast(parts, d):
        return _exact_dot(parts, esel_scr[d])

    def stacked_states(d):
        return [jnp.concatenate([h_scr[d, 2 * i], h_scr[d, 2 * i + 1]], axis=0).astype(BF16)
                for i in range(SSD_GROUPS // 2)]

    def group_c(bc, g):
        i, r = divmod(g, 2)
        cpair = bc[:, n_bc + i * 128:n_bc + (i + 1) * 128]
        return jnp.where(low_half if r == 0 else ~low_half, cpair, 0.0).astype(BF16)

    @pl.when(s == 0)
    def _():
        k = lax.broadcasted_iota(jnp.int32, (128, SSD_INNER), 0)
        head = lax.broadcasted_iota(jnp.int32, (128, SSD_INNER), 1) // SSD_HEADDIM
        for d in range(2):
            esel_scr[d] = jnp.where(k == d * SSD_HEADS + head, 1.0, 0.0).astype(BF16)
        if has_h0:
            for d in range(2):
                for g in range(SSD_GROUPS):
                    hpn = h0_ref[0, d, g * HEADS_PER_GROUP:(g + 1) * HEADS_PER_GROUP].reshape(GROUP_COLS, SSD_STATE)
                    h_scr[d, g] = hpn.T
        else:
            h_scr[...] = jnp.zeros(h_scr.shape, F32)
        tri_f = jnp.where(lower, 1.0, 0.0).astype(BF16)
        tri_b = jnp.where(upper, 1.0, 0.0).astype(BF16)
        fwd_col = lane < SSD_HEADS
        for c in range(nc):
            crow = slice(c * q, (c + 1) * q)
            dt = _softplus(dt_ref[0, crow, :] + dtb_ref[...])
            parts = _split3(dt * a_ref[...])
            acum = jnp.where(fwd_col, _exact_dot_rows(tri_f, parts),
                             _exact_dot_rows(tri_b, parts))
            acum_scr[crow, :] = acum
            tot = jnp.where(fwd_col[0:1], acum[q - 1:q, :], acum[0:1, :])
            tot_scr[c] = jnp.broadcast_to(tot, (8, 128))
            acumt_scr[c] = acum.T
            dtt_scr[c] = dt.T

    def first_sweep(c, blk):
        rows = pl.ds(pl.multiple_of(c * q, q), q)
        x = xs_ref[0, blk, :]
        bc = bc_ref[0, blk, :].astype(F32)
        acum = acum_scr[rows, :]
        acum_t, dt_t = acumt_scr[c], dtt_scr[c]
        tot8 = tot_scr[c]
        e = jnp.exp(acum)
        e_hi = e.astype(BF16)
        eb_f = lane_bcast([e_hi, (e - e_hi.astype(F32)).astype(BF16)], 0)
        cd_f = jnp.exp(lane_bcast(_split3(tot8), 0))[0:1]
        b_t = [bc[:, i * 128:(i + 1) * 128].T for i in range(SSD_GROUPS // 2)]
        h_in = stacked_states(0)
        for g in range(SSD_GROUPS):
            i, r = divmod(g, 2)
            cm = group_c(bc, g)
            scores = _dot_nt(cm, bc[:, i * 128:(i + 1) * 128].astype(BF16))
            bg_t = b_t[i][r * SSD_STATE:(r + 1) * SSD_STATE, :]
            gcols = slice(g * GROUP_COLS, (g + 1) * GROUP_COLS)
            y_off = _dot(cm, h_in[i]) * eb_f[:, gcols]
            for t in range(HEADS_PER_GROUP // 2):
                pc = slice((2 * g + t) * 128, (2 * g + t + 1) * 128)
                xp = x[:, pc]
                x_half = [jnp.where(low_half, xp, 0.0).astype(BF16), jnp.where(low_half, 0.0, xp).astype(BF16)]
                m, sf, sb = [], [], []
                for u in range(2):
                    kf = g * HEADS_PER_GROUP + 2 * t + u
                    kb = SSD_HEADS + kf
                    af_col = jnp.broadcast_to(acum[:, kf:kf + 1], (q, q))
                    ab_col = jnp.broadcast_to(acum[:, kb:kb + 1], (q, q))
                    af_row, ab_row = acum_t[kf:kf + 1, :], acum_t[kb:kb + 1, :]
                    dtf_row, dtb_row = dt_t[kf:kf + 1, :], dt_t[kb:kb + 1, :]
                    decay = (jnp.exp(jnp.where(lower, af_col - af_row, NEG_BIG)) * dtf_row
                             + jnp.exp(jnp.where(upper, ab_col - ab_row, NEG_BIG)) * dtb_row)
                    m.append((scores * decay).astype(BF16))
                    wf = jnp.exp(af_row[:, q - 1:q] - af_row) * dtf_row
                    wb = jnp.exp(ab_row[:, 0:1] - ab_row) * dtb_row
                    sf.append((bg_t * wf).astype(BF16))
                    sb.append((bg_t * wb).astype(BF16))
                y_pair = _dot(m[0], x_half[0]) + _dot(m[1], x_half[1])
                tc = slice(t * 128, (t + 1) * 128)
                yl_scr[rows, pc] = y_pair + y_off[:, tc] + dx_ref[:, pc] * xp
                h_scr[0, g, :, tc] = (h_scr[0, g, :, tc] * cd_f[:, pc]
                                      + _dot(sf[0], x_half[0]) + _dot(sf[1], x_half[1]))
                sb_scr[c, g, :, tc] = _dot(sb[0], x_half[0]) + _dot(sb[1], x_half[1])

    def second_sweep(c, blk):
        rows = pl.ds(pl.multiple_of(c * q, q), q)
        bc = bc_ref[0, blk, :].astype(F32)
        e = jnp.exp(acum_scr[rows, :])
        e_hi = e.astype(BF16)
        eb_b = lane_bcast([e_hi, (e - e_hi.astype(F32)).astype(BF16)], 1)
        cd_b = jnp.exp(lane_bcast(_split3(tot_scr[c]), 1))[0:1]
        h_in = stacked_states(1)
        for g in range(SSD_GROUPS):
            gcols = slice(g * GROUP_COLS, (g + 1) * GROUP_COLS)
            y_off = _dot(group_c(bc, g), h_in[g // 2]) * eb_b[:, gcols]
            y_ref[0, blk, gcols] = ((yl_scr[rows, gcols] + y_off) * zs_ref[0, blk, gcols].astype(F32)).astype(y_ref.dtype)
            h_scr[1, g] = h_scr[1, g] * cd_b[:, gcols] + sb_scr[c, g]

    n_steps = nc // cps

    @pl.when(s < n_steps)
    def _():
        for ci in range(cps):
            first_sweep(s * cps + ci, slice(ci * q, (ci + 1) * q))

    @pl.when(s >= n_steps)
    def _():
        for ci in reversed(range(cps)):
            second_sweep((2 * n_steps - 1 - s) * cps + ci, slice(ci * q, (ci + 1) * q))

    @pl.when(s == 2 * n_steps - 1)
    def _():
        for d in range(2):
            for g in range(SSD_GROUPS):
                ht_ref[0, d, g * HEADS_PER_GROUP:(g + 1) * HEADS_PER_GROUP] = h_scr[d, g].T.reshape(
                    HEADS_PER_GROUP, SSD_HEADDIM, SSD_STATE)


def _ssd(proj_lo, proj_hi, h0, dt_bias128, a128, d_exp):
    b, l, _ = proj_hi.shape
    q = SSD_CHUNK
    nc = l // q
    cps = SSD_CHUNKS_PER_STEP
    n_steps = nc // cps
    rows = cps * q
    early = lambda s: jnp.minimum(s, n_steps - 1)
    both = lambda s: jnp.where(s < n_steps, s, 2 * n_steps - 1 - s)
    late = lambda s: jnp.where(s < n_steps, n_steps - 1, 2 * n_steps - 1 - s)
    st_shape = (1, 2, SSD_HEADS, SSD_HEADDIM, SSD_STATE)
    st_spec = pl.BlockSpec(st_shape, lambda i, s: (i, 0, 0, 0, 0))
    has_h0 = h0 is not None
    return pl.pallas_call(
        functools.partial(_ssd_kernel, nc=nc, cps=cps, has_h0=has_h0),
        grid=(b, 2 * n_steps),
        in_specs=[pl.BlockSpec((1, rows, SSD_INNER), lambda i, s: (i, early(s), XS_BLK)),
                  pl.BlockSpec((1, rows, SSD_INNER), lambda i, s: (i, late(s), Z_BLK)),
                  pl.BlockSpec((1, rows, IN_TILE), lambda i, s: (i, both(s), BC_TILE)),
                  pl.BlockSpec((1, l, 128), lambda i, s: (i, 0, DT_BLK)),
                  pl.BlockSpec((1, 128), lambda i, s: (0, 0)),
                  pl.BlockSpec((1, 128), lambda i, s: (0, 0)),
                  pl.BlockSpec((1, SSD_INNER), lambda i, s: (0, 0))] + ([st_spec] if has_h0 else []),
        out_specs=[pl.BlockSpec((1, rows, SSD_INNER), lambda i, s: (i, late(s), 0)),
                   pl.BlockSpec(st_shape, lambda i, s: (i, 0, 0, 0, 0))],
        out_shape=[jax.ShapeDtypeStruct((b, l, SSD_INNER), BF16),
                   jax.ShapeDtypeStruct((b,) + st_shape[1:], F32)],
        scratch_shapes=[pltpu.VMEM((l, SSD_INNER), F32),
                        pltpu.VMEM((l, 128), F32),
                        pltpu.VMEM((nc, 128, q), F32),
                        pltpu.VMEM((nc, 128, q), F32),
                        pltpu.VMEM((nc, 8, 128), F32),
                        pltpu.VMEM((nc, SSD_GROUPS, SSD_STATE, GROUP_COLS), F32),
                        pltpu.VMEM((2, SSD_GROUPS, SSD_STATE, GROUP_COLS), F32),
                        pltpu.VMEM((2, 128, SSD_INNER), BF16)],
        compiler_params=_params("arbitrary", "arbitrary"),
        name="ssd_scan",
    )(proj_hi, proj_lo, proj_lo, proj_hi, dt_bias128, a128, d_exp, *([h0] if has_h0 else []))


def _merge_kernel(attn_ref, yz_ref, gm_ref, gs_ref, x_ref, g1_ref, ng_ref, womla_ref, wossd_ref, wout_ref, o_ref, w_scr):
    @pl.when(pl.program_id(0) == 0)
    def _():
        w_scr[0] = womla_ref[...].astype(BF16)
        w_scr[1] = wossd_ref[...].astype(BF16)
        w_scr[2] = wout_ref[...].astype(BF16)

    o_mla = _dot(attn_ref[...], w_scr[0])
    o_ssd = _dot(_rmsnorm(yz_ref[...].astype(F32), ng_ref[...]).astype(BF16), w_scr[1])
    merged = gm_ref[...].astype(F32) * o_mla + gs_ref[...].astype(F32) * o_ssd
    o_ref[...] = x_ref[...] + g1_ref[...] * _dot(merged.astype(BF16), w_scr[2])


def _merge(attn2d, yz2d, proj, x2d, mod48, mod_row, ssd_norm_g, w_o_mla, w_o_ssd, w_out):
    t = x2d.shape[0]
    tm = TOKEN_TILE
    row = lambda i: (i, 0)
    const = lambda i: (0, 0)
    wspec = pl.BlockSpec((D_MODEL, D_MODEL), const, pipeline_mode=pl.Buffered(1))
    return pl.pallas_call(
        _merge_kernel,
        grid=(t // tm,),
        in_specs=[pl.BlockSpec((tm, D_MODEL), row),
                  pl.BlockSpec((tm, D_MODEL), row),
                  pl.BlockSpec((tm, D_MODEL), lambda i: (i, GM_BLK)),
                  pl.BlockSpec((tm, D_MODEL), lambda i: (i, GS_BLK)),
                  pl.BlockSpec((tm, D_MODEL), row),
                  pl.BlockSpec((None, 1, D_MODEL), lambda i: (mod_row(i * tm // ROW_GROUP) * 6 + 2, 0, 0)),
                  pl.BlockSpec((1, D_MODEL), const),
                  wspec, wspec, wspec],
        out_specs=pl.BlockSpec((tm, D_MODEL), row),
        out_shape=jax.ShapeDtypeStruct((t, D_MODEL), F32),
        scratch_shapes=[pltpu.VMEM((3, D_MODEL, D_MODEL), BF16)],
        compiler_params=_params("arbitrary"),
        name="merge_out",
    )(attn2d, yz2d, proj, proj, x2d, mod48, ssd_norm_g, w_o_mla, w_o_ssd, w_out)


def _ffn_kernel(x_ref, sh_ref, sc_ref, g2_ref, ng_ref, wg_ref, wv_ref, cwg_ref, cwv_ref, cbg_ref, cbv_ref, wd_ref,
                fg_ref, o_ref, h_scr, u_scr, *, seq_len):
    j = pl.program_id(1)
    n_chunks = ROW_GROUP // CONV_CHUNK
    rows = lambda c: slice(c * CONV_CHUNK, (c + 1) * CONV_CHUNK)

    @pl.when(j == 0)
    def _():
        _norm_mod_rows(x_ref, ng_ref, sc_ref, sh_ref, h_scr)
        _zero_conv_pads(u_scr)
        o_ref[...] = jnp.zeros_like(o_ref)

    def gated(c):
        ug = _dwconv3_rows(u_scr, c, slice(0, FFN_TILE), cwg_ref, cbg_ref, seq_len)
        uv = _dwconv3_rows(u_scr, c, slice(FFN_TILE, 2 * FFN_TILE), cwv_ref, cbv_ref, seq_len)
        return (_silu(ug) * uv).astype(BF16)

    act = {}
    for c in range(n_chunks + 2):
        if c < n_chunks:
            u_scr[_stage_rows(c), 0:FFN_TILE] = _dot(h_scr[rows(c), :], wg_ref[...])
            u_scr[_stage_rows(c), FFN_TILE:2 * FFN_TILE] = _dot(h_scr[rows(c), :], wv_ref[...])
        if c >= 2:
            o_ref[rows(c - 2), :] += _dot(act.pop(c - 2), wd_ref[...])
        if 1 <= c <= n_chunks:
            act[c - 1] = gated(c - 1)

    @pl.when(j == pl.num_programs(1) - 1)
    def _():
        def body(i, carry):
            rows = pl.ds(pl.multiple_of(i * 256, 256), 256)
            o_ref[rows, :] = _rmsnorm(x_ref[rows, :] + g2_ref[...] * o_ref[rows, :], fg_ref[...])
            return carry
        lax.fori_loop(0, ROW_GROUP // 256, body, 0)


def _ffn(x2d, mod48, mod_row, norm_g, w_up, conv_w, conv_b, w_down, final_g, seq_len):
    t = x2d.shape[0]
    nj = D_FF // FFN_TILE
    gate = lambda r, j: (0, j)
    val = lambda r, j: (0, nj + j)
    const = lambda r, j: (0, 0)
    mod = lambda k: pl.BlockSpec((None, 1, D_MODEL), lambda r, j: (mod_row(r) * 6 + k, 0, 0))
    return pl.pallas_call(
        functools.partial(_ffn_kernel, seq_len=seq_len),
        grid=(t // ROW_GROUP, nj),
        in_specs=[pl.BlockSpec((ROW_GROUP, D_MODEL), lambda r, j: (r, 0)),
                  mod(3), mod(4), mod(5),
                  pl.BlockSpec((1, D_MODEL), const),
                  pl.BlockSpec((D_MODEL, FFN_TILE), gate),
                  pl.BlockSpec((D_MODEL, FFN_TILE), val),
                  pl.BlockSpec((3, FFN_TILE), gate),
                  pl.BlockSpec((3, FFN_TILE), val),
                  pl.BlockSpec((1, FFN_TILE), gate),
                  pl.BlockSpec((1, FFN_TILE), val),
                  pl.BlockSpec((FFN_TILE, D_MODEL), lambda r, j: (j, 0)),
                  pl.BlockSpec((1, D_MODEL), const)],
        out_specs=pl.BlockSpec((ROW_GROUP, D_MODEL), lambda r, j: (r, 0)),
        out_shape=jax.ShapeDtypeStruct((t, D_MODEL), F32),
        scratch_shapes=[pltpu.VMEM((ROW_GROUP, D_MODEL), BF16),
                        pltpu.VMEM((ROW_GROUP + 2 * CONV_PAD, 2 * FFN_TILE), F32)],
        compiler_params=_params("arbitrary", "arbitrary"),
        name="conv_ffn",
    )(x2d, mod48, mod48, mod48, norm_g, w_up, w_up, conv_w, conv_w, conv_b, conv_b, w_down, final_g)


def _rope_tables(seq_len):
    t = np.arange(seq_len)
    row = (t // GRID_W).astype(np.float32)
    col = (t % GRID_W).astype(np.float32)
    n = QK_ROPE // 4
    inv = (np.float32(ROPE_BASE) ** (-np.arange(n, dtype=np.float32) / np.float32(n))).astype(np.float32)
    ar, ac = row[:, None] * inv, col[:, None] * inv
    cos64 = np.concatenate([np.cos(ar), np.cos(ar), np.cos(ac), np.cos(ac)], axis=1)
    sin64 = np.concatenate([-np.sin(ar), np.sin(ar), -np.sin(ac), np.sin(ac)], axis=1)
    zeros = np.zeros_like(cos64)
    return (jnp.asarray(np.concatenate([cos64, zeros], axis=1), F32),
            jnp.asarray(np.concatenate([sin64, zeros], axis=1), F32))


def _swap_rope_halves(w):
    lead = w.shape[:-1]
    return w.reshape(lead + (2, 2, QK_ROPE // 4))[..., ::-1, :].reshape(lead + (QK_ROPE,))


def _trunk_pass(x, mod48, mod_row, wts, ctx, latent):
    b, l, _ = x.shape
    x2d = x.reshape(b * l, D_MODEL)
    proj_lo, proj_hi = _in_proj(x2d, mod48, mod_row, wts["norm_attn_g"], wts["w_in_r"], wts["ssd_conv_w"],
                                wts["ssd_conv_b"], l)
    rope = _rope_tables(l) if latent else None
    w_uq_r = wts["w_uq_lat"] if latent else wts["w_uq_ctx"]
    q, k, v, ckv_n, kr = _mla_prep(proj_hi, wts["q_norm_g"], wts["kv_norm_g"], w_uq_r, wts["w_ukv_r"], rope, l)
    shape3 = lambda a, n: a.reshape(b, n, a.shape[-1])
    segs = [(shape3(k, l), shape3(v, l))]
    h0 = None
    if ctx is not None:
        cache_ckv, cache_krope, h0 = ctx
        past = cache_ckv.shape[1]
        k_c, v_c = _kv_up(cache_ckv.reshape(b * past, KV_LORA), cache_krope.reshape(b * past, QK_ROPE), wts["w_ukv_r"])
        segs = [(shape3(k_c, past), shape3(v_c, past))] + segs
    attn = _attention(shape3(q, l), segs)
    kr3 = shape3(kr, l)
    yz, h_t = _ssd(shape3(proj_lo, l), shape3(proj_hi, l), h0, wts["dt_bias128"], wts["a128"], wts["d_exp"])
    x1 = _merge(attn.reshape(b * l, -1), yz.reshape(b * l, -1), proj_lo, x2d, mod48, mod_row, wts["ssd_norm_g"],
                wts["w_o_mla"], wts["w_o_ssd"], wts["w_out"])
    y = _ffn(x1, mod48, mod_row, wts["norm_ffn_g"], wts["w_up"], wts["ffn_conv_w"], wts["ffn_conv_b"], wts["w_down"],
             wts["final_norm_g"], l)
    return y.reshape(b, l, D_MODEL), ckv_n.reshape(b, l, KV_LORA), kr3, h_t


def kernel(x_prompt, x_sample, c, cache_ckv, cache_krope, state_ssd, c_ctx, w_ada, b_ada, norm_attn_g, w_in, q_norm_g,
           kv_norm_g, w_uq, w_ukv, w_o_mla, ssd_conv_w, ssd_conv_b, ssd_dt_bias, ssd_A_log, ssd_D, ssd_norm_g, w_o_ssd,
           w_out, norm_ffn_g, w_up, ffn_conv_w, ffn_conv_b, w_down, final_norm_g):
    depth = w_in.shape[0]
    assert depth == 1, "single trunk layer"
    dec_b = x_sample.shape[0]
    assert x_sample.shape[1] == ROW_GROUP and ROW_GROUP % x_prompt.shape[1] == 0
    lyr = 0

    cvec = jnp.zeros((8, D_MODEL), F32).at[0].set(c_ctx).at[1:1 + dec_b].set(c)
    mod48 = _ada(cvec, w_ada[lyr], b_ada[lyr]).reshape(8 * 6, 1, D_MODEL)

    wi = w_in[lyr]
    o = np.cumsum((0, Q_LORA, KV_LORA, QK_ROPE, SSD_INNER, SSD_INNER, SSD_GROUPS * SSD_STATE,
                   SSD_GROUPS * SSD_STATE, 2 * SSD_HEADS, D_MODEL, D_MODEL))
    piece = lambda i: wi[:, o[i]:o[i + 1]]
    w_kr = piece(2)
    zcols = lambda n: jnp.zeros((D_MODEL, n), F32)
    misc_pad = IN_TILE - 256 - 2 * SSD_HEADS
    w_in_r = jnp.concatenate([piece(3), piece(8), piece(9), piece(5), piece(6),
                              piece(4), piece(0), piece(1),
                              w_kr, zcols(QK_ROPE), _swap_rope_halves(w_kr), zcols(QK_ROPE), piece(7),
                              zcols(misc_pad)], axis=1).astype(BF16)
    wq = w_uq[lyr].reshape(Q_LORA, N_HEADS, QK_NOPE + QK_ROPE)
    wq_nope = wq[:, :, :QK_NOPE].reshape(Q_LORA, -1)
    wq_rope = wq[:, :, QK_NOPE:]
    pad_rope = lambda w: jnp.pad(w, ((0, 0), (0, 0), (0, 128 - QK_ROPE))).reshape(Q_LORA, -1)
    w_uq_ctx = jnp.concatenate([wq_nope, pad_rope(wq_rope)], axis=1).astype(BF16)
    w_uq_lat = jnp.concatenate([wq_nope, pad_rope(wq_rope), pad_rope(_swap_rope_halves(wq_rope))], axis=1).astype(BF16)
    wkv = w_ukv[lyr].reshape(KV_LORA, N_HEADS, QK_NOPE + V_HEAD)
    w_ukv_r = jnp.concatenate([wkv[:, :, :QK_NOPE].reshape(KV_LORA, -1),
                               wkv[:, :, QK_NOPE:].reshape(KV_LORA, -1)], axis=1).astype(BF16)
    pad128 = lambda a: jnp.pad(a.reshape(1, -1), ((0, 0), (0, 128 - a.size)))
    wts = {
        "norm_attn_g": norm_attn_g[lyr].reshape(1, -1), "w_in_r": w_in_r,
        "ssd_conv_w": ssd_conv_w[lyr], "ssd_conv_b": ssd_conv_b[lyr].reshape(1, -1),
        "q_norm_g": q_norm_g[lyr].reshape(1, -1), "kv_norm_g": kv_norm_g[lyr].reshape(1, -1),
        "w_uq_ctx": w_uq_ctx, "w_uq_lat": w_uq_lat, "w_ukv_r": w_ukv_r,
        "dt_bias128": pad128(ssd_dt_bias[lyr]), "a128": pad128(-jnp.exp(ssd_A_log[lyr])),
        "d_exp": jnp.repeat(ssd_D[lyr], SSD_HEADDIM).reshape(1, -1),
        "ssd_norm_g": ssd_norm_g[lyr].reshape(1, -1),
        "w_o_mla": w_o_mla[lyr], "w_o_ssd": w_o_ssd[lyr], "w_out": w_out[lyr],
        "norm_ffn_g": norm_ffn_g[lyr].reshape(1, -1), "w_up": w_up[lyr].astype(BF16),
        "ffn_conv_w": ffn_conv_w[lyr], "ffn_conv_b": ffn_conv_b[lyr].reshape(1, -1),
        "w_down": w_down[lyr].astype(BF16), "final_norm_g": final_norm_g.reshape(1, -1),
    }

    y_p, ckv_p, kr_p, st_p = _trunk_pass(x_prompt, mod48, lambda r: 0, wts, None, False)
    ctx = (cache_ckv[:, lyr], cache_krope[:, lyr], state_ssd[:, lyr])
    y_s, _, _, _ = _trunk_pass(x_sample, mod48, lambda r: 1 + r, wts, ctx, True)
    return y_p, y_s, ckv_p[:, None], kr_p[:, None], st_p[:, None]
```

```python
import functools
import math

import jax
import jax.numpy as jnp
import numpy as np
from jax import lax
from jax.experimental import pallas as pl
from jax.experimental.pallas import tpu as pltpu

F32 = jnp.float32
BF16 = jnp.bfloat16

D_MODEL = 1024
GRID_W = 64
N_HEADS = 8
QK_NOPE = 128
QK_ROPE = 64
V_HEAD = 128
Q_LORA = 256
KV_LORA = 256
ROPE_BASE = 10000.0
SSD_HEADS = 16
SSD_HEADDIM = 64
SSD_INNER = SSD_HEADS * SSD_HEADDIM
SSD_GROUPS = 4
SSD_STATE = 64
SSD_CHUNK = 128
D_FF = 2816
EPS = 1e-6

ROW_GROUP = 2048
IN_TILE = 512
SSD_CHUNKS_PER_STEP = 2
FFN_TILE = 256
TOKEN_TILE = 512
ATTN_Q_TILE = 256
CONV_CHUNK = 256
CONV_PAD = 8
VMEM_LIMIT = 56 * 1024 * 1024
NEG_BIG = -1e30


def _sigmoid(x):
    return 1.0 / (1.0 + jnp.exp(-x))


def _silu(x):
    return x * _sigmoid(x)


def _softplus(x):
    e = jnp.exp(-jnp.abs(x))
    u = 1.0 + e
    log1p_e = jnp.where(u == 1.0, e, e * jnp.log(u) / jnp.where(u == 1.0, 1.0, u - 1.0))
    return jnp.maximum(x, 0.0) + log1p_e


def _rmsnorm(x, g):
    return x * lax.rsqrt(jnp.mean(x * x, axis=-1, keepdims=True) + EPS) * g


def _dot(a, b):
    return jnp.dot(a, b, preferred_element_type=F32)


def _dot_nt(a, b):
    return lax.dot_general(a, b, (((1,), (1,)), ((), ())), preferred_element_type=F32)


def _params(*sem):
    return pltpu.CompilerParams(dimension_semantics=sem, vmem_limit_bytes=VMEM_LIMIT)


def _norm_mod_rows(x_ref, g_ref, sc_ref, sh_ref, h_scr):
    def body(i, carry):
        rows = pl.ds(pl.multiple_of(i * CONV_CHUNK, CONV_CHUNK), CONV_CHUNK)
        h = _rmsnorm(x_ref[rows, :], g_ref[...]) * (1.0 + sc_ref[...]) + sh_ref[...]
        h_scr[rows, :] = h.astype(BF16)
        return carry
    lax.fori_loop(0, ROW_GROUP // CONV_CHUNK, body, 0)


def _zero_conv_pads(u_scr):
    zeros = jnp.zeros((CONV_PAD, u_scr.shape[1]), F32)
    u_scr[0:CONV_PAD, :] = zeros
    u_scr[CONV_PAD + ROW_GROUP:2 * CONV_PAD + ROW_GROUP, :] = zeros


def _stage_rows(c):
    return slice(CONV_PAD + c * CONV_CHUNK, CONV_PAD + (c + 1) * CONV_CHUNK)


def _dwconv3_rows(u_scr, c, cols, w_ref, b_ref, seq_len):
    r0 = c * CONV_CHUNK
    base = CONV_PAD + r0
    width = cols.stop - cols.start
    prev = u_scr[base - 1:base - 1 + CONV_CHUNK, cols]
    cur = u_scr[base:base + CONV_CHUNK, cols]
    nxt = u_scr[base + 1:base + 1 + CONV_CHUNK, cols]
    row = lax.broadcasted_iota(jnp.int32, (CONV_CHUNK, width), 0)
    if r0 % seq_len == 0:
        prev = jnp.where(row == 0, 0.0, prev)
    if (r0 + CONV_CHUNK) % seq_len == 0:
        nxt = jnp.where(row == CONV_CHUNK - 1, 0.0, nxt)
    return prev * w_ref[0:1, :] + cur * w_ref[1:2, :] + nxt * w_ref[2:3, :] + b_ref[...]


def _ada_kernel(c_ref, w_ref, b_ref, o_ref):
    a = _silu(c_ref[...]).astype(BF16)
    o_ref[...] = _dot(a, w_ref[...].astype(BF16)) + b_ref[...]


def _ada(cvec, w_ada, b_ada):
    tn = 1536
    return pl.pallas_call(
        _ada_kernel,
        grid=(6 * D_MODEL // tn,),
        in_specs=[pl.BlockSpec((8, D_MODEL), lambda j: (0, 0)),
                  pl.BlockSpec((D_MODEL, tn), lambda j: (0, j)),
                  pl.BlockSpec((1, tn), lambda j: (0, j))],
        out_specs=pl.BlockSpec((8, tn), lambda j: (0, j)),
        out_shape=jax.ShapeDtypeStruct((8, 6 * D_MODEL), F32),
        compiler_params=_params("arbitrary"),
        name="ada_mod",
    )(cvec, w_ada, b_ada.reshape(1, -1))


N_LOW_TILES, N_F32_TILES = 7, 4
Z_BLK, GM_BLK, GS_BLK = 0, 1, 2
BC_TILE = 6
XS_BLK = 0
MLA_TILE, MISC_TILE = 2, 3
DT_BLK = (MISC_TILE * IN_TILE + 256) // 128
ATTN_HEAD_COLS = 256


def _in_kernel(x_ref, sh_ref, sc_ref, g_ref, w_ref, cw_ref, cb_ref, lo_ref, hi_ref, h_scr, u_scr, *, seq_len):
    j = pl.program_id(1)
    n_chunks = ROW_GROUP // CONV_CHUNK
    rows = lambda c: slice(c * CONV_CHUNK, (c + 1) * CONV_CHUNK)

    @pl.when(j == 0)
    def _():
        _norm_mod_rows(x_ref, g_ref, sc_ref, sh_ref, h_scr)
        _zero_conv_pads(u_scr)

    def pointwise(fn, o_ref):
        for c in range(n_chunks):
            o_ref[rows(c), :] = fn(_dot(h_scr[rows(c), :], w_ref[...])).astype(o_ref.dtype)

    def conv(o_ref):
        def conv_out(c):
            v = _silu(_dwconv3_rows(u_scr, c, slice(0, IN_TILE), cw_ref, cb_ref, seq_len))
            o_ref[rows(c), :] = v.astype(o_ref.dtype)
        for c in range(n_chunks):
            u_scr[_stage_rows(c), :] = _dot(h_scr[rows(c), :], w_ref[...])
            if c >= 1:
                conv_out(c - 1)
        conv_out(n_chunks - 1)

    @pl.when(j <= 1)
    def _():
        pointwise(_silu, lo_ref)

    @pl.when((j >= 2) & (j <= 5))
    def _():
        pointwise(_sigmoid, lo_ref)

    @pl.when(j == BC_TILE)
    def _():
        conv(lo_ref)

    @pl.when((j == N_LOW_TILES) | (j == N_LOW_TILES + 1))
    def _():
        conv(hi_ref)

    @pl.when(j >= N_LOW_TILES + MLA_TILE)
    def _():
        pointwise(lambda u: u, hi_ref)


def _in_proj(x2d, mod48, mod_row, norm_g, w_in_r, conv_w, conv_b, seq_len):
    t = x2d.shape[0]
    n_tiles = N_LOW_TILES + N_F32_TILES
    conv_idx = lambda r, j: (0, jnp.where(j == BC_TILE, 2, jnp.clip(j - N_LOW_TILES, 0, 1)))
    return pl.pallas_call(
        functools.partial(_in_kernel, seq_len=seq_len),
        grid=(t // ROW_GROUP, n_tiles),
        in_specs=[pl.BlockSpec((ROW_GROUP, D_MODEL), lambda r, j: (r, 0)),
                  pl.BlockSpec((None, 1, D_MODEL), lambda r, j: (mod_row(r) * 6 + 0, 0, 0)),
                  pl.BlockSpec((None, 1, D_MODEL), lambda r, j: (mod_row(r) * 6 + 1, 0, 0)),
                  pl.BlockSpec((1, D_MODEL), lambda r, j: (0, 0)),
                  pl.BlockSpec((D_MODEL, IN_TILE), lambda r, j: (0, j)),
                  pl.BlockSpec((3, IN_TILE), conv_idx),
                  pl.BlockSpec((1, IN_TILE), conv_idx)],
        out_specs=[pl.BlockSpec((ROW_GROUP, IN_TILE), lambda r, j: (r, jnp.minimum(j, N_LOW_TILES - 1))),
                   pl.BlockSpec((ROW_GROUP, IN_TILE), lambda r, j: (r, jnp.maximum(j - N_LOW_TILES, 0)))],
        out_shape=[jax.ShapeDtypeStruct((t, N_LOW_TILES * IN_TILE), BF16),
                   jax.ShapeDtypeStruct((t, N_F32_TILES * IN_TILE), F32)],
        scratch_shapes=[pltpu.VMEM((ROW_GROUP, D_MODEL), BF16),
                        pltpu.VMEM((ROW_GROUP + 2 * CONV_PAD, IN_TILE), F32)],
        compiler_params=_params("arbitrary", "arbitrary"),
        name="in_proj",
    )(x2d, mod48, mod48, norm_g, w_in_r, conv_w, conv_b)


def _mla_kernel(*refs, latent, has_ctx, seq_len):
    refs = list(refs)
    pm_ref, px_ref, qg_ref, kvg_ref, wuq_ref, wukv_ref = refs[:6]
    del refs[:6]
    if latent:
        cos_ref, sin_ref = refs[:2]
        del refs[:2]
    if has_ctx:
        cckv_ref, ckr_ref = refs[:2]
        del refs[:2]
    o_ref, ckv_ref, kr_ref, k_scr, v_scr = refs[:5]
    del refs[:5]
    if has_ctx:
        kc_scr, vc_scr = refs
    t = pl.program_id(1)
    n_nope = N_HEADS * QK_NOPE
    tq = ATTN_Q_TILE

    def put_keys(k_dst, v_dst, rows, ckv_n, kr128):
        kv = _dot(ckv_n.astype(BF16), wukv_ref[...])
        v_dst[rows, :] = kv[:, n_nope:].astype(BF16)
        kr_bf = kr128.astype(BF16)
        for h in range(N_HEADS):
            base = h * ATTN_HEAD_COLS
            k_dst[rows, base:base + QK_NOPE] = kv[:, h * QK_NOPE:(h + 1) * QK_NOPE].astype(BF16)
            k_dst[rows, base + QK_NOPE:base + ATTN_HEAD_COLS] = kr_bf

    @pl.when(t == 0)
    def _():
        step = min(seq_len, TOKEN_TILE)
        for r0 in range(0, seq_len, step):
            rows = slice(r0, r0 + step)
            ckv_n = _rmsnorm(pm_ref[0, rows, Q_LORA:], kvg_ref[...])
            ckv_ref[0, rows, :] = ckv_n
            kr = px_ref[0, rows, 0:128]
            if latent:
                kr = kr * cos_ref[rows, :] + px_ref[0, rows, 128:256] * sin_ref[rows, :]
            kr_ref[0, rows, :] = kr[:, :QK_ROPE]
            put_keys(k_scr, v_scr, rows, ckv_n, kr)
        if has_ctx:
            past = cckv_ref.shape[1]
            put_keys(kc_scr, vc_scr, slice(0, past), cckv_ref[0], jnp.zeros((past, 128), F32))
            ckr_bf = ckr_ref[0].astype(BF16)
            for h in range(N_HEADS):
                base = h * ATTN_HEAD_COLS + QK_NOPE
                kc_scr[:, base:base + QK_ROPE] = ckr_bf

    qrows = pl.ds(pl.multiple_of(t * tq, tq), tq)
    scale = 1.0 / math.sqrt(QK_NOPE + QK_ROPE)
    cqn = _rmsnorm(pm_ref[0, qrows, 0:Q_LORA], qg_ref[...]).astype(BF16)
    q = _dot(cqn, wuq_ref[...])
    q_rope = q[:, n_nope:2 * n_nope]
    if latent:
        q_rope = (q_rope * jnp.concatenate([cos_ref[qrows, :]] * N_HEADS, axis=1)
                  + q[:, 2 * n_nope:3 * n_nope] * jnp.concatenate([sin_ref[qrows, :]] * N_HEADS, axis=1))
    segs = ([(kc_scr, vc_scr)] if has_ctx else []) + [(k_scr, v_scr)]
    for h in range(N_HEADS):
        head = slice(h * QK_NOPE, (h + 1) * QK_NOPE)
        qk = slice(h * ATTN_HEAD_COLS, (h + 1) * ATTN_HEAD_COLS)
        qh = (jnp.concatenate([q[:, head], q_rope[:, head]], axis=1) * scale).astype(BF16)
        s = [_dot_nt(qh, k[:, qk]) for k, _ in segs]
        m = functools.reduce(jnp.maximum, [jnp.max(si, axis=-1, keepdims=True) for si in s])
        p = [jnp.exp(si - m) for si in s]
        l = functools.reduce(jnp.add, [jnp.sum(pi, axis=-1, keepdims=True) for pi in p])
        o = functools.reduce(jnp.add, [_dot(pi.astype(BF16), v[:, head]) for pi, (_, v) in zip(p, segs)])
        o_ref[0, :, head] = (o / l).astype(BF16)


def _mla(proj_hi, q_norm_g, kv_norm_g, w_uq_r, w_ukv_r, rope_tables, ctx):
    b, l, _ = proj_hi.shape
    tq = ATTN_Q_TILE
    latent = rope_tables is not None
    has_ctx = ctx is not None
    once = dict(pipeline_mode=pl.Buffered(1))
    const2 = lambda i, t: (0, 0)
    in_specs = [pl.BlockSpec((1, l, IN_TILE), lambda i, t: (i, 0, MLA_TILE), **once),
                pl.BlockSpec((1, l, 256), lambda i, t: (i, 0, MISC_TILE * IN_TILE // 256), **once),
                pl.BlockSpec((1, Q_LORA), const2),
                pl.BlockSpec((1, KV_LORA), const2),
                pl.BlockSpec(w_uq_r.shape, const2, **once),
                pl.BlockSpec(w_ukv_r.shape, const2, **once)]
    args = [proj_hi, proj_hi, q_norm_g, kv_norm_g, w_uq_r, w_ukv_r]
    scratch = [pltpu.VMEM((l, N_HEADS * ATTN_HEAD_COLS), BF16), pltpu.VMEM((l, N_HEADS * V_HEAD), BF16)]
    if latent:
        in_specs += [pl.BlockSpec((l, 128), const2, **once)] * 2
        args += list(rope_tables)
    if has_ctx:
        past = ctx[0].shape[1]
        in_specs += [pl.BlockSpec((1, past, KV_LORA), lambda i, t: (i, 0, 0), **once),
                     pl.BlockSpec((1, past, QK_ROPE), lambda i, t: (i, 0, 0), **once)]
        args += list(ctx)
        scratch += [pltpu.VMEM((past, N_HEADS * ATTN_HEAD_COLS), BF16), pltpu.VMEM((past, N_HEADS * V_HEAD), BF16)]
    return pl.pallas_call(
        functools.partial(_mla_kernel, latent=latent, has_ctx=has_ctx, seq_len=l),
        grid=(b, l // tq),
        in_specs=in_specs,
        out_specs=[pl.BlockSpec((1, tq, N_HEADS * V_HEAD), lambda i, t: (i, t, 0)),
                   pl.BlockSpec((1, l, KV_LORA), lambda i, t: (i, 0, 0)),
                   pl.BlockSpec((1, l, QK_ROPE), lambda i, t: (i, 0, 0))],
        out_shape=[jax.ShapeDtypeStruct((b, l, N_HEADS * V_HEAD), BF16),
                   jax.ShapeDtypeStruct((b, l, KV_LORA), F32),
                   jax.ShapeDtypeStruct((b, l, QK_ROPE), F32)],
        scratch_shapes=scratch,
        compiler_params=_params("arbitrary", "arbitrary"),
        name="mla_attention",
    )(*args)


def _split3(x):
    hi = x.astype(BF16)
    r = x - hi.astype(F32)
    mid = r.astype(BF16)
    lo = (r - mid.astype(F32)).astype(BF16)
    return hi, mid, lo


def _exact_dot(parts, sel):
    return functools.reduce(jnp.add, [_dot(p, sel) for p in parts])


def _exact_dot_rows(sel, parts):
    return functools.reduce(jnp.add, [_dot(sel, p) for p in parts])


HEADS_PER_GROUP = SSD_HEADS // SSD_GROUPS
GROUP_COLS = HEADS_PER_GROUP * SSD_HEADDIM


def _ssd_kernel(*refs, nc, cps, has_h0):
    if has_h0:
        (xs_ref, zs_ref, bc_ref, dt_ref, dtb_ref, a_ref, dx_ref, h0_ref, y_ref, ht_ref,
         yl_scr, acum_scr, acumt_scr, dtt_scr, tot_scr, sb_scr, h_scr, esel_scr) = refs
    else:
        (xs_ref, zs_ref, bc_ref, dt_ref, dtb_ref, a_ref, dx_ref, y_ref, ht_ref,
         yl_scr, acum_scr, acumt_scr, dtt_scr, tot_scr, sb_scr, h_scr, esel_scr) = refs
    s = pl.program_id(1)
    q = SSD_CHUNK
    n_bc = SSD_GROUPS * SSD_STATE
    lane = lax.broadcasted_iota(jnp.int32, (q, 128), 1)
    low_half = lane < SSD_HEADDIM
    ii = lax.broadcasted_iota(jnp.int32, (q, q), 0)
    jj = lax.broadcasted_iota(jnp.int32, (q, q), 1)
    lower, upper = ii >= jj, ii <= jj

    def lane_bcast(parts, d):
        return _exact_dot(parts, esel_scr[d])

    def stacked_states(d):
        return [jnp.concatenate([h_scr[d, 2 * i], h_scr[d, 2 * i + 1]], axis=0).astype(BF16)
                for i in range(SSD_GROUPS // 2)]

    def group_c(bc, g):
        i, r = divmod(g, 2)
        cpair = bc[:, n_bc + i * 128:n_bc + (i + 1) * 128]
        return jnp.where(low_half if r == 0 else ~low_half, cpair, 0.0).astype(BF16)

    @pl.when(s == 0)
    def _():
        k = lax.broadcasted_iota(jnp.int32, (128, SSD_INNER), 0)
        head = lax.broadcasted_iota(jnp.int32, (128, SSD_INNER), 1) // SSD_HEADDIM
        for d in range(2):
            esel_scr[d] = jnp.where(k == d * SSD_HEADS + head, 1.0, 0.0).astype(BF16)
        if has_h0:
            for d in range(2):
                for g in range(SSD_GROUPS):
                    hpn = h0_ref[0, d, g * HEADS_PER_GROUP:(g + 1) * HEADS_PER_GROUP].reshape(GROUP_COLS, SSD_STATE)
                    h_scr[d, g] = hpn.T
        else:
            h_scr[...] = jnp.zeros(h_scr.shape, F32)
        tri_f = jnp.where(lower, 1.0, 0.0).astype(BF16)
        tri_b = jnp.where(upper, 1.0, 0.0).astype(BF16)
        fwd_col = lane < SSD_HEADS
        for c in range(nc):
            crow = slice(c * q, (c + 1) * q)
            dt = _softplus(dt_ref[0, crow, :] + dtb_ref[...])
            parts = _split3(dt * a_ref[...])
            acum = jnp.where(fwd_col, _exact_dot_rows(tri_f, parts),
                             _exact_dot_rows(tri_b, parts))
            acum_scr[crow, :] = acum
            tot = jnp.where(fwd_col[0:1], acum[q - 1:q, :], acum[0:1, :])
            tot_scr[c] = jnp.broadcast_to(tot, (8, 128))
            acumt_scr[c] = acum.T
            dtt_scr[c] = dt.T

    def first_sweep(c, blk):
        rows = pl.ds(pl.multiple_of(c * q, q), q)
        x = xs_ref[0, blk, :]
        bc = bc_ref[0, blk, :].astype(F32)
        acum = acum_scr[rows, :]
        acum_t, dt_t = acumt_scr[c], dtt_scr[c]
        tot8 = tot_scr[c]
        e = jnp.exp(acum)
        e_hi = e.astype(BF16)
        eb_f = lane_bcast([e_hi, (e - e_hi.astype(F32)).astype(BF16)], 0)
        cd_f = jnp.exp(lane_bcast(_split3(tot8), 0))[0:1]
        b_t = [bc[:, i * 128:(i + 1) * 128].T for i in range(SSD_GROUPS // 2)]
        h_in = stacked_states(0)
        for g in range(SSD_GROUPS):
            i, r = divmod(g, 2)
            cm = group_c(bc, g)
            scores = _dot_nt(cm, bc[:, i * 128:(i + 1) * 128].astype(BF16))
            bg_t = b_t[i][r * SSD_STATE:(r + 1) * SSD_STATE, :]
            gcols = slice(g * GROUP_COLS, (g + 1) * GROUP_COLS)
            y_off = _dot(cm, h_in[i]) * eb_f[:, gcols]
            for t in range(HEADS_PER_GROUP // 2):
                pc = slice((2 * g + t) * 128, (2 * g + t + 1) * 128)
                xp = x[:, pc]
                x_half = [jnp.where(low_half, xp, 0.0).astype(BF16), jnp.where(low_half, 0.0, xp).astype(BF16)]
                m, sf, sb = [], [], []
                for u in range(2):
                    kf = g * HEADS_PER_GROUP + 2 * t + u
                    kb = SSD_HEADS + kf
                    af_col = jnp.broadcast_to(acum[:, kf:kf + 1], (q, q))
                    ab_col = jnp.broadcast_to(acum[:, kb:kb + 1], (q, q))
                    af_row, ab_row = acum_t[kf:kf + 1, :], acum_t[kb:kb + 1, :]
                    dtf_row, dtb_row = dt_t[kf:kf + 1, :], dt_t[kb:kb + 1, :]
                    decay = (jnp.exp(jnp.where(lower, af_col - af_row, NEG_BIG)) * dtf_row
                             + jnp.exp(jnp.where(upper, ab_col - ab_row, NEG_BIG)) * dtb_row)
                    m.append((scores * decay).astype(BF16))
                    wf = jnp.exp(af_row[:, q - 1:q] - af_row) * dtf_row
                    wb = jnp.exp(ab_row[:, 0:1] - ab_row) * dtb_row
                    sf.append((bg_t * wf).astype(BF16))
                    sb.append((bg_t * wb).astype(BF16))
                y_pair = _dot(m[0], x_half[0]) + _dot(m[1], x_half[1])
                tc = slice(t * 128, (t + 1) * 128)
                yl_scr[rows, pc] = y_pair + y_off[:, tc] + dx_ref[:, pc] * xp
                h_scr[0, g, :, tc] = (h_scr[0, g, :, tc] * cd_f[:, pc]
                                      + _dot(sf[0], x_half[0]) + _dot(sf[1], x_half[1]))
                sb_scr[c, g, :, tc] = _dot(sb[0], x_half[0]) + _dot(sb[1], x_half[1])

    def second_sweep(c, blk):
        rows = pl.ds(pl.multiple_of(c * q, q), q)
        bc = bc_ref[0, blk, :].astype(F32)
        e = jnp.exp(acum_scr[rows, :])
        e_hi = e.astype(BF16)
        eb_b = lane_bcast([e_hi, (e - e_hi.astype(F32)).astype(BF16)], 1)
        cd_b = jnp.exp(lane_bcast(_split3(tot_scr[c]), 1))[0:1]
        h_in = stacked_states(1)
        for g in range(SSD_GROUPS):
            gcols = slice(g * GROUP_COLS, (g + 1) * GROUP_COLS)
            y_off = _dot(group_c(bc, g), h_in[g // 2]) * eb_b[:, gcols]
            y_ref[0, blk, gcols] = ((yl_scr[rows, gcols] + y_off) * zs_ref[0, blk, gcols].astype(F32)).astype(y_ref.dtype)
            h_scr[1, g] = h_scr[1, g] * cd_b[:, gcols] + sb_scr[c, g]

    n_steps = nc // cps

    @pl.when(s < n_steps)
    def _():
        for ci in range(cps):
            first_sweep(s * cps + ci, slice(ci * q, (ci + 1) * q))

    @pl.when(s >= n_steps)
    def _():
        for ci in reversed(range(cps)):
            second_sweep((2 * n_steps - 1 - s) * cps + ci, slice(ci * q, (ci + 1) * q))

    @pl.when(s == 2 * n_steps - 1)
    def _():
        for d in range(2):
            for g in range(SSD_GROUPS):
                ht_ref[0, d, g * HEADS_PER_GROUP:(g + 1) * HEADS_PER_GROUP] = h_scr[d, g].T.reshape(
                    HEADS_PER_GROUP, SSD_HEADDIM, SSD_STATE)


def _ssd(proj_lo, proj_hi, h0, dt_bias128, a128, d_exp):
    b, l, _ = proj_hi.shape
    q = SSD_CHUNK
    nc = l // q
    cps = SSD_CHUNKS_PER_STEP
    n_steps = nc // cps
    rows = cps * q
    early = lambda s: jnp.minimum(s, n_steps - 1)
    both = lambda s: jnp.where(s < n_steps, s, 2 * n_steps - 1 - s)
    late = lambda s: jnp.where(s < n_steps, n_steps - 1, 2 * n_steps - 1 - s)
    st_shape = (1, 2, SSD_HEADS, SSD_HEADDIM, SSD_STATE)
    st_spec = pl.BlockSpec(st_shape, lambda i, s: (i, 0, 0, 0, 0))
    has_h0 = h0 is not None
    return pl.pallas_call(
        functools.partial(_ssd_kernel, nc=nc, cps=cps, has_h0=has_h0),
        grid=(b, 2 * n_steps),
        in_specs=[pl.BlockSpec((1, rows, SSD_INNER), lambda i, s: (i, early(s), XS_BLK)),
                  pl.BlockSpec((1, rows, SSD_INNER), lambda i, s: (i, late(s), Z_BLK)),
                  pl.BlockSpec((1, rows, IN_TILE), lambda i, s: (i, both(s), BC_TILE)),
                  pl.BlockSpec((1, l, 128), lambda i, s: (i, 0, DT_BLK)),
                  pl.BlockSpec((1, 128), lambda i, s: (0, 0)),
                  pl.BlockSpec((1, 128), lambda i, s: (0, 0)),
                  pl.BlockSpec((1, SSD_INNER), lambda i, s: (0, 0))] + ([st_spec] if has_h0 else []),
        out_specs=[pl.BlockSpec((1, rows, SSD_INNER), lambda i, s: (i, late(s), 0)),
                   pl.BlockSpec(st_shape, lambda i, s: (i, 0, 0, 0, 0))],
        out_shape=[jax.ShapeDtypeStruct((b, l, SSD_INNER), BF16),
                   jax.ShapeDtypeStruct((b,) + st_shape[1:], F32)],
        scratch_shapes=[pltpu.VMEM((l, SSD_INNER), F32),
                        pltpu.VMEM((l, 128), F32),
                        pltpu.VMEM((nc, 128, q), F32),
                        pltpu.VMEM((nc, 128, q), F32),
                        pltpu.VMEM((nc, 8, 128), F32),
                        pltpu.VMEM((nc, SSD_GROUPS, SSD_STATE, GROUP_COLS), F32),
                        pltpu.VMEM((2, SSD_GROUPS, SSD_STATE, GROUP_COLS), F32),
                        pltpu.VMEM((2, 128, SSD_INNER), BF16)],
        compiler_params=_params("arbitrary", "arbitrary"),
        name="ssd_scan",
    )(proj_hi, proj_lo, proj_lo, proj_hi, dt_bias128, a128, d_exp, *([h0] if has_h0 else []))


def _merge_kernel(attn_ref, yz_ref, gm_ref, gs_ref, x_ref, g1_ref, ng_ref, womla_ref, wossd_ref, wout_ref, o_ref, w_scr):
    @pl.when(pl.program_id(0) == 0)
    def _():
        w_scr[0] = womla_ref[...].astype(BF16)
        w_scr[1] = wossd_ref[...].astype(BF16)
        w_scr[2] = wout_ref[...].astype(BF16)

    o_mla = _dot(attn_ref[...], w_scr[0])
    o_ssd = _dot(_rmsnorm(yz_ref[...].astype(F32), ng_ref[...]).astype(BF16), w_scr[1])
    merged = gm_ref[...].astype(F32) * o_mla + gs_ref[...].astype(F32) * o_ssd
    o_ref[...] = x_ref[...] + g1_ref[...] * _dot(merged.astype(BF16), w_scr[2])


def _merge(attn2d, yz2d, proj, x2d, mod48, mod_row, ssd_norm_g, w_o_mla, w_o_ssd, w_out):
    t = x2d.shape[0]
    tm = TOKEN_TILE
    row = lambda i: (i, 0)
    const = lambda i: (0, 0)
    wspec = pl.BlockSpec((D_MODEL, D_MODEL), const, pipeline_mode=pl.Buffered(1))
    return pl.pallas_call(
        _merge_kernel,
        grid=(t // tm,),
        in_specs=[pl.BlockSpec((tm, D_MODEL), row),
                  pl.BlockSpec((tm, D_MODEL), row),
                  pl.BlockSpec((tm, D_MODEL), lambda i: (i, GM_BLK)),
                  pl.BlockSpec((tm, D_MODEL), lambda i: (i, GS_BLK)),
                  pl.BlockSpec((tm, D_MODEL), row),
                  pl.BlockSpec((None, 1, D_MODEL), lambda i: (mod_row(i * tm // ROW_GROUP) * 6 + 2, 0, 0)),
                  pl.BlockSpec((1, D_MODEL), const),
                  wspec, wspec, wspec],
        out_specs=pl.BlockSpec((tm, D_MODEL), row),
        out_shape=jax.ShapeDtypeStruct((t, D_MODEL), F32),
        scratch_shapes=[pltpu.VMEM((3, D_MODEL, D_MODEL), BF16)],
        compiler_params=_params("arbitrary"),
        name="merge_out",
    )(attn2d, yz2d, proj, proj, x2d, mod48, ssd_norm_g, w_o_mla, w_o_ssd, w_out)


def _ffn_kernel(x_ref, sh_ref, sc_ref, g2_ref, ng_ref, wg_ref, wv_ref, cwg_ref, cwv_ref, cbg_ref, cbv_ref, wd_ref,
                fg_ref, o_ref, h_scr, wup_scr, wd_scr, u_scr, *, seq_len):
    j = pl.program_id(1)
    n_chunks = ROW_GROUP // CONV_CHUNK
    rows = lambda c: slice(c * CONV_CHUNK, (c + 1) * CONV_CHUNK)

    @pl.when(j == 0)
    def _():
        _norm_mod_rows(x_ref, ng_ref, sc_ref, sh_ref, h_scr)
        _zero_conv_pads(u_scr)
        o_ref[...] = jnp.zeros_like(o_ref)

    wup_scr[:, 0:FFN_TILE] = wg_ref[...].astype(BF16)
    wup_scr[:, FFN_TILE:2 * FFN_TILE] = wv_ref[...].astype(BF16)
    wd_scr[...] = wd_ref[...].astype(BF16)

    def gated(c):
        ug = _dwconv3_rows(u_scr, c, slice(0, FFN_TILE), cwg_ref, cbg_ref, seq_len)
        uv = _dwconv3_rows(u_scr, c, slice(FFN_TILE, 2 * FFN_TILE), cwv_ref, cbv_ref, seq_len)
        return (_silu(ug) * uv).astype(BF16)

    act = {}
    for c in range(n_chunks + 2):
        if c < n_chunks:
            u_scr[_stage_rows(c), :] = _dot(h_scr[rows(c), :], wup_scr[...])
        if c >= 2:
            o_ref[rows(c - 2), :] += _dot(act.pop(c - 2), wd_scr[...])
        if 1 <= c <= n_chunks:
            act[c - 1] = gated(c - 1)

    @pl.when(j == pl.num_programs(1) - 1)
    def _():
        def body(i, carry):
            rows = pl.ds(pl.multiple_of(i * 256, 256), 256)
            o_ref[rows, :] = _rmsnorm(x_ref[rows, :] + g2_ref[...] * o_ref[rows, :], fg_ref[...])
            return carry
        lax.fori_loop(0, ROW_GROUP // 256, body, 0)


def _ffn(x2d, mod48, mod_row, norm_g, w_up, conv_w, conv_b, w_down, final_g, seq_len):
    t = x2d.shape[0]
    nj = D_FF // FFN_TILE
    gate = lambda r, j: (0, j)
    val = lambda r, j: (0, nj + j)
    const = lambda r, j: (0, 0)
    mod = lambda k: pl.BlockSpec((None, 1, D_MODEL), lambda r, j: (mod_row(r) * 6 + k, 0, 0))
    return pl.pallas_call(
        functools.partial(_ffn_kernel, seq_len=seq_len),
        grid=(t // ROW_GROUP, nj),
        in_specs=[pl.BlockSpec((ROW_GROUP, D_MODEL), lambda r, j: (r, 0)),
                  mod(3), mod(4), mod(5),
                  pl.BlockSpec((1, D_MODEL), const),
                  pl.BlockSpec((D_MODEL, FFN_TILE), gate),
                  pl.BlockSpec((D_MODEL, FFN_TILE), val),
                  pl.BlockSpec((3, FFN_TILE), gate),
                  pl.BlockSpec((3, FFN_TILE), val),
                  pl.BlockSpec((1, FFN_TILE), gate),
                  pl.BlockSpec((1, FFN_TILE), val),
                  pl.BlockSpec((FFN_TILE, D_MODEL), lambda r, j: (j, 0)),
                  pl.BlockSpec((1, D_MODEL), const)],
        out_specs=pl.BlockSpec((ROW_GROUP, D_MODEL), lambda r, j: (r, 0)),
        out_shape=jax.ShapeDtypeStruct((t, D_MODEL), F32),
        scratch_shapes=[pltpu.VMEM((ROW_GROUP, D_MODEL), BF16),
                        pltpu.VMEM((D_MODEL, 2 * FFN_TILE), BF16),
                        pltpu.VMEM((FFN_TILE, D_MODEL), BF16),
                        pltpu.VMEM((ROW_GROUP + 2 * CONV_PAD, 2 * FFN_TILE), F32)],
        compiler_params=_params("arbitrary", "arbitrary"),
        name="conv_ffn",
    )(x2d, mod48, mod48, mod48, norm_g, w_up, w_up, conv_w, conv_w, conv_b, conv_b, w_down, final_g)


def _rope_tables(seq_len):
    t = np.arange(seq_len)
    row = (t // GRID_W).astype(np.float32)
    col = (t % GRID_W).astype(np.float32)
    n = QK_ROPE // 4
    inv = (np.float32(ROPE_BASE) ** (-np.arange(n, dtype=np.float32) / np.float32(n))).astype(np.float32)
    ar, ac = row[:, None] * inv, col[:, None] * inv
    cos64 = np.concatenate([np.cos(ar), np.cos(ar), np.cos(ac), np.cos(ac)], axis=1)
    sin64 = np.concatenate([-np.sin(ar), np.sin(ar), -np.sin(ac), np.sin(ac)], axis=1)
    zeros = np.zeros_like(cos64)
    return (jnp.asarray(np.concatenate([cos64, zeros], axis=1), F32),
            jnp.asarray(np.concatenate([sin64, zeros], axis=1), F32))


def _swap_rope_halves(w):
    lead = w.shape[:-1]
    return w.reshape(lead + (2, 2, QK_ROPE // 4))[..., ::-1, :].reshape(lead + (QK_ROPE,))


def _trunk_pass(x, mod48, mod_row, wts, ctx, latent):
    b, l, _ = x.shape
    x2d = x.reshape(b * l, D_MODEL)
    proj_lo, proj_hi = _in_proj(x2d, mod48, mod_row, wts["norm_attn_g"], wts["w_in_r"], wts["ssd_conv_w"],
                                wts["ssd_conv_b"], l)
    rope = _rope_tables(l) if latent else None
    w_uq_r = wts["w_uq_lat"] if latent else wts["w_uq_ctx"]
    shape3 = lambda a, n: a.reshape(b, n, a.shape[-1])
    h0 = None
    mla_ctx = None
    if ctx is not None:
        cache_ckv, cache_krope, h0 = ctx
        mla_ctx = (cache_ckv, cache_krope)
    attn, ckv_n, kr3 = _mla(shape3(proj_hi, l), wts["q_norm_g"], wts["kv_norm_g"], w_uq_r, wts["w_ukv_r"], rope, mla_ctx)
    yz, h_t = _ssd(shape3(proj_lo, l), shape3(proj_hi, l), h0, wts["dt_bias128"], wts["a128"], wts["d_exp"])
    x1 = _merge(attn.reshape(b * l, -1), yz.reshape(b * l, -1), proj_lo, x2d, mod48, mod_row, wts["ssd_norm_g"],
                wts["w_o_mla"], wts["w_o_ssd"], wts["w_out"])
    y = _ffn(x1, mod48, mod_row, wts["norm_ffn_g"], wts["w_up"], wts["ffn_conv_w"], wts["ffn_conv_b"], wts["w_down"],
             wts["final_norm_g"], l)
    return y.reshape(b, l, D_MODEL), ckv_n, kr3, h_t


def kernel(x_prompt, x_sample, c, cache_ckv, cache_krope, state_ssd, c_ctx, w_ada, b_ada, norm_attn_g, w_in, q_norm_g,
           kv_norm_g, w_uq, w_ukv, w_o_mla, ssd_conv_w, ssd_conv_b, ssd_dt_bias, ssd_A_log, ssd_D, ssd_norm_g, w_o_ssd,
           w_out, norm_ffn_g, w_up, ffn_conv_w, ffn_conv_b, w_down, final_norm_g):
    depth = w_in.shape[0]
    assert depth == 1, "single trunk layer"
    dec_b = x_sample.shape[0]
    assert x_sample.shape[1] == ROW_GROUP and ROW_GROUP % x_prompt.shape[1] == 0
    lyr = 0

    cvec = jnp.zeros((8, D_MODEL), F32).at[0].set(c_ctx).at[1:1 + dec_b].set(c)
    mod48 = _ada(cvec, w_ada[lyr], b_ada[lyr]).reshape(8 * 6, 1, D_MODEL)

    wi = w_in[lyr]
    o = np.cumsum((0, Q_LORA, KV_LORA, QK_ROPE, SSD_INNER, SSD_INNER, SSD_GROUPS * SSD_STATE,
                   SSD_GROUPS * SSD_STATE, 2 * SSD_HEADS, D_MODEL, D_MODEL))
    piece = lambda i: wi[:, o[i]:o[i + 1]]
    w_kr = piece(2)
    zcols = lambda n: jnp.zeros((D_MODEL, n), F32)
    misc_pad = IN_TILE - 256 - 2 * SSD_HEADS
    w_in_r = jnp.concatenate([piece(3), piece(8), piece(9), piece(5), piece(6),
                              piece(4), piece(0), piece(1),
                              w_kr, zcols(QK_ROPE), _swap_rope_halves(w_kr), zcols(QK_ROPE), piece(7),
                              zcols(misc_pad)], axis=1).astype(BF16)
    wq = w_uq[lyr].reshape(Q_LORA, N_HEADS, QK_NOPE + QK_ROPE)
    wq_nope = wq[:, :, :QK_NOPE].reshape(Q_LORA, -1)
    wq_rope = wq[:, :, QK_NOPE:]
    pad_rope = lambda w: jnp.pad(w, ((0, 0), (0, 0), (0, 128 - QK_ROPE))).reshape(Q_LORA, -1)
    w_uq_ctx = jnp.concatenate([wq_nope, pad_rope(wq_rope)], axis=1).astype(BF16)
    w_uq_lat = jnp.concatenate([wq_nope, pad_rope(wq_rope), pad_rope(_swap_rope_halves(wq_rope))], axis=1).astype(BF16)
    wkv = w_ukv[lyr].reshape(KV_LORA, N_HEADS, QK_NOPE + V_HEAD)
    w_ukv_r = jnp.concatenate([wkv[:, :, :QK_NOPE].reshape(KV_LORA, -1),
                               wkv[:, :, QK_NOPE:].reshape(KV_LORA, -1)], axis=1).astype(BF16)
    pad128 = lambda a: jnp.pad(a.reshape(1, -1), ((0, 0), (0, 128 - a.size)))
    wts = {
        "norm_attn_g": norm_attn_g[lyr].reshape(1, -1), "w_in_r": w_in_r,
        "ssd_conv_w": ssd_conv_w[lyr], "ssd_conv_b": ssd_conv_b[lyr].reshape(1, -1),
        "q_norm_g": q_norm_g[lyr].reshape(1, -1), "kv_norm_g": kv_norm_g[lyr].reshape(1, -1),
        "w_uq_ctx": w_uq_ctx, "w_uq_lat": w_uq_lat, "w_ukv_r": w_ukv_r,
        "dt_bias128": pad128(ssd_dt_bias[lyr]), "a128": pad128(-jnp.exp(ssd_A_log[lyr])),
        "d_exp": jnp.repeat(ssd_D[lyr], SSD_HEADDIM).reshape(1, -1),
        "ssd_norm_g": ssd_norm_g[lyr].reshape(1, -1),
        "w_o_mla": w_o_mla[lyr], "w_o_ssd": w_o_ssd[lyr], "w_out": w_out[lyr],
        "norm_ffn_g": norm_ffn_g[lyr].reshape(1, -1), "w_up": w_up[lyr],
        "ffn_conv_w": ffn_conv_w[lyr], "ffn_conv_b": ffn_conv_b[lyr].reshape(1, -1),
        "w_down": w_down[lyr], "final_norm_g": final_norm_g.reshape(1, -1),
    }

    y_p, ckv_p, kr_p, st_p = _trunk_pass(x_prompt, mod48, lambda r: 0, wts, None, False)
    ctx = (cache_ckv[:, lyr], cache_krope[:, lyr], state_ssd[:, lyr])
    y_s, _, _, _ = _trunk_pass(x_sample, mod48, lambda r: 1 + r, wts, ctx, True)
    return y_p, y_s, ckv_p[:, None], kr_p[:, None], st_p[:, None]
```

```python
import functools
import math

import jax
import jax.numpy as jnp
import numpy as np
from jax import lax
from jax.experimental import pallas as pl
from jax.experimental.pallas import tpu as pltpu

F32 = jnp.float32
BF16 = jnp.bfloat16

D_MODEL = 1024
GRID_W = 64
N_HEADS = 8
QK_NOPE = 128
QK_ROPE = 64
V_HEAD = 128
Q_LORA = 256
KV_LORA = 256
ROPE_BASE = 10000.0
SSD_HEADS = 16
SSD_HEADDIM = 64
SSD_INNER = SSD_HEADS * SSD_HEADDIM
SSD_GROUPS = 4
SSD_STATE = 64
SSD_CHUNK = 128
D_FF = 2816
EPS = 1e-6

ROW_GROUP = 2048
IN_TILE = 512
SSD_CHUNKS_PER_STEP = 2
FFN_TILE = 256
TOKEN_TILE = 512
ATTN_Q_TILE = 256
CONV_CHUNK = 256
CONV_PAD = 8
VMEM_LIMIT = 56 * 1024 * 1024
NEG_BIG = -1e30


def _sigmoid(x):
    return 1.0 / (1.0 + jnp.exp(-x))


def _silu(x):
    return x * _sigmoid(x)


def _softplus(x):
    e = jnp.exp(-jnp.abs(x))
    u = 1.0 + e
    log1p_e = jnp.where(u == 1.0, e, e * jnp.log(u) / jnp.where(u == 1.0, 1.0, u - 1.0))
    return jnp.maximum(x, 0.0) + log1p_e


def _rmsnorm(x, g):
    return x * lax.rsqrt(jnp.mean(x * x, axis=-1, keepdims=True) + EPS) * g


def _dot(a, b):
    return jnp.dot(a, b, preferred_element_type=F32)


def _dot_nt(a, b):
    return lax.dot_general(a, b, (((1,), (1,)), ((), ())), preferred_element_type=F32)


def _params(*sem):
    return pltpu.CompilerParams(dimension_semantics=sem, vmem_limit_bytes=VMEM_LIMIT)


def _norm_mod_rows(x_ref, g_ref, sc_ref, sh_ref, h_scr):
    def body(i, carry):
        rows = pl.ds(pl.multiple_of(i * CONV_CHUNK, CONV_CHUNK), CONV_CHUNK)
        h = _rmsnorm(x_ref[rows, :], g_ref[...]) * (1.0 + sc_ref[...]) + sh_ref[...]
        h_scr[rows, :] = h.astype(BF16)
        return carry
    lax.fori_loop(0, ROW_GROUP // CONV_CHUNK, body, 0)


def _zero_conv_pads(u_scr):
    zeros = jnp.zeros((CONV_PAD, u_scr.shape[1]), F32)
    u_scr[0:CONV_PAD, :] = zeros
    u_scr[CONV_PAD + ROW_GROUP:2 * CONV_PAD + ROW_GROUP, :] = zeros


def _stage_rows(c):
    return slice(CONV_PAD + c * CONV_CHUNK, CONV_PAD + (c + 1) * CONV_CHUNK)


def _dwconv3_rows(u_scr, c, cols, w_ref, b_ref, seq_len):
    r0 = c * CONV_CHUNK
    base = CONV_PAD + r0
    width = cols.stop - cols.start
    prev = u_scr[base - 1:base - 1 + CONV_CHUNK, cols]
    cur = u_scr[base:base + CONV_CHUNK, cols]
    nxt = u_scr[base + 1:base + 1 + CONV_CHUNK, cols]
    row = lax.broadcasted_iota(jnp.int32, (CONV_CHUNK, width), 0)
    if r0 % seq_len == 0:
        prev = jnp.where(row == 0, 0.0, prev)
    if (r0 + CONV_CHUNK) % seq_len == 0:
        nxt = jnp.where(row == CONV_CHUNK - 1, 0.0, nxt)
    return prev * w_ref[0:1, :] + cur * w_ref[1:2, :] + nxt * w_ref[2:3, :] + b_ref[...]


def _ada_kernel(c_ref, w_ref, b_ref, o_ref):
    a = _silu(c_ref[...]).astype(BF16)
    o_ref[...] = _dot(a, w_ref[...].astype(BF16)) + b_ref[...]


def _ada(cvec, w_ada, b_ada):
    tn = 1536
    return pl.pallas_call(
        _ada_kernel,
        grid=(6 * D_MODEL // tn,),
        in_specs=[pl.BlockSpec((8, D_MODEL), lambda j: (0, 0)),
                  pl.BlockSpec((D_MODEL, tn), lambda j: (0, j)),
                  pl.BlockSpec((1, tn), lambda j: (0, j))],
        out_specs=pl.BlockSpec((8, tn), lambda j: (0, j)),
        out_shape=jax.ShapeDtypeStruct((8, 6 * D_MODEL), F32),
        compiler_params=_params("arbitrary"),
        name="ada_mod",
    )(cvec, w_ada, b_ada.reshape(1, -1))


IN_SPLITS = (Q_LORA, KV_LORA, QK_ROPE, SSD_INNER, SSD_INNER, SSD_GROUPS * SSD_STATE, SSD_GROUPS * SSD_STATE,
             2 * SSD_HEADS, D_MODEL, D_MODEL)
IN_OFFSETS = tuple(int(v) for v in np.cumsum((0,) + IN_SPLITS))


def _regroup_kernel(w_ref, o_ref):
    dst = 0

    def put(src, width):
        nonlocal dst
        for off in range(0, width, IN_TILE):
            n = min(IN_TILE, width - off)
            o_ref[:, dst + off:dst + off + n] = w_ref[:, src + off:src + off + n].astype(BF16)
        dst += width

    def zeros(width):
        nonlocal dst
        o_ref[:, dst:dst + width] = jnp.zeros((o_ref.shape[0], width), BF16)
        dst += width

    piece = lambda i: put(IN_OFFSETS[i], IN_SPLITS[i])
    for i in (3, 8, 9, 5, 6, 4, 0, 1):
        piece(i)
    kr0 = IN_OFFSETS[2]
    piece(2)
    zeros(QK_ROPE)
    n = QK_ROPE // 4
    for blk in (1, 0, 3, 2):
        put(kr0 + blk * n, n)
    zeros(QK_ROPE)
    piece(7)
    zeros(o_ref.shape[1] - dst)


def _regroup_w_in(w_in2d):
    rows = 256
    n_out = (N_LOW_TILES + N_F32_TILES) * IN_TILE
    return pl.pallas_call(
        _regroup_kernel,
        grid=(D_MODEL // rows,),
        in_specs=[pl.BlockSpec((rows, w_in2d.shape[1]), lambda i: (i, 0))],
        out_specs=pl.BlockSpec((rows, n_out), lambda i: (i, 0)),
        out_shape=jax.ShapeDtypeStruct((D_MODEL, n_out), BF16),
        compiler_params=_params("arbitrary"),
        name="w_in_regroup",
    )(w_in2d)


N_LOW_TILES, N_F32_TILES = 7, 4
Z_BLK, GM_BLK, GS_BLK = 0, 1, 2
BC_TILE = 6
XS_BLK = 0
MLA_TILE, MISC_TILE = 2, 3
DT_BLK = (MISC_TILE * IN_TILE + 256) // 128
ATTN_HEAD_COLS = 256


def _in_kernel(x_ref, sh_ref, sc_ref, g_ref, w_ref, cw_ref, cb_ref, lo_ref, hi_ref, h_scr, u_scr, *, seq_len):
    j = pl.program_id(1)
    n_chunks = ROW_GROUP // CONV_CHUNK
    rows = lambda c: slice(c * CONV_CHUNK, (c + 1) * CONV_CHUNK)

    @pl.when(j == 0)
    def _():
        _norm_mod_rows(x_ref, g_ref, sc_ref, sh_ref, h_scr)
        _zero_conv_pads(u_scr)

    def pointwise(fn, o_ref):
        for c in range(n_chunks):
            o_ref[rows(c), :] = fn(_dot(h_scr[rows(c), :], w_ref[...])).astype(o_ref.dtype)

    def conv(o_ref):
        def conv_out(c):
            v = _silu(_dwconv3_rows(u_scr, c, slice(0, IN_TILE), cw_ref, cb_ref, seq_len))
            o_ref[rows(c), :] = v.astype(o_ref.dtype)
        for c in range(n_chunks):
            u_scr[_stage_rows(c), :] = _dot(h_scr[rows(c), :], w_ref[...])
            if c >= 1:
                conv_out(c - 1)
        conv_out(n_chunks - 1)

    @pl.when(j <= 1)
    def _():
        pointwise(_silu, lo_ref)

    @pl.when((j >= 2) & (j <= 5))
    def _():
        pointwise(_sigmoid, lo_ref)

    @pl.when(j == BC_TILE)
    def _():
        conv(lo_ref)

    @pl.when((j == N_LOW_TILES) | (j == N_LOW_TILES + 1))
    def _():
        conv(hi_ref)

    @pl.when(j >= N_LOW_TILES + MLA_TILE)
    def _():
        pointwise(lambda u: u, hi_ref)


def _in_proj(x2d, mod48, mod_row, norm_g, w_in_r, conv_w, conv_b, seq_len):
    t = x2d.shape[0]
    n_tiles = N_LOW_TILES + N_F32_TILES
    conv_idx = lambda r, j: (0, jnp.where(j == BC_TILE, 2, jnp.clip(j - N_LOW_TILES, 0, 1)))
    return pl.pallas_call(
        functools.partial(_in_kernel, seq_len=seq_len),
        grid=(t // ROW_GROUP, n_tiles),
        in_specs=[pl.BlockSpec((ROW_GROUP, D_MODEL), lambda r, j: (r, 0)),
                  pl.BlockSpec((None, 1, D_MODEL), lambda r, j: (mod_row(r) * 6 + 0, 0, 0)),
                  pl.BlockSpec((None, 1, D_MODEL), lambda r, j: (mod_row(r) * 6 + 1, 0, 0)),
                  pl.BlockSpec((1, D_MODEL), lambda r, j: (0, 0)),
                  pl.BlockSpec((D_MODEL, IN_TILE), lambda r, j: (0, j)),
                  pl.BlockSpec((3, IN_TILE), conv_idx),
                  pl.BlockSpec((1, IN_TILE), conv_idx)],
        out_specs=[pl.BlockSpec((ROW_GROUP, IN_TILE), lambda r, j: (r, jnp.minimum(j, N_LOW_TILES - 1))),
                   pl.BlockSpec((ROW_GROUP, IN_TILE), lambda r, j: (r, jnp.maximum(j - N_LOW_TILES, 0)))],
        out_shape=[jax.ShapeDtypeStruct((t, N_LOW_TILES * IN_TILE), BF16),
                   jax.ShapeDtypeStruct((t, N_F32_TILES * IN_TILE), F32)],
        scratch_shapes=[pltpu.VMEM((ROW_GROUP, D_MODEL), BF16),
                        pltpu.VMEM((ROW_GROUP + 2 * CONV_PAD, IN_TILE), F32)],
        compiler_params=_params("arbitrary", "arbitrary"),
        name="in_proj",
    )(x2d, mod48, mod48, norm_g, w_in_r, conv_w, conv_b)


def _mla_kernel(*refs, latent, has_ctx, seq_len):
    refs = list(refs)
    pm_ref, px_ref, qg_ref, kvg_ref, wuq_ref, wukv_ref = refs[:6]
    del refs[:6]
    if latent:
        cos_ref, sin_ref = refs[:2]
        del refs[:2]
    if has_ctx:
        cckv_ref, ckr_ref = refs[:2]
        del refs[:2]
    o_ref, ckv_ref, kr_ref, k_scr, v_scr = refs[:5]
    del refs[:5]
    if has_ctx:
        kc_scr, vc_scr = refs
    t = pl.program_id(1)
    n_nope = N_HEADS * QK_NOPE
    tq = ATTN_Q_TILE

    def put_keys(k_dst, v_dst, rows, ckv_n, kr128):
        kv = _dot(ckv_n.astype(BF16), wukv_ref[...])
        v_dst[rows, :] = kv[:, n_nope:].astype(BF16)
        kr_bf = kr128.astype(BF16)
        for h in range(N_HEADS):
            base = h * ATTN_HEAD_COLS
            k_dst[rows, base:base + QK_NOPE] = kv[:, h * QK_NOPE:(h + 1) * QK_NOPE].astype(BF16)
            k_dst[rows, base + QK_NOPE:base + ATTN_HEAD_COLS] = kr_bf

    @pl.when(t == 0)
    def _():
        step = min(seq_len, TOKEN_TILE)
        for r0 in range(0, seq_len, step):
            rows = slice(r0, r0 + step)
            ckv_n = _rmsnorm(pm_ref[0, rows, Q_LORA:], kvg_ref[...])
            ckv_ref[0, rows, :] = ckv_n
            kr = px_ref[0, rows, 0:128]
            if latent:
                kr = kr * cos_ref[rows, :] + px_ref[0, rows, 128:256] * sin_ref[rows, :]
            kr_ref[0, rows, :] = kr[:, :QK_ROPE]
            put_keys(k_scr, v_scr, rows, ckv_n, kr)
        if has_ctx:
            past = cckv_ref.shape[1]
            put_keys(kc_scr, vc_scr, slice(0, past), cckv_ref[0], jnp.zeros((past, 128), F32))
            ckr_bf = ckr_ref[0].astype(BF16)
            for h in range(N_HEADS):
                base = h * ATTN_HEAD_COLS + QK_NOPE
                kc_scr[:, base:base + QK_ROPE] = ckr_bf

    qrows = pl.ds(pl.multiple_of(t * tq, tq), tq)
    scale = 1.0 / math.sqrt(QK_NOPE + QK_ROPE)
    cqn = _rmsnorm(pm_ref[0, qrows, 0:Q_LORA], qg_ref[...]).astype(BF16)
    q = _dot(cqn, wuq_ref[...])
    q_rope = q[:, n_nope:2 * n_nope]
    if latent:
        q_rope = (q_rope * jnp.concatenate([cos_ref[qrows, :]] * N_HEADS, axis=1)
                  + q[:, 2 * n_nope:3 * n_nope] * jnp.concatenate([sin_ref[qrows, :]] * N_HEADS, axis=1))
    segs = ([(kc_scr, vc_scr)] if has_ctx else []) + [(k_scr, v_scr)]
    for h in range(N_HEADS):
        head = slice(h * QK_NOPE, (h + 1) * QK_NOPE)
        qk = slice(h * ATTN_HEAD_COLS, (h + 1) * ATTN_HEAD_COLS)
        qh = (jnp.concatenate([q[:, head], q_rope[:, head]], axis=1) * scale).astype(BF16)
        s = [_dot_nt(qh, k[:, qk]) for k, _ in segs]
        m = functools.reduce(jnp.maximum, [jnp.max(si, axis=-1, keepdims=True) for si in s])
        p = [jnp.exp(si - m) for si in s]
        l = functools.reduce(jnp.add, [jnp.sum(pi, axis=-1, keepdims=True) for pi in p])
        o = functools.reduce(jnp.add, [_dot(pi.astype(BF16), v[:, head]) for pi, (_, v) in zip(p, segs)])
        o_ref[0, :, head] = (o / l).astype(BF16)


def _mla(proj_hi, q_norm_g, kv_norm_g, w_uq_r, w_ukv_r, rope_tables, ctx):
    b, l, _ = proj_hi.shape
    tq = ATTN_Q_TILE
    latent = rope_tables is not None
    has_ctx = ctx is not None
    once = dict(pipeline_mode=pl.Buffered(1))
    per_batch = once if l // tq > 1 else {}
    const2 = lambda i, t: (0, 0)
    in_specs = [pl.BlockSpec((1, l, IN_TILE), lambda i, t: (i, 0, MLA_TILE), **per_batch),
                pl.BlockSpec((1, l, 256), lambda i, t: (i, 0, MISC_TILE * IN_TILE // 256), **per_batch),
                pl.BlockSpec((1, Q_LORA), const2),
                pl.BlockSpec((1, KV_LORA), const2),
                pl.BlockSpec(w_uq_r.shape, const2, **once),
                pl.BlockSpec(w_ukv_r.shape, const2, **once)]
    args = [proj_hi, proj_hi, q_norm_g, kv_norm_g, w_uq_r, w_ukv_r]
    scratch = [pltpu.VMEM((l, N_HEADS * ATTN_HEAD_COLS), BF16), pltpu.VMEM((l, N_HEADS * V_HEAD), BF16)]
    if latent:
        in_specs += [pl.BlockSpec((l, 128), const2, **once)] * 2
        args += list(rope_tables)
    if has_ctx:
        past = ctx[0].shape[1]
        in_specs += [pl.BlockSpec((1, past, KV_LORA), lambda i, t: (i, 0, 0), **per_batch),
                     pl.BlockSpec((1, past, QK_ROPE), lambda i, t: (i, 0, 0), **per_batch)]
        args += list(ctx)
        scratch += [pltpu.VMEM((past, N_HEADS * ATTN_HEAD_COLS), BF16), pltpu.VMEM((past, N_HEADS * V_HEAD), BF16)]
    return pl.pallas_call(
        functools.partial(_mla_kernel, latent=latent, has_ctx=has_ctx, seq_len=l),
        grid=(b, l // tq),
        in_specs=in_specs,
        out_specs=[pl.BlockSpec((1, tq, N_HEADS * V_HEAD), lambda i, t: (i, t, 0)),
                   pl.BlockSpec((1, l, KV_LORA), lambda i, t: (i, 0, 0)),
                   pl.BlockSpec((1, l, QK_ROPE), lambda i, t: (i, 0, 0))],
        out_shape=[jax.ShapeDtypeStruct((b, l, N_HEADS * V_HEAD), BF16),
                   jax.ShapeDtypeStruct((b, l, KV_LORA), F32),
                   jax.ShapeDtypeStruct((b, l, QK_ROPE), F32)],
        scratch_shapes=scratch,
        compiler_params=_params("arbitrary", "arbitrary"),
        name="mla_attention",
    )(*args)


def _split3(x):
    hi = x.astype(BF16)
    r = x - hi.astype(F32)
    mid = r.astype(BF16)
    lo = (r - mid.astype(F32)).astype(BF16)
    return hi, mid, lo


def _exact_dot(parts, sel):
    return functools.reduce(jnp.add, [_dot(p, sel) for p in parts])


def _exact_dot_rows(sel, parts):
    return functools.reduce(jnp.add, [_dot(sel, p) for p in parts])


HEADS_PER_GROUP = SSD_HEADS // SSD_GROUPS
GROUP_COLS = HEADS_PER_GROUP * SSD_HEADDIM


def _ssd_kernel(*refs, nc, cps, has_h0):
    if has_h0:
        (xs_ref, zs_ref, bc_ref, dt_ref, dtb_ref, a_ref, dx_ref, h0_ref, y_ref, ht_ref,
         yl_scr, acum_scr, acumt_scr, dtt_scr, tot_scr, sb_scr, h_scr, esel_scr) = refs
    else:
        (xs_ref, zs_ref, bc_ref, dt_ref, dtb_ref, a_ref, dx_ref, y_ref, ht_ref,
         yl_scr, acum_scr, acumt_scr, dtt_scr, tot_scr, sb_scr, h_scr, esel_scr) = refs
    s = pl.program_id(1)
    q = SSD_CHUNK
    n_bc = SSD_GROUPS * SSD_STATE
    lane = lax.broadcasted_iota(jnp.int32, (q, 128), 1)
    low_half = lane < SSD_HEADDIM
    ii = lax.broadcasted_iota(jnp.int32, (q, q), 0)
    jj = lax.broadcasted_iota(jnp.int32, (q, q), 1)
    lower, upper = ii >= jj, ii <= jj

    def lane_bcast(parts, d):
        return _exact_dot(parts, esel_scr[d])

    def stacked_states(d):
        return [jnp.concatenate([h_scr[d, 2 * i], h_scr[d, 2 * i + 1]], axis=0).astype(BF16)
                for i in range(SSD_GROUPS // 2)]

    def group_c(bc, g):
        i, r = divmod(g, 2)
        cpair = bc[:, n_bc + i * 128:n_bc + (i + 1) * 128]
        return jnp.where(low_half if r == 0 else ~low_half, cpair, 0.0).astype(BF16)

    @pl.when(s == 0)
    def _():
        k = lax.broadcasted_iota(jnp.int32, (128, SSD_INNER), 0)
        head = lax.broadcasted_iota(jnp.int32, (128, SSD_INNER), 1) // SSD_HEADDIM
        for d in range(2):
            esel_scr[d] = jnp.where(k == d * SSD_HEADS + head, 1.0, 0.0).astype(BF16)
        if has_h0:
            for d in range(2):
                for g in range(SSD_GROUPS):
                    hpn = h0_ref[0, d, g * HEADS_PER_GROUP:(g + 1) * HEADS_PER_GROUP].reshape(GROUP_COLS, SSD_STATE)
                    h_scr[d, g] = hpn.T
        else:
            h_scr[...] = jnp.zeros(h_scr.shape, F32)
        tri_f = jnp.where(lower, 1.0, 0.0).astype(BF16)
        tri_b = jnp.where(upper, 1.0, 0.0).astype(BF16)
        fwd_col = lane < SSD_HEADS
        for c in range(nc):
            crow = slice(c * q, (c + 1) * q)
            dt = _softplus(dt_ref[0, crow, :] + dtb_ref[...])
            parts = _split3(dt * a_ref[...])
            acum = jnp.where(fwd_col, _exact_dot_rows(tri_f, parts),
                             _exact_dot_rows(tri_b, parts))
            acum_scr[crow, :] = acum
            tot = jnp.where(fwd_col[0:1], acum[q - 1:q, :], acum[0:1, :])
            tot_scr[c] = jnp.broadcast_to(tot, (8, 128))
            acumt_scr[c] = acum.T
            dtt_scr[c] = dt.T

    def first_sweep(c, blk):
        rows = pl.ds(pl.multiple_of(c * q, q), q)
        x = xs_ref[0, blk, :]
        bc = bc_ref[0, blk, :].astype(F32)
        acum = acum_scr[rows, :]
        acum_t, dt_t = acumt_scr[c], dtt_scr[c]
        tot8 = tot_scr[c]
        e = jnp.exp(acum)
        e_hi = e.astype(BF16)
        eb_f = lane_bcast([e_hi, (e - e_hi.astype(F32)).astype(BF16)], 0)
        cd_f = jnp.exp(lane_bcast(_split3(tot8), 0))[0:1]
        b_t = [bc[:, i * 128:(i + 1) * 128].T for i in range(SSD_GROUPS // 2)]
        h_in = stacked_states(0)
        for g in range(SSD_GROUPS):
            i, r = divmod(g, 2)
            cm = group_c(bc, g)
            scores = _dot_nt(cm, bc[:, i * 128:(i + 1) * 128].astype(BF16))
            bg_t = b_t[i][r * SSD_STATE:(r + 1) * SSD_STATE, :]
            gcols = slice(g * GROUP_COLS, (g + 1) * GROUP_COLS)
            y_off = _dot(cm, h_in[i]) * eb_f[:, gcols]
            for t in range(HEADS_PER_GROUP // 2):
                pc = slice((2 * g + t) * 128, (2 * g + t + 1) * 128)
                xp = x[:, pc]
                x2 = jnp.concatenate([jnp.where(low_half, xp, 0.0), jnp.where(low_half, 0.0, xp)], axis=0).astype(BF16)
                m, sf, sb = [], [], []
                for u in range(2):
                    kf = g * HEADS_PER_GROUP + 2 * t + u
                    kb = SSD_HEADS + kf
                    af_col = jnp.broadcast_to(acum[:, kf:kf + 1], (q, q))
                    ab_col = jnp.broadcast_to(acum[:, kb:kb + 1], (q, q))
                    af_row, ab_row = acum_t[kf:kf + 1, :], acum_t[kb:kb + 1, :]
                    dtf_row, dtb_row = dt_t[kf:kf + 1, :], dt_t[kb:kb + 1, :]
                    decay = (jnp.exp(jnp.where(lower, af_col - af_row, NEG_BIG)) * dtf_row
                             + jnp.exp(jnp.where(upper, ab_col - ab_row, NEG_BIG)) * dtb_row)
                    m.append((scores * decay).astype(BF16))
                    wf = jnp.exp(af_row[:, q - 1:q] - af_row) * dtf_row
                    wb = jnp.exp(ab_row[:, 0:1] - ab_row) * dtb_row
                    sf.append((bg_t * wf).astype(BF16))
                    sb.append((bg_t * wb).astype(BF16))
                y_pair = _dot(jnp.concatenate(m, axis=1), x2)
                tc = slice(t * 128, (t + 1) * 128)
                yl_scr[rows, pc] = y_pair + y_off[:, tc] + dx_ref[:, pc] * xp
                h_scr[0, g, :, tc] = h_scr[0, g, :, tc] * cd_f[:, pc] + _dot(jnp.concatenate(sf, axis=1), x2)
                sb_scr[c, g, :, tc] = _dot(jnp.concatenate(sb, axis=1), x2)

    def second_sweep(c, blk):
        rows = pl.ds(pl.multiple_of(c * q, q), q)
        bc = bc_ref[0, blk, :].astype(F32)
        e = jnp.exp(acum_scr[rows, :])
        e_hi = e.astype(BF16)
        eb_b = lane_bcast([e_hi, (e - e_hi.astype(F32)).astype(BF16)], 1)
        cd_b = jnp.exp(lane_bcast(_split3(tot_scr[c]), 1))[0:1]
        h_in = stacked_states(1)
        for g in range(SSD_GROUPS):
            gcols = slice(g * GROUP_COLS, (g + 1) * GROUP_COLS)
            y_off = _dot(group_c(bc, g), h_in[g // 2]) * eb_b[:, gcols]
            y_ref[0, blk, gcols] = ((yl_scr[rows, gcols] + y_off) * zs_ref[0, blk, gcols].astype(F32)).astype(y_ref.dtype)
            h_scr[1, g] = h_scr[1, g] * cd_b[:, gcols] + sb_scr[c, g]

    n_steps = nc // cps

    @pl.when(s < n_steps)
    def _():
        for ci in range(cps):
            first_sweep(s * cps + ci, slice(ci * q, (ci + 1) * q))

    @pl.when(s >= n_steps)
    def _():
        for ci in reversed(range(cps)):
            second_sweep((2 * n_steps - 1 - s) * cps + ci, slice(ci * q, (ci + 1) * q))

    @pl.when(s == 2 * n_steps - 1)
    def _():
        for d in range(2):
            for g in range(SSD_GROUPS):
                ht_ref[0, d, g * HEADS_PER_GROUP:(g + 1) * HEADS_PER_GROUP] = h_scr[d, g].T.reshape(
                    HEADS_PER_GROUP, SSD_HEADDIM, SSD_STATE)


def _ssd(proj_lo, proj_hi, h0, dt_bias128, a128, d_exp):
    b, l, _ = proj_hi.shape
    q = SSD_CHUNK
    nc = l // q
    cps = SSD_CHUNKS_PER_STEP
    n_steps = nc // cps
    rows = cps * q
    early = lambda s: jnp.minimum(s, n_steps - 1)
    both = lambda s: jnp.where(s < n_steps, s, 2 * n_steps - 1 - s)
    late = lambda s: jnp.where(s < n_steps, n_steps - 1, 2 * n_steps - 1 - s)
    st_shape = (1, 2, SSD_HEADS, SSD_HEADDIM, SSD_STATE)
    st_spec = pl.BlockSpec(st_shape, lambda i, s: (i, 0, 0, 0, 0))
    has_h0 = h0 is not None
    return pl.pallas_call(
        functools.partial(_ssd_kernel, nc=nc, cps=cps, has_h0=has_h0),
        grid=(b, 2 * n_steps),
        in_specs=[pl.BlockSpec((1, rows, SSD_INNER), lambda i, s: (i, early(s), XS_BLK)),
                  pl.BlockSpec((1, rows, SSD_INNER), lambda i, s: (i, late(s), Z_BLK)),
                  pl.BlockSpec((1, rows, IN_TILE), lambda i, s: (i, both(s), BC_TILE)),
                  pl.BlockSpec((1, l, 128), lambda i, s: (i, 0, DT_BLK)),
                  pl.BlockSpec((1, 128), lambda i, s: (0, 0)),
                  pl.BlockSpec((1, 128), lambda i, s: (0, 0)),
                  pl.BlockSpec((1, SSD_INNER), lambda i, s: (0, 0))] + ([st_spec] if has_h0 else []),
        out_specs=[pl.BlockSpec((1, rows, SSD_INNER), lambda i, s: (i, late(s), 0)),
                   pl.BlockSpec(st_shape, lambda i, s: (i, 0, 0, 0, 0))],
        out_shape=[jax.ShapeDtypeStruct((b, l, SSD_INNER), BF16),
                   jax.ShapeDtypeStruct((b,) + st_shape[1:], F32)],
        scratch_shapes=[pltpu.VMEM((l, SSD_INNER), F32),
                        pltpu.VMEM((l, 128), F32),
                        pltpu.VMEM((nc, 128, q), F32),
                        pltpu.VMEM((nc, 128, q), F32),
                        pltpu.VMEM((nc, 8, 128), F32),
                        pltpu.VMEM((nc, SSD_GROUPS, SSD_STATE, GROUP_COLS), F32),
                        pltpu.VMEM((2, SSD_GROUPS, SSD_STATE, GROUP_COLS), F32),
                        pltpu.VMEM((2, 128, SSD_INNER), BF16)],
        compiler_params=_params("arbitrary", "arbitrary"),
        name="ssd_scan",
    )(proj_hi, proj_lo, proj_lo, proj_hi, dt_bias128, a128, d_exp, *([h0] if has_h0 else []))


def _merge_kernel(attn_ref, yz_ref, gm_ref, gs_ref, x_ref, g1_ref, ng_ref, womla_ref, wossd_ref, wout_ref, o_ref, w_scr):
    @pl.when(pl.program_id(0) == 0)
    def _():
        w_scr[0] = womla_ref[...].astype(BF16)
        w_scr[1] = wossd_ref[...].astype(BF16)
        w_scr[2] = wout_ref[...].astype(BF16)

    o_mla = _dot(attn_ref[...], w_scr[0])
    o_ssd = _dot(_rmsnorm(yz_ref[...].astype(F32), ng_ref[...]).astype(BF16), w_scr[1])
    merged = gm_ref[...].astype(F32) * o_mla + gs_ref[...].astype(F32) * o_ssd
    o_ref[...] = x_ref[...] + g1_ref[...] * _dot(merged.astype(BF16), w_scr[2])


def _merge(attn2d, yz2d, proj, x2d, mod48, mod_row, ssd_norm_g, w_o_mla, w_o_ssd, w_out):
    t = x2d.shape[0]
    tm = TOKEN_TILE
    row = lambda i: (i, 0)
    const = lambda i: (0, 0)
    wspec = pl.BlockSpec((D_MODEL, D_MODEL), const, pipeline_mode=pl.Buffered(1))
    return pl.pallas_call(
        _merge_kernel,
        grid=(t // tm,),
        in_specs=[pl.BlockSpec((tm, D_MODEL), row),
                  pl.BlockSpec((tm, D_MODEL), row),
                  pl.BlockSpec((tm, D_MODEL), lambda i: (i, GM_BLK)),
                  pl.BlockSpec((tm, D_MODEL), lambda i: (i, GS_BLK)),
                  pl.BlockSpec((tm, D_MODEL), row),
                  pl.BlockSpec((None, 1, D_MODEL), lambda i: (mod_row(i * tm // ROW_GROUP) * 6 + 2, 0, 0)),
                  pl.BlockSpec((1, D_MODEL), const),
                  wspec, wspec, wspec],
        out_specs=pl.BlockSpec((tm, D_MODEL), row),
        out_shape=jax.ShapeDtypeStruct((t, D_MODEL), F32),
        scratch_shapes=[pltpu.VMEM((3, D_MODEL, D_MODEL), BF16)],
        compiler_params=_params("arbitrary"),
        name="merge_out",
    )(attn2d, yz2d, proj, proj, x2d, mod48, ssd_norm_g, w_o_mla, w_o_ssd, w_out)


def _ffn_kernel(x_ref, sh_ref, sc_ref, g2_ref, ng_ref, wg_ref, wv_ref, cwg_ref, cwv_ref, cbg_ref, cbv_ref, wd_ref,
                fg_ref, o_ref, h_scr, wup_scr, wd_scr, u_scr, *, seq_len):
    j = pl.program_id(1)
    n_chunks = ROW_GROUP // CONV_CHUNK
    rows = lambda c: slice(c * CONV_CHUNK, (c + 1) * CONV_CHUNK)

    @pl.when(j == 0)
    def _():
        _norm_mod_rows(x_ref, ng_ref, sc_ref, sh_ref, h_scr)
        _zero_conv_pads(u_scr)
        o_ref[...] = jnp.zeros_like(o_ref)

    wup_scr[:, 0:FFN_TILE] = wg_ref[...].astype(BF16)
    wup_scr[:, FFN_TILE:2 * FFN_TILE] = wv_ref[...].astype(BF16)
    wd_scr[...] = wd_ref[...].astype(BF16)

    def gated(c):
        ug = _dwconv3_rows(u_scr, c, slice(0, FFN_TILE), cwg_ref, cbg_ref, seq_len)
        uv = _dwconv3_rows(u_scr, c, slice(FFN_TILE, 2 * FFN_TILE), cwv_ref, cbv_ref, seq_len)
        return (_silu(ug) * uv).astype(BF16)

    act = {}
    for c in range(n_chunks + 2):
        if c < n_chunks:
            u_scr[_stage_rows(c), :] = _dot(h_scr[rows(c), :], wup_scr[...])
        if c >= 2:
            o_ref[rows(c - 2), :] += _dot(act.pop(c - 2), wd_scr[...])
        if 1 <= c <= n_chunks:
            act[c - 1] = gated(c - 1)

    @pl.when(j == pl.num_programs(1) - 1)
    def _():
        def body(i, carry):
            rows = pl.ds(pl.multiple_of(i * 256, 256), 256)
            o_ref[rows, :] = _rmsnorm(x_ref[rows, :] + g2_ref[...] * o_ref[rows, :], fg_ref[...])
            return carry
        lax.fori_loop(0, ROW_GROUP // 256, body, 0)


def _ffn(x2d, mod48, mod_row, norm_g, w_up, conv_w, conv_b, w_down, final_g, seq_len):
    t = x2d.shape[0]
    nj = D_FF // FFN_TILE
    gate = lambda r, j: (0, j)
    val = lambda r, j: (0, nj + j)
    const = lambda r, j: (0, 0)
    mod = lambda k: pl.BlockSpec((None, 1, D_MODEL), lambda r, j: (mod_row(r) * 6 + k, 0, 0))
    return pl.pallas_call(
        functools.partial(_ffn_kernel, seq_len=seq_len),
        grid=(t // ROW_GROUP, nj),
        in_specs=[pl.BlockSpec((ROW_GROUP, D_MODEL), lambda r, j: (r, 0)),
                  mod(3), mod(4), mod(5),
                  pl.BlockSpec((1, D_MODEL), const),
                  pl.BlockSpec((D_MODEL, FFN_TILE), gate),
                  pl.BlockSpec((D_MODEL, FFN_TILE), val),
                  pl.BlockSpec((3, FFN_TILE), gate),
                  pl.BlockSpec((3, FFN_TILE), val),
                  pl.BlockSpec((1, FFN_TILE), gate),
                  pl.BlockSpec((1, FFN_TILE), val),
                  pl.BlockSpec((FFN_TILE, D_MODEL), lambda r, j: (j, 0)),
                  pl.BlockSpec((1, D_MODEL), const)],
        out_specs=pl.BlockSpec((ROW_GROUP, D_MODEL), lambda r, j: (r, 0)),
        out_shape=jax.ShapeDtypeStruct((t, D_MODEL), F32),
        scratch_shapes=[pltpu.VMEM((ROW_GROUP, D_MODEL), BF16),
                        pltpu.VMEM((D_MODEL, 2 * FFN_TILE), BF16),
                        pltpu.VMEM((FFN_TILE, D_MODEL), BF16),
                        pltpu.VMEM((ROW_GROUP + 2 * CONV_PAD, 2 * FFN_TILE), F32)],
        compiler_params=_params("arbitrary", "arbitrary"),
        name="conv_ffn",
    )(x2d, mod48, mod48, mod48, norm_g, w_up, w_up, conv_w, conv_w, conv_b, conv_b, w_down, final_g)


def _rope_tables(seq_len):
    t = np.arange(seq_len)
    row = (t // GRID_W).astype(np.float32)
    col = (t % GRID_W).astype(np.float32)
    n = QK_ROPE // 4
    inv = (np.float32(ROPE_BASE) ** (-np.arange(n, dtype=np.float32) / np.float32(n))).astype(np.float32)
    ar, ac = row[:, None] * inv, col[:, None] * inv
    cos64 = np.concatenate([np.cos(ar), np.cos(ar), np.cos(ac), np.cos(ac)], axis=1)
    sin64 = np.concatenate([-np.sin(ar), np.sin(ar), -np.sin(ac), np.sin(ac)], axis=1)
    zeros = np.zeros_like(cos64)
    return (jnp.asarray(np.concatenate([cos64, zeros], axis=1), F32),
            jnp.asarray(np.concatenate([sin64, zeros], axis=1), F32))


def _swap_rope_halves(w):
    lead = w.shape[:-1]
    return w.reshape(lead + (2, 2, QK_ROPE // 4))[..., ::-1, :].reshape(lead + (QK_ROPE,))


def _trunk_pass(x, mod48, mod_row, wts, ctx, latent):
    b, l, _ = x.shape
    x2d = x.reshape(b * l, D_MODEL)
    proj_lo, proj_hi = _in_proj(x2d, mod48, mod_row, wts["norm_attn_g"], wts["w_in_r"], wts["ssd_conv_w"],
                                wts["ssd_conv_b"], l)
    rope = _rope_tables(l) if latent else None
    w_uq_r = wts["w_uq_lat"] if latent else wts["w_uq_ctx"]
    shape3 = lambda a, n: a.reshape(b, n, a.shape[-1])
    h0 = None
    mla_ctx = None
    if ctx is not None:
        cache_ckv, cache_krope, h0 = ctx
        mla_ctx = (cache_ckv, cache_krope)
    attn, ckv_n, kr3 = _mla(shape3(proj_hi, l), wts["q_norm_g"], wts["kv_norm_g"], w_uq_r, wts["w_ukv_r"], rope, mla_ctx)
    yz, h_t = _ssd(shape3(proj_lo, l), shape3(proj_hi, l), h0, wts["dt_bias128"], wts["a128"], wts["d_exp"])
    x1 = _merge(attn.reshape(b * l, -1), yz.reshape(b * l, -1), proj_lo, x2d, mod48, mod_row, wts["ssd_norm_g"],
                wts["w_o_mla"], wts["w_o_ssd"], wts["w_out"])
    y = _ffn(x1, mod48, mod_row, wts["norm_ffn_g"], wts["w_up"], wts["ffn_conv_w"], wts["ffn_conv_b"], wts["w_down"],
             wts["final_norm_g"], l)
    return y.reshape(b, l, D_MODEL), ckv_n, kr3, h_t


def kernel(x_prompt, x_sample, c, cache_ckv, cache_krope, state_ssd, c_ctx, w_ada, b_ada, norm_attn_g, w_in, q_norm_g,
           kv_norm_g, w_uq, w_ukv, w_o_mla, ssd_conv_w, ssd_conv_b, ssd_dt_bias, ssd_A_log, ssd_D, ssd_norm_g, w_o_ssd,
           w_out, norm_ffn_g, w_up, ffn_conv_w, ffn_conv_b, w_down, final_norm_g):
    depth = w_in.shape[0]
    assert depth == 1, "single trunk layer"
    dec_b = x_sample.shape[0]
    assert x_sample.shape[1] == ROW_GROUP and ROW_GROUP % x_prompt.shape[1] == 0
    lyr = 0

    cvec = jnp.zeros((8, D_MODEL), F32).at[0].set(c_ctx).at[1:1 + dec_b].set(c)
    mod48 = _ada(cvec, w_ada[lyr], b_ada[lyr]).reshape(8 * 6, 1, D_MODEL)

    w_in_r = _regroup_w_in(w_in[lyr])
    wq = w_uq[lyr].reshape(Q_LORA, N_HEADS, QK_NOPE + QK_ROPE)
    wq_nope = wq[:, :, :QK_NOPE].reshape(Q_LORA, -1)
    wq_rope = wq[:, :, QK_NOPE:]
    pad_rope = lambda w: jnp.pad(w, ((0, 0), (0, 0), (0, 128 - QK_ROPE))).reshape(Q_LORA, -1)
    w_uq_ctx = jnp.concatenate([wq_nope, pad_rope(wq_rope)], axis=1).astype(BF16)
    w_uq_lat = jnp.concatenate([wq_nope, pad_rope(wq_rope), pad_rope(_swap_rope_halves(wq_rope))], axis=1).astype(BF16)
    wkv = w_ukv[lyr].reshape(KV_LORA, N_HEADS, QK_NOPE + V_HEAD)
    w_ukv_r = jnp.concatenate([wkv[:, :, :QK_NOPE].reshape(KV_LORA, -1),
                               wkv[:, :, QK_NOPE:].reshape(KV_LORA, -1)], axis=1).astype(BF16)
    pad128 = lambda a: jnp.pad(a.reshape(1, -1), ((0, 0), (0, 128 - a.size)))
    wts = {
        "norm_attn_g": norm_attn_g[lyr].reshape(1, -1), "w_in_r": w_in_r,
        "ssd_conv_w": ssd_conv_w[lyr], "ssd_conv_b": ssd_conv_b[lyr].reshape(1, -1),
        "q_norm_g": q_norm_g[lyr].reshape(1, -1), "kv_norm_g": kv_norm_g[lyr].reshape(1, -1),
        "w_uq_ctx": w_uq_ctx, "w_uq_lat": w_uq_lat, "w_ukv_r": w_ukv_r,
        "dt_bias128": pad128(ssd_dt_bias[lyr]), "a128": pad128(-jnp.exp(ssd_A_log[lyr])),
        "d_exp": jnp.repeat(ssd_D[lyr], SSD_HEADDIM).reshape(1, -1),
        "ssd_norm_g": ssd_norm_g[lyr].reshape(1, -1),
        "w_o_mla": w_o_mla[lyr], "w_o_ssd": w_o_ssd[lyr], "w_out": w_out[lyr],
        "norm_ffn_g": norm_ffn_g[lyr].reshape(1, -1), "w_up": w_up[lyr],
        "ffn_conv_w": ffn_conv_w[lyr], "ffn_conv_b": ffn_conv_b[lyr].reshape(1, -1),
        "w_down": w_down[lyr], "final_norm_g": final_norm_g.reshape(1, -1),
    }

    y_p, ckv_p, kr_p, st_p = _trunk_pass(x_prompt, mod48, lambda r: 0, wts, None, False)
    ctx = (cache_ckv[:, lyr], cache_krope[:, lyr], state_ssd[:, lyr])
    y_s, _, _, _ = _trunk_pass(x_sample, mod48, lambda r: 1 + r, wts, ctx, True)
    return y_p, y_s, ckv_p[:, None], kr_p[:, None], st_p[:, None]
```

```python
import functools
import math

import jax
import jax.numpy as jnp
import numpy as np
from jax import lax
from jax.experimental import pallas as pl
from jax.experimental.pallas import tpu as pltpu

F32 = jnp.float32
BF16 = jnp.bfloat16

D_MODEL = 1024
GRID_W = 64
N_HEADS = 8
QK_NOPE = 128
QK_ROPE = 64
V_HEAD = 128
Q_LORA = 256
KV_LORA = 256
ROPE_BASE = 10000.0
SSD_HEADS = 16
SSD_HEADDIM = 64
SSD_INNER = SSD_HEADS * SSD_HEADDIM
SSD_GROUPS = 4
SSD_STATE = 64
SSD_CHUNK = 128
D_FF = 2816
EPS = 1e-6

ROW_GROUP = 2048
IN_TILE = 512
SSD_CHUNKS_PER_STEP = 2
FFN_TILE = 256
TOKEN_TILE = 512
ATTN_Q_TILE = 256
CONV_CHUNK = 256
CONV_PAD = 8
VMEM_LIMIT = 56 * 1024 * 1024
NEG_BIG = -1e30
LOG2_E = 1.4426950408889634


def _sigmoid(x):
    return 1.0 / (1.0 + jnp.exp(-x))


def _silu(x):
    return x * _sigmoid(x)


def _softplus(x):
    e = jnp.exp(-jnp.abs(x))
    u = 1.0 + e
    log1p_e = jnp.where(u == 1.0, e, e * jnp.log(u) / jnp.where(u == 1.0, 1.0, u - 1.0))
    return jnp.maximum(x, 0.0) + log1p_e


def _rmsnorm(x, g):
    return x * lax.rsqrt(jnp.mean(x * x, axis=-1, keepdims=True) + EPS) * g


def _dot(a, b):
    return jnp.dot(a, b, preferred_element_type=F32)


def _dot_nt(a, b):
    return lax.dot_general(a, b, (((1,), (1,)), ((), ())), preferred_element_type=F32)


def _params(*sem):
    return pltpu.CompilerParams(dimension_semantics=sem, vmem_limit_bytes=VMEM_LIMIT)


def _norm_mod_rows(x_ref, g_ref, sc_ref, sh_ref, h_scr):
    def body(i, carry):
        rows = pl.ds(pl.multiple_of(i * CONV_CHUNK, CONV_CHUNK), CONV_CHUNK)
        h = _rmsnorm(x_ref[rows, :], g_ref[...]) * (1.0 + sc_ref[...]) + sh_ref[...]
        h_scr[rows, :] = h.astype(BF16)
        return carry
    lax.fori_loop(0, ROW_GROUP // CONV_CHUNK, body, 0)


def _zero_conv_pads(u_scr):
    zeros = jnp.zeros((CONV_PAD, u_scr.shape[1]), F32)
    u_scr[0:CONV_PAD, :] = zeros
    u_scr[CONV_PAD + ROW_GROUP:2 * CONV_PAD + ROW_GROUP, :] = zeros


def _stage_rows(c):
    return slice(CONV_PAD + c * CONV_CHUNK, CONV_PAD + (c + 1) * CONV_CHUNK)


def _dwconv3_rows(u_scr, c, cols, w_ref, b_ref, seq_len):
    r0 = c * CONV_CHUNK
    base = CONV_PAD + r0
    width = cols.stop - cols.start
    prev = u_scr[base - 1:base - 1 + CONV_CHUNK, cols]
    cur = u_scr[base:base + CONV_CHUNK, cols]
    nxt = u_scr[base + 1:base + 1 + CONV_CHUNK, cols]
    pos = (lax.broadcasted_iota(jnp.int32, (CONV_CHUNK, width), 0) + r0) & (seq_len - 1)
    if r0 % seq_len == 0 or CONV_CHUNK > seq_len:
        prev = jnp.where(pos == 0, 0.0, prev)
    if (r0 + CONV_CHUNK) % seq_len == 0 or CONV_CHUNK > seq_len:
        nxt = jnp.where(pos == seq_len - 1, 0.0, nxt)
    return prev * w_ref[0:1, :] + cur * w_ref[1:2, :] + nxt * w_ref[2:3, :] + b_ref[...]


def _ada_kernel(c_ref, w_ref, b_ref, o_ref):
    a = _silu(c_ref[...]).astype(BF16)
    o_ref[...] = _dot(a, w_ref[...].astype(BF16)) + b_ref[...]


def _ada(cvec, w_ada, b_ada):
    tn = 1536
    return pl.pallas_call(
        _ada_kernel,
        grid=(6 * D_MODEL // tn,),
        in_specs=[pl.BlockSpec((8, D_MODEL), lambda j: (0, 0)),
                  pl.BlockSpec((D_MODEL, tn), lambda j: (0, j)),
                  pl.BlockSpec((1, tn), lambda j: (0, j))],
        out_specs=pl.BlockSpec((8, tn), lambda j: (0, j)),
        out_shape=jax.ShapeDtypeStruct((8, 6 * D_MODEL), F32),
        compiler_params=_params("arbitrary"),
        name="ada_mod",
    )(cvec, w_ada, b_ada.reshape(1, -1))


IN_SPLITS = (Q_LORA, KV_LORA, QK_ROPE, SSD_INNER, SSD_INNER, SSD_GROUPS * SSD_STATE, SSD_GROUPS * SSD_STATE,
             2 * SSD_HEADS, D_MODEL, D_MODEL)
IN_OFFSETS = tuple(int(v) for v in np.cumsum((0,) + IN_SPLITS))


def _regroup_kernel(w_ref, o_ref):
    dst = 0

    def put(src, width):
        nonlocal dst
        for off in range(0, width, IN_TILE):
            n = min(IN_TILE, width - off)
            o_ref[:, dst + off:dst + off + n] = w_ref[:, src + off:src + off + n].astype(BF16)
        dst += width

    def zeros(width):
        nonlocal dst
        o_ref[:, dst:dst + width] = jnp.zeros((o_ref.shape[0], width), BF16)
        dst += width

    piece = lambda i: put(IN_OFFSETS[i], IN_SPLITS[i])
    for i in (3, 8, 9, 5, 6, 4, 0, 1):
        piece(i)
    kr0 = IN_OFFSETS[2]
    piece(2)
    zeros(QK_ROPE)
    n = QK_ROPE // 4
    for blk in (1, 0, 3, 2):
        put(kr0 + blk * n, n)
    zeros(QK_ROPE)
    piece(7)
    zeros(o_ref.shape[1] - dst)


def _regroup_w_in(w_in2d):
    rows = 256
    n_out = (N_LOW_TILES + N_F32_TILES) * IN_TILE
    return pl.pallas_call(
        _regroup_kernel,
        grid=(D_MODEL // rows,),
        in_specs=[pl.BlockSpec((rows, w_in2d.shape[1]), lambda i: (i, 0))],
        out_specs=pl.BlockSpec((rows, n_out), lambda i: (i, 0)),
        out_shape=jax.ShapeDtypeStruct((D_MODEL, n_out), BF16),
        compiler_params=_params("arbitrary"),
        name="w_in_regroup",
    )(w_in2d)


N_LOW_TILES, N_F32_TILES = 7, 4
Z_BLK, GM_BLK, GS_BLK = 0, 1, 2
BC_TILE = 6
XS_BLK = 0
MLA_TILE, MISC_TILE = 2, 3
DT_BLK = (MISC_TILE * IN_TILE + 256) // 128
ATTN_HEAD_COLS = 256


def _in_kernel(x_ref, sh_ref, sc_ref, g_ref, w_ref, cw_ref, cb_ref, lo_ref, hi_ref, h_scr, u_scr, *, seq_len):
    j = pl.program_id(1)
    n_chunks = ROW_GROUP // CONV_CHUNK
    rows = lambda c: slice(c * CONV_CHUNK, (c + 1) * CONV_CHUNK)

    @pl.when(j == 0)
    def _():
        _norm_mod_rows(x_ref, g_ref, sc_ref, sh_ref, h_scr)
        _zero_conv_pads(u_scr)

    def pointwise(fn, o_ref):
        for c in range(n_chunks):
            o_ref[rows(c), :] = fn(_dot(h_scr[rows(c), :], w_ref[...])).astype(o_ref.dtype)

    def conv(o_ref):
        def conv_out(c):
            v = _silu(_dwconv3_rows(u_scr, c, slice(0, IN_TILE), cw_ref, cb_ref, seq_len))
            o_ref[rows(c), :] = v.astype(o_ref.dtype)
        for c in range(n_chunks):
            u_scr[_stage_rows(c), :] = _dot(h_scr[rows(c), :], w_ref[...])
            if c >= 1:
                conv_out(c - 1)
        conv_out(n_chunks - 1)

    @pl.when(j <= 1)
    def _():
        pointwise(_silu, lo_ref)

    @pl.when((j >= 2) & (j <= 5))
    def _():
        pointwise(_sigmoid, lo_ref)

    @pl.when(j == BC_TILE)
    def _():
        conv(lo_ref)

    @pl.when((j == N_LOW_TILES) | (j == N_LOW_TILES + 1))
    def _():
        conv(hi_ref)

    @pl.when(j >= N_LOW_TILES + MLA_TILE)
    def _():
        pointwise(lambda u: u, hi_ref)


def _in_proj(x2d, mod48, mod_row, norm_g, w_in_r, conv_w, conv_b, seq_len):
    t = x2d.shape[0]
    n_tiles = N_LOW_TILES + N_F32_TILES
    conv_idx = lambda r, j: (0, jnp.where(j == BC_TILE, 2, jnp.clip(j - N_LOW_TILES, 0, 1)))
    return pl.pallas_call(
        functools.partial(_in_kernel, seq_len=seq_len),
        grid=(t // ROW_GROUP, n_tiles),
        in_specs=[pl.BlockSpec((ROW_GROUP, D_MODEL), lambda r, j: (r, 0)),
                  pl.BlockSpec((None, 1, D_MODEL), lambda r, j: (mod_row(r) * 6 + 0, 0, 0)),
                  pl.BlockSpec((None, 1, D_MODEL), lambda r, j: (mod_row(r) * 6 + 1, 0, 0)),
                  pl.BlockSpec((1, D_MODEL), lambda r, j: (0, 0)),
                  pl.BlockSpec((D_MODEL, IN_TILE), lambda r, j: (0, j)),
                  pl.BlockSpec((3, IN_TILE), conv_idx),
                  pl.BlockSpec((1, IN_TILE), conv_idx)],
        out_specs=[pl.BlockSpec((ROW_GROUP, IN_TILE), lambda r, j: (r, jnp.minimum(j, N_LOW_TILES - 1))),
                   pl.BlockSpec((ROW_GROUP, IN_TILE), lambda r, j: (r, jnp.maximum(j - N_LOW_TILES, 0)))],
        out_shape=[jax.ShapeDtypeStruct((t, N_LOW_TILES * IN_TILE), BF16),
                   jax.ShapeDtypeStruct((t, N_F32_TILES * IN_TILE), F32)],
        scratch_shapes=[pltpu.VMEM((ROW_GROUP, D_MODEL), BF16),
                        pltpu.VMEM((ROW_GROUP + 2 * CONV_PAD, IN_TILE), F32)],
        compiler_params=_params("arbitrary", "arbitrary"),
        name="in_proj",
    )(x2d, mod48, mod48, norm_g, w_in_r, conv_w, conv_b)


def _mla_kernel(*refs, latent, has_ctx, seq_len):
    refs = list(refs)
    pm_ref, px_ref, qg_ref, kvg_ref, wuq_ref, wukv_ref = refs[:6]
    del refs[:6]
    if latent:
        cos_ref, sin_ref = refs[:2]
        del refs[:2]
    if has_ctx:
        cckv_ref, ckr_ref = refs[:2]
        del refs[:2]
    o_ref, ckv_ref, kr_ref, k_scr, v_scr = refs[:5]
    del refs[:5]
    if has_ctx:
        kc_scr, vc_scr = refs
    t = pl.program_id(1)
    n_nope = N_HEADS * QK_NOPE
    tq = ATTN_Q_TILE

    def put_keys(k_dst, v_dst, rows, ckv_n, kr128):
        kv = _dot(ckv_n.astype(BF16), wukv_ref[...])
        v_dst[rows, :] = kv[:, n_nope:].astype(BF16)
        kr_bf = kr128.astype(BF16)
        for h in range(N_HEADS):
            base = h * ATTN_HEAD_COLS
            k_dst[rows, base:base + QK_NOPE] = kv[:, h * QK_NOPE:(h + 1) * QK_NOPE].astype(BF16)
            k_dst[rows, base + QK_NOPE:base + ATTN_HEAD_COLS] = kr_bf

    @pl.when(t == 0)
    def _():
        step = min(seq_len, TOKEN_TILE)
        for r0 in range(0, seq_len, step):
            rows = slice(r0, r0 + step)
            ckv_n = _rmsnorm(pm_ref[0, rows, Q_LORA:], kvg_ref[...])
            ckv_ref[0, rows, :] = ckv_n
            kr = px_ref[0, rows, 0:128]
            if latent:
                kr = kr * cos_ref[rows, :] + px_ref[0, rows, 128:256] * sin_ref[rows, :]
            kr_ref[0, rows, :] = kr[:, :QK_ROPE]
            put_keys(k_scr, v_scr, rows, ckv_n, kr)
        if has_ctx:
            past = cckv_ref.shape[1]
            put_keys(kc_scr, vc_scr, slice(0, past), cckv_ref[0], jnp.zeros((past, 128), F32))
            ckr_bf = ckr_ref[0].astype(BF16)
            for h in range(N_HEADS):
                base = h * ATTN_HEAD_COLS + QK_NOPE
                kc_scr[:, base:base + QK_ROPE] = ckr_bf

    qrows = pl.ds(pl.multiple_of(t * tq, tq), tq)
    scale = LOG2_E / math.sqrt(QK_NOPE + QK_ROPE)
    cqn = _rmsnorm(pm_ref[0, qrows, 0:Q_LORA], qg_ref[...]).astype(BF16)
    q = _dot(cqn, wuq_ref[...])
    q_rope = q[:, n_nope:2 * n_nope]
    if latent:
        q_rope = (q_rope * jnp.concatenate([cos_ref[qrows, :]] * N_HEADS, axis=1)
                  + q[:, 2 * n_nope:3 * n_nope] * jnp.concatenate([sin_ref[qrows, :]] * N_HEADS, axis=1))
    segs = ([(kc_scr, vc_scr)] if has_ctx else []) + [(k_scr, v_scr)]
    for h in range(N_HEADS):
        head = slice(h * QK_NOPE, (h + 1) * QK_NOPE)
        qk = slice(h * ATTN_HEAD_COLS, (h + 1) * ATTN_HEAD_COLS)
        qh = (jnp.concatenate([q[:, head], q_rope[:, head]], axis=1) * scale).astype(BF16)
        s = [_dot_nt(qh, k[:, qk]) for k, _ in segs]
        m = functools.reduce(jnp.maximum, [jnp.max(si, axis=-1, keepdims=True) for si in s])
        p = [jnp.exp2(si - m) for si in s]
        l = functools.reduce(jnp.add, [jnp.sum(pi, axis=-1, keepdims=True) for pi in p])
        o = functools.reduce(jnp.add, [_dot(pi.astype(BF16), v[:, head]) for pi, (_, v) in zip(p, segs)])
        o_ref[0, :, head] = (o / l).astype(BF16)


def _mla(proj_hi, q_norm_g, kv_norm_g, w_uq_r, w_ukv_r, rope_tables, ctx):
    b, l, _ = proj_hi.shape
    tq = ATTN_Q_TILE
    latent = rope_tables is not None
    has_ctx = ctx is not None
    once = dict(pipeline_mode=pl.Buffered(1))
    per_batch = once if l // tq > 1 else {}
    const2 = lambda i, t: (0, 0)
    in_specs = [pl.BlockSpec((1, l, IN_TILE), lambda i, t: (i, 0, MLA_TILE), **per_batch),
                pl.BlockSpec((1, l, 256), lambda i, t: (i, 0, MISC_TILE * IN_TILE // 256), **per_batch),
                pl.BlockSpec((1, Q_LORA), const2),
                pl.BlockSpec((1, KV_LORA), const2),
                pl.BlockSpec(w_uq_r.shape, const2, **once),
                pl.BlockSpec(w_ukv_r.shape, const2, **once)]
    args = [proj_hi, proj_hi, q_norm_g, kv_norm_g, w_uq_r, w_ukv_r]
    scratch = [pltpu.VMEM((l, N_HEADS * ATTN_HEAD_COLS), BF16), pltpu.VMEM((l, N_HEADS * V_HEAD), BF16)]
    if latent:
        in_specs += [pl.BlockSpec((l, 128), const2, **once)] * 2
        args += list(rope_tables)
    if has_ctx:
        past = ctx[0].shape[1]
        in_specs += [pl.BlockSpec((1, past, KV_LORA), lambda i, t: (i, 0, 0), **per_batch),
                     pl.BlockSpec((1, past, QK_ROPE), lambda i, t: (i, 0, 0), **per_batch)]
        args += list(ctx)
        scratch += [pltpu.VMEM((past, N_HEADS * ATTN_HEAD_COLS), BF16), pltpu.VMEM((past, N_HEADS * V_HEAD), BF16)]
    return pl.pallas_call(
        functools.partial(_mla_kernel, latent=latent, has_ctx=has_ctx, seq_len=l),
        grid=(b, l // tq),
        in_specs=in_specs,
        out_specs=[pl.BlockSpec((1, tq, N_HEADS * V_HEAD), lambda i, t: (i, t, 0)),
                   pl.BlockSpec((1, l, KV_LORA), lambda i, t: (i, 0, 0)),
                   pl.BlockSpec((1, l, QK_ROPE), lambda i, t: (i, 0, 0))],
        out_shape=[jax.ShapeDtypeStruct((b, l, N_HEADS * V_HEAD), BF16),
                   jax.ShapeDtypeStruct((b, l, KV_LORA), F32),
                   jax.ShapeDtypeStruct((b, l, QK_ROPE), F32)],
        scratch_shapes=scratch,
        compiler_params=_params("arbitrary", "arbitrary"),
        name="mla_attention",
    )(*args)


def _split3(x):
    hi = x.astype(BF16)
    r = x - hi.astype(F32)
    mid = r.astype(BF16)
    lo = (r - mid.astype(F32)).astype(BF16)
    return hi, mid, lo


def _exact_dot(parts, sel):
    return functools.reduce(jnp.add, [_dot(p, sel) for p in parts])


def _exact_dot_rows(sel, parts):
    return functools.reduce(jnp.add, [_dot(sel, p) for p in parts])


HEADS_PER_GROUP = SSD_HEADS // SSD_GROUPS
GROUP_COLS = HEADS_PER_GROUP * SSD_HEADDIM


def _ssd_kernel(*refs, nc, cps, has_h0):
    if has_h0:
        (xs_ref, zs_ref, bc_ref, dt_ref, dtb_ref, a_ref, dx_ref, h0_ref, y_ref, ht_ref,
         yl_scr, acum_scr, src_scr, tot_scr, sb_scr, h_scr, esel_scr) = refs
    else:
        (xs_ref, zs_ref, bc_ref, dt_ref, dtb_ref, a_ref, dx_ref, y_ref, ht_ref,
         yl_scr, acum_scr, src_scr, tot_scr, sb_scr, h_scr, esel_scr) = refs
    s = pl.program_id(1)
    q = SSD_CHUNK
    n_bc = SSD_GROUPS * SSD_STATE
    lane = lax.broadcasted_iota(jnp.int32, (q, 128), 1)
    low_half = lane < SSD_HEADDIM
    ii = lax.broadcasted_iota(jnp.int32, (q, q), 0)
    jj = lax.broadcasted_iota(jnp.int32, (q, q), 1)
    lower, upper = ii >= jj, ii <= jj

    def lane_bcast(parts, d):
        return _exact_dot(parts, esel_scr[d])

    def stacked_states(d):
        return [jnp.concatenate([h_scr[d, 2 * i], h_scr[d, 2 * i + 1]], axis=0).astype(BF16)
                for i in range(SSD_GROUPS // 2)]

    def group_c(bc, g):
        i, r = divmod(g, 2)
        cpair = bc[:, n_bc + i * 128:n_bc + (i + 1) * 128]
        return jnp.where(low_half if r == 0 else ~low_half, cpair, 0.0).astype(BF16)

    @pl.when(s == 0)
    def _():
        k = lax.broadcasted_iota(jnp.int32, (128, SSD_INNER), 0)
        head = lax.broadcasted_iota(jnp.int32, (128, SSD_INNER), 1) // SSD_HEADDIM
        for d in range(2):
            esel_scr[d] = jnp.where(k == d * SSD_HEADS + head, 1.0, 0.0).astype(BF16)
        if has_h0:
            for d in range(2):
                for g in range(SSD_GROUPS):
                    hpn = h0_ref[0, d, g * HEADS_PER_GROUP:(g + 1) * HEADS_PER_GROUP].reshape(GROUP_COLS, SSD_STATE)
                    h_scr[d, g] = hpn.T
        else:
            h_scr[...] = jnp.zeros(h_scr.shape, F32)
        tri_f = jnp.where(lower, 1.0, 0.0).astype(BF16)
        tri_b = jnp.where(upper, 1.0, 0.0).astype(BF16)
        fwd_col = lane < SSD_HEADS
        for c in range(nc):
            crow = slice(c * q, (c + 1) * q)
            dt = _softplus(dt_ref[0, crow, :] + dtb_ref[...])
            parts = _split3(dt * a_ref[...])
            acum = jnp.where(fwd_col, _exact_dot_rows(tri_f, parts),
                             _exact_dot_rows(tri_b, parts))
            acum = acum * LOG2_E
            acum_scr[crow, :] = acum
            tot = jnp.where(fwd_col[0:1], acum[q - 1:q, :], acum[0:1, :])
            tot_scr[c] = jnp.broadcast_to(tot, (8, 128))
            src_scr[c] = (acum - jnp.log2(dt)).T

    def first_sweep(c, blk):
        rows = pl.ds(pl.multiple_of(c * q, q), q)
        x = xs_ref[0, blk, :]
        bc = bc_ref[0, blk, :].astype(F32)
        acum = acum_scr[rows, :]
        src_t = src_scr[c]
        tot8 = tot_scr[c]
        e = jnp.exp2(acum)
        e_hi = e.astype(BF16)
        eb_f = lane_bcast([e_hi, (e - e_hi.astype(F32)).astype(BF16)], 0)
        cd_f = jnp.exp2(lane_bcast(_split3(tot8), 0))[0:1]
        b_t = [bc[:, i * 128:(i + 1) * 128].T for i in range(SSD_GROUPS // 2)]
        h_in = stacked_states(0)
        for g in range(SSD_GROUPS):
            i, r = divmod(g, 2)
            cm = group_c(bc, g)
            scores = _dot_nt(cm, bc[:, i * 128:(i + 1) * 128].astype(BF16))
            bg_t = b_t[i][r * SSD_STATE:(r + 1) * SSD_STATE, :]
            gcols = slice(g * GROUP_COLS, (g + 1) * GROUP_COLS)
            y_off = _dot(cm, h_in[i]) * eb_f[:, gcols]
            for t in range(HEADS_PER_GROUP // 2):
                pc = slice((2 * g + t) * 128, (2 * g + t + 1) * 128)
                xp = x[:, pc]
                x2 = jnp.concatenate([jnp.where(low_half, xp, 0.0), jnp.where(low_half, 0.0, xp)], axis=0).astype(BF16)
                m, sf, sb = [], [], []
                for u in range(2):
                    kf = g * HEADS_PER_GROUP + 2 * t + u
                    kb = SSD_HEADS + kf
                    af_col = jnp.broadcast_to(acum[:, kf:kf + 1], (q, q))
                    ab_col = jnp.broadcast_to(acum[:, kb:kb + 1], (q, q))
                    sf_row, sb_row = src_t[kf:kf + 1, :], src_t[kb:kb + 1, :]
                    decay = (jnp.exp2(jnp.where(lower, af_col - sf_row, NEG_BIG))
                             + jnp.exp2(jnp.where(upper, ab_col - sb_row, NEG_BIG)))
                    m.append((scores * decay).astype(BF16))
                    wf = jnp.exp2(tot8[0:1, kf:kf + 1] - sf_row)
                    wb = jnp.exp2(tot8[0:1, kb:kb + 1] - sb_row)
                    sf.append((bg_t * wf).astype(BF16))
                    sb.append((bg_t * wb).astype(BF16))
                y_pair = _dot(jnp.concatenate(m, axis=1), x2)
                tc = slice(t * 128, (t + 1) * 128)
                yl_scr[rows, pc] = y_pair + y_off[:, tc] + dx_ref[:, pc] * xp
                h_scr[0, g, :, tc] = h_scr[0, g, :, tc] * cd_f[:, pc] + _dot(jnp.concatenate(sf, axis=1), x2)
                sb_scr[c, g, :, tc] = _dot(jnp.concatenate(sb, axis=1), x2)

    def second_sweep(c, blk):
        rows = pl.ds(pl.multiple_of(c * q, q), q)
        bc = bc_ref[0, blk, :].astype(F32)
        e = jnp.exp2(acum_scr[rows, :])
        e_hi = e.astype(BF16)
        eb_b = lane_bcast([e_hi, (e - e_hi.astype(F32)).astype(BF16)], 1)
        cd_b = jnp.exp2(lane_bcast(_split3(tot_scr[c]), 1))[0:1]
        h_in = stacked_states(1)
        for g in range(SSD_GROUPS):
            gcols = slice(g * GROUP_COLS, (g + 1) * GROUP_COLS)
            y_off = _dot(group_c(bc, g), h_in[g // 2]) * eb_b[:, gcols]
            y_ref[0, blk, gcols] = ((yl_scr[rows, gcols] + y_off) * zs_ref[0, blk, gcols].astype(F32)).astype(y_ref.dtype)
            h_scr[1, g] = h_scr[1, g] * cd_b[:, gcols] + sb_scr[c, g]

    n_steps = nc // cps

    @pl.when(s < n_steps)
    def _():
        for ci in range(cps):
            first_sweep(s * cps + ci, slice(ci * q, (ci + 1) * q))

    @pl.when(s >= n_steps)
    def _():
        for ci in reversed(range(cps)):
            second_sweep((2 * n_steps - 1 - s) * cps + ci, slice(ci * q, (ci + 1) * q))

    @pl.when(s == 2 * n_steps - 1)
    def _():
        for d in range(2):
            for g in range(SSD_GROUPS):
                ht_ref[0, d, g * HEADS_PER_GROUP:(g + 1) * HEADS_PER_GROUP] = h_scr[d, g].T.reshape(
                    HEADS_PER_GROUP, SSD_HEADDIM, SSD_STATE)


def _ssd(proj_lo, proj_hi, h0, dt_bias128, a128, d_exp):
    b, l, _ = proj_hi.shape
    q = SSD_CHUNK
    nc = l // q
    cps = SSD_CHUNKS_PER_STEP
    n_steps = nc // cps
    rows = cps * q
    early = lambda s: jnp.minimum(s, n_steps - 1)
    both = lambda s: jnp.where(s < n_steps, s, 2 * n_steps - 1 - s)
    late = lambda s: jnp.where(s < n_steps, n_steps - 1, 2 * n_steps - 1 - s)
    st_shape = (1, 2, SSD_HEADS, SSD_HEADDIM, SSD_STATE)
    st_spec = pl.BlockSpec(st_shape, lambda i, s: (i, 0, 0, 0, 0))
    has_h0 = h0 is not None
    return pl.pallas_call(
        functools.partial(_ssd_kernel, nc=nc, cps=cps, has_h0=has_h0),
        grid=(b, 2 * n_steps),
        in_specs=[pl.BlockSpec((1, rows, SSD_INNER), lambda i, s: (i, early(s), XS_BLK)),
                  pl.BlockSpec((1, rows, SSD_INNER), lambda i, s: (i, late(s), Z_BLK)),
                  pl.BlockSpec((1, rows, IN_TILE), lambda i, s: (i, both(s), BC_TILE)),
                  pl.BlockSpec((1, l, 128), lambda i, s: (i, 0, DT_BLK)),
                  pl.BlockSpec((1, 128), lambda i, s: (0, 0)),
                  pl.BlockSpec((1, 128), lambda i, s: (0, 0)),
                  pl.BlockSpec((1, SSD_INNER), lambda i, s: (0, 0))] + ([st_spec] if has_h0 else []),
        out_specs=[pl.BlockSpec((1, rows, SSD_INNER), lambda i, s: (i, late(s), 0)),
                   pl.BlockSpec(st_shape, lambda i, s: (i, 0, 0, 0, 0))],
        out_shape=[jax.ShapeDtypeStruct((b, l, SSD_INNER), BF16),
                   jax.ShapeDtypeStruct((b,) + st_shape[1:], F32)],
        scratch_shapes=[pltpu.VMEM((l, SSD_INNER), F32),
                        pltpu.VMEM((l, 128), F32),
                        pltpu.VMEM((nc, 128, q), F32),
                        pltpu.VMEM((nc, 8, 128), F32),
                        pltpu.VMEM((nc, SSD_GROUPS, SSD_STATE, GROUP_COLS), F32),
                        pltpu.VMEM((2, SSD_GROUPS, SSD_STATE, GROUP_COLS), F32),
                        pltpu.VMEM((2, 128, SSD_INNER), BF16)],
        compiler_params=_params("arbitrary", "arbitrary"),
        name="ssd_scan",
    )(proj_hi, proj_lo, proj_lo, proj_hi, dt_bias128, a128, d_exp, *([h0] if has_h0 else []))


def _merge_kernel(attn_ref, yz_ref, gm_ref, gs_ref, x_ref, g1_ref, ng_ref, womla_ref, wossd_ref, wout_ref, o_ref, w_scr):
    @pl.when(pl.program_id(0) == 0)
    def _():
        w_scr[0] = womla_ref[...].astype(BF16)
        w_scr[1] = wossd_ref[...].astype(BF16)
        w_scr[2] = wout_ref[...].astype(BF16)

    o_mla = _dot(attn_ref[...], w_scr[0])
    o_ssd = _dot(_rmsnorm(yz_ref[...].astype(F32), ng_ref[...]).astype(BF16), w_scr[1])
    merged = gm_ref[...].astype(F32) * o_mla + gs_ref[...].astype(F32) * o_ssd
    o_ref[...] = x_ref[...] + g1_ref[...] * _dot(merged.astype(BF16), w_scr[2])


def _merge(attn2d, yz2d, proj, x2d, mod48, mod_row, ssd_norm_g, w_o_mla, w_o_ssd, w_out):
    t = x2d.shape[0]
    tm = TOKEN_TILE
    row = lambda i: (i, 0)
    const = lambda i: (0, 0)
    wspec = pl.BlockSpec((D_MODEL, D_MODEL), const, pipeline_mode=pl.Buffered(1))
    return pl.pallas_call(
        _merge_kernel,
        grid=(t // tm,),
        in_specs=[pl.BlockSpec((tm, D_MODEL), row),
                  pl.BlockSpec((tm, D_MODEL), row),
                  pl.BlockSpec((tm, D_MODEL), lambda i: (i, GM_BLK)),
                  pl.BlockSpec((tm, D_MODEL), lambda i: (i, GS_BLK)),
                  pl.BlockSpec((tm, D_MODEL), row),
                  pl.BlockSpec((None, 1, D_MODEL), lambda i: (mod_row(i * tm // ROW_GROUP) * 6 + 2, 0, 0)),
                  pl.BlockSpec((1, D_MODEL), const),
                  wspec, wspec, wspec],
        out_specs=pl.BlockSpec((tm, D_MODEL), row),
        out_shape=jax.ShapeDtypeStruct((t, D_MODEL), F32),
        scratch_shapes=[pltpu.VMEM((3, D_MODEL, D_MODEL), BF16)],
        compiler_params=_params("arbitrary"),
        name="merge_out",
    )(attn2d, yz2d, proj, proj, x2d, mod48, ssd_norm_g, w_o_mla, w_o_ssd, w_out)


def _ffn_kernel(x_ref, sh_ref, sc_ref, g2_ref, ng_ref, wg_ref, wv_ref, cwg_ref, cwv_ref, cbg_ref, cbv_ref, wd_ref,
                fg_ref, o_ref, h_scr, wup_scr, wd_scr, u_scr, *, seq_len):
    j = pl.program_id(1)
    n_chunks = ROW_GROUP // CONV_CHUNK
    rows = lambda c: slice(c * CONV_CHUNK, (c + 1) * CONV_CHUNK)

    @pl.when(j == 0)
    def _():
        _norm_mod_rows(x_ref, ng_ref, sc_ref, sh_ref, h_scr)
        _zero_conv_pads(u_scr)
        o_ref[...] = jnp.zeros_like(o_ref)

    wup_scr[:, 0:FFN_TILE] = wg_ref[...].astype(BF16)
    wup_scr[:, FFN_TILE:2 * FFN_TILE] = wv_ref[...].astype(BF16)
    wd_scr[...] = wd_ref[...].astype(BF16)

    def gated(c):
        ug = _dwconv3_rows(u_scr, c, slice(0, FFN_TILE), cwg_ref, cbg_ref, seq_len)
        uv = _dwconv3_rows(u_scr, c, slice(FFN_TILE, 2 * FFN_TILE), cwv_ref, cbv_ref, seq_len)
        return (_silu(ug) * uv).astype(BF16)

    act = {}
    for c in range(n_chunks + 2):
        if c < n_chunks:
            u_scr[_stage_rows(c), :] = _dot(h_scr[rows(c), :], wup_scr[...])
        if c >= 2:
            o_ref[rows(c - 2), :] += _dot(act.pop(c - 2), wd_scr[...])
        if 1 <= c <= n_chunks:
            act[c - 1] = gated(c - 1)

    @pl.when(j == pl.num_programs(1) - 1)
    def _():
        def body(i, carry):
            rows = pl.ds(pl.multiple_of(i * 256, 256), 256)
            o_ref[rows, :] = _rmsnorm(x_ref[rows, :] + g2_ref[...] * o_ref[rows, :], fg_ref[...])
            return carry
        lax.fori_loop(0, ROW_GROUP // 256, body, 0)


def _ffn(x2d, mod48, mod_row, norm_g, w_up, conv_w, conv_b, w_down, final_g, seq_len):
    t = x2d.shape[0]
    nj = D_FF // FFN_TILE
    gate = lambda r, j: (0, j)
    val = lambda r, j: (0, nj + j)
    const = lambda r, j: (0, 0)
    mod = lambda k: pl.BlockSpec((None, 1, D_MODEL), lambda r, j: (mod_row(r) * 6 + k, 0, 0))
    return pl.pallas_call(
        functools.partial(_ffn_kernel, seq_len=seq_len),
        grid=(t // ROW_GROUP, nj),
        in_specs=[pl.BlockSpec((ROW_GROUP, D_MODEL), lambda r, j: (r, 0)),
                  mod(3), mod(4), mod(5),
                  pl.BlockSpec((1, D_MODEL), const),
                  pl.BlockSpec((D_MODEL, FFN_TILE), gate),
                  pl.BlockSpec((D_MODEL, FFN_TILE), val),
                  pl.BlockSpec((3, FFN_TILE), gate),
                  pl.BlockSpec((3, FFN_TILE), val),
                  pl.BlockSpec((1, FFN_TILE), gate),
                  pl.BlockSpec((1, FFN_TILE), val),
                  pl.BlockSpec((FFN_TILE, D_MODEL), lambda r, j: (j, 0)),
                  pl.BlockSpec((1, D_MODEL), const)],
        out_specs=pl.BlockSpec((ROW_GROUP, D_MODEL), lambda r, j: (r, 0)),
        out_shape=jax.ShapeDtypeStruct((t, D_MODEL), F32),
        scratch_shapes=[pltpu.VMEM((ROW_GROUP, D_MODEL), BF16),
                        pltpu.VMEM((D_MODEL, 2 * FFN_TILE), BF16),
                        pltpu.VMEM((FFN_TILE, D_MODEL), BF16),
                        pltpu.VMEM((ROW_GROUP + 2 * CONV_PAD, 2 * FFN_TILE), F32)],
        compiler_params=_params("arbitrary", "arbitrary"),
        name="conv_ffn",
    )(x2d, mod48, mod48, mod48, norm_g, w_up, w_up, conv_w, conv_w, conv_b, conv_b, w_down, final_g)


def _rope_tables(seq_len):
    t = np.arange(seq_len)
    row = (t // GRID_W).astype(np.float32)
    col = (t % GRID_W).astype(np.float32)
    n = QK_ROPE // 4
    inv = (np.float32(ROPE_BASE) ** (-np.arange(n, dtype=np.float32) / np.float32(n))).astype(np.float32)
    ar, ac = row[:, None] * inv, col[:, None] * inv
    cos64 = np.concatenate([np.cos(ar), np.cos(ar), np.cos(ac), np.cos(ac)], axis=1)
    sin64 = np.concatenate([-np.sin(ar), np.sin(ar), -np.sin(ac), np.sin(ac)], axis=1)
    zeros = np.zeros_like(cos64)
    return (jnp.asarray(np.concatenate([cos64, zeros], axis=1), F32),
            jnp.asarray(np.concatenate([sin64, zeros], axis=1), F32))


def _swap_rope_halves(w):
    lead = w.shape[:-1]
    return w.reshape(lead + (2, 2, QK_ROPE // 4))[..., ::-1, :].reshape(lead + (QK_ROPE,))


def _trunk_pass(x, mod48, mod_row, wts, ctx, latent):
    b, l, _ = x.shape
    x2d = x.reshape(b * l, D_MODEL)
    proj_lo, proj_hi = _in_proj(x2d, mod48, mod_row, wts["norm_attn_g"], wts["w_in_r"], wts["ssd_conv_w"],
                                wts["ssd_conv_b"], l)
    rope = _rope_tables(l) if latent else None
    w_uq_r = wts["w_uq_lat"] if latent else wts["w_uq_ctx"]
    shape3 = lambda a, n: a.reshape(b, n, a.shape[-1])
    h0 = None
    mla_ctx = None
    if ctx is not None:
        cache_ckv, cache_krope, h0 = ctx
        mla_ctx = (cache_ckv, cache_krope)
    attn, ckv_n, kr3 = _mla(shape3(proj_hi, l), wts["q_norm_g"], wts["kv_norm_g"], w_uq_r, wts["w_ukv_r"], rope, mla_ctx)
    yz, h_t = _ssd(shape3(proj_lo, l), shape3(proj_hi, l), h0, wts["dt_bias128"], wts["a128"], wts["d_exp"])
    x1 = _merge(attn.reshape(b * l, -1), yz.reshape(b * l, -1), proj_lo, x2d, mod48, mod_row, wts["ssd_norm_g"],
                wts["w_o_mla"], wts["w_o_ssd"], wts["w_out"])
    y = _ffn(x1, mod48, mod_row, wts["norm_ffn_g"], wts["w_up"], wts["ffn_conv_w"], wts["ffn_conv_b"], wts["w_down"],
             wts["final_norm_g"], l)
    return y.reshape(b, l, D_MODEL), ckv_n, kr3, h_t


def kernel(x_prompt, x_sample, c, cache_ckv, cache_krope, state_ssd, c_ctx, w_ada, b_ada, norm_attn_g, w_in, q_norm_g,
           kv_norm_g, w_uq, w_ukv, w_o_mla, ssd_conv_w, ssd_conv_b, ssd_dt_bias, ssd_A_log, ssd_D, ssd_norm_g, w_o_ssd,
           w_out, norm_ffn_g, w_up, ffn_conv_w, ffn_conv_b, w_down, final_norm_g):
    depth = w_in.shape[0]
    assert depth == 1, "single trunk layer"
    dec_b = x_sample.shape[0]
    assert x_sample.shape[1] == ROW_GROUP and ROW_GROUP % x_prompt.shape[1] == 0
    lyr = 0

    cvec = jnp.zeros((8, D_MODEL), F32).at[0].set(c_ctx).at[1:1 + dec_b].set(c)
    mod48 = _ada(cvec, w_ada[lyr], b_ada[lyr]).reshape(8 * 6, 1, D_MODEL)

    w_in_r = _regroup_w_in(w_in[lyr])
    wq = w_uq[lyr].reshape(Q_LORA, N_HEADS, QK_NOPE + QK_ROPE)
    wq_nope = wq[:, :, :QK_NOPE].reshape(Q_LORA, -1)
    wq_rope = wq[:, :, QK_NOPE:]
    pad_rope = lambda w: jnp.pad(w, ((0, 0), (0, 0), (0, 128 - QK_ROPE))).reshape(Q_LORA, -1)
    w_uq_ctx = jnp.concatenate([wq_nope, pad_rope(wq_rope)], axis=1).astype(BF16)
    w_uq_lat = jnp.concatenate([wq_nope, pad_rope(wq_rope), pad_rope(_swap_rope_halves(wq_rope))], axis=1).astype(BF16)
    wkv = w_ukv[lyr].reshape(KV_LORA, N_HEADS, QK_NOPE + V_HEAD)
    w_ukv_r = jnp.concatenate([wkv[:, :, :QK_NOPE].reshape(KV_LORA, -1),
                               wkv[:, :, QK_NOPE:].reshape(KV_LORA, -1)], axis=1).astype(BF16)
    pad128 = lambda a: jnp.pad(a.reshape(1, -1), ((0, 0), (0, 128 - a.size)))
    wts = {
        "norm_attn_g": norm_attn_g[lyr].reshape(1, -1), "w_in_r": w_in_r,
        "ssd_conv_w": ssd_conv_w[lyr], "ssd_conv_b": ssd_conv_b[lyr].reshape(1, -1),
        "q_norm_g": q_norm_g[lyr].reshape(1, -1), "kv_norm_g": kv_norm_g[lyr].reshape(1, -1),
        "w_uq_ctx": w_uq_ctx, "w_uq_lat": w_uq_lat, "w_ukv_r": w_ukv_r,
        "dt_bias128": pad128(ssd_dt_bias[lyr]), "a128": pad128(-jnp.exp(ssd_A_log[lyr])),
        "d_exp": jnp.repeat(ssd_D[lyr], SSD_HEADDIM).reshape(1, -1),
        "ssd_norm_g": ssd_norm_g[lyr].reshape(1, -1),
        "w_o_mla": w_o_mla[lyr], "w_o_ssd": w_o_ssd[lyr], "w_out": w_out[lyr],
        "norm_ffn_g": norm_ffn_g[lyr].reshape(1, -1), "w_up": w_up[lyr],
        "ffn_conv_w": ffn_conv_w[lyr], "ffn_conv_b": ffn_conv_b[lyr].reshape(1, -1),
        "w_down": w_down[lyr], "final_norm_g": final_norm_g.reshape(1, -1),
    }

    y_p, ckv_p, kr_p, st_p = _trunk_pass(x_prompt, mod48, lambda r: 0, wts, None, False)
    ctx = (cache_ckv[:, lyr], cache_krope[:, lyr], state_ssd[:, lyr])
    y_s, _, _, _ = _trunk_pass(x_sample, mod48, lambda r: 1 + r, wts, ctx, True)
    return y_p, y_s, ckv_p[:, None], kr_p[:, None], st_p[:, None]
```

```python
import functools
import math

import jax
import jax.numpy as jnp
import numpy as np
from jax import lax
from jax.experimental import pallas as pl
from jax.experimental.pallas import tpu as pltpu

F32 = jnp.float32
BF16 = jnp.bfloat16

D_MODEL = 1024
GRID_W = 64
N_HEADS = 8
QK_NOPE = 128
QK_ROPE = 64
V_HEAD = 128
Q_LORA = 256
KV_LORA = 256
ROPE_BASE = 10000.0
SSD_HEADS = 16
SSD_HEADDIM = 64
SSD_INNER = SSD_HEADS * SSD_HEADDIM
SSD_GROUPS = 4
SSD_STATE = 64
SSD_CHUNK = 128
D_FF = 2816
EPS = 1e-6

ROW_GROUP = 2048
IN_TILE = 512
SSD_CHUNKS_PER_STEP = 2
FFN_TILE = 256
TOKEN_TILE = 512
ATTN_Q_TILE = 256
CONV_CHUNK = 256
CONV_PAD = 8
VMEM_LIMIT = 56 * 1024 * 1024
NEG_BIG = -1e30
LOG2_E = 1.4426950408889634


def _sigmoid(x):
    return 1.0 / (1.0 + jnp.exp(-x))


def _silu(x):
    return x * _sigmoid(x)


def _softplus(x):
    e = jnp.exp(-jnp.abs(x))
    u = 1.0 + e
    log1p_e = jnp.where(u == 1.0, e, e * jnp.log(u) / jnp.where(u == 1.0, 1.0, u - 1.0))
    return jnp.maximum(x, 0.0) + log1p_e


def _rmsnorm(x, g):
    return x * lax.rsqrt(jnp.mean(x * x, axis=-1, keepdims=True) + EPS) * g


def _dot(a, b):
    return jnp.dot(a, b, preferred_element_type=F32)


def _dot_nt(a, b):
    return lax.dot_general(a, b, (((1,), (1,)), ((), ())), preferred_element_type=F32)


def _params(*sem):
    return pltpu.CompilerParams(dimension_semantics=sem, vmem_limit_bytes=VMEM_LIMIT)


def _norm_mod_rows(x_ref, g_ref, sc_ref, sh_ref, h_scr):
    def body(i, carry):
        rows = pl.ds(pl.multiple_of(i * CONV_CHUNK, CONV_CHUNK), CONV_CHUNK)
        h = _rmsnorm(x_ref[rows, :], g_ref[...]) * (1.0 + sc_ref[...]) + sh_ref[...]
        h_scr[rows, :] = h.astype(BF16)
        return carry
    lax.fori_loop(0, ROW_GROUP // CONV_CHUNK, body, 0)


def _zero_conv_pads(u_scr):
    zeros = jnp.zeros((CONV_PAD, u_scr.shape[1]), F32)
    u_scr[0:CONV_PAD, :] = zeros
    u_scr[CONV_PAD + ROW_GROUP:2 * CONV_PAD + ROW_GROUP, :] = zeros


def _stage_rows(c):
    return slice(CONV_PAD + c * CONV_CHUNK, CONV_PAD + (c + 1) * CONV_CHUNK)


def _dwconv3_rows(u_scr, c, cols, w_ref, b_ref, seq_len):
    r0 = c * CONV_CHUNK
    base = CONV_PAD + r0
    width = cols.stop - cols.start
    prev = u_scr[base - 1:base - 1 + CONV_CHUNK, cols]
    cur = u_scr[base:base + CONV_CHUNK, cols]
    nxt = u_scr[base + 1:base + 1 + CONV_CHUNK, cols]
    pos = (lax.broadcasted_iota(jnp.int32, (CONV_CHUNK, width), 0) + r0) & (seq_len - 1)
    if r0 % seq_len == 0 or CONV_CHUNK > seq_len:
        prev = jnp.where(pos == 0, 0.0, prev)
    if (r0 + CONV_CHUNK) % seq_len == 0 or CONV_CHUNK > seq_len:
        nxt = jnp.where(pos == seq_len - 1, 0.0, nxt)
    return prev * w_ref[0:1, :] + cur * w_ref[1:2, :] + nxt * w_ref[2:3, :] + b_ref[...]


def _ada_kernel(c_ref, w_ref, b_ref, o_ref):
    a = _silu(c_ref[...]).astype(BF16)
    o_ref[...] = _dot(a, w_ref[...].astype(BF16)) + b_ref[...]


def _ada(cvec, w_ada, b_ada):
    tn = 1536
    return pl.pallas_call(
        _ada_kernel,
        grid=(6 * D_MODEL // tn,),
        in_specs=[pl.BlockSpec((8, D_MODEL), lambda j: (0, 0)),
                  pl.BlockSpec((D_MODEL, tn), lambda j: (0, j)),
                  pl.BlockSpec((1, tn), lambda j: (0, j))],
        out_specs=pl.BlockSpec((8, tn), lambda j: (0, j)),
        out_shape=jax.ShapeDtypeStruct((8, 6 * D_MODEL), F32),
        compiler_params=_params("arbitrary"),
        name="ada_mod",
    )(cvec, w_ada, b_ada.reshape(1, -1))


IN_SPLITS = (Q_LORA, KV_LORA, QK_ROPE, SSD_INNER, SSD_INNER, SSD_GROUPS * SSD_STATE, SSD_GROUPS * SSD_STATE,
             2 * SSD_HEADS, D_MODEL, D_MODEL)
IN_OFFSETS = tuple(int(v) for v in np.cumsum((0,) + IN_SPLITS))


def _regroup_kernel(w_ref, o_ref):
    dst = 0

    def put(src, width):
        nonlocal dst
        for off in range(0, width, IN_TILE):
            n = min(IN_TILE, width - off)
            o_ref[:, dst + off:dst + off + n] = w_ref[:, src + off:src + off + n].astype(BF16)
        dst += width

    def zeros(width):
        nonlocal dst
        o_ref[:, dst:dst + width] = jnp.zeros((o_ref.shape[0], width), BF16)
        dst += width

    piece = lambda i: put(IN_OFFSETS[i], IN_SPLITS[i])
    for i in (3, 8, 9, 5, 6, 4, 0, 1):
        piece(i)
    kr0 = IN_OFFSETS[2]
    piece(2)
    zeros(QK_ROPE)
    n = QK_ROPE // 4
    for blk in (1, 0, 3, 2):
        put(kr0 + blk * n, n)
    zeros(QK_ROPE)
    piece(7)
    zeros(o_ref.shape[1] - dst)


def _regroup_w_in(w_in2d):
    rows = 256
    n_out = (N_LOW_TILES + N_F32_TILES) * IN_TILE
    return pl.pallas_call(
        _regroup_kernel,
        grid=(D_MODEL // rows,),
        in_specs=[pl.BlockSpec((rows, w_in2d.shape[1]), lambda i: (i, 0))],
        out_specs=pl.BlockSpec((rows, n_out), lambda i: (i, 0)),
        out_shape=jax.ShapeDtypeStruct((D_MODEL, n_out), BF16),
        compiler_params=_params("arbitrary"),
        name="w_in_regroup",
    )(w_in2d)


N_LOW_TILES, N_F32_TILES = 7, 4
Z_BLK, GM_BLK, GS_BLK = 0, 1, 2
BC_TILE = 6
XS_BLK = 0
MLA_TILE, MISC_TILE = 2, 3
DT_BLK = (MISC_TILE * IN_TILE + 256) // 128
ATTN_HEAD_COLS = 256


def _in_kernel(x_ref, sh_ref, sc_ref, g_ref, w_ref, cw_ref, cb_ref, lo_ref, hi_ref, h_scr, u_scr, *, seq_len):
    j = pl.program_id(1)
    n_chunks = ROW_GROUP // CONV_CHUNK
    rows = lambda c: slice(c * CONV_CHUNK, (c + 1) * CONV_CHUNK)

    @pl.when(j == 0)
    def _():
        _norm_mod_rows(x_ref, g_ref, sc_ref, sh_ref, h_scr)
        _zero_conv_pads(u_scr)

    def pointwise(fn, o_ref):
        for c in range(n_chunks):
            o_ref[rows(c), :] = fn(_dot(h_scr[rows(c), :], w_ref[...])).astype(o_ref.dtype)

    def conv(o_ref):
        def conv_out(c):
            v = _silu(_dwconv3_rows(u_scr, c, slice(0, IN_TILE), cw_ref, cb_ref, seq_len))
            o_ref[rows(c), :] = v.astype(o_ref.dtype)
        for c in range(n_chunks):
            u_scr[_stage_rows(c), :] = _dot(h_scr[rows(c), :], w_ref[...])
            if c >= 1:
                conv_out(c - 1)
        conv_out(n_chunks - 1)

    @pl.when(j <= 1)
    def _():
        pointwise(_silu, lo_ref)

    @pl.when((j >= 2) & (j <= 5))
    def _():
        pointwise(_sigmoid, lo_ref)

    @pl.when(j == BC_TILE)
    def _():
        conv(lo_ref)

    @pl.when((j == N_LOW_TILES) | (j == N_LOW_TILES + 1))
    def _():
        conv(hi_ref)

    @pl.when(j >= N_LOW_TILES + MLA_TILE)
    def _():
        pointwise(lambda u: u, hi_ref)


def _in_proj(x2d, mod48, mod_row, norm_g, w_in_r, conv_w, conv_b, seq_len):
    t = x2d.shape[0]
    n_tiles = N_LOW_TILES + N_F32_TILES
    conv_idx = lambda r, j: (0, jnp.where(j == BC_TILE, 2, jnp.clip(j - N_LOW_TILES, 0, 1)))
    return pl.pallas_call(
        functools.partial(_in_kernel, seq_len=seq_len),
        grid=(t // ROW_GROUP, n_tiles),
        in_specs=[pl.BlockSpec((ROW_GROUP, D_MODEL), lambda r, j: (r, 0)),
                  pl.BlockSpec((None, 1, D_MODEL), lambda r, j: (mod_row(r) * 6 + 0, 0, 0)),
                  pl.BlockSpec((None, 1, D_MODEL), lambda r, j: (mod_row(r) * 6 + 1, 0, 0)),
                  pl.BlockSpec((1, D_MODEL), lambda r, j: (0, 0)),
                  pl.BlockSpec((D_MODEL, IN_TILE), lambda r, j: (0, j)),
                  pl.BlockSpec((3, IN_TILE), conv_idx),
                  pl.BlockSpec((1, IN_TILE), conv_idx)],
        out_specs=[pl.BlockSpec((ROW_GROUP, IN_TILE), lambda r, j: (r, jnp.minimum(j, N_LOW_TILES - 1))),
                   pl.BlockSpec((ROW_GROUP, IN_TILE), lambda r, j: (r, jnp.maximum(j - N_LOW_TILES, 0)))],
        out_shape=[jax.ShapeDtypeStruct((t, N_LOW_TILES * IN_TILE), BF16),
                   jax.ShapeDtypeStruct((t, N_F32_TILES * IN_TILE), F32)],
        scratch_shapes=[pltpu.VMEM((ROW_GROUP, D_MODEL), BF16),
                        pltpu.VMEM((ROW_GROUP + 2 * CONV_PAD, IN_TILE), F32)],
        compiler_params=_params("arbitrary", "arbitrary"),
        name="in_proj",
    )(x2d, mod48, mod48, norm_g, w_in_r, conv_w, conv_b)


def _mla_kernel(*refs, latent, has_ctx, seq_len):
    refs = list(refs)
    pm_ref, px_ref, qg_ref, kvg_ref, wuq_ref, wkt_ref, wv_ref = refs[:7]
    del refs[:7]
    if latent:
        cos_ref, sin_ref = refs[:2]
        del refs[:2]
    if has_ctx:
        cckv_ref, ckr_ref = refs[:2]
        del refs[:2]
    o_ref, ckv_ref, kr_ref, k_scr, v_scr = refs[:5]
    del refs[:5]
    if has_ctx:
        kc_scr, vc_scr = refs
    t = pl.program_id(1)
    n_nope = N_HEADS * QK_NOPE
    tq = ATTN_Q_TILE

    def put_keys(k_dst, v_dst, rows, ckv_n, kr_bf):
        ckv_bf = ckv_n.astype(BF16)
        v_dst[rows, :] = _dot(ckv_bf, wv_ref[...]).astype(BF16)
        kn_t = _dot_nt(wkt_ref[...], ckv_bf).astype(BF16)
        width = kr_bf.shape[1]
        eye = jnp.where(lax.broadcasted_iota(jnp.int32, (128, width), 0)
                        == lax.broadcasted_iota(jnp.int32, (128, width), 1), 1.0, 0.0).astype(BF16)
        kr_t = _dot_nt(eye, kr_bf).astype(BF16)
        for h in range(N_HEADS):
            base = h * ATTN_HEAD_COLS
            k_dst[base:base + QK_NOPE, rows] = kn_t[h * QK_NOPE:(h + 1) * QK_NOPE, :]
            k_dst[base + QK_NOPE:base + ATTN_HEAD_COLS, rows] = kr_t

    @pl.when(t == 0)
    def _():
        step = min(seq_len, TOKEN_TILE)
        for r0 in range(0, seq_len, step):
            rows = slice(r0, r0 + step)
            ckv_n = _rmsnorm(pm_ref[0, rows, Q_LORA:], kvg_ref[...])
            ckv_ref[0, rows, :] = ckv_n
            kr = px_ref[0, rows, 0:128]
            if latent:
                kr = kr * cos_ref[rows, :] + px_ref[0, rows, 128:256] * sin_ref[rows, :]
            kr_ref[0, rows, :] = kr[:, :QK_ROPE]
            put_keys(k_scr, v_scr, rows, ckv_n, kr.astype(BF16))
        if has_ctx:
            past = cckv_ref.shape[1]
            put_keys(kc_scr, vc_scr, slice(0, past), cckv_ref[0], ckr_ref[0].astype(BF16))

    qrows = pl.ds(pl.multiple_of(t * tq, tq), tq)
    scale = LOG2_E / math.sqrt(QK_NOPE + QK_ROPE)
    cqn = _rmsnorm(pm_ref[0, qrows, 0:Q_LORA], qg_ref[...]).astype(BF16)
    q = _dot(cqn, wuq_ref[...])
    q_rope = q[:, n_nope:2 * n_nope]
    if latent:
        q_rope = (q_rope * jnp.concatenate([cos_ref[qrows, :]] * N_HEADS, axis=1)
                  + q[:, 2 * n_nope:3 * n_nope] * jnp.concatenate([sin_ref[qrows, :]] * N_HEADS, axis=1))
    segs = ([(kc_scr, vc_scr)] if has_ctx else []) + [(k_scr, v_scr)]
    for h in range(N_HEADS):
        head = slice(h * QK_NOPE, (h + 1) * QK_NOPE)
        qk = slice(h * ATTN_HEAD_COLS, (h + 1) * ATTN_HEAD_COLS)
        qh = (jnp.concatenate([q[:, head], q_rope[:, head]], axis=1) * scale).astype(BF16)
        s = [_dot(qh, k[qk, :]) for k, _ in segs]
        m = functools.reduce(jnp.maximum, [jnp.max(si, axis=-1, keepdims=True) for si in s])
        p = [jnp.exp2(si - m) for si in s]
        l = functools.reduce(jnp.add, [jnp.sum(pi, axis=-1, keepdims=True) for pi in p])
        o = functools.reduce(jnp.add, [_dot(pi.astype(BF16), v[:, head]) for pi, (_, v) in zip(p, segs)])
        o_ref[0, :, head] = (o / l).astype(BF16)


def _mla(proj_hi, q_norm_g, kv_norm_g, w_uq_r, w_uk_t, w_uv, rope_tables, ctx):
    b, l, _ = proj_hi.shape
    tq = ATTN_Q_TILE
    latent = rope_tables is not None
    has_ctx = ctx is not None
    once = dict(pipeline_mode=pl.Buffered(1))
    per_batch = once if l // tq > 1 else {}
    const2 = lambda i, t: (0, 0)
    in_specs = [pl.BlockSpec((1, l, IN_TILE), lambda i, t: (i, 0, MLA_TILE), **per_batch),
                pl.BlockSpec((1, l, 256), lambda i, t: (i, 0, MISC_TILE * IN_TILE // 256), **per_batch),
                pl.BlockSpec((1, Q_LORA), const2),
                pl.BlockSpec((1, KV_LORA), const2),
                pl.BlockSpec(w_uq_r.shape, const2, **once),
                pl.BlockSpec(w_uk_t.shape, const2, **once),
                pl.BlockSpec(w_uv.shape, const2, **once)]
    args = [proj_hi, proj_hi, q_norm_g, kv_norm_g, w_uq_r, w_uk_t, w_uv]
    scratch = [pltpu.VMEM((N_HEADS * ATTN_HEAD_COLS, l), BF16), pltpu.VMEM((l, N_HEADS * V_HEAD), BF16)]
    if latent:
        in_specs += [pl.BlockSpec((l, 128), const2, **once)] * 2
        args += list(rope_tables)
    if has_ctx:
        past = ctx[0].shape[1]
        in_specs += [pl.BlockSpec((1, past, KV_LORA), lambda i, t: (i, 0, 0), **per_batch),
                     pl.BlockSpec((1, past, QK_ROPE), lambda i, t: (i, 0, 0), **per_batch)]
        args += list(ctx)
        scratch += [pltpu.VMEM((N_HEADS * ATTN_HEAD_COLS, past), BF16), pltpu.VMEM((past, N_HEADS * V_HEAD), BF16)]
    return pl.pallas_call(
        functools.partial(_mla_kernel, latent=latent, has_ctx=has_ctx, seq_len=l),
        grid=(b, l // tq),
        in_specs=in_specs,
        out_specs=[pl.BlockSpec((1, tq, N_HEADS * V_HEAD), lambda i, t: (i, t, 0)),
                   pl.BlockSpec((1, l, KV_LORA), lambda i, t: (i, 0, 0)),
                   pl.BlockSpec((1, l, QK_ROPE), lambda i, t: (i, 0, 0))],
        out_shape=[jax.ShapeDtypeStruct((b, l, N_HEADS * V_HEAD), BF16),
                   jax.ShapeDtypeStruct((b, l, KV_LORA), F32),
                   jax.ShapeDtypeStruct((b, l, QK_ROPE), F32)],
        scratch_shapes=scratch,
        compiler_params=_params("arbitrary", "arbitrary"),
        name="mla_attention",
    )(*args)


def _split3(x):
    hi = x.astype(BF16)
    r = x - hi.astype(F32)
    mid = r.astype(BF16)
    lo = (r - mid.astype(F32)).astype(BF16)
    return hi, mid, lo


def _exact_dot(parts, sel):
    return functools.reduce(jnp.add, [_dot(p, sel) for p in parts])


def _exact_dot_rows(sel, parts):
    return functools.reduce(jnp.add, [_dot(sel, p) for p in parts])


HEADS_PER_GROUP = SSD_HEADS // SSD_GROUPS
GROUP_COLS = HEADS_PER_GROUP * SSD_HEADDIM


def _ssd_kernel(*refs, nc, cps, has_h0):
    if has_h0:
        (xs_ref, zs_ref, bc_ref, dt_ref, dtb_ref, a_ref, dx_ref, h0_ref, y_ref, ht_ref,
         yl_scr, acum_scr, src_scr, tot_scr, sb_scr, h_scr, esel_scr) = refs
    else:
        (xs_ref, zs_ref, bc_ref, dt_ref, dtb_ref, a_ref, dx_ref, y_ref, ht_ref,
         yl_scr, acum_scr, src_scr, tot_scr, sb_scr, h_scr, esel_scr) = refs
    s = pl.program_id(1)
    q = SSD_CHUNK
    n_bc = SSD_GROUPS * SSD_STATE
    lane = lax.broadcasted_iota(jnp.int32, (q, 128), 1)
    low_half = lane < SSD_HEADDIM
    ii = lax.broadcasted_iota(jnp.int32, (q, q), 0)
    jj = lax.broadcasted_iota(jnp.int32, (q, q), 1)
    lower, upper = ii >= jj, ii <= jj

    def lane_bcast(parts, d):
        return _exact_dot(parts, esel_scr[d])

    def stacked_states(d):
        return [jnp.concatenate([h_scr[d, 2 * i], h_scr[d, 2 * i + 1]], axis=0).astype(BF16)
                for i in range(SSD_GROUPS // 2)]

    def group_c(bc, g):
        i, r = divmod(g, 2)
        cpair = bc[:, n_bc + i * 128:n_bc + (i + 1) * 128]
        return jnp.where(low_half if r == 0 else ~low_half, cpair, 0.0).astype(BF16)

    @pl.when(s == 0)
    def _():
        k = lax.broadcasted_iota(jnp.int32, (128, SSD_INNER), 0)
        head = lax.broadcasted_iota(jnp.int32, (128, SSD_INNER), 1) // SSD_HEADDIM
        for d in range(2):
            esel_scr[d] = jnp.where(k == d * SSD_HEADS + head, 1.0, 0.0).astype(BF16)
        if has_h0:
            for d in range(2):
                for g in range(SSD_GROUPS):
                    hpn = h0_ref[0, d, g * HEADS_PER_GROUP:(g + 1) * HEADS_PER_GROUP].reshape(GROUP_COLS, SSD_STATE)
                    h_scr[d, g] = hpn.T
        else:
            h_scr[...] = jnp.zeros(h_scr.shape, F32)
        tri_f = jnp.where(lower, 1.0, 0.0).astype(BF16)
        tri_b = jnp.where(upper, 1.0, 0.0).astype(BF16)
        fwd_col = lane < SSD_HEADS
        for c in range(nc):
            crow = slice(c * q, (c + 1) * q)
            dt = _softplus(dt_ref[0, crow, :] + dtb_ref[...])
            parts = _split3(dt * a_ref[...])
            acum = jnp.where(fwd_col, _exact_dot_rows(tri_f, parts),
                             _exact_dot_rows(tri_b, parts))
            acum = acum * LOG2_E
            acum_scr[crow, :] = acum
            tot = jnp.where(fwd_col[0:1], acum[q - 1:q, :], acum[0:1, :])
            tot_scr[c] = jnp.broadcast_to(tot, (8, 128))
            src_scr[c] = (acum - jnp.log2(dt)).T

    def first_sweep(c, blk):
        rows = pl.ds(pl.multiple_of(c * q, q), q)
        x = xs_ref[0, blk, :]
        bc = bc_ref[0, blk, :].astype(F32)
        acum = acum_scr[rows, :]
        src_t = src_scr[c]
        tot8 = tot_scr[c]
        e = jnp.exp2(acum)
        e_hi = e.astype(BF16)
        eb_f = lane_bcast([e_hi, (e - e_hi.astype(F32)).astype(BF16)], 0)
        cd_f = jnp.exp2(lane_bcast(_split3(tot8), 0))[0:1]
        b_t = [bc[:, i * 128:(i + 1) * 128].T for i in range(SSD_GROUPS // 2)]
        h_in = stacked_states(0)
        for g in range(SSD_GROUPS):
            i, r = divmod(g, 2)
            cm = group_c(bc, g)
            scores = _dot_nt(cm, bc[:, i * 128:(i + 1) * 128].astype(BF16))
            bg_t = b_t[i][r * SSD_STATE:(r + 1) * SSD_STATE, :]
            gcols = slice(g * GROUP_COLS, (g + 1) * GROUP_COLS)
            y_off = _dot(cm, h_in[i]) * eb_f[:, gcols]
            for t in range(HEADS_PER_GROUP // 2):
                pc = slice((2 * g + t) * 128, (2 * g + t + 1) * 128)
                xp = x[:, pc]
                x2 = jnp.concatenate([jnp.where(low_half, xp, 0.0), jnp.where(low_half, 0.0, xp)], axis=0).astype(BF16)
                m, sf, sb = [], [], []
                for u in range(2):
                    kf = g * HEADS_PER_GROUP + 2 * t + u
                    kb = SSD_HEADS + kf
                    af_col = jnp.broadcast_to(acum[:, kf:kf + 1], (q, q))
                    ab_col = jnp.broadcast_to(acum[:, kb:kb + 1], (q, q))
                    sf_row, sb_row = src_t[kf:kf + 1, :], src_t[kb:kb + 1, :]
                    decay = (jnp.exp2(jnp.where(lower, af_col - sf_row, NEG_BIG))
                             + jnp.exp2(jnp.where(upper, ab_col - sb_row, NEG_BIG)))
                    m.append((scores * decay).astype(BF16))
                    wf = jnp.exp2(tot8[0:1, kf:kf + 1] - sf_row)
                    wb = jnp.exp2(tot8[0:1, kb:kb + 1] - sb_row)
                    sf.append((bg_t * wf).astype(BF16))
                    sb.append((bg_t * wb).astype(BF16))
                y_pair = _dot(jnp.concatenate(m, axis=1), x2)
                tc = slice(t * 128, (t + 1) * 128)
                yl_scr[rows, pc] = y_pair + y_off[:, tc] + dx_ref[:, pc] * xp
                h_scr[0, g, :, tc] = h_scr[0, g, :, tc] * cd_f[:, pc] + _dot(jnp.concatenate(sf, axis=1), x2)
                sb_scr[c, g, :, tc] = _dot(jnp.concatenate(sb, axis=1), x2)

    def second_sweep(c, blk):
        rows = pl.ds(pl.multiple_of(c * q, q), q)
        bc = bc_ref[0, blk, :].astype(F32)
        e = jnp.exp2(acum_scr[rows, :])
        e_hi = e.astype(BF16)
        eb_b = lane_bcast([e_hi, (e - e_hi.astype(F32)).astype(BF16)], 1)
        cd_b = jnp.exp2(lane_bcast(_split3(tot_scr[c]), 1))[0:1]
        h_in = stacked_states(1)
        for g in range(SSD_GROUPS):
            gcols = slice(g * GROUP_COLS, (g + 1) * GROUP_COLS)
            y_off = _dot(group_c(bc, g), h_in[g // 2]) * eb_b[:, gcols]
            y_ref[0, blk, gcols] = ((yl_scr[rows, gcols] + y_off) * zs_ref[0, blk, gcols].astype(F32)).astype(y_ref.dtype)
            h_scr[1, g] = h_scr[1, g] * cd_b[:, gcols] + sb_scr[c, g]

    n_steps = nc // cps

    @pl.when(s < n_steps)
    def _():
        for ci in range(cps):
            first_sweep(s * cps + ci, slice(ci * q, (ci + 1) * q))

    @pl.when(s >= n_steps)
    def _():
        for ci in reversed(range(cps)):
            second_sweep((2 * n_steps - 1 - s) * cps + ci, slice(ci * q, (ci + 1) * q))

    @pl.when(s == 2 * n_steps - 1)
    def _():
        for d in range(2):
            for g in range(SSD_GROUPS):
                ht_ref[0, d, g * HEADS_PER_GROUP:(g + 1) * HEADS_PER_GROUP] = h_scr[d, g].T.reshape(
                    HEADS_PER_GROUP, SSD_HEADDIM, SSD_STATE)


def _ssd(proj_lo, proj_hi, h0, dt_bias128, a128, d_exp):
    b, l, _ = proj_hi.shape
    q = SSD_CHUNK
    nc = l // q
    cps = SSD_CHUNKS_PER_STEP
    n_steps = nc // cps
    rows = cps * q
    early = lambda s: jnp.minimum(s, n_steps - 1)
    both = lambda s: jnp.where(s < n_steps, s, 2 * n_steps - 1 - s)
    late = lambda s: jnp.where(s < n_steps, n_steps - 1, 2 * n_steps - 1 - s)
    st_shape = (1, 2, SSD_HEADS, SSD_HEADDIM, SSD_STATE)
    st_spec = pl.BlockSpec(st_shape, lambda i, s: (i, 0, 0, 0, 0))
    has_h0 = h0 is not None
    return pl.pallas_call(
        functools.partial(_ssd_kernel, nc=nc, cps=cps, has_h0=has_h0),
        grid=(b, 2 * n_steps),
        in_specs=[pl.BlockSpec((1, rows, SSD_INNER), lambda i, s: (i, early(s), XS_BLK)),
                  pl.BlockSpec((1, rows, SSD_INNER), lambda i, s: (i, late(s), Z_BLK)),
                  pl.BlockSpec((1, rows, IN_TILE), lambda i, s: (i, both(s), BC_TILE)),
                  pl.BlockSpec((1, l, 128), lambda i, s: (i, 0, DT_BLK)),
                  pl.BlockSpec((1, 128), lambda i, s: (0, 0)),
                  pl.BlockSpec((1, 128), lambda i, s: (0, 0)),
                  pl.BlockSpec((1, SSD_INNER), lambda i, s: (0, 0))] + ([st_spec] if has_h0 else []),
        out_specs=[pl.BlockSpec((1, rows, SSD_INNER), lambda i, s: (i, late(s), 0)),
                   pl.BlockSpec(st_shape, lambda i, s: (i, 0, 0, 0, 0))],
        out_shape=[jax.ShapeDtypeStruct((b, l, SSD_INNER), BF16),
                   jax.ShapeDtypeStruct((b,) + st_shape[1:], F32)],
        scratch_shapes=[pltpu.VMEM((l, SSD_INNER), F32),
                        pltpu.VMEM((l, 128), F32),
                        pltpu.VMEM((nc, 128, q), F32),
                        pltpu.VMEM((nc, 8, 128), F32),
                        pltpu.VMEM((nc, SSD_GROUPS, SSD_STATE, GROUP_COLS), F32),
                        pltpu.VMEM((2, SSD_GROUPS, SSD_STATE, GROUP_COLS), F32),
                        pltpu.VMEM((2, 128, SSD_INNER), BF16)],
        compiler_params=_params("arbitrary", "arbitrary"),
        name="ssd_scan",
    )(proj_hi, proj_lo, proj_lo, proj_hi, dt_bias128, a128, d_exp, *([h0] if has_h0 else []))


def _merge_kernel(attn_ref, yz_ref, gm_ref, gs_ref, x_ref, g1_ref, ng_ref, womla_ref, wossd_ref, wout_ref, o_ref, w_scr):
    @pl.when(pl.program_id(0) == 0)
    def _():
        w_scr[0] = womla_ref[...].astype(BF16)
        w_scr[1] = wossd_ref[...].astype(BF16)
        w_scr[2] = wout_ref[...].astype(BF16)

    o_mla = _dot(attn_ref[...], w_scr[0])
    o_ssd = _dot(_rmsnorm(yz_ref[...].astype(F32), ng_ref[...]).astype(BF16), w_scr[1])
    merged = gm_ref[...].astype(F32) * o_mla + gs_ref[...].astype(F32) * o_ssd
    o_ref[...] = x_ref[...] + g1_ref[...] * _dot(merged.astype(BF16), w_scr[2])


def _merge(attn2d, yz2d, proj, x2d, mod48, mod_row, ssd_norm_g, w_o_mla, w_o_ssd, w_out):
    t = x2d.shape[0]
    tm = TOKEN_TILE
    row = lambda i: (i, 0)
    const = lambda i: (0, 0)
    wspec = pl.BlockSpec((D_MODEL, D_MODEL), const, pipeline_mode=pl.Buffered(1))
    return pl.pallas_call(
        _merge_kernel,
        grid=(t // tm,),
        in_specs=[pl.BlockSpec((tm, D_MODEL), row),
                  pl.BlockSpec((tm, D_MODEL), row),
                  pl.BlockSpec((tm, D_MODEL), lambda i: (i, GM_BLK)),
                  pl.BlockSpec((tm, D_MODEL), lambda i: (i, GS_BLK)),
                  pl.BlockSpec((tm, D_MODEL), row),
                  pl.BlockSpec((None, 1, D_MODEL), lambda i: (mod_row(i * tm // ROW_GROUP) * 6 + 2, 0, 0)),
                  pl.BlockSpec((1, D_MODEL), const),
                  wspec, wspec, wspec],
        out_specs=pl.BlockSpec((tm, D_MODEL), row),
        out_shape=jax.ShapeDtypeStruct((t, D_MODEL), F32),
        scratch_shapes=[pltpu.VMEM((3, D_MODEL, D_MODEL), BF16)],
        compiler_params=_params("arbitrary"),
        name="merge_out",
    )(attn2d, yz2d, proj, proj, x2d, mod48, ssd_norm_g, w_o_mla, w_o_ssd, w_out)


def _ffn_kernel(x_ref, sh_ref, sc_ref, g2_ref, ng_ref, wg_ref, wv_ref, cwg_ref, cwv_ref, cbg_ref, cbv_ref, wd_ref,
                fg_ref, o_ref, h_scr, wup_scr, wd_scr, u_scr, *, seq_len):
    j = pl.program_id(1)
    n_chunks = ROW_GROUP // CONV_CHUNK
    rows = lambda c: slice(c * CONV_CHUNK, (c + 1) * CONV_CHUNK)

    @pl.when(j == 0)
    def _():
        _norm_mod_rows(x_ref, ng_ref, sc_ref, sh_ref, h_scr)
        _zero_conv_pads(u_scr)
        o_ref[...] = jnp.zeros_like(o_ref)

    wup_scr[:, 0:FFN_TILE] = wg_ref[...].astype(BF16)
    wup_scr[:, FFN_TILE:2 * FFN_TILE] = wv_ref[...].astype(BF16)
    wd_scr[...] = wd_ref[...].astype(BF16)

    def gated(c):
        ug = _dwconv3_rows(u_scr, c, slice(0, FFN_TILE), cwg_ref, cbg_ref, seq_len)
        uv = _dwconv3_rows(u_scr, c, slice(FFN_TILE, 2 * FFN_TILE), cwv_ref, cbv_ref, seq_len)
        return (_silu(ug) * uv).astype(BF16)

    act = {}
    for c in range(n_chunks + 2):
        if c < n_chunks:
            u_scr[_stage_rows(c), :] = _dot(h_scr[rows(c), :], wup_scr[...])
        if c >= 2:
            o_ref[rows(c - 2), :] += _dot(act.pop(c - 2), wd_scr[...])
        if 1 <= c <= n_chunks:
            act[c - 1] = gated(c - 1)

    @pl.when(j == pl.num_programs(1) - 1)
    def _():
        def body(i, carry):
            rows = pl.ds(pl.multiple_of(i * 256, 256), 256)
            o_ref[rows, :] = _rmsnorm(x_ref[rows, :] + g2_ref[...] * o_ref[rows, :], fg_ref[...])
            return carry
        lax.fori_loop(0, ROW_GROUP // 256, body, 0)


def _ffn(x2d, mod48, mod_row, norm_g, w_up, conv_w, conv_b, w_down, final_g, seq_len):
    t = x2d.shape[0]
    nj = D_FF // FFN_TILE
    gate = lambda r, j: (0, j)
    val = lambda r, j: (0, nj + j)
    const = lambda r, j: (0, 0)
    mod = lambda k: pl.BlockSpec((None, 1, D_MODEL), lambda r, j: (mod_row(r) * 6 + k, 0, 0))
    return pl.pallas_call(
        functools.partial(_ffn_kernel, seq_len=seq_len),
        grid=(t // ROW_GROUP, nj),
        in_specs=[pl.BlockSpec((ROW_GROUP, D_MODEL), lambda r, j: (r, 0)),
                  mod(3), mod(4), mod(5),
                  pl.BlockSpec((1, D_MODEL), const),
                  pl.BlockSpec((D_MODEL, FFN_TILE), gate),
                  pl.BlockSpec((D_MODEL, FFN_TILE), val),
                  pl.BlockSpec((3, FFN_TILE), gate),
                  pl.BlockSpec((3, FFN_TILE), val),
                  pl.BlockSpec((1, FFN_TILE), gate),
                  pl.BlockSpec((1, FFN_TILE), val),
                  pl.BlockSpec((FFN_TILE, D_MODEL), lambda r, j: (j, 0)),
                  pl.BlockSpec((1, D_MODEL), const)],
        out_specs=pl.BlockSpec((ROW_GROUP, D_MODEL), lambda r, j: (r, 0)),
        out_shape=jax.ShapeDtypeStruct((t, D_MODEL), F32),
        scratch_shapes=[pltpu.VMEM((ROW_GROUP, D_MODEL), BF16),
                        pltpu.VMEM((D_MODEL, 2 * FFN_TILE), BF16),
                        pltpu.VMEM((FFN_TILE, D_MODEL), BF16),
                        pltpu.VMEM((ROW_GROUP + 2 * CONV_PAD, 2 * FFN_TILE), F32)],
        compiler_params=_params("arbitrary", "arbitrary"),
        name="conv_ffn",
    )(x2d, mod48, mod48, mod48, norm_g, w_up, w_up, conv_w, conv_w, conv_b, conv_b, w_down, final_g)


def _rope_tables(seq_len):
    t = np.arange(seq_len)
    row = (t // GRID_W).astype(np.float32)
    col = (t % GRID_W).astype(np.float32)
    n = QK_ROPE // 4
    inv = (np.float32(ROPE_BASE) ** (-np.arange(n, dtype=np.float32) / np.float32(n))).astype(np.float32)
    ar, ac = row[:, None] * inv, col[:, None] * inv
    cos64 = np.concatenate([np.cos(ar), np.cos(ar), np.cos(ac), np.cos(ac)], axis=1)
    sin64 = np.concatenate([-np.sin(ar), np.sin(ar), -np.sin(ac), np.sin(ac)], axis=1)
    zeros = np.zeros_like(cos64)
    return (jnp.asarray(np.concatenate([cos64, zeros], axis=1), F32),
            jnp.asarray(np.concatenate([sin64, zeros], axis=1), F32))


def _swap_rope_halves(w):
    lead = w.shape[:-1]
    return w.reshape(lead + (2, 2, QK_ROPE // 4))[..., ::-1, :].reshape(lead + (QK_ROPE,))


def _trunk_pass(x, mod48, mod_row, wts, ctx, latent):
    b, l, _ = x.shape
    x2d = x.reshape(b * l, D_MODEL)
    proj_lo, proj_hi = _in_proj(x2d, mod48, mod_row, wts["norm_attn_g"], wts["w_in_r"], wts["ssd_conv_w"],
                                wts["ssd_conv_b"], l)
    rope = _rope_tables(l) if latent else None
    w_uq_r = wts["w_uq_lat"] if latent else wts["w_uq_ctx"]
    shape3 = lambda a, n: a.reshape(b, n, a.shape[-1])
    h0 = None
    mla_ctx = None
    if ctx is not None:
        cache_ckv, cache_krope, h0 = ctx
        mla_ctx = (cache_ckv, cache_krope)
    attn, ckv_n, kr3 = _mla(shape3(proj_hi, l), wts["q_norm_g"], wts["kv_norm_g"], w_uq_r, wts["w_uk_t"], wts["w_uv"],
                            rope, mla_ctx)
    yz, h_t = _ssd(shape3(proj_lo, l), shape3(proj_hi, l), h0, wts["dt_bias128"], wts["a128"], wts["d_exp"])
    x1 = _merge(attn.reshape(b * l, -1), yz.reshape(b * l, -1), proj_lo, x2d, mod48, mod_row, wts["ssd_norm_g"],
                wts["w_o_mla"], wts["w_o_ssd"], wts["w_out"])
    y = _ffn(x1, mod48, mod_row, wts["norm_ffn_g"], wts["w_up"], wts["ffn_conv_w"], wts["ffn_conv_b"], wts["w_down"],
             wts["final_norm_g"], l)
    return y.reshape(b, l, D_MODEL), ckv_n, kr3, h_t


def kernel(x_prompt, x_sample, c, cache_ckv, cache_krope, state_ssd, c_ctx, w_ada, b_ada, norm_attn_g, w_in, q_norm_g,
           kv_norm_g, w_uq, w_ukv, w_o_mla, ssd_conv_w, ssd_conv_b, ssd_dt_bias, ssd_A_log, ssd_D, ssd_norm_g, w_o_ssd,
           w_out, norm_ffn_g, w_up, ffn_conv_w, ffn_conv_b, w_down, final_norm_g):
    depth = w_in.shape[0]
    assert depth == 1, "single trunk layer"
    dec_b = x_sample.shape[0]
    assert x_sample.shape[1] == ROW_GROUP and ROW_GROUP % x_prompt.shape[1] == 0
    lyr = 0

    cvec = jnp.zeros((8, D_MODEL), F32).at[0].set(c_ctx).at[1:1 + dec_b].set(c)
    mod48 = _ada(cvec, w_ada[lyr], b_ada[lyr]).reshape(8 * 6, 1, D_MODEL)

    w_in_r = _regroup_w_in(w_in[lyr])
    wq = w_uq[lyr].reshape(Q_LORA, N_HEADS, QK_NOPE + QK_ROPE)
    wq_nope = wq[:, :, :QK_NOPE].reshape(Q_LORA, -1)
    wq_rope = wq[:, :, QK_NOPE:]
    pad_rope = lambda w: jnp.pad(w, ((0, 0), (0, 0), (0, 128 - QK_ROPE))).reshape(Q_LORA, -1)
    w_uq_ctx = jnp.concatenate([wq_nope, pad_rope(wq_rope)], axis=1).astype(BF16)
    w_uq_lat = jnp.concatenate([wq_nope, pad_rope(wq_rope), pad_rope(_swap_rope_halves(wq_rope))], axis=1).astype(BF16)
    wkv = w_ukv[lyr].reshape(KV_LORA, N_HEADS, QK_NOPE + V_HEAD)
    w_uk_t = wkv[:, :, :QK_NOPE].reshape(KV_LORA, -1).T.astype(BF16)
    w_uv = wkv[:, :, QK_NOPE:].reshape(KV_LORA, -1).astype(BF16)
    pad128 = lambda a: jnp.pad(a.reshape(1, -1), ((0, 0), (0, 128 - a.size)))
    wts = {
        "norm_attn_g": norm_attn_g[lyr].reshape(1, -1), "w_in_r": w_in_r,
        "ssd_conv_w": ssd_conv_w[lyr], "ssd_conv_b": ssd_conv_b[lyr].reshape(1, -1),
        "q_norm_g": q_norm_g[lyr].reshape(1, -1), "kv_norm_g": kv_norm_g[lyr].reshape(1, -1),
        "w_uq_ctx": w_uq_ctx, "w_uq_lat": w_uq_lat, "w_uk_t": w_uk_t, "w_uv": w_uv,
        "dt_bias128": pad128(ssd_dt_bias[lyr]), "a128": pad128(-jnp.exp(ssd_A_log[lyr])),
        "d_exp": jnp.repeat(ssd_D[lyr], SSD_HEADDIM).reshape(1, -1),
        "ssd_norm_g": ssd_norm_g[lyr].reshape(1, -1),
        "w_o_mla": w_o_mla[lyr], "w_o_ssd": w_o_ssd[lyr], "w_out": w_out[lyr],
        "norm_ffn_g": norm_ffn_g[lyr].reshape(1, -1), "w_up": w_up[lyr],
        "ffn_conv_w": ffn_conv_w[lyr], "ffn_conv_b": ffn_conv_b[lyr].reshape(1, -1),
        "w_down": w_down[lyr], "final_norm_g": final_norm_g.reshape(1, -1),
    }

    y_p, ckv_p, kr_p, st_p = _trunk_pass(x_prompt, mod48, lambda r: 0, wts, None, False)
    ctx = (cache_ckv[:, lyr], cache_krope[:, lyr], state_ssd[:, lyr])
    y_s, _, _, _ = _trunk_pass(x_sample, mod48, lambda r: 1 + r, wts, ctx, True)
    return y_p, y_s, ckv_p[:, None], kr_p[:, None], st_p[:, None]
```

```python
import functools
import math

import jax
import jax.numpy as jnp
import numpy as np
from jax import lax
from jax.experimental import pallas as pl
from jax.experimental.pallas import tpu as pltpu

F32 = jnp.float32
BF16 = jnp.bfloat16

D_MODEL = 1024
GRID_W = 64
N_HEADS = 8
QK_NOPE = 128
QK_ROPE = 64
V_HEAD = 128
Q_LORA = 256
KV_LORA = 256
ROPE_BASE = 10000.0
SSD_HEADS = 16
SSD_HEADDIM = 64
SSD_INNER = SSD_HEADS * SSD_HEADDIM
SSD_GROUPS = 4
SSD_STATE = 64
SSD_CHUNK = 128
D_FF = 2816
EPS = 1e-6

ROW_GROUP = 2048
IN_TILE = 512
SSD_CHUNKS_PER_STEP = 4
FFN_TILE = 256
TOKEN_TILE = 512
ATTN_Q_TILE = 256
CONV_CHUNK = 256
CONV_PAD = 8
VMEM_LIMIT = 56 * 1024 * 1024
NEG_BIG = -1e30
LOG2_E = 1.4426950408889634


def _sigmoid(x):
    return 1.0 / (1.0 + jnp.exp(-x))


def _silu(x):
    return x * _sigmoid(x)


def _softplus(x):
    e = jnp.exp(-jnp.abs(x))
    u = 1.0 + e
    log1p_e = jnp.where(u == 1.0, e, e * jnp.log(u) / jnp.where(u == 1.0, 1.0, u - 1.0))
    return jnp.maximum(x, 0.0) + log1p_e


def _rmsnorm(x, g):
    return x * lax.rsqrt(jnp.mean(x * x, axis=-1, keepdims=True) + EPS) * g


def _dot(a, b):
    return jnp.dot(a, b, preferred_element_type=F32)


def _dot_nt(a, b):
    return lax.dot_general(a, b, (((1,), (1,)), ((), ())), preferred_element_type=F32)


def _params(*sem):
    return pltpu.CompilerParams(dimension_semantics=sem, vmem_limit_bytes=VMEM_LIMIT)


def _norm_mod(x, g_ref, sc_ref, sh_ref):
    return (_rmsnorm(x, g_ref[...]) * (1.0 + sc_ref[...]) + sh_ref[...]).astype(BF16)


def _zero_conv_pads(u_scr):
    zeros = jnp.zeros((CONV_PAD, u_scr.shape[1]), F32)
    u_scr[0:CONV_PAD, :] = zeros
    u_scr[CONV_PAD + ROW_GROUP:2 * CONV_PAD + ROW_GROUP, :] = zeros


def _stage_rows(c):
    return slice(CONV_PAD + c * CONV_CHUNK, CONV_PAD + (c + 1) * CONV_CHUNK)


def _dwconv3_rows(u_scr, c, cols, w_ref, b_ref, seq_len):
    r0 = c * CONV_CHUNK
    base = CONV_PAD + r0
    width = cols.stop - cols.start
    prev = u_scr[base - 1:base - 1 + CONV_CHUNK, cols]
    cur = u_scr[base:base + CONV_CHUNK, cols]
    nxt = u_scr[base + 1:base + 1 + CONV_CHUNK, cols]
    pos = (lax.broadcasted_iota(jnp.int32, (CONV_CHUNK, width), 0) + r0) & (seq_len - 1)
    if r0 % seq_len == 0 or CONV_CHUNK > seq_len:
        prev = jnp.where(pos == 0, 0.0, prev)
    if (r0 + CONV_CHUNK) % seq_len == 0 or CONV_CHUNK > seq_len:
        nxt = jnp.where(pos == seq_len - 1, 0.0, nxt)
    return prev * w_ref[0:1, :] + cur * w_ref[1:2, :] + nxt * w_ref[2:3, :] + b_ref[...]


def _ada_kernel(c_ref, w_ref, b_ref, o_ref):
    a = _silu(c_ref[...]).astype(BF16)
    o_ref[...] = _dot(a, w_ref[...].astype(BF16)) + b_ref[...]


def _ada(cvec, w_ada, b_ada):
    tn = 1536
    return pl.pallas_call(
        _ada_kernel,
        grid=(6 * D_MODEL // tn,),
        in_specs=[pl.BlockSpec((8, D_MODEL), lambda j: (0, 0)),
                  pl.BlockSpec((D_MODEL, tn), lambda j: (0, j)),
                  pl.BlockSpec((1, tn), lambda j: (0, j))],
        out_specs=pl.BlockSpec((8, tn), lambda j: (0, j)),
        out_shape=jax.ShapeDtypeStruct((8, 6 * D_MODEL), F32),
        compiler_params=_params("arbitrary"),
        name="ada_mod",
    )(cvec, w_ada, b_ada.reshape(1, -1))


IN_SPLITS = (Q_LORA, KV_LORA, QK_ROPE, SSD_INNER, SSD_INNER, SSD_GROUPS * SSD_STATE, SSD_GROUPS * SSD_STATE,
             2 * SSD_HEADS, D_MODEL, D_MODEL)
IN_OFFSETS = tuple(int(v) for v in np.cumsum((0,) + IN_SPLITS))


def _regroup_kernel(w_ref, o_ref):
    dst = 0

    def put_block(block):
        nonlocal dst
        o_ref[:, dst:dst + 128] = block.T.astype(BF16)
        dst += 128

    def piece(i):
        for off in range(0, IN_SPLITS[i], 128):
            put_block(w_ref[IN_OFFSETS[i] + off:IN_OFFSETS[i] + off + 128, :])

    def padded(parts):
        n = sum(p.shape[0] for p in parts)
        put_block(jnp.concatenate(parts + [jnp.zeros((128 - n, w_ref.shape[1]), F32)], axis=0))

    for i in (3, 8, 9, 5, 6, 4, 0, 1):
        piece(i)
    kr0, n = IN_OFFSETS[2], QK_ROPE // 4
    padded([w_ref[kr0:kr0 + QK_ROPE, :]])
    padded([w_ref[kr0 + blk * n:kr0 + (blk + 1) * n, :] for blk in (1, 0, 3, 2)])
    padded([w_ref[IN_OFFSETS[7]:IN_OFFSETS[7] + IN_SPLITS[7], :]])
    o_ref[:, dst:] = jnp.zeros((o_ref.shape[0], o_ref.shape[1] - dst), BF16)


def _regroup_w_in(w_in_t):
    cols = 256
    n_out = (N_LOW_TILES + N_F32_TILES) * IN_TILE
    return pl.pallas_call(
        _regroup_kernel,
        grid=(D_MODEL // cols,),
        in_specs=[pl.BlockSpec((w_in_t.shape[0], cols), lambda i: (0, i))],
        out_specs=pl.BlockSpec((cols, n_out), lambda i: (i, 0)),
        out_shape=jax.ShapeDtypeStruct((D_MODEL, n_out), BF16),
        compiler_params=_params("arbitrary"),
        name="w_in_regroup",
    )(w_in_t)


N_LOW_TILES, N_F32_TILES = 7, 4
Z_BLK, GM_BLK, GS_BLK = 0, 1, 2
BC_TILE = 6
XS_BLK = 0
MLA_TILE, MISC_TILE = 2, 3
DT_BLK = (MISC_TILE * IN_TILE + 256) // 128
ATTN_HEAD_COLS = 256


def _in_kernel(x_ref, sh_ref, sc_ref, g_ref, w_ref, cw_ref, cb_ref, lo_ref, hi_ref, h_scr, u_scr, *, seq_len):
    j = pl.program_id(1)
    n_chunks = ROW_GROUP // CONV_CHUNK
    rows = lambda c: slice(c * CONV_CHUNK, (c + 1) * CONV_CHUNK)

    def pointwise(fn, o_ref, stage_h=False):
        for c in range(n_chunks):
            if stage_h:
                h_scr[rows(c), :] = _norm_mod(x_ref[rows(c), :], g_ref, sc_ref, sh_ref)
            o_ref[rows(c), :] = fn(_dot(h_scr[rows(c), :], w_ref[...])).astype(o_ref.dtype)

    def conv(o_ref):
        def conv_out(c):
            v = _silu(_dwconv3_rows(u_scr, c, slice(0, IN_TILE), cw_ref, cb_ref, seq_len))
            o_ref[rows(c), :] = v.astype(o_ref.dtype)
        for c in range(n_chunks):
            u_scr[_stage_rows(c), :] = _dot(h_scr[rows(c), :], w_ref[...])
            if c >= 1:
                conv_out(c - 1)
        conv_out(n_chunks - 1)

    @pl.when(j == 0)
    def _():
        _zero_conv_pads(u_scr)
        pointwise(_silu, lo_ref, stage_h=True)

    @pl.when(j == 1)
    def _():
        pointwise(_silu, lo_ref)

    @pl.when((j >= 2) & (j <= 5))
    def _():
        pointwise(_sigmoid, lo_ref)

    @pl.when(j == BC_TILE)
    def _():
        conv(lo_ref)

    @pl.when((j == N_LOW_TILES) | (j == N_LOW_TILES + 1))
    def _():
        conv(hi_ref)

    @pl.when(j >= N_LOW_TILES + MLA_TILE)
    def _():
        pointwise(lambda u: u, hi_ref)


def _in_proj(x2d, mod48, mod_row, norm_g, w_in_r, conv_w, conv_b, seq_len):
    t = x2d.shape[0]
    n_tiles = N_LOW_TILES + N_F32_TILES
    conv_idx = lambda r, j: (0, jnp.where(j == BC_TILE, 2, jnp.clip(j - N_LOW_TILES, 0, 1)))
    return pl.pallas_call(
        functools.partial(_in_kernel, seq_len=seq_len),
        grid=(t // ROW_GROUP, n_tiles),
        in_specs=[pl.BlockSpec((ROW_GROUP, D_MODEL), lambda r, j: (r, 0)),
                  pl.BlockSpec((None, 1, D_MODEL), lambda r, j: (mod_row(r) * 6 + 0, 0, 0)),
                  pl.BlockSpec((None, 1, D_MODEL), lambda r, j: (mod_row(r) * 6 + 1, 0, 0)),
                  pl.BlockSpec((1, D_MODEL), lambda r, j: (0, 0)),
                  pl.BlockSpec((D_MODEL, IN_TILE), lambda r, j: (0, j)),
                  pl.BlockSpec((3, IN_TILE), conv_idx),
                  pl.BlockSpec((1, IN_TILE), conv_idx)],
        out_specs=[pl.BlockSpec((ROW_GROUP, IN_TILE), lambda r, j: (r, jnp.minimum(j, N_LOW_TILES - 1))),
                   pl.BlockSpec((ROW_GROUP, IN_TILE), lambda r, j: (r, jnp.maximum(j - N_LOW_TILES, 0)))],
        out_shape=[jax.ShapeDtypeStruct((t, N_LOW_TILES * IN_TILE), BF16),
                   jax.ShapeDtypeStruct((t, N_F32_TILES * IN_TILE), F32)],
        scratch_shapes=[pltpu.VMEM((ROW_GROUP, D_MODEL), BF16),
                        pltpu.VMEM((ROW_GROUP + 2 * CONV_PAD, IN_TILE), F32)],
        compiler_params=_params("arbitrary", "arbitrary"),
        name="in_proj",
    )(x2d, mod48, mod48, norm_g, w_in_r, conv_w, conv_b)


def _mla_kernel(*refs, latent, has_ctx, seq_len):
    refs = list(refs)
    pm_ref, px_ref, qg_ref, kvg_ref, wuq_ref, wkt_ref, wv_ref = refs[:7]
    del refs[:7]
    if latent:
        cos_ref, sin_ref = refs[:2]
        del refs[:2]
    if has_ctx:
        cckv_ref, ckr_ref = refs[:2]
        del refs[:2]
    o_ref, ckv_ref, kr_ref, k_scr, v_scr = refs[:5]
    del refs[:5]
    if has_ctx:
        kc_scr, vc_scr = refs
    t = pl.program_id(1)
    n_nope = N_HEADS * QK_NOPE
    tq = ATTN_Q_TILE

    def put_keys(k_dst, v_dst, rows, ckv_n, kr_bf):
        ckv_bf = ckv_n.astype(BF16)
        v_dst[rows, :] = _dot(ckv_bf, wv_ref[...]).astype(BF16)
        kn_t = _dot_nt(wkt_ref[...], ckv_bf).astype(BF16)
        width = kr_bf.shape[1]
        eye = jnp.where(lax.broadcasted_iota(jnp.int32, (128, width), 0)
                        == lax.broadcasted_iota(jnp.int32, (128, width), 1), 1.0, 0.0).astype(BF16)
        kr_t = _dot_nt(eye, kr_bf).astype(BF16)
        for h in range(N_HEADS):
            base = h * ATTN_HEAD_COLS
            k_dst[base:base + QK_NOPE, rows] = kn_t[h * QK_NOPE:(h + 1) * QK_NOPE, :]
            k_dst[base + QK_NOPE:base + ATTN_HEAD_COLS, rows] = kr_t

    @pl.when(t == 0)
    def _():
        step = min(seq_len, TOKEN_TILE)
        for r0 in range(0, seq_len, step):
            rows = slice(r0, r0 + step)
            ckv_n = _rmsnorm(pm_ref[0, rows, Q_LORA:], kvg_ref[...])
            ckv_ref[0, rows, :] = ckv_n
            kr = px_ref[0, rows, 0:128]
            if latent:
                kr = kr * cos_ref[rows, :] + px_ref[0, rows, 128:256] * sin_ref[rows, :]
            kr_ref[0, rows, :] = kr[:, :QK_ROPE]
            put_keys(k_scr, v_scr, rows, ckv_n, kr.astype(BF16))
        if has_ctx:
            past = cckv_ref.shape[1]
            put_keys(kc_scr, vc_scr, slice(0, past), cckv_ref[0], ckr_ref[0].astype(BF16))

    qrows = pl.ds(pl.multiple_of(t * tq, tq), tq)
    scale = LOG2_E / math.sqrt(QK_NOPE + QK_ROPE)
    cqn = _rmsnorm(pm_ref[0, qrows, 0:Q_LORA], qg_ref[...]).astype(BF16)
    q = _dot(cqn, wuq_ref[...])
    q_rope = q[:, n_nope:2 * n_nope]
    if latent:
        q_rope = (q_rope * jnp.concatenate([cos_ref[qrows, :]] * N_HEADS, axis=1)
                  + q[:, 2 * n_nope:3 * n_nope] * jnp.concatenate([sin_ref[qrows, :]] * N_HEADS, axis=1))
    segs = ([(kc_scr, vc_scr)] if has_ctx else []) + [(k_scr, v_scr)]
    for h in range(N_HEADS):
        head = slice(h * QK_NOPE, (h + 1) * QK_NOPE)
        qk = slice(h * ATTN_HEAD_COLS, (h + 1) * ATTN_HEAD_COLS)
        qh = (jnp.concatenate([q[:, head], q_rope[:, head]], axis=1) * scale).astype(BF16)
        s = [_dot(qh, k[qk, :]) for k, _ in segs]
        m = functools.reduce(jnp.maximum, [jnp.max(si, axis=-1, keepdims=True) for si in s])
        p = [jnp.exp2(si - m) for si in s]
        l = functools.reduce(jnp.add, [jnp.sum(pi, axis=-1, keepdims=True) for pi in p])
        o = functools.reduce(jnp.add, [_dot(pi.astype(BF16), v[:, head]) for pi, (_, v) in zip(p, segs)])
        o_ref[0, :, head] = (o / l).astype(BF16)


def _mla(proj_hi, q_norm_g, kv_norm_g, w_uq_r, w_uk_t, w_uv, rope_tables, ctx):
    b, l, _ = proj_hi.shape
    tq = ATTN_Q_TILE
    latent = rope_tables is not None
    has_ctx = ctx is not None
    once = dict(pipeline_mode=pl.Buffered(1))
    per_batch = once if l // tq > 1 else {}
    const2 = lambda i, t: (0, 0)
    in_specs = [pl.BlockSpec((1, l, IN_TILE), lambda i, t: (i, 0, MLA_TILE), **per_batch),
                pl.BlockSpec((1, l, 256), lambda i, t: (i, 0, MISC_TILE * IN_TILE // 256), **per_batch),
                pl.BlockSpec((1, Q_LORA), const2),
                pl.BlockSpec((1, KV_LORA), const2),
                pl.BlockSpec(w_uq_r.shape, const2, **once),
                pl.BlockSpec(w_uk_t.shape, const2, **once),
                pl.BlockSpec(w_uv.shape, const2, **once)]
    args = [proj_hi, proj_hi, q_norm_g, kv_norm_g, w_uq_r, w_uk_t, w_uv]
    scratch = [pltpu.VMEM((N_HEADS * ATTN_HEAD_COLS, l), BF16), pltpu.VMEM((l, N_HEADS * V_HEAD), BF16)]
    if latent:
        in_specs += [pl.BlockSpec((l, 128), const2, **once)] * 2
        args += list(rope_tables)
    if has_ctx:
        past = ctx[0].shape[1]
        in_specs += [pl.BlockSpec((1, past, KV_LORA), lambda i, t: (i, 0, 0), **per_batch),
                     pl.BlockSpec((1, past, QK_ROPE), lambda i, t: (i, 0, 0), **per_batch)]
        args += list(ctx)
        scratch += [pltpu.VMEM((N_HEADS * ATTN_HEAD_COLS, past), BF16), pltpu.VMEM((past, N_HEADS * V_HEAD), BF16)]
    return pl.pallas_call(
        functools.partial(_mla_kernel, latent=latent, has_ctx=has_ctx, seq_len=l),
        grid=(b, l // tq),
        in_specs=in_specs,
        out_specs=[pl.BlockSpec((1, tq, N_HEADS * V_HEAD), lambda i, t: (i, t, 0)),
                   pl.BlockSpec((1, l, KV_LORA), lambda i, t: (i, 0, 0)),
                   pl.BlockSpec((1, l, QK_ROPE), lambda i, t: (i, 0, 0))],
        out_shape=[jax.ShapeDtypeStruct((b, l, N_HEADS * V_HEAD), BF16),
                   jax.ShapeDtypeStruct((b, l, KV_LORA), F32),
                   jax.ShapeDtypeStruct((b, l, QK_ROPE), F32)],
        scratch_shapes=scratch,
        compiler_params=_params("arbitrary", "arbitrary"),
        name="mla_attention",
    )(*args)


def _split3(x):
    hi = x.astype(BF16)
    r = x - hi.astype(F32)
    mid = r.astype(BF16)
    lo = (r - mid.astype(F32)).astype(BF16)
    return hi, mid, lo


def _exact_dot(parts, sel):
    return functools.reduce(jnp.add, [_dot(p, sel) for p in parts])


def _exact_dot_rows(sel, parts):
    return functools.reduce(jnp.add, [_dot(sel, p) for p in parts])


HEADS_PER_GROUP = SSD_HEADS // SSD_GROUPS
GROUP_COLS = HEADS_PER_GROUP * SSD_HEADDIM


def _ssd_kernel(*refs, nc, cps, has_h0):
    if has_h0:
        (xs_ref, zs_ref, bc_ref, dt_ref, dtb_ref, a_ref, dx_ref, h0_ref, y_ref, ht_ref,
         yl_scr, acum_scr, src_scr, tot_scr, sb_scr, h_scr, esel_scr) = refs
    else:
        (xs_ref, zs_ref, bc_ref, dt_ref, dtb_ref, a_ref, dx_ref, y_ref, ht_ref,
         yl_scr, acum_scr, src_scr, tot_scr, sb_scr, h_scr, esel_scr) = refs
    s = pl.program_id(1)
    q = SSD_CHUNK
    n_bc = SSD_GROUPS * SSD_STATE
    lane = lax.broadcasted_iota(jnp.int32, (q, 128), 1)
    low_half = lane < SSD_HEADDIM
    ii = lax.broadcasted_iota(jnp.int32, (q, q), 0)
    jj = lax.broadcasted_iota(jnp.int32, (q, q), 1)
    lower, upper = ii >= jj, ii <= jj

    def lane_bcast(parts, d):
        return _exact_dot(parts, esel_scr[d])

    def stacked_states(d):
        return [jnp.concatenate([h_scr[d, 2 * i], h_scr[d, 2 * i + 1]], axis=0).astype(BF16)
                for i in range(SSD_GROUPS // 2)]

    def group_c(bc, g):
        i, r = divmod(g, 2)
        cpair = bc[:, n_bc + i * 128:n_bc + (i + 1) * 128]
        return jnp.where(low_half if r == 0 else ~low_half, cpair, 0.0).astype(BF16)

    @pl.when(s == 0)
    def _():
        k = lax.broadcasted_iota(jnp.int32, (128, SSD_INNER), 0)
        head = lax.broadcasted_iota(jnp.int32, (128, SSD_INNER), 1) // SSD_HEADDIM
        for d in range(2):
            esel_scr[d] = jnp.where(k == d * SSD_HEADS + head, 1.0, 0.0).astype(BF16)
        if has_h0:
            for d in range(2):
                for g in range(SSD_GROUPS):
                    hpn = h0_ref[0, d, g * HEADS_PER_GROUP:(g + 1) * HEADS_PER_GROUP].reshape(GROUP_COLS, SSD_STATE)
                    h_scr[d, g] = hpn.T
        else:
            h_scr[...] = jnp.zeros(h_scr.shape, F32)
        tri_f = jnp.where(lower, 1.0, 0.0).astype(BF16)
        tri_b = jnp.where(upper, 1.0, 0.0).astype(BF16)
        fwd_col = lane < SSD_HEADS
        for c in range(nc):
            crow = slice(c * q, (c + 1) * q)
            dt = _softplus(dt_ref[0, crow, :] + dtb_ref[...])
            parts = _split3(dt * a_ref[...])
            acum = jnp.where(fwd_col, _exact_dot_rows(tri_f, parts),
                             _exact_dot_rows(tri_b, parts))
            acum = acum * LOG2_E
            acum_scr[crow, :] = acum
            tot = jnp.where(fwd_col[0:1], acum[q - 1:q, :], acum[0:1, :])
            tot_scr[c] = jnp.broadcast_to(tot, (8, 128))
            src_scr[c] = (acum - jnp.log2(dt)).T

    def first_sweep(c, blk):
        rows = pl.ds(pl.multiple_of(c * q, q), q)
        x = xs_ref[0, blk, :]
        bc = bc_ref[0, blk, :].astype(F32)
        acum = acum_scr[rows, :]
        src_t = src_scr[c]
        tot8 = tot_scr[c]
        e = jnp.exp2(acum)
        e_hi = e.astype(BF16)
        eb_f = lane_bcast([e_hi, (e - e_hi.astype(F32)).astype(BF16)], 0)
        cd_f = jnp.exp2(lane_bcast(_split3(tot8), 0))[0:1]
        b_t = [bc[:, i * 128:(i + 1) * 128].T for i in range(SSD_GROUPS // 2)]
        h_in = stacked_states(0)
        for g in range(SSD_GROUPS):
            i, r = divmod(g, 2)
            cm = group_c(bc, g)
            scores = _dot_nt(cm, bc[:, i * 128:(i + 1) * 128].astype(BF16))
            bg_t = b_t[i][r * SSD_STATE:(r + 1) * SSD_STATE, :]
            gcols = slice(g * GROUP_COLS, (g + 1) * GROUP_COLS)
            y_off = _dot(cm, h_in[i]) * eb_f[:, gcols]
            for t in range(HEADS_PER_GROUP // 2):
                pc = slice((2 * g + t) * 128, (2 * g + t + 1) * 128)
                xp = x[:, pc]
                x2 = jnp.concatenate([jnp.where(low_half, xp, 0.0), jnp.where(low_half, 0.0, xp)], axis=0).astype(BF16)
                m, sf, sb = [], [], []
                for u in range(2):
                    kf = g * HEADS_PER_GROUP + 2 * t + u
                    kb = SSD_HEADS + kf
                    af_col = jnp.broadcast_to(acum[:, kf:kf + 1], (q, q))
                    ab_col = jnp.broadcast_to(acum[:, kb:kb + 1], (q, q))
                    sf_row, sb_row = src_t[kf:kf + 1, :], src_t[kb:kb + 1, :]
                    decay = (jnp.exp2(jnp.where(lower, af_col - sf_row, NEG_BIG))
                             + jnp.exp2(jnp.where(upper, ab_col - sb_row, NEG_BIG)))
                    m.append((scores * decay).astype(BF16))
                    wf = jnp.exp2(tot8[0:1, kf:kf + 1] - sf_row)
                    wb = jnp.exp2(tot8[0:1, kb:kb + 1] - sb_row)
                    sf.append((bg_t * wf).astype(BF16))
                    sb.append((bg_t * wb).astype(BF16))
                y_pair = _dot(jnp.concatenate(m, axis=1), x2)
                tc = slice(t * 128, (t + 1) * 128)
                yl_scr[rows, pc] = y_pair + y_off[:, tc] + dx_ref[:, pc] * xp
                h_scr[0, g, :, tc] = h_scr[0, g, :, tc] * cd_f[:, pc] + _dot(jnp.concatenate(sf, axis=1), x2)
                sb_scr[c, g, :, tc] = _dot(jnp.concatenate(sb, axis=1), x2)

    def second_sweep(c, blk):
        rows = pl.ds(pl.multiple_of(c * q, q), q)
        bc = bc_ref[0, blk, :].astype(F32)
        e = jnp.exp2(acum_scr[rows, :])
        e_hi = e.astype(BF16)
        eb_b = lane_bcast([e_hi, (e - e_hi.astype(F32)).astype(BF16)], 1)
        cd_b = jnp.exp2(lane_bcast(_split3(tot_scr[c]), 1))[0:1]
        h_in = stacked_states(1)
        for g in range(SSD_GROUPS):
            gcols = slice(g * GROUP_COLS, (g + 1) * GROUP_COLS)
            y_off = _dot(group_c(bc, g), h_in[g // 2]) * eb_b[:, gcols]
            y_ref[0, blk, gcols] = ((yl_scr[rows, gcols] + y_off) * zs_ref[0, blk, gcols].astype(F32)).astype(y_ref.dtype)
            h_scr[1, g] = h_scr[1, g] * cd_b[:, gcols] + sb_scr[c, g]

    n_steps = nc // cps
    one_step = n_steps == 1

    @pl.when(s < n_steps)
    def _():
        for ci in range(cps):
            first_sweep(s * cps + ci, slice(ci * q, (ci + 1) * q))

    @pl.when(jnp.logical_or(one_step, s >= n_steps))
    def _():
        blk_id = s * 0 if one_step else 2 * n_steps - 1 - s
        for ci in reversed(range(cps)):
            second_sweep(blk_id * cps + ci, slice(ci * q, (ci + 1) * q))

    @pl.when(jnp.logical_or(one_step, s == 2 * n_steps - 1))
    def _():
        for d in range(2):
            for g in range(SSD_GROUPS):
                ht_ref[0, d, g * HEADS_PER_GROUP:(g + 1) * HEADS_PER_GROUP] = h_scr[d, g].T.reshape(
                    HEADS_PER_GROUP, SSD_HEADDIM, SSD_STATE)


def _ssd(proj_lo, proj_hi, h0, dt_bias128, a128, d_exp):
    b, l, _ = proj_hi.shape
    q = SSD_CHUNK
    nc = l // q
    cps = min(nc, SSD_CHUNKS_PER_STEP)
    n_steps = nc // cps
    n_grid = 1 if n_steps == 1 else 2 * n_steps
    rows = cps * q
    early = lambda s: jnp.minimum(s, n_steps - 1)
    both = lambda s: jnp.where(s < n_steps, s, 2 * n_steps - 1 - s)
    late = lambda s: jnp.where(s < n_steps, n_steps - 1, 2 * n_steps - 1 - s)
    st_shape = (1, 2, SSD_HEADS, SSD_HEADDIM, SSD_STATE)
    st_spec = pl.BlockSpec(st_shape, lambda i, s: (i, 0, 0, 0, 0))
    has_h0 = h0 is not None
    return pl.pallas_call(
        functools.partial(_ssd_kernel, nc=nc, cps=cps, has_h0=has_h0),
        grid=(b, n_grid),
        in_specs=[pl.BlockSpec((1, rows, SSD_INNER), lambda i, s: (i, early(s), XS_BLK)),
                  pl.BlockSpec((1, rows, SSD_INNER), lambda i, s: (i, late(s), Z_BLK)),
                  pl.BlockSpec((1, rows, IN_TILE), lambda i, s: (i, both(s), BC_TILE)),
                  pl.BlockSpec((1, l, 128), lambda i, s: (i, 0, DT_BLK)),
                  pl.BlockSpec((1, 128), lambda i, s: (0, 0)),
                  pl.BlockSpec((1, 128), lambda i, s: (0, 0)),
                  pl.BlockSpec((1, SSD_INNER), lambda i, s: (0, 0))] + ([st_spec] if has_h0 else []),
        out_specs=[pl.BlockSpec((1, rows, SSD_INNER), lambda i, s: (i, late(s), 0)),
                   pl.BlockSpec(st_shape, lambda i, s: (i, 0, 0, 0, 0))],
        out_shape=[jax.ShapeDtypeStruct((b, l, SSD_INNER), BF16),
                   jax.ShapeDtypeStruct((b,) + st_shape[1:], F32)],
        scratch_shapes=[pltpu.VMEM((l, SSD_INNER), F32),
                        pltpu.VMEM((l, 128), F32),
                        pltpu.VMEM((nc, 128, q), F32),
                        pltpu.VMEM((nc, 8, 128), F32),
                        pltpu.VMEM((nc, SSD_GROUPS, SSD_STATE, GROUP_COLS), F32),
                        pltpu.VMEM((2, SSD_GROUPS, SSD_STATE, GROUP_COLS), F32),
                        pltpu.VMEM((2, 128, SSD_INNER), BF16)],
        compiler_params=_params("arbitrary", "arbitrary"),
        name="ssd_scan",
    )(proj_hi, proj_lo, proj_lo, proj_hi, dt_bias128, a128, d_exp, *([h0] if has_h0 else []))


def _merge_kernel(attn_ref, yz_ref, gm_ref, gs_ref, x_ref, g1_ref, ng_ref, womla_ref, wossd_ref, wout_ref, o_ref, w_scr):
    @pl.when(pl.program_id(0) == 0)
    def _():
        w_scr[0] = womla_ref[...].astype(BF16)
        w_scr[1] = wossd_ref[...].astype(BF16)
        w_scr[2] = wout_ref[...].astype(BF16)

    o_mla = _dot(attn_ref[...], w_scr[0])
    o_ssd = _dot(_rmsnorm(yz_ref[...].astype(F32), ng_ref[...]).astype(BF16), w_scr[1])
    merged = gm_ref[...].astype(F32) * o_mla + gs_ref[...].astype(F32) * o_ssd
    o_ref[...] = x_ref[...] + g1_ref[...] * _dot(merged.astype(BF16), w_scr[2])


def _merge(attn2d, yz2d, proj, x2d, mod48, mod_row, ssd_norm_g, w_o_mla, w_o_ssd, w_out):
    t = x2d.shape[0]
    tm = TOKEN_TILE
    row = lambda i: (i, 0)
    const = lambda i: (0, 0)
    wspec = pl.BlockSpec((D_MODEL, D_MODEL), const, pipeline_mode=pl.Buffered(1))
    return pl.pallas_call(
        _merge_kernel,
        grid=(t // tm,),
        in_specs=[pl.BlockSpec((tm, D_MODEL), row),
                  pl.BlockSpec((tm, D_MODEL), row),
                  pl.BlockSpec((tm, D_MODEL), lambda i: (i, GM_BLK)),
                  pl.BlockSpec((tm, D_MODEL), lambda i: (i, GS_BLK)),
                  pl.BlockSpec((tm, D_MODEL), row),
                  pl.BlockSpec((None, 1, D_MODEL), lambda i: (mod_row(i * tm // ROW_GROUP) * 6 + 2, 0, 0)),
                  pl.BlockSpec((1, D_MODEL), const),
                  wspec, wspec, wspec],
        out_specs=pl.BlockSpec((tm, D_MODEL), row),
        out_shape=jax.ShapeDtypeStruct((t, D_MODEL), F32),
        scratch_shapes=[pltpu.VMEM((3, D_MODEL, D_MODEL), BF16)],
        compiler_params=_params("arbitrary"),
        name="merge_out",
    )(attn2d, yz2d, proj, proj, x2d, mod48, ssd_norm_g, w_o_mla, w_o_ssd, w_out)


def _ffn_kernel(x_ref, sh_ref, sc_ref, g2_ref, ng_ref, wg_ref, wv_ref, cwg_ref, cwv_ref, cbg_ref, cbv_ref, wd_ref,
                fg_ref, o_ref, h_scr, wup_scr, wd_scr, u_scr, *, seq_len):
    j = pl.program_id(1)
    n_chunks = ROW_GROUP // CONV_CHUNK
    rows = lambda c: slice(c * CONV_CHUNK, (c + 1) * CONV_CHUNK)

    wup_scr[:, 0:FFN_TILE] = wg_ref[...].astype(BF16)
    wup_scr[:, FFN_TILE:2 * FFN_TILE] = wv_ref[...].astype(BF16)
    wd_scr[...] = wd_ref[...].astype(BF16)

    def gated(c):
        ug = _dwconv3_rows(u_scr, c, slice(0, FFN_TILE), cwg_ref, cbg_ref, seq_len)
        uv = _dwconv3_rows(u_scr, c, slice(FFN_TILE, 2 * FFN_TILE), cwv_ref, cbv_ref, seq_len)
        return (_silu(ug) * uv).astype(BF16)

    def pipeline(first, last):
        act = {}
        for c in range(n_chunks + 2):
            if c < n_chunks:
                if first:
                    h_scr[rows(c), :] = _norm_mod(x_ref[rows(c), :], ng_ref, sc_ref, sh_ref)
                u_scr[_stage_rows(c), :] = _dot(h_scr[rows(c), :], wup_scr[...])
            if c >= 2:
                r = rows(c - 2)
                acc = _dot(act.pop(c - 2), wd_scr[...])
                if not first:
                    acc = o_ref[r, :] + acc
                if last:
                    acc = _rmsnorm(x_ref[r, :] + g2_ref[...] * acc, fg_ref[...])
                o_ref[r, :] = acc
            if 1 <= c <= n_chunks:
                act[c - 1] = gated(c - 1)

    last_j = pl.num_programs(1) - 1

    @pl.when(j == 0)
    def _():
        _zero_conv_pads(u_scr)
        pipeline(True, False)

    @pl.when((j > 0) & (j < last_j))
    def _():
        pipeline(False, False)

    @pl.when(j == last_j)
    def _():
        pipeline(False, True)


def _ffn(x2d, mod48, mod_row, norm_g, w_up, conv_w, conv_b, w_down, final_g, seq_len):
    t = x2d.shape[0]
    nj = D_FF // FFN_TILE
    gate = lambda r, j: (0, j)
    val = lambda r, j: (0, nj + j)
    const = lambda r, j: (0, 0)
    mod = lambda k: pl.BlockSpec((None, 1, D_MODEL), lambda r, j: (mod_row(r) * 6 + k, 0, 0))
    return pl.pallas_call(
        functools.partial(_ffn_kernel, seq_len=seq_len),
        grid=(t // ROW_GROUP, nj),
        in_specs=[pl.BlockSpec((ROW_GROUP, D_MODEL), lambda r, j: (r, 0)),
                  mod(3), mod(4), mod(5),
                  pl.BlockSpec((1, D_MODEL), const),
                  pl.BlockSpec((D_MODEL, FFN_TILE), gate),
                  pl.BlockSpec((D_MODEL, FFN_TILE), val),
                  pl.BlockSpec((3, FFN_TILE), gate),
                  pl.BlockSpec((3, FFN_TILE), val),
                  pl.BlockSpec((1, FFN_TILE), gate),
                  pl.BlockSpec((1, FFN_TILE), val),
                  pl.BlockSpec((FFN_TILE, D_MODEL), lambda r, j: (j, 0)),
                  pl.BlockSpec((1, D_MODEL), const)],
        out_specs=pl.BlockSpec((ROW_GROUP, D_MODEL), lambda r, j: (r, 0)),
        out_shape=jax.ShapeDtypeStruct((t, D_MODEL), F32),
        scratch_shapes=[pltpu.VMEM((ROW_GROUP, D_MODEL), BF16),
                        pltpu.VMEM((D_MODEL, 2 * FFN_TILE), BF16),
                        pltpu.VMEM((FFN_TILE, D_MODEL), BF16),
                        pltpu.VMEM((ROW_GROUP + 2 * CONV_PAD, 2 * FFN_TILE), F32)],
        compiler_params=_params("arbitrary", "arbitrary"),
        name="conv_ffn",
    )(x2d, mod48, mod48, mod48, norm_g, w_up, w_up, conv_w, conv_w, conv_b, conv_b, w_down, final_g)


def _rope_tables(seq_len):
    t = np.arange(seq_len)
    row = (t // GRID_W).astype(np.float32)
    col = (t % GRID_W).astype(np.float32)
    n = QK_ROPE // 4
    inv = (np.float32(ROPE_BASE) ** (-np.arange(n, dtype=np.float32) / np.float32(n))).astype(np.float32)
    ar, ac = row[:, None] * inv, col[:, None] * inv
    cos64 = np.concatenate([np.cos(ar), np.cos(ar), np.cos(ac), np.cos(ac)], axis=1)
    sin64 = np.concatenate([-np.sin(ar), np.sin(ar), -np.sin(ac), np.sin(ac)], axis=1)
    zeros = np.zeros_like(cos64)
    return (jnp.asarray(np.concatenate([cos64, zeros], axis=1), F32),
            jnp.asarray(np.concatenate([sin64, zeros], axis=1), F32))


def _swap_rope_halves(w):
    lead = w.shape[:-1]
    return w.reshape(lead + (2, 2, QK_ROPE // 4))[..., ::-1, :].reshape(lead + (QK_ROPE,))


def _trunk_pass(x, mod48, mod_row, wts, ctx, latent):
    b, l, _ = x.shape
    x2d = x.reshape(b * l, D_MODEL)
    proj_lo, proj_hi = _in_proj(x2d, mod48, mod_row, wts["norm_attn_g"], wts["w_in_r"], wts["ssd_conv_w"],
                                wts["ssd_conv_b"], l)
    rope = _rope_tables(l) if latent else None
    w_uq_r = wts["w_uq_lat"] if latent else wts["w_uq_ctx"]
    shape3 = lambda a, n: a.reshape(b, n, a.shape[-1])
    h0 = None
    mla_ctx = None
    if ctx is not None:
        cache_ckv, cache_krope, h0 = ctx
        mla_ctx = (cache_ckv, cache_krope)
    attn, ckv_n, kr3 = _mla(shape3(proj_hi, l), wts["q_norm_g"], wts["kv_norm_g"], w_uq_r, wts["w_uk_t"], wts["w_uv"],
                            rope, mla_ctx)
    yz, h_t = _ssd(shape3(proj_lo, l), shape3(proj_hi, l), h0, wts["dt_bias128"], wts["a128"], wts["d_exp"])
    x1 = _merge(attn.reshape(b * l, -1), yz.reshape(b * l, -1), proj_lo, x2d, mod48, mod_row, wts["ssd_norm_g"],
                wts["w_o_mla"], wts["w_o_ssd"], wts["w_out"])
    y = _ffn(x1, mod48, mod_row, wts["norm_ffn_g"], wts["w_up"], wts["ffn_conv_w"], wts["ffn_conv_b"], wts["w_down"],
             wts["final_norm_g"], l)
    return y.reshape(b, l, D_MODEL), ckv_n, kr3, h_t


def kernel(x_prompt, x_sample, c, cache_ckv, cache_krope, state_ssd, c_ctx, w_ada, b_ada, norm_attn_g, w_in, q_norm_g,
           kv_norm_g, w_uq, w_ukv, w_o_mla, ssd_conv_w, ssd_conv_b, ssd_dt_bias, ssd_A_log, ssd_D, ssd_norm_g, w_o_ssd,
           w_out, norm_ffn_g, w_up, ffn_conv_w, ffn_conv_b, w_down, final_norm_g):
    depth = w_in.shape[0]
    assert depth == 1, "single trunk layer"
    dec_b = x_sample.shape[0]
    assert x_sample.shape[1] == ROW_GROUP and ROW_GROUP % x_prompt.shape[1] == 0
    lyr = 0

    cvec = jnp.zeros((8, D_MODEL), F32).at[0].set(c_ctx).at[1:1 + dec_b].set(c)
    mod48 = _ada(cvec, w_ada[lyr], b_ada[lyr]).reshape(8 * 6, 1, D_MODEL)

    w_in_r = _regroup_w_in(w_in[lyr].T)
    wq = w_uq[lyr].reshape(Q_LORA, N_HEADS, QK_NOPE + QK_ROPE)
    wq_nope = wq[:, :, :QK_NOPE].reshape(Q_LORA, -1)
    wq_rope = wq[:, :, QK_NOPE:]
    pad_rope = lambda w: jnp.pad(w, ((0, 0), (0, 0), (0, 128 - QK_ROPE))).reshape(Q_LORA, -1)
    w_uq_ctx = jnp.concatenate([wq_nope, pad_rope(wq_rope)], axis=1).astype(BF16)
    w_uq_lat = jnp.concatenate([wq_nope, pad_rope(wq_rope), pad_rope(_swap_rope_halves(wq_rope))], axis=1).astype(BF16)
    wkv = w_ukv[lyr].reshape(KV_LORA, N_HEADS, QK_NOPE + V_HEAD)
    w_uk_t = wkv[:, :, :QK_NOPE].reshape(KV_LORA, -1).T.astype(BF16)
    w_uv = wkv[:, :, QK_NOPE:].reshape(KV_LORA, -1).astype(BF16)
    pad128 = lambda a: jnp.pad(a.reshape(1, -1), ((0, 0), (0, 128 - a.size)))
    wts = {
        "norm_attn_g": norm_attn_g[lyr].reshape(1, -1), "w_in_r": w_in_r,
        "ssd_conv_w": ssd_conv_w[lyr], "ssd_conv_b": ssd_conv_b[lyr].reshape(1, -1),
        "q_norm_g": q_norm_g[lyr].reshape(1, -1), "kv_norm_g": kv_norm_g[lyr].reshape(1, -1),
        "w_uq_ctx": w_uq_ctx, "w_uq_lat": w_uq_lat, "w_uk_t": w_uk_t, "w_uv": w_uv,
        "dt_bias128": pad128(ssd_dt_bias[lyr]), "a128": pad128(-jnp.exp(ssd_A_log[lyr])),
        "d_exp": jnp.repeat(ssd_D[lyr], SSD_HEADDIM).reshape(1, -1),
        "ssd_norm_g": ssd_norm_g[lyr].reshape(1, -1),
        "w_o_mla": w_o_mla[lyr], "w_o_ssd": w_o_ssd[lyr], "w_out": w_out[lyr],
        "norm_ffn_g": norm_ffn_g[lyr].reshape(1, -1), "w_up": w_up[lyr],
        "ffn_conv_w": ffn_conv_w[lyr], "ffn_conv_b": ffn_conv_b[lyr].reshape(1, -1),
        "w_down": w_down[lyr], "final_norm_g": final_norm_g.reshape(1, -1),
    }

    y_p, ckv_p, kr_p, st_p = _trunk_pass(x_prompt, mod48, lambda r: 0, wts, None, False)
    ctx = (cache_ckv[:, lyr], cache_krope[:, lyr], state_ssd[:, lyr])
    y_s, _, _, _ = _trunk_pass(x_sample, mod48, lambda r: 1 + r, wts, ctx, True)
    return y_p, y_s, ckv_p[:, None], kr_p[:, None], st_p[:, None]
```

```python
import functools
import math

import jax
import jax.numpy as jnp
import numpy as np
from jax import lax
from jax.experimental import pallas as pl
from jax.experimental.pallas import tpu as pltpu

F32 = jnp.float32
BF16 = jnp.bfloat16

D_MODEL = 1024
GRID_W = 64
N_HEADS = 8
QK_NOPE = 128
QK_ROPE = 64
V_HEAD = 128
Q_LORA = 256
KV_LORA = 256
ROPE_BASE = 10000.0
SSD_HEADS = 16
SSD_HEADDIM = 64
SSD_INNER = SSD_HEADS * SSD_HEADDIM
SSD_GROUPS = 4
SSD_STATE = 64
SSD_CHUNK = 128
D_FF = 2816
EPS = 1e-6

ROW_GROUP = 2048
IN_TILE = 512
SSD_CHUNKS_PER_STEP = 4
FFN_TILE = 256
TOKEN_TILE = 512
ATTN_Q_TILE = 512
CONV_CHUNK = 256
FFN_DOWN_LAG = 2
CONV_PAD = 8
VMEM_LIMIT = 56 * 1024 * 1024
MLA_VMEM_LIMIT = 62 * 1024 * 1024
NEG_BIG = -1e30
LOG2_E = 1.4426950408889634


def _sigmoid(x):
    return 1.0 / (1.0 + jnp.exp(-x))


def _silu(x):
    return x * _sigmoid(x)


def _softplus(x):
    e = jnp.exp(-jnp.abs(x))
    u = 1.0 + e
    log1p_e = jnp.where(u == 1.0, e, e * jnp.log(u) / jnp.where(u == 1.0, 1.0, u - 1.0))
    return jnp.maximum(x, 0.0) + log1p_e


def _rmsnorm(x, g):
    return x * lax.rsqrt(jnp.mean(x * x, axis=-1, keepdims=True) + EPS) * g


def _dot(a, b):
    return jnp.dot(a, b, preferred_element_type=F32)


def _dot_nt(a, b):
    return lax.dot_general(a, b, (((1,), (1,)), ((), ())), preferred_element_type=F32)


def _params(*sem, vmem_limit=VMEM_LIMIT):
    return pltpu.CompilerParams(dimension_semantics=sem, vmem_limit_bytes=vmem_limit)


def _norm_mod(x, g_ref, sc_ref, sh_ref):
    return (_rmsnorm(x, g_ref[...]) * (1.0 + sc_ref[...]) + sh_ref[...]).astype(BF16)


def _zero_conv_pads(u_scr):
    zeros = jnp.zeros((CONV_PAD, u_scr.shape[1]), F32)
    u_scr[0:CONV_PAD, :] = zeros
    u_scr[CONV_PAD + ROW_GROUP:2 * CONV_PAD + ROW_GROUP, :] = zeros


def _stage_rows(c):
    return slice(CONV_PAD + c * CONV_CHUNK, CONV_PAD + (c + 1) * CONV_CHUNK)


def _dwconv3_rows(u_scr, c, cols, w_ref, b_ref, seq_len):
    r0 = c * CONV_CHUNK
    base = CONV_PAD + r0
    width = cols.stop - cols.start
    prev = u_scr[base - 1:base - 1 + CONV_CHUNK, cols]
    cur = u_scr[base:base + CONV_CHUNK, cols]
    nxt = u_scr[base + 1:base + 1 + CONV_CHUNK, cols]
    pos = (lax.broadcasted_iota(jnp.int32, (CONV_CHUNK, width), 0) + r0) & (seq_len - 1)
    if r0 % seq_len == 0 or CONV_CHUNK > seq_len:
        prev = jnp.where(pos == 0, 0.0, prev)
    if (r0 + CONV_CHUNK) % seq_len == 0 or CONV_CHUNK > seq_len:
        nxt = jnp.where(pos == seq_len - 1, 0.0, nxt)
    return prev * w_ref[0:1, :] + cur * w_ref[1:2, :] + nxt * w_ref[2:3, :] + b_ref[...]


def _ada_kernel(c_ref, w_ref, b_ref, o_ref):
    a = _silu(c_ref[...]).astype(BF16)
    o_ref[...] = _dot(a, w_ref[...].astype(BF16)) + b_ref[...]


def _ada(cvec, w_ada, b_ada):
    tn = 1536
    return pl.pallas_call(
        _ada_kernel,
        grid=(6 * D_MODEL // tn,),
        in_specs=[pl.BlockSpec((8, D_MODEL), lambda j: (0, 0)),
                  pl.BlockSpec((D_MODEL, tn), lambda j: (0, j)),
                  pl.BlockSpec((1, tn), lambda j: (0, j))],
        out_specs=pl.BlockSpec((8, tn), lambda j: (0, j)),
        out_shape=jax.ShapeDtypeStruct((8, 6 * D_MODEL), F32),
        compiler_params=_params("arbitrary"),
        name="ada_mod",
    )(cvec, w_ada, b_ada.reshape(1, -1))


IN_SPLITS = (Q_LORA, KV_LORA, QK_ROPE, SSD_INNER, SSD_INNER, SSD_GROUPS * SSD_STATE, SSD_GROUPS * SSD_STATE,
             2 * SSD_HEADS, D_MODEL, D_MODEL)
IN_OFFSETS = tuple(int(v) for v in np.cumsum((0,) + IN_SPLITS))


def _regroup_kernel(w_ref, o_ref):
    dst = 0

    def put_block(block):
        nonlocal dst
        o_ref[:, dst:dst + 128] = block.T.astype(BF16)
        dst += 128

    def piece(i):
        for off in range(0, IN_SPLITS[i], 128):
            put_block(w_ref[IN_OFFSETS[i] + off:IN_OFFSETS[i] + off + 128, :])

    def padded(parts):
        n = sum(p.shape[0] for p in parts)
        put_block(jnp.concatenate(parts + [jnp.zeros((128 - n, w_ref.shape[1]), F32)], axis=0))

    for i in (3, 8, 9, 5, 6, 4, 0, 1):
        piece(i)
    kr0, n = IN_OFFSETS[2], QK_ROPE // 4
    padded([w_ref[kr0:kr0 + QK_ROPE, :]])
    padded([w_ref[kr0 + blk * n:kr0 + (blk + 1) * n, :] for blk in (1, 0, 3, 2)])
    padded([w_ref[IN_OFFSETS[7]:IN_OFFSETS[7] + IN_SPLITS[7], :]])
    o_ref[:, dst:] = jnp.zeros((o_ref.shape[0], o_ref.shape[1] - dst), BF16)


def _regroup_w_in(w_in_t):
    cols = 256
    n_out = (N_LOW_TILES + N_F32_TILES) * IN_TILE
    return pl.pallas_call(
        _regroup_kernel,
        grid=(D_MODEL // cols,),
        in_specs=[pl.BlockSpec((w_in_t.shape[0], cols), lambda i: (0, i))],
        out_specs=pl.BlockSpec((cols, n_out), lambda i: (i, 0)),
        out_shape=jax.ShapeDtypeStruct((D_MODEL, n_out), BF16),
        compiler_params=_params("arbitrary"),
        name="w_in_regroup",
    )(w_in_t)


N_LOW_TILES, N_F32_TILES = 7, 4
Z_BLK, GM_BLK, GS_BLK = 0, 1, 2
BC_TILE = 6
XS_BLK = 0
MLA_TILE, MISC_TILE = 2, 3
DT_BLK = (MISC_TILE * IN_TILE + 256) // 128
ATTN_HEAD_COLS = 256


def _in_kernel(x_ref, sh_ref, sc_ref, g_ref, w_ref, cw_ref, cb_ref, lo_ref, hi_ref, h_scr, u_scr, *, seq_len):
    j = pl.program_id(1)
    n_chunks = ROW_GROUP // CONV_CHUNK
    rows = lambda c: slice(c * CONV_CHUNK, (c + 1) * CONV_CHUNK)

    def pointwise(fn, o_ref, stage_h=False):
        for c in range(n_chunks):
            if stage_h:
                h_scr[rows(c), :] = _norm_mod(x_ref[rows(c), :], g_ref, sc_ref, sh_ref)
            o_ref[rows(c), :] = fn(_dot(h_scr[rows(c), :], w_ref[...])).astype(o_ref.dtype)

    def conv(o_ref):
        def conv_out(c):
            v = _silu(_dwconv3_rows(u_scr, c, slice(0, IN_TILE), cw_ref, cb_ref, seq_len))
            o_ref[rows(c), :] = v.astype(o_ref.dtype)
        for c in range(n_chunks):
            u_scr[_stage_rows(c), :] = _dot(h_scr[rows(c), :], w_ref[...])
            if c >= 1:
                conv_out(c - 1)
        conv_out(n_chunks - 1)

    @pl.when(j == 0)
    def _():
        _zero_conv_pads(u_scr)
        pointwise(_silu, lo_ref, stage_h=True)

    @pl.when(j == 1)
    def _():
        pointwise(_silu, lo_ref)

    @pl.when((j >= 2) & (j <= 5))
    def _():
        pointwise(_sigmoid, lo_ref)

    @pl.when(j == BC_TILE)
    def _():
        conv(lo_ref)

    @pl.when((j == N_LOW_TILES) | (j == N_LOW_TILES + 1))
    def _():
        conv(hi_ref)

    @pl.when(j >= N_LOW_TILES + MLA_TILE)
    def _():
        pointwise(lambda u: u, hi_ref)


def _in_proj(x2d, mod48, mod_row, norm_g, w_in_r, conv_w, conv_b, seq_len):
    t = x2d.shape[0]
    n_tiles = N_LOW_TILES + N_F32_TILES
    conv_idx = lambda r, j: (0, jnp.where(j == BC_TILE, 2, jnp.clip(j - N_LOW_TILES, 0, 1)))
    return pl.pallas_call(
        functools.partial(_in_kernel, seq_len=seq_len),
        grid=(t // ROW_GROUP, n_tiles),
        in_specs=[pl.BlockSpec((ROW_GROUP, D_MODEL), lambda r, j: (r, 0)),
                  pl.BlockSpec((None, 1, D_MODEL), lambda r, j: (mod_row(r) * 6 + 0, 0, 0)),
                  pl.BlockSpec((None, 1, D_MODEL), lambda r, j: (mod_row(r) * 6 + 1, 0, 0)),
                  pl.BlockSpec((1, D_MODEL), lambda r, j: (0, 0)),
                  pl.BlockSpec((D_MODEL, IN_TILE), lambda r, j: (0, j)),
                  pl.BlockSpec((3, IN_TILE), conv_idx),
                  pl.BlockSpec((1, IN_TILE), conv_idx)],
        out_specs=[pl.BlockSpec((ROW_GROUP, IN_TILE), lambda r, j: (r, jnp.minimum(j, N_LOW_TILES - 1))),
                   pl.BlockSpec((ROW_GROUP, IN_TILE), lambda r, j: (r, jnp.maximum(j - N_LOW_TILES, 0)))],
        out_shape=[jax.ShapeDtypeStruct((t, N_LOW_TILES * IN_TILE), BF16),
                   jax.ShapeDtypeStruct((t, N_F32_TILES * IN_TILE), F32)],
        scratch_shapes=[pltpu.VMEM((ROW_GROUP, D_MODEL), BF16),
                        pltpu.VMEM((ROW_GROUP + 2 * CONV_PAD, IN_TILE), F32)],
        compiler_params=_params("arbitrary", "arbitrary"),
        name="in_proj",
    )(x2d, mod48, mod48, norm_g, w_in_r, conv_w, conv_b)


def _mla_kernel(*refs, latent, has_ctx, emit_cache, seq_len, tq):
    refs = list(refs)
    pm_ref, px_ref, qg_ref, kvg_ref, wuq_ref, wkt_ref, wv_ref = refs[:7]
    del refs[:7]
    if latent:
        cos_ref, sin_ref = refs[:2]
        del refs[:2]
    if has_ctx:
        cckv_ref, ckr_ref = refs[:2]
        del refs[:2]
    o_ref = refs.pop(0)
    if emit_cache:
        ckv_ref, kr_ref = refs[:2]
        del refs[:2]
    k_scr, v_scr = refs[:2]
    del refs[:2]
    if has_ctx:
        kc_scr, vc_scr = refs
    t = pl.program_id(1)
    n_nope = N_HEADS * QK_NOPE

    def put_keys(k_dst, v_dst, rows, ckv_n, kr_bf):
        ckv_bf = ckv_n.astype(BF16)
        v_dst[rows, :] = _dot(ckv_bf, wv_ref[...]).astype(BF16)
        kn_t = _dot_nt(wkt_ref[...], ckv_bf).astype(BF16)
        width = kr_bf.shape[1]
        eye = jnp.where(lax.broadcasted_iota(jnp.int32, (128, width), 0)
                        == lax.broadcasted_iota(jnp.int32, (128, width), 1), 1.0, 0.0).astype(BF16)
        kr_t = _dot_nt(eye, kr_bf).astype(BF16)
        for h in range(N_HEADS):
            base = h * ATTN_HEAD_COLS
            k_dst[base:base + QK_NOPE, rows] = kn_t[h * QK_NOPE:(h + 1) * QK_NOPE, :]
            k_dst[base + QK_NOPE:base + ATTN_HEAD_COLS, rows] = kr_t

    @pl.when(t == 0)
    def _():
        step = min(seq_len, TOKEN_TILE)
        for r0 in range(0, seq_len, step):
            rows = slice(r0, r0 + step)
            ckv_n = _rmsnorm(pm_ref[0, rows, Q_LORA:], kvg_ref[...])
            kr = px_ref[0, rows, 0:128]
            if latent:
                kr = kr * cos_ref[rows, :] + px_ref[0, rows, 128:256] * sin_ref[rows, :]
            if emit_cache:
                ckv_ref[0, rows, :] = ckv_n
                kr_ref[0, rows, :] = kr[:, :QK_ROPE]
            put_keys(k_scr, v_scr, rows, ckv_n, kr.astype(BF16))
        if has_ctx:
            past = cckv_ref.shape[1]
            put_keys(kc_scr, vc_scr, slice(0, past), cckv_ref[0], ckr_ref[0].astype(BF16))

    qrows = pl.ds(pl.multiple_of(t * tq, tq), tq)
    scale = LOG2_E / math.sqrt(QK_NOPE + QK_ROPE)
    cqn = _rmsnorm(pm_ref[0, qrows, 0:Q_LORA], qg_ref[...]).astype(BF16)
    q = _dot(cqn, wuq_ref[...])
    q_rope = q[:, n_nope:2 * n_nope]
    if latent:
        q_rope = (q_rope * jnp.concatenate([cos_ref[qrows, :]] * N_HEADS, axis=1)
                  + q[:, 2 * n_nope:3 * n_nope] * jnp.concatenate([sin_ref[qrows, :]] * N_HEADS, axis=1))
    segs = ([(kc_scr, vc_scr)] if has_ctx else []) + [(k_scr, v_scr)]
    for h in range(N_HEADS):
        head = slice(h * QK_NOPE, (h + 1) * QK_NOPE)
        qk = slice(h * ATTN_HEAD_COLS, (h + 1) * ATTN_HEAD_COLS)
        qh = (jnp.concatenate([q[:, head], q_rope[:, head]], axis=1) * scale).astype(BF16)
        s = [_dot(qh, k[qk, :]) for k, _ in segs]
        m = functools.reduce(jnp.maximum, [jnp.max(si, axis=-1, keepdims=True) for si in s])
        p = [jnp.exp2(si - m) for si in s]
        l = functools.reduce(jnp.add, [jnp.sum(pi, axis=-1, keepdims=True) for pi in p])
        o = functools.reduce(jnp.add, [_dot(pi.astype(BF16), v[:, head]) for pi, (_, v) in zip(p, segs)])
        o_ref[0, :, head] = (o / l).astype(BF16)


def _mla(proj_hi, q_norm_g, kv_norm_g, w_uq_r, w_uk_t, w_uv, rope_tables, ctx, emit_cache):
    b, l, _ = proj_hi.shape
    tq = min(l, ATTN_Q_TILE)
    latent = rope_tables is not None
    has_ctx = ctx is not None
    once = dict(pipeline_mode=pl.Buffered(1))
    per_batch = once if l // tq > 1 else {}
    const2 = lambda i, t: (0, 0)
    in_specs = [pl.BlockSpec((1, l, IN_TILE), lambda i, t: (i, 0, MLA_TILE), **per_batch),
                pl.BlockSpec((1, l, 256), lambda i, t: (i, 0, MISC_TILE * IN_TILE // 256), **per_batch),
                pl.BlockSpec((1, Q_LORA), const2),
                pl.BlockSpec((1, KV_LORA), const2),
                pl.BlockSpec(w_uq_r.shape, const2, **once),
                pl.BlockSpec(w_uk_t.shape, const2, **once),
                pl.BlockSpec(w_uv.shape, const2, **once)]
    args = [proj_hi, proj_hi, q_norm_g, kv_norm_g, w_uq_r, w_uk_t, w_uv]
    scratch = [pltpu.VMEM((N_HEADS * ATTN_HEAD_COLS, l), BF16), pltpu.VMEM((l, N_HEADS * V_HEAD), BF16)]
    if latent:
        in_specs += [pl.BlockSpec((l, 128), const2, **once)] * 2
        args += list(rope_tables)
    if has_ctx:
        past = ctx[0].shape[1]
        in_specs += [pl.BlockSpec((1, past, KV_LORA), lambda i, t: (i, 0, 0), **per_batch),
                     pl.BlockSpec((1, past, QK_ROPE), lambda i, t: (i, 0, 0), **per_batch)]
        args += list(ctx)
        scratch += [pltpu.VMEM((N_HEADS * ATTN_HEAD_COLS, past), BF16), pltpu.VMEM((past, N_HEADS * V_HEAD), BF16)]
    out_specs = [pl.BlockSpec((1, tq, N_HEADS * V_HEAD), lambda i, t: (i, t, 0))]
    out_shape = [jax.ShapeDtypeStruct((b, l, N_HEADS * V_HEAD), BF16)]
    if emit_cache:
        out_specs += [pl.BlockSpec((1, l, KV_LORA), lambda i, t: (i, 0, 0)),
                      pl.BlockSpec((1, l, QK_ROPE), lambda i, t: (i, 0, 0))]
        out_shape += [jax.ShapeDtypeStruct((b, l, KV_LORA), F32), jax.ShapeDtypeStruct((b, l, QK_ROPE), F32)]
    return pl.pallas_call(
        functools.partial(_mla_kernel, latent=latent, has_ctx=has_ctx, emit_cache=emit_cache, seq_len=l, tq=tq),
        grid=(b, l // tq),
        in_specs=in_specs,
        out_specs=out_specs,
        out_shape=out_shape,
        scratch_shapes=scratch,
        compiler_params=_params("arbitrary", "arbitrary", vmem_limit=MLA_VMEM_LIMIT),
        name="mla_attention",
    )(*args)


def _split3(x):
    hi = x.astype(BF16)
    r = x - hi.astype(F32)
    mid = r.astype(BF16)
    lo = (r - mid.astype(F32)).astype(BF16)
    return hi, mid, lo


def _exact_dot(parts, sel):
    return functools.reduce(jnp.add, [_dot(p, sel) for p in parts])


def _exact_dot_rows(sel, parts):
    return functools.reduce(jnp.add, [_dot(sel, p) for p in parts])


HEADS_PER_GROUP = SSD_HEADS // SSD_GROUPS
GROUP_COLS = HEADS_PER_GROUP * SSD_HEADDIM


def _ssd_kernel(*refs, nc, cps, has_h0):
    if has_h0:
        (xs_ref, zs_ref, bc_ref, dt_ref, dtb_ref, a_ref, dx_ref, h0_ref, y_ref, ht_ref,
         yl_scr, acum_scr, src_scr, tot_scr, sb_scr, h_scr, esel_scr) = refs
    else:
        (xs_ref, zs_ref, bc_ref, dt_ref, dtb_ref, a_ref, dx_ref, y_ref, ht_ref,
         yl_scr, acum_scr, src_scr, tot_scr, sb_scr, h_scr, esel_scr) = refs
    s = pl.program_id(1)
    q = SSD_CHUNK
    n_bc = SSD_GROUPS * SSD_STATE
    lane = lax.broadcasted_iota(jnp.int32, (q, 128), 1)
    low_half = lane < SSD_HEADDIM
    ii = lax.broadcasted_iota(jnp.int32, (q, q), 0)
    jj = lax.broadcasted_iota(jnp.int32, (q, q), 1)
    lower, upper = ii >= jj, ii <= jj

    def lane_bcast(parts, d):
        return _exact_dot(parts, esel_scr[d])

    def stacked_states(d):
        return [jnp.concatenate([h_scr[d, 2 * i], h_scr[d, 2 * i + 1]], axis=0).astype(BF16)
                for i in range(SSD_GROUPS // 2)]

    def group_c(bc, g):
        i, r = divmod(g, 2)
        cpair = bc[:, n_bc + i * 128:n_bc + (i + 1) * 128]
        return jnp.where(low_half if r == 0 else ~low_half, cpair, 0.0).astype(BF16)

    @pl.when(s == 0)
    def _():
        k = lax.broadcasted_iota(jnp.int32, (128, SSD_INNER), 0)
        head = lax.broadcasted_iota(jnp.int32, (128, SSD_INNER), 1) // SSD_HEADDIM
        for d in range(2):
            esel_scr[d] = jnp.where(k == d * SSD_HEADS + head, 1.0, 0.0).astype(BF16)
        if has_h0:
            for d in range(2):
                for g in range(SSD_GROUPS):
                    hpn = h0_ref[0, d, g * HEADS_PER_GROUP:(g + 1) * HEADS_PER_GROUP].reshape(GROUP_COLS, SSD_STATE)
                    h_scr[d, g] = hpn.T
        else:
            h_scr[...] = jnp.zeros(h_scr.shape, F32)
        tri_f = jnp.where(lower, 1.0, 0.0).astype(BF16)
        tri_b = jnp.where(upper, 1.0, 0.0).astype(BF16)
        fwd_col = lane < SSD_HEADS
        for c in range(nc):
            crow = slice(c * q, (c + 1) * q)
            dt = _softplus(dt_ref[0, crow, :] + dtb_ref[...])
            parts = _split3(dt * a_ref[...])
            acum = jnp.where(fwd_col, _exact_dot_rows(tri_f, parts),
                             _exact_dot_rows(tri_b, parts))
            acum = acum * LOG2_E
            acum_scr[crow, :] = acum
            tot = jnp.where(fwd_col[0:1], acum[q - 1:q, :], acum[0:1, :])
            tot_scr[c] = jnp.broadcast_to(tot, (8, 128))
            src_scr[c] = (acum - jnp.log2(dt)).T

    def first_sweep(c, blk):
        rows = pl.ds(pl.multiple_of(c * q, q), q)
        x = xs_ref[0, blk, :]
        bc = bc_ref[0, blk, :].astype(F32)
        acum = acum_scr[rows, :]
        src_t = src_scr[c]
        tot8 = tot_scr[c]
        e = jnp.exp2(acum)
        e_hi = e.astype(BF16)
        eb_f = lane_bcast([e_hi, (e - e_hi.astype(F32)).astype(BF16)], 0)
        cd_f = jnp.exp2(lane_bcast(_split3(tot8), 0))[0:1]
        b_t = [bc[:, i * 128:(i + 1) * 128].T for i in range(SSD_GROUPS // 2)]
        h_in = stacked_states(0)
        for g in range(SSD_GROUPS):
            i, r = divmod(g, 2)
            cm = group_c(bc, g)
            scores = _dot_nt(cm, bc[:, i * 128:(i + 1) * 128].astype(BF16))
            bg_t = b_t[i][r * SSD_STATE:(r + 1) * SSD_STATE, :]
            gcols = slice(g * GROUP_COLS, (g + 1) * GROUP_COLS)
            y_off = _dot(cm, h_in[i]) * eb_f[:, gcols]
            for t in range(HEADS_PER_GROUP // 2):
                pc = slice((2 * g + t) * 128, (2 * g + t + 1) * 128)
                xp = x[:, pc]
                x2 = jnp.concatenate([jnp.where(low_half, xp, 0.0), jnp.where(low_half, 0.0, xp)], axis=0).astype(BF16)
                m, sf, sb = [], [], []
                for u in range(2):
                    kf = g * HEADS_PER_GROUP + 2 * t + u
                    kb = SSD_HEADS + kf
                    af_col = jnp.broadcast_to(acum[:, kf:kf + 1], (q, q))
                    ab_col = jnp.broadcast_to(acum[:, kb:kb + 1], (q, q))
                    sf_row, sb_row = src_t[kf:kf + 1, :], src_t[kb:kb + 1, :]
                    decay = (jnp.exp2(jnp.where(lower, af_col - sf_row, NEG_BIG))
                             + jnp.exp2(jnp.where(upper, ab_col - sb_row, NEG_BIG)))
                    m.append((scores * decay).astype(BF16))
                    wf = jnp.exp2(tot8[0:1, kf:kf + 1] - sf_row)
                    wb = jnp.exp2(tot8[0:1, kb:kb + 1] - sb_row)
                    sf.append((bg_t * wf).astype(BF16))
                    sb.append((bg_t * wb).astype(BF16))
                y_pair = _dot(jnp.concatenate(m, axis=1), x2)
                tc = slice(t * 128, (t + 1) * 128)
                yl_scr[rows, pc] = y_pair + y_off[:, tc] + dx_ref[:, pc] * xp
                h_scr[0, g, :, tc] = h_scr[0, g, :, tc] * cd_f[:, pc] + _dot(jnp.concatenate(sf, axis=1), x2)
                sb_scr[c, g, :, tc] = _dot(jnp.concatenate(sb, axis=1), x2)

    def second_sweep(c, blk):
        rows = pl.ds(pl.multiple_of(c * q, q), q)
        bc = bc_ref[0, blk, :].astype(F32)
        e = jnp.exp2(acum_scr[rows, :])
        e_hi = e.astype(BF16)
        eb_b = lane_bcast([e_hi, (e - e_hi.astype(F32)).astype(BF16)], 1)
        cd_b = jnp.exp2(lane_bcast(_split3(tot_scr[c]), 1))[0:1]
        h_in = stacked_states(1)
        for g in range(SSD_GROUPS):
            gcols = slice(g * GROUP_COLS, (g + 1) * GROUP_COLS)
            y_off = _dot(group_c(bc, g), h_in[g // 2]) * eb_b[:, gcols]
            y_ref[0, blk, gcols] = ((yl_scr[rows, gcols] + y_off) * zs_ref[0, blk, gcols].astype(F32)).astype(y_ref.dtype)
            h_scr[1, g] = h_scr[1, g] * cd_b[:, gcols] + sb_scr[c, g]

    n_steps = nc // cps
    one_step = n_steps == 1

    @pl.when(s < n_steps)
    def _():
        for ci in range(cps):
            first_sweep(s * cps + ci, slice(ci * q, (ci + 1) * q))

    @pl.when(jnp.logical_or(one_step, s >= n_steps))
    def _():
        blk_id = s * 0 if one_step else 2 * n_steps - 1 - s
        for ci in reversed(range(cps)):
            second_sweep(blk_id * cps + ci, slice(ci * q, (ci + 1) * q))

    @pl.when(jnp.logical_or(one_step, s == 2 * n_steps - 1))
    def _():
        for d in range(2):
            for g in range(SSD_GROUPS):
                ht_ref[0, d, g * HEADS_PER_GROUP:(g + 1) * HEADS_PER_GROUP] = h_scr[d, g].T.reshape(
                    HEADS_PER_GROUP, SSD_HEADDIM, SSD_STATE)


def _ssd(proj_lo, proj_hi, h0, dt_bias128, a128, d_exp):
    b, l, _ = proj_hi.shape
    q = SSD_CHUNK
    nc = l // q
    cps = min(nc, SSD_CHUNKS_PER_STEP)
    n_steps = nc // cps
    n_grid = 1 if n_steps == 1 else 2 * n_steps
    rows = cps * q
    early = lambda s: jnp.minimum(s, n_steps - 1)
    both = lambda s: jnp.where(s < n_steps, s, 2 * n_steps - 1 - s)
    late = lambda s: jnp.where(s < n_steps, n_steps - 1, 2 * n_steps - 1 - s)
    st_shape = (1, 2, SSD_HEADS, SSD_HEADDIM, SSD_STATE)
    st_spec = pl.BlockSpec(st_shape, lambda i, s: (i, 0, 0, 0, 0))
    has_h0 = h0 is not None
    return pl.pallas_call(
        functools.partial(_ssd_kernel, nc=nc, cps=cps, has_h0=has_h0),
        grid=(b, n_grid),
        in_specs=[pl.BlockSpec((1, rows, SSD_INNER), lambda i, s: (i, early(s), XS_BLK)),
                  pl.BlockSpec((1, rows, SSD_INNER), lambda i, s: (i, late(s), Z_BLK)),
                  pl.BlockSpec((1, rows, IN_TILE), lambda i, s: (i, both(s), BC_TILE)),
                  pl.BlockSpec((1, l, 128), lambda i, s: (i, 0, DT_BLK)),
                  pl.BlockSpec((1, 128), lambda i, s: (0, 0)),
                  pl.BlockSpec((1, 128), lambda i, s: (0, 0)),
                  pl.BlockSpec((1, SSD_INNER), lambda i, s: (0, 0))] + ([st_spec] if has_h0 else []),
        out_specs=[pl.BlockSpec((1, rows, SSD_INNER), lambda i, s: (i, late(s), 0)),
                   pl.BlockSpec(st_shape, lambda i, s: (i, 0, 0, 0, 0))],
        out_shape=[jax.ShapeDtypeStruct((b, l, SSD_INNER), BF16),
                   jax.ShapeDtypeStruct((b,) + st_shape[1:], F32)],
        scratch_shapes=[pltpu.VMEM((l, SSD_INNER), F32),
                        pltpu.VMEM((l, 128), F32),
                        pltpu.VMEM((nc, 128, q), F32),
                        pltpu.VMEM((nc, 8, 128), F32),
                        pltpu.VMEM((nc, SSD_GROUPS, SSD_STATE, GROUP_COLS), F32),
                        pltpu.VMEM((2, SSD_GROUPS, SSD_STATE, GROUP_COLS), F32),
                        pltpu.VMEM((2, 128, SSD_INNER), BF16)],
        compiler_params=_params("arbitrary", "arbitrary"),
        name="ssd_scan",
    )(proj_hi, proj_lo, proj_lo, proj_hi, dt_bias128, a128, d_exp, *([h0] if has_h0 else []))


def _merge_kernel(attn_ref, yz_ref, gm_ref, gs_ref, x_ref, g1_ref, ng_ref, womla_ref, wossd_ref, wout_ref, o_ref, w_scr):
    @pl.when(pl.program_id(0) == 0)
    def _():
        w_scr[0] = womla_ref[...].astype(BF16)
        w_scr[1] = wossd_ref[...].astype(BF16)
        w_scr[2] = wout_ref[...].astype(BF16)

    o_mla = _dot(attn_ref[...], w_scr[0])
    o_ssd = _dot(_rmsnorm(yz_ref[...].astype(F32), ng_ref[...]).astype(BF16), w_scr[1])
    merged = gm_ref[...].astype(F32) * o_mla + gs_ref[...].astype(F32) * o_ssd
    o_ref[...] = x_ref[...] + g1_ref[...] * _dot(merged.astype(BF16), w_scr[2])


def _merge(attn2d, yz2d, proj, x2d, mod48, mod_row, ssd_norm_g, w_o_mla, w_o_ssd, w_out):
    t = x2d.shape[0]
    tm = TOKEN_TILE
    row = lambda i: (i, 0)
    const = lambda i: (0, 0)
    wspec = pl.BlockSpec((D_MODEL, D_MODEL), const, pipeline_mode=pl.Buffered(1))
    return pl.pallas_call(
        _merge_kernel,
        grid=(t // tm,),
        in_specs=[pl.BlockSpec((tm, D_MODEL), row),
                  pl.BlockSpec((tm, D_MODEL), row),
                  pl.BlockSpec((tm, D_MODEL), lambda i: (i, GM_BLK)),
                  pl.BlockSpec((tm, D_MODEL), lambda i: (i, GS_BLK)),
                  pl.BlockSpec((tm, D_MODEL), row),
                  pl.BlockSpec((None, 1, D_MODEL), lambda i: (mod_row(i * tm // ROW_GROUP) * 6 + 2, 0, 0)),
                  pl.BlockSpec((1, D_MODEL), const),
                  wspec, wspec, wspec],
        out_specs=pl.BlockSpec((tm, D_MODEL), row),
        out_shape=jax.ShapeDtypeStruct((t, D_MODEL), F32),
        scratch_shapes=[pltpu.VMEM((3, D_MODEL, D_MODEL), BF16)],
        compiler_params=_params("arbitrary"),
        name="merge_out",
    )(attn2d, yz2d, proj, proj, x2d, mod48, ssd_norm_g, w_o_mla, w_o_ssd, w_out)


def _ffn_kernel(x_ref, sh_ref, sc_ref, g2_ref, ng_ref, wg_ref, wv_ref, cwg_ref, cwv_ref, cbg_ref, cbv_ref, wd_ref,
                fg_ref, o_ref, h_scr, wup_scr, wd_scr, u_scr, *, seq_len):
    j = pl.program_id(1)
    n_chunks = ROW_GROUP // CONV_CHUNK
    rows = lambda c: slice(c * CONV_CHUNK, (c + 1) * CONV_CHUNK)

    wup_scr[:, 0:FFN_TILE] = wg_ref[...].astype(BF16)
    wup_scr[:, FFN_TILE:2 * FFN_TILE] = wv_ref[...].astype(BF16)
    wd_scr[...] = wd_ref[...].astype(BF16)

    def gated(c):
        ug = _dwconv3_rows(u_scr, c, slice(0, FFN_TILE), cwg_ref, cbg_ref, seq_len)
        uv = _dwconv3_rows(u_scr, c, slice(FFN_TILE, 2 * FFN_TILE), cwv_ref, cbv_ref, seq_len)
        return (_silu(ug) * uv).astype(BF16)

    def pipeline(first, last):
        act = {}
        lag = FFN_DOWN_LAG
        for c in range(n_chunks + lag):
            if c < n_chunks:
                if first:
                    h_scr[rows(c), :] = _norm_mod(x_ref[rows(c), :], ng_ref, sc_ref, sh_ref)
                u_scr[_stage_rows(c), :] = _dot(h_scr[rows(c), :], wup_scr[...])
            if c >= lag:
                r = rows(c - lag)
                acc = _dot(act.pop(c - lag), wd_scr[...])
                if not first:
                    acc = o_ref[r, :] + acc
                if last:
                    acc = _rmsnorm(x_ref[r, :] + g2_ref[...] * acc, fg_ref[...])
                o_ref[r, :] = acc
            if 1 <= c <= n_chunks:
                act[c - 1] = gated(c - 1)

    last_j = pl.num_programs(1) - 1

    @pl.when(j == 0)
    def _():
        _zero_conv_pads(u_scr)
        pipeline(True, False)

    @pl.when((j > 0) & (j < last_j))
    def _():
        pipeline(False, False)

    @pl.when(j == last_j)
    def _():
        pipeline(False, True)


def _ffn(x2d, mod48, mod_row, norm_g, w_up, conv_w, conv_b, w_down, final_g, seq_len):
    t = x2d.shape[0]
    nj = D_FF // FFN_TILE
    gate = lambda r, j: (0, j)
    val = lambda r, j: (0, nj + j)
    const = lambda r, j: (0, 0)
    mod = lambda k: pl.BlockSpec((None, 1, D_MODEL), lambda r, j: (mod_row(r) * 6 + k, 0, 0))
    return pl.pallas_call(
        functools.partial(_ffn_kernel, seq_len=seq_len),
        grid=(t // ROW_GROUP, nj),
        in_specs=[pl.BlockSpec((ROW_GROUP, D_MODEL), lambda r, j: (r, 0)),
                  mod(3), mod(4), mod(5),
                  pl.BlockSpec((1, D_MODEL), const),
                  pl.BlockSpec((D_MODEL, FFN_TILE), gate),
                  pl.BlockSpec((D_MODEL, FFN_TILE), val),
                  pl.BlockSpec((3, FFN_TILE), gate),
                  pl.BlockSpec((3, FFN_TILE), val),
                  pl.BlockSpec((1, FFN_TILE), gate),
                  pl.BlockSpec((1, FFN_TILE), val),
                  pl.BlockSpec((FFN_TILE, D_MODEL), lambda r, j: (j, 0)),
                  pl.BlockSpec((1, D_MODEL), const)],
        out_specs=pl.BlockSpec((ROW_GROUP, D_MODEL), lambda r, j: (r, 0)),
        out_shape=jax.ShapeDtypeStruct((t, D_MODEL), F32),
        scratch_shapes=[pltpu.VMEM((ROW_GROUP, D_MODEL), BF16),
                        pltpu.VMEM((D_MODEL, 2 * FFN_TILE), BF16),
                        pltpu.VMEM((FFN_TILE, D_MODEL), BF16),
                        pltpu.VMEM((ROW_GROUP + 2 * CONV_PAD, 2 * FFN_TILE), F32)],
        compiler_params=_params("arbitrary", "arbitrary"),
        name="conv_ffn",
    )(x2d, mod48, mod48, mod48, norm_g, w_up, w_up, conv_w, conv_w, conv_b, conv_b, w_down, final_g)


def _rope_tables(seq_len):
    t = np.arange(seq_len)
    row = (t // GRID_W).astype(np.float32)
    col = (t % GRID_W).astype(np.float32)
    n = QK_ROPE // 4
    inv = (np.float32(ROPE_BASE) ** (-np.arange(n, dtype=np.float32) / np.float32(n))).astype(np.float32)
    ar, ac = row[:, None] * inv, col[:, None] * inv
    cos64 = np.concatenate([np.cos(ar), np.cos(ar), np.cos(ac), np.cos(ac)], axis=1)
    sin64 = np.concatenate([-np.sin(ar), np.sin(ar), -np.sin(ac), np.sin(ac)], axis=1)
    zeros = np.zeros_like(cos64)
    return (jnp.asarray(np.concatenate([cos64, zeros], axis=1), F32),
            jnp.asarray(np.concatenate([sin64, zeros], axis=1), F32))


def _swap_rope_halves(w):
    lead = w.shape[:-1]
    return w.reshape(lead + (2, 2, QK_ROPE // 4))[..., ::-1, :].reshape(lead + (QK_ROPE,))


def _trunk_pass(x, mod48, mod_row, wts, ctx, latent):
    b, l, _ = x.shape
    x2d = x.reshape(b * l, D_MODEL)
    proj_lo, proj_hi = _in_proj(x2d, mod48, mod_row, wts["norm_attn_g"], wts["w_in_r"], wts["ssd_conv_w"],
                                wts["ssd_conv_b"], l)
    rope = _rope_tables(l) if latent else None
    w_uq_r = wts["w_uq_lat"] if latent else wts["w_uq_ctx"]
    shape3 = lambda a, n: a.reshape(b, n, a.shape[-1])
    h0 = None
    mla_ctx = None
    if ctx is not None:
        cache_ckv, cache_krope, h0 = ctx
        mla_ctx = (cache_ckv, cache_krope)
    emit_cache = ctx is None
    mla_out = _mla(shape3(proj_hi, l), wts["q_norm_g"], wts["kv_norm_g"], w_uq_r, wts["w_uk_t"], wts["w_uv"],
                   rope, mla_ctx, emit_cache)
    attn, ckv_n, kr3 = mla_out if emit_cache else (mla_out[0], None, None)
    yz, h_t = _ssd(shape3(proj_lo, l), shape3(proj_hi, l), h0, wts["dt_bias128"], wts["a128"], wts["d_exp"])
    x1 = _merge(attn.reshape(b * l, -1), yz.reshape(b * l, -1), proj_lo, x2d, mod48, mod_row, wts["ssd_norm_g"],
                wts["w_o_mla"], wts["w_o_ssd"], wts["w_out"])
    y = _ffn(x1, mod48, mod_row, wts["norm_ffn_g"], wts["w_up"], wts["ffn_conv_w"], wts["ffn_conv_b"], wts["w_down"],
             wts["final_norm_g"], l)
    return y.reshape(b, l, D_MODEL), ckv_n, kr3, h_t


def kernel(x_prompt, x_sample, c, cache_ckv, cache_krope, state_ssd, c_ctx, w_ada, b_ada, norm_attn_g, w_in, q_norm_g,
           kv_norm_g, w_uq, w_ukv, w_o_mla, ssd_conv_w, ssd_conv_b, ssd_dt_bias, ssd_A_log, ssd_D, ssd_norm_g, w_o_ssd,
           w_out, norm_ffn_g, w_up, ffn_conv_w, ffn_conv_b, w_down, final_norm_g):
    depth = w_in.shape[0]
    assert depth == 1, "single trunk layer"
    dec_b = x_sample.shape[0]
    assert x_sample.shape[1] == ROW_GROUP and ROW_GROUP % x_prompt.shape[1] == 0
    lyr = 0

    cvec = jnp.zeros((8, D_MODEL), F32).at[0].set(c_ctx).at[1:1 + dec_b].set(c)
    mod48 = _ada(cvec, w_ada[lyr], b_ada[lyr]).reshape(8 * 6, 1, D_MODEL)

    w_in_r = _regroup_w_in(w_in[lyr].T)
    wq = w_uq[lyr].reshape(Q_LORA, N_HEADS, QK_NOPE + QK_ROPE)
    wq_nope = wq[:, :, :QK_NOPE].reshape(Q_LORA, -1)
    wq_rope = wq[:, :, QK_NOPE:]
    pad_rope = lambda w: jnp.pad(w, ((0, 0), (0, 0), (0, 128 - QK_ROPE))).reshape(Q_LORA, -1)
    w_uq_ctx = jnp.concatenate([wq_nope, pad_rope(wq_rope)], axis=1).astype(BF16)
    w_uq_lat = jnp.concatenate([wq_nope, pad_rope(wq_rope), pad_rope(_swap_rope_halves(wq_rope))], axis=1).astype(BF16)
    wkv = w_ukv[lyr].reshape(KV_LORA, N_HEADS, QK_NOPE + V_HEAD)
    w_uk_t = wkv[:, :, :QK_NOPE].reshape(KV_LORA, -1).T.astype(BF16)
    w_uv = wkv[:, :, QK_NOPE:].reshape(KV_LORA, -1).astype(BF16)
    pad128 = lambda a: jnp.pad(a.reshape(1, -1), ((0, 0), (0, 128 - a.size)))
    wts = {
        "norm_attn_g": norm_attn_g[lyr].reshape(1, -1), "w_in_r": w_in_r,
        "ssd_conv_w": ssd_conv_w[lyr], "ssd_conv_b": ssd_conv_b[lyr].reshape(1, -1),
        "q_norm_g": q_norm_g[lyr].reshape(1, -1), "kv_norm_g": kv_norm_g[lyr].reshape(1, -1),
        "w_uq_ctx": w_uq_ctx, "w_uq_lat": w_uq_lat, "w_uk_t": w_uk_t, "w_uv": w_uv,
        "dt_bias128": pad128(ssd_dt_bias[lyr]), "a128": pad128(-jnp.exp(ssd_A_log[lyr])),
        "d_exp": jnp.repeat(ssd_D[lyr], SSD_HEADDIM).reshape(1, -1),
        "ssd_norm_g": ssd_norm_g[lyr].reshape(1, -1),
        "w_o_mla": w_o_mla[lyr], "w_o_ssd": w_o_ssd[lyr], "w_out": w_out[lyr],
        "norm_ffn_g": norm_ffn_g[lyr].reshape(1, -1), "w_up": w_up[lyr],
        "ffn_conv_w": ffn_conv_w[lyr], "ffn_conv_b": ffn_conv_b[lyr].reshape(1, -1),
        "w_down": w_down[lyr], "final_norm_g": final_norm_g.reshape(1, -1),
    }

    y_p, ckv_p, kr_p, st_p = _trunk_pass(x_prompt, mod48, lambda r: 0, wts, None, False)
    ctx = (cache_ckv[:, lyr], cache_krope[:, lyr], state_ssd[:, lyr])
    y_s, _, _, _ = _trunk_pass(x_sample, mod48, lambda r: 1 + r, wts, ctx, True)
    return y_p, y_s, ckv_p[:, None], kr_p[:, None], st_p[:, None]
```

```python
import functools
import math

import jax
import jax.numpy as jnp
import numpy as np
from jax import lax
from jax.experimental import pallas as pl
from jax.experimental.pallas import tpu as pltpu

F32 = jnp.float32
BF16 = jnp.bfloat16

D_MODEL = 1024
GRID_W = 64
N_HEADS = 8
QK_NOPE = 128
QK_ROPE = 64
V_HEAD = 128
Q_LORA = 256
KV_LORA = 256
ROPE_BASE = 10000.0
SSD_HEADS = 16
SSD_HEADDIM = 64
SSD_INNER = SSD_HEADS * SSD_HEADDIM
SSD_GROUPS = 4
SSD_STATE = 64
SSD_CHUNK = 128
D_FF = 2816
EPS = 1e-6

ROW_GROUP = 2048
IN_TILE = 512
SSD_CHUNKS_PER_STEP = 4
FFN_TILE = 256
TOKEN_TILE = 512
ATTN_Q_TILE = 512
CONV_CHUNK = 512
FFN_DOWN_LAG = 2
CONV_PAD = 8
VMEM_LIMIT = 56 * 1024 * 1024
MLA_VMEM_LIMIT = 62 * 1024 * 1024
NEG_BIG = -1e30
LOG2_E = 1.4426950408889634


def _sigmoid(x):
    return 1.0 / (1.0 + jnp.exp(-x))


def _silu(x):
    return x * _sigmoid(x)


def _softplus(x):
    e = jnp.exp(-jnp.abs(x))
    u = 1.0 + e
    log1p_e = jnp.where(u == 1.0, e, e * jnp.log(u) / jnp.where(u == 1.0, 1.0, u - 1.0))
    return jnp.maximum(x, 0.0) + log1p_e


def _rmsnorm(x, g):
    return x * lax.rsqrt(jnp.mean(x * x, axis=-1, keepdims=True) + EPS) * g


def _dot(a, b):
    return jnp.dot(a, b, preferred_element_type=F32)


def _dot_nt(a, b):
    return lax.dot_general(a, b, (((1,), (1,)), ((), ())), preferred_element_type=F32)


def _params(*sem, vmem_limit=VMEM_LIMIT):
    return pltpu.CompilerParams(dimension_semantics=sem, vmem_limit_bytes=vmem_limit)


def _norm_mod(x, g_ref, sc_ref, sh_ref):
    return (_rmsnorm(x, g_ref[...]) * (1.0 + sc_ref[...]) + sh_ref[...]).astype(BF16)


def _zero_conv_pads(u_scr):
    zeros = jnp.zeros((CONV_PAD, u_scr.shape[1]), F32)
    u_scr[0:CONV_PAD, :] = zeros
    u_scr[CONV_PAD + ROW_GROUP:2 * CONV_PAD + ROW_GROUP, :] = zeros


def _stage_rows(c):
    return slice(CONV_PAD + c * CONV_CHUNK, CONV_PAD + (c + 1) * CONV_CHUNK)


def _dwconv3_rows(u_scr, c, cols, w_ref, b_ref, seq_len):
    r0 = c * CONV_CHUNK
    base = CONV_PAD + r0
    width = cols.stop - cols.start
    prev = u_scr[base - 1:base - 1 + CONV_CHUNK, cols]
    cur = u_scr[base:base + CONV_CHUNK, cols]
    nxt = u_scr[base + 1:base + 1 + CONV_CHUNK, cols]
    pos = (lax.broadcasted_iota(jnp.int32, (CONV_CHUNK, width), 0) + r0) & (seq_len - 1)
    if r0 % seq_len == 0 or CONV_CHUNK > seq_len:
        prev = jnp.where(pos == 0, 0.0, prev)
    if (r0 + CONV_CHUNK) % seq_len == 0 or CONV_CHUNK > seq_len:
        nxt = jnp.where(pos == seq_len - 1, 0.0, nxt)
    return prev * w_ref[0:1, :] + cur * w_ref[1:2, :] + nxt * w_ref[2:3, :] + b_ref[...]


def _ada_kernel(c_ref, w_ref, b_ref, o_ref):
    a = _silu(c_ref[...]).astype(BF16)
    o_ref[...] = _dot(a, w_ref[...].astype(BF16)) + b_ref[...]


def _ada(cvec, w_ada, b_ada):
    tn = 1536
    return pl.pallas_call(
        _ada_kernel,
        grid=(6 * D_MODEL // tn,),
        in_specs=[pl.BlockSpec((8, D_MODEL), lambda j: (0, 0)),
                  pl.BlockSpec((D_MODEL, tn), lambda j: (0, j)),
                  pl.BlockSpec((1, tn), lambda j: (0, j))],
        out_specs=pl.BlockSpec((8, tn), lambda j: (0, j)),
        out_shape=jax.ShapeDtypeStruct((8, 6 * D_MODEL), F32),
        compiler_params=_params("arbitrary"),
        name="ada_mod",
    )(cvec, w_ada, b_ada.reshape(1, -1))


IN_SPLITS = (Q_LORA, KV_LORA, QK_ROPE, SSD_INNER, SSD_INNER, SSD_GROUPS * SSD_STATE, SSD_GROUPS * SSD_STATE,
             2 * SSD_HEADS, D_MODEL, D_MODEL)
IN_OFFSETS = tuple(int(v) for v in np.cumsum((0,) + IN_SPLITS))


def _regroup_kernel(w_ref, o_ref):
    dst = 0

    def put_block(block):
        nonlocal dst
        o_ref[:, dst:dst + 128] = block.T.astype(BF16)
        dst += 128

    def piece(i):
        for off in range(0, IN_SPLITS[i], 128):
            put_block(w_ref[IN_OFFSETS[i] + off:IN_OFFSETS[i] + off + 128, :])

    def padded(parts):
        n = sum(p.shape[0] for p in parts)
        put_block(jnp.concatenate(parts + [jnp.zeros((128 - n, w_ref.shape[1]), F32)], axis=0))

    for i in (3, 8, 9, 5, 6, 4, 0, 1):
        piece(i)
    kr0, n = IN_OFFSETS[2], QK_ROPE // 4
    padded([w_ref[kr0:kr0 + QK_ROPE, :]])
    padded([w_ref[kr0 + blk * n:kr0 + (blk + 1) * n, :] for blk in (1, 0, 3, 2)])
    padded([w_ref[IN_OFFSETS[7]:IN_OFFSETS[7] + IN_SPLITS[7], :]])
    o_ref[:, dst:] = jnp.zeros((o_ref.shape[0], o_ref.shape[1] - dst), BF16)


def _regroup_w_in(w_in_t):
    cols = 256
    n_out = (N_LOW_TILES + N_F32_TILES) * IN_TILE
    return pl.pallas_call(
        _regroup_kernel,
        grid=(D_MODEL // cols,),
        in_specs=[pl.BlockSpec((w_in_t.shape[0], cols), lambda i: (0, i))],
        out_specs=pl.BlockSpec((cols, n_out), lambda i: (i, 0)),
        out_shape=jax.ShapeDtypeStruct((D_MODEL, n_out), BF16),
        compiler_params=_params("arbitrary"),
        name="w_in_regroup",
    )(w_in_t)


N_LOW_TILES, N_F32_TILES = 7, 4
Z_BLK, GM_BLK, GS_BLK = 0, 1, 2
BC_TILE = 6
XS_BLK = 0
MLA_TILE, MISC_TILE = 2, 3
DT_BLK = (MISC_TILE * IN_TILE + 256) // 128
ATTN_HEAD_COLS = 256


def _in_kernel(x_ref, sh_ref, sc_ref, g_ref, w_ref, cw_ref, cb_ref, lo_ref, hi_ref, h_scr, u_scr, *, seq_len):
    j = pl.program_id(1)
    n_chunks = ROW_GROUP // CONV_CHUNK
    rows = lambda c: slice(c * CONV_CHUNK, (c + 1) * CONV_CHUNK)

    def pointwise(fn, o_ref, stage_h=False):
        for c in range(n_chunks):
            if stage_h:
                h_scr[rows(c), :] = _norm_mod(x_ref[rows(c), :], g_ref, sc_ref, sh_ref)
            o_ref[rows(c), :] = fn(_dot(h_scr[rows(c), :], w_ref[...])).astype(o_ref.dtype)

    def conv(o_ref):
        def conv_out(c):
            v = _silu(_dwconv3_rows(u_scr, c, slice(0, IN_TILE), cw_ref, cb_ref, seq_len))
            o_ref[rows(c), :] = v.astype(o_ref.dtype)
        for c in range(n_chunks):
            u_scr[_stage_rows(c), :] = _dot(h_scr[rows(c), :], w_ref[...])
            if c >= 1:
                conv_out(c - 1)
        conv_out(n_chunks - 1)

    @pl.when(j == 0)
    def _():
        _zero_conv_pads(u_scr)
        pointwise(_silu, lo_ref, stage_h=True)

    @pl.when(j == 1)
    def _():
        pointwise(_silu, lo_ref)

    @pl.when((j >= 2) & (j <= 5))
    def _():
        pointwise(_sigmoid, lo_ref)

    @pl.when(j == BC_TILE)
    def _():
        conv(lo_ref)

    @pl.when((j == N_LOW_TILES) | (j == N_LOW_TILES + 1))
    def _():
        conv(hi_ref)

    @pl.when(j >= N_LOW_TILES + MLA_TILE)
    def _():
        pointwise(lambda u: u, hi_ref)


def _in_proj(x2d, mod48, mod_row, norm_g, w_in_r, conv_w, conv_b, seq_len):
    t = x2d.shape[0]
    n_tiles = N_LOW_TILES + N_F32_TILES
    conv_idx = lambda r, j: (0, jnp.where(j == BC_TILE, 2, jnp.clip(j - N_LOW_TILES, 0, 1)))
    return pl.pallas_call(
        functools.partial(_in_kernel, seq_len=seq_len),
        grid=(t // ROW_GROUP, n_tiles),
        in_specs=[pl.BlockSpec((ROW_GROUP, D_MODEL), lambda r, j: (r, 0)),
                  pl.BlockSpec((None, 1, D_MODEL), lambda r, j: (mod_row(r) * 6 + 0, 0, 0)),
                  pl.BlockSpec((None, 1, D_MODEL), lambda r, j: (mod_row(r) * 6 + 1, 0, 0)),
                  pl.BlockSpec((1, D_MODEL), lambda r, j: (0, 0)),
                  pl.BlockSpec((D_MODEL, IN_TILE), lambda r, j: (0, j)),
                  pl.BlockSpec((3, IN_TILE), conv_idx),
                  pl.BlockSpec((1, IN_TILE), conv_idx)],
        out_specs=[pl.BlockSpec((ROW_GROUP, IN_TILE), lambda r, j: (r, jnp.minimum(j, N_LOW_TILES - 1))),
                   pl.BlockSpec((ROW_GROUP, IN_TILE), lambda r, j: (r, jnp.maximum(j - N_LOW_TILES, 0)))],
        out_shape=[jax.ShapeDtypeStruct((t, N_LOW_TILES * IN_TILE), BF16),
                   jax.ShapeDtypeStruct((t, N_F32_TILES * IN_TILE), F32)],
        scratch_shapes=[pltpu.VMEM((ROW_GROUP, D_MODEL), BF16),
                        pltpu.VMEM((ROW_GROUP + 2 * CONV_PAD, IN_TILE), F32)],
        compiler_params=_params("arbitrary", "arbitrary"),
        name="in_proj",
    )(x2d, mod48, mod48, norm_g, w_in_r, conv_w, conv_b)


def _mla_kernel(*refs, latent, has_ctx, emit_cache, seq_len, tq):
    refs = list(refs)
    pm_ref, px_ref, qg_ref, kvg_ref, wuq_ref, wkt_ref, wv_ref = refs[:7]
    del refs[:7]
    if latent:
        cos_ref, sin_ref = refs[:2]
        del refs[:2]
    if has_ctx:
        cckv_ref, ckr_ref = refs[:2]
        del refs[:2]
    o_ref = refs.pop(0)
    if emit_cache:
        ckv_ref, kr_ref = refs[:2]
        del refs[:2]
    k_scr, v_scr = refs[:2]
    del refs[:2]
    if has_ctx:
        kc_scr, vc_scr = refs
    t = pl.program_id(1)
    n_nope = N_HEADS * QK_NOPE

    def put_keys(k_dst, v_dst, rows, ckv_n, kr_bf):
        ckv_bf = ckv_n.astype(BF16)
        v_dst[rows, :] = _dot(ckv_bf, wv_ref[...]).astype(BF16)
        kn_t = _dot_nt(wkt_ref[...], ckv_bf).astype(BF16)
        width = kr_bf.shape[1]
        eye = jnp.where(lax.broadcasted_iota(jnp.int32, (128, width), 0)
                        == lax.broadcasted_iota(jnp.int32, (128, width), 1), 1.0, 0.0).astype(BF16)
        kr_t = _dot_nt(eye, kr_bf).astype(BF16)
        for h in range(N_HEADS):
            base = h * ATTN_HEAD_COLS
            k_dst[base:base + QK_NOPE, rows] = kn_t[h * QK_NOPE:(h + 1) * QK_NOPE, :]
            k_dst[base + QK_NOPE:base + ATTN_HEAD_COLS, rows] = kr_t

    @pl.when(t == 0)
    def _():
        step = min(seq_len, TOKEN_TILE)
        for r0 in range(0, seq_len, step):
            rows = slice(r0, r0 + step)
            ckv_n = _rmsnorm(pm_ref[0, rows, Q_LORA:], kvg_ref[...])
            kr = px_ref[0, rows, 0:128]
            if latent:
                kr = kr * cos_ref[rows, :] + px_ref[0, rows, 128:256] * sin_ref[rows, :]
            if emit_cache:
                ckv_ref[0, rows, :] = ckv_n
                kr_ref[0, rows, :] = kr[:, :QK_ROPE]
            put_keys(k_scr, v_scr, rows, ckv_n, kr.astype(BF16))
        if has_ctx:
            past = cckv_ref.shape[1]
            put_keys(kc_scr, vc_scr, slice(0, past), cckv_ref[0], ckr_ref[0].astype(BF16))

    qrows = pl.ds(pl.multiple_of(t * tq, tq), tq)
    scale = LOG2_E / math.sqrt(QK_NOPE + QK_ROPE)
    cqn = _rmsnorm(pm_ref[0, qrows, 0:Q_LORA], qg_ref[...]).astype(BF16)
    q = _dot(cqn, wuq_ref[...])
    q_rope = q[:, n_nope:2 * n_nope]
    if latent:
        q_rope = (q_rope * jnp.concatenate([cos_ref[qrows, :]] * N_HEADS, axis=1)
                  + q[:, 2 * n_nope:3 * n_nope] * jnp.concatenate([sin_ref[qrows, :]] * N_HEADS, axis=1))
    segs = ([(kc_scr, vc_scr)] if has_ctx else []) + [(k_scr, v_scr)]
    for h in range(N_HEADS):
        head = slice(h * QK_NOPE, (h + 1) * QK_NOPE)
        qk = slice(h * ATTN_HEAD_COLS, (h + 1) * ATTN_HEAD_COLS)
        qh = (jnp.concatenate([q[:, head], q_rope[:, head]], axis=1) * scale).astype(BF16)
        s = [_dot(qh, k[qk, :]) for k, _ in segs]
        m = functools.reduce(jnp.maximum, [jnp.max(si, axis=-1, keepdims=True) for si in s])
        p = [jnp.exp2(si - m) for si in s]
        l = functools.reduce(jnp.add, [jnp.sum(pi, axis=-1, keepdims=True) for pi in p])
        o = functools.reduce(jnp.add, [_dot(pi.astype(BF16), v[:, head]) for pi, (_, v) in zip(p, segs)])
        o_ref[0, :, head] = (o / l).astype(BF16)


def _mla(proj_hi, q_norm_g, kv_norm_g, w_uq_r, w_uk_t, w_uv, rope_tables, ctx, emit_cache):
    b, l, _ = proj_hi.shape
    tq = min(l, ATTN_Q_TILE)
    latent = rope_tables is not None
    has_ctx = ctx is not None
    once = dict(pipeline_mode=pl.Buffered(1))
    per_batch = once if l // tq > 1 else {}
    const2 = lambda i, t: (0, 0)
    in_specs = [pl.BlockSpec((1, l, IN_TILE), lambda i, t: (i, 0, MLA_TILE), **per_batch),
                pl.BlockSpec((1, l, 256), lambda i, t: (i, 0, MISC_TILE * IN_TILE // 256), **per_batch),
                pl.BlockSpec((1, Q_LORA), const2),
                pl.BlockSpec((1, KV_LORA), const2),
                pl.BlockSpec(w_uq_r.shape, const2, **once),
                pl.BlockSpec(w_uk_t.shape, const2, **once),
                pl.BlockSpec(w_uv.shape, const2, **once)]
    args = [proj_hi, proj_hi, q_norm_g, kv_norm_g, w_uq_r, w_uk_t, w_uv]
    scratch = [pltpu.VMEM((N_HEADS * ATTN_HEAD_COLS, l), BF16), pltpu.VMEM((l, N_HEADS * V_HEAD), BF16)]
    if latent:
        in_specs += [pl.BlockSpec((l, 128), const2, **once)] * 2
        args += list(rope_tables)
    if has_ctx:
        past = ctx[0].shape[1]
        in_specs += [pl.BlockSpec((1, past, KV_LORA), lambda i, t: (i, 0, 0), **per_batch),
                     pl.BlockSpec((1, past, QK_ROPE), lambda i, t: (i, 0, 0), **per_batch)]
        args += list(ctx)
        scratch += [pltpu.VMEM((N_HEADS * ATTN_HEAD_COLS, past), BF16), pltpu.VMEM((past, N_HEADS * V_HEAD), BF16)]
    out_specs = [pl.BlockSpec((1, tq, N_HEADS * V_HEAD), lambda i, t: (i, t, 0))]
    out_shape = [jax.ShapeDtypeStruct((b, l, N_HEADS * V_HEAD), BF16)]
    if emit_cache:
        out_specs += [pl.BlockSpec((1, l, KV_LORA), lambda i, t: (i, 0, 0)),
                      pl.BlockSpec((1, l, QK_ROPE), lambda i, t: (i, 0, 0))]
        out_shape += [jax.ShapeDtypeStruct((b, l, KV_LORA), F32), jax.ShapeDtypeStruct((b, l, QK_ROPE), F32)]
    return pl.pallas_call(
        functools.partial(_mla_kernel, latent=latent, has_ctx=has_ctx, emit_cache=emit_cache, seq_len=l, tq=tq),
        grid=(b, l // tq),
        in_specs=in_specs,
        out_specs=out_specs,
        out_shape=out_shape,
        scratch_shapes=scratch,
        compiler_params=_params("arbitrary", "arbitrary", vmem_limit=MLA_VMEM_LIMIT),
        name="mla_attention",
    )(*args)


def _split3(x):
    hi = x.astype(BF16)
    r = x - hi.astype(F32)
    mid = r.astype(BF16)
    lo = (r - mid.astype(F32)).astype(BF16)
    return hi, mid, lo


def _exact_dot(parts, sel):
    return functools.reduce(jnp.add, [_dot(p, sel) for p in parts])


def _exact_dot_rows(sel, parts):
    return functools.reduce(jnp.add, [_dot(sel, p) for p in parts])


HEADS_PER_GROUP = SSD_HEADS // SSD_GROUPS
GROUP_COLS = HEADS_PER_GROUP * SSD_HEADDIM


def _ssd_kernel(*refs, nc, cps, has_h0):
    if has_h0:
        (xs_ref, zs_ref, bc_ref, dt_ref, dtb_ref, a_ref, dx_ref, h0_ref, y_ref, ht_ref,
         yl_scr, acum_scr, src_scr, tot_scr, sb_scr, h_scr, esel_scr) = refs
    else:
        (xs_ref, zs_ref, bc_ref, dt_ref, dtb_ref, a_ref, dx_ref, y_ref, ht_ref,
         yl_scr, acum_scr, src_scr, tot_scr, sb_scr, h_scr, esel_scr) = refs
    s = pl.program_id(1)
    q = SSD_CHUNK
    n_bc = SSD_GROUPS * SSD_STATE
    lane = lax.broadcasted_iota(jnp.int32, (q, 128), 1)
    low_half = lane < SSD_HEADDIM
    ii = lax.broadcasted_iota(jnp.int32, (q, q), 0)
    jj = lax.broadcasted_iota(jnp.int32, (q, q), 1)
    lower, upper = ii >= jj, ii <= jj

    def lane_bcast(parts, d):
        return _exact_dot(parts, esel_scr[d])

    def stacked_states(d):
        return [jnp.concatenate([h_scr[d, 2 * i], h_scr[d, 2 * i + 1]], axis=0).astype(BF16)
                for i in range(SSD_GROUPS // 2)]

    def group_c(bc, g):
        i, r = divmod(g, 2)
        cpair = bc[:, n_bc + i * 128:n_bc + (i + 1) * 128]
        return jnp.where(low_half if r == 0 else ~low_half, cpair, 0.0).astype(BF16)

    @pl.when(s == 0)
    def _():
        k = lax.broadcasted_iota(jnp.int32, (128, SSD_INNER), 0)
        head = lax.broadcasted_iota(jnp.int32, (128, SSD_INNER), 1) // SSD_HEADDIM
        for d in range(2):
            esel_scr[d] = jnp.where(k == d * SSD_HEADS + head, 1.0, 0.0).astype(BF16)
        if has_h0:
            for d in range(2):
                for g in range(SSD_GROUPS):
                    hpn = h0_ref[0, d, g * HEADS_PER_GROUP:(g + 1) * HEADS_PER_GROUP].reshape(GROUP_COLS, SSD_STATE)
                    h_scr[d, g] = hpn.T
        else:
            h_scr[...] = jnp.zeros(h_scr.shape, F32)
        tri_f = jnp.where(lower, 1.0, 0.0).astype(BF16)
        tri_b = jnp.where(upper, 1.0, 0.0).astype(BF16)
        fwd_col = lane < SSD_HEADS
        for c in range(nc):
            crow = slice(c * q, (c + 1) * q)
            dt = _softplus(dt_ref[0, crow, :] + dtb_ref[...])
            parts = _split3(dt * a_ref[...])
            acum = jnp.where(fwd_col, _exact_dot_rows(tri_f, parts),
                             _exact_dot_rows(tri_b, parts))
            acum = acum * LOG2_E
            acum_scr[crow, :] = acum
            tot = jnp.where(fwd_col[0:1], acum[q - 1:q, :], acum[0:1, :])
            tot_scr[c] = jnp.broadcast_to(tot, (8, 128))
            src_scr[c] = (acum - jnp.log2(dt)).T

    def first_sweep(c, blk):
        rows = pl.ds(pl.multiple_of(c * q, q), q)
        x = xs_ref[0, blk, :]
        bc = bc_ref[0, blk, :].astype(F32)
        acum = acum_scr[rows, :]
        src_t = src_scr[c]
        tot8 = tot_scr[c]
        e = jnp.exp2(acum)
        e_hi = e.astype(BF16)
        eb_f = lane_bcast([e_hi, (e - e_hi.astype(F32)).astype(BF16)], 0)
        cd_f = jnp.exp2(lane_bcast(_split3(tot8), 0))[0:1]
        b_t = [bc[:, i * 128:(i + 1) * 128].T for i in range(SSD_GROUPS // 2)]
        h_in = stacked_states(0)
        for g in range(SSD_GROUPS):
            i, r = divmod(g, 2)
            cm = group_c(bc, g)
            scores = _dot_nt(cm, bc[:, i * 128:(i + 1) * 128].astype(BF16))
            bg_t = b_t[i][r * SSD_STATE:(r + 1) * SSD_STATE, :]
            gcols = slice(g * GROUP_COLS, (g + 1) * GROUP_COLS)
            y_off = _dot(cm, h_in[i]) * eb_f[:, gcols]
            for t in range(HEADS_PER_GROUP // 2):
                pc = slice((2 * g + t) * 128, (2 * g + t + 1) * 128)
                xp = x[:, pc]
                x2 = jnp.concatenate([jnp.where(low_half, xp, 0.0), jnp.where(low_half, 0.0, xp)], axis=0).astype(BF16)
                m, sf, sb = [], [], []
                for u in range(2):
                    kf = g * HEADS_PER_GROUP + 2 * t + u
                    kb = SSD_HEADS + kf
                    af_col = jnp.broadcast_to(acum[:, kf:kf + 1], (q, q))
                    ab_col = jnp.broadcast_to(acum[:, kb:kb + 1], (q, q))
                    sf_row, sb_row = src_t[kf:kf + 1, :], src_t[kb:kb + 1, :]
                    decay = (jnp.exp2(jnp.where(lower, af_col - sf_row, NEG_BIG))
                             + jnp.exp2(jnp.where(upper, ab_col - sb_row, NEG_BIG)))
                    m.append((scores * decay).astype(BF16))
                    wf = jnp.exp2(tot8[0:1, kf:kf + 1] - sf_row)
                    wb = jnp.exp2(tot8[0:1, kb:kb + 1] - sb_row)
                    sf.append((bg_t * wf).astype(BF16))
                    sb.append((bg_t * wb).astype(BF16))
                y_pair = _dot(jnp.concatenate(m, axis=1), x2)
                tc = slice(t * 128, (t + 1) * 128)
                yl_scr[rows, pc] = y_pair + y_off[:, tc] + dx_ref[:, pc] * xp
                h_scr[0, g, :, tc] = h_scr[0, g, :, tc] * cd_f[:, pc] + _dot(jnp.concatenate(sf, axis=1), x2)
                sb_scr[c, g, :, tc] = _dot(jnp.concatenate(sb, axis=1), x2)

    def second_sweep(c, blk):
        rows = pl.ds(pl.multiple_of(c * q, q), q)
        bc = bc_ref[0, blk, :].astype(F32)
        e = jnp.exp2(acum_scr[rows, :])
        e_hi = e.astype(BF16)
        eb_b = lane_bcast([e_hi, (e - e_hi.astype(F32)).astype(BF16)], 1)
        cd_b = jnp.exp2(lane_bcast(_split3(tot_scr[c]), 1))[0:1]
        h_in = stacked_states(1)
        for g in range(SSD_GROUPS):
            gcols = slice(g * GROUP_COLS, (g + 1) * GROUP_COLS)
            y_off = _dot(group_c(bc, g), h_in[g // 2]) * eb_b[:, gcols]
            y_ref[0, blk, gcols] = ((yl_scr[rows, gcols] + y_off) * zs_ref[0, blk, gcols].astype(F32)).astype(y_ref.dtype)
            h_scr[1, g] = h_scr[1, g] * cd_b[:, gcols] + sb_scr[c, g]

    n_steps = nc // cps
    one_step = n_steps == 1

    @pl.when(s < n_steps)
    def _():
        for ci in range(cps):
            first_sweep(s * cps + ci, slice(ci * q, (ci + 1) * q))

    @pl.when(jnp.logical_or(one_step, s >= n_steps))
    def _():
        blk_id = s * 0 if one_step else 2 * n_steps - 1 - s
        for ci in reversed(range(cps)):
            second_sweep(blk_id * cps + ci, slice(ci * q, (ci + 1) * q))

    @pl.when(jnp.logical_or(one_step, s == 2 * n_steps - 1))
    def _():
        for d in range(2):
            for g in range(SSD_GROUPS):
                ht_ref[0, d, g * HEADS_PER_GROUP:(g + 1) * HEADS_PER_GROUP] = h_scr[d, g].T.reshape(
                    HEADS_PER_GROUP, SSD_HEADDIM, SSD_STATE)


def _ssd(proj_lo, proj_hi, h0, dt_bias128, a128, d_exp):
    b, l, _ = proj_hi.shape
    q = SSD_CHUNK
    nc = l // q
    cps = min(nc, SSD_CHUNKS_PER_STEP)
    n_steps = nc // cps
    n_grid = 1 if n_steps == 1 else 2 * n_steps
    rows = cps * q
    early = lambda s: jnp.minimum(s, n_steps - 1)
    both = lambda s: jnp.where(s < n_steps, s, 2 * n_steps - 1 - s)
    late = lambda s: jnp.where(s < n_steps, n_steps - 1, 2 * n_steps - 1 - s)
    st_shape = (1, 2, SSD_HEADS, SSD_HEADDIM, SSD_STATE)
    st_spec = pl.BlockSpec(st_shape, lambda i, s: (i, 0, 0, 0, 0))
    has_h0 = h0 is not None
    return pl.pallas_call(
        functools.partial(_ssd_kernel, nc=nc, cps=cps, has_h0=has_h0),
        grid=(b, n_grid),
        in_specs=[pl.BlockSpec((1, rows, SSD_INNER), lambda i, s: (i, early(s), XS_BLK)),
                  pl.BlockSpec((1, rows, SSD_INNER), lambda i, s: (i, late(s), Z_BLK)),
                  pl.BlockSpec((1, rows, IN_TILE), lambda i, s: (i, both(s), BC_TILE)),
                  pl.BlockSpec((1, l, 128), lambda i, s: (i, 0, DT_BLK)),
                  pl.BlockSpec((1, 128), lambda i, s: (0, 0)),
                  pl.BlockSpec((1, 128), lambda i, s: (0, 0)),
                  pl.BlockSpec((1, SSD_INNER), lambda i, s: (0, 0))] + ([st_spec] if has_h0 else []),
        out_specs=[pl.BlockSpec((1, rows, SSD_INNER), lambda i, s: (i, late(s), 0)),
                   pl.BlockSpec(st_shape, lambda i, s: (i, 0, 0, 0, 0))],
        out_shape=[jax.ShapeDtypeStruct((b, l, SSD_INNER), BF16),
                   jax.ShapeDtypeStruct((b,) + st_shape[1:], F32)],
        scratch_shapes=[pltpu.VMEM((l, SSD_INNER), F32),
                        pltpu.VMEM((l, 128), F32),
                        pltpu.VMEM((nc, 128, q), F32),
                        pltpu.VMEM((nc, 8, 128), F32),
                        pltpu.VMEM((nc, SSD_GROUPS, SSD_STATE, GROUP_COLS), F32),
                        pltpu.VMEM((2, SSD_GROUPS, SSD_STATE, GROUP_COLS), F32),
                        pltpu.VMEM((2, 128, SSD_INNER), BF16)],
        compiler_params=_params("arbitrary", "arbitrary"),
        name="ssd_scan",
    )(proj_hi, proj_lo, proj_lo, proj_hi, dt_bias128, a128, d_exp, *([h0] if has_h0 else []))


def _merge_kernel(attn_ref, yz_ref, gm_ref, gs_ref, x_ref, g1_ref, ng_ref, womla_ref, wossd_ref, wout_ref, o_ref, w_scr):
    @pl.when(pl.program_id(0) == 0)
    def _():
        w_scr[0] = womla_ref[...].astype(BF16)
        w_scr[1] = wossd_ref[...].astype(BF16)
        w_scr[2] = wout_ref[...].astype(BF16)

    o_mla = _dot(attn_ref[...], w_scr[0])
    o_ssd = _dot(_rmsnorm(yz_ref[...].astype(F32), ng_ref[...]).astype(BF16), w_scr[1])
    merged = gm_ref[...].astype(F32) * o_mla + gs_ref[...].astype(F32) * o_ssd
    o_ref[...] = x_ref[...] + g1_ref[...] * _dot(merged.astype(BF16), w_scr[2])


def _merge(attn2d, yz2d, proj, x2d, mod48, mod_row, ssd_norm_g, w_o_mla, w_o_ssd, w_out):
    t = x2d.shape[0]
    tm = TOKEN_TILE
    row = lambda i: (i, 0)
    const = lambda i: (0, 0)
    wspec = pl.BlockSpec((D_MODEL, D_MODEL), const, pipeline_mode=pl.Buffered(1))
    return pl.pallas_call(
        _merge_kernel,
        grid=(t // tm,),
        in_specs=[pl.BlockSpec((tm, D_MODEL), row),
                  pl.BlockSpec((tm, D_MODEL), row),
                  pl.BlockSpec((tm, D_MODEL), lambda i: (i, GM_BLK)),
                  pl.BlockSpec((tm, D_MODEL), lambda i: (i, GS_BLK)),
                  pl.BlockSpec((tm, D_MODEL), row),
                  pl.BlockSpec((None, 1, D_MODEL), lambda i: (mod_row(i * tm // ROW_GROUP) * 6 + 2, 0, 0)),
                  pl.BlockSpec((1, D_MODEL), const),
                  wspec, wspec, wspec],
        out_specs=pl.BlockSpec((tm, D_MODEL), row),
        out_shape=jax.ShapeDtypeStruct((t, D_MODEL), F32),
        scratch_shapes=[pltpu.VMEM((3, D_MODEL, D_MODEL), BF16)],
        compiler_params=_params("arbitrary"),
        name="merge_out",
    )(attn2d, yz2d, proj, proj, x2d, mod48, ssd_norm_g, w_o_mla, w_o_ssd, w_out)


def _ffn_kernel(x_ref, sh_ref, sc_ref, g2_ref, ng_ref, wg_ref, wv_ref, cwg_ref, cwv_ref, cbg_ref, cbv_ref, wd_ref,
                fg_ref, o_ref, h_scr, wup_scr, wd_scr, u_scr, *, seq_len):
    j = pl.program_id(1)
    n_chunks = ROW_GROUP // CONV_CHUNK
    rows = lambda c: slice(c * CONV_CHUNK, (c + 1) * CONV_CHUNK)

    wup_scr[:, 0:FFN_TILE] = wg_ref[...].astype(BF16)
    wup_scr[:, FFN_TILE:2 * FFN_TILE] = wv_ref[...].astype(BF16)
    wd_scr[...] = wd_ref[...].astype(BF16)

    def gated(c):
        ug = _dwconv3_rows(u_scr, c, slice(0, FFN_TILE), cwg_ref, cbg_ref, seq_len)
        uv = _dwconv3_rows(u_scr, c, slice(FFN_TILE, 2 * FFN_TILE), cwv_ref, cbv_ref, seq_len)
        return (_silu(ug) * uv).astype(BF16)

    def pipeline(first, last):
        act = {}
        lag = FFN_DOWN_LAG
        for c in range(n_chunks + lag):
            if c < n_chunks:
                if first:
                    h_scr[rows(c), :] = _norm_mod(x_ref[rows(c), :], ng_ref, sc_ref, sh_ref)
                u_scr[_stage_rows(c), :] = _dot(h_scr[rows(c), :], wup_scr[...])
            if c >= lag:
                r = rows(c - lag)
                acc = _dot(act.pop(c - lag), wd_scr[...])
                if not first:
                    acc = o_ref[r, :] + acc
                if last:
                    acc = _rmsnorm(x_ref[r, :] + g2_ref[...] * acc, fg_ref[...])
                o_ref[r, :] = acc
            if 1 <= c <= n_chunks:
                act[c - 1] = gated(c - 1)

    last_j = pl.num_programs(1) - 1

    @pl.when(j == 0)
    def _():
        _zero_conv_pads(u_scr)
        pipeline(True, False)

    @pl.when((j > 0) & (j < last_j))
    def _():
        pipeline(False, False)

    @pl.when(j == last_j)
    def _():
        pipeline(False, True)


def _ffn(x2d, mod48, mod_row, norm_g, w_up, conv_w, conv_b, w_down, final_g, seq_len):
    t = x2d.shape[0]
    nj = D_FF // FFN_TILE
    gate = lambda r, j: (0, j)
    val = lambda r, j: (0, nj + j)
    const = lambda r, j: (0, 0)
    mod = lambda k: pl.BlockSpec((None, 1, D_MODEL), lambda r, j: (mod_row(r) * 6 + k, 0, 0))
    return pl.pallas_call(
        functools.partial(_ffn_kernel, seq_len=seq_len),
        grid=(t // ROW_GROUP, nj),
        in_specs=[pl.BlockSpec((ROW_GROUP, D_MODEL), lambda r, j: (r, 0)),
                  mod(3), mod(4), mod(5),
                  pl.BlockSpec((1, D_MODEL), const),
                  pl.BlockSpec((D_MODEL, FFN_TILE), gate),
                  pl.BlockSpec((D_MODEL, FFN_TILE), val),
                  pl.BlockSpec((3, FFN_TILE), gate),
                  pl.BlockSpec((3, FFN_TILE), val),
                  pl.BlockSpec((1, FFN_TILE), gate),
                  pl.BlockSpec((1, FFN_TILE), val),
                  pl.BlockSpec((FFN_TILE, D_MODEL), lambda r, j: (j, 0)),
                  pl.BlockSpec((1, D_MODEL), const)],
        out_specs=pl.BlockSpec((ROW_GROUP, D_MODEL), lambda r, j: (r, 0)),
        out_shape=jax.ShapeDtypeStruct((t, D_MODEL), F32),
        scratch_shapes=[pltpu.VMEM((ROW_GROUP, D_MODEL), BF16),
                        pltpu.VMEM((D_MODEL, 2 * FFN_TILE), BF16),
                        pltpu.VMEM((FFN_TILE, D_MODEL), BF16),
                        pltpu.VMEM((ROW_GROUP + 2 * CONV_PAD, 2 * FFN_TILE), F32)],
        compiler_params=_params("arbitrary", "arbitrary"),
        name="conv_ffn",
    )(x2d, mod48, mod48, mod48, norm_g, w_up, w_up, conv_w, conv_w, conv_b, conv_b, w_down, final_g)


def _rope_tables(seq_len):
    t = np.arange(seq_len)
    row = (t // GRID_W).astype(np.float32)
    col = (t % GRID_W).astype(np.float32)
    n = QK_ROPE // 4
    inv = (np.float32(ROPE_BASE) ** (-np.arange(n, dtype=np.float32) / np.float32(n))).astype(np.float32)
    ar, ac = row[:, None] * inv, col[:, None] * inv
    cos64 = np.concatenate([np.cos(ar), np.cos(ar), np.cos(ac), np.cos(ac)], axis=1)
    sin64 = np.concatenate([-np.sin(ar), np.sin(ar), -np.sin(ac), np.sin(ac)], axis=1)
    zeros = np.zeros_like(cos64)
    return (jnp.asarray(np.concatenate([cos64, zeros], axis=1), F32),
            jnp.asarray(np.concatenate([sin64, zeros], axis=1), F32))


def _swap_rope_halves(w):
    lead = w.shape[:-1]
    return w.reshape(lead + (2, 2, QK_ROPE // 4))[..., ::-1, :].reshape(lead + (QK_ROPE,))


def _trunk_pass(x, mod48, mod_row, wts, ctx, latent):
    b, l, _ = x.shape
    x2d = x.reshape(b * l, D_MODEL)
    proj_lo, proj_hi = _in_proj(x2d, mod48, mod_row, wts["norm_attn_g"], wts["w_in_r"], wts["ssd_conv_w"],
                                wts["ssd_conv_b"], l)
    rope = _rope_tables(l) if latent else None
    w_uq_r = wts["w_uq_lat"] if latent else wts["w_uq_ctx"]
    shape3 = lambda a, n: a.reshape(b, n, a.shape[-1])
    h0 = None
    mla_ctx = None
    if ctx is not None:
        cache_ckv, cache_krope, h0 = ctx
        mla_ctx = (cache_ckv, cache_krope)
    emit_cache = ctx is None
    mla_out = _mla(shape3(proj_hi, l), wts["q_norm_g"], wts["kv_norm_g"], w_uq_r, wts["w_uk_t"], wts["w_uv"],
                   rope, mla_ctx, emit_cache)
    attn, ckv_n, kr3 = mla_out if emit_cache else (mla_out[0], None, None)
    yz, h_t = _ssd(shape3(proj_lo, l), shape3(proj_hi, l), h0, wts["dt_bias128"], wts["a128"], wts["d_exp"])
    x1 = _merge(attn.reshape(b * l, -1), yz.reshape(b * l, -1), proj_lo, x2d, mod48, mod_row, wts["ssd_norm_g"],
                wts["w_o_mla"], wts["w_o_ssd"], wts["w_out"])
    y = _ffn(x1, mod48, mod_row, wts["norm_ffn_g"], wts["w_up"], wts["ffn_conv_w"], wts["ffn_conv_b"], wts["w_down"],
             wts["final_norm_g"], l)
    return y.reshape(b, l, D_MODEL), ckv_n, kr3, h_t


def kernel(x_prompt, x_sample, c, cache_ckv, cache_krope, state_ssd, c_ctx, w_ada, b_ada, norm_attn_g, w_in, q_norm_g,
           kv_norm_g, w_uq, w_ukv, w_o_mla, ssd_conv_w, ssd_conv_b, ssd_dt_bias, ssd_A_log, ssd_D, ssd_norm_g, w_o_ssd,
           w_out, norm_ffn_g, w_up, ffn_conv_w, ffn_conv_b, w_down, final_norm_g):
    depth = w_in.shape[0]
    assert depth == 1, "single trunk layer"
    dec_b = x_sample.shape[0]
    assert x_sample.shape[1] == ROW_GROUP and ROW_GROUP % x_prompt.shape[1] == 0
    lyr = 0

    cvec = jnp.zeros((8, D_MODEL), F32).at[0].set(c_ctx).at[1:1 + dec_b].set(c)
    mod48 = _ada(cvec, w_ada[lyr], b_ada[lyr]).reshape(8 * 6, 1, D_MODEL)

    w_in_r = _regroup_w_in(w_in[lyr].T)
    wq = w_uq[lyr].reshape(Q_LORA, N_HEADS, QK_NOPE + QK_ROPE)
    wq_nope = wq[:, :, :QK_NOPE].reshape(Q_LORA, -1)
    wq_rope = wq[:, :, QK_NOPE:]
    pad_rope = lambda w: jnp.pad(w, ((0, 0), (0, 0), (0, 128 - QK_ROPE))).reshape(Q_LORA, -1)
    w_uq_ctx = jnp.concatenate([wq_nope, pad_rope(wq_rope)], axis=1).astype(BF16)
    w_uq_lat = jnp.concatenate([wq_nope, pad_rope(wq_rope), pad_rope(_swap_rope_halves(wq_rope))], axis=1).astype(BF16)
    wkv = w_ukv[lyr].reshape(KV_LORA, N_HEADS, QK_NOPE + V_HEAD)
    w_uk_t = wkv[:, :, :QK_NOPE].reshape(KV_LORA, -1).T.astype(BF16)
    w_uv = wkv[:, :, QK_NOPE:].reshape(KV_LORA, -1).astype(BF16)
    pad128 = lambda a: jnp.pad(a.reshape(1, -1), ((0, 0), (0, 128 - a.size)))
    wts = {
        "norm_attn_g": norm_attn_g[lyr].reshape(1, -1), "w_in_r": w_in_r,
        "ssd_conv_w": ssd_conv_w[lyr], "ssd_conv_b": ssd_conv_b[lyr].reshape(1, -1),
        "q_norm_g": q_norm_g[lyr].reshape(1, -1), "kv_norm_g": kv_norm_g[lyr].reshape(1, -1),
        "w_uq_ctx": w_uq_ctx, "w_uq_lat": w_uq_lat, "w_uk_t": w_uk_t, "w_uv": w_uv,
        "dt_bias128": pad128(ssd_dt_bias[lyr]), "a128": pad128(-jnp.exp(ssd_A_log[lyr])),
        "d_exp": jnp.repeat(ssd_D[lyr], SSD_HEADDIM).reshape(1, -1),
        "ssd_norm_g": ssd_norm_g[lyr].reshape(1, -1),
        "w_o_mla": w_o_mla[lyr], "w_o_ssd": w_o_ssd[lyr], "w_out": w_out[lyr],
        "norm_ffn_g": norm_ffn_g[lyr].reshape(1, -1), "w_up": w_up[lyr],
        "ffn_conv_w": ffn_conv_w[lyr], "ffn_conv_b": ffn_conv_b[lyr].reshape(1, -1),
        "w_down": w_down[lyr], "final_norm_g": final_norm_g.reshape(1, -1),
    }

    y_p, ckv_p, kr_p, st_p = _trunk_pass(x_prompt, mod48, lambda r: 0, wts, None, False)
    ctx = (cache_ckv[:, lyr], cache_krope[:, lyr], state_ssd[:, lyr])
    y_s, _, _, _ = _trunk_pass(x_sample, mod48, lambda r: 1 + r, wts, ctx, True)
    return y_p, y_s, ckv_p[:, None], kr_p[:, None], st_p[:, None]
```

```python
import functools
import math

import jax
import jax.numpy as jnp
import numpy as np
from jax import lax
from jax.experimental import pallas as pl
from jax.experimental.pallas import tpu as pltpu

F32 = jnp.float32
BF16 = jnp.bfloat16

D_MODEL = 1024
GRID_W = 64
N_HEADS = 8
QK_NOPE = 128
QK_ROPE = 64
V_HEAD = 128
Q_LORA = 256
KV_LORA = 256
ROPE_BASE = 10000.0
SSD_HEADS = 16
SSD_HEADDIM = 64
SSD_INNER = SSD_HEADS * SSD_HEADDIM
SSD_GROUPS = 4
SSD_STATE = 64
SSD_CHUNK = 128
D_FF = 2816
EPS = 1e-6

ROW_GROUP = 2048
IN_TILE = 512
SSD_CHUNKS_PER_STEP = 4
FFN_TILE = 256
TOKEN_TILE = 512
ATTN_Q_TILE = 512
IN_CHUNK = 256
FFN_CHUNK = 512
FFN_DOWN_LAG = 2
CONV_PAD = 8
VMEM_LIMIT = 56 * 1024 * 1024
MLA_VMEM_LIMIT = 62 * 1024 * 1024
NEG_BIG = -1e30
LOG2_E = 1.4426950408889634


def _sigmoid(x):
    return 1.0 / (1.0 + jnp.exp(-x))


def _silu(x):
    return x * _sigmoid(x)


def _softplus(x):
    e = jnp.exp(-jnp.abs(x))
    u = 1.0 + e
    log1p_e = jnp.where(u == 1.0, e, e * jnp.log(u) / jnp.where(u == 1.0, 1.0, u - 1.0))
    return jnp.maximum(x, 0.0) + log1p_e


def _rmsnorm(x, g):
    return x * lax.rsqrt(jnp.mean(x * x, axis=-1, keepdims=True) + EPS) * g


def _dot(a, b):
    return jnp.dot(a, b, preferred_element_type=F32)


def _dot_nt(a, b):
    return lax.dot_general(a, b, (((1,), (1,)), ((), ())), preferred_element_type=F32)


def _params(*sem, vmem_limit=VMEM_LIMIT):
    return pltpu.CompilerParams(dimension_semantics=sem, vmem_limit_bytes=vmem_limit)


def _norm_mod(x, g_ref, sc_ref, sh_ref):
    return (_rmsnorm(x, g_ref[...]) * (1.0 + sc_ref[...]) + sh_ref[...]).astype(BF16)


def _zero_conv_pads(u_scr):
    zeros = jnp.zeros((CONV_PAD, u_scr.shape[1]), F32)
    u_scr[0:CONV_PAD, :] = zeros
    u_scr[CONV_PAD + ROW_GROUP:2 * CONV_PAD + ROW_GROUP, :] = zeros


def _stage_rows(c, chunk):
    return slice(CONV_PAD + c * chunk, CONV_PAD + (c + 1) * chunk)


def _dwconv3_rows(u_scr, c, chunk, cols, w_ref, b_ref, seq_len):
    r0 = c * chunk
    base = CONV_PAD + r0
    width = cols.stop - cols.start
    prev = u_scr[base - 1:base - 1 + chunk, cols]
    cur = u_scr[base:base + chunk, cols]
    nxt = u_scr[base + 1:base + 1 + chunk, cols]
    pos = (lax.broadcasted_iota(jnp.int32, (chunk, width), 0) + r0) & (seq_len - 1)
    if r0 % seq_len == 0 or chunk > seq_len:
        prev = jnp.where(pos == 0, 0.0, prev)
    if (r0 + chunk) % seq_len == 0 or chunk > seq_len:
        nxt = jnp.where(pos == seq_len - 1, 0.0, nxt)
    return prev * w_ref[0:1, :] + cur * w_ref[1:2, :] + nxt * w_ref[2:3, :] + b_ref[...]


def _ada_kernel(c_ref, w_ref, b_ref, o_ref):
    a = _silu(c_ref[...]).astype(BF16)
    o_ref[...] = _dot(a, w_ref[...].astype(BF16)) + b_ref[...]


def _ada(cvec, w_ada, b_ada):
    tn = 1536
    return pl.pallas_call(
        _ada_kernel,
        grid=(6 * D_MODEL // tn,),
        in_specs=[pl.BlockSpec((8, D_MODEL), lambda j: (0, 0)),
                  pl.BlockSpec((D_MODEL, tn), lambda j: (0, j)),
                  pl.BlockSpec((1, tn), lambda j: (0, j))],
        out_specs=pl.BlockSpec((8, tn), lambda j: (0, j)),
        out_shape=jax.ShapeDtypeStruct((8, 6 * D_MODEL), F32),
        compiler_params=_params("arbitrary"),
        name="ada_mod",
    )(cvec, w_ada, b_ada.reshape(1, -1))


IN_SPLITS = (Q_LORA, KV_LORA, QK_ROPE, SSD_INNER, SSD_INNER, SSD_GROUPS * SSD_STATE, SSD_GROUPS * SSD_STATE,
             2 * SSD_HEADS, D_MODEL, D_MODEL)
IN_OFFSETS = tuple(int(v) for v in np.cumsum((0,) + IN_SPLITS))


def _regroup_kernel(w_ref, o_ref):
    dst = 0

    def put_block(block):
        nonlocal dst
        o_ref[:, dst:dst + 128] = block.T.astype(BF16)
        dst += 128

    def piece(i):
        for off in range(0, IN_SPLITS[i], 128):
            put_block(w_ref[IN_OFFSETS[i] + off:IN_OFFSETS[i] + off + 128, :])

    def padded(parts):
        n = sum(p.shape[0] for p in parts)
        put_block(jnp.concatenate(parts + [jnp.zeros((128 - n, w_ref.shape[1]), F32)], axis=0))

    for i in (3, 8, 9, 5, 6, 4, 0, 1):
        piece(i)
    kr0, n = IN_OFFSETS[2], QK_ROPE // 4
    padded([w_ref[kr0:kr0 + QK_ROPE, :]])
    padded([w_ref[kr0 + blk * n:kr0 + (blk + 1) * n, :] for blk in (1, 0, 3, 2)])
    padded([w_ref[IN_OFFSETS[7]:IN_OFFSETS[7] + IN_SPLITS[7], :]])
    o_ref[:, dst:] = jnp.zeros((o_ref.shape[0], o_ref.shape[1] - dst), BF16)


def _regroup_w_in(w_in_t):
    cols = 256
    n_out = (N_LOW_TILES + N_F32_TILES) * IN_TILE
    return pl.pallas_call(
        _regroup_kernel,
        grid=(D_MODEL // cols,),
        in_specs=[pl.BlockSpec((w_in_t.shape[0], cols), lambda i: (0, i))],
        out_specs=pl.BlockSpec((cols, n_out), lambda i: (i, 0)),
        out_shape=jax.ShapeDtypeStruct((D_MODEL, n_out), BF16),
        compiler_params=_params("arbitrary"),
        name="w_in_regroup",
    )(w_in_t)


N_LOW_TILES, N_F32_TILES = 7, 4
Z_BLK, GM_BLK, GS_BLK = 0, 1, 2
BC_TILE = 6
XS_BLK = 0
MLA_TILE, MISC_TILE = 2, 3
DT_BLK = (MISC_TILE * IN_TILE + 256) // 128
ATTN_HEAD_COLS = 256


def _in_kernel(x_ref, sh_ref, sc_ref, g_ref, w_ref, cw_ref, cb_ref, lo_ref, hi_ref, h_scr, u_scr, *, seq_len):
    j = pl.program_id(1)
    chunk = IN_CHUNK
    n_chunks = ROW_GROUP // chunk
    rows = lambda c: slice(c * chunk, (c + 1) * chunk)

    def pointwise(fn, o_ref, stage_h=False):
        for c in range(n_chunks):
            if stage_h:
                h_scr[rows(c), :] = _norm_mod(x_ref[rows(c), :], g_ref, sc_ref, sh_ref)
            o_ref[rows(c), :] = fn(_dot(h_scr[rows(c), :], w_ref[...])).astype(o_ref.dtype)

    def conv(o_ref):
        def conv_out(c):
            v = _silu(_dwconv3_rows(u_scr, c, chunk, slice(0, IN_TILE), cw_ref, cb_ref, seq_len))
            o_ref[rows(c), :] = v.astype(o_ref.dtype)
        for c in range(n_chunks):
            u_scr[_stage_rows(c, chunk), :] = _dot(h_scr[rows(c), :], w_ref[...])
            if c >= 1:
                conv_out(c - 1)
        conv_out(n_chunks - 1)

    @pl.when(j == 0)
    def _():
        _zero_conv_pads(u_scr)
        pointwise(_silu, lo_ref, stage_h=True)

    @pl.when(j == 1)
    def _():
        pointwise(_silu, lo_ref)

    @pl.when((j >= 2) & (j <= 5))
    def _():
        pointwise(_sigmoid, lo_ref)

    @pl.when(j == BC_TILE)
    def _():
        conv(lo_ref)

    @pl.when((j == N_LOW_TILES) | (j == N_LOW_TILES + 1))
    def _():
        conv(hi_ref)

    @pl.when(j >= N_LOW_TILES + MLA_TILE)
    def _():
        pointwise(lambda u: u, hi_ref)


def _in_proj(x2d, mod48, mod_row, norm_g, w_in_r, conv_w, conv_b, seq_len):
    t = x2d.shape[0]
    n_tiles = N_LOW_TILES + N_F32_TILES
    conv_idx = lambda r, j: (0, jnp.where(j == BC_TILE, 2, jnp.clip(j - N_LOW_TILES, 0, 1)))
    return pl.pallas_call(
        functools.partial(_in_kernel, seq_len=seq_len),
        grid=(t // ROW_GROUP, n_tiles),
        in_specs=[pl.BlockSpec((ROW_GROUP, D_MODEL), lambda r, j: (r, 0)),
                  pl.BlockSpec((None, 1, D_MODEL), lambda r, j: (mod_row(r) * 6 + 0, 0, 0)),
                  pl.BlockSpec((None, 1, D_MODEL), lambda r, j: (mod_row(r) * 6 + 1, 0, 0)),
                  pl.BlockSpec((1, D_MODEL), lambda r, j: (0, 0)),
                  pl.BlockSpec((D_MODEL, IN_TILE), lambda r, j: (0, j)),
                  pl.BlockSpec((3, IN_TILE), conv_idx),
                  pl.BlockSpec((1, IN_TILE), conv_idx)],
        out_specs=[pl.BlockSpec((ROW_GROUP, IN_TILE), lambda r, j: (r, jnp.minimum(j, N_LOW_TILES - 1))),
                   pl.BlockSpec((ROW_GROUP, IN_TILE), lambda r, j: (r, jnp.maximum(j - N_LOW_TILES, 0)))],
        out_shape=[jax.ShapeDtypeStruct((t, N_LOW_TILES * IN_TILE), BF16),
                   jax.ShapeDtypeStruct((t, N_F32_TILES * IN_TILE), F32)],
        scratch_shapes=[pltpu.VMEM((ROW_GROUP, D_MODEL), BF16),
                        pltpu.VMEM((ROW_GROUP + 2 * CONV_PAD, IN_TILE), F32)],
        compiler_params=_params("arbitrary", "arbitrary"),
        name="in_proj",
    )(x2d, mod48, mod48, norm_g, w_in_r, conv_w, conv_b)


def _mla_kernel(*refs, latent, has_ctx, emit_cache, seq_len, tq):
    refs = list(refs)
    pm_ref, px_ref, qg_ref, kvg_ref, wuq_ref, wkt_ref, wv_ref = refs[:7]
    del refs[:7]
    if latent:
        cos_ref, sin_ref = refs[:2]
        del refs[:2]
    if has_ctx:
        cckv_ref, ckr_ref = refs[:2]
        del refs[:2]
    o_ref = refs.pop(0)
    if emit_cache:
        ckv_ref, kr_ref = refs[:2]
        del refs[:2]
    k_scr, v_scr = refs[:2]
    del refs[:2]
    if has_ctx:
        kc_scr, vc_scr = refs
    t = pl.program_id(1)
    n_nope = N_HEADS * QK_NOPE

    def put_keys(k_dst, v_dst, rows, ckv_n, kr_bf):
        ckv_bf = ckv_n.astype(BF16)
        v_dst[rows, :] = _dot(ckv_bf, wv_ref[...]).astype(BF16)
        kn_t = _dot_nt(wkt_ref[...], ckv_bf).astype(BF16)
        width = kr_bf.shape[1]
        eye = jnp.where(lax.broadcasted_iota(jnp.int32, (128, width), 0)
                        == lax.broadcasted_iota(jnp.int32, (128, width), 1), 1.0, 0.0).astype(BF16)
        kr_t = _dot_nt(eye, kr_bf).astype(BF16)
        for h in range(N_HEADS):
            base = h * ATTN_HEAD_COLS
            k_dst[base:base + QK_NOPE, rows] = kn_t[h * QK_NOPE:(h + 1) * QK_NOPE, :]
            k_dst[base + QK_NOPE:base + ATTN_HEAD_COLS, rows] = kr_t

    @pl.when(t == 0)
    def _():
        step = min(seq_len, TOKEN_TILE)
        for r0 in range(0, seq_len, step):
            rows = slice(r0, r0 + step)
            ckv_n = _rmsnorm(pm_ref[0, rows, Q_LORA:], kvg_ref[...])
            kr = px_ref[0, rows, 0:128]
            if latent:
                kr = kr * cos_ref[rows, :] + px_ref[0, rows, 128:256] * sin_ref[rows, :]
            if emit_cache:
                ckv_ref[0, rows, :] = ckv_n
                kr_ref[0, rows, :] = kr[:, :QK_ROPE]
            put_keys(k_scr, v_scr, rows, ckv_n, kr.astype(BF16))
        if has_ctx:
            past = cckv_ref.shape[1]
            put_keys(kc_scr, vc_scr, slice(0, past), cckv_ref[0], ckr_ref[0].astype(BF16))

    qrows = pl.ds(pl.multiple_of(t * tq, tq), tq)
    scale = LOG2_E / math.sqrt(QK_NOPE + QK_ROPE)
    cqn = _rmsnorm(pm_ref[0, qrows, 0:Q_LORA], qg_ref[...]).astype(BF16)
    q = _dot(cqn, wuq_ref[...])
    q_rope = q[:, n_nope:2 * n_nope]
    if latent:
        q_rope = (q_rope * jnp.concatenate([cos_ref[qrows, :]] * N_HEADS, axis=1)
                  + q[:, 2 * n_nope:3 * n_nope] * jnp.concatenate([sin_ref[qrows, :]] * N_HEADS, axis=1))
    segs = ([(kc_scr, vc_scr)] if has_ctx else []) + [(k_scr, v_scr)]
    for h in range(N_HEADS):
        head = slice(h * QK_NOPE, (h + 1) * QK_NOPE)
        qk = slice(h * ATTN_HEAD_COLS, (h + 1) * ATTN_HEAD_COLS)
        qh = (jnp.concatenate([q[:, head], q_rope[:, head]], axis=1) * scale).astype(BF16)
        s = [_dot(qh, k[qk, :]) for k, _ in segs]
        m = functools.reduce(jnp.maximum, [jnp.max(si, axis=-1, keepdims=True) for si in s])
        p = [jnp.exp2(si - m) for si in s]
        l = functools.reduce(jnp.add, [jnp.sum(pi, axis=-1, keepdims=True) for pi in p])
        o = functools.reduce(jnp.add, [_dot(pi.astype(BF16), v[:, head]) for pi, (_, v) in zip(p, segs)])
        o_ref[0, :, head] = (o / l).astype(BF16)


def _mla(proj_hi, q_norm_g, kv_norm_g, w_uq_r, w_uk_t, w_uv, rope_tables, ctx, emit_cache):
    b, l, _ = proj_hi.shape
    tq = min(l, ATTN_Q_TILE)
    latent = rope_tables is not None
    has_ctx = ctx is not None
    once = dict(pipeline_mode=pl.Buffered(1))
    per_batch = once if l // tq > 1 else {}
    const2 = lambda i, t: (0, 0)
    in_specs = [pl.BlockSpec((1, l, IN_TILE), lambda i, t: (i, 0, MLA_TILE), **per_batch),
                pl.BlockSpec((1, l, 256), lambda i, t: (i, 0, MISC_TILE * IN_TILE // 256), **per_batch),
                pl.BlockSpec((1, Q_LORA), const2),
                pl.BlockSpec((1, KV_LORA), const2),
                pl.BlockSpec(w_uq_r.shape, const2, **once),
                pl.BlockSpec(w_uk_t.shape, const2, **once),
                pl.BlockSpec(w_uv.shape, const2, **once)]
    args = [proj_hi, proj_hi, q_norm_g, kv_norm_g, w_uq_r, w_uk_t, w_uv]
    scratch = [pltpu.VMEM((N_HEADS * ATTN_HEAD_COLS, l), BF16), pltpu.VMEM((l, N_HEADS * V_HEAD), BF16)]
    if latent:
        in_specs += [pl.BlockSpec((l, 128), const2, **once)] * 2
        args += list(rope_tables)
    if has_ctx:
        past = ctx[0].shape[1]
        in_specs += [pl.BlockSpec((1, past, KV_LORA), lambda i, t: (i, 0, 0), **per_batch),
                     pl.BlockSpec((1, past, QK_ROPE), lambda i, t: (i, 0, 0), **per_batch)]
        args += list(ctx)
        scratch += [pltpu.VMEM((N_HEADS * ATTN_HEAD_COLS, past), BF16), pltpu.VMEM((past, N_HEADS * V_HEAD), BF16)]
    out_specs = [pl.BlockSpec((1, tq, N_HEADS * V_HEAD), lambda i, t: (i, t, 0))]
    out_shape = [jax.ShapeDtypeStruct((b, l, N_HEADS * V_HEAD), BF16)]
    if emit_cache:
        out_specs += [pl.BlockSpec((1, l, KV_LORA), lambda i, t: (i, 0, 0)),
                      pl.BlockSpec((1, l, QK_ROPE), lambda i, t: (i, 0, 0))]
        out_shape += [jax.ShapeDtypeStruct((b, l, KV_LORA), F32), jax.ShapeDtypeStruct((b, l, QK_ROPE), F32)]
    return pl.pallas_call(
        functools.partial(_mla_kernel, latent=latent, has_ctx=has_ctx, emit_cache=emit_cache, seq_len=l, tq=tq),
        grid=(b, l // tq),
        in_specs=in_specs,
        out_specs=out_specs,
        out_shape=out_shape,
        scratch_shapes=scratch,
        compiler_params=_params("arbitrary", "arbitrary", vmem_limit=MLA_VMEM_LIMIT),
        name="mla_attention",
    )(*args)


def _split3(x):
    hi = x.astype(BF16)
    r = x - hi.astype(F32)
    mid = r.astype(BF16)
    lo = (r - mid.astype(F32)).astype(BF16)
    return hi, mid, lo


def _exact_dot(parts, sel):
    return functools.reduce(jnp.add, [_dot(p, sel) for p in parts])


def _exact_dot_rows(sel, parts):
    return functools.reduce(jnp.add, [_dot(sel, p) for p in parts])


HEADS_PER_GROUP = SSD_HEADS // SSD_GROUPS
GROUP_COLS = HEADS_PER_GROUP * SSD_HEADDIM


def _ssd_kernel(*refs, nc, cps, has_h0):
    if has_h0:
        (xs_ref, zs_ref, bc_ref, dt_ref, dtb_ref, a_ref, dx_ref, h0_ref, y_ref, ht_ref,
         yl_scr, acum_scr, src_scr, tot_scr, sb_scr, h_scr, esel_scr) = refs
    else:
        (xs_ref, zs_ref, bc_ref, dt_ref, dtb_ref, a_ref, dx_ref, y_ref, ht_ref,
         yl_scr, acum_scr, src_scr, tot_scr, sb_scr, h_scr, esel_scr) = refs
    s = pl.program_id(1)
    q = SSD_CHUNK
    n_bc = SSD_GROUPS * SSD_STATE
    lane = lax.broadcasted_iota(jnp.int32, (q, 128), 1)
    low_half = lane < SSD_HEADDIM
    ii = lax.broadcasted_iota(jnp.int32, (q, q), 0)
    jj = lax.broadcasted_iota(jnp.int32, (q, q), 1)
    lower, upper = ii >= jj, ii <= jj

    def lane_bcast(parts, d):
        return _exact_dot(parts, esel_scr[d])

    def stacked_states(d):
        return [jnp.concatenate([h_scr[d, 2 * i], h_scr[d, 2 * i + 1]], axis=0).astype(BF16)
                for i in range(SSD_GROUPS // 2)]

    def group_c(bc, g):
        i, r = divmod(g, 2)
        cpair = bc[:, n_bc + i * 128:n_bc + (i + 1) * 128]
        return jnp.where(low_half if r == 0 else ~low_half, cpair, 0.0).astype(BF16)

    @pl.when(s == 0)
    def _():
        k = lax.broadcasted_iota(jnp.int32, (128, SSD_INNER), 0)
        head = lax.broadcasted_iota(jnp.int32, (128, SSD_INNER), 1) // SSD_HEADDIM
        for d in range(2):
            esel_scr[d] = jnp.where(k == d * SSD_HEADS + head, 1.0, 0.0).astype(BF16)
        if has_h0:
            for d in range(2):
                for g in range(SSD_GROUPS):
                    hpn = h0_ref[0, d, g * HEADS_PER_GROUP:(g + 1) * HEADS_PER_GROUP].reshape(GROUP_COLS, SSD_STATE)
                    h_scr[d, g] = hpn.T
        else:
            h_scr[...] = jnp.zeros(h_scr.shape, F32)
        tri_f = jnp.where(lower, 1.0, 0.0).astype(BF16)
        tri_b = jnp.where(upper, 1.0, 0.0).astype(BF16)
        fwd_col = lane < SSD_HEADS
        for c in range(nc):
            crow = slice(c * q, (c + 1) * q)
            dt = _softplus(dt_ref[0, crow, :] + dtb_ref[...])
            parts = _split3(dt * a_ref[...])
            acum = jnp.where(fwd_col, _exact_dot_rows(tri_f, parts),
                             _exact_dot_rows(tri_b, parts))
            acum = acum * LOG2_E
            acum_scr[crow, :] = acum
            tot = jnp.where(fwd_col[0:1], acum[q - 1:q, :], acum[0:1, :])
            tot_scr[c] = jnp.broadcast_to(tot, (8, 128))
            src_scr[c] = (acum - jnp.log2(dt)).T

    def first_sweep(c, blk):
        rows = pl.ds(pl.multiple_of(c * q, q), q)
        x = xs_ref[0, blk, :]
        bc = bc_ref[0, blk, :].astype(F32)
        acum = acum_scr[rows, :]
        src_t = src_scr[c]
        tot8 = tot_scr[c]
        e = jnp.exp2(acum)
        e_hi = e.astype(BF16)
        eb_f = lane_bcast([e_hi, (e - e_hi.astype(F32)).astype(BF16)], 0)
        cd_f = jnp.exp2(lane_bcast(_split3(tot8), 0))[0:1]
        b_t = [bc[:, i * 128:(i + 1) * 128].T for i in range(SSD_GROUPS // 2)]
        h_in = stacked_states(0)
        for g in range(SSD_GROUPS):
            i, r = divmod(g, 2)
            cm = group_c(bc, g)
            scores = _dot_nt(cm, bc[:, i * 128:(i + 1) * 128].astype(BF16))
            bg_t = b_t[i][r * SSD_STATE:(r + 1) * SSD_STATE, :]
            gcols = slice(g * GROUP_COLS, (g + 1) * GROUP_COLS)
            y_off = _dot(cm, h_in[i]) * eb_f[:, gcols]
            for t in range(HEADS_PER_GROUP // 2):
                pc = slice((2 * g + t) * 128, (2 * g + t + 1) * 128)
                xp = x[:, pc]
                x2 = jnp.concatenate([jnp.where(low_half, xp, 0.0), jnp.where(low_half, 0.0, xp)], axis=0).astype(BF16)
                m, sf, sb = [], [], []
                for u in range(2):
                    kf = g * HEADS_PER_GROUP + 2 * t + u
                    kb = SSD_HEADS + kf
                    af_col = jnp.broadcast_to(acum[:, kf:kf + 1], (q, q))
                    ab_col = jnp.broadcast_to(acum[:, kb:kb + 1], (q, q))
                    sf_row, sb_row = src_t[kf:kf + 1, :], src_t[kb:kb + 1, :]
                    decay = (jnp.exp2(jnp.where(lower, af_col - sf_row, NEG_BIG))
                             + jnp.exp2(jnp.where(upper, ab_col - sb_row, NEG_BIG)))
                    m.append((scores * decay).astype(BF16))
                    wf = jnp.exp2(tot8[0:1, kf:kf + 1] - sf_row)
                    wb = jnp.exp2(tot8[0:1, kb:kb + 1] - sb_row)
                    sf.append((bg_t * wf).astype(BF16))
                    sb.append((bg_t * wb).astype(BF16))
                y_pair = _dot(jnp.concatenate(m, axis=1), x2)
                tc = slice(t * 128, (t + 1) * 128)
                yl_scr[rows, pc] = y_pair + y_off[:, tc] + dx_ref[:, pc] * xp
                h_scr[0, g, :, tc] = h_scr[0, g, :, tc] * cd_f[:, pc] + _dot(jnp.concatenate(sf, axis=1), x2)
                sb_scr[c, g, :, tc] = _dot(jnp.concatenate(sb, axis=1), x2)

    def second_sweep(c, blk):
        rows = pl.ds(pl.multiple_of(c * q, q), q)
        bc = bc_ref[0, blk, :].astype(F32)
        e = jnp.exp2(acum_scr[rows, :])
        e_hi = e.astype(BF16)
        eb_b = lane_bcast([e_hi, (e - e_hi.astype(F32)).astype(BF16)], 1)
        cd_b = jnp.exp2(lane_bcast(_split3(tot_scr[c]), 1))[0:1]
        h_in = stacked_states(1)
        for g in range(SSD_GROUPS):
            gcols = slice(g * GROUP_COLS, (g + 1) * GROUP_COLS)
            y_off = _dot(group_c(bc, g), h_in[g // 2]) * eb_b[:, gcols]
            y_ref[0, blk, gcols] = ((yl_scr[rows, gcols] + y_off) * zs_ref[0, blk, gcols].astype(F32)).astype(y_ref.dtype)
            h_scr[1, g] = h_scr[1, g] * cd_b[:, gcols] + sb_scr[c, g]

    n_steps = nc // cps
    one_step = n_steps == 1

    @pl.when(s < n_steps)
    def _():
        for ci in range(cps):
            first_sweep(s * cps + ci, slice(ci * q, (ci + 1) * q))

    @pl.when(jnp.logical_or(one_step, s >= n_steps))
    def _():
        blk_id = s * 0 if one_step else 2 * n_steps - 1 - s
        for ci in reversed(range(cps)):
            second_sweep(blk_id * cps + ci, slice(ci * q, (ci + 1) * q))

    @pl.when(jnp.logical_or(one_step, s == 2 * n_steps - 1))
    def _():
        for d in range(2):
            for g in range(SSD_GROUPS):
                ht_ref[0, d, g * HEADS_PER_GROUP:(g + 1) * HEADS_PER_GROUP] = h_scr[d, g].T.reshape(
                    HEADS_PER_GROUP, SSD_HEADDIM, SSD_STATE)


def _ssd(proj_lo, proj_hi, h0, dt_bias128, a128, d_exp):
    b, l, _ = proj_hi.shape
    q = SSD_CHUNK
    nc = l // q
    cps = min(nc, SSD_CHUNKS_PER_STEP)
    n_steps = nc // cps
    n_grid = 1 if n_steps == 1 else 2 * n_steps
    rows = cps * q
    early = lambda s: jnp.minimum(s, n_steps - 1)
    both = lambda s: jnp.where(s < n_steps, s, 2 * n_steps - 1 - s)
    late = lambda s: jnp.where(s < n_steps, n_steps - 1, 2 * n_steps - 1 - s)
    st_shape = (1, 2, SSD_HEADS, SSD_HEADDIM, SSD_STATE)
    st_spec = pl.BlockSpec(st_shape, lambda i, s: (i, 0, 0, 0, 0))
    has_h0 = h0 is not None
    return pl.pallas_call(
        functools.partial(_ssd_kernel, nc=nc, cps=cps, has_h0=has_h0),
        grid=(b, n_grid),
        in_specs=[pl.BlockSpec((1, rows, SSD_INNER), lambda i, s: (i, early(s), XS_BLK)),
                  pl.BlockSpec((1, rows, SSD_INNER), lambda i, s: (i, late(s), Z_BLK)),
                  pl.BlockSpec((1, rows, IN_TILE), lambda i, s: (i, both(s), BC_TILE)),
                  pl.BlockSpec((1, l, 128), lambda i, s: (i, 0, DT_BLK)),
                  pl.BlockSpec((1, 128), lambda i, s: (0, 0)),
                  pl.BlockSpec((1, 128), lambda i, s: (0, 0)),
                  pl.BlockSpec((1, SSD_INNER), lambda i, s: (0, 0))] + ([st_spec] if has_h0 else []),
        out_specs=[pl.BlockSpec((1, rows, SSD_INNER), lambda i, s: (i, late(s), 0)),
                   pl.BlockSpec(st_shape, lambda i, s: (i, 0, 0, 0, 0))],
        out_shape=[jax.ShapeDtypeStruct((b, l, SSD_INNER), BF16),
                   jax.ShapeDtypeStruct((b,) + st_shape[1:], F32)],
        scratch_shapes=[pltpu.VMEM((l, SSD_INNER), F32),
                        pltpu.VMEM((l, 128), F32),
                        pltpu.VMEM((nc, 128, q), F32),
                        pltpu.VMEM((nc, 8, 128), F32),
                        pltpu.VMEM((nc, SSD_GROUPS, SSD_STATE, GROUP_COLS), F32),
                        pltpu.VMEM((2, SSD_GROUPS, SSD_STATE, GROUP_COLS), F32),
                        pltpu.VMEM((2, 128, SSD_INNER), BF16)],
        compiler_params=_params("arbitrary", "arbitrary"),
        name="ssd_scan",
    )(proj_hi, proj_lo, proj_lo, proj_hi, dt_bias128, a128, d_exp, *([h0] if has_h0 else []))


def _merge_kernel(attn_ref, yz_ref, gm_ref, gs_ref, x_ref, g1_ref, ng_ref, womla_ref, wossd_ref, wout_ref, o_ref, w_scr):
    @pl.when(pl.program_id(0) == 0)
    def _():
        w_scr[0] = womla_ref[...].astype(BF16)
        w_scr[1] = wossd_ref[...].astype(BF16)
        w_scr[2] = wout_ref[...].astype(BF16)

    o_mla = _dot(attn_ref[...], w_scr[0])
    o_ssd = _dot(_rmsnorm(yz_ref[...].astype(F32), ng_ref[...]).astype(BF16), w_scr[1])
    merged = gm_ref[...].astype(F32) * o_mla + gs_ref[...].astype(F32) * o_ssd
    o_ref[...] = x_ref[...] + g1_ref[...] * _dot(merged.astype(BF16), w_scr[2])


def _merge(attn2d, yz2d, proj, x2d, mod48, mod_row, ssd_norm_g, w_o_mla, w_o_ssd, w_out):
    t = x2d.shape[0]
    tm = TOKEN_TILE
    row = lambda i: (i, 0)
    const = lambda i: (0, 0)
    wspec = pl.BlockSpec((D_MODEL, D_MODEL), const, pipeline_mode=pl.Buffered(1))
    return pl.pallas_call(
        _merge_kernel,
        grid=(t // tm,),
        in_specs=[pl.BlockSpec((tm, D_MODEL), row),
                  pl.BlockSpec((tm, D_MODEL), row),
                  pl.BlockSpec((tm, D_MODEL), lambda i: (i, GM_BLK)),
                  pl.BlockSpec((tm, D_MODEL), lambda i: (i, GS_BLK)),
                  pl.BlockSpec((tm, D_MODEL), row),
                  pl.BlockSpec((None, 1, D_MODEL), lambda i: (mod_row(i * tm // ROW_GROUP) * 6 + 2, 0, 0)),
                  pl.BlockSpec((1, D_MODEL), const),
                  wspec, wspec, wspec],
        out_specs=pl.BlockSpec((tm, D_MODEL), row),
        out_shape=jax.ShapeDtypeStruct((t, D_MODEL), F32),
        scratch_shapes=[pltpu.VMEM((3, D_MODEL, D_MODEL), BF16)],
        compiler_params=_params("arbitrary"),
        name="merge_out",
    )(attn2d, yz2d, proj, proj, x2d, mod48, ssd_norm_g, w_o_mla, w_o_ssd, w_out)


def _ffn_kernel(x_ref, sh_ref, sc_ref, g2_ref, ng_ref, wg_ref, wv_ref, cwg_ref, cwv_ref, cbg_ref, cbv_ref, wd_ref,
                fg_ref, o_ref, h_scr, wup_scr, wd_scr, u_scr, *, seq_len):
    j = pl.program_id(1)
    chunk = FFN_CHUNK
    n_chunks = ROW_GROUP // chunk
    rows = lambda c: slice(c * chunk, (c + 1) * chunk)

    wup_scr[:, 0:FFN_TILE] = wg_ref[...].astype(BF16)
    wup_scr[:, FFN_TILE:2 * FFN_TILE] = wv_ref[...].astype(BF16)
    wd_scr[...] = wd_ref[...].astype(BF16)

    def gated(c):
        ug = _dwconv3_rows(u_scr, c, chunk, slice(0, FFN_TILE), cwg_ref, cbg_ref, seq_len)
        uv = _dwconv3_rows(u_scr, c, chunk, slice(FFN_TILE, 2 * FFN_TILE), cwv_ref, cbv_ref, seq_len)
        return (_silu(ug) * uv).astype(BF16)

    def pipeline(first, last):
        act = {}
        lag = FFN_DOWN_LAG
        for c in range(n_chunks + lag):
            if c < n_chunks:
                if first:
                    h_scr[rows(c), :] = _norm_mod(x_ref[rows(c), :], ng_ref, sc_ref, sh_ref)
                u_scr[_stage_rows(c, chunk), :] = _dot(h_scr[rows(c), :], wup_scr[...])
            if c >= lag:
                r = rows(c - lag)
                acc = _dot(act.pop(c - lag), wd_scr[...])
                if not first:
                    acc = o_ref[r, :] + acc
                if last:
                    acc = _rmsnorm(x_ref[r, :] + g2_ref[...] * acc, fg_ref[...])
                o_ref[r, :] = acc
            if 1 <= c <= n_chunks:
                act[c - 1] = gated(c - 1)

    last_j = pl.num_programs(1) - 1

    @pl.when(j == 0)
    def _():
        _zero_conv_pads(u_scr)
        pipeline(True, False)

    @pl.when((j > 0) & (j < last_j))
    def _():
        pipeline(False, False)

    @pl.when(j == last_j)
    def _():
        pipeline(False, True)


def _ffn(x2d, mod48, mod_row, norm_g, w_up, conv_w, conv_b, w_down, final_g, seq_len):
    t = x2d.shape[0]
    nj = D_FF // FFN_TILE
    gate = lambda r, j: (0, j)
    val = lambda r, j: (0, nj + j)
    const = lambda r, j: (0, 0)
    mod = lambda k: pl.BlockSpec((None, 1, D_MODEL), lambda r, j: (mod_row(r) * 6 + k, 0, 0))
    return pl.pallas_call(
        functools.partial(_ffn_kernel, seq_len=seq_len),
        grid=(t // ROW_GROUP, nj),
        in_specs=[pl.BlockSpec((ROW_GROUP, D_MODEL), lambda r, j: (r, 0)),
                  mod(3), mod(4), mod(5),
                  pl.BlockSpec((1, D_MODEL), const),
                  pl.BlockSpec((D_MODEL, FFN_TILE), gate),
                  pl.BlockSpec((D_MODEL, FFN_TILE), val),
                  pl.BlockSpec((3, FFN_TILE), gate),
                  pl.BlockSpec((3, FFN_TILE), val),
                  pl.BlockSpec((1, FFN_TILE), gate),
                  pl.BlockSpec((1, FFN_TILE), val),
                  pl.BlockSpec((FFN_TILE, D_MODEL), lambda r, j: (j, 0)),
                  pl.BlockSpec((1, D_MODEL), const)],
        out_specs=pl.BlockSpec((ROW_GROUP, D_MODEL), lambda r, j: (r, 0)),
        out_shape=jax.ShapeDtypeStruct((t, D_MODEL), F32),
        scratch_shapes=[pltpu.VMEM((ROW_GROUP, D_MODEL), BF16),
                        pltpu.VMEM((D_MODEL, 2 * FFN_TILE), BF16),
                        pltpu.VMEM((FFN_TILE, D_MODEL), BF16),
                        pltpu.VMEM((ROW_GROUP + 2 * CONV_PAD, 2 * FFN_TILE), F32)],
        compiler_params=_params("arbitrary", "arbitrary"),
        name="conv_ffn",
    )(x2d, mod48, mod48, mod48, norm_g, w_up, w_up, conv_w, conv_w, conv_b, conv_b, w_down, final_g)


def _rope_tables(seq_len):
    t = np.arange(seq_len)
    row = (t // GRID_W).astype(np.float32)
    col = (t % GRID_W).astype(np.float32)
    n = QK_ROPE // 4
    inv = (np.float32(ROPE_BASE) ** (-np.arange(n, dtype=np.float32) / np.float32(n))).astype(np.float32)
    ar, ac = row[:, None] * inv, col[:, None] * inv
    cos64 = np.concatenate([np.cos(ar), np.cos(ar), np.cos(ac), np.cos(ac)], axis=1)
    sin64 = np.concatenate([-np.sin(ar), np.sin(ar), -np.sin(ac), np.sin(ac)], axis=1)
    zeros = np.zeros_like(cos64)
    return (jnp.asarray(np.concatenate([cos64, zeros], axis=1), F32),
            jnp.asarray(np.concatenate([sin64, zeros], axis=1), F32))


def _swap_rope_halves(w):
    lead = w.shape[:-1]
    return w.reshape(lead + (2, 2, QK_ROPE // 4))[..., ::-1, :].reshape(lead + (QK_ROPE,))


def _trunk_pass(x, mod48, mod_row, wts, ctx, latent):
    b, l, _ = x.shape
    x2d = x.reshape(b * l, D_MODEL)
    proj_lo, proj_hi = _in_proj(x2d, mod48, mod_row, wts["norm_attn_g"], wts["w_in_r"], wts["ssd_conv_w"],
                                wts["ssd_conv_b"], l)
    rope = _rope_tables(l) if latent else None
    w_uq_r = wts["w_uq_lat"] if latent else wts["w_uq_ctx"]
    shape3 = lambda a, n: a.reshape(b, n, a.shape[-1])
    h0 = None
    mla_ctx = None
    if ctx is not None:
        cache_ckv, cache_krope, h0 = ctx
        mla_ctx = (cache_ckv, cache_krope)
    emit_cache = ctx is None
    mla_out = _mla(shape3(proj_hi, l), wts["q_norm_g"], wts["kv_norm_g"], w_uq_r, wts["w_uk_t"], wts["w_uv"],
                   rope, mla_ctx, emit_cache)
    attn, ckv_n, kr3 = mla_out if emit_cache else (mla_out[0], None, None)
    yz, h_t = _ssd(shape3(proj_lo, l), shape3(proj_hi, l), h0, wts["dt_bias128"], wts["a128"], wts["d_exp"])
    x1 = _merge(attn.reshape(b * l, -1), yz.reshape(b * l, -1), proj_lo, x2d, mod48, mod_row, wts["ssd_norm_g"],
                wts["w_o_mla"], wts["w_o_ssd"], wts["w_out"])
    y = _ffn(x1, mod48, mod_row, wts["norm_ffn_g"], wts["w_up"], wts["ffn_conv_w"], wts["ffn_conv_b"], wts["w_down"],
             wts["final_norm_g"], l)
    return y.reshape(b, l, D_MODEL), ckv_n, kr3, h_t


def kernel(x_prompt, x_sample, c, cache_ckv, cache_krope, state_ssd, c_ctx, w_ada, b_ada, norm_attn_g, w_in, q_norm_g,
           kv_norm_g, w_uq, w_ukv, w_o_mla, ssd_conv_w, ssd_conv_b, ssd_dt_bias, ssd_A_log, ssd_D, ssd_norm_g, w_o_ssd,
           w_out, norm_ffn_g, w_up, ffn_conv_w, ffn_conv_b, w_down, final_norm_g):
    depth = w_in.shape[0]
    assert depth == 1, "single trunk layer"
    dec_b = x_sample.shape[0]
    assert x_sample.shape[1] == ROW_GROUP and ROW_GROUP % x_prompt.shape[1] == 0
    lyr = 0

    cvec = jnp.zeros((8, D_MODEL), F32).at[0].set(c_ctx).at[1:1 + dec_b].set(c)
    mod48 = _ada(cvec, w_ada[lyr], b_ada[lyr]).reshape(8 * 6, 1, D_MODEL)

    w_in_r = _regroup_w_in(w_in[lyr].T)
    wq = w_uq[lyr].reshape(Q_LORA, N_HEADS, QK_NOPE + QK_ROPE)
    wq_nope = wq[:, :, :QK_NOPE].reshape(Q_LORA, -1)
    wq_rope = wq[:, :, QK_NOPE:]
    pad_rope = lambda w: jnp.pad(w, ((0, 0), (0, 0), (0, 128 - QK_ROPE))).reshape(Q_LORA, -1)
    w_uq_ctx = jnp.concatenate([wq_nope, pad_rope(wq_rope)], axis=1).astype(BF16)
    w_uq_lat = jnp.concatenate([wq_nope, pad_rope(wq_rope), pad_rope(_swap_rope_halves(wq_rope))], axis=1).astype(BF16)
    wkv = w_ukv[lyr].reshape(KV_LORA, N_HEADS, QK_NOPE + V_HEAD)
    w_uk_t = wkv[:, :, :QK_NOPE].reshape(KV_LORA, -1).T.astype(BF16)
    w_uv = wkv[:, :, QK_NOPE:].reshape(KV_LORA, -1).astype(BF16)
    pad128 = lambda a: jnp.pad(a.reshape(1, -1), ((0, 0), (0, 128 - a.size)))
    wts = {
        "norm_attn_g": norm_attn_g[lyr].reshape(1, -1), "w_in_r": w_in_r,
        "ssd_conv_w": ssd_conv_w[lyr], "ssd_conv_b": ssd_conv_b[lyr].reshape(1, -1),
        "q_norm_g": q_norm_g[lyr].reshape(1, -1), "kv_norm_g": kv_norm_g[lyr].reshape(1, -1),
        "w_uq_ctx": w_uq_ctx, "w_uq_lat": w_uq_lat, "w_uk_t": w_uk_t, "w_uv": w_uv,
        "dt_bias128": pad128(ssd_dt_bias[lyr]), "a128": pad128(-jnp.exp(ssd_A_log[lyr])),
        "d_exp": jnp.repeat(ssd_D[lyr], SSD_HEADDIM).reshape(1, -1),
        "ssd_norm_g": ssd_norm_g[lyr].reshape(1, -1),
        "w_o_mla": w_o_mla[lyr], "w_o_ssd": w_o_ssd[lyr], "w_out": w_out[lyr],
        "norm_ffn_g": norm_ffn_g[lyr].reshape(1, -1), "w_up": w_up[lyr],
        "ffn_conv_w": ffn_conv_w[lyr], "ffn_conv_b": ffn_conv_b[lyr].reshape(1, -1),
        "w_down": w_down[lyr], "final_norm_g": final_norm_g.reshape(1, -1),
    }

    y_p, ckv_p, kr_p, st_p = _trunk_pass(x_prompt, mod48, lambda r: 0, wts, None, False)
    ctx = (cache_ckv[:, lyr], cache_krope[:, lyr], state_ssd[:, lyr])
    y_s, _, _, _ = _trunk_pass(x_sample, mod48, lambda r: 1 + r, wts, ctx, True)
    return y_p, y_s, ckv_p[:, None], kr_p[:, None], st_p[:, None]
```

```python
import functools
import math

import jax
import jax.numpy as jnp
import numpy as np
from jax import lax
from jax.experimental import pallas as pl
from jax.experimental.pallas import tpu as pltpu

F32 = jnp.float32
BF16 = jnp.bfloat16

D_MODEL = 1024
GRID_W = 64
N_HEADS = 8
QK_NOPE = 128
QK_ROPE = 64
V_HEAD = 128
Q_LORA = 256
KV_LORA = 256
ROPE_BASE = 10000.0
SSD_HEADS = 16
SSD_HEADDIM = 64
SSD_INNER = SSD_HEADS * SSD_HEADDIM
SSD_GROUPS = 4
SSD_STATE = 64
SSD_CHUNK = 128
D_FF = 2816
EPS = 1e-6

ROW_GROUP = 2048
IN_TILE = 512
SSD_CHUNKS_PER_STEP = 4
FFN_TILE = 256
TOKEN_TILE = 512
ATTN_Q_TILE = 512
IN_CHUNK = 256
FFN_CHUNK = 512
FFN_DOWN_LAG = 2
CONV_PAD = 8
VMEM_LIMIT = 56 * 1024 * 1024
MLA_VMEM_LIMIT = 62 * 1024 * 1024
NEG_BIG = -1e30
LOG2_E = 1.4426950408889634


def _sigmoid(x):
    return 1.0 / (1.0 + jnp.exp(-x))


def _silu(x):
    return x * _sigmoid(x)


def _softplus(x):
    e = jnp.exp(-jnp.abs(x))
    u = 1.0 + e
    log1p_e = jnp.where(u == 1.0, e, e * jnp.log(u) / jnp.where(u == 1.0, 1.0, u - 1.0))
    return jnp.maximum(x, 0.0) + log1p_e


def _rmsnorm(x, g):
    return x * lax.rsqrt(jnp.mean(x * x, axis=-1, keepdims=True) + EPS) * g


def _dot(a, b):
    return jnp.dot(a, b, preferred_element_type=F32)


def _dot_nt(a, b):
    return lax.dot_general(a, b, (((1,), (1,)), ((), ())), preferred_element_type=F32)


def _params(*sem, vmem_limit=VMEM_LIMIT):
    return pltpu.CompilerParams(dimension_semantics=sem, vmem_limit_bytes=vmem_limit)


def _norm_mod(x, g_ref, sc_ref, sh_ref):
    return (_rmsnorm(x, g_ref[...]) * (1.0 + sc_ref[...]) + sh_ref[...]).astype(BF16)


def _zero_conv_pads(u_scr):
    zeros = jnp.zeros((CONV_PAD, u_scr.shape[1]), F32)
    u_scr[0:CONV_PAD, :] = zeros
    u_scr[CONV_PAD + ROW_GROUP:2 * CONV_PAD + ROW_GROUP, :] = zeros


def _stage_rows(c, chunk):
    return slice(CONV_PAD + c * chunk, CONV_PAD + (c + 1) * chunk)


def _dwconv3_rows(u_scr, c, chunk, cols, w_ref, b_ref, seq_len):
    r0 = c * chunk
    base = CONV_PAD + r0
    width = cols.stop - cols.start
    prev = u_scr[base - 1:base - 1 + chunk, cols]
    cur = u_scr[base:base + chunk, cols]
    nxt = u_scr[base + 1:base + 1 + chunk, cols]
    pos = (lax.broadcasted_iota(jnp.int32, (chunk, width), 0) + r0) & (seq_len - 1)
    if r0 % seq_len == 0 or chunk > seq_len:
        prev = jnp.where(pos == 0, 0.0, prev)
    if (r0 + chunk) % seq_len == 0 or chunk > seq_len:
        nxt = jnp.where(pos == seq_len - 1, 0.0, nxt)
    return prev * w_ref[0:1, :] + cur * w_ref[1:2, :] + nxt * w_ref[2:3, :] + b_ref[...]


def _ada_kernel(c_ref, w_ref, b_ref, o_ref):
    a = _silu(c_ref[...]).astype(BF16)
    o_ref[...] = _dot(a, w_ref[...].astype(BF16)) + b_ref[...]


def _ada(cvec, w_ada, b_ada):
    tn = 1536
    return pl.pallas_call(
        _ada_kernel,
        grid=(6 * D_MODEL // tn,),
        in_specs=[pl.BlockSpec((8, D_MODEL), lambda j: (0, 0)),
                  pl.BlockSpec((D_MODEL, tn), lambda j: (0, j)),
                  pl.BlockSpec((1, tn), lambda j: (0, j))],
        out_specs=pl.BlockSpec((8, tn), lambda j: (0, j)),
        out_shape=jax.ShapeDtypeStruct((8, 6 * D_MODEL), F32),
        compiler_params=_params("arbitrary"),
        name="ada_mod",
    )(cvec, w_ada, b_ada.reshape(1, -1))


IN_SPLITS = (Q_LORA, KV_LORA, QK_ROPE, SSD_INNER, SSD_INNER, SSD_GROUPS * SSD_STATE, SSD_GROUPS * SSD_STATE,
             2 * SSD_HEADS, D_MODEL, D_MODEL)
IN_OFFSETS = tuple(int(v) for v in np.cumsum((0,) + IN_SPLITS))


def _regroup_kernel(w_ref, o_ref):
    dst = 0

    def put_block(block):
        nonlocal dst
        o_ref[:, dst:dst + 128] = block.T.astype(BF16)
        dst += 128

    def piece(i):
        for off in range(0, IN_SPLITS[i], 128):
            put_block(w_ref[IN_OFFSETS[i] + off:IN_OFFSETS[i] + off + 128, :])

    def padded(parts):
        n = sum(p.shape[0] for p in parts)
        put_block(jnp.concatenate(parts + [jnp.zeros((128 - n, w_ref.shape[1]), F32)], axis=0))

    for i in (3, 8, 9, 5, 6, 4, 0, 1):
        piece(i)
    kr0, n = IN_OFFSETS[2], QK_ROPE // 4
    padded([w_ref[kr0:kr0 + QK_ROPE, :]])
    padded([w_ref[kr0 + blk * n:kr0 + (blk + 1) * n, :] for blk in (1, 0, 3, 2)])
    padded([w_ref[IN_OFFSETS[7]:IN_OFFSETS[7] + IN_SPLITS[7], :]])
    o_ref[:, dst:] = jnp.zeros((o_ref.shape[0], o_ref.shape[1] - dst), BF16)


def _regroup_w_in(w_in_t):
    cols = 256
    n_out = (N_LOW_TILES + N_F32_TILES) * IN_TILE
    return pl.pallas_call(
        _regroup_kernel,
        grid=(D_MODEL // cols,),
        in_specs=[pl.BlockSpec((w_in_t.shape[0], cols), lambda i: (0, i))],
        out_specs=pl.BlockSpec((cols, n_out), lambda i: (i, 0)),
        out_shape=jax.ShapeDtypeStruct((D_MODEL, n_out), BF16),
        compiler_params=_params("arbitrary"),
        name="w_in_regroup",
    )(w_in_t)


N_LOW_TILES, N_F32_TILES = 7, 4
Z_BLK, GM_BLK, GS_BLK = 0, 1, 2
BC_TILE = 6
XS_BLK = 0
MLA_TILE, MISC_TILE = 2, 3
DT_BLK = (MISC_TILE * IN_TILE + 256) // 128
ATTN_HEAD_COLS = 256


def _in_kernel(x_ref, sh_ref, sc_ref, g_ref, w_ref, cw_ref, cb_ref, lo_ref, hi_ref, h_scr, u_scr, *, seq_len):
    j = pl.program_id(1)
    chunk = IN_CHUNK
    n_chunks = ROW_GROUP // chunk
    rows = lambda c: slice(c * chunk, (c + 1) * chunk)

    def pointwise(fn, o_ref, stage_h=False):
        for c in range(n_chunks):
            if stage_h:
                h_scr[rows(c), :] = _norm_mod(x_ref[rows(c), :], g_ref, sc_ref, sh_ref)
            o_ref[rows(c), :] = fn(_dot(h_scr[rows(c), :], w_ref[...])).astype(o_ref.dtype)

    def conv(o_ref):
        def conv_out(c):
            v = _silu(_dwconv3_rows(u_scr, c, chunk, slice(0, IN_TILE), cw_ref, cb_ref, seq_len))
            o_ref[rows(c), :] = v.astype(o_ref.dtype)
        for c in range(n_chunks):
            u_scr[_stage_rows(c, chunk), :] = _dot(h_scr[rows(c), :], w_ref[...])
            if c >= 1:
                conv_out(c - 1)
        conv_out(n_chunks - 1)

    @pl.when(j == 0)
    def _():
        _zero_conv_pads(u_scr)
        pointwise(_silu, lo_ref, stage_h=True)

    @pl.when(j == 1)
    def _():
        pointwise(_silu, lo_ref)

    @pl.when((j >= 2) & (j <= 5))
    def _():
        pointwise(_sigmoid, lo_ref)

    @pl.when(j == BC_TILE)
    def _():
        conv(lo_ref)

    @pl.when((j == N_LOW_TILES) | (j == N_LOW_TILES + 1))
    def _():
        conv(hi_ref)

    @pl.when(j >= N_LOW_TILES + MLA_TILE)
    def _():
        pointwise(lambda u: u, hi_ref)


def _in_proj(x2d, mod48, mod_row, norm_g, w_in_r, conv_w, conv_b, seq_len):
    t = x2d.shape[0]
    n_tiles = N_LOW_TILES + N_F32_TILES
    conv_idx = lambda r, j: (0, jnp.where(j == BC_TILE, 2, jnp.clip(j - N_LOW_TILES, 0, 1)))
    return pl.pallas_call(
        functools.partial(_in_kernel, seq_len=seq_len),
        grid=(t // ROW_GROUP, n_tiles),
        in_specs=[pl.BlockSpec((ROW_GROUP, D_MODEL), lambda r, j: (r, 0)),
                  pl.BlockSpec((None, 1, D_MODEL), lambda r, j: (mod_row(r) * 6 + 0, 0, 0)),
                  pl.BlockSpec((None, 1, D_MODEL), lambda r, j: (mod_row(r) * 6 + 1, 0, 0)),
                  pl.BlockSpec((1, D_MODEL), lambda r, j: (0, 0)),
                  pl.BlockSpec((D_MODEL, IN_TILE), lambda r, j: (0, j)),
                  pl.BlockSpec((3, IN_TILE), conv_idx),
                  pl.BlockSpec((1, IN_TILE), conv_idx)],
        out_specs=[pl.BlockSpec((ROW_GROUP, IN_TILE), lambda r, j: (r, jnp.minimum(j, N_LOW_TILES - 1))),
                   pl.BlockSpec((ROW_GROUP, IN_TILE), lambda r, j: (r, jnp.maximum(j - N_LOW_TILES, 0)))],
        out_shape=[jax.ShapeDtypeStruct((t, N_LOW_TILES * IN_TILE), BF16),
                   jax.ShapeDtypeStruct((t, N_F32_TILES * IN_TILE), F32)],
        scratch_shapes=[pltpu.VMEM((ROW_GROUP, D_MODEL), BF16),
                        pltpu.VMEM((ROW_GROUP + 2 * CONV_PAD, IN_TILE), F32)],
        compiler_params=_params("arbitrary", "arbitrary"),
        name="in_proj",
    )(x2d, mod48, mod48, norm_g, w_in_r, conv_w, conv_b)


def _mla_kernel(*refs, latent, has_ctx, emit_cache, seq_len, tq):
    refs = list(refs)
    pm_ref, px_ref, qg_ref, kvg_ref, wuq_ref, wkt_ref, wv_ref = refs[:7]
    del refs[:7]
    if latent:
        cos_ref, sin_ref = refs[:2]
        del refs[:2]
    if has_ctx:
        cckv_ref, ckr_ref = refs[:2]
        del refs[:2]
    o_ref = refs.pop(0)
    if emit_cache:
        ckv_ref, kr_ref = refs[:2]
        del refs[:2]
    k_scr, v_scr = refs[:2]
    del refs[:2]
    if has_ctx:
        kc_scr, vc_scr = refs
    t = pl.program_id(1)
    n_nope = N_HEADS * QK_NOPE

    def put_keys(k_dst, v_dst, rows, ckv_n, kr_bf):
        ckv_bf = ckv_n.astype(BF16)
        v_dst[rows, :] = _dot(ckv_bf, wv_ref[...]).astype(BF16)
        kn_t = _dot_nt(wkt_ref[...], ckv_bf).astype(BF16)
        width = kr_bf.shape[1]
        eye = jnp.where(lax.broadcasted_iota(jnp.int32, (128, width), 0)
                        == lax.broadcasted_iota(jnp.int32, (128, width), 1), 1.0, 0.0).astype(BF16)
        kr_t = _dot_nt(eye, kr_bf).astype(BF16)
        for h in range(N_HEADS):
            base = h * ATTN_HEAD_COLS
            k_dst[base:base + QK_NOPE, rows] = kn_t[h * QK_NOPE:(h + 1) * QK_NOPE, :]
            k_dst[base + QK_NOPE:base + ATTN_HEAD_COLS, rows] = kr_t

    @pl.when(t == 0)
    def _():
        step = min(seq_len, TOKEN_TILE)
        for r0 in range(0, seq_len, step):
            rows = slice(r0, r0 + step)
            ckv_n = _rmsnorm(pm_ref[0, rows, Q_LORA:], kvg_ref[...])
            kr = px_ref[0, rows, 0:128]
            if latent:
                kr = kr * cos_ref[rows, :] + px_ref[0, rows, 128:256] * sin_ref[rows, :]
            if emit_cache:
                ckv_ref[0, rows, :] = ckv_n
                kr_ref[0, rows, :] = kr[:, :QK_ROPE]
            put_keys(k_scr, v_scr, rows, ckv_n, kr.astype(BF16))
        if has_ctx:
            past = cckv_ref.shape[1]
            put_keys(kc_scr, vc_scr, slice(0, past), cckv_ref[0], ckr_ref[0].astype(BF16))

    qrows = pl.ds(pl.multiple_of(t * tq, tq), tq)
    scale = LOG2_E / math.sqrt(QK_NOPE + QK_ROPE)
    cqn = _rmsnorm(pm_ref[0, qrows, 0:Q_LORA], qg_ref[...]).astype(BF16)
    q = _dot(cqn, wuq_ref[...])
    q_rope = q[:, n_nope:2 * n_nope]
    if latent:
        q_rope = (q_rope * jnp.concatenate([cos_ref[qrows, :]] * N_HEADS, axis=1)
                  + q[:, 2 * n_nope:3 * n_nope] * jnp.concatenate([sin_ref[qrows, :]] * N_HEADS, axis=1))
    segs = ([(kc_scr, vc_scr)] if has_ctx else []) + [(k_scr, v_scr)]
    for h in range(N_HEADS):
        head = slice(h * QK_NOPE, (h + 1) * QK_NOPE)
        qk = slice(h * ATTN_HEAD_COLS, (h + 1) * ATTN_HEAD_COLS)
        qh = (jnp.concatenate([q[:, head], q_rope[:, head]], axis=1) * scale).astype(BF16)
        s = [_dot(qh, k[qk, :]) for k, _ in segs]
        m = functools.reduce(jnp.maximum, [jnp.max(si, axis=-1, keepdims=True) for si in s])
        p = [jnp.exp2(si - m) for si in s]
        l = functools.reduce(jnp.add, [jnp.sum(pi, axis=-1, keepdims=True) for pi in p])
        o = functools.reduce(jnp.add, [_dot(pi.astype(BF16), v[:, head]) for pi, (_, v) in zip(p, segs)])
        o_ref[0, :, head] = (o / l).astype(BF16)


def _mla(proj_hi, q_norm_g, kv_norm_g, w_uq_r, w_uk_t, w_uv, rope_tables, ctx, emit_cache):
    b, l, _ = proj_hi.shape
    tq = min(l, ATTN_Q_TILE)
    latent = rope_tables is not None
    has_ctx = ctx is not None
    once = dict(pipeline_mode=pl.Buffered(1))
    per_batch = once if l // tq > 1 else {}
    const2 = lambda i, t: (0, 0)
    in_specs = [pl.BlockSpec((1, l, IN_TILE), lambda i, t: (i, 0, MLA_TILE), **per_batch),
                pl.BlockSpec((1, l, 256), lambda i, t: (i, 0, MISC_TILE * IN_TILE // 256), **per_batch),
                pl.BlockSpec((1, Q_LORA), const2),
                pl.BlockSpec((1, KV_LORA), const2),
                pl.BlockSpec(w_uq_r.shape, const2, **once),
                pl.BlockSpec(w_uk_t.shape, const2, **once),
                pl.BlockSpec(w_uv.shape, const2, **once)]
    args = [proj_hi, proj_hi, q_norm_g, kv_norm_g, w_uq_r, w_uk_t, w_uv]
    scratch = [pltpu.VMEM((N_HEADS * ATTN_HEAD_COLS, l), BF16), pltpu.VMEM((l, N_HEADS * V_HEAD), BF16)]
    if latent:
        in_specs += [pl.BlockSpec((l, 128), const2, **once)] * 2
        args += list(rope_tables)
    if has_ctx:
        past = ctx[0].shape[1]
        in_specs += [pl.BlockSpec((1, past, KV_LORA), lambda i, t: (i, 0, 0), **per_batch),
                     pl.BlockSpec((1, past, QK_ROPE), lambda i, t: (i, 0, 0), **per_batch)]
        args += list(ctx)
        scratch += [pltpu.VMEM((N_HEADS * ATTN_HEAD_COLS, past), BF16), pltpu.VMEM((past, N_HEADS * V_HEAD), BF16)]
    out_specs = [pl.BlockSpec((1, tq, N_HEADS * V_HEAD), lambda i, t: (i, t, 0))]
    out_shape = [jax.ShapeDtypeStruct((b, l, N_HEADS * V_HEAD), BF16)]
    if emit_cache:
        out_specs += [pl.BlockSpec((1, l, KV_LORA), lambda i, t: (i, 0, 0)),
                      pl.BlockSpec((1, l, QK_ROPE), lambda i, t: (i, 0, 0))]
        out_shape += [jax.ShapeDtypeStruct((b, l, KV_LORA), F32), jax.ShapeDtypeStruct((b, l, QK_ROPE), F32)]
    return pl.pallas_call(
        functools.partial(_mla_kernel, latent=latent, has_ctx=has_ctx, emit_cache=emit_cache, seq_len=l, tq=tq),
        grid=(b, l // tq),
        in_specs=in_specs,
        out_specs=out_specs,
        out_shape=out_shape,
        scratch_shapes=scratch,
        compiler_params=_params("arbitrary", "arbitrary", vmem_limit=MLA_VMEM_LIMIT),
        name="mla_attention",
    )(*args)


def _split3(x):
    hi = x.astype(BF16)
    r = x - hi.astype(F32)
    mid = r.astype(BF16)
    lo = (r - mid.astype(F32)).astype(BF16)
    return hi, mid, lo


def _exact_dot(parts, sel):
    return functools.reduce(jnp.add, [_dot(p, sel) for p in parts])


def _exact_dot_rows(sel, parts):
    return functools.reduce(jnp.add, [_dot(sel, p) for p in parts])


HEADS_PER_GROUP = SSD_HEADS // SSD_GROUPS
GROUP_COLS = HEADS_PER_GROUP * SSD_HEADDIM


def _ssd_kernel(*refs, nc, cps, has_h0):
    refs = list(refs)
    (xs_ref, zs_ref, bc_ref, dt_ref, dtb_ref, a_ref, dx_ref,
     attn_ref, gm_ref, gs_ref, x_ref, g1_ref, ng_ref, wom_ref, wos_ref, wout_ref) = refs[:16]
    del refs[:16]
    if has_h0:
        h0_ref = refs.pop(0)
    (x1_ref, ht_ref, yl_scr, acum_scr, src_scr, tot_scr, sb_scr, h_scr, esel_scr, yz_scr) = refs
    s = pl.program_id(1)
    q = SSD_CHUNK
    n_bc = SSD_GROUPS * SSD_STATE
    lane = lax.broadcasted_iota(jnp.int32, (q, 128), 1)
    low_half = lane < SSD_HEADDIM
    ii = lax.broadcasted_iota(jnp.int32, (q, q), 0)
    jj = lax.broadcasted_iota(jnp.int32, (q, q), 1)
    lower, upper = ii >= jj, ii <= jj

    def lane_bcast(parts, d):
        return _exact_dot(parts, esel_scr[d])

    def stacked_states(d):
        return [jnp.concatenate([h_scr[d, 2 * i], h_scr[d, 2 * i + 1]], axis=0).astype(BF16)
                for i in range(SSD_GROUPS // 2)]

    def group_c(bc, g):
        i, r = divmod(g, 2)
        cpair = bc[:, n_bc + i * 128:n_bc + (i + 1) * 128]
        return jnp.where(low_half if r == 0 else ~low_half, cpair, 0.0).astype(BF16)

    @pl.when(s == 0)
    def _():
        k = lax.broadcasted_iota(jnp.int32, (128, SSD_INNER), 0)
        head = lax.broadcasted_iota(jnp.int32, (128, SSD_INNER), 1) // SSD_HEADDIM
        for d in range(2):
            esel_scr[d] = jnp.where(k == d * SSD_HEADS + head, 1.0, 0.0).astype(BF16)
        if has_h0:
            for d in range(2):
                for g in range(SSD_GROUPS):
                    hpn = h0_ref[0, d, g * HEADS_PER_GROUP:(g + 1) * HEADS_PER_GROUP].reshape(GROUP_COLS, SSD_STATE)
                    h_scr[d, g] = hpn.T
        else:
            h_scr[...] = jnp.zeros(h_scr.shape, F32)
        tri_f = jnp.where(lower, 1.0, 0.0).astype(BF16)
        tri_b = jnp.where(upper, 1.0, 0.0).astype(BF16)
        fwd_col = lane < SSD_HEADS
        for c in range(nc):
            crow = slice(c * q, (c + 1) * q)
            dt = _softplus(dt_ref[0, crow, :] + dtb_ref[...])
            parts = _split3(dt * a_ref[...])
            acum = jnp.where(fwd_col, _exact_dot_rows(tri_f, parts),
                             _exact_dot_rows(tri_b, parts))
            acum = acum * LOG2_E
            acum_scr[crow, :] = acum
            tot = jnp.where(fwd_col[0:1], acum[q - 1:q, :], acum[0:1, :])
            tot_scr[c] = jnp.broadcast_to(tot, (8, 128))
            src_scr[c] = (acum - jnp.log2(dt)).T

    def first_sweep(c, blk):
        rows = pl.ds(pl.multiple_of(c * q, q), q)
        x = xs_ref[0, blk, :]
        bc = bc_ref[0, blk, :].astype(F32)
        acum = acum_scr[rows, :]
        src_t = src_scr[c]
        tot8 = tot_scr[c]
        e = jnp.exp2(acum)
        e_hi = e.astype(BF16)
        eb_f = lane_bcast([e_hi, (e - e_hi.astype(F32)).astype(BF16)], 0)
        cd_f = jnp.exp2(lane_bcast(_split3(tot8), 0))[0:1]
        b_t = [bc[:, i * 128:(i + 1) * 128].T for i in range(SSD_GROUPS // 2)]
        h_in = stacked_states(0)
        for g in range(SSD_GROUPS):
            i, r = divmod(g, 2)
            cm = group_c(bc, g)
            scores = _dot_nt(cm, bc[:, i * 128:(i + 1) * 128].astype(BF16))
            bg_t = b_t[i][r * SSD_STATE:(r + 1) * SSD_STATE, :]
            gcols = slice(g * GROUP_COLS, (g + 1) * GROUP_COLS)
            y_off = _dot(cm, h_in[i]) * eb_f[:, gcols]
            for t in range(HEADS_PER_GROUP // 2):
                pc = slice((2 * g + t) * 128, (2 * g + t + 1) * 128)
                xp = x[:, pc]
                x2 = jnp.concatenate([jnp.where(low_half, xp, 0.0), jnp.where(low_half, 0.0, xp)], axis=0).astype(BF16)
                m, sf, sb = [], [], []
                for u in range(2):
                    kf = g * HEADS_PER_GROUP + 2 * t + u
                    kb = SSD_HEADS + kf
                    af_col = jnp.broadcast_to(acum[:, kf:kf + 1], (q, q))
                    ab_col = jnp.broadcast_to(acum[:, kb:kb + 1], (q, q))
                    sf_row, sb_row = src_t[kf:kf + 1, :], src_t[kb:kb + 1, :]
                    decay = (jnp.exp2(jnp.where(lower, af_col - sf_row, NEG_BIG))
                             + jnp.exp2(jnp.where(upper, ab_col - sb_row, NEG_BIG)))
                    m.append((scores * decay).astype(BF16))
                    wf = jnp.exp2(tot8[0:1, kf:kf + 1] - sf_row)
                    wb = jnp.exp2(tot8[0:1, kb:kb + 1] - sb_row)
                    sf.append((bg_t * wf).astype(BF16))
                    sb.append((bg_t * wb).astype(BF16))
                y_pair = _dot(jnp.concatenate(m, axis=1), x2)
                tc = slice(t * 128, (t + 1) * 128)
                yl_scr[rows, pc] = y_pair + y_off[:, tc] + dx_ref[:, pc] * xp
                h_scr[0, g, :, tc] = h_scr[0, g, :, tc] * cd_f[:, pc] + _dot(jnp.concatenate(sf, axis=1), x2)
                sb_scr[c, g, :, tc] = _dot(jnp.concatenate(sb, axis=1), x2)

    def second_sweep(c, blk):
        rows = pl.ds(pl.multiple_of(c * q, q), q)
        bc = bc_ref[0, blk, :].astype(F32)
        e = jnp.exp2(acum_scr[rows, :])
        e_hi = e.astype(BF16)
        eb_b = lane_bcast([e_hi, (e - e_hi.astype(F32)).astype(BF16)], 1)
        cd_b = jnp.exp2(lane_bcast(_split3(tot_scr[c]), 1))[0:1]
        h_in = stacked_states(1)
        for g in range(SSD_GROUPS):
            gcols = slice(g * GROUP_COLS, (g + 1) * GROUP_COLS)
            y_off = _dot(group_c(bc, g), h_in[g // 2]) * eb_b[:, gcols]
            yz_scr[blk, gcols] = (yl_scr[rows, gcols] + y_off) * zs_ref[0, blk, gcols].astype(F32)
            h_scr[1, g] = h_scr[1, g] * cd_b[:, gcols] + sb_scr[c, g]

    def merge_branches():
        o_ssd = _dot(_rmsnorm(yz_scr[...], ng_ref[...]).astype(BF16), wos_ref[...])
        o_mla = _dot(attn_ref[0], wom_ref[...])
        merged = gm_ref[0].astype(F32) * o_mla + gs_ref[0].astype(F32) * o_ssd
        x1_ref[0] = x_ref[0] + g1_ref[...] * _dot(merged.astype(BF16), wout_ref[...])

    n_steps = nc // cps
    one_step = n_steps == 1

    @pl.when(s < n_steps)
    def _():
        for ci in range(cps):
            first_sweep(s * cps + ci, slice(ci * q, (ci + 1) * q))

    @pl.when(jnp.logical_or(one_step, s >= n_steps))
    def _():
        blk_id = s * 0 if one_step else 2 * n_steps - 1 - s
        for ci in reversed(range(cps)):
            second_sweep(blk_id * cps + ci, slice(ci * q, (ci + 1) * q))
        merge_branches()

    @pl.when(jnp.logical_or(one_step, s == 2 * n_steps - 1))
    def _():
        for d in range(2):
            for g in range(SSD_GROUPS):
                ht_ref[0, d, g * HEADS_PER_GROUP:(g + 1) * HEADS_PER_GROUP] = h_scr[d, g].T.reshape(
                    HEADS_PER_GROUP, SSD_HEADDIM, SSD_STATE)


def _ssd_merge(proj_lo, proj_hi, h0, dt_bias128, a128, d_exp, attn, x, mod48, mod_row, ssd_norm_g, w_o_mla, w_o_ssd,
               w_out):
    b, l, _ = proj_hi.shape
    q = SSD_CHUNK
    nc = l // q
    cps = min(nc, SSD_CHUNKS_PER_STEP)
    n_steps = nc // cps
    n_grid = 1 if n_steps == 1 else 2 * n_steps
    rows = cps * q
    early = lambda s: jnp.minimum(s, n_steps - 1)
    both = lambda s: jnp.where(s < n_steps, s, 2 * n_steps - 1 - s)
    late = lambda s: jnp.where(s < n_steps, n_steps - 1, 2 * n_steps - 1 - s)
    st_shape = (1, 2, SSD_HEADS, SSD_HEADDIM, SSD_STATE)
    st_spec = pl.BlockSpec(st_shape, lambda i, s: (i, 0, 0, 0, 0))
    has_h0 = h0 is not None
    const2 = lambda i, s: (0, 0)
    late_rows = lambda blk: pl.BlockSpec((1, rows, D_MODEL), lambda i, s: (i, late(s), blk))
    wspec = pl.BlockSpec((D_MODEL, D_MODEL), const2, pipeline_mode=pl.Buffered(1))
    return pl.pallas_call(
        functools.partial(_ssd_kernel, nc=nc, cps=cps, has_h0=has_h0),
        grid=(b, n_grid),
        in_specs=[pl.BlockSpec((1, rows, SSD_INNER), lambda i, s: (i, early(s), XS_BLK)),
                  pl.BlockSpec((1, rows, SSD_INNER), lambda i, s: (i, late(s), Z_BLK)),
                  pl.BlockSpec((1, rows, IN_TILE), lambda i, s: (i, both(s), BC_TILE)),
                  pl.BlockSpec((1, l, 128), lambda i, s: (i, 0, DT_BLK)),
                  pl.BlockSpec((1, 128), const2),
                  pl.BlockSpec((1, 128), const2),
                  pl.BlockSpec((1, SSD_INNER), const2),
                  late_rows(0), late_rows(GM_BLK), late_rows(GS_BLK), late_rows(0),
                  pl.BlockSpec((None, 1, D_MODEL), lambda i, s: (mod_row(i * l // ROW_GROUP) * 6 + 2, 0, 0)),
                  pl.BlockSpec((1, SSD_INNER), const2),
                  wspec, wspec, wspec] + ([st_spec] if has_h0 else []),
        out_specs=[late_rows(0),
                   pl.BlockSpec(st_shape, lambda i, s: (i, 0, 0, 0, 0))],
        out_shape=[jax.ShapeDtypeStruct((b, l, D_MODEL), F32),
                   jax.ShapeDtypeStruct((b,) + st_shape[1:], F32)],
        scratch_shapes=[pltpu.VMEM((l, SSD_INNER), F32),
                        pltpu.VMEM((l, 128), F32),
                        pltpu.VMEM((nc, 128, q), F32),
                        pltpu.VMEM((nc, 8, 128), F32),
                        pltpu.VMEM((nc, SSD_GROUPS, SSD_STATE, GROUP_COLS), F32),
                        pltpu.VMEM((2, SSD_GROUPS, SSD_STATE, GROUP_COLS), F32),
                        pltpu.VMEM((2, 128, SSD_INNER), BF16),
                        pltpu.VMEM((rows, SSD_INNER), F32)],
        compiler_params=_params("arbitrary", "arbitrary"),
        name="ssd_scan_merge",
    )(proj_hi, proj_lo, proj_lo, proj_hi, dt_bias128, a128, d_exp, attn, proj_lo, proj_lo, x, mod48, ssd_norm_g,
      w_o_mla, w_o_ssd, w_out, *([h0] if has_h0 else []))


def _ffn_kernel(x_ref, sh_ref, sc_ref, g2_ref, ng_ref, wg_ref, wv_ref, cwg_ref, cwv_ref, cbg_ref, cbv_ref, wd_ref,
                fg_ref, o_ref, h_scr, wup_scr, wd_scr, u_scr, *, seq_len):
    j = pl.program_id(1)
    chunk = FFN_CHUNK
    n_chunks = ROW_GROUP // chunk
    rows = lambda c: slice(c * chunk, (c + 1) * chunk)

    wup_scr[:, 0:FFN_TILE] = wg_ref[...].astype(BF16)
    wup_scr[:, FFN_TILE:2 * FFN_TILE] = wv_ref[...].astype(BF16)
    wd_scr[...] = wd_ref[...].astype(BF16)

    def gated(c):
        ug = _dwconv3_rows(u_scr, c, chunk, slice(0, FFN_TILE), cwg_ref, cbg_ref, seq_len)
        uv = _dwconv3_rows(u_scr, c, chunk, slice(FFN_TILE, 2 * FFN_TILE), cwv_ref, cbv_ref, seq_len)
        return (_silu(ug) * uv).astype(BF16)

    def pipeline(first, last):
        act = {}
        lag = FFN_DOWN_LAG
        for c in range(n_chunks + lag):
            if c < n_chunks:
                if first:
                    h_scr[rows(c), :] = _norm_mod(x_ref[rows(c), :], ng_ref, sc_ref, sh_ref)
                u_scr[_stage_rows(c, chunk), :] = _dot(h_scr[rows(c), :], wup_scr[...])
            if c >= lag:
                r = rows(c - lag)
                acc = _dot(act.pop(c - lag), wd_scr[...])
                if not first:
                    acc = o_ref[r, :] + acc
                if last:
                    acc = _rmsnorm(x_ref[r, :] + g2_ref[...] * acc, fg_ref[...])
                o_ref[r, :] = acc
            if 1 <= c <= n_chunks:
                act[c - 1] = gated(c - 1)

    last_j = pl.num_programs(1) - 1

    @pl.when(j == 0)
    def _():
        _zero_conv_pads(u_scr)
        pipeline(True, False)

    @pl.when((j > 0) & (j < last_j))
    def _():
        pipeline(False, False)

    @pl.when(j == last_j)
    def _():
        pipeline(False, True)


def _ffn(x2d, mod48, mod_row, norm_g, w_up, conv_w, conv_b, w_down, final_g, seq_len):
    t = x2d.shape[0]
    nj = D_FF // FFN_TILE
    gate = lambda r, j: (0, j)
    val = lambda r, j: (0, nj + j)
    const = lambda r, j: (0, 0)
    mod = lambda k: pl.BlockSpec((None, 1, D_MODEL), lambda r, j: (mod_row(r) * 6 + k, 0, 0))
    return pl.pallas_call(
        functools.partial(_ffn_kernel, seq_len=seq_len),
        grid=(t // ROW_GROUP, nj),
        in_specs=[pl.BlockSpec((ROW_GROUP, D_MODEL), lambda r, j: (r, 0)),
                  mod(3), mod(4), mod(5),
                  pl.BlockSpec((1, D_MODEL), const),
                  pl.BlockSpec((D_MODEL, FFN_TILE), gate),
                  pl.BlockSpec((D_MODEL, FFN_TILE), val),
                  pl.BlockSpec((3, FFN_TILE), gate),
                  pl.BlockSpec((3, FFN_TILE), val),
                  pl.BlockSpec((1, FFN_TILE), gate),
                  pl.BlockSpec((1, FFN_TILE), val),
                  pl.BlockSpec((FFN_TILE, D_MODEL), lambda r, j: (j, 0)),
                  pl.BlockSpec((1, D_MODEL), const)],
        out_specs=pl.BlockSpec((ROW_GROUP, D_MODEL), lambda r, j: (r, 0)),
        out_shape=jax.ShapeDtypeStruct((t, D_MODEL), F32),
        scratch_shapes=[pltpu.VMEM((ROW_GROUP, D_MODEL), BF16),
                        pltpu.VMEM((D_MODEL, 2 * FFN_TILE), BF16),
                        pltpu.VMEM((FFN_TILE, D_MODEL), BF16),
                        pltpu.VMEM((ROW_GROUP + 2 * CONV_PAD, 2 * FFN_TILE), F32)],
        compiler_params=_params("arbitrary", "arbitrary"),
        name="conv_ffn",
    )(x2d, mod48, mod48, mod48, norm_g, w_up, w_up, conv_w, conv_w, conv_b, conv_b, w_down, final_g)


def _rope_tables(seq_len):
    t = np.arange(seq_len)
    row = (t // GRID_W).astype(np.float32)
    col = (t % GRID_W).astype(np.float32)
    n = QK_ROPE // 4
    inv = (np.float32(ROPE_BASE) ** (-np.arange(n, dtype=np.float32) / np.float32(n))).astype(np.float32)
    ar, ac = row[:, None] * inv, col[:, None] * inv
    cos64 = np.concatenate([np.cos(ar), np.cos(ar), np.cos(ac), np.cos(ac)], axis=1)
    sin64 = np.concatenate([-np.sin(ar), np.sin(ar), -np.sin(ac), np.sin(ac)], axis=1)
    zeros = np.zeros_like(cos64)
    return (jnp.asarray(np.concatenate([cos64, zeros], axis=1), F32),
            jnp.asarray(np.concatenate([sin64, zeros], axis=1), F32))


def _swap_rope_halves(w):
    lead = w.shape[:-1]
    return w.reshape(lead + (2, 2, QK_ROPE // 4))[..., ::-1, :].reshape(lead + (QK_ROPE,))


def _trunk_pass(x, mod48, mod_row, wts, ctx, latent):
    b, l, _ = x.shape
    x2d = x.reshape(b * l, D_MODEL)
    proj_lo, proj_hi = _in_proj(x2d, mod48, mod_row, wts["norm_attn_g"], wts["w_in_r"], wts["ssd_conv_w"],
                                wts["ssd_conv_b"], l)
    rope = _rope_tables(l) if latent else None
    w_uq_r = wts["w_uq_lat"] if latent else wts["w_uq_ctx"]
    shape3 = lambda a, n: a.reshape(b, n, a.shape[-1])
    h0 = None
    mla_ctx = None
    if ctx is not None:
        cache_ckv, cache_krope, h0 = ctx
        mla_ctx = (cache_ckv, cache_krope)
    emit_cache = ctx is None
    mla_out = _mla(shape3(proj_hi, l), wts["q_norm_g"], wts["kv_norm_g"], w_uq_r, wts["w_uk_t"], wts["w_uv"],
                   rope, mla_ctx, emit_cache)
    attn, ckv_n, kr3 = mla_out if emit_cache else (mla_out[0], None, None)
    x1, h_t = _ssd_merge(shape3(proj_lo, l), shape3(proj_hi, l), h0, wts["dt_bias128"], wts["a128"], wts["d_exp"],
                         attn, x, mod48, mod_row, wts["ssd_norm_g"], wts["w_o_mla"], wts["w_o_ssd"], wts["w_out"])
    y = _ffn(x1.reshape(b * l, D_MODEL), mod48, mod_row, wts["norm_ffn_g"], wts["w_up"], wts["ffn_conv_w"], wts["ffn_conv_b"], wts["w_down"],
             wts["final_norm_g"], l)
    return y.reshape(b, l, D_MODEL), ckv_n, kr3, h_t


def kernel(x_prompt, x_sample, c, cache_ckv, cache_krope, state_ssd, c_ctx, w_ada, b_ada, norm_attn_g, w_in, q_norm_g,
           kv_norm_g, w_uq, w_ukv, w_o_mla, ssd_conv_w, ssd_conv_b, ssd_dt_bias, ssd_A_log, ssd_D, ssd_norm_g, w_o_ssd,
           w_out, norm_ffn_g, w_up, ffn_conv_w, ffn_conv_b, w_down, final_norm_g):
    depth = w_in.shape[0]
    assert depth == 1, "single trunk layer"
    dec_b = x_sample.shape[0]
    assert x_sample.shape[1] == ROW_GROUP and ROW_GROUP % x_prompt.shape[1] == 0
    lyr = 0

    cvec = jnp.zeros((8, D_MODEL), F32).at[0].set(c_ctx).at[1:1 + dec_b].set(c)
    mod48 = _ada(cvec, w_ada[lyr], b_ada[lyr]).reshape(8 * 6, 1, D_MODEL)

    w_in_r = _regroup_w_in(w_in[lyr].T)
    wq = w_uq[lyr].reshape(Q_LORA, N_HEADS, QK_NOPE + QK_ROPE)
    wq_nope = wq[:, :, :QK_NOPE].reshape(Q_LORA, -1)
    wq_rope = wq[:, :, QK_NOPE:]
    pad_rope = lambda w: jnp.pad(w, ((0, 0), (0, 0), (0, 128 - QK_ROPE))).reshape(Q_LORA, -1)
    w_uq_ctx = jnp.concatenate([wq_nope, pad_rope(wq_rope)], axis=1).astype(BF16)
    w_uq_lat = jnp.concatenate([wq_nope, pad_rope(wq_rope), pad_rope(_swap_rope_halves(wq_rope))], axis=1).astype(BF16)
    wkv = w_ukv[lyr].reshape(KV_LORA, N_HEADS, QK_NOPE + V_HEAD)
    w_uk_t = wkv[:, :, :QK_NOPE].reshape(KV_LORA, -1).T.astype(BF16)
    w_uv = wkv[:, :, QK_NOPE:].reshape(KV_LORA, -1).astype(BF16)
    pad128 = lambda a: jnp.pad(a.reshape(1, -1), ((0, 0), (0, 128 - a.size)))
    wts = {
        "norm_attn_g": norm_attn_g[lyr].reshape(1, -1), "w_in_r": w_in_r,
        "ssd_conv_w": ssd_conv_w[lyr], "ssd_conv_b": ssd_conv_b[lyr].reshape(1, -1),
        "q_norm_g": q_norm_g[lyr].reshape(1, -1), "kv_norm_g": kv_norm_g[lyr].reshape(1, -1),
        "w_uq_ctx": w_uq_ctx, "w_uq_lat": w_uq_lat, "w_uk_t": w_uk_t, "w_uv": w_uv,
        "dt_bias128": pad128(ssd_dt_bias[lyr]), "a128": pad128(-jnp.exp(ssd_A_log[lyr])),
        "d_exp": jnp.repeat(ssd_D[lyr], SSD_HEADDIM).reshape(1, -1),
        "ssd_norm_g": ssd_norm_g[lyr].reshape(1, -1),
        "w_o_mla": w_o_mla[lyr].astype(BF16), "w_o_ssd": w_o_ssd[lyr].astype(BF16), "w_out": w_out[lyr].astype(BF16),
        "norm_ffn_g": norm_ffn_g[lyr].reshape(1, -1), "w_up": w_up[lyr],
        "ffn_conv_w": ffn_conv_w[lyr], "ffn_conv_b": ffn_conv_b[lyr].reshape(1, -1),
        "w_down": w_down[lyr], "final_norm_g": final_norm_g.reshape(1, -1),
    }

    y_p, ckv_p, kr_p, st_p = _trunk_pass(x_prompt, mod48, lambda r: 0, wts, None, False)
    ctx = (cache_ckv[:, lyr], cache_krope[:, lyr], state_ssd[:, lyr])
    y_s, _, _, _ = _trunk_pass(x_sample, mod48, lambda r: 1 + r, wts, ctx, True)
    return y_p, y_s, ckv_p[:, None], kr_p[:, None], st_p[:, None]
```

```python
import functools
import math

import jax
import jax.numpy as jnp
import numpy as np
from jax import lax
from jax.experimental import pallas as pl
from jax.experimental.pallas import tpu as pltpu

F32 = jnp.float32
BF16 = jnp.bfloat16

D_MODEL = 1024
GRID_W = 64
N_HEADS = 8
QK_NOPE = 128
QK_ROPE = 64
V_HEAD = 128
Q_LORA = 256
KV_LORA = 256
ROPE_BASE = 10000.0
SSD_HEADS = 16
SSD_HEADDIM = 64
SSD_INNER = SSD_HEADS * SSD_HEADDIM
SSD_GROUPS = 4
SSD_STATE = 64
SSD_CHUNK = 128
D_FF = 2816
EPS = 1e-6

ROW_GROUP = 2048
IN_TILE = 512
SSD_CHUNKS_PER_STEP = 4
SSD_SEQS_PER_STEP = 2
MLA_SEQS_PER_STEP = 2
FFN_TILE = 256
TOKEN_TILE = 512
ATTN_Q_TILE = 512
IN_CHUNK = 256
FFN_CHUNK = 512
FFN_DOWN_LAG = 2
CONV_PAD = 8
VMEM_LIMIT = 56 * 1024 * 1024
MLA_VMEM_LIMIT = 62 * 1024 * 1024
NEG_BIG = -1e30
LOG2_E = 1.4426950408889634


def _sigmoid(x):
    return 1.0 / (1.0 + jnp.exp(-x))


def _silu(x):
    return x * _sigmoid(x)


def _softplus(x):
    e = jnp.exp(-jnp.abs(x))
    u = 1.0 + e
    log1p_e = jnp.where(u == 1.0, e, e * jnp.log(u) / jnp.where(u == 1.0, 1.0, u - 1.0))
    return jnp.maximum(x, 0.0) + log1p_e


def _rmsnorm(x, g):
    return x * lax.rsqrt(jnp.mean(x * x, axis=-1, keepdims=True) + EPS) * g


def _dot(a, b):
    return jnp.dot(a, b, preferred_element_type=F32)


def _dot_nt(a, b):
    return lax.dot_general(a, b, (((1,), (1,)), ((), ())), preferred_element_type=F32)


def _params(*sem, vmem_limit=VMEM_LIMIT):
    return pltpu.CompilerParams(dimension_semantics=sem, vmem_limit_bytes=vmem_limit)


def _norm_mod(x, g_ref, sc_ref, sh_ref):
    return (_rmsnorm(x, g_ref[...]) * (1.0 + sc_ref[...]) + sh_ref[...]).astype(BF16)


def _zero_conv_pads(u_scr):
    zeros = jnp.zeros((CONV_PAD, u_scr.shape[1]), F32)
    u_scr[0:CONV_PAD, :] = zeros
    u_scr[CONV_PAD + ROW_GROUP:2 * CONV_PAD + ROW_GROUP, :] = zeros


def _stage_rows(c, chunk):
    return slice(CONV_PAD + c * chunk, CONV_PAD + (c + 1) * chunk)


def _dwconv3_rows(u_scr, c, chunk, cols, w_ref, b_ref, seq_len):
    r0 = c * chunk
    base = CONV_PAD + r0
    width = cols.stop - cols.start
    prev = u_scr[base - 1:base - 1 + chunk, cols]
    cur = u_scr[base:base + chunk, cols]
    nxt = u_scr[base + 1:base + 1 + chunk, cols]
    pos = (lax.broadcasted_iota(jnp.int32, (chunk, width), 0) + r0) & (seq_len - 1)
    if r0 % seq_len == 0 or chunk > seq_len:
        prev = jnp.where(pos == 0, 0.0, prev)
    if (r0 + chunk) % seq_len == 0 or chunk > seq_len:
        nxt = jnp.where(pos == seq_len - 1, 0.0, nxt)
    return prev * w_ref[0:1, :] + cur * w_ref[1:2, :] + nxt * w_ref[2:3, :] + b_ref[...]


def _ada_kernel(c_ref, w_ref, b_ref, o_ref):
    a = _silu(c_ref[...]).astype(BF16)
    o_ref[...] = _dot(a, w_ref[...].astype(BF16)) + b_ref[...]


def _ada(cvec, w_ada, b_ada):
    tn = 1536
    return pl.pallas_call(
        _ada_kernel,
        grid=(6 * D_MODEL // tn,),
        in_specs=[pl.BlockSpec((8, D_MODEL), lambda j: (0, 0)),
                  pl.BlockSpec((D_MODEL, tn), lambda j: (0, j)),
                  pl.BlockSpec((1, tn), lambda j: (0, j))],
        out_specs=pl.BlockSpec((8, tn), lambda j: (0, j)),
        out_shape=jax.ShapeDtypeStruct((8, 6 * D_MODEL), F32),
        compiler_params=_params("arbitrary"),
        name="ada_mod",
    )(cvec, w_ada, b_ada.reshape(1, -1))


IN_SPLITS = (Q_LORA, KV_LORA, QK_ROPE, SSD_INNER, SSD_INNER, SSD_GROUPS * SSD_STATE, SSD_GROUPS * SSD_STATE,
             2 * SSD_HEADS, D_MODEL, D_MODEL)
IN_OFFSETS = tuple(int(v) for v in np.cumsum((0,) + IN_SPLITS))


def _regroup_kernel(w_ref, o_ref):
    dst = 0

    def put_block(block):
        nonlocal dst
        o_ref[:, dst:dst + 128] = block.T.astype(BF16)
        dst += 128

    def piece(i):
        for off in range(0, IN_SPLITS[i], 128):
            put_block(w_ref[IN_OFFSETS[i] + off:IN_OFFSETS[i] + off + 128, :])

    def padded(parts):
        n = sum(p.shape[0] for p in parts)
        put_block(jnp.concatenate(parts + [jnp.zeros((128 - n, w_ref.shape[1]), F32)], axis=0))

    for i in (3, 8, 9, 5, 6, 4, 0, 1):
        piece(i)
    kr0, n = IN_OFFSETS[2], QK_ROPE // 4
    padded([w_ref[kr0:kr0 + QK_ROPE, :]])
    padded([w_ref[kr0 + blk * n:kr0 + (blk + 1) * n, :] for blk in (1, 0, 3, 2)])
    padded([w_ref[IN_OFFSETS[7]:IN_OFFSETS[7] + IN_SPLITS[7], :]])
    o_ref[:, dst:] = jnp.zeros((o_ref.shape[0], o_ref.shape[1] - dst), BF16)


def _regroup_w_in(w_in_t):
    cols = 256
    n_out = (N_LOW_TILES + N_F32_TILES) * IN_TILE
    return pl.pallas_call(
        _regroup_kernel,
        grid=(D_MODEL // cols,),
        in_specs=[pl.BlockSpec((w_in_t.shape[0], cols), lambda i: (0, i))],
        out_specs=pl.BlockSpec((cols, n_out), lambda i: (i, 0)),
        out_shape=jax.ShapeDtypeStruct((D_MODEL, n_out), BF16),
        compiler_params=_params("arbitrary"),
        name="w_in_regroup",
    )(w_in_t)


N_LOW_TILES, N_F32_TILES = 7, 4
Z_BLK, GM_BLK, GS_BLK = 0, 1, 2
BC_TILE = 6
XS_BLK = 0
MLA_TILE, MISC_TILE = 2, 3
DT_BLK = (MISC_TILE * IN_TILE + 256) // 128
ATTN_HEAD_COLS = 256


IN_OUT_SLOTS = 2


def _in_kernel(x_ref, sh_ref, sc_ref, g_ref, w_ref, cw_ref, cb_ref, lo_hbm, hi_hbm, h_scr, u_scr, lo_stage, hi_stage,
               sems, *, seq_len):
    r = pl.program_id(0)
    chunk = IN_CHUNK
    n_chunks = ROW_GROUP // chunk
    rows = lambda c: slice(c * chunk, (c + 1) * chunk)
    outputs = ((lo_hbm, lo_stage), (hi_hbm, hi_stage))
    in_flight = [[None] * IN_OUT_SLOTS for _ in outputs]
    n_tiles_out = [0 for _ in outputs]

    def put(val, which, col_tile, c):
        hbm, stage = outputs[which]
        slot = n_tiles_out[which] % IN_OUT_SLOTS
        if c == 0 and in_flight[which][slot] is not None:
            in_flight[which][slot].wait()
        stage[slot, rows(c), :] = val.astype(stage.dtype)
        if c == n_chunks - 1:
            row0 = pl.multiple_of(r * ROW_GROUP, ROW_GROUP)
            dst = hbm.at[pl.ds(row0, ROW_GROUP), pl.ds(col_tile * IN_TILE, IN_TILE)]
            copy = pltpu.make_async_copy(stage.at[slot], dst, sems.at[which, slot])
            copy.start()
            in_flight[which][slot] = copy
            n_tiles_out[which] += 1

    def pointwise(fn, tile, which, col_tile, stage_h=False):
        w = w_ref[:, tile * IN_TILE:(tile + 1) * IN_TILE]
        for c in range(n_chunks):
            if stage_h:
                h_scr[rows(c), :] = _norm_mod(x_ref[rows(c), :], g_ref, sc_ref, sh_ref)
            put(fn(_dot(h_scr[rows(c), :], w)), which, col_tile, c)

    def conv(tile, conv_tile, which, col_tile):
        w = w_ref[:, tile * IN_TILE:(tile + 1) * IN_TILE]
        cols = pl.ds(conv_tile * IN_TILE, IN_TILE)
        cw, cb = cw_ref.at[:, cols], cb_ref.at[:, cols]

        def conv_out(c):
            put(_silu(_dwconv3_rows(u_scr, c, chunk, slice(0, IN_TILE), cw, cb, seq_len)), which, col_tile, c)
        for c in range(n_chunks):
            u_scr[_stage_rows(c, chunk), :] = _dot(h_scr[rows(c), :], w)
            if c >= 1:
                conv_out(c - 1)
        conv_out(n_chunks - 1)

    _zero_conv_pads(u_scr)
    pointwise(_silu, 0, 0, 0, stage_h=True)
    pointwise(_silu, 1, 0, 1)
    for tile in range(2, 6):
        pointwise(_sigmoid, tile, 0, tile)
    conv(BC_TILE, 2, 0, BC_TILE)
    conv(N_LOW_TILES, 0, 1, 0)
    conv(N_LOW_TILES + 1, 1, 1, 1)
    for tile in range(N_LOW_TILES + MLA_TILE, N_LOW_TILES + N_F32_TILES):
        pointwise(lambda u: u, tile, 1, tile - N_LOW_TILES)
    for slots in in_flight:
        for copy in slots:
            if copy is not None:
                copy.wait()


def _in_proj(x2d, mod48, mod_row, norm_g, w_in_r, conv_w, conv_b, seq_len):
    t = x2d.shape[0]
    once = dict(pipeline_mode=pl.Buffered(1))
    return pl.pallas_call(
        functools.partial(_in_kernel, seq_len=seq_len),
        grid=(t // ROW_GROUP,),
        in_specs=[pl.BlockSpec((ROW_GROUP, D_MODEL), lambda r: (r, 0)),
                  pl.BlockSpec((None, 1, D_MODEL), lambda r: (mod_row(r) * 6 + 0, 0, 0)),
                  pl.BlockSpec((None, 1, D_MODEL), lambda r: (mod_row(r) * 6 + 1, 0, 0)),
                  pl.BlockSpec((1, D_MODEL), lambda r: (0, 0)),
                  pl.BlockSpec(w_in_r.shape, lambda r: (0, 0), **once),
                  pl.BlockSpec(conv_w.shape, lambda r: (0, 0)),
                  pl.BlockSpec(conv_b.shape, lambda r: (0, 0))],
        out_specs=[pl.BlockSpec(memory_space=pl.ANY), pl.BlockSpec(memory_space=pl.ANY)],
        out_shape=[jax.ShapeDtypeStruct((t, N_LOW_TILES * IN_TILE), BF16),
                   jax.ShapeDtypeStruct((t, N_F32_TILES * IN_TILE), F32)],
        scratch_shapes=[pltpu.VMEM((ROW_GROUP, D_MODEL), BF16),
                        pltpu.VMEM((ROW_GROUP + 2 * CONV_PAD, IN_TILE), F32),
                        pltpu.VMEM((IN_OUT_SLOTS, ROW_GROUP, IN_TILE), BF16),
                        pltpu.VMEM((IN_OUT_SLOTS, ROW_GROUP, IN_TILE), F32),
                        pltpu.SemaphoreType.DMA((2, IN_OUT_SLOTS))],
        compiler_params=_params("arbitrary"),
        name="in_proj",
    )(x2d, mod48, mod48, norm_g, w_in_r, conv_w, conv_b)


def _mla_kernel(*refs, latent, has_ctx, emit_cache, seq_len, tq, seqs):
    n_in = 7 + (2 if latent else 0) + (2 if has_ctx else 0)
    n_out = 3 if emit_cache else 1
    per_seq = {0, 1} | ({n_in - 2, n_in - 1} if has_ctx else set())
    for bi in range(seqs):
        view = lambda r: r.at[pl.ds(bi, 1)]
        ins = [view(r) if i in per_seq else r for i, r in enumerate(refs[:n_in])]
        outs = [view(r) for r in refs[n_in:n_in + n_out]]
        _mla_sequence(*ins, *outs, *refs[n_in + n_out:], latent=latent, has_ctx=has_ctx, emit_cache=emit_cache,
                      seq_len=seq_len, tq=tq)


def _mla_sequence(*refs, latent, has_ctx, emit_cache, seq_len, tq):
    refs = list(refs)
    pm_ref, px_ref, qg_ref, kvg_ref, wuq_ref, wkt_ref, wv_ref = refs[:7]
    del refs[:7]
    if latent:
        cos_ref, sin_ref = refs[:2]
        del refs[:2]
    if has_ctx:
        cckv_ref, ckr_ref = refs[:2]
        del refs[:2]
    o_ref = refs.pop(0)
    if emit_cache:
        ckv_ref, kr_ref = refs[:2]
        del refs[:2]
    k_scr, v_scr = refs[:2]
    del refs[:2]
    if has_ctx:
        kc_scr, vc_scr = refs
    t = pl.program_id(1)
    n_nope = N_HEADS * QK_NOPE

    def put_keys(k_dst, v_dst, rows, ckv_n, kr_bf):
        ckv_bf = ckv_n.astype(BF16)
        v_dst[rows, :] = _dot(ckv_bf, wv_ref[...]).astype(BF16)
        kn_t = _dot_nt(wkt_ref[...], ckv_bf).astype(BF16)
        width = kr_bf.shape[1]
        eye = jnp.where(lax.broadcasted_iota(jnp.int32, (128, width), 0)
                        == lax.broadcasted_iota(jnp.int32, (128, width), 1), 1.0, 0.0).astype(BF16)
        kr_t = _dot_nt(eye, kr_bf).astype(BF16)
        for h in range(N_HEADS):
            base = h * ATTN_HEAD_COLS
            k_dst[base:base + QK_NOPE, rows] = kn_t[h * QK_NOPE:(h + 1) * QK_NOPE, :]
            k_dst[base + QK_NOPE:base + ATTN_HEAD_COLS, rows] = kr_t

    @pl.when(t == 0)
    def _():
        step = min(seq_len, TOKEN_TILE)
        for r0 in range(0, seq_len, step):
            rows = slice(r0, r0 + step)
            ckv_n = _rmsnorm(pm_ref[0, rows, Q_LORA:], kvg_ref[...])
            kr = px_ref[0, rows, 0:128]
            if latent:
                kr = kr * cos_ref[rows, :] + px_ref[0, rows, 128:256] * sin_ref[rows, :]
            if emit_cache:
                ckv_ref[0, rows, :] = ckv_n
                kr_ref[0, rows, :] = kr[:, :QK_ROPE]
            put_keys(k_scr, v_scr, rows, ckv_n, kr.astype(BF16))
        if has_ctx:
            past = cckv_ref.shape[1]
            put_keys(kc_scr, vc_scr, slice(0, past), cckv_ref[0], ckr_ref[0].astype(BF16))

    qrows = pl.ds(pl.multiple_of(t * tq, tq), tq)
    scale = LOG2_E / math.sqrt(QK_NOPE + QK_ROPE)
    cqn = _rmsnorm(pm_ref[0, qrows, 0:Q_LORA], qg_ref[...]).astype(BF16)
    q = _dot(cqn, wuq_ref[...])
    q_rope = q[:, n_nope:2 * n_nope]
    if latent:
        q_rope = (q_rope * jnp.concatenate([cos_ref[qrows, :]] * N_HEADS, axis=1)
                  + q[:, 2 * n_nope:3 * n_nope] * jnp.concatenate([sin_ref[qrows, :]] * N_HEADS, axis=1))
    segs = ([(kc_scr, vc_scr)] if has_ctx else []) + [(k_scr, v_scr)]
    for h in range(N_HEADS):
        head = slice(h * QK_NOPE, (h + 1) * QK_NOPE)
        qk = slice(h * ATTN_HEAD_COLS, (h + 1) * ATTN_HEAD_COLS)
        qh = (jnp.concatenate([q[:, head], q_rope[:, head]], axis=1) * scale).astype(BF16)
        s = [_dot(qh, k[qk, :]) for k, _ in segs]
        m = functools.reduce(jnp.maximum, [jnp.max(si, axis=-1, keepdims=True) for si in s])
        p = [jnp.exp2(si - m) for si in s]
        l = functools.reduce(jnp.add, [jnp.sum(pi, axis=-1, keepdims=True) for pi in p])
        o = functools.reduce(jnp.add, [_dot(pi.astype(BF16), v[:, head]) for pi, (_, v) in zip(p, segs)])
        o_ref[0, :, head] = (o / l).astype(BF16)


def _mla(proj_hi, q_norm_g, kv_norm_g, w_uq_r, w_uk_t, w_uv, rope_tables, ctx, emit_cache):
    b, l, _ = proj_hi.shape
    tq = min(l, ATTN_Q_TILE)
    latent = rope_tables is not None
    has_ctx = ctx is not None
    once = dict(pipeline_mode=pl.Buffered(1))
    per_batch = once if l // tq > 1 else {}
    seqs = MLA_SEQS_PER_STEP if (l == tq and b % MLA_SEQS_PER_STEP == 0) else 1
    const2 = lambda i, t: (0, 0)
    in_specs = [pl.BlockSpec((seqs, l, IN_TILE), lambda i, t: (i, 0, MLA_TILE), **per_batch),
                pl.BlockSpec((seqs, l, 256), lambda i, t: (i, 0, MISC_TILE * IN_TILE // 256), **per_batch),
                pl.BlockSpec((1, Q_LORA), const2),
                pl.BlockSpec((1, KV_LORA), const2),
                pl.BlockSpec(w_uq_r.shape, const2, **once),
                pl.BlockSpec(w_uk_t.shape, const2, **once),
                pl.BlockSpec(w_uv.shape, const2, **once)]
    args = [proj_hi, proj_hi, q_norm_g, kv_norm_g, w_uq_r, w_uk_t, w_uv]
    scratch = [pltpu.VMEM((N_HEADS * ATTN_HEAD_COLS, l), BF16), pltpu.VMEM((l, N_HEADS * V_HEAD), BF16)]
    if latent:
        in_specs += [pl.BlockSpec((l, 128), const2, **once)] * 2
        args += list(rope_tables)
    if has_ctx:
        past = ctx[0].shape[1]
        in_specs += [pl.BlockSpec((seqs, past, KV_LORA), lambda i, t: (i, 0, 0), **per_batch),
                     pl.BlockSpec((seqs, past, QK_ROPE), lambda i, t: (i, 0, 0), **per_batch)]
        args += list(ctx)
        scratch += [pltpu.VMEM((N_HEADS * ATTN_HEAD_COLS, past), BF16), pltpu.VMEM((past, N_HEADS * V_HEAD), BF16)]
    out_specs = [pl.BlockSpec((seqs, tq, N_HEADS * V_HEAD), lambda i, t: (i, t, 0))]
    out_shape = [jax.ShapeDtypeStruct((b, l, N_HEADS * V_HEAD), BF16)]
    if emit_cache:
        out_specs += [pl.BlockSpec((seqs, l, KV_LORA), lambda i, t: (i, 0, 0)),
                      pl.BlockSpec((seqs, l, QK_ROPE), lambda i, t: (i, 0, 0))]
        out_shape += [jax.ShapeDtypeStruct((b, l, KV_LORA), F32), jax.ShapeDtypeStruct((b, l, QK_ROPE), F32)]
    return pl.pallas_call(
        functools.partial(_mla_kernel, latent=latent, has_ctx=has_ctx, emit_cache=emit_cache, seq_len=l, tq=tq,
                          seqs=seqs),
        grid=(b // seqs, l // tq),
        in_specs=in_specs,
        out_specs=out_specs,
        out_shape=out_shape,
        scratch_shapes=scratch,
        compiler_params=_params("arbitrary", "arbitrary", vmem_limit=MLA_VMEM_LIMIT),
        name="mla_attention",
    )(*args)


def _split3(x):
    hi = x.astype(BF16)
    r = x - hi.astype(F32)
    mid = r.astype(BF16)
    lo = (r - mid.astype(F32)).astype(BF16)
    return hi, mid, lo


def _exact_dot(parts, sel):
    return functools.reduce(jnp.add, [_dot(p, sel) for p in parts])


def _exact_dot_rows(sel, parts):
    return functools.reduce(jnp.add, [_dot(sel, p) for p in parts])


HEADS_PER_GROUP = SSD_HEADS // SSD_GROUPS
GROUP_COLS = HEADS_PER_GROUP * SSD_HEADDIM


def _ssd_kernel(*refs, nc, cps, has_h0, seqs):
    n_in = 8 if has_h0 else 7
    per_seq = {0, 1, 2, 3, 7} if has_h0 else {0, 1, 2, 3}
    for bi in range(seqs):
        view = lambda r: r.at[pl.ds(bi, 1)]
        ins = [view(r) if i in per_seq else r for i, r in enumerate(refs[:n_in])]
        outs = [view(r) for r in refs[n_in:n_in + 2]]
        _ssd_sequence(*ins, *outs, *refs[n_in + 2:], nc=nc, cps=cps, has_h0=has_h0)


def _ssd_sequence(*refs, nc, cps, has_h0):
    if has_h0:
        (xs_ref, zs_ref, bc_ref, dt_ref, dtb_ref, a_ref, dx_ref, h0_ref, y_ref, ht_ref,
         yl_scr, acum_scr, src_scr, tot_scr, sb_scr, h_scr, esel_scr) = refs
    else:
        (xs_ref, zs_ref, bc_ref, dt_ref, dtb_ref, a_ref, dx_ref, y_ref, ht_ref,
         yl_scr, acum_scr, src_scr, tot_scr, sb_scr, h_scr, esel_scr) = refs
    s = pl.program_id(1)
    q = SSD_CHUNK
    n_bc = SSD_GROUPS * SSD_STATE
    lane = lax.broadcasted_iota(jnp.int32, (q, 128), 1)
    low_half = lane < SSD_HEADDIM
    ii = lax.broadcasted_iota(jnp.int32, (q, q), 0)
    jj = lax.broadcasted_iota(jnp.int32, (q, q), 1)
    lower, upper = ii >= jj, ii <= jj

    def lane_bcast(parts, d):
        return _exact_dot(parts, esel_scr[d])

    def stacked_states(d):
        return [jnp.concatenate([h_scr[d, 2 * i], h_scr[d, 2 * i + 1]], axis=0).astype(BF16)
                for i in range(SSD_GROUPS // 2)]

    def group_c(bc, g):
        i, r = divmod(g, 2)
        cpair = bc[:, n_bc + i * 128:n_bc + (i + 1) * 128]
        return jnp.where(low_half if r == 0 else ~low_half, cpair, 0.0).astype(BF16)

    @pl.when(s == 0)
    def _():
        k = lax.broadcasted_iota(jnp.int32, (128, SSD_INNER), 0)
        head = lax.broadcasted_iota(jnp.int32, (128, SSD_INNER), 1) // SSD_HEADDIM
        for d in range(2):
            esel_scr[d] = jnp.where(k == d * SSD_HEADS + head, 1.0, 0.0).astype(BF16)
        if has_h0:
            for d in range(2):
                for g in range(SSD_GROUPS):
                    hpn = h0_ref[0, d, g * HEADS_PER_GROUP:(g + 1) * HEADS_PER_GROUP].reshape(GROUP_COLS, SSD_STATE)
                    h_scr[d, g] = hpn.T
        else:
            h_scr[...] = jnp.zeros(h_scr.shape, F32)
        tri_f = jnp.where(lower, 1.0, 0.0).astype(BF16)
        tri_b = jnp.where(upper, 1.0, 0.0).astype(BF16)
        fwd_col = lane < SSD_HEADS
        for c in range(nc):
            crow = slice(c * q, (c + 1) * q)
            dt = _softplus(dt_ref[0, crow, :] + dtb_ref[...])
            parts = _split3(dt * a_ref[...])
            acum = jnp.where(fwd_col, _exact_dot_rows(tri_f, parts),
                             _exact_dot_rows(tri_b, parts))
            acum = acum * LOG2_E
            acum_scr[crow, :] = acum
            tot = jnp.where(fwd_col[0:1], acum[q - 1:q, :], acum[0:1, :])
            tot_scr[c] = jnp.broadcast_to(tot, (8, 128))
            src_scr[c] = (acum - jnp.log2(dt)).T

    def first_sweep(c, blk):
        rows = pl.ds(pl.multiple_of(c * q, q), q)
        x = xs_ref[0, blk, :]
        bc = bc_ref[0, blk, :].astype(F32)
        acum = acum_scr[rows, :]
        src_t = src_scr[c]
        tot8 = tot_scr[c]
        e = jnp.exp2(acum)
        e_hi = e.astype(BF16)
        eb_f = lane_bcast([e_hi, (e - e_hi.astype(F32)).astype(BF16)], 0)
        cd_f = jnp.exp2(lane_bcast(_split3(tot8), 0))[0:1]
        b_t = [bc[:, i * 128:(i + 1) * 128].T for i in range(SSD_GROUPS // 2)]
        h_in = stacked_states(0)
        for g in range(SSD_GROUPS):
            i, r = divmod(g, 2)
            cm = group_c(bc, g)
            scores = _dot_nt(cm, bc[:, i * 128:(i + 1) * 128].astype(BF16))
            bg_t = b_t[i][r * SSD_STATE:(r + 1) * SSD_STATE, :]
            gcols = slice(g * GROUP_COLS, (g + 1) * GROUP_COLS)
            y_off = _dot(cm, h_in[i]) * eb_f[:, gcols]
            for t in range(HEADS_PER_GROUP // 2):
                pc = slice((2 * g + t) * 128, (2 * g + t + 1) * 128)
                xp = x[:, pc]
                x2 = jnp.concatenate([jnp.where(low_half, xp, 0.0), jnp.where(low_half, 0.0, xp)], axis=0).astype(BF16)
                m, sf, sb = [], [], []
                for u in range(2):
                    kf = g * HEADS_PER_GROUP + 2 * t + u
                    kb = SSD_HEADS + kf
                    af_col = jnp.broadcast_to(acum[:, kf:kf + 1], (q, q))
                    ab_col = jnp.broadcast_to(acum[:, kb:kb + 1], (q, q))
                    sf_row, sb_row = src_t[kf:kf + 1, :], src_t[kb:kb + 1, :]
                    decay = (jnp.exp2(jnp.where(lower, af_col - sf_row, NEG_BIG))
                             + jnp.exp2(jnp.where(upper, ab_col - sb_row, NEG_BIG)))
                    m.append((scores * decay).astype(BF16))
                    wf = jnp.exp2(tot8[0:1, kf:kf + 1] - sf_row)
                    wb = jnp.exp2(tot8[0:1, kb:kb + 1] - sb_row)
                    sf.append((bg_t * wf).astype(BF16))
                    sb.append((bg_t * wb).astype(BF16))
                y_pair = _dot(jnp.concatenate(m, axis=1), x2)
                tc = slice(t * 128, (t + 1) * 128)
                yl_scr[rows, pc] = y_pair + y_off[:, tc] + dx_ref[:, pc] * xp
                h_scr[0, g, :, tc] = h_scr[0, g, :, tc] * cd_f[:, pc] + _dot(jnp.concatenate(sf, axis=1), x2)
                sb_scr[c, g, :, tc] = _dot(jnp.concatenate(sb, axis=1), x2)

    def second_sweep(c, blk):
        rows = pl.ds(pl.multiple_of(c * q, q), q)
        bc = bc_ref[0, blk, :].astype(F32)
        e = jnp.exp2(acum_scr[rows, :])
        e_hi = e.astype(BF16)
        eb_b = lane_bcast([e_hi, (e - e_hi.astype(F32)).astype(BF16)], 1)
        cd_b = jnp.exp2(lane_bcast(_split3(tot_scr[c]), 1))[0:1]
        h_in = stacked_states(1)
        for g in range(SSD_GROUPS):
            gcols = slice(g * GROUP_COLS, (g + 1) * GROUP_COLS)
            y_off = _dot(group_c(bc, g), h_in[g // 2]) * eb_b[:, gcols]
            y_ref[0, blk, gcols] = ((yl_scr[rows, gcols] + y_off) * zs_ref[0, blk, gcols].astype(F32)).astype(y_ref.dtype)
            h_scr[1, g] = h_scr[1, g] * cd_b[:, gcols] + sb_scr[c, g]

    n_steps = nc // cps
    one_step = n_steps == 1

    @pl.when(s < n_steps)
    def _():
        for ci in range(cps):
            first_sweep(s * cps + ci, slice(ci * q, (ci + 1) * q))

    @pl.when(jnp.logical_or(one_step, s >= n_steps))
    def _():
        blk_id = s * 0 if one_step else 2 * n_steps - 1 - s
        for ci in reversed(range(cps)):
            second_sweep(blk_id * cps + ci, slice(ci * q, (ci + 1) * q))

    @pl.when(jnp.logical_or(one_step, s == 2 * n_steps - 1))
    def _():
        for d in range(2):
            for g in range(SSD_GROUPS):
                ht_ref[0, d, g * HEADS_PER_GROUP:(g + 1) * HEADS_PER_GROUP] = h_scr[d, g].T.reshape(
                    HEADS_PER_GROUP, SSD_HEADDIM, SSD_STATE)


def _ssd(proj_lo, proj_hi, h0, dt_bias128, a128, d_exp):
    b, l, _ = proj_hi.shape
    q = SSD_CHUNK
    nc = l // q
    cps = min(nc, SSD_CHUNKS_PER_STEP)
    n_steps = nc // cps
    n_grid = 1 if n_steps == 1 else 2 * n_steps
    rows = cps * q
    early = lambda s: jnp.minimum(s, n_steps - 1)
    both = lambda s: jnp.where(s < n_steps, s, 2 * n_steps - 1 - s)
    late = lambda s: jnp.where(s < n_steps, n_steps - 1, 2 * n_steps - 1 - s)
    seqs = SSD_SEQS_PER_STEP if (n_steps == 1 and b % SSD_SEQS_PER_STEP == 0) else 1
    st_shape = (seqs, 2, SSD_HEADS, SSD_HEADDIM, SSD_STATE)
    st_spec = pl.BlockSpec(st_shape, lambda i, s: (i, 0, 0, 0, 0))
    has_h0 = h0 is not None
    return pl.pallas_call(
        functools.partial(_ssd_kernel, nc=nc, cps=cps, has_h0=has_h0, seqs=seqs),
        grid=(b // seqs, n_grid),
        in_specs=[pl.BlockSpec((seqs, rows, SSD_INNER), lambda i, s: (i, early(s), XS_BLK)),
                  pl.BlockSpec((seqs, rows, SSD_INNER), lambda i, s: (i, late(s), Z_BLK)),
                  pl.BlockSpec((seqs, rows, IN_TILE), lambda i, s: (i, both(s), BC_TILE)),
                  pl.BlockSpec((seqs, l, 128), lambda i, s: (i, 0, DT_BLK)),
                  pl.BlockSpec((1, 128), lambda i, s: (0, 0)),
                  pl.BlockSpec((1, 128), lambda i, s: (0, 0)),
                  pl.BlockSpec((1, SSD_INNER), lambda i, s: (0, 0))] + ([st_spec] if has_h0 else []),
        out_specs=[pl.BlockSpec((seqs, rows, SSD_INNER), lambda i, s: (i, late(s), 0)),
                   pl.BlockSpec(st_shape, lambda i, s: (i, 0, 0, 0, 0))],
        out_shape=[jax.ShapeDtypeStruct((b, l, SSD_INNER), BF16),
                   jax.ShapeDtypeStruct((b,) + st_shape[1:], F32)],
        scratch_shapes=[pltpu.VMEM((l, SSD_INNER), F32),
                        pltpu.VMEM((l, 128), F32),
                        pltpu.VMEM((nc, 128, q), F32),
                        pltpu.VMEM((nc, 8, 128), F32),
                        pltpu.VMEM((nc, SSD_GROUPS, SSD_STATE, GROUP_COLS), F32),
                        pltpu.VMEM((2, SSD_GROUPS, SSD_STATE, GROUP_COLS), F32),
                        pltpu.VMEM((2, 128, SSD_INNER), BF16)],
        compiler_params=_params("arbitrary", "arbitrary"),
        name="ssd_scan",
    )(proj_hi, proj_lo, proj_lo, proj_hi, dt_bias128, a128, d_exp, *([h0] if has_h0 else []))


def _merge_kernel(attn_ref, yz_ref, gm_ref, gs_ref, x_ref, g1_ref, ng_ref, womla_ref, wossd_ref, wout_ref, o_ref, w_scr):
    @pl.when(pl.program_id(0) == 0)
    def _():
        w_scr[0] = womla_ref[...].astype(BF16)
        w_scr[1] = wossd_ref[...].astype(BF16)
        w_scr[2] = wout_ref[...].astype(BF16)

    o_mla = _dot(attn_ref[...], w_scr[0])
    o_ssd = _dot(_rmsnorm(yz_ref[...].astype(F32), ng_ref[...]).astype(BF16), w_scr[1])
    merged = gm_ref[...].astype(F32) * o_mla + gs_ref[...].astype(F32) * o_ssd
    o_ref[...] = x_ref[...] + g1_ref[...] * _dot(merged.astype(BF16), w_scr[2])


def _merge(attn2d, yz2d, proj, x2d, mod48, mod_row, ssd_norm_g, w_o_mla, w_o_ssd, w_out):
    t = x2d.shape[0]
    tm = TOKEN_TILE
    row = lambda i: (i, 0)
    const = lambda i: (0, 0)
    wspec = pl.BlockSpec((D_MODEL, D_MODEL), const, pipeline_mode=pl.Buffered(1))
    return pl.pallas_call(
        _merge_kernel,
        grid=(t // tm,),
        in_specs=[pl.BlockSpec((tm, D_MODEL), row),
                  pl.BlockSpec((tm, D_MODEL), row),
                  pl.BlockSpec((tm, D_MODEL), lambda i: (i, GM_BLK)),
                  pl.BlockSpec((tm, D_MODEL), lambda i: (i, GS_BLK)),
                  pl.BlockSpec((tm, D_MODEL), row),
                  pl.BlockSpec((None, 1, D_MODEL), lambda i: (mod_row(i * tm // ROW_GROUP) * 6 + 2, 0, 0)),
                  pl.BlockSpec((1, D_MODEL), const),
                  wspec, wspec, wspec],
        out_specs=pl.BlockSpec((tm, D_MODEL), row),
        out_shape=jax.ShapeDtypeStruct((t, D_MODEL), F32),
        scratch_shapes=[pltpu.VMEM((3, D_MODEL, D_MODEL), BF16)],
        compiler_params=_params("arbitrary"),
        name="merge_out",
    )(attn2d, yz2d, proj, proj, x2d, mod48, ssd_norm_g, w_o_mla, w_o_ssd, w_out)


def _ffn_kernel(x_ref, sh_ref, sc_ref, g2_ref, ng_ref, wg_ref, wv_ref, cwg_ref, cwv_ref, cbg_ref, cbv_ref, wd_ref,
                fg_ref, o_ref, h_scr, wup_scr, wd_scr, u_scr, *, seq_len):
    j = pl.program_id(1)
    chunk = FFN_CHUNK
    n_chunks = ROW_GROUP // chunk
    rows = lambda c: slice(c * chunk, (c + 1) * chunk)

    wup_scr[:, 0:FFN_TILE] = wg_ref[...].astype(BF16)
    wup_scr[:, FFN_TILE:2 * FFN_TILE] = wv_ref[...].astype(BF16)
    wd_scr[...] = wd_ref[...].astype(BF16)

    def gated(c):
        ug = _dwconv3_rows(u_scr, c, chunk, slice(0, FFN_TILE), cwg_ref, cbg_ref, seq_len)
        uv = _dwconv3_rows(u_scr, c, chunk, slice(FFN_TILE, 2 * FFN_TILE), cwv_ref, cbv_ref, seq_len)
        return (_silu(ug) * uv).astype(BF16)

    def pipeline(first, last):
        act = {}
        lag = FFN_DOWN_LAG
        for c in range(n_chunks + lag):
            if c < n_chunks:
                if first:
                    h_scr[rows(c), :] = _norm_mod(x_ref[rows(c), :], ng_ref, sc_ref, sh_ref)
                u_scr[_stage_rows(c, chunk), :] = _dot(h_scr[rows(c), :], wup_scr[...])
            if c >= lag:
                r = rows(c - lag)
                acc = _dot(act.pop(c - lag), wd_scr[...])
                if not first:
                    acc = o_ref[r, :] + acc
                if last:
                    acc = _rmsnorm(x_ref[r, :] + g2_ref[...] * acc, fg_ref[...])
                o_ref[r, :] = acc
            if 1 <= c <= n_chunks:
                act[c - 1] = gated(c - 1)

    last_j = pl.num_programs(1) - 1

    @pl.when(j == 0)
    def _():
        _zero_conv_pads(u_scr)
        pipeline(True, False)

    @pl.when((j > 0) & (j < last_j))
    def _():
        pipeline(False, False)

    @pl.when(j == last_j)
    def _():
        pipeline(False, True)


def _ffn(x2d, mod48, mod_row, norm_g, w_up, conv_w, conv_b, w_down, final_g, seq_len):
    t = x2d.shape[0]
    nj = D_FF // FFN_TILE
    gate = lambda r, j: (0, j)
    val = lambda r, j: (0, nj + j)
    const = lambda r, j: (0, 0)
    mod = lambda k: pl.BlockSpec((None, 1, D_MODEL), lambda r, j: (mod_row(r) * 6 + k, 0, 0))
    return pl.pallas_call(
        functools.partial(_ffn_kernel, seq_len=seq_len),
        grid=(t // ROW_GROUP, nj),
        in_specs=[pl.BlockSpec((ROW_GROUP, D_MODEL), lambda r, j: (r, 0)),
                  mod(3), mod(4), mod(5),
                  pl.BlockSpec((1, D_MODEL), const),
                  pl.BlockSpec((D_MODEL, FFN_TILE), gate),
                  pl.BlockSpec((D_MODEL, FFN_TILE), val),
                  pl.BlockSpec((3, FFN_TILE), gate),
                  pl.BlockSpec((3, FFN_TILE), val),
                  pl.BlockSpec((1, FFN_TILE), gate),
                  pl.BlockSpec((1, FFN_TILE), val),
                  pl.BlockSpec((FFN_TILE, D_MODEL), lambda r, j: (j, 0)),
                  pl.BlockSpec((1, D_MODEL), const)],
        out_specs=pl.BlockSpec((ROW_GROUP, D_MODEL), lambda r, j: (r, 0)),
        out_shape=jax.ShapeDtypeStruct((t, D_MODEL), F32),
        scratch_shapes=[pltpu.VMEM((ROW_GROUP, D_MODEL), BF16),
                        pltpu.VMEM((D_MODEL, 2 * FFN_TILE), BF16),
                        pltpu.VMEM((FFN_TILE, D_MODEL), BF16),
                        pltpu.VMEM((ROW_GROUP + 2 * CONV_PAD, 2 * FFN_TILE), F32)],
        compiler_params=_params("arbitrary", "arbitrary"),
        name="conv_ffn",
    )(x2d, mod48, mod48, mod48, norm_g, w_up, w_up, conv_w, conv_w, conv_b, conv_b, w_down, final_g)


def _rope_tables(seq_len):
    t = np.arange(seq_len)
    row = (t // GRID_W).astype(np.float32)
    col = (t % GRID_W).astype(np.float32)
    n = QK_ROPE // 4
    inv = (np.float32(ROPE_BASE) ** (-np.arange(n, dtype=np.float32) / np.float32(n))).astype(np.float32)
    ar, ac = row[:, None] * inv, col[:, None] * inv
    cos64 = np.concatenate([np.cos(ar), np.cos(ar), np.cos(ac), np.cos(ac)], axis=1)
    sin64 = np.concatenate([-np.sin(ar), np.sin(ar), -np.sin(ac), np.sin(ac)], axis=1)
    zeros = np.zeros_like(cos64)
    return (jnp.asarray(np.concatenate([cos64, zeros], axis=1), F32),
            jnp.asarray(np.concatenate([sin64, zeros], axis=1), F32))


def _swap_rope_halves(w):
    lead = w.shape[:-1]
    return w.reshape(lead + (2, 2, QK_ROPE // 4))[..., ::-1, :].reshape(lead + (QK_ROPE,))


def _trunk_pass(x, mod48, mod_row, wts, ctx, latent):
    b, l, _ = x.shape
    x2d = x.reshape(b * l, D_MODEL)
    proj_lo, proj_hi = _in_proj(x2d, mod48, mod_row, wts["norm_attn_g"], wts["w_in_r"], wts["ssd_conv_w"],
                                wts["ssd_conv_b"], l)
    rope = _rope_tables(l) if latent else None
    w_uq_r = wts["w_uq_lat"] if latent else wts["w_uq_ctx"]
    shape3 = lambda a, n: a.reshape(b, n, a.shape[-1])
    h0 = None
    mla_ctx = None
    if ctx is not None:
        cache_ckv, cache_krope, h0 = ctx
        mla_ctx = (cache_ckv, cache_krope)
    emit_cache = ctx is None
    mla_out = _mla(shape3(proj_hi, l), wts["q_norm_g"], wts["kv_norm_g"], w_uq_r, wts["w_uk_t"], wts["w_uv"],
                   rope, mla_ctx, emit_cache)
    attn, ckv_n, kr3 = mla_out if emit_cache else (mla_out[0], None, None)
    yz, h_t = _ssd(shape3(proj_lo, l), shape3(proj_hi, l), h0, wts["dt_bias128"], wts["a128"], wts["d_exp"])
    x1 = _merge(attn.reshape(b * l, -1), yz.reshape(b * l, -1), proj_lo, x2d, mod48, mod_row, wts["ssd_norm_g"],
                wts["w_o_mla"], wts["w_o_ssd"], wts["w_out"])
    y = _ffn(x1, mod48, mod_row, wts["norm_ffn_g"], wts["w_up"], wts["ffn_conv_w"], wts["ffn_conv_b"], wts["w_down"],
             wts["final_norm_g"], l)
    return y.reshape(b, l, D_MODEL), ckv_n, kr3, h_t


def kernel(x_prompt, x_sample, c, cache_ckv, cache_krope, state_ssd, c_ctx, w_ada, b_ada, norm_attn_g, w_in, q_norm_g,
           kv_norm_g, w_uq, w_ukv, w_o_mla, ssd_conv_w, ssd_conv_b, ssd_dt_bias, ssd_A_log, ssd_D, ssd_norm_g, w_o_ssd,
           w_out, norm_ffn_g, w_up, ffn_conv_w, ffn_conv_b, w_down, final_norm_g):
    depth = w_in.shape[0]
    assert depth == 1, "single trunk layer"
    dec_b = x_sample.shape[0]
    assert x_sample.shape[1] == ROW_GROUP and ROW_GROUP % x_prompt.shape[1] == 0
    lyr = 0

    cvec = jnp.zeros((8, D_MODEL), F32).at[0].set(c_ctx).at[1:1 + dec_b].set(c)
    mod48 = _ada(cvec, w_ada[lyr], b_ada[lyr]).reshape(8 * 6, 1, D_MODEL)

    w_in_r = _regroup_w_in(w_in[lyr].T)
    wq = w_uq[lyr].reshape(Q_LORA, N_HEADS, QK_NOPE + QK_ROPE)
    wq_nope = wq[:, :, :QK_NOPE].reshape(Q_LORA, -1)
    wq_rope = wq[:, :, QK_NOPE:]
    pad_rope = lambda w: jnp.pad(w, ((0, 0), (0, 0), (0, 128 - QK_ROPE))).reshape(Q_LORA, -1)
    w_uq_ctx = jnp.concatenate([wq_nope, pad_rope(wq_rope)], axis=1).astype(BF16)
    w_uq_lat = jnp.concatenate([wq_nope, pad_rope(wq_rope), pad_rope(_swap_rope_halves(wq_rope))], axis=1).astype(BF16)
    wkv = w_ukv[lyr].reshape(KV_LORA, N_HEADS, QK_NOPE + V_HEAD)
    w_uk_t = wkv[:, :, :QK_NOPE].reshape(KV_LORA, -1).T.astype(BF16)
    w_uv = wkv[:, :, QK_NOPE:].reshape(KV_LORA, -1).astype(BF16)
    pad128 = lambda a: jnp.pad(a.reshape(1, -1), ((0, 0), (0, 128 - a.size)))
    wts = {
        "norm_attn_g": norm_attn_g[lyr].reshape(1, -1), "w_in_r": w_in_r,
        "ssd_conv_w": ssd_conv_w[lyr], "ssd_conv_b": ssd_conv_b[lyr].reshape(1, -1),
        "q_norm_g": q_norm_g[lyr].reshape(1, -1), "kv_norm_g": kv_norm_g[lyr].reshape(1, -1),
        "w_uq_ctx": w_uq_ctx, "w_uq_lat": w_uq_lat, "w_uk_t": w_uk_t, "w_uv": w_uv,
        "dt_bias128": pad128(ssd_dt_bias[lyr]), "a128": pad128(-jnp.exp(ssd_A_log[lyr])),
        "d_exp": jnp.repeat(ssd_D[lyr], SSD_HEADDIM).reshape(1, -1),
        "ssd_norm_g": ssd_norm_g[lyr].reshape(1, -1),
        "w_o_mla": w_o_mla[lyr], "w_o_ssd": w_o_ssd[lyr], "w_out": w_out[lyr],
        "norm_ffn_g": norm_ffn_g[lyr].reshape(1, -1), "w_up": w_up[lyr],
        "ffn_conv_w": ffn_conv_w[lyr], "ffn_conv_b": ffn_conv_b[lyr].reshape(1, -1),
        "w_down": w_down[lyr], "final_norm_g": final_norm_g.reshape(1, -1),
    }

    y_p, ckv_p, kr_p, st_p = _trunk_pass(x_prompt, mod48, lambda r: 0, wts, None, False)
    ctx = (cache_ckv[:, lyr], cache_krope[:, lyr], state_ssd[:, lyr])
    y_s, _, _, _ = _trunk_pass(x_sample, mod48, lambda r: 1 + r, wts, ctx, True)
    return y_p, y_s, ckv_p[:, None], kr_p[:, None], st_p[:, None]
```

```python
import functools
import math

import jax
import jax.numpy as jnp
import numpy as np
from jax import lax
from jax.experimental import pallas as pl
from jax.experimental.pallas import tpu as pltpu

F32 = jnp.float32
BF16 = jnp.bfloat16

D_MODEL = 1024
GRID_W = 64
N_HEADS = 8
QK_NOPE = 128
QK_ROPE = 64
V_HEAD = 128
Q_LORA = 256
KV_LORA = 256
ROPE_BASE = 10000.0
SSD_HEADS = 16
SSD_HEADDIM = 64
SSD_INNER = SSD_HEADS * SSD_HEADDIM
SSD_GROUPS = 4
SSD_STATE = 64
SSD_CHUNK = 128
D_FF = 2816
EPS = 1e-6

ROW_GROUP = 2048
IN_TILE = 512
SSD_CHUNKS_PER_STEP = 4
SSD_SEQS_PER_STEP = 2
MLA_SEQS_PER_STEP = 4
FFN_TILE = 256
TOKEN_TILE = 512
ATTN_Q_TILE = 512
IN_CHUNK = 256
FFN_CHUNK = 1024
FFN_DOWN_LAG = 2
CONV_PAD = 8
VMEM_LIMIT = 56 * 1024 * 1024
MLA_VMEM_LIMIT = 62 * 1024 * 1024
NEG_BIG = -1e30
LOG2_E = 1.4426950408889634


def _sigmoid(x):
    return 1.0 / (1.0 + jnp.exp(-x))


def _silu(x):
    return x * _sigmoid(x)


def _softplus(x):
    e = jnp.exp(-jnp.abs(x))
    u = 1.0 + e
    log1p_e = jnp.where(u == 1.0, e, e * jnp.log(u) / jnp.where(u == 1.0, 1.0, u - 1.0))
    return jnp.maximum(x, 0.0) + log1p_e


def _rmsnorm(x, g):
    return x * lax.rsqrt(jnp.mean(x * x, axis=-1, keepdims=True) + EPS) * g


def _dot(a, b):
    return jnp.dot(a, b, preferred_element_type=F32)


def _dot_nt(a, b):
    return lax.dot_general(a, b, (((1,), (1,)), ((), ())), preferred_element_type=F32)


def _params(*sem, vmem_limit=VMEM_LIMIT):
    return pltpu.CompilerParams(dimension_semantics=sem, vmem_limit_bytes=vmem_limit)


def _norm_mod(x, g_ref, sc_ref, sh_ref):
    return (_rmsnorm(x, g_ref[...]) * (1.0 + sc_ref[...]) + sh_ref[...]).astype(BF16)


def _zero_conv_pads(u_scr):
    zeros = jnp.zeros((CONV_PAD, u_scr.shape[1]), F32)
    u_scr[0:CONV_PAD, :] = zeros
    u_scr[CONV_PAD + ROW_GROUP:2 * CONV_PAD + ROW_GROUP, :] = zeros


def _stage_rows(c, chunk):
    return slice(CONV_PAD + c * chunk, CONV_PAD + (c + 1) * chunk)


def _dwconv3_rows(u_scr, c, chunk, cols, w_ref, b_ref, seq_len):
    r0 = c * chunk
    base = CONV_PAD + r0
    width = cols.stop - cols.start
    prev = u_scr[base - 1:base - 1 + chunk, cols]
    cur = u_scr[base:base + chunk, cols]
    nxt = u_scr[base + 1:base + 1 + chunk, cols]
    pos = (lax.broadcasted_iota(jnp.int32, (chunk, width), 0) + r0) & (seq_len - 1)
    if r0 % seq_len == 0 or chunk > seq_len:
        prev = jnp.where(pos == 0, 0.0, prev)
    if (r0 + chunk) % seq_len == 0 or chunk > seq_len:
        nxt = jnp.where(pos == seq_len - 1, 0.0, nxt)
    return prev * w_ref[0:1, :] + cur * w_ref[1:2, :] + nxt * w_ref[2:3, :] + b_ref[...]


def _ada_kernel(c_ref, w_ref, b_ref, o_ref):
    a = _silu(c_ref[...]).astype(BF16)
    o_ref[...] = _dot(a, w_ref[...].astype(BF16)) + b_ref[...]


def _ada(cvec, w_ada, b_ada):
    tn = 1536
    return pl.pallas_call(
        _ada_kernel,
        grid=(6 * D_MODEL // tn,),
        in_specs=[pl.BlockSpec((8, D_MODEL), lambda j: (0, 0)),
                  pl.BlockSpec((D_MODEL, tn), lambda j: (0, j)),
                  pl.BlockSpec((1, tn), lambda j: (0, j))],
        out_specs=pl.BlockSpec((8, tn), lambda j: (0, j)),
        out_shape=jax.ShapeDtypeStruct((8, 6 * D_MODEL), F32),
        compiler_params=_params("arbitrary"),
        name="ada_mod",
    )(cvec, w_ada, b_ada.reshape(1, -1))


IN_SPLITS = (Q_LORA, KV_LORA, QK_ROPE, SSD_INNER, SSD_INNER, SSD_GROUPS * SSD_STATE, SSD_GROUPS * SSD_STATE,
             2 * SSD_HEADS, D_MODEL, D_MODEL)
IN_OFFSETS = tuple(int(v) for v in np.cumsum((0,) + IN_SPLITS))


def _regroup_kernel(w_ref, o_ref):
    dst = 0

    def put_block(block):
        nonlocal dst
        o_ref[:, dst:dst + 128] = block.T.astype(BF16)
        dst += 128

    def piece(i):
        for off in range(0, IN_SPLITS[i], 128):
            put_block(w_ref[IN_OFFSETS[i] + off:IN_OFFSETS[i] + off + 128, :])

    def padded(parts):
        n = sum(p.shape[0] for p in parts)
        put_block(jnp.concatenate(parts + [jnp.zeros((128 - n, w_ref.shape[1]), F32)], axis=0))

    for i in (3, 8, 9, 5, 6, 4, 0, 1):
        piece(i)
    kr0, n = IN_OFFSETS[2], QK_ROPE // 4
    padded([w_ref[kr0:kr0 + QK_ROPE, :]])
    padded([w_ref[kr0 + blk * n:kr0 + (blk + 1) * n, :] for blk in (1, 0, 3, 2)])
    padded([w_ref[IN_OFFSETS[7]:IN_OFFSETS[7] + IN_SPLITS[7], :]])
    o_ref[:, dst:] = jnp.zeros((o_ref.shape[0], o_ref.shape[1] - dst), BF16)


def _regroup_w_in(w_in_t):
    cols = 256
    n_out = (N_LOW_TILES + N_F32_TILES) * IN_TILE
    return pl.pallas_call(
        _regroup_kernel,
        grid=(D_MODEL // cols,),
        in_specs=[pl.BlockSpec((w_in_t.shape[0], cols), lambda i: (0, i))],
        out_specs=pl.BlockSpec((cols, n_out), lambda i: (i, 0)),
        out_shape=jax.ShapeDtypeStruct((D_MODEL, n_out), BF16),
        compiler_params=_params("arbitrary"),
        name="w_in_regroup",
    )(w_in_t)


N_LOW_TILES, N_F32_TILES = 7, 4
Z_BLK, GM_BLK, GS_BLK = 0, 1, 2
BC_TILE = 6
XS_BLK = 0
MLA_TILE, MISC_TILE = 2, 3
DT_BLK = (MISC_TILE * IN_TILE + 256) // 128
ATTN_HEAD_COLS = 256


def _in_kernel(x_ref, sh_ref, sc_ref, g_ref, w_ref, cw_ref, cb_ref, lo_ref, hi_ref, h_scr, u_scr, *, seq_len):
    j = pl.program_id(1)
    chunk = IN_CHUNK
    n_chunks = ROW_GROUP // chunk
    rows = lambda c: slice(c * chunk, (c + 1) * chunk)

    def pointwise(fn, o_ref, stage_h=False):
        for c in range(n_chunks):
            if stage_h:
                h_scr[rows(c), :] = _norm_mod(x_ref[rows(c), :], g_ref, sc_ref, sh_ref)
            o_ref[rows(c), :] = fn(_dot(h_scr[rows(c), :], w_ref[...])).astype(o_ref.dtype)

    def conv(o_ref):
        def conv_out(c):
            v = _silu(_dwconv3_rows(u_scr, c, chunk, slice(0, IN_TILE), cw_ref, cb_ref, seq_len))
            o_ref[rows(c), :] = v.astype(o_ref.dtype)
        for c in range(n_chunks):
            u_scr[_stage_rows(c, chunk), :] = _dot(h_scr[rows(c), :], w_ref[...])
            if c >= 1:
                conv_out(c - 1)
        conv_out(n_chunks - 1)

    @pl.when(j == 0)
    def _():
        _zero_conv_pads(u_scr)
        pointwise(_silu, lo_ref, stage_h=True)

    @pl.when(j == 1)
    def _():
        pointwise(_silu, lo_ref)

    @pl.when((j >= 2) & (j <= 5))
    def _():
        pointwise(_sigmoid, lo_ref)

    @pl.when(j == BC_TILE)
    def _():
        conv(lo_ref)

    @pl.when((j == N_LOW_TILES) | (j == N_LOW_TILES + 1))
    def _():
        conv(hi_ref)

    @pl.when(j >= N_LOW_TILES + MLA_TILE)
    def _():
        pointwise(lambda u: u, hi_ref)


def _in_proj(x2d, mod48, mod_row, norm_g, w_in_r, conv_w, conv_b, seq_len):
    t = x2d.shape[0]
    n_tiles = N_LOW_TILES + N_F32_TILES
    conv_idx = lambda r, j: (0, jnp.where(j == BC_TILE, 2, jnp.clip(j - N_LOW_TILES, 0, 1)))
    return pl.pallas_call(
        functools.partial(_in_kernel, seq_len=seq_len),
        grid=(t // ROW_GROUP, n_tiles),
        in_specs=[pl.BlockSpec((ROW_GROUP, D_MODEL), lambda r, j: (r, 0)),
                  pl.BlockSpec((None, 1, D_MODEL), lambda r, j: (mod_row(r) * 6 + 0, 0, 0)),
                  pl.BlockSpec((None, 1, D_MODEL), lambda r, j: (mod_row(r) * 6 + 1, 0, 0)),
                  pl.BlockSpec((1, D_MODEL), lambda r, j: (0, 0)),
                  pl.BlockSpec((D_MODEL, IN_TILE), lambda r, j: (0, j)),
                  pl.BlockSpec((3, IN_TILE), conv_idx),
                  pl.BlockSpec((1, IN_TILE), conv_idx)],
        out_specs=[pl.BlockSpec((ROW_GROUP, IN_TILE), lambda r, j: (r, jnp.minimum(j, N_LOW_TILES - 1))),
                   pl.BlockSpec((ROW_GROUP, IN_TILE), lambda r, j: (r, jnp.maximum(j - N_LOW_TILES, 0)))],
        out_shape=[jax.ShapeDtypeStruct((t, N_LOW_TILES * IN_TILE), BF16),
                   jax.ShapeDtypeStruct((t, N_F32_TILES * IN_TILE), F32)],
        scratch_shapes=[pltpu.VMEM((ROW_GROUP, D_MODEL), BF16),
                        pltpu.VMEM((ROW_GROUP + 2 * CONV_PAD, IN_TILE), F32)],
        compiler_params=_params("arbitrary", "arbitrary"),
        name="in_proj",
    )(x2d, mod48, mod48, norm_g, w_in_r, conv_w, conv_b)


def _mla_kernel(*refs, latent, has_ctx, emit_cache, seq_len, tq, seqs):
    n_in = 7 + (2 if latent else 0) + (2 if has_ctx else 0)
    n_out = 3 if emit_cache else 1
    per_seq = {0, 1} | ({n_in - 2, n_in - 1} if has_ctx else set())
    for bi in range(seqs):
        view = lambda r: r.at[pl.ds(bi, 1)]
        ins = [view(r) if i in per_seq else r for i, r in enumerate(refs[:n_in])]
        outs = [view(r) for r in refs[n_in:n_in + n_out]]
        _mla_sequence(*ins, *outs, *refs[n_in + n_out:], latent=latent, has_ctx=has_ctx, emit_cache=emit_cache,
                      seq_len=seq_len, tq=tq)


def _mla_sequence(*refs, latent, has_ctx, emit_cache, seq_len, tq):
    refs = list(refs)
    pm_ref, px_ref, qg_ref, kvg_ref, wuq_ref, wkt_ref, wv_ref = refs[:7]
    del refs[:7]
    if latent:
        cos_ref, sin_ref = refs[:2]
        del refs[:2]
    if has_ctx:
        cckv_ref, ckr_ref = refs[:2]
        del refs[:2]
    o_ref = refs.pop(0)
    if emit_cache:
        ckv_ref, kr_ref = refs[:2]
        del refs[:2]
    k_scr, v_scr = refs[:2]
    del refs[:2]
    if has_ctx:
        kc_scr, vc_scr = refs
    t = pl.program_id(1)
    n_nope = N_HEADS * QK_NOPE

    def put_keys(k_dst, v_dst, rows, ckv_n, kr_bf):
        ckv_bf = ckv_n.astype(BF16)
        v_dst[rows, :] = _dot(ckv_bf, wv_ref[...]).astype(BF16)
        kn_t = _dot_nt(wkt_ref[...], ckv_bf).astype(BF16)
        width = kr_bf.shape[1]
        eye = jnp.where(lax.broadcasted_iota(jnp.int32, (128, width), 0)
                        == lax.broadcasted_iota(jnp.int32, (128, width), 1), 1.0, 0.0).astype(BF16)
        kr_t = _dot_nt(eye, kr_bf).astype(BF16)
        for h in range(N_HEADS):
            base = h * ATTN_HEAD_COLS
            k_dst[base:base + QK_NOPE, rows] = kn_t[h * QK_NOPE:(h + 1) * QK_NOPE, :]
            k_dst[base + QK_NOPE:base + ATTN_HEAD_COLS, rows] = kr_t

    @pl.when(t == 0)
    def _():
        step = min(seq_len, TOKEN_TILE)
        for r0 in range(0, seq_len, step):
            rows = slice(r0, r0 + step)
            ckv_n = _rmsnorm(pm_ref[0, rows, Q_LORA:], kvg_ref[...])
            kr = px_ref[0, rows, 0:128]
            if latent:
                kr = kr * cos_ref[rows, :] + px_ref[0, rows, 128:256] * sin_ref[rows, :]
            if emit_cache:
                ckv_ref[0, rows, :] = ckv_n
                kr_ref[0, rows, :] = kr[:, :QK_ROPE]
            put_keys(k_scr, v_scr, rows, ckv_n, kr.astype(BF16))
        if has_ctx:
            past = cckv_ref.shape[1]
            put_keys(kc_scr, vc_scr, slice(0, past), cckv_ref[0], ckr_ref[0].astype(BF16))

    qrows = pl.ds(pl.multiple_of(t * tq, tq), tq)
    scale = LOG2_E / math.sqrt(QK_NOPE + QK_ROPE)
    cqn = _rmsnorm(pm_ref[0, qrows, 0:Q_LORA], qg_ref[...]).astype(BF16)
    q = _dot(cqn, wuq_ref[...])
    q_rope = q[:, n_nope:2 * n_nope]
    if latent:
        q_rope = (q_rope * jnp.concatenate([cos_ref[qrows, :]] * N_HEADS, axis=1)
                  + q[:, 2 * n_nope:3 * n_nope] * jnp.concatenate([sin_ref[qrows, :]] * N_HEADS, axis=1))
    segs = ([(kc_scr, vc_scr)] if has_ctx else []) + [(k_scr, v_scr)]
    for h in range(N_HEADS):
        head = slice(h * QK_NOPE, (h + 1) * QK_NOPE)
        qk = slice(h * ATTN_HEAD_COLS, (h + 1) * ATTN_HEAD_COLS)
        qh = (jnp.concatenate([q[:, head], q_rope[:, head]], axis=1) * scale).astype(BF16)
        s = [_dot(qh, k[qk, :]) for k, _ in segs]
        m = functools.reduce(jnp.maximum, [jnp.max(si, axis=-1, keepdims=True) for si in s])
        p = [jnp.exp2(si - m) for si in s]
        l = functools.reduce(jnp.add, [jnp.sum(pi, axis=-1, keepdims=True) for pi in p])
        o = functools.reduce(jnp.add, [_dot(pi.astype(BF16), v[:, head]) for pi, (_, v) in zip(p, segs)])
        o_ref[0, :, head] = (o / l).astype(BF16)


def _mla(proj_hi, q_norm_g, kv_norm_g, w_uq_r, w_uk_t, w_uv, rope_tables, ctx, emit_cache):
    b, l, _ = proj_hi.shape
    tq = min(l, ATTN_Q_TILE)
    latent = rope_tables is not None
    has_ctx = ctx is not None
    once = dict(pipeline_mode=pl.Buffered(1))
    per_batch = once if l // tq > 1 else {}
    seqs = MLA_SEQS_PER_STEP if (l == tq and b % MLA_SEQS_PER_STEP == 0) else 1
    const2 = lambda i, t: (0, 0)
    in_specs = [pl.BlockSpec((seqs, l, IN_TILE), lambda i, t: (i, 0, MLA_TILE), **per_batch),
                pl.BlockSpec((seqs, l, 256), lambda i, t: (i, 0, MISC_TILE * IN_TILE // 256), **per_batch),
                pl.BlockSpec((1, Q_LORA), const2),
                pl.BlockSpec((1, KV_LORA), const2),
                pl.BlockSpec(w_uq_r.shape, const2, **once),
                pl.BlockSpec(w_uk_t.shape, const2, **once),
                pl.BlockSpec(w_uv.shape, const2, **once)]
    args = [proj_hi, proj_hi, q_norm_g, kv_norm_g, w_uq_r, w_uk_t, w_uv]
    scratch = [pltpu.VMEM((N_HEADS * ATTN_HEAD_COLS, l), BF16), pltpu.VMEM((l, N_HEADS * V_HEAD), BF16)]
    if latent:
        in_specs += [pl.BlockSpec((l, 128), const2, **once)] * 2
        args += list(rope_tables)
    if has_ctx:
        past = ctx[0].shape[1]
        in_specs += [pl.BlockSpec((seqs, past, KV_LORA), lambda i, t: (i, 0, 0), **per_batch),
                     pl.BlockSpec((seqs, past, QK_ROPE), lambda i, t: (i, 0, 0), **per_batch)]
        args += list(ctx)
        scratch += [pltpu.VMEM((N_HEADS * ATTN_HEAD_COLS, past), BF16), pltpu.VMEM((past, N_HEADS * V_HEAD), BF16)]
    out_specs = [pl.BlockSpec((seqs, tq, N_HEADS * V_HEAD), lambda i, t: (i, t, 0))]
    out_shape = [jax.ShapeDtypeStruct((b, l, N_HEADS * V_HEAD), BF16)]
    if emit_cache:
        out_specs += [pl.BlockSpec((seqs, l, KV_LORA), lambda i, t: (i, 0, 0)),
                      pl.BlockSpec((seqs, l, QK_ROPE), lambda i, t: (i, 0, 0))]
        out_shape += [jax.ShapeDtypeStruct((b, l, KV_LORA), F32), jax.ShapeDtypeStruct((b, l, QK_ROPE), F32)]
    return pl.pallas_call(
        functools.partial(_mla_kernel, latent=latent, has_ctx=has_ctx, emit_cache=emit_cache, seq_len=l, tq=tq,
                          seqs=seqs),
        grid=(b // seqs, l // tq),
        in_specs=in_specs,
        out_specs=out_specs,
        out_shape=out_shape,
        scratch_shapes=scratch,
        compiler_params=_params("arbitrary", "arbitrary", vmem_limit=MLA_VMEM_LIMIT),
        name="mla_attention",
    )(*args)


def _split3(x):
    hi = x.astype(BF16)
    r = x - hi.astype(F32)
    mid = r.astype(BF16)
    lo = (r - mid.astype(F32)).astype(BF16)
    return hi, mid, lo


def _exact_dot(parts, sel):
    return functools.reduce(jnp.add, [_dot(p, sel) for p in parts])


def _exact_dot_rows(sel, parts):
    return functools.reduce(jnp.add, [_dot(sel, p) for p in parts])


HEADS_PER_GROUP = SSD_HEADS // SSD_GROUPS
GROUP_COLS = HEADS_PER_GROUP * SSD_HEADDIM


def _ssd_kernel(*refs, nc, cps, has_h0, seqs):
    n_in = 8 if has_h0 else 7
    per_seq = {0, 1, 2, 3, 7} if has_h0 else {0, 1, 2, 3}
    for bi in range(seqs):
        view = lambda r: r.at[pl.ds(bi, 1)]
        ins = [view(r) if i in per_seq else r for i, r in enumerate(refs[:n_in])]
        outs = [view(r) for r in refs[n_in:n_in + 2]]
        _ssd_sequence(*ins, *outs, *refs[n_in + 2:], nc=nc, cps=cps, has_h0=has_h0)


def _ssd_sequence(*refs, nc, cps, has_h0):
    if has_h0:
        (xs_ref, zs_ref, bc_ref, dt_ref, dtb_ref, a_ref, dx_ref, h0_ref, y_ref, ht_ref,
         yl_scr, acum_scr, src_scr, tot_scr, sb_scr, h_scr, esel_scr) = refs
    else:
        (xs_ref, zs_ref, bc_ref, dt_ref, dtb_ref, a_ref, dx_ref, y_ref, ht_ref,
         yl_scr, acum_scr, src_scr, tot_scr, sb_scr, h_scr, esel_scr) = refs
    s = pl.program_id(1)
    q = SSD_CHUNK
    n_bc = SSD_GROUPS * SSD_STATE
    lane = lax.broadcasted_iota(jnp.int32, (q, 128), 1)
    low_half = lane < SSD_HEADDIM
    ii = lax.broadcasted_iota(jnp.int32, (q, q), 0)
    jj = lax.broadcasted_iota(jnp.int32, (q, q), 1)
    lower, upper = ii >= jj, ii <= jj

    def lane_bcast(parts, d):
        return _exact_dot(parts, esel_scr[d])

    def stacked_states(d):
        return [jnp.concatenate([h_scr[d, 2 * i], h_scr[d, 2 * i + 1]], axis=0).astype(BF16)
                for i in range(SSD_GROUPS // 2)]

    def group_c(bc, g):
        i, r = divmod(g, 2)
        cpair = bc[:, n_bc + i * 128:n_bc + (i + 1) * 128]
        return jnp.where(low_half if r == 0 else ~low_half, cpair, 0.0).astype(BF16)

    @pl.when(s == 0)
    def _():
        k = lax.broadcasted_iota(jnp.int32, (128, SSD_INNER), 0)
        head = lax.broadcasted_iota(jnp.int32, (128, SSD_INNER), 1) // SSD_HEADDIM
        for d in range(2):
            esel_scr[d] = jnp.where(k == d * SSD_HEADS + head, 1.0, 0.0).astype(BF16)
        if has_h0:
            for d in range(2):
                for g in range(SSD_GROUPS):
                    hpn = h0_ref[0, d, g * HEADS_PER_GROUP:(g + 1) * HEADS_PER_GROUP].reshape(GROUP_COLS, SSD_STATE)
                    h_scr[d, g] = hpn.T
        else:
            h_scr[...] = jnp.zeros(h_scr.shape, F32)
        tri_f = jnp.where(lower, 1.0, 0.0).astype(BF16)
        tri_b = jnp.where(upper, 1.0, 0.0).astype(BF16)
        fwd_col = lane < SSD_HEADS
        for c in range(nc):
            crow = slice(c * q, (c + 1) * q)
            dt = _softplus(dt_ref[0, crow, :] + dtb_ref[...])
            parts = _split3(dt * a_ref[...])
            acum = jnp.where(fwd_col, _exact_dot_rows(tri_f, parts),
                             _exact_dot_rows(tri_b, parts))
            acum = acum * LOG2_E
            acum_scr[crow, :] = acum
            tot = jnp.where(fwd_col[0:1], acum[q - 1:q, :], acum[0:1, :])
            tot_scr[c] = jnp.broadcast_to(tot, (8, 128))
            src_scr[c] = (acum - jnp.log2(dt)).T

    def first_sweep(c, blk):
        rows = pl.ds(pl.multiple_of(c * q, q), q)
        x = xs_ref[0, blk, :]
        bc = bc_ref[0, blk, :].astype(F32)
        acum = acum_scr[rows, :]
        src_t = src_scr[c]
        tot8 = tot_scr[c]
        e = jnp.exp2(acum)
        e_hi = e.astype(BF16)
        eb_f = lane_bcast([e_hi, (e - e_hi.astype(F32)).astype(BF16)], 0)
        cd_f = jnp.exp2(lane_bcast(_split3(tot8), 0))[0:1]
        b_t = [bc[:, i * 128:(i + 1) * 128].T for i in range(SSD_GROUPS // 2)]
        h_in = stacked_states(0)
        for g in range(SSD_GROUPS):
            i, r = divmod(g, 2)
            cm = group_c(bc, g)
            scores = _dot_nt(cm, bc[:, i * 128:(i + 1) * 128].astype(BF16))
            bg_t = b_t[i][r * SSD_STATE:(r + 1) * SSD_STATE, :]
            gcols = slice(g * GROUP_COLS, (g + 1) * GROUP_COLS)
            y_off = _dot(cm, h_in[i]) * eb_f[:, gcols]
            for t in range(HEADS_PER_GROUP // 2):
                pc = slice((2 * g + t) * 128, (2 * g + t + 1) * 128)
                xp = x[:, pc]
                x2 = jnp.concatenate([jnp.where(low_half, xp, 0.0), jnp.where(low_half, 0.0, xp)], axis=0).astype(BF16)
                m, sf, sb = [], [], []
                for u in range(2):
                    kf = g * HEADS_PER_GROUP + 2 * t + u
                    kb = SSD_HEADS + kf
                    af_col = jnp.broadcast_to(acum[:, kf:kf + 1], (q, q))
                    ab_col = jnp.broadcast_to(acum[:, kb:kb + 1], (q, q))
                    sf_row, sb_row = src_t[kf:kf + 1, :], src_t[kb:kb + 1, :]
                    decay = (jnp.exp2(jnp.where(lower, af_col - sf_row, NEG_BIG))
                             + jnp.exp2(jnp.where(upper, ab_col - sb_row, NEG_BIG)))
                    m.append((scores * decay).astype(BF16))
                    wf = jnp.exp2(tot8[0:1, kf:kf + 1] - sf_row)
                    wb = jnp.exp2(tot8[0:1, kb:kb + 1] - sb_row)
                    sf.append((bg_t * wf).astype(BF16))
                    sb.append((bg_t * wb).astype(BF16))
                y_pair = _dot(jnp.concatenate(m, axis=1), x2)
                tc = slice(t * 128, (t + 1) * 128)
                yl_scr[rows, pc] = y_pair + y_off[:, tc] + dx_ref[:, pc] * xp
                h_scr[0, g, :, tc] = h_scr[0, g, :, tc] * cd_f[:, pc] + _dot(jnp.concatenate(sf, axis=1), x2)
                sb_scr[c, g, :, tc] = _dot(jnp.concatenate(sb, axis=1), x2)

    def second_sweep(c, blk):
        rows = pl.ds(pl.multiple_of(c * q, q), q)
        bc = bc_ref[0, blk, :].astype(F32)
        e = jnp.exp2(acum_scr[rows, :])
        e_hi = e.astype(BF16)
        eb_b = lane_bcast([e_hi, (e - e_hi.astype(F32)).astype(BF16)], 1)
        cd_b = jnp.exp2(lane_bcast(_split3(tot_scr[c]), 1))[0:1]
        h_in = stacked_states(1)
        for g in range(SSD_GROUPS):
            gcols = slice(g * GROUP_COLS, (g + 1) * GROUP_COLS)
            y_off = _dot(group_c(bc, g), h_in[g // 2]) * eb_b[:, gcols]
            y_ref[0, blk, gcols] = ((yl_scr[rows, gcols] + y_off) * zs_ref[0, blk, gcols].astype(F32)).astype(y_ref.dtype)
            h_scr[1, g] = h_scr[1, g] * cd_b[:, gcols] + sb_scr[c, g]

    n_steps = nc // cps
    one_step = n_steps == 1

    @pl.when(s < n_steps)
    def _():
        for ci in range(cps):
            first_sweep(s * cps + ci, slice(ci * q, (ci + 1) * q))

    @pl.when(jnp.logical_or(one_step, s >= n_steps))
    def _():
        blk_id = s * 0 if one_step else 2 * n_steps - 1 - s
        for ci in reversed(range(cps)):
            second_sweep(blk_id * cps + ci, slice(ci * q, (ci + 1) * q))

    @pl.when(jnp.logical_or(one_step, s == 2 * n_steps - 1))
    def _():
        for d in range(2):
            for g in range(SSD_GROUPS):
                ht_ref[0, d, g * HEADS_PER_GROUP:(g + 1) * HEADS_PER_GROUP] = h_scr[d, g].T.reshape(
                    HEADS_PER_GROUP, SSD_HEADDIM, SSD_STATE)


def _ssd(proj_lo, proj_hi, h0, dt_bias128, a128, d_exp):
    b, l, _ = proj_hi.shape
    q = SSD_CHUNK
    nc = l // q
    cps = min(nc, SSD_CHUNKS_PER_STEP)
    n_steps = nc // cps
    n_grid = 1 if n_steps == 1 else 2 * n_steps
    rows = cps * q
    early = lambda s: jnp.minimum(s, n_steps - 1)
    both = lambda s: jnp.where(s < n_steps, s, 2 * n_steps - 1 - s)
    late = lambda s: jnp.where(s < n_steps, n_steps - 1, 2 * n_steps - 1 - s)
    seqs = SSD_SEQS_PER_STEP if (n_steps == 1 and b % SSD_SEQS_PER_STEP == 0) else 1
    st_shape = (seqs, 2, SSD_HEADS, SSD_HEADDIM, SSD_STATE)
    st_spec = pl.BlockSpec(st_shape, lambda i, s: (i, 0, 0, 0, 0))
    has_h0 = h0 is not None
    return pl.pallas_call(
        functools.partial(_ssd_kernel, nc=nc, cps=cps, has_h0=has_h0, seqs=seqs),
        grid=(b // seqs, n_grid),
        in_specs=[pl.BlockSpec((seqs, rows, SSD_INNER), lambda i, s: (i, early(s), XS_BLK)),
                  pl.BlockSpec((seqs, rows, SSD_INNER), lambda i, s: (i, late(s), Z_BLK)),
                  pl.BlockSpec((seqs, rows, IN_TILE), lambda i, s: (i, both(s), BC_TILE)),
                  pl.BlockSpec((seqs, l, 128), lambda i, s: (i, 0, DT_BLK)),
                  pl.BlockSpec((1, 128), lambda i, s: (0, 0)),
                  pl.BlockSpec((1, 128), lambda i, s: (0, 0)),
                  pl.BlockSpec((1, SSD_INNER), lambda i, s: (0, 0))] + ([st_spec] if has_h0 else []),
        out_specs=[pl.BlockSpec((seqs, rows, SSD_INNER), lambda i, s: (i, late(s), 0)),
                   pl.BlockSpec(st_shape, lambda i, s: (i, 0, 0, 0, 0))],
        out_shape=[jax.ShapeDtypeStruct((b, l, SSD_INNER), BF16),
                   jax.ShapeDtypeStruct((b,) + st_shape[1:], F32)],
        scratch_shapes=[pltpu.VMEM((l, SSD_INNER), F32),
                        pltpu.VMEM((l, 128), F32),
                        pltpu.VMEM((nc, 128, q), F32),
                        pltpu.VMEM((nc, 8, 128), F32),
                        pltpu.VMEM((nc, SSD_GROUPS, SSD_STATE, GROUP_COLS), F32),
                        pltpu.VMEM((2, SSD_GROUPS, SSD_STATE, GROUP_COLS), F32),
                        pltpu.VMEM((2, 128, SSD_INNER), BF16)],
        compiler_params=_params("arbitrary", "arbitrary"),
        name="ssd_scan",
    )(proj_hi, proj_lo, proj_lo, proj_hi, dt_bias128, a128, d_exp, *([h0] if has_h0 else []))


def _merge_kernel(attn_ref, yz_ref, gm_ref, gs_ref, x_ref, g1_ref, ng_ref, womla_ref, wossd_ref, wout_ref, o_ref, w_scr):
    @pl.when(pl.program_id(0) == 0)
    def _():
        w_scr[0] = womla_ref[...].astype(BF16)
        w_scr[1] = wossd_ref[...].astype(BF16)
        w_scr[2] = wout_ref[...].astype(BF16)

    o_mla = _dot(attn_ref[...], w_scr[0])
    o_ssd = _dot(_rmsnorm(yz_ref[...].astype(F32), ng_ref[...]).astype(BF16), w_scr[1])
    merged = gm_ref[...].astype(F32) * o_mla + gs_ref[...].astype(F32) * o_ssd
    o_ref[...] = x_ref[...] + g1_ref[...] * _dot(merged.astype(BF16), w_scr[2])


def _merge(attn2d, yz2d, proj, x2d, mod48, mod_row, ssd_norm_g, w_o_mla, w_o_ssd, w_out):
    t = x2d.shape[0]
    tm = TOKEN_TILE
    row = lambda i: (i, 0)
    const = lambda i: (0, 0)
    wspec = pl.BlockSpec((D_MODEL, D_MODEL), const, pipeline_mode=pl.Buffered(1))
    return pl.pallas_call(
        _merge_kernel,
        grid=(t // tm,),
        in_specs=[pl.BlockSpec((tm, D_MODEL), row),
                  pl.BlockSpec((tm, D_MODEL), row),
                  pl.BlockSpec((tm, D_MODEL), lambda i: (i, GM_BLK)),
                  pl.BlockSpec((tm, D_MODEL), lambda i: (i, GS_BLK)),
                  pl.BlockSpec((tm, D_MODEL), row),
                  pl.BlockSpec((None, 1, D_MODEL), lambda i: (mod_row(i * tm // ROW_GROUP) * 6 + 2, 0, 0)),
                  pl.BlockSpec((1, D_MODEL), const),
                  wspec, wspec, wspec],
        out_specs=pl.BlockSpec((tm, D_MODEL), row),
        out_shape=jax.ShapeDtypeStruct((t, D_MODEL), F32),
        scratch_shapes=[pltpu.VMEM((3, D_MODEL, D_MODEL), BF16)],
        compiler_params=_params("arbitrary"),
        name="merge_out",
    )(attn2d, yz2d, proj, proj, x2d, mod48, ssd_norm_g, w_o_mla, w_o_ssd, w_out)


def _ffn_kernel(x_ref, sh_ref, sc_ref, g2_ref, ng_ref, wg_ref, wv_ref, cwg_ref, cwv_ref, cbg_ref, cbv_ref, wd_ref,
                fg_ref, o_ref, h_scr, wup_scr, wd_scr, u_scr, *, seq_len):
    j = pl.program_id(1)
    chunk = FFN_CHUNK
    n_chunks = ROW_GROUP // chunk
    rows = lambda c: slice(c * chunk, (c + 1) * chunk)

    wup_scr[:, 0:FFN_TILE] = wg_ref[...].astype(BF16)
    wup_scr[:, FFN_TILE:2 * FFN_TILE] = wv_ref[...].astype(BF16)
    wd_scr[...] = wd_ref[...].astype(BF16)

    def gated(c):
        ug = _dwconv3_rows(u_scr, c, chunk, slice(0, FFN_TILE), cwg_ref, cbg_ref, seq_len)
        uv = _dwconv3_rows(u_scr, c, chunk, slice(FFN_TILE, 2 * FFN_TILE), cwv_ref, cbv_ref, seq_len)
        return (_silu(ug) * uv).astype(BF16)

    def pipeline(first, last):
        act = {}
        lag = FFN_DOWN_LAG
        for c in range(n_chunks + lag):
            if c < n_chunks:
                if first:
                    h_scr[rows(c), :] = _norm_mod(x_ref[rows(c), :], ng_ref, sc_ref, sh_ref)
                u_scr[_stage_rows(c, chunk), :] = _dot(h_scr[rows(c), :], wup_scr[...])
            if c >= lag:
                r = rows(c - lag)
                acc = _dot(act.pop(c - lag), wd_scr[...])
                if not first:
                    acc = o_ref[r, :] + acc
                if last:
                    acc = _rmsnorm(x_ref[r, :] + g2_ref[...] * acc, fg_ref[...])
                o_ref[r, :] = acc
            if 1 <= c <= n_chunks:
                act[c - 1] = gated(c - 1)

    last_j = pl.num_programs(1) - 1

    @pl.when(j == 0)
    def _():
        _zero_conv_pads(u_scr)
        pipeline(True, False)

    @pl.when((j > 0) & (j < last_j))
    def _():
        pipeline(False, False)

    @pl.when(j == last_j)
    def _():
        pipeline(False, True)


def _ffn(x2d, mod48, mod_row, norm_g, w_up, conv_w, conv_b, w_down, final_g, seq_len):
    t = x2d.shape[0]
    nj = D_FF // FFN_TILE
    gate = lambda r, j: (0, j)
    val = lambda r, j: (0, nj + j)
    const = lambda r, j: (0, 0)
    mod = lambda k: pl.BlockSpec((None, 1, D_MODEL), lambda r, j: (mod_row(r) * 6 + k, 0, 0))
    return pl.pallas_call(
        functools.partial(_ffn_kernel, seq_len=seq_len),
        grid=(t // ROW_GROUP, nj),
        in_specs=[pl.BlockSpec((ROW_GROUP, D_MODEL), lambda r, j: (r, 0)),
                  mod(3), mod(4), mod(5),
                  pl.BlockSpec((1, D_MODEL), const),
                  pl.BlockSpec((D_MODEL, FFN_TILE), gate),
                  pl.BlockSpec((D_MODEL, FFN_TILE), val),
                  pl.BlockSpec((3, FFN_TILE), gate),
                  pl.BlockSpec((3, FFN_TILE), val),
                  pl.BlockSpec((1, FFN_TILE), gate),
                  pl.BlockSpec((1, FFN_TILE), val),
                  pl.BlockSpec((FFN_TILE, D_MODEL), lambda r, j: (j, 0)),
                  pl.BlockSpec((1, D_MODEL), const)],
        out_specs=pl.BlockSpec((ROW_GROUP, D_MODEL), lambda r, j: (r, 0)),
        out_shape=jax.ShapeDtypeStruct((t, D_MODEL), F32),
        scratch_shapes=[pltpu.VMEM((ROW_GROUP, D_MODEL), BF16),
                        pltpu.VMEM((D_MODEL, 2 * FFN_TILE), BF16),
                        pltpu.VMEM((FFN_TILE, D_MODEL), BF16),
                        pltpu.VMEM((ROW_GROUP + 2 * CONV_PAD, 2 * FFN_TILE), F32)],
        compiler_params=_params("arbitrary", "arbitrary"),
        name="conv_ffn",
    )(x2d, mod48, mod48, mod48, norm_g, w_up, w_up, conv_w, conv_w, conv_b, conv_b, w_down, final_g)


def _rope_tables(seq_len):
    t = np.arange(seq_len)
    row = (t // GRID_W).astype(np.float32)
    col = (t % GRID_W).astype(np.float32)
    n = QK_ROPE // 4
    inv = (np.float32(ROPE_BASE) ** (-np.arange(n, dtype=np.float32) / np.float32(n))).astype(np.float32)
    ar, ac = row[:, None] * inv, col[:, None] * inv
    cos64 = np.concatenate([np.cos(ar), np.cos(ar), np.cos(ac), np.cos(ac)], axis=1)
    sin64 = np.concatenate([-np.sin(ar), np.sin(ar), -np.sin(ac), np.sin(ac)], axis=1)
    zeros = np.zeros_like(cos64)
    return (jnp.asarray(np.concatenate([cos64, zeros], axis=1), F32),
            jnp.asarray(np.concatenate([sin64, zeros], axis=1), F32))


def _swap_rope_halves(w):
    lead = w.shape[:-1]
    return w.reshape(lead + (2, 2, QK_ROPE // 4))[..., ::-1, :].reshape(lead + (QK_ROPE,))


def _trunk_pass(x, mod48, mod_row, wts, ctx, latent):
    b, l, _ = x.shape
    x2d = x.reshape(b * l, D_MODEL)
    proj_lo, proj_hi = _in_proj(x2d, mod48, mod_row, wts["norm_attn_g"], wts["w_in_r"], wts["ssd_conv_w"],
                                wts["ssd_conv_b"], l)
    rope = _rope_tables(l) if latent else None
    w_uq_r = wts["w_uq_lat"] if latent else wts["w_uq_ctx"]
    shape3 = lambda a, n: a.reshape(b, n, a.shape[-1])
    h0 = None
    mla_ctx = None
    if ctx is not None:
        cache_ckv, cache_krope, h0 = ctx
        mla_ctx = (cache_ckv, cache_krope)
    emit_cache = ctx is None
    mla_out = _mla(shape3(proj_hi, l), wts["q_norm_g"], wts["kv_norm_g"], w_uq_r, wts["w_uk_t"], wts["w_uv"],
                   rope, mla_ctx, emit_cache)
    attn, ckv_n, kr3 = mla_out if emit_cache else (mla_out[0], None, None)
    yz, h_t = _ssd(shape3(proj_lo, l), shape3(proj_hi, l), h0, wts["dt_bias128"], wts["a128"], wts["d_exp"])
    x1 = _merge(attn.reshape(b * l, -1), yz.reshape(b * l, -1), proj_lo, x2d, mod48, mod_row, wts["ssd_norm_g"],
                wts["w_o_mla"], wts["w_o_ssd"], wts["w_out"])
    y = _ffn(x1, mod48, mod_row, wts["norm_ffn_g"], wts["w_up"], wts["ffn_conv_w"], wts["ffn_conv_b"], wts["w_down"],
             wts["final_norm_g"], l)
    return y.reshape(b, l, D_MODEL), ckv_n, kr3, h_t


def kernel(x_prompt, x_sample, c, cache_ckv, cache_krope, state_ssd, c_ctx, w_ada, b_ada, norm_attn_g, w_in, q_norm_g,
           kv_norm_g, w_uq, w_ukv, w_o_mla, ssd_conv_w, ssd_conv_b, ssd_dt_bias, ssd_A_log, ssd_D, ssd_norm_g, w_o_ssd,
           w_out, norm_ffn_g, w_up, ffn_conv_w, ffn_conv_b, w_down, final_norm_g):
    depth = w_in.shape[0]
    assert depth == 1, "single trunk layer"
    dec_b = x_sample.shape[0]
    assert x_sample.shape[1] == ROW_GROUP and ROW_GROUP % x_prompt.shape[1] == 0
    lyr = 0

    cvec = jnp.zeros((8, D_MODEL), F32).at[0].set(c_ctx).at[1:1 + dec_b].set(c)
    mod48 = _ada(cvec, w_ada[lyr], b_ada[lyr]).reshape(8 * 6, 1, D_MODEL)

    w_in_r = _regroup_w_in(w_in[lyr].T)
    wq = w_uq[lyr].reshape(Q_LORA, N_HEADS, QK_NOPE + QK_ROPE)
    wq_nope = wq[:, :, :QK_NOPE].reshape(Q_LORA, -1)
    wq_rope = wq[:, :, QK_NOPE:]
    pad_rope = lambda w: jnp.pad(w, ((0, 0), (0, 0), (0, 128 - QK_ROPE))).reshape(Q_LORA, -1)
    w_uq_ctx = jnp.concatenate([wq_nope, pad_rope(wq_rope)], axis=1).astype(BF16)
    w_uq_lat = jnp.concatenate([wq_nope, pad_rope(wq_rope), pad_rope(_swap_rope_halves(wq_rope))], axis=1).astype(BF16)
    wkv = w_ukv[lyr].reshape(KV_LORA, N_HEADS, QK_NOPE + V_HEAD)
    w_uk_t = wkv[:, :, :QK_NOPE].reshape(KV_LORA, -1).T.astype(BF16)
    w_uv = wkv[:, :, QK_NOPE:].reshape(KV_LORA, -1).astype(BF16)
    pad128 = lambda a: jnp.pad(a.reshape(1, -1), ((0, 0), (0, 128 - a.size)))
    wts = {
        "norm_attn_g": norm_attn_g[lyr].reshape(1, -1), "w_in_r": w_in_r,
        "ssd_conv_w": ssd_conv_w[lyr], "ssd_conv_b": ssd_conv_b[lyr].reshape(1, -1),
        "q_norm_g": q_norm_g[lyr].reshape(1, -1), "kv_norm_g": kv_norm_g[lyr].reshape(1, -1),
        "w_uq_ctx": w_uq_ctx, "w_uq_lat": w_uq_lat, "w_uk_t": w_uk_t, "w_uv": w_uv,
        "dt_bias128": pad128(ssd_dt_bias[lyr]), "a128": pad128(-jnp.exp(ssd_A_log[lyr])),
        "d_exp": jnp.repeat(ssd_D[lyr], SSD_HEADDIM).reshape(1, -1),
        "ssd_norm_g": ssd_norm_g[lyr].reshape(1, -1),
        "w_o_mla": w_o_mla[lyr], "w_o_ssd": w_o_ssd[lyr], "w_out": w_out[lyr],
        "norm_ffn_g": norm_ffn_g[lyr].reshape(1, -1), "w_up": w_up[lyr],
        "ffn_conv_w": ffn_conv_w[lyr], "ffn_conv_b": ffn_conv_b[lyr].reshape(1, -1),
        "w_down": w_down[lyr], "final_norm_g": final_norm_g.reshape(1, -1),
    }

    y_p, ckv_p, kr_p, st_p = _trunk_pass(x_prompt, mod48, lambda r: 0, wts, None, False)
    ctx = (cache_ckv[:, lyr], cache_krope[:, lyr], state_ssd[:, lyr])
    y_s, _, _, _ = _trunk_pass(x_sample, mod48, lambda r: 1 + r, wts, ctx, True)
    return y_p, y_s, ckv_p[:, None], kr_p[:, None], st_p[:, None]
```

```python
import functools
import math

import jax
import jax.numpy as jnp
import numpy as np
from jax import lax
from jax.experimental import pallas as pl
from jax.experimental.pallas import tpu as pltpu

F32 = jnp.float32
BF16 = jnp.bfloat16

D_MODEL = 1024
GRID_W = 64
N_HEADS = 8
QK_NOPE = 128
QK_ROPE = 64
V_HEAD = 128
Q_LORA = 256
KV_LORA = 256
ROPE_BASE = 10000.0
SSD_HEADS = 16
SSD_HEADDIM = 64
SSD_INNER = SSD_HEADS * SSD_HEADDIM
SSD_GROUPS = 4
SSD_STATE = 64
SSD_CHUNK = 128
D_FF = 2816
EPS = 1e-6

ROW_GROUP = 2048
IN_TILE = 512
SSD_CHUNKS_PER_STEP = 8
SSD_SEQS_PER_STEP = 2
MLA_SEQS_PER_STEP = 4
FFN_TILE = 256
TOKEN_TILE = 512
ATTN_Q_TILE = 512
IN_CHUNK = 256
FFN_CHUNK = 512
FFN_DOWN_LAG = 2
CONV_PAD = 8
VMEM_LIMIT = 56 * 1024 * 1024
MLA_VMEM_LIMIT = 62 * 1024 * 1024
NEG_BIG = -1e30
LOG2_E = 1.4426950408889634


def _sigmoid(x):
    return 1.0 / (1.0 + jnp.exp(-x))


def _silu(x):
    return x * _sigmoid(x)


def _softplus(x):
    e = jnp.exp(-jnp.abs(x))
    u = 1.0 + e
    log1p_e = jnp.where(u == 1.0, e, e * jnp.log(u) / jnp.where(u == 1.0, 1.0, u - 1.0))
    return jnp.maximum(x, 0.0) + log1p_e


def _rmsnorm(x, g):
    return x * lax.rsqrt(jnp.mean(x * x, axis=-1, keepdims=True) + EPS) * g


def _dot(a, b):
    return jnp.dot(a, b, preferred_element_type=F32)


def _dot_nt(a, b):
    return lax.dot_general(a, b, (((1,), (1,)), ((), ())), preferred_element_type=F32)


def _params(*sem, vmem_limit=VMEM_LIMIT):
    return pltpu.CompilerParams(dimension_semantics=sem, vmem_limit_bytes=vmem_limit)


def _norm_mod(x, g_ref, sc_ref, sh_ref):
    return (_rmsnorm(x, g_ref[...]) * (1.0 + sc_ref[...]) + sh_ref[...]).astype(BF16)


def _zero_conv_pads(u_scr):
    zeros = jnp.zeros((CONV_PAD, u_scr.shape[1]), F32)
    u_scr[0:CONV_PAD, :] = zeros
    u_scr[CONV_PAD + ROW_GROUP:2 * CONV_PAD + ROW_GROUP, :] = zeros


def _stage_rows(c, chunk):
    return slice(CONV_PAD + c * chunk, CONV_PAD + (c + 1) * chunk)


def _dwconv3_rows(u_scr, c, chunk, cols, w_ref, b_ref, seq_len):
    r0 = c * chunk
    base = CONV_PAD + r0
    width = cols.stop - cols.start
    prev = u_scr[base - 1:base - 1 + chunk, cols]
    cur = u_scr[base:base + chunk, cols]
    nxt = u_scr[base + 1:base + 1 + chunk, cols]
    pos = (lax.broadcasted_iota(jnp.int32, (chunk, width), 0) + r0) & (seq_len - 1)
    if r0 % seq_len == 0 or chunk > seq_len:
        prev = jnp.where(pos == 0, 0.0, prev)
    if (r0 + chunk) % seq_len == 0 or chunk > seq_len:
        nxt = jnp.where(pos == seq_len - 1, 0.0, nxt)
    return prev * w_ref[0:1, :] + cur * w_ref[1:2, :] + nxt * w_ref[2:3, :] + b_ref[...]


def _ada_kernel(c_ref, w_ref, b_ref, o_ref):
    a = _silu(c_ref[...]).astype(BF16)
    o_ref[...] = _dot(a, w_ref[...].astype(BF16)) + b_ref[...]


def _ada(cvec, w_ada, b_ada):
    tn = 1536
    return pl.pallas_call(
        _ada_kernel,
        grid=(6 * D_MODEL // tn,),
        in_specs=[pl.BlockSpec((8, D_MODEL), lambda j: (0, 0)),
                  pl.BlockSpec((D_MODEL, tn), lambda j: (0, j)),
                  pl.BlockSpec((1, tn), lambda j: (0, j))],
        out_specs=pl.BlockSpec((8, tn), lambda j: (0, j)),
        out_shape=jax.ShapeDtypeStruct((8, 6 * D_MODEL), F32),
        compiler_params=_params("arbitrary"),
        name="ada_mod",
    )(cvec, w_ada, b_ada.reshape(1, -1))


IN_SPLITS = (Q_LORA, KV_LORA, QK_ROPE, SSD_INNER, SSD_INNER, SSD_GROUPS * SSD_STATE, SSD_GROUPS * SSD_STATE,
             2 * SSD_HEADS, D_MODEL, D_MODEL)
IN_OFFSETS = tuple(int(v) for v in np.cumsum((0,) + IN_SPLITS))


def _regroup_kernel(w_ref, o_ref):
    dst = 0

    def put_block(block):
        nonlocal dst
        o_ref[:, dst:dst + 128] = block.T.astype(BF16)
        dst += 128

    def piece(i):
        for off in range(0, IN_SPLITS[i], 128):
            put_block(w_ref[IN_OFFSETS[i] + off:IN_OFFSETS[i] + off + 128, :])

    def padded(parts):
        n = sum(p.shape[0] for p in parts)
        put_block(jnp.concatenate(parts + [jnp.zeros((128 - n, w_ref.shape[1]), F32)], axis=0))

    for i in (3, 8, 9, 5, 6, 4, 0, 1):
        piece(i)
    kr0, n = IN_OFFSETS[2], QK_ROPE // 4
    padded([w_ref[kr0:kr0 + QK_ROPE, :]])
    padded([w_ref[kr0 + blk * n:kr0 + (blk + 1) * n, :] for blk in (1, 0, 3, 2)])
    padded([w_ref[IN_OFFSETS[7]:IN_OFFSETS[7] + IN_SPLITS[7], :]])
    o_ref[:, dst:] = jnp.zeros((o_ref.shape[0], o_ref.shape[1] - dst), BF16)


def _regroup_w_in(w_in_t):
    cols = 256
    n_out = (N_LOW_TILES + N_F32_TILES) * IN_TILE
    return pl.pallas_call(
        _regroup_kernel,
        grid=(D_MODEL // cols,),
        in_specs=[pl.BlockSpec((w_in_t.shape[0], cols), lambda i: (0, i))],
        out_specs=pl.BlockSpec((cols, n_out), lambda i: (i, 0)),
        out_shape=jax.ShapeDtypeStruct((D_MODEL, n_out), BF16),
        compiler_params=_params("arbitrary"),
        name="w_in_regroup",
    )(w_in_t)


N_LOW_TILES, N_F32_TILES = 7, 4
Z_BLK, GM_BLK, GS_BLK = 0, 1, 2
BC_TILE = 6
XS_BLK = 0
MLA_TILE, MISC_TILE = 2, 3
DT_BLK = (MISC_TILE * IN_TILE + 256) // 128
ATTN_HEAD_COLS = 256


def _in_kernel(x_ref, sh_ref, sc_ref, g_ref, w_ref, cw_ref, cb_ref, lo_ref, hi_ref, h_scr, u_scr, *, seq_len):
    j = pl.program_id(1)
    chunk = IN_CHUNK
    n_chunks = ROW_GROUP // chunk
    rows = lambda c: slice(c * chunk, (c + 1) * chunk)

    def pointwise(fn, o_ref, stage_h=False):
        for c in range(n_chunks):
            if stage_h:
                h_scr[rows(c), :] = _norm_mod(x_ref[rows(c), :], g_ref, sc_ref, sh_ref)
            o_ref[rows(c), :] = fn(_dot(h_scr[rows(c), :], w_ref[...])).astype(o_ref.dtype)

    def conv(o_ref):
        def conv_out(c):
            v = _silu(_dwconv3_rows(u_scr, c, chunk, slice(0, IN_TILE), cw_ref, cb_ref, seq_len))
            o_ref[rows(c), :] = v.astype(o_ref.dtype)
        for c in range(n_chunks):
            u_scr[_stage_rows(c, chunk), :] = _dot(h_scr[rows(c), :], w_ref[...])
            if c >= 1:
                conv_out(c - 1)
        conv_out(n_chunks - 1)

    @pl.when(j == 0)
    def _():
        _zero_conv_pads(u_scr)
        pointwise(_silu, lo_ref, stage_h=True)

    @pl.when(j == 1)
    def _():
        pointwise(_silu, lo_ref)

    @pl.when((j >= 2) & (j <= 5))
    def _():
        pointwise(_sigmoid, lo_ref)

    @pl.when(j == BC_TILE)
    def _():
        conv(lo_ref)

    @pl.when((j == N_LOW_TILES) | (j == N_LOW_TILES + 1))
    def _():
        conv(hi_ref)

    @pl.when(j >= N_LOW_TILES + MLA_TILE)
    def _():
        pointwise(lambda u: u, hi_ref)


def _in_proj(x2d, mod48, mod_row, norm_g, w_in_r, conv_w, conv_b, seq_len):
    t = x2d.shape[0]
    n_tiles = N_LOW_TILES + N_F32_TILES
    conv_idx = lambda r, j: (0, jnp.where(j == BC_TILE, 2, jnp.clip(j - N_LOW_TILES, 0, 1)))
    return pl.pallas_call(
        functools.partial(_in_kernel, seq_len=seq_len),
        grid=(t // ROW_GROUP, n_tiles),
        in_specs=[pl.BlockSpec((ROW_GROUP, D_MODEL), lambda r, j: (r, 0)),
                  pl.BlockSpec((None, 1, D_MODEL), lambda r, j: (mod_row(r) * 6 + 0, 0, 0)),
                  pl.BlockSpec((None, 1, D_MODEL), lambda r, j: (mod_row(r) * 6 + 1, 0, 0)),
                  pl.BlockSpec((1, D_MODEL), lambda r, j: (0, 0)),
                  pl.BlockSpec((D_MODEL, IN_TILE), lambda r, j: (0, j)),
                  pl.BlockSpec((3, IN_TILE), conv_idx),
                  pl.BlockSpec((1, IN_TILE), conv_idx)],
        out_specs=[pl.BlockSpec((ROW_GROUP, IN_TILE), lambda r, j: (r, jnp.minimum(j, N_LOW_TILES - 1))),
                   pl.BlockSpec((ROW_GROUP, IN_TILE), lambda r, j: (r, jnp.maximum(j - N_LOW_TILES, 0)))],
        out_shape=[jax.ShapeDtypeStruct((t, N_LOW_TILES * IN_TILE), BF16),
                   jax.ShapeDtypeStruct((t, N_F32_TILES * IN_TILE), F32)],
        scratch_shapes=[pltpu.VMEM((ROW_GROUP, D_MODEL), BF16),
                        pltpu.VMEM((ROW_GROUP + 2 * CONV_PAD, IN_TILE), F32)],
        compiler_params=_params("arbitrary", "arbitrary"),
        name="in_proj",
    )(x2d, mod48, mod48, norm_g, w_in_r, conv_w, conv_b)


def _mla_kernel(*refs, latent, has_ctx, emit_cache, seq_len, tq, seqs):
    n_in = 7 + (2 if latent else 0) + (2 if has_ctx else 0)
    n_out = 3 if emit_cache else 1
    per_seq = {0, 1} | ({n_in - 2, n_in - 1} if has_ctx else set())
    for bi in range(seqs):
        view = lambda r: r.at[pl.ds(bi, 1)]
        ins = [view(r) if i in per_seq else r for i, r in enumerate(refs[:n_in])]
        outs = [view(r) for r in refs[n_in:n_in + n_out]]
        _mla_sequence(*ins, *outs, *refs[n_in + n_out:], latent=latent, has_ctx=has_ctx, emit_cache=emit_cache,
                      seq_len=seq_len, tq=tq)


def _mla_sequence(*refs, latent, has_ctx, emit_cache, seq_len, tq):
    refs = list(refs)
    pm_ref, px_ref, qg_ref, kvg_ref, wuq_ref, wkt_ref, wv_ref = refs[:7]
    del refs[:7]
    if latent:
        cos_ref, sin_ref = refs[:2]
        del refs[:2]
    if has_ctx:
        cckv_ref, ckr_ref = refs[:2]
        del refs[:2]
    o_ref = refs.pop(0)
    if emit_cache:
        ckv_ref, kr_ref = refs[:2]
        del refs[:2]
    k_scr, v_scr = refs[:2]
    del refs[:2]
    if has_ctx:
        kc_scr, vc_scr = refs
    t = pl.program_id(1)
    n_nope = N_HEADS * QK_NOPE

    def put_keys(k_dst, v_dst, rows, ckv_n, kr_bf):
        ckv_bf = ckv_n.astype(BF16)
        v_dst[rows, :] = _dot(ckv_bf, wv_ref[...]).astype(BF16)
        kn_t = _dot_nt(wkt_ref[...], ckv_bf).astype(BF16)
        width = kr_bf.shape[1]
        eye = jnp.where(lax.broadcasted_iota(jnp.int32, (128, width), 0)
                        == lax.broadcasted_iota(jnp.int32, (128, width), 1), 1.0, 0.0).astype(BF16)
        kr_t = _dot_nt(eye, kr_bf).astype(BF16)
        for h in range(N_HEADS):
            base = h * ATTN_HEAD_COLS
            k_dst[base:base + QK_NOPE, rows] = kn_t[h * QK_NOPE:(h + 1) * QK_NOPE, :]
            k_dst[base + QK_NOPE:base + ATTN_HEAD_COLS, rows] = kr_t

    @pl.when(t == 0)
    def _():
        step = min(seq_len, TOKEN_TILE)
        for r0 in range(0, seq_len, step):
            rows = slice(r0, r0 + step)
            ckv_n = _rmsnorm(pm_ref[0, rows, Q_LORA:], kvg_ref[...])
            kr = px_ref[0, rows, 0:128]
            if latent:
                kr = kr * cos_ref[rows, :] + px_ref[0, rows, 128:256] * sin_ref[rows, :]
            if emit_cache:
                ckv_ref[0, rows, :] = ckv_n
                kr_ref[0, rows, :] = kr[:, :QK_ROPE]
            put_keys(k_scr, v_scr, rows, ckv_n, kr.astype(BF16))
        if has_ctx:
            past = cckv_ref.shape[1]
            put_keys(kc_scr, vc_scr, slice(0, past), cckv_ref[0], ckr_ref[0].astype(BF16))

    qrows = pl.ds(pl.multiple_of(t * tq, tq), tq)
    scale = LOG2_E / math.sqrt(QK_NOPE + QK_ROPE)
    cqn = _rmsnorm(pm_ref[0, qrows, 0:Q_LORA], qg_ref[...]).astype(BF16)
    q = _dot(cqn, wuq_ref[...])
    q_rope = q[:, n_nope:2 * n_nope]
    if latent:
        q_rope = (q_rope * jnp.concatenate([cos_ref[qrows, :]] * N_HEADS, axis=1)
                  + q[:, 2 * n_nope:3 * n_nope] * jnp.concatenate([sin_ref[qrows, :]] * N_HEADS, axis=1))
    segs = ([(kc_scr, vc_scr)] if has_ctx else []) + [(k_scr, v_scr)]
    for h in range(N_HEADS):
        head = slice(h * QK_NOPE, (h + 1) * QK_NOPE)
        qk = slice(h * ATTN_HEAD_COLS, (h + 1) * ATTN_HEAD_COLS)
        qh = (jnp.concatenate([q[:, head], q_rope[:, head]], axis=1) * scale).astype(BF16)
        s = [_dot(qh, k[qk, :]) for k, _ in segs]
        m = functools.reduce(jnp.maximum, [jnp.max(si, axis=-1, keepdims=True) for si in s])
        p = [jnp.exp2(si - m) for si in s]
        l = functools.reduce(jnp.add, [jnp.sum(pi, axis=-1, keepdims=True) for pi in p])
        o = functools.reduce(jnp.add, [_dot(pi.astype(BF16), v[:, head]) for pi, (_, v) in zip(p, segs)])
        o_ref[0, :, head] = (o / l).astype(BF16)


def _mla(proj_hi, q_norm_g, kv_norm_g, w_uq_r, w_uk_t, w_uv, rope_tables, ctx, emit_cache):
    b, l, _ = proj_hi.shape
    tq = min(l, ATTN_Q_TILE)
    latent = rope_tables is not None
    has_ctx = ctx is not None
    once = dict(pipeline_mode=pl.Buffered(1))
    per_batch = once if l // tq > 1 else {}
    seqs = MLA_SEQS_PER_STEP if (l == tq and b % MLA_SEQS_PER_STEP == 0) else 1
    const2 = lambda i, t: (0, 0)
    in_specs = [pl.BlockSpec((seqs, l, IN_TILE), lambda i, t: (i, 0, MLA_TILE), **per_batch),
                pl.BlockSpec((seqs, l, 256), lambda i, t: (i, 0, MISC_TILE * IN_TILE // 256), **per_batch),
                pl.BlockSpec((1, Q_LORA), const2),
                pl.BlockSpec((1, KV_LORA), const2),
                pl.BlockSpec(w_uq_r.shape, const2, **once),
                pl.BlockSpec(w_uk_t.shape, const2, **once),
                pl.BlockSpec(w_uv.shape, const2, **once)]
    args = [proj_hi, proj_hi, q_norm_g, kv_norm_g, w_uq_r, w_uk_t, w_uv]
    scratch = [pltpu.VMEM((N_HEADS * ATTN_HEAD_COLS, l), BF16), pltpu.VMEM((l, N_HEADS * V_HEAD), BF16)]
    if latent:
        in_specs += [pl.BlockSpec((l, 128), const2, **once)] * 2
        args += list(rope_tables)
    if has_ctx:
        past = ctx[0].shape[1]
        in_specs += [pl.BlockSpec((seqs, past, KV_LORA), lambda i, t: (i, 0, 0), **per_batch),
                     pl.BlockSpec((seqs, past, QK_ROPE), lambda i, t: (i, 0, 0), **per_batch)]
        args += list(ctx)
        scratch += [pltpu.VMEM((N_HEADS * ATTN_HEAD_COLS, past), BF16), pltpu.VMEM((past, N_HEADS * V_HEAD), BF16)]
    out_specs = [pl.BlockSpec((seqs, tq, N_HEADS * V_HEAD), lambda i, t: (i, t, 0))]
    out_shape = [jax.ShapeDtypeStruct((b, l, N_HEADS * V_HEAD), BF16)]
    if emit_cache:
        out_specs += [pl.BlockSpec((seqs, l, KV_LORA), lambda i, t: (i, 0, 0)),
                      pl.BlockSpec((seqs, l, QK_ROPE), lambda i, t: (i, 0, 0))]
        out_shape += [jax.ShapeDtypeStruct((b, l, KV_LORA), F32), jax.ShapeDtypeStruct((b, l, QK_ROPE), F32)]
    return pl.pallas_call(
        functools.partial(_mla_kernel, latent=latent, has_ctx=has_ctx, emit_cache=emit_cache, seq_len=l, tq=tq,
                          seqs=seqs),
        grid=(b // seqs, l // tq),
        in_specs=in_specs,
        out_specs=out_specs,
        out_shape=out_shape,
        scratch_shapes=scratch,
        compiler_params=_params("arbitrary", "arbitrary", vmem_limit=MLA_VMEM_LIMIT),
        name="mla_attention",
    )(*args)


def _split3(x):
    hi = x.astype(BF16)
    r = x - hi.astype(F32)
    mid = r.astype(BF16)
    lo = (r - mid.astype(F32)).astype(BF16)
    return hi, mid, lo


def _exact_dot(parts, sel):
    return functools.reduce(jnp.add, [_dot(p, sel) for p in parts])


def _exact_dot_rows(sel, parts):
    return functools.reduce(jnp.add, [_dot(sel, p) for p in parts])


HEADS_PER_GROUP = SSD_HEADS // SSD_GROUPS
GROUP_COLS = HEADS_PER_GROUP * SSD_HEADDIM


def _ssd_kernel(*refs, nc, cps, has_h0, seqs):
    n_in = 8 if has_h0 else 7
    per_seq = {0, 1, 2, 3, 7} if has_h0 else {0, 1, 2, 3}
    for bi in range(seqs):
        view = lambda r: r.at[pl.ds(bi, 1)]
        ins = [view(r) if i in per_seq else r for i, r in enumerate(refs[:n_in])]
        outs = [view(r) for r in refs[n_in:n_in + 2]]
        _ssd_sequence(*ins, *outs, *refs[n_in + 2:], nc=nc, cps=cps, has_h0=has_h0)


def _ssd_sequence(*refs, nc, cps, has_h0):
    if has_h0:
        (xs_ref, zs_ref, bc_ref, dt_ref, dtb_ref, a_ref, dx_ref, h0_ref, y_ref, ht_ref,
         yl_scr, acum_scr, src_scr, tot_scr, sb_scr, h_scr, esel_scr) = refs
    else:
        (xs_ref, zs_ref, bc_ref, dt_ref, dtb_ref, a_ref, dx_ref, y_ref, ht_ref,
         yl_scr, acum_scr, src_scr, tot_scr, sb_scr, h_scr, esel_scr) = refs
    s = pl.program_id(1)
    q = SSD_CHUNK
    n_bc = SSD_GROUPS * SSD_STATE
    lane = lax.broadcasted_iota(jnp.int32, (q, 128), 1)
    low_half = lane < SSD_HEADDIM
    ii = lax.broadcasted_iota(jnp.int32, (q, q), 0)
    jj = lax.broadcasted_iota(jnp.int32, (q, q), 1)
    lower, upper = ii >= jj, ii <= jj

    def lane_bcast(parts, d):
        return _exact_dot(parts, esel_scr[d])

    def stacked_states(d):
        return [jnp.concatenate([h_scr[d, 2 * i], h_scr[d, 2 * i + 1]], axis=0).astype(BF16)
                for i in range(SSD_GROUPS // 2)]

    def group_c(bc, g):
        i, r = divmod(g, 2)
        cpair = bc[:, n_bc + i * 128:n_bc + (i + 1) * 128]
        return jnp.where(low_half if r == 0 else ~low_half, cpair, 0.0).astype(BF16)

    @pl.when(s == 0)
    def _():
        k = lax.broadcasted_iota(jnp.int32, (128, SSD_INNER), 0)
        head = lax.broadcasted_iota(jnp.int32, (128, SSD_INNER), 1) // SSD_HEADDIM
        for d in range(2):
            esel_scr[d] = jnp.where(k == d * SSD_HEADS + head, 1.0, 0.0).astype(BF16)
        if has_h0:
            for d in range(2):
                for g in range(SSD_GROUPS):
                    hpn = h0_ref[0, d, g * HEADS_PER_GROUP:(g + 1) * HEADS_PER_GROUP].reshape(GROUP_COLS, SSD_STATE)
                    h_scr[d, g] = hpn.T
        else:
            h_scr[...] = jnp.zeros(h_scr.shape, F32)
        tri_f = jnp.where(lower, 1.0, 0.0).astype(BF16)
        tri_b = jnp.where(upper, 1.0, 0.0).astype(BF16)
        fwd_col = lane < SSD_HEADS
        for c in range(nc):
            crow = slice(c * q, (c + 1) * q)
            dt = _softplus(dt_ref[0, crow, :] + dtb_ref[...])
            parts = _split3(dt * a_ref[...])
            acum = jnp.where(fwd_col, _exact_dot_rows(tri_f, parts),
                             _exact_dot_rows(tri_b, parts))
            acum = acum * LOG2_E
            acum_scr[crow, :] = acum
            tot = jnp.where(fwd_col[0:1], acum[q - 1:q, :], acum[0:1, :])
            tot_scr[c] = jnp.broadcast_to(tot, (8, 128))
            src_scr[c] = (acum - jnp.log2(dt)).T

    def first_sweep(c, blk):
        rows = pl.ds(pl.multiple_of(c * q, q), q)
        x = xs_ref[0, blk, :]
        bc = bc_ref[0, blk, :].astype(F32)
        acum = acum_scr[rows, :]
        src_t = src_scr[c]
        tot8 = tot_scr[c]
        e = jnp.exp2(acum)
        e_hi = e.astype(BF16)
        eb_f = lane_bcast([e_hi, (e - e_hi.astype(F32)).astype(BF16)], 0)
        cd_f = jnp.exp2(lane_bcast(_split3(tot8), 0))[0:1]
        b_t = [bc[:, i * 128:(i + 1) * 128].T for i in range(SSD_GROUPS // 2)]
        h_in = stacked_states(0)
        for g in range(SSD_GROUPS):
            i, r = divmod(g, 2)
            cm = group_c(bc, g)
            scores = _dot_nt(cm, bc[:, i * 128:(i + 1) * 128].astype(BF16))
            bg_t = b_t[i][r * SSD_STATE:(r + 1) * SSD_STATE, :]
            gcols = slice(g * GROUP_COLS, (g + 1) * GROUP_COLS)
            y_off = _dot(cm, h_in[i]) * eb_f[:, gcols]
            for t in range(HEADS_PER_GROUP // 2):
                pc = slice((2 * g + t) * 128, (2 * g + t + 1) * 128)
                xp = x[:, pc]
                x2 = jnp.concatenate([jnp.where(low_half, xp, 0.0), jnp.where(low_half, 0.0, xp)], axis=0).astype(BF16)
                m, sf, sb = [], [], []
                for u in range(2):
                    kf = g * HEADS_PER_GROUP + 2 * t + u
                    kb = SSD_HEADS + kf
                    af_col = jnp.broadcast_to(acum[:, kf:kf + 1], (q, q))
                    ab_col = jnp.broadcast_to(acum[:, kb:kb + 1], (q, q))
                    sf_row, sb_row = src_t[kf:kf + 1, :], src_t[kb:kb + 1, :]
                    decay = (jnp.exp2(jnp.where(lower, af_col - sf_row, NEG_BIG))
                             + jnp.exp2(jnp.where(upper, ab_col - sb_row, NEG_BIG)))
                    m.append((scores * decay).astype(BF16))
                    wf = jnp.exp2(tot8[0:1, kf:kf + 1] - sf_row)
                    wb = jnp.exp2(tot8[0:1, kb:kb + 1] - sb_row)
                    sf.append((bg_t * wf).astype(BF16))
                    sb.append((bg_t * wb).astype(BF16))
                y_pair = _dot(jnp.concatenate(m, axis=1), x2)
                tc = slice(t * 128, (t + 1) * 128)
                yl_scr[rows, pc] = y_pair + y_off[:, tc] + dx_ref[:, pc] * xp
                h_scr[0, g, :, tc] = h_scr[0, g, :, tc] * cd_f[:, pc] + _dot(jnp.concatenate(sf, axis=1), x2)
                sb_scr[c, g, :, tc] = _dot(jnp.concatenate(sb, axis=1), x2)

    def second_sweep(c, blk):
        rows = pl.ds(pl.multiple_of(c * q, q), q)
        bc = bc_ref[0, blk, :].astype(F32)
        e = jnp.exp2(acum_scr[rows, :])
        e_hi = e.astype(BF16)
        eb_b = lane_bcast([e_hi, (e - e_hi.astype(F32)).astype(BF16)], 1)
        cd_b = jnp.exp2(lane_bcast(_split3(tot_scr[c]), 1))[0:1]
        h_in = stacked_states(1)
        for g in range(SSD_GROUPS):
            gcols = slice(g * GROUP_COLS, (g + 1) * GROUP_COLS)
            y_off = _dot(group_c(bc, g), h_in[g // 2]) * eb_b[:, gcols]
            y_ref[0, blk, gcols] = ((yl_scr[rows, gcols] + y_off) * zs_ref[0, blk, gcols].astype(F32)).astype(y_ref.dtype)
            h_scr[1, g] = h_scr[1, g] * cd_b[:, gcols] + sb_scr[c, g]

    n_steps = nc // cps
    one_step = n_steps == 1

    @pl.when(s < n_steps)
    def _():
        for ci in range(cps):
            first_sweep(s * cps + ci, slice(ci * q, (ci + 1) * q))

    @pl.when(jnp.logical_or(one_step, s >= n_steps))
    def _():
        blk_id = s * 0 if one_step else 2 * n_steps - 1 - s
        for ci in reversed(range(cps)):
            second_sweep(blk_id * cps + ci, slice(ci * q, (ci + 1) * q))

    @pl.when(jnp.logical_or(one_step, s == 2 * n_steps - 1))
    def _():
        for d in range(2):
            for g in range(SSD_GROUPS):
                ht_ref[0, d, g * HEADS_PER_GROUP:(g + 1) * HEADS_PER_GROUP] = h_scr[d, g].T.reshape(
                    HEADS_PER_GROUP, SSD_HEADDIM, SSD_STATE)


def _ssd(proj_lo, proj_hi, h0, dt_bias128, a128, d_exp):
    b, l, _ = proj_hi.shape
    q = SSD_CHUNK
    nc = l // q
    cps = min(nc, SSD_CHUNKS_PER_STEP)
    n_steps = nc // cps
    n_grid = 1 if n_steps == 1 else 2 * n_steps
    rows = cps * q
    early = lambda s: jnp.minimum(s, n_steps - 1)
    both = lambda s: jnp.where(s < n_steps, s, 2 * n_steps - 1 - s)
    late = lambda s: jnp.where(s < n_steps, n_steps - 1, 2 * n_steps - 1 - s)
    seqs = SSD_SEQS_PER_STEP if (n_steps == 1 and b % SSD_SEQS_PER_STEP == 0) else 1
    st_shape = (seqs, 2, SSD_HEADS, SSD_HEADDIM, SSD_STATE)
    st_spec = pl.BlockSpec(st_shape, lambda i, s: (i, 0, 0, 0, 0))
    has_h0 = h0 is not None
    return pl.pallas_call(
        functools.partial(_ssd_kernel, nc=nc, cps=cps, has_h0=has_h0, seqs=seqs),
        grid=(b // seqs, n_grid),
        in_specs=[pl.BlockSpec((seqs, rows, SSD_INNER), lambda i, s: (i, early(s), XS_BLK)),
                  pl.BlockSpec((seqs, rows, SSD_INNER), lambda i, s: (i, late(s), Z_BLK)),
                  pl.BlockSpec((seqs, rows, IN_TILE), lambda i, s: (i, both(s), BC_TILE)),
                  pl.BlockSpec((seqs, l, 128), lambda i, s: (i, 0, DT_BLK)),
                  pl.BlockSpec((1, 128), lambda i, s: (0, 0)),
                  pl.BlockSpec((1, 128), lambda i, s: (0, 0)),
                  pl.BlockSpec((1, SSD_INNER), lambda i, s: (0, 0))] + ([st_spec] if has_h0 else []),
        out_specs=[pl.BlockSpec((seqs, rows, SSD_INNER), lambda i, s: (i, late(s), 0)),
                   pl.BlockSpec(st_shape, lambda i, s: (i, 0, 0, 0, 0))],
        out_shape=[jax.ShapeDtypeStruct((b, l, SSD_INNER), BF16),
                   jax.ShapeDtypeStruct((b,) + st_shape[1:], F32)],
        scratch_shapes=[pltpu.VMEM((l, SSD_INNER), F32),
                        pltpu.VMEM((l, 128), F32),
                        pltpu.VMEM((nc, 128, q), F32),
                        pltpu.VMEM((nc, 8, 128), F32),
                        pltpu.VMEM((nc, SSD_GROUPS, SSD_STATE, GROUP_COLS), F32),
                        pltpu.VMEM((2, SSD_GROUPS, SSD_STATE, GROUP_COLS), F32),
                        pltpu.VMEM((2, 128, SSD_INNER), BF16)],
        compiler_params=_params("arbitrary", "arbitrary"),
        name="ssd_scan",
    )(proj_hi, proj_lo, proj_lo, proj_hi, dt_bias128, a128, d_exp, *([h0] if has_h0 else []))


def _merge_kernel(attn_ref, yz_ref, gm_ref, gs_ref, x_ref, g1_ref, ng_ref, womla_ref, wossd_ref, wout_ref, o_ref, w_scr):
    @pl.when(pl.program_id(0) == 0)
    def _():
        w_scr[0] = womla_ref[...].astype(BF16)
        w_scr[1] = wossd_ref[...].astype(BF16)
        w_scr[2] = wout_ref[...].astype(BF16)

    o_mla = _dot(attn_ref[...], w_scr[0])
    o_ssd = _dot(_rmsnorm(yz_ref[...].astype(F32), ng_ref[...]).astype(BF16), w_scr[1])
    merged = gm_ref[...].astype(F32) * o_mla + gs_ref[...].astype(F32) * o_ssd
    o_ref[...] = x_ref[...] + g1_ref[...] * _dot(merged.astype(BF16), w_scr[2])


def _merge(attn2d, yz2d, proj, x2d, mod48, mod_row, ssd_norm_g, w_o_mla, w_o_ssd, w_out):
    t = x2d.shape[0]
    tm = TOKEN_TILE
    row = lambda i: (i, 0)
    const = lambda i: (0, 0)
    wspec = pl.BlockSpec((D_MODEL, D_MODEL), const, pipeline_mode=pl.Buffered(1))
    return pl.pallas_call(
        _merge_kernel,
        grid=(t // tm,),
        in_specs=[pl.BlockSpec((tm, D_MODEL), row),
                  pl.BlockSpec((tm, D_MODEL), row),
                  pl.BlockSpec((tm, D_MODEL), lambda i: (i, GM_BLK)),
                  pl.BlockSpec((tm, D_MODEL), lambda i: (i, GS_BLK)),
                  pl.BlockSpec((tm, D_MODEL), row),
                  pl.BlockSpec((None, 1, D_MODEL), lambda i: (mod_row(i * tm // ROW_GROUP) * 6 + 2, 0, 0)),
                  pl.BlockSpec((1, D_MODEL), const),
                  wspec, wspec, wspec],
        out_specs=pl.BlockSpec((tm, D_MODEL), row),
        out_shape=jax.ShapeDtypeStruct((t, D_MODEL), F32),
        scratch_shapes=[pltpu.VMEM((3, D_MODEL, D_MODEL), BF16)],
        compiler_params=_params("arbitrary"),
        name="merge_out",
    )(attn2d, yz2d, proj, proj, x2d, mod48, ssd_norm_g, w_o_mla, w_o_ssd, w_out)


def _ffn_kernel(x_ref, sh_ref, sc_ref, g2_ref, ng_ref, wg_ref, wv_ref, cwg_ref, cwv_ref, cbg_ref, cbv_ref, wd_ref,
                fg_ref, o_ref, h_scr, wup_scr, wd_scr, u_scr, *, seq_len):
    j = pl.program_id(1)
    chunk = FFN_CHUNK
    n_chunks = ROW_GROUP // chunk
    rows = lambda c: slice(c * chunk, (c + 1) * chunk)

    wd_scr[...] = wd_ref[...].astype(BF16)

    def first_up(h_rows):
        n_k = 4
        acc = None
        for k in range(n_k):
            ks = slice(k * (D_MODEL // n_k), (k + 1) * (D_MODEL // n_k))
            wup_scr[ks, 0:FFN_TILE] = wg_ref[ks, :].astype(BF16)
            wup_scr[ks, FFN_TILE:2 * FFN_TILE] = wv_ref[ks, :].astype(BF16)
            part = _dot(h_scr[h_rows, ks], wup_scr[ks, :])
            acc = part if acc is None else acc + part
        return acc

    def gated(c):
        ug = _dwconv3_rows(u_scr, c, chunk, slice(0, FFN_TILE), cwg_ref, cbg_ref, seq_len)
        uv = _dwconv3_rows(u_scr, c, chunk, slice(FFN_TILE, 2 * FFN_TILE), cwv_ref, cbv_ref, seq_len)
        return (_silu(ug) * uv).astype(BF16)

    def pipeline(first, last):
        act = {}
        lag = FFN_DOWN_LAG
        for c in range(n_chunks + lag):
            if c < n_chunks:
                if first:
                    h_scr[rows(c), :] = _norm_mod(x_ref[rows(c), :], ng_ref, sc_ref, sh_ref)
                up = first_up(rows(c)) if c == 0 else _dot(h_scr[rows(c), :], wup_scr[...])
                u_scr[_stage_rows(c, chunk), :] = up
            if c >= lag:
                r = rows(c - lag)
                acc = _dot(act.pop(c - lag), wd_scr[...])
                if not first:
                    acc = o_ref[r, :] + acc
                if last:
                    acc = _rmsnorm(x_ref[r, :] + g2_ref[...] * acc, fg_ref[...])
                o_ref[r, :] = acc
            if 1 <= c <= n_chunks:
                act[c - 1] = gated(c - 1)

    last_j = pl.num_programs(1) - 1

    @pl.when(j == 0)
    def _():
        _zero_conv_pads(u_scr)
        pipeline(True, False)

    @pl.when((j > 0) & (j < last_j))
    def _():
        pipeline(False, False)

    @pl.when(j == last_j)
    def _():
        pipeline(False, True)


def _ffn(x2d, mod48, mod_row, norm_g, w_up, conv_w, conv_b, w_down, final_g, seq_len):
    t = x2d.shape[0]
    nj = D_FF // FFN_TILE
    gate = lambda r, j: (0, j)
    val = lambda r, j: (0, nj + j)
    const = lambda r, j: (0, 0)
    mod = lambda k: pl.BlockSpec((None, 1, D_MODEL), lambda r, j: (mod_row(r) * 6 + k, 0, 0))
    return pl.pallas_call(
        functools.partial(_ffn_kernel, seq_len=seq_len),
        grid=(t // ROW_GROUP, nj),
        in_specs=[pl.BlockSpec((ROW_GROUP, D_MODEL), lambda r, j: (r, 0)),
                  mod(3), mod(4), mod(5),
                  pl.BlockSpec((1, D_MODEL), const),
                  pl.BlockSpec((D_MODEL, FFN_TILE), gate),
                  pl.BlockSpec((D_MODEL, FFN_TILE), val),
                  pl.BlockSpec((3, FFN_TILE), gate),
                  pl.BlockSpec((3, FFN_TILE), val),
                  pl.BlockSpec((1, FFN_TILE), gate),
                  pl.BlockSpec((1, FFN_TILE), val),
                  pl.BlockSpec((FFN_TILE, D_MODEL), lambda r, j: (j, 0)),
                  pl.BlockSpec((1, D_MODEL), const)],
        out_specs=pl.BlockSpec((ROW_GROUP, D_MODEL), lambda r, j: (r, 0)),
        out_shape=jax.ShapeDtypeStruct((t, D_MODEL), F32),
        scratch_shapes=[pltpu.VMEM((ROW_GROUP, D_MODEL), BF16),
                        pltpu.VMEM((D_MODEL, 2 * FFN_TILE), BF16),
                        pltpu.VMEM((FFN_TILE, D_MODEL), BF16),
                        pltpu.VMEM((ROW_GROUP + 2 * CONV_PAD, 2 * FFN_TILE), F32)],
        compiler_params=_params("arbitrary", "arbitrary"),
        name="conv_ffn",
    )(x2d, mod48, mod48, mod48, norm_g, w_up, w_up, conv_w, conv_w, conv_b, conv_b, w_down, final_g)


def _rope_tables(seq_len):
    t = np.arange(seq_len)
    row = (t // GRID_W).astype(np.float32)
    col = (t % GRID_W).astype(np.float32)
    n = QK_ROPE // 4
    inv = (np.float32(ROPE_BASE) ** (-np.arange(n, dtype=np.float32) / np.float32(n))).astype(np.float32)
    ar, ac = row[:, None] * inv, col[:, None] * inv
    cos64 = np.concatenate([np.cos(ar), np.cos(ar), np.cos(ac), np.cos(ac)], axis=1)
    sin64 = np.concatenate([-np.sin(ar), np.sin(ar), -np.sin(ac), np.sin(ac)], axis=1)
    zeros = np.zeros_like(cos64)
    return (jnp.asarray(np.concatenate([cos64, zeros], axis=1), F32),
            jnp.asarray(np.concatenate([sin64, zeros], axis=1), F32))


def _swap_rope_halves(w):
    lead = w.shape[:-1]
    return w.reshape(lead + (2, 2, QK_ROPE // 4))[..., ::-1, :].reshape(lead + (QK_ROPE,))


def _trunk_pass(x, mod48, mod_row, wts, ctx, latent):
    b, l, _ = x.shape
    x2d = x.reshape(b * l, D_MODEL)
    proj_lo, proj_hi = _in_proj(x2d, mod48, mod_row, wts["norm_attn_g"], wts["w_in_r"], wts["ssd_conv_w"],
                                wts["ssd_conv_b"], l)
    rope = _rope_tables(l) if latent else None
    w_uq_r = wts["w_uq_lat"] if latent else wts["w_uq_ctx"]
    shape3 = lambda a, n: a.reshape(b, n, a.shape[-1])
    h0 = None
    mla_ctx = None
    if ctx is not None:
        cache_ckv, cache_krope, h0 = ctx
        mla_ctx = (cache_ckv, cache_krope)
    emit_cache = ctx is None
    mla_out = _mla(shape3(proj_hi, l), wts["q_norm_g"], wts["kv_norm_g"], w_uq_r, wts["w_uk_t"], wts["w_uv"],
                   rope, mla_ctx, emit_cache)
    attn, ckv_n, kr3 = mla_out if emit_cache else (mla_out[0], None, None)
    yz, h_t = _ssd(shape3(proj_lo, l), shape3(proj_hi, l), h0, wts["dt_bias128"], wts["a128"], wts["d_exp"])
    x1 = _merge(attn.reshape(b * l, -1), yz.reshape(b * l, -1), proj_lo, x2d, mod48, mod_row, wts["ssd_norm_g"],
                wts["w_o_mla"], wts["w_o_ssd"], wts["w_out"])
    y = _ffn(x1, mod48, mod_row, wts["norm_ffn_g"], wts["w_up"], wts["ffn_conv_w"], wts["ffn_conv_b"], wts["w_down"],
             wts["final_norm_g"], l)
    return y.reshape(b, l, D_MODEL), ckv_n, kr3, h_t


def kernel(x_prompt, x_sample, c, cache_ckv, cache_krope, state_ssd, c_ctx, w_ada, b_ada, norm_attn_g, w_in, q_norm_g,
           kv_norm_g, w_uq, w_ukv, w_o_mla, ssd_conv_w, ssd_conv_b, ssd_dt_bias, ssd_A_log, ssd_D, ssd_norm_g, w_o_ssd,
           w_out, norm_ffn_g, w_up, ffn_conv_w, ffn_conv_b, w_down, final_norm_g):
    depth = w_in.shape[0]
    assert depth == 1, "single trunk layer"
    dec_b = x_sample.shape[0]
    assert x_sample.shape[1] == ROW_GROUP and ROW_GROUP % x_prompt.shape[1] == 0
    lyr = 0

    cvec = jnp.zeros((8, D_MODEL), F32).at[0].set(c_ctx).at[1:1 + dec_b].set(c)
    mod48 = _ada(cvec, w_ada[lyr], b_ada[lyr]).reshape(8 * 6, 1, D_MODEL)

    w_in_r = _regroup_w_in(w_in[lyr].T)
    wq = w_uq[lyr].reshape(Q_LORA, N_HEADS, QK_NOPE + QK_ROPE)
    wq_nope = wq[:, :, :QK_NOPE].reshape(Q_LORA, -1)
    wq_rope = wq[:, :, QK_NOPE:]
    pad_rope = lambda w: jnp.pad(w, ((0, 0), (0, 0), (0, 128 - QK_ROPE))).reshape(Q_LORA, -1)
    w_uq_ctx = jnp.concatenate([wq_nope, pad_rope(wq_rope)], axis=1).astype(BF16)
    w_uq_lat = jnp.concatenate([wq_nope, pad_rope(wq_rope), pad_rope(_swap_rope_halves(wq_rope))], axis=1).astype(BF16)
    wkv = w_ukv[lyr].reshape(KV_LORA, N_HEADS, QK_NOPE + V_HEAD)
    w_uk_t = wkv[:, :, :QK_NOPE].reshape(KV_LORA, -1).T.astype(BF16)
    w_uv = wkv[:, :, QK_NOPE:].reshape(KV_LORA, -1).astype(BF16)
    pad128 = lambda a: jnp.pad(a.reshape(1, -1), ((0, 0), (0, 128 - a.size)))
    wts = {
        "norm_attn_g": norm_attn_g[lyr].reshape(1, -1), "w_in_r": w_in_r,
        "ssd_conv_w": ssd_conv_w[lyr], "ssd_conv_b": ssd_conv_b[lyr].reshape(1, -1),
        "q_norm_g": q_norm_g[lyr].reshape(1, -1), "kv_norm_g": kv_norm_g[lyr].reshape(1, -1),
        "w_uq_ctx": w_uq_ctx, "w_uq_lat": w_uq_lat, "w_uk_t": w_uk_t, "w_uv": w_uv,
        "dt_bias128": pad128(ssd_dt_bias[lyr]), "a128": pad128(-jnp.exp(ssd_A_log[lyr])),
        "d_exp": jnp.repeat(ssd_D[lyr], SSD_HEADDIM).reshape(1, -1),
        "ssd_norm_g": ssd_norm_g[lyr].reshape(1, -1),
        "w_o_mla": w_o_mla[lyr], "w_o_ssd": w_o_ssd[lyr], "w_out": w_out[lyr],
        "norm_ffn_g": norm_ffn_g[lyr].reshape(1, -1), "w_up": w_up[lyr],
        "ffn_conv_w": ffn_conv_w[lyr], "ffn_conv_b": ffn_conv_b[lyr].reshape(1, -1),
        "w_down": w_down[lyr], "final_norm_g": final_norm_g.reshape(1, -1),
    }

    y_p, ckv_p, kr_p, st_p = _trunk_pass(x_prompt, mod48, lambda r: 0, wts, None, False)
    ctx = (cache_ckv[:, lyr], cache_krope[:, lyr], state_ssd[:, lyr])
    y_s, _, _, _ = _trunk_pass(x_sample, mod48, lambda r: 1 + r, wts, ctx, True)
    return y_p, y_s, ckv_p[:, None], kr_p[:, None], st_p[:, None]
```

```python
import functools
import math

import jax
import jax.numpy as jnp
import numpy as np
from jax import lax
from jax.experimental import pallas as pl
from jax.experimental.pallas import tpu as pltpu

F32 = jnp.float32
BF16 = jnp.bfloat16

D_MODEL = 1024
GRID_W = 64
N_HEADS = 8
QK_NOPE = 128
QK_ROPE = 64
V_HEAD = 128
Q_LORA = 256
KV_LORA = 256
ROPE_BASE = 10000.0
SSD_HEADS = 16
SSD_HEADDIM = 64
SSD_INNER = SSD_HEADS * SSD_HEADDIM
SSD_GROUPS = 4
SSD_STATE = 64
SSD_CHUNK = 128
D_FF = 2816
EPS = 1e-6

ROW_GROUP = 2048
IN_TILE = 512
SSD_CHUNKS_PER_STEP = 8
SSD_SEQS_PER_STEP = 2
MLA_SEQS_PER_STEP = 2
FFN_TILE = 256
TOKEN_TILE = 512
ATTN_Q_TILE = 512
IN_CHUNK = 256
FFN_CHUNK = 512
FFN_DOWN_LAG = 2
CONV_PAD = 8
VMEM_LIMIT = 56 * 1024 * 1024
MLA_VMEM_LIMIT = 62 * 1024 * 1024
NEG_BIG = -1e30
LOG2_E = 1.4426950408889634


def _sigmoid(x):
    return 1.0 / (1.0 + jnp.exp(-x))


def _silu(x):
    return x * _sigmoid(x)


def _softplus(x):
    e = jnp.exp(-jnp.abs(x))
    u = 1.0 + e
    log1p_e = jnp.where(u == 1.0, e, e * jnp.log(u) / jnp.where(u == 1.0, 1.0, u - 1.0))
    return jnp.maximum(x, 0.0) + log1p_e


def _rmsnorm(x, g):
    return x * lax.rsqrt(jnp.mean(x * x, axis=-1, keepdims=True) + EPS) * g


def _dot(a, b):
    return jnp.dot(a, b, preferred_element_type=F32)


def _dot_nt(a, b):
    return lax.dot_general(a, b, (((1,), (1,)), ((), ())), preferred_element_type=F32)


def _params(*sem, vmem_limit=VMEM_LIMIT):
    return pltpu.CompilerParams(dimension_semantics=sem, vmem_limit_bytes=vmem_limit)


def _norm_mod(x, g_ref, sc_ref, sh_ref):
    return (_rmsnorm(x, g_ref[...]) * (1.0 + sc_ref[...]) + sh_ref[...]).astype(BF16)


def _zero_conv_pads(u_scr):
    zeros = jnp.zeros((CONV_PAD, u_scr.shape[1]), F32)
    u_scr[0:CONV_PAD, :] = zeros
    u_scr[CONV_PAD + ROW_GROUP:2 * CONV_PAD + ROW_GROUP, :] = zeros


def _stage_rows(c, chunk):
    return slice(CONV_PAD + c * chunk, CONV_PAD + (c + 1) * chunk)


def _dwconv3_rows(u_scr, c, chunk, cols, w_ref, b_ref, seq_len):
    r0 = c * chunk
    base = CONV_PAD + r0
    width = cols.stop - cols.start
    prev = u_scr[base - 1:base - 1 + chunk, cols]
    cur = u_scr[base:base + chunk, cols]
    nxt = u_scr[base + 1:base + 1 + chunk, cols]
    pos = (lax.broadcasted_iota(jnp.int32, (chunk, width), 0) + r0) & (seq_len - 1)
    if r0 % seq_len == 0 or chunk > seq_len:
        prev = jnp.where(pos == 0, 0.0, prev)
    if (r0 + chunk) % seq_len == 0 or chunk > seq_len:
        nxt = jnp.where(pos == seq_len - 1, 0.0, nxt)
    return prev * w_ref[0:1, :] + cur * w_ref[1:2, :] + nxt * w_ref[2:3, :] + b_ref[...]


def _ada_kernel(c_ref, w_ref, b_ref, o_ref):
    a = _silu(c_ref[...]).astype(BF16)
    o_ref[...] = _dot(a, w_ref[...].astype(BF16)) + b_ref[...]


def _ada(cvec, w_ada, b_ada):
    tn = 1536
    return pl.pallas_call(
        _ada_kernel,
        grid=(6 * D_MODEL // tn,),
        in_specs=[pl.BlockSpec((8, D_MODEL), lambda j: (0, 0)),
                  pl.BlockSpec((D_MODEL, tn), lambda j: (0, j)),
                  pl.BlockSpec((1, tn), lambda j: (0, j))],
        out_specs=pl.BlockSpec((8, tn), lambda j: (0, j)),
        out_shape=jax.ShapeDtypeStruct((8, 6 * D_MODEL), F32),
        compiler_params=_params("arbitrary"),
        name="ada_mod",
    )(cvec, w_ada, b_ada.reshape(1, -1))


IN_SPLITS = (Q_LORA, KV_LORA, QK_ROPE, SSD_INNER, SSD_INNER, SSD_GROUPS * SSD_STATE, SSD_GROUPS * SSD_STATE,
             2 * SSD_HEADS, D_MODEL, D_MODEL)
IN_OFFSETS = tuple(int(v) for v in np.cumsum((0,) + IN_SPLITS))


def _regroup_kernel(w_ref, o_ref):
    dst = 0

    def put_block(block):
        nonlocal dst
        o_ref[:, dst:dst + 128] = block.T.astype(BF16)
        dst += 128

    def piece(i):
        for off in range(0, IN_SPLITS[i], 128):
            put_block(w_ref[IN_OFFSETS[i] + off:IN_OFFSETS[i] + off + 128, :])

    def padded(parts):
        n = sum(p.shape[0] for p in parts)
        put_block(jnp.concatenate(parts + [jnp.zeros((128 - n, w_ref.shape[1]), F32)], axis=0))

    for i in (3, 8, 9, 5, 6, 4, 0, 1):
        piece(i)
    kr0, n = IN_OFFSETS[2], QK_ROPE // 4
    padded([w_ref[kr0:kr0 + QK_ROPE, :]])
    padded([w_ref[kr0 + blk * n:kr0 + (blk + 1) * n, :] for blk in (1, 0, 3, 2)])
    padded([w_ref[IN_OFFSETS[7]:IN_OFFSETS[7] + IN_SPLITS[7], :]])
    o_ref[:, dst:] = jnp.zeros((o_ref.shape[0], o_ref.shape[1] - dst), BF16)


def _regroup_w_in(w_in_t):
    cols = 256
    n_out = (N_LOW_TILES + N_F32_TILES) * IN_TILE
    return pl.pallas_call(
        _regroup_kernel,
        grid=(D_MODEL // cols,),
        in_specs=[pl.BlockSpec((w_in_t.shape[0], cols), lambda i: (0, i))],
        out_specs=pl.BlockSpec((cols, n_out), lambda i: (i, 0)),
        out_shape=jax.ShapeDtypeStruct((D_MODEL, n_out), BF16),
        compiler_params=_params("arbitrary"),
        name="w_in_regroup",
    )(w_in_t)


N_LOW_TILES, N_F32_TILES = 7, 4
Z_BLK, GM_BLK, GS_BLK = 0, 1, 2
BC_TILE = 6
XS_BLK = 0
MLA_TILE, MISC_TILE = 2, 3
DT_BLK = (MISC_TILE * IN_TILE + 256) // 128
ATTN_HEAD_COLS = 256


def _in_kernel(x_ref, sh_ref, sc_ref, g_ref, w_ref, cw_ref, cb_ref, lo_ref, hi_ref, h_scr, u_scr, *, seq_len):
    j = pl.program_id(1)
    chunk = IN_CHUNK
    n_chunks = ROW_GROUP // chunk
    rows = lambda c: slice(c * chunk, (c + 1) * chunk)

    def pointwise(fn, o_ref, stage_h=False):
        for c in range(n_chunks):
            if stage_h:
                h_scr[rows(c), :] = _norm_mod(x_ref[rows(c), :], g_ref, sc_ref, sh_ref)
            o_ref[rows(c), :] = fn(_dot(h_scr[rows(c), :], w_ref[...])).astype(o_ref.dtype)

    def conv(o_ref):
        def conv_out(c):
            v = _silu(_dwconv3_rows(u_scr, c, chunk, slice(0, IN_TILE), cw_ref, cb_ref, seq_len))
            o_ref[rows(c), :] = v.astype(o_ref.dtype)
        for c in range(n_chunks):
            u_scr[_stage_rows(c, chunk), :] = _dot(h_scr[rows(c), :], w_ref[...])
            if c >= 1:
                conv_out(c - 1)
        conv_out(n_chunks - 1)

    @pl.when(j == 0)
    def _():
        _zero_conv_pads(u_scr)
        pointwise(_silu, lo_ref, stage_h=True)

    @pl.when(j == 1)
    def _():
        pointwise(_silu, lo_ref)

    @pl.when((j >= 2) & (j <= 5))
    def _():
        pointwise(_sigmoid, lo_ref)

    @pl.when(j == BC_TILE)
    def _():
        conv(lo_ref)

    @pl.when((j == N_LOW_TILES) | (j == N_LOW_TILES + 1))
    def _():
        conv(hi_ref)

    @pl.when(j >= N_LOW_TILES + MLA_TILE)
    def _():
        pointwise(lambda u: u, hi_ref)


def _in_proj(x2d, mod48, mod_row, norm_g, w_in_r, conv_w, conv_b, seq_len):
    t = x2d.shape[0]
    n_tiles = N_LOW_TILES + N_F32_TILES
    conv_idx = lambda r, j: (0, jnp.where(j == BC_TILE, 2, jnp.clip(j - N_LOW_TILES, 0, 1)))
    return pl.pallas_call(
        functools.partial(_in_kernel, seq_len=seq_len),
        grid=(t // ROW_GROUP, n_tiles),
        in_specs=[pl.BlockSpec((ROW_GROUP, D_MODEL), lambda r, j: (r, 0)),
                  pl.BlockSpec((None, 1, D_MODEL), lambda r, j: (mod_row(r) * 6 + 0, 0, 0)),
                  pl.BlockSpec((None, 1, D_MODEL), lambda r, j: (mod_row(r) * 6 + 1, 0, 0)),
                  pl.BlockSpec((1, D_MODEL), lambda r, j: (0, 0)),
                  pl.BlockSpec((D_MODEL, IN_TILE), lambda r, j: (0, j)),
                  pl.BlockSpec((3, IN_TILE), conv_idx),
                  pl.BlockSpec((1, IN_TILE), conv_idx)],
        out_specs=[pl.BlockSpec((ROW_GROUP, IN_TILE), lambda r, j: (r, jnp.minimum(j, N_LOW_TILES - 1))),
                   pl.BlockSpec((ROW_GROUP, IN_TILE), lambda r, j: (r, jnp.maximum(j - N_LOW_TILES, 0)))],
        out_shape=[jax.ShapeDtypeStruct((t, N_LOW_TILES * IN_TILE), BF16),
                   jax.ShapeDtypeStruct((t, N_F32_TILES * IN_TILE), F32)],
        scratch_shapes=[pltpu.VMEM((ROW_GROUP, D_MODEL), BF16),
                        pltpu.VMEM((ROW_GROUP + 2 * CONV_PAD, IN_TILE), F32)],
        compiler_params=_params("arbitrary", "arbitrary"),
        name="in_proj",
    )(x2d, mod48, mod48, norm_g, w_in_r, conv_w, conv_b)


def _mla_kernel(*refs, latent, has_ctx, emit_cache, seq_len, tq, seqs):
    n_in = 7 + (2 if latent else 0) + (2 if has_ctx else 0)
    n_out = 3 if emit_cache else 1
    per_seq = {0, 1} | ({n_in - 2, n_in - 1} if has_ctx else set())
    for bi in range(seqs):
        view = lambda r: r.at[pl.ds(bi, 1)]
        ins = [view(r) if i in per_seq else r for i, r in enumerate(refs[:n_in])]
        outs = [view(r) for r in refs[n_in:n_in + n_out]]
        _mla_sequence(*ins, *outs, *refs[n_in + n_out:], latent=latent, has_ctx=has_ctx, emit_cache=emit_cache,
                      seq_len=seq_len, tq=tq)


def _mla_sequence(*refs, latent, has_ctx, emit_cache, seq_len, tq):
    refs = list(refs)
    pm_ref, px_ref, qg_ref, kvg_ref, wuq_ref, wkt_ref, wv_ref = refs[:7]
    del refs[:7]
    if latent:
        cos_ref, sin_ref = refs[:2]
        del refs[:2]
    if has_ctx:
        cckv_ref, ckr_ref = refs[:2]
        del refs[:2]
    o_ref = refs.pop(0)
    if emit_cache:
        ckv_ref, kr_ref = refs[:2]
        del refs[:2]
    k_scr, v_scr = refs[:2]
    del refs[:2]
    if has_ctx:
        kc_scr, vc_scr = refs
    t = pl.program_id(1)
    n_nope = N_HEADS * QK_NOPE

    def put_keys(k_dst, v_dst, rows, ckv_n, kr_bf):
        ckv_bf = ckv_n.astype(BF16)
        v_dst[rows, :] = _dot(ckv_bf, wv_ref[...]).astype(BF16)
        kn_t = _dot_nt(wkt_ref[...], ckv_bf).astype(BF16)
        width = kr_bf.shape[1]
        eye = jnp.where(lax.broadcasted_iota(jnp.int32, (128, width), 0)
                        == lax.broadcasted_iota(jnp.int32, (128, width), 1), 1.0, 0.0).astype(BF16)
        kr_t = _dot_nt(eye, kr_bf).astype(BF16)
        for h in range(N_HEADS):
            base = h * ATTN_HEAD_COLS
            k_dst[base:base + QK_NOPE, rows] = kn_t[h * QK_NOPE:(h + 1) * QK_NOPE, :]
            k_dst[base + QK_NOPE:base + ATTN_HEAD_COLS, rows] = kr_t

    @pl.when(t == 0)
    def _():
        step = min(seq_len, TOKEN_TILE)
        for r0 in range(0, seq_len, step):
            rows = slice(r0, r0 + step)
            ckv_n = _rmsnorm(pm_ref[0, rows, Q_LORA:], kvg_ref[...])
            kr = px_ref[0, rows, 0:128]
            if latent:
                kr = kr * cos_ref[rows, :] + px_ref[0, rows, 128:256] * sin_ref[rows, :]
            if emit_cache:
                ckv_ref[0, rows, :] = ckv_n
                kr_ref[0, rows, :] = kr[:, :QK_ROPE]
            put_keys(k_scr, v_scr, rows, ckv_n, kr.astype(BF16))
        if has_ctx:
            past = cckv_ref.shape[1]
            put_keys(kc_scr, vc_scr, slice(0, past), cckv_ref[0], ckr_ref[0].astype(BF16))

    qrows = pl.ds(pl.multiple_of(t * tq, tq), tq)
    scale = LOG2_E / math.sqrt(QK_NOPE + QK_ROPE)
    cqn = _rmsnorm(pm_ref[0, qrows, 0:Q_LORA], qg_ref[...]).astype(BF16)
    q = _dot(cqn, wuq_ref[...])
    q_rope = q[:, n_nope:2 * n_nope]
    if latent:
        q_rope = (q_rope * jnp.concatenate([cos_ref[qrows, :]] * N_HEADS, axis=1)
                  + q[:, 2 * n_nope:3 * n_nope] * jnp.concatenate([sin_ref[qrows, :]] * N_HEADS, axis=1))
    segs = ([(kc_scr, vc_scr)] if has_ctx else []) + [(k_scr, v_scr)]
    for h in range(N_HEADS):
        head = slice(h * QK_NOPE, (h + 1) * QK_NOPE)
        qk = slice(h * ATTN_HEAD_COLS, (h + 1) * ATTN_HEAD_COLS)
        qh = (jnp.concatenate([q[:, head], q_rope[:, head]], axis=1) * scale).astype(BF16)
        s = [_dot(qh, k[qk, :]) for k, _ in segs]
        m = functools.reduce(jnp.maximum, [jnp.max(si, axis=-1, keepdims=True) for si in s])
        p = [jnp.exp2(si - m) for si in s]
        l = functools.reduce(jnp.add, [jnp.sum(pi, axis=-1, keepdims=True) for pi in p])
        o = functools.reduce(jnp.add, [_dot(pi.astype(BF16), v[:, head]) for pi, (_, v) in zip(p, segs)])
        o_ref[0, :, head] = (o / l).astype(BF16)


def _mla(proj_hi, q_norm_g, kv_norm_g, w_uq_r, w_uk_t, w_uv, rope_tables, ctx, emit_cache):
    b, l, _ = proj_hi.shape
    tq = min(l, ATTN_Q_TILE)
    latent = rope_tables is not None
    has_ctx = ctx is not None
    once = dict(pipeline_mode=pl.Buffered(1))
    per_batch = once if l // tq > 1 else {}
    seqs = MLA_SEQS_PER_STEP if (l == tq and b % MLA_SEQS_PER_STEP == 0) else 1
    const2 = lambda i, t: (0, 0)
    in_specs = [pl.BlockSpec((seqs, l, IN_TILE), lambda i, t: (i, 0, MLA_TILE), **per_batch),
                pl.BlockSpec((seqs, l, 256), lambda i, t: (i, 0, MISC_TILE * IN_TILE // 256), **per_batch),
                pl.BlockSpec((1, Q_LORA), const2),
                pl.BlockSpec((1, KV_LORA), const2),
                pl.BlockSpec(w_uq_r.shape, const2, **once),
                pl.BlockSpec(w_uk_t.shape, const2, **once),
                pl.BlockSpec(w_uv.shape, const2, **once)]
    args = [proj_hi, proj_hi, q_norm_g, kv_norm_g, w_uq_r, w_uk_t, w_uv]
    scratch = [pltpu.VMEM((N_HEADS * ATTN_HEAD_COLS, l), BF16), pltpu.VMEM((l, N_HEADS * V_HEAD), BF16)]
    if latent:
        in_specs += [pl.BlockSpec((l, 128), const2, **once)] * 2
        args += list(rope_tables)
    if has_ctx:
        past = ctx[0].shape[1]
        in_specs += [pl.BlockSpec((seqs, past, KV_LORA), lambda i, t: (i, 0, 0), **per_batch),
                     pl.BlockSpec((seqs, past, QK_ROPE), lambda i, t: (i, 0, 0), **per_batch)]
        args += list(ctx)
        scratch += [pltpu.VMEM((N_HEADS * ATTN_HEAD_COLS, past), BF16), pltpu.VMEM((past, N_HEADS * V_HEAD), BF16)]
    out_specs = [pl.BlockSpec((seqs, tq, N_HEADS * V_HEAD), lambda i, t: (i, t, 0))]
    out_shape = [jax.ShapeDtypeStruct((b, l, N_HEADS * V_HEAD), BF16)]
    if emit_cache:
        out_specs += [pl.BlockSpec((seqs, l, KV_LORA), lambda i, t: (i, 0, 0)),
                      pl.BlockSpec((seqs, l, QK_ROPE), lambda i, t: (i, 0, 0))]
        out_shape += [jax.ShapeDtypeStruct((b, l, KV_LORA), F32), jax.ShapeDtypeStruct((b, l, QK_ROPE), F32)]
    return pl.pallas_call(
        functools.partial(_mla_kernel, latent=latent, has_ctx=has_ctx, emit_cache=emit_cache, seq_len=l, tq=tq,
                          seqs=seqs),
        grid=(b // seqs, l // tq),
        in_specs=in_specs,
        out_specs=out_specs,
        out_shape=out_shape,
        scratch_shapes=scratch,
        compiler_params=_params("arbitrary", "arbitrary", vmem_limit=MLA_VMEM_LIMIT),
        name="mla_attention",
    )(*args)


def _split3(x):
    hi = x.astype(BF16)
    r = x - hi.astype(F32)
    mid = r.astype(BF16)
    lo = (r - mid.astype(F32)).astype(BF16)
    return hi, mid, lo


def _exact_dot(parts, sel):
    return functools.reduce(jnp.add, [_dot(p, sel) for p in parts])


def _exact_dot_rows(sel, parts):
    return functools.reduce(jnp.add, [_dot(sel, p) for p in parts])


HEADS_PER_GROUP = SSD_HEADS // SSD_GROUPS
GROUP_COLS = HEADS_PER_GROUP * SSD_HEADDIM


def _ssd_kernel(*refs, nc, cps, has_h0, seqs):
    n_in = 8 if has_h0 else 7
    per_seq = {0, 1, 2, 3, 7} if has_h0 else {0, 1, 2, 3}
    for bi in range(seqs):
        view = lambda r: r.at[pl.ds(bi, 1)]
        ins = [view(r) if i in per_seq else r for i, r in enumerate(refs[:n_in])]
        outs = [view(r) for r in refs[n_in:n_in + 2]]
        _ssd_sequence(*ins, *outs, *refs[n_in + 2:], nc=nc, cps=cps, has_h0=has_h0)


def _ssd_sequence(*refs, nc, cps, has_h0):
    if has_h0:
        (xs_ref, zs_ref, bc_ref, dt_ref, dtb_ref, a_ref, dx_ref, h0_ref, y_ref, ht_ref,
         yl_scr, acum_scr, src_scr, tot_scr, sb_scr, h_scr, esel_scr) = refs
    else:
        (xs_ref, zs_ref, bc_ref, dt_ref, dtb_ref, a_ref, dx_ref, y_ref, ht_ref,
         yl_scr, acum_scr, src_scr, tot_scr, sb_scr, h_scr, esel_scr) = refs
    s = pl.program_id(1)
    q = SSD_CHUNK
    n_bc = SSD_GROUPS * SSD_STATE
    lane = lax.broadcasted_iota(jnp.int32, (q, 128), 1)
    low_half = lane < SSD_HEADDIM
    ii = lax.broadcasted_iota(jnp.int32, (q, q), 0)
    jj = lax.broadcasted_iota(jnp.int32, (q, q), 1)
    lower, upper = ii >= jj, ii <= jj

    def lane_bcast(parts, d):
        return _exact_dot(parts, esel_scr[d])

    def stacked_states(d):
        return [jnp.concatenate([h_scr[d, 2 * i], h_scr[d, 2 * i + 1]], axis=0).astype(BF16)
                for i in range(SSD_GROUPS // 2)]

    def group_c(bc, g):
        i, r = divmod(g, 2)
        cpair = bc[:, n_bc + i * 128:n_bc + (i + 1) * 128]
        return jnp.where(low_half if r == 0 else ~low_half, cpair, 0.0).astype(BF16)

    @pl.when(s == 0)
    def _():
        k = lax.broadcasted_iota(jnp.int32, (128, SSD_INNER), 0)
        head = lax.broadcasted_iota(jnp.int32, (128, SSD_INNER), 1) // SSD_HEADDIM
        for d in range(2):
            esel_scr[d] = jnp.where(k == d * SSD_HEADS + head, 1.0, 0.0).astype(BF16)
        if has_h0:
            for d in range(2):
                for g in range(SSD_GROUPS):
                    hpn = h0_ref[0, d, g * HEADS_PER_GROUP:(g + 1) * HEADS_PER_GROUP].reshape(GROUP_COLS, SSD_STATE)
                    h_scr[d, g] = hpn.T
        else:
            h_scr[...] = jnp.zeros(h_scr.shape, F32)
        tri_f = jnp.where(lower, 1.0, 0.0).astype(BF16)
        tri_b = jnp.where(upper, 1.0, 0.0).astype(BF16)
        fwd_col = lane < SSD_HEADS
        for c in range(nc):
            crow = slice(c * q, (c + 1) * q)
            dt = _softplus(dt_ref[0, crow, :] + dtb_ref[...])
            parts = _split3(dt * a_ref[...])
            acum = jnp.where(fwd_col, _exact_dot_rows(tri_f, parts),
                             _exact_dot_rows(tri_b, parts))
            acum = acum * LOG2_E
            acum_scr[crow, :] = acum
            tot = jnp.where(fwd_col[0:1], acum[q - 1:q, :], acum[0:1, :])
            tot_scr[c] = jnp.broadcast_to(tot, (8, 128))
            src_scr[c] = (acum - jnp.log2(dt)).T

    def first_sweep(c, blk):
        rows = pl.ds(pl.multiple_of(c * q, q), q)
        x = xs_ref[0, blk, :]
        bc = bc_ref[0, blk, :].astype(F32)
        acum = acum_scr[rows, :]
        src_t = src_scr[c]
        tot8 = tot_scr[c]
        e = jnp.exp2(acum)
        e_hi = e.astype(BF16)
        eb_f = lane_bcast([e_hi, (e - e_hi.astype(F32)).astype(BF16)], 0)
        cd_f = jnp.exp2(lane_bcast(_split3(tot8), 0))[0:1]
        b_t = [bc[:, i * 128:(i + 1) * 128].T for i in range(SSD_GROUPS // 2)]
        h_in = stacked_states(0)
        for g in range(SSD_GROUPS):
            i, r = divmod(g, 2)
            cm = group_c(bc, g)
            scores = _dot_nt(cm, bc[:, i * 128:(i + 1) * 128].astype(BF16))
            bg_t = b_t[i][r * SSD_STATE:(r + 1) * SSD_STATE, :]
            gcols = slice(g * GROUP_COLS, (g + 1) * GROUP_COLS)
            y_off = _dot(cm, h_in[i]) * eb_f[:, gcols]
            for t in range(HEADS_PER_GROUP // 2):
                pc = slice((2 * g + t) * 128, (2 * g + t + 1) * 128)
                xp = x[:, pc]
                x2 = jnp.concatenate([jnp.where(low_half, xp, 0.0), jnp.where(low_half, 0.0, xp)], axis=0).astype(BF16)
                m, sf, sb = [], [], []
                for u in range(2):
                    kf = g * HEADS_PER_GROUP + 2 * t + u
                    kb = SSD_HEADS + kf
                    af_col = jnp.broadcast_to(acum[:, kf:kf + 1], (q, q))
                    ab_col = jnp.broadcast_to(acum[:, kb:kb + 1], (q, q))
                    sf_row, sb_row = src_t[kf:kf + 1, :], src_t[kb:kb + 1, :]
                    decay = (jnp.exp2(jnp.where(lower, af_col - sf_row, NEG_BIG))
                             + jnp.exp2(jnp.where(upper, ab_col - sb_row, NEG_BIG)))
                    m.append((scores * decay).astype(BF16))
                    wf = jnp.exp2(tot8[0:1, kf:kf + 1] - sf_row)
                    wb = jnp.exp2(tot8[0:1, kb:kb + 1] - sb_row)
                    sf.append((bg_t * wf).astype(BF16))
                    sb.append((bg_t * wb).astype(BF16))
                y_pair = _dot(jnp.concatenate(m, axis=1), x2)
                tc = slice(t * 128, (t + 1) * 128)
                yl_scr[rows, pc] = y_pair + y_off[:, tc] + dx_ref[:, pc] * xp
                h_scr[0, g, :, tc] = h_scr[0, g, :, tc] * cd_f[:, pc] + _dot(jnp.concatenate(sf, axis=1), x2)
                sb_scr[c, g, :, tc] = _dot(jnp.concatenate(sb, axis=1), x2)

    def second_sweep(c, blk):
        rows = pl.ds(pl.multiple_of(c * q, q), q)
        bc = bc_ref[0, blk, :].astype(F32)
        e = jnp.exp2(acum_scr[rows, :])
        e_hi = e.astype(BF16)
        eb_b = lane_bcast([e_hi, (e - e_hi.astype(F32)).astype(BF16)], 1)
        cd_b = jnp.exp2(lane_bcast(_split3(tot_scr[c]), 1))[0:1]
        h_in = stacked_states(1)
        for g in range(SSD_GROUPS):
            gcols = slice(g * GROUP_COLS, (g + 1) * GROUP_COLS)
            y_off = _dot(group_c(bc, g), h_in[g // 2]) * eb_b[:, gcols]
            y_ref[0, blk, gcols] = ((yl_scr[rows, gcols] + y_off) * zs_ref[0, blk, gcols].astype(F32)).astype(y_ref.dtype)
            h_scr[1, g] = h_scr[1, g] * cd_b[:, gcols] + sb_scr[c, g]

    n_steps = nc // cps
    one_step = n_steps == 1

    @pl.when(s < n_steps)
    def _():
        for ci in range(cps):
            first_sweep(s * cps + ci, slice(ci * q, (ci + 1) * q))

    @pl.when(jnp.logical_or(one_step, s >= n_steps))
    def _():
        blk_id = s * 0 if one_step else 2 * n_steps - 1 - s
        for ci in reversed(range(cps)):
            second_sweep(blk_id * cps + ci, slice(ci * q, (ci + 1) * q))

    @pl.when(jnp.logical_or(one_step, s == 2 * n_steps - 1))
    def _():
        for d in range(2):
            for g in range(SSD_GROUPS):
                ht_ref[0, d, g * HEADS_PER_GROUP:(g + 1) * HEADS_PER_GROUP] = h_scr[d, g].T.reshape(
                    HEADS_PER_GROUP, SSD_HEADDIM, SSD_STATE)


def _ssd(proj_lo, proj_hi, h0, dt_bias128, a128, d_exp):
    b, l, _ = proj_hi.shape
    q = SSD_CHUNK
    nc = l // q
    cps = min(nc, SSD_CHUNKS_PER_STEP)
    n_steps = nc // cps
    n_grid = 1 if n_steps == 1 else 2 * n_steps
    rows = cps * q
    early = lambda s: jnp.minimum(s, n_steps - 1)
    both = lambda s: jnp.where(s < n_steps, s, 2 * n_steps - 1 - s)
    late = lambda s: jnp.where(s < n_steps, n_steps - 1, 2 * n_steps - 1 - s)
    seqs = SSD_SEQS_PER_STEP if (n_steps == 1 and b % SSD_SEQS_PER_STEP == 0) else 1
    st_shape = (seqs, 2, SSD_HEADS, SSD_HEADDIM, SSD_STATE)
    st_spec = pl.BlockSpec(st_shape, lambda i, s: (i, 0, 0, 0, 0))
    has_h0 = h0 is not None
    return pl.pallas_call(
        functools.partial(_ssd_kernel, nc=nc, cps=cps, has_h0=has_h0, seqs=seqs),
        grid=(b // seqs, n_grid),
        in_specs=[pl.BlockSpec((seqs, rows, SSD_INNER), lambda i, s: (i, early(s), XS_BLK)),
                  pl.BlockSpec((seqs, rows, SSD_INNER), lambda i, s: (i, late(s), Z_BLK)),
                  pl.BlockSpec((seqs, rows, IN_TILE), lambda i, s: (i, both(s), BC_TILE)),
                  pl.BlockSpec((seqs, l, 128), lambda i, s: (i, 0, DT_BLK)),
                  pl.BlockSpec((1, 128), lambda i, s: (0, 0)),
                  pl.BlockSpec((1, 128), lambda i, s: (0, 0)),
                  pl.BlockSpec((1, SSD_INNER), lambda i, s: (0, 0))] + ([st_spec] if has_h0 else []),
        out_specs=[pl.BlockSpec((seqs, rows, SSD_INNER), lambda i, s: (i, late(s), 0)),
                   pl.BlockSpec(st_shape, lambda i, s: (i, 0, 0, 0, 0))],
        out_shape=[jax.ShapeDtypeStruct((b, l, SSD_INNER), BF16),
                   jax.ShapeDtypeStruct((b,) + st_shape[1:], F32)],
        scratch_shapes=[pltpu.VMEM((l, SSD_INNER), F32),
                        pltpu.VMEM((l, 128), F32),
                        pltpu.VMEM((nc, 128, q), F32),
                        pltpu.VMEM((nc, 8, 128), F32),
                        pltpu.VMEM((nc, SSD_GROUPS, SSD_STATE, GROUP_COLS), F32),
                        pltpu.VMEM((2, SSD_GROUPS, SSD_STATE, GROUP_COLS), F32),
                        pltpu.VMEM((2, 128, SSD_INNER), BF16)],
        compiler_params=_params("arbitrary", "arbitrary"),
        name="ssd_scan",
    )(proj_hi, proj_lo, proj_lo, proj_hi, dt_bias128, a128, d_exp, *([h0] if has_h0 else []))


def _merge_kernel(attn_ref, yz_ref, gm_ref, gs_ref, x_ref, g1_ref, ng_ref, womla_ref, wossd_ref, wout_ref, o_ref, w_scr):
    @pl.when(pl.program_id(0) == 0)
    def _():
        w_scr[0] = womla_ref[...].astype(BF16)
        w_scr[1] = wossd_ref[...].astype(BF16)
        w_scr[2] = wout_ref[...].astype(BF16)

    o_mla = _dot(attn_ref[...], w_scr[0])
    o_ssd = _dot(_rmsnorm(yz_ref[...].astype(F32), ng_ref[...]).astype(BF16), w_scr[1])
    merged = gm_ref[...].astype(F32) * o_mla + gs_ref[...].astype(F32) * o_ssd
    o_ref[...] = x_ref[...] + g1_ref[...] * _dot(merged.astype(BF16), w_scr[2])


def _merge(attn2d, yz2d, proj, x2d, mod48, mod_row, ssd_norm_g, w_o_mla, w_o_ssd, w_out):
    t = x2d.shape[0]
    tm = TOKEN_TILE
    row = lambda i: (i, 0)
    const = lambda i: (0, 0)
    wspec = pl.BlockSpec((D_MODEL, D_MODEL), const, pipeline_mode=pl.Buffered(1))
    return pl.pallas_call(
        _merge_kernel,
        grid=(t // tm,),
        in_specs=[pl.BlockSpec((tm, D_MODEL), row),
                  pl.BlockSpec((tm, D_MODEL), row),
                  pl.BlockSpec((tm, D_MODEL), lambda i: (i, GM_BLK)),
                  pl.BlockSpec((tm, D_MODEL), lambda i: (i, GS_BLK)),
                  pl.BlockSpec((tm, D_MODEL), row),
                  pl.BlockSpec((None, 1, D_MODEL), lambda i: (mod_row(i * tm // ROW_GROUP) * 6 + 2, 0, 0)),
                  pl.BlockSpec((1, D_MODEL), const),
                  wspec, wspec, wspec],
        out_specs=pl.BlockSpec((tm, D_MODEL), row),
        out_shape=jax.ShapeDtypeStruct((t, D_MODEL), F32),
        scratch_shapes=[pltpu.VMEM((3, D_MODEL, D_MODEL), BF16)],
        compiler_params=_params("arbitrary"),
        name="merge_out",
    )(attn2d, yz2d, proj, proj, x2d, mod48, ssd_norm_g, w_o_mla, w_o_ssd, w_out)


def _ffn_kernel(x_ref, sh_ref, sc_ref, g2_ref, ng_ref, wg_ref, wv_ref, cwg_ref, cwv_ref, cbg_ref, cbv_ref, wd_ref,
                fg_ref, o_ref, h_scr, wup_scr, wd_scr, u_scr, *, seq_len):
    j = pl.program_id(1)
    chunk = FFN_CHUNK
    n_chunks = ROW_GROUP // chunk
    rows = lambda c: slice(c * chunk, (c + 1) * chunk)

    wd_scr[...] = wd_ref[...].astype(BF16)

    def first_up(h_rows):
        n_k = 4
        acc = None
        for k in range(n_k):
            ks = slice(k * (D_MODEL // n_k), (k + 1) * (D_MODEL // n_k))
            wup_scr[ks, 0:FFN_TILE] = wg_ref[ks, :].astype(BF16)
            wup_scr[ks, FFN_TILE:2 * FFN_TILE] = wv_ref[ks, :].astype(BF16)
            part = _dot(h_scr[h_rows, ks], wup_scr[ks, :])
            acc = part if acc is None else acc + part
        return acc

    def gated(c):
        ug = _dwconv3_rows(u_scr, c, chunk, slice(0, FFN_TILE), cwg_ref, cbg_ref, seq_len)
        uv = _dwconv3_rows(u_scr, c, chunk, slice(FFN_TILE, 2 * FFN_TILE), cwv_ref, cbv_ref, seq_len)
        return (_silu(ug) * uv).astype(BF16)

    def pipeline(first, last):
        act = {}
        lag = FFN_DOWN_LAG
        for c in range(n_chunks + lag):
            if c < n_chunks:
                if first:
                    h_scr[rows(c), :] = _norm_mod(x_ref[rows(c), :], ng_ref, sc_ref, sh_ref)
                up = first_up(rows(c)) if c == 0 else _dot(h_scr[rows(c), :], wup_scr[...])
                u_scr[_stage_rows(c, chunk), :] = up
            if c >= lag:
                r = rows(c - lag)
                acc = _dot(act.pop(c - lag), wd_scr[...])
                if not first:
                    acc = o_ref[r, :] + acc
                if last:
                    acc = _rmsnorm(x_ref[r, :] + g2_ref[...] * acc, fg_ref[...])
                o_ref[r, :] = acc
            if 1 <= c <= n_chunks:
                act[c - 1] = gated(c - 1)

    last_j = pl.num_programs(1) - 1

    @pl.when(j == 0)
    def _():
        _zero_conv_pads(u_scr)
        pipeline(True, False)

    @pl.when((j > 0) & (j < last_j))
    def _():
        pipeline(False, False)

    @pl.when(j == last_j)
    def _():
        pipeline(False, True)


def _ffn(x2d, mod48, mod_row, norm_g, w_up, conv_w, conv_b, w_down, final_g, seq_len):
    t = x2d.shape[0]
    nj = D_FF // FFN_TILE
    gate = lambda r, j: (0, j)
    val = lambda r, j: (0, nj + j)
    const = lambda r, j: (0, 0)
    mod = lambda k: pl.BlockSpec((None, 1, D_MODEL), lambda r, j: (mod_row(r) * 6 + k, 0, 0))
    return pl.pallas_call(
        functools.partial(_ffn_kernel, seq_len=seq_len),
        grid=(t // ROW_GROUP, nj),
        in_specs=[pl.BlockSpec((ROW_GROUP, D_MODEL), lambda r, j: (r, 0)),
                  mod(3), mod(4), mod(5),
                  pl.BlockSpec((1, D_MODEL), const),
                  pl.BlockSpec((D_MODEL, FFN_TILE), gate),
                  pl.BlockSpec((D_MODEL, FFN_TILE), val),
                  pl.BlockSpec((3, FFN_TILE), gate),
                  pl.BlockSpec((3, FFN_TILE), val),
                  pl.BlockSpec((1, FFN_TILE), gate),
                  pl.BlockSpec((1, FFN_TILE), val),
                  pl.BlockSpec((FFN_TILE, D_MODEL), lambda r, j: (j, 0)),
                  pl.BlockSpec((1, D_MODEL), const)],
        out_specs=pl.BlockSpec((ROW_GROUP, D_MODEL), lambda r, j: (r, 0)),
        out_shape=jax.ShapeDtypeStruct((t, D_MODEL), F32),
        scratch_shapes=[pltpu.VMEM((ROW_GROUP, D_MODEL), BF16),
                        pltpu.VMEM((D_MODEL, 2 * FFN_TILE), BF16),
                        pltpu.VMEM((FFN_TILE, D_MODEL), BF16),
                        pltpu.VMEM((ROW_GROUP + 2 * CONV_PAD, 2 * FFN_TILE), F32)],
        compiler_params=_params("arbitrary", "arbitrary"),
        name="conv_ffn",
    )(x2d, mod48, mod48, mod48, norm_g, w_up, w_up, conv_w, conv_w, conv_b, conv_b, w_down, final_g)


def _rope_tables(seq_len):
    t = np.arange(seq_len)
    row = (t // GRID_W).astype(np.float32)
    col = (t % GRID_W).astype(np.float32)
    n = QK_ROPE // 4
    inv = (np.float32(ROPE_BASE) ** (-np.arange(n, dtype=np.float32) / np.float32(n))).astype(np.float32)
    ar, ac = row[:, None] * inv, col[:, None] * inv
    cos64 = np.concatenate([np.cos(ar), np.cos(ar), np.cos(ac), np.cos(ac)], axis=1)
    sin64 = np.concatenate([-np.sin(ar), np.sin(ar), -np.sin(ac), np.sin(ac)], axis=1)
    zeros = np.zeros_like(cos64)
    return (jnp.asarray(np.concatenate([cos64, zeros], axis=1), F32),
            jnp.asarray(np.concatenate([sin64, zeros], axis=1), F32))


def _swap_rope_halves(w):
    lead = w.shape[:-1]
    return w.reshape(lead + (2, 2, QK_ROPE // 4))[..., ::-1, :].reshape(lead + (QK_ROPE,))


def _trunk_pass(x, mod48, mod_row, wts, ctx, latent):
    b, l, _ = x.shape
    x2d = x.reshape(b * l, D_MODEL)
    proj_lo, proj_hi = _in_proj(x2d, mod48, mod_row, wts["norm_attn_g"], wts["w_in_r"], wts["ssd_conv_w"],
                                wts["ssd_conv_b"], l)
    rope = _rope_tables(l) if latent else None
    w_uq_r = wts["w_uq_lat"] if latent else wts["w_uq_ctx"]
    shape3 = lambda a, n: a.reshape(b, n, a.shape[-1])
    h0 = None
    mla_ctx = None
    if ctx is not None:
        cache_ckv, cache_krope, h0 = ctx
        mla_ctx = (cache_ckv, cache_krope)
    emit_cache = ctx is None
    mla_out = _mla(shape3(proj_hi, l), wts["q_norm_g"], wts["kv_norm_g"], w_uq_r, wts["w_uk_t"], wts["w_uv"],
                   rope, mla_ctx, emit_cache)
    attn, ckv_n, kr3 = mla_out if emit_cache else (mla_out[0], None, None)
    yz, h_t = _ssd(shape3(proj_lo, l), shape3(proj_hi, l), h0, wts["dt_bias128"], wts["a128"], wts["d_exp"])
    x1 = _merge(attn.reshape(b * l, -1), yz.reshape(b * l, -1), proj_lo, x2d, mod48, mod_row, wts["ssd_norm_g"],
                wts["w_o_mla"], wts["w_o_ssd"], wts["w_out"])
    y = _ffn(x1, mod48, mod_row, wts["norm_ffn_g"], wts["w_up"], wts["ffn_conv_w"], wts["ffn_conv_b"], wts["w_down"],
             wts["final_norm_g"], l)
    return y.reshape(b, l, D_MODEL), ckv_n, kr3, h_t


def kernel(x_prompt, x_sample, c, cache_ckv, cache_krope, state_ssd, c_ctx, w_ada, b_ada, norm_attn_g, w_in, q_norm_g,
           kv_norm_g, w_uq, w_ukv, w_o_mla, ssd_conv_w, ssd_conv_b, ssd_dt_bias, ssd_A_log, ssd_D, ssd_norm_g, w_o_ssd,
           w_out, norm_ffn_g, w_up, ffn_conv_w, ffn_conv_b, w_down, final_norm_g):
    depth = w_in.shape[0]
    assert depth == 1, "single trunk layer"
    dec_b = x_sample.shape[0]
    assert x_sample.shape[1] == ROW_GROUP and ROW_GROUP % x_prompt.shape[1] == 0
    lyr = 0

    cvec = jnp.zeros((8, D_MODEL), F32).at[0].set(c_ctx).at[1:1 + dec_b].set(c)
    mod48 = _ada(cvec, w_ada[lyr], b_ada[lyr]).reshape(8 * 6, 1, D_MODEL)

    w_in_r = _regroup_w_in(w_in[lyr].T)
    wq = w_uq[lyr].reshape(Q_LORA, N_HEADS, QK_NOPE + QK_ROPE)
    wq_nope = wq[:, :, :QK_NOPE].reshape(Q_LORA, -1)
    wq_rope = wq[:, :, QK_NOPE:]
    pad_rope = lambda w: jnp.pad(w, ((0, 0), (0, 0), (0, 128 - QK_ROPE))).reshape(Q_LORA, -1)
    w_uq_ctx = jnp.concatenate([wq_nope, pad_rope(wq_rope)], axis=1).astype(BF16)
    w_uq_lat = jnp.concatenate([wq_nope, pad_rope(wq_rope), pad_rope(_swap_rope_halves(wq_rope))], axis=1).astype(BF16)
    wkv = w_ukv[lyr].reshape(KV_LORA, N_HEADS, QK_NOPE + V_HEAD)
    w_uk_t = wkv[:, :, :QK_NOPE].reshape(KV_LORA, -1).T.astype(BF16)
    w_uv = wkv[:, :, QK_NOPE:].reshape(KV_LORA, -1).astype(BF16)
    pad128 = lambda a: jnp.pad(a.reshape(1, -1), ((0, 0), (0, 128 - a.size)))
    wts = {
        "norm_attn_g": norm_attn_g[lyr].reshape(1, -1), "w_in_r": w_in_r,
        "ssd_conv_w": ssd_conv_w[lyr], "ssd_conv_b": ssd_conv_b[lyr].reshape(1, -1),
        "q_norm_g": q_norm_g[lyr].reshape(1, -1), "kv_norm_g": kv_norm_g[lyr].reshape(1, -1),
        "w_uq_ctx": w_uq_ctx, "w_uq_lat": w_uq_lat, "w_uk_t": w_uk_t, "w_uv": w_uv,
        "dt_bias128": pad128(ssd_dt_bias[lyr]), "a128": pad128(-jnp.exp(ssd_A_log[lyr])),
        "d_exp": jnp.repeat(ssd_D[lyr], SSD_HEADDIM).reshape(1, -1),
        "ssd_norm_g": ssd_norm_g[lyr].reshape(1, -1),
        "w_o_mla": w_o_mla[lyr], "w_o_ssd": w_o_ssd[lyr], "w_out": w_out[lyr],
        "norm_ffn_g": norm_ffn_g[lyr].reshape(1, -1), "w_up": w_up[lyr],
        "ffn_conv_w": ffn_conv_w[lyr], "ffn_conv_b": ffn_conv_b[lyr].reshape(1, -1),
        "w_down": w_down[lyr], "final_norm_g": final_norm_g.reshape(1, -1),
    }

    y_p, ckv_p, kr_p, st_p = _trunk_pass(x_prompt, mod48, lambda r: 0, wts, None, False)
    ctx = (cache_ckv[:, lyr], cache_krope[:, lyr], state_ssd[:, lyr])
    y_s, _, _, _ = _trunk_pass(x_sample, mod48, lambda r: 1 + r, wts, ctx, True)
    return y_p, y_s, ckv_p[:, None], kr_p[:, None], st_p[:, None]
```

```python
import functools
import math

import jax
import jax.numpy as jnp
import numpy as np
from jax import lax
from jax.experimental import pallas as pl
from jax.experimental.pallas import tpu as pltpu

F32 = jnp.float32
BF16 = jnp.bfloat16

D_MODEL = 1024
GRID_W = 64
N_HEADS = 8
QK_NOPE = 128
QK_ROPE = 64
V_HEAD = 128
Q_LORA = 256
KV_LORA = 256
ROPE_BASE = 10000.0
SSD_HEADS = 16
SSD_HEADDIM = 64
SSD_INNER = SSD_HEADS * SSD_HEADDIM
SSD_GROUPS = 4
SSD_STATE = 64
SSD_CHUNK = 128
D_FF = 2816
EPS = 1e-6

ROW_GROUP = 2048
IN_TILE = 512
SSD_CHUNKS_PER_STEP = 8
SSD_SEQS_PER_STEP = 2
MLA_SEQS_PER_STEP = 2
FFN_TILE = 256
TOKEN_TILE = 512
ATTN_Q_TILE = 512
IN_CHUNK = 256
FFN_CHUNK = 512
FFN_DOWN_LAG = 2
CONV_PAD = 8
VMEM_LIMIT = 56 * 1024 * 1024
MLA_VMEM_LIMIT = 62 * 1024 * 1024
NEG_BIG = -1e30
LOG2_E = 1.4426950408889634


def _sigmoid(x):
    return 1.0 / (1.0 + jnp.exp(-x))


def _silu(x):
    return x * _sigmoid(x)


def _softplus(x):
    e = jnp.exp(-jnp.abs(x))
    u = 1.0 + e
    log1p_e = jnp.where(u == 1.0, e, e * jnp.log(u) / jnp.where(u == 1.0, 1.0, u - 1.0))
    return jnp.maximum(x, 0.0) + log1p_e


def _rmsnorm(x, g):
    return x * lax.rsqrt(jnp.mean(x * x, axis=-1, keepdims=True) + EPS) * g


def _dot(a, b):
    return jnp.dot(a, b, preferred_element_type=F32)


def _dot_nt(a, b):
    return lax.dot_general(a, b, (((1,), (1,)), ((), ())), preferred_element_type=F32)


def _params(*sem, vmem_limit=VMEM_LIMIT):
    return pltpu.CompilerParams(dimension_semantics=sem, vmem_limit_bytes=vmem_limit)


def _norm_mod(x, g_ref, sc_ref, sh_ref):
    return (_rmsnorm(x, g_ref[...]) * (1.0 + sc_ref[...]) + sh_ref[...]).astype(BF16)


def _zero_conv_pads(u_scr):
    zeros = jnp.zeros((CONV_PAD, u_scr.shape[1]), F32)
    u_scr[0:CONV_PAD, :] = zeros
    u_scr[CONV_PAD + ROW_GROUP:2 * CONV_PAD + ROW_GROUP, :] = zeros


def _stage_rows(c, chunk):
    return slice(CONV_PAD + c * chunk, CONV_PAD + (c + 1) * chunk)


def _dwconv3_rows(u_scr, c, chunk, cols, w_ref, b_ref, seq_len):
    r0 = c * chunk
    base = CONV_PAD + r0
    width = cols.stop - cols.start
    prev = u_scr[base - 1:base - 1 + chunk, cols]
    cur = u_scr[base:base + chunk, cols]
    nxt = u_scr[base + 1:base + 1 + chunk, cols]
    pos = (lax.broadcasted_iota(jnp.int32, (chunk, width), 0) + r0) & (seq_len - 1)
    if r0 % seq_len == 0 or chunk > seq_len:
        prev = jnp.where(pos == 0, 0.0, prev)
    if (r0 + chunk) % seq_len == 0 or chunk > seq_len:
        nxt = jnp.where(pos == seq_len - 1, 0.0, nxt)
    return prev * w_ref[0:1, :] + cur * w_ref[1:2, :] + nxt * w_ref[2:3, :] + b_ref[...]


def _ada_kernel(c_ref, w_ref, b_ref, o_ref):
    a = _silu(c_ref[...]).astype(BF16)
    o_ref[...] = _dot(a, w_ref[...].astype(BF16)) + b_ref[...]


def _ada(cvec, w_ada, b_ada):
    tn = 1536
    return pl.pallas_call(
        _ada_kernel,
        grid=(6 * D_MODEL // tn,),
        in_specs=[pl.BlockSpec((8, D_MODEL), lambda j: (0, 0)),
                  pl.BlockSpec((D_MODEL, tn), lambda j: (0, j)),
                  pl.BlockSpec((1, tn), lambda j: (0, j))],
        out_specs=pl.BlockSpec((8, tn), lambda j: (0, j)),
        out_shape=jax.ShapeDtypeStruct((8, 6 * D_MODEL), F32),
        compiler_params=_params("arbitrary"),
        name="ada_mod",
    )(cvec, w_ada, b_ada.reshape(1, -1))


IN_SPLITS = (Q_LORA, KV_LORA, QK_ROPE, SSD_INNER, SSD_INNER, SSD_GROUPS * SSD_STATE, SSD_GROUPS * SSD_STATE,
             2 * SSD_HEADS, D_MODEL, D_MODEL)
IN_OFFSETS = tuple(int(v) for v in np.cumsum((0,) + IN_SPLITS))


def _regroup_kernel(w_ref, o_ref):
    dst = 0

    def put_block(block):
        nonlocal dst
        o_ref[:, dst:dst + 128] = block.T.astype(BF16)
        dst += 128

    def piece(i):
        for off in range(0, IN_SPLITS[i], 128):
            put_block(w_ref[IN_OFFSETS[i] + off:IN_OFFSETS[i] + off + 128, :])

    def padded(parts):
        n = sum(p.shape[0] for p in parts)
        put_block(jnp.concatenate(parts + [jnp.zeros((128 - n, w_ref.shape[1]), F32)], axis=0))

    for i in (3, 8, 9, 5, 6, 4, 0, 1):
        piece(i)
    kr0, n = IN_OFFSETS[2], QK_ROPE // 4
    padded([w_ref[kr0:kr0 + QK_ROPE, :]])
    padded([w_ref[kr0 + blk * n:kr0 + (blk + 1) * n, :] for blk in (1, 0, 3, 2)])
    padded([w_ref[IN_OFFSETS[7]:IN_OFFSETS[7] + IN_SPLITS[7], :]])
    o_ref[:, dst:] = jnp.zeros((o_ref.shape[0], o_ref.shape[1] - dst), BF16)


def _regroup_w_in(w_in_t):
    cols = 256
    n_out = (N_LOW_TILES + N_F32_TILES) * IN_TILE
    return pl.pallas_call(
        _regroup_kernel,
        grid=(D_MODEL // cols,),
        in_specs=[pl.BlockSpec((w_in_t.shape[0], cols), lambda i: (0, i))],
        out_specs=pl.BlockSpec((cols, n_out), lambda i: (i, 0)),
        out_shape=jax.ShapeDtypeStruct((D_MODEL, n_out), BF16),
        compiler_params=_params("arbitrary"),
        name="w_in_regroup",
    )(w_in_t)


N_LOW_TILES, N_F32_TILES = 7, 4
Z_BLK, GM_BLK, GS_BLK = 0, 1, 2
BC_TILE = 6
XS_BLK = 0
MLA_TILE, MISC_TILE = 2, 3
DT_BLK = (MISC_TILE * IN_TILE + 256) // 128
ATTN_HEAD_COLS = 256


def _in_kernel(x_ref, sh_ref, sc_ref, g_ref, w_ref, cw_ref, cb_ref, lo_ref, hi_ref, h_scr, u_scr, *, seq_len):
    j = pl.program_id(1)
    chunk = IN_CHUNK
    n_chunks = ROW_GROUP // chunk
    rows = lambda c: slice(c * chunk, (c + 1) * chunk)

    def pointwise(fn, o_ref, stage_h=False):
        for c in range(n_chunks):
            if stage_h:
                h_scr[rows(c), :] = _norm_mod(x_ref[rows(c), :], g_ref, sc_ref, sh_ref)
            o_ref[rows(c), :] = fn(_dot(h_scr[rows(c), :], w_ref[...])).astype(o_ref.dtype)

    def conv(o_ref):
        def conv_out(c):
            v = _silu(_dwconv3_rows(u_scr, c, chunk, slice(0, IN_TILE), cw_ref, cb_ref, seq_len))
            o_ref[rows(c), :] = v.astype(o_ref.dtype)
        for c in range(n_chunks):
            u_scr[_stage_rows(c, chunk), :] = _dot(h_scr[rows(c), :], w_ref[...])
            if c >= 1:
                conv_out(c - 1)
        conv_out(n_chunks - 1)

    @pl.when(j == 0)
    def _():
        _zero_conv_pads(u_scr)
        pointwise(_silu, lo_ref, stage_h=True)

    @pl.when(j == 1)
    def _():
        pointwise(_silu, lo_ref)

    @pl.when((j >= 2) & (j <= 5))
    def _():
        pointwise(_sigmoid, lo_ref)

    @pl.when(j == BC_TILE)
    def _():
        conv(lo_ref)

    @pl.when((j == N_LOW_TILES) | (j == N_LOW_TILES + 1))
    def _():
        conv(hi_ref)

    @pl.when(j >= N_LOW_TILES + MLA_TILE)
    def _():
        pointwise(lambda u: u, hi_ref)


def _in_proj(x2d, mod48, mod_row, norm_g, w_in_r, conv_w, conv_b, seq_len):
    t = x2d.shape[0]
    n_tiles = N_LOW_TILES + N_F32_TILES
    conv_idx = lambda r, j: (0, jnp.where(j == BC_TILE, 2, jnp.clip(j - N_LOW_TILES, 0, 1)))
    return pl.pallas_call(
        functools.partial(_in_kernel, seq_len=seq_len),
        grid=(t // ROW_GROUP, n_tiles),
        in_specs=[pl.BlockSpec((ROW_GROUP, D_MODEL), lambda r, j: (r, 0)),
                  pl.BlockSpec((None, 1, D_MODEL), lambda r, j: (mod_row(r) * 6 + 0, 0, 0)),
                  pl.BlockSpec((None, 1, D_MODEL), lambda r, j: (mod_row(r) * 6 + 1, 0, 0)),
                  pl.BlockSpec((1, D_MODEL), lambda r, j: (0, 0)),
                  pl.BlockSpec((D_MODEL, IN_TILE), lambda r, j: (0, j)),
                  pl.BlockSpec((3, IN_TILE), conv_idx),
                  pl.BlockSpec((1, IN_TILE), conv_idx)],
        out_specs=[pl.BlockSpec((ROW_GROUP, IN_TILE), lambda r, j: (r, jnp.minimum(j, N_LOW_TILES - 1))),
                   pl.BlockSpec((ROW_GROUP, IN_TILE), lambda r, j: (r, jnp.maximum(j - N_LOW_TILES, 0)))],
        out_shape=[jax.ShapeDtypeStruct((t, N_LOW_TILES * IN_TILE), BF16),
                   jax.ShapeDtypeStruct((t, N_F32_TILES * IN_TILE), F32)],
        scratch_shapes=[pltpu.VMEM((ROW_GROUP, D_MODEL), BF16),
                        pltpu.VMEM((ROW_GROUP + 2 * CONV_PAD, IN_TILE), F32)],
        compiler_params=_params("arbitrary", "arbitrary"),
        name="in_proj",
    )(x2d, mod48, mod48, norm_g, w_in_r, conv_w, conv_b)


def _mla_kernel(*refs, latent, has_ctx, emit_cache, seq_len, tq, seqs):
    n_in = 7 + (2 if latent else 0) + (2 if has_ctx else 0)
    n_out = 3 if emit_cache else 1
    per_seq = {0, 1} | ({n_in - 2, n_in - 1} if has_ctx else set())
    for bi in range(seqs):
        view = lambda r: r.at[pl.ds(bi, 1)]
        ins = [view(r) if i in per_seq else r for i, r in enumerate(refs[:n_in])]
        outs = [view(r) for r in refs[n_in:n_in + n_out]]
        _mla_sequence(*ins, *outs, *refs[n_in + n_out:], latent=latent, has_ctx=has_ctx, emit_cache=emit_cache,
                      seq_len=seq_len, tq=tq)


def _mla_sequence(*refs, latent, has_ctx, emit_cache, seq_len, tq):
    refs = list(refs)
    pm_ref, px_ref, qg_ref, kvg_ref, wuq_ref, wkt_ref, wv_ref = refs[:7]
    del refs[:7]
    if latent:
        cos_ref, sin_ref = refs[:2]
        del refs[:2]
    if has_ctx:
        cckv_ref, ckr_ref = refs[:2]
        del refs[:2]
    o_ref = refs.pop(0)
    if emit_cache:
        ckv_ref, kr_ref = refs[:2]
        del refs[:2]
    k_scr, v_scr = refs[:2]
    del refs[:2]
    if has_ctx:
        kc_scr, vc_scr = refs
    t = pl.program_id(1)
    n_nope = N_HEADS * QK_NOPE

    def put_keys(k_dst, v_dst, rows, ckv_n, kr_bf):
        ckv_bf = ckv_n.astype(BF16)
        v_dst[rows, :] = _dot(ckv_bf, wv_ref[...]).astype(BF16)
        kn_t = _dot_nt(wkt_ref[...], ckv_bf).astype(BF16)
        width = kr_bf.shape[1]
        eye = jnp.where(lax.broadcasted_iota(jnp.int32, (128, width), 0)
                        == lax.broadcasted_iota(jnp.int32, (128, width), 1), 1.0, 0.0).astype(BF16)
        kr_t = _dot_nt(eye, kr_bf).astype(BF16)
        for h in range(N_HEADS):
            base = h * ATTN_HEAD_COLS
            k_dst[base:base + QK_NOPE, rows] = kn_t[h * QK_NOPE:(h + 1) * QK_NOPE, :]
            k_dst[base + QK_NOPE:base + ATTN_HEAD_COLS, rows] = kr_t

    @pl.when(t == 0)
    def _():
        step = min(seq_len, TOKEN_TILE)
        for r0 in range(0, seq_len, step):
            rows = slice(r0, r0 + step)
            ckv_n = _rmsnorm(pm_ref[0, rows, Q_LORA:], kvg_ref[...])
            kr = px_ref[0, rows, 0:128]
            if latent:
                kr = kr * cos_ref[rows, :] + px_ref[0, rows, 128:256] * sin_ref[rows, :]
            if emit_cache:
                ckv_ref[0, rows, :] = ckv_n
                kr_ref[0, rows, :] = kr[:, :QK_ROPE]
            put_keys(k_scr, v_scr, rows, ckv_n, kr.astype(BF16))
        if has_ctx:
            past = cckv_ref.shape[1]
            put_keys(kc_scr, vc_scr, slice(0, past), cckv_ref[0], ckr_ref[0].astype(BF16))

    qrows = pl.ds(pl.multiple_of(t * tq, tq), tq)
    scale = LOG2_E / math.sqrt(QK_NOPE + QK_ROPE)
    cqn = _rmsnorm(pm_ref[0, qrows, 0:Q_LORA], qg_ref[...]).astype(BF16)
    q = _dot(cqn, wuq_ref[...])
    q_rope = q[:, n_nope:2 * n_nope]
    if latent:
        q_rope = (q_rope * jnp.concatenate([cos_ref[qrows, :]] * N_HEADS, axis=1)
                  + q[:, 2 * n_nope:3 * n_nope] * jnp.concatenate([sin_ref[qrows, :]] * N_HEADS, axis=1))
    segs = ([(kc_scr, vc_scr)] if has_ctx else []) + [(k_scr, v_scr)]
    for h in range(N_HEADS):
        head = slice(h * QK_NOPE, (h + 1) * QK_NOPE)
        qk = slice(h * ATTN_HEAD_COLS, (h + 1) * ATTN_HEAD_COLS)
        qh = (jnp.concatenate([q[:, head], q_rope[:, head]], axis=1) * scale).astype(BF16)
        s = [_dot(qh, k[qk, :]) for k, _ in segs]
        m = functools.reduce(jnp.maximum, [jnp.max(si, axis=-1, keepdims=True) for si in s])
        p = [jnp.exp2(si - m) for si in s]
        l = functools.reduce(jnp.add, [jnp.sum(pi, axis=-1, keepdims=True) for pi in p])
        o = functools.reduce(jnp.add, [_dot(pi.astype(BF16), v[:, head]) for pi, (_, v) in zip(p, segs)])
        o_ref[0, :, head] = (o / l).astype(BF16)


def _mla(proj_hi, q_norm_g, kv_norm_g, w_uq_r, w_uk_t, w_uv, rope_tables, ctx, emit_cache):
    b, l, _ = proj_hi.shape
    tq = min(l, ATTN_Q_TILE)
    latent = rope_tables is not None
    has_ctx = ctx is not None
    once = dict(pipeline_mode=pl.Buffered(1))
    per_batch = once if l // tq > 1 else {}
    seqs = MLA_SEQS_PER_STEP if (l == tq and b % MLA_SEQS_PER_STEP == 0) else 1
    const2 = lambda i, t: (0, 0)
    in_specs = [pl.BlockSpec((seqs, l, IN_TILE), lambda i, t: (i, 0, MLA_TILE), **per_batch),
                pl.BlockSpec((seqs, l, 256), lambda i, t: (i, 0, MISC_TILE * IN_TILE // 256), **per_batch),
                pl.BlockSpec((1, Q_LORA), const2),
                pl.BlockSpec((1, KV_LORA), const2),
                pl.BlockSpec(w_uq_r.shape, const2, **once),
                pl.BlockSpec(w_uk_t.shape, const2, **once),
                pl.BlockSpec(w_uv.shape, const2, **once)]
    args = [proj_hi, proj_hi, q_norm_g, kv_norm_g, w_uq_r, w_uk_t, w_uv]
    scratch = [pltpu.VMEM((N_HEADS * ATTN_HEAD_COLS, l), BF16), pltpu.VMEM((l, N_HEADS * V_HEAD), BF16)]
    if latent:
        in_specs += [pl.BlockSpec((l, 128), const2, **once)] * 2
        args += list(rope_tables)
    if has_ctx:
        past = ctx[0].shape[1]
        in_specs += [pl.BlockSpec((seqs, past, KV_LORA), lambda i, t: (i, 0, 0), **per_batch),
                     pl.BlockSpec((seqs, past, QK_ROPE), lambda i, t: (i, 0, 0), **per_batch)]
        args += list(ctx)
        scratch += [pltpu.VMEM((N_HEADS * ATTN_HEAD_COLS, past), BF16), pltpu.VMEM((past, N_HEADS * V_HEAD), BF16)]
    out_specs = [pl.BlockSpec((seqs, tq, N_HEADS * V_HEAD), lambda i, t: (i, t, 0))]
    out_shape = [jax.ShapeDtypeStruct((b, l, N_HEADS * V_HEAD), BF16)]
    if emit_cache:
        out_specs += [pl.BlockSpec((seqs, l, KV_LORA), lambda i, t: (i, 0, 0)),
                      pl.BlockSpec((seqs, l, QK_ROPE), lambda i, t: (i, 0, 0))]
        out_shape += [jax.ShapeDtypeStruct((b, l, KV_LORA), F32), jax.ShapeDtypeStruct((b, l, QK_ROPE), F32)]
    return pl.pallas_call(
        functools.partial(_mla_kernel, latent=latent, has_ctx=has_ctx, emit_cache=emit_cache, seq_len=l, tq=tq,
                          seqs=seqs),
        grid=(b // seqs, l // tq),
        in_specs=in_specs,
        out_specs=out_specs,
        out_shape=out_shape,
        scratch_shapes=scratch,
        compiler_params=_params("arbitrary", "arbitrary", vmem_limit=MLA_VMEM_LIMIT),
        name="mla_attention",
    )(*args)


def _split3(x):
    hi = x.astype(BF16)
    r = x - hi.astype(F32)
    mid = r.astype(BF16)
    lo = (r - mid.astype(F32)).astype(BF16)
    return hi, mid, lo


def _exact_dot(parts, sel):
    return functools.reduce(jnp.add, [_dot(p, sel) for p in parts])


def _exact_dot_rows(sel, parts):
    return functools.reduce(jnp.add, [_dot(sel, p) for p in parts])


HEADS_PER_GROUP = SSD_HEADS // SSD_GROUPS
GROUP_COLS = HEADS_PER_GROUP * SSD_HEADDIM


def _ssd_kernel(*refs, nc, cps, has_h0, seqs):
    n_in = 8 if has_h0 else 7
    per_seq = {0, 1, 2, 3, 7} if has_h0 else {0, 1, 2, 3}
    for bi in range(seqs):
        view = lambda r: r.at[pl.ds(bi, 1)]
        ins = [view(r) if i in per_seq else r for i, r in enumerate(refs[:n_in])]
        outs = [view(r) for r in refs[n_in:n_in + 2]]
        _ssd_sequence(*ins, *outs, *refs[n_in + 2:], nc=nc, cps=cps, has_h0=has_h0, first_seq=bi == 0)


def _ssd_sequence(*refs, nc, cps, has_h0, first_seq):
    if has_h0:
        (xs_ref, zs_ref, bc_ref, dt_ref, dtb_ref, a_ref, dx_ref, h0_ref, y_ref, ht_ref,
         yl_scr, acum_scr, src_scr, tot_scr, sb_scr, h_scr, esel_scr) = refs
    else:
        (xs_ref, zs_ref, bc_ref, dt_ref, dtb_ref, a_ref, dx_ref, y_ref, ht_ref,
         yl_scr, acum_scr, src_scr, tot_scr, sb_scr, h_scr, esel_scr) = refs
    s = pl.program_id(1)
    q = SSD_CHUNK
    n_bc = SSD_GROUPS * SSD_STATE
    lane = lax.broadcasted_iota(jnp.int32, (q, 128), 1)
    low_half = lane < SSD_HEADDIM
    ii = lax.broadcasted_iota(jnp.int32, (q, q), 0)
    jj = lax.broadcasted_iota(jnp.int32, (q, q), 1)
    lower, upper = ii >= jj, ii <= jj

    def lane_bcast(parts, d):
        return _exact_dot(parts, esel_scr[d])

    def stacked_states(d):
        return [jnp.concatenate([h_scr[d, 2 * i], h_scr[d, 2 * i + 1]], axis=0).astype(BF16)
                for i in range(SSD_GROUPS // 2)]

    def group_c(bc, g):
        i, r = divmod(g, 2)
        cpair = bc[:, n_bc + i * 128:n_bc + (i + 1) * 128]
        return jnp.where(low_half if r == 0 else ~low_half, cpair, 0.0).astype(BF16)

    if first_seq:
        @pl.when((pl.program_id(0) == 0) & (s == 0))
        def _():
            k = lax.broadcasted_iota(jnp.int32, (128, SSD_INNER), 0)
            head = lax.broadcasted_iota(jnp.int32, (128, SSD_INNER), 1) // SSD_HEADDIM
            for d in range(2):
                esel_scr[d] = jnp.where(k == d * SSD_HEADS + head, 1.0, 0.0).astype(BF16)

    @pl.when(s == 0)
    def _():
        if has_h0:
            for d in range(2):
                for g in range(SSD_GROUPS):
                    hpn = h0_ref[0, d, g * HEADS_PER_GROUP:(g + 1) * HEADS_PER_GROUP].reshape(GROUP_COLS, SSD_STATE)
                    h_scr[d, g] = hpn.T
        else:
            h_scr[...] = jnp.zeros(h_scr.shape, F32)
        tri_f = jnp.where(lower, 1.0, 0.0).astype(BF16)
        tri_b = jnp.where(upper, 1.0, 0.0).astype(BF16)
        fwd_col = lane < SSD_HEADS
        for c in range(nc):
            crow = slice(c * q, (c + 1) * q)
            dt = _softplus(dt_ref[0, crow, :] + dtb_ref[...])
            parts = _split3(dt * a_ref[...])
            acum = jnp.where(fwd_col, _exact_dot_rows(tri_f, parts),
                             _exact_dot_rows(tri_b, parts))
            acum = acum * LOG2_E
            acum_scr[crow, :] = acum
            tot = jnp.where(fwd_col[0:1], acum[q - 1:q, :], acum[0:1, :])
            tot_scr[c] = jnp.broadcast_to(tot, (8, 128))
            src_scr[c] = (acum - jnp.log2(dt)).T

    def first_sweep(c, blk):
        rows = pl.ds(pl.multiple_of(c * q, q), q)
        x = xs_ref[0, blk, :]
        bc = bc_ref[0, blk, :].astype(F32)
        acum = acum_scr[rows, :]
        src_t = src_scr[c]
        tot8 = tot_scr[c]
        e = jnp.exp2(acum)
        e_hi = e.astype(BF16)
        eb_f = lane_bcast([e_hi, (e - e_hi.astype(F32)).astype(BF16)], 0)
        cd_f = jnp.exp2(lane_bcast(_split3(tot8), 0))[0:1]
        b_t = [bc[:, i * 128:(i + 1) * 128].T for i in range(SSD_GROUPS // 2)]
        h_in = stacked_states(0)
        for g in range(SSD_GROUPS):
            i, r = divmod(g, 2)
            cm = group_c(bc, g)
            scores = _dot_nt(cm, bc[:, i * 128:(i + 1) * 128].astype(BF16))
            bg_t = b_t[i][r * SSD_STATE:(r + 1) * SSD_STATE, :]
            gcols = slice(g * GROUP_COLS, (g + 1) * GROUP_COLS)
            y_off = _dot(cm, h_in[i]) * eb_f[:, gcols]
            for t in range(HEADS_PER_GROUP // 2):
                pc = slice((2 * g + t) * 128, (2 * g + t + 1) * 128)
                xp = x[:, pc]
                x2 = jnp.concatenate([jnp.where(low_half, xp, 0.0), jnp.where(low_half, 0.0, xp)], axis=0).astype(BF16)
                m, sf, sb = [], [], []
                for u in range(2):
                    kf = g * HEADS_PER_GROUP + 2 * t + u
                    kb = SSD_HEADS + kf
                    af_col = jnp.broadcast_to(acum[:, kf:kf + 1], (q, q))
                    ab_col = jnp.broadcast_to(acum[:, kb:kb + 1], (q, q))
                    sf_row, sb_row = src_t[kf:kf + 1, :], src_t[kb:kb + 1, :]
                    decay = (jnp.exp2(jnp.where(lower, af_col - sf_row, NEG_BIG))
                             + jnp.exp2(jnp.where(upper, ab_col - sb_row, NEG_BIG)))
                    m.append((scores * decay).astype(BF16))
                    wf = jnp.exp2(tot8[0:1, kf:kf + 1] - sf_row)
                    wb = jnp.exp2(tot8[0:1, kb:kb + 1] - sb_row)
                    sf.append((bg_t * wf).astype(BF16))
                    sb.append((bg_t * wb).astype(BF16))
                y_pair = _dot(jnp.concatenate(m, axis=1), x2)
                tc = slice(t * 128, (t + 1) * 128)
                yl_scr[rows, pc] = y_pair + y_off[:, tc] + dx_ref[:, pc] * xp
                h_scr[0, g, :, tc] = h_scr[0, g, :, tc] * cd_f[:, pc] + _dot(jnp.concatenate(sf, axis=1), x2)
                sb_scr[c, g, :, tc] = _dot(jnp.concatenate(sb, axis=1), x2)

    def second_sweep(c, blk):
        rows = pl.ds(pl.multiple_of(c * q, q), q)
        bc = bc_ref[0, blk, :].astype(F32)
        e = jnp.exp2(acum_scr[rows, :])
        e_hi = e.astype(BF16)
        eb_b = lane_bcast([e_hi, (e - e_hi.astype(F32)).astype(BF16)], 1)
        cd_b = jnp.exp2(lane_bcast(_split3(tot_scr[c]), 1))[0:1]
        h_in = stacked_states(1)
        for g in range(SSD_GROUPS):
            gcols = slice(g * GROUP_COLS, (g + 1) * GROUP_COLS)
            y_off = _dot(group_c(bc, g), h_in[g // 2]) * eb_b[:, gcols]
            y_ref[0, blk, gcols] = ((yl_scr[rows, gcols] + y_off) * zs_ref[0, blk, gcols].astype(F32)).astype(y_ref.dtype)
            h_scr[1, g] = h_scr[1, g] * cd_b[:, gcols] + sb_scr[c, g]

    n_steps = nc // cps
    one_step = n_steps == 1

    @pl.when(s < n_steps)
    def _():
        for ci in range(cps):
            first_sweep(s * cps + ci, slice(ci * q, (ci + 1) * q))

    @pl.when(jnp.logical_or(one_step, s >= n_steps))
    def _():
        blk_id = s * 0 if one_step else 2 * n_steps - 1 - s
        for ci in reversed(range(cps)):
            second_sweep(blk_id * cps + ci, slice(ci * q, (ci + 1) * q))

    @pl.when(jnp.logical_or(one_step, s == 2 * n_steps - 1))
    def _():
        for d in range(2):
            for g in range(SSD_GROUPS):
                ht_ref[0, d, g * HEADS_PER_GROUP:(g + 1) * HEADS_PER_GROUP] = h_scr[d, g].T.reshape(
                    HEADS_PER_GROUP, SSD_HEADDIM, SSD_STATE)


def _ssd(proj_lo, proj_hi, h0, dt_bias128, a128, d_exp):
    b, l, _ = proj_hi.shape
    q = SSD_CHUNK
    nc = l // q
    cps = min(nc, SSD_CHUNKS_PER_STEP)
    n_steps = nc // cps
    n_grid = 1 if n_steps == 1 else 2 * n_steps
    rows = cps * q
    early = lambda s: jnp.minimum(s, n_steps - 1)
    both = lambda s: jnp.where(s < n_steps, s, 2 * n_steps - 1 - s)
    late = lambda s: jnp.where(s < n_steps, n_steps - 1, 2 * n_steps - 1 - s)
    seqs = SSD_SEQS_PER_STEP if (n_steps == 1 and b % SSD_SEQS_PER_STEP == 0) else 1
    st_shape = (seqs, 2, SSD_HEADS, SSD_HEADDIM, SSD_STATE)
    st_spec = pl.BlockSpec(st_shape, lambda i, s: (i, 0, 0, 0, 0))
    has_h0 = h0 is not None
    return pl.pallas_call(
        functools.partial(_ssd_kernel, nc=nc, cps=cps, has_h0=has_h0, seqs=seqs),
        grid=(b // seqs, n_grid),
        in_specs=[pl.BlockSpec((seqs, rows, SSD_INNER), lambda i, s: (i, early(s), XS_BLK)),
                  pl.BlockSpec((seqs, rows, SSD_INNER), lambda i, s: (i, late(s), Z_BLK)),
                  pl.BlockSpec((seqs, rows, IN_TILE), lambda i, s: (i, both(s), BC_TILE)),
                  pl.BlockSpec((seqs, l, 128), lambda i, s: (i, 0, DT_BLK)),
                  pl.BlockSpec((1, 128), lambda i, s: (0, 0)),
                  pl.BlockSpec((1, 128), lambda i, s: (0, 0)),
                  pl.BlockSpec((1, SSD_INNER), lambda i, s: (0, 0))] + ([st_spec] if has_h0 else []),
        out_specs=[pl.BlockSpec((seqs, rows, SSD_INNER), lambda i, s: (i, late(s), 0)),
                   pl.BlockSpec(st_shape, lambda i, s: (i, 0, 0, 0, 0))],
        out_shape=[jax.ShapeDtypeStruct((b, l, SSD_INNER), BF16),
                   jax.ShapeDtypeStruct((b,) + st_shape[1:], F32)],
        scratch_shapes=[pltpu.VMEM((l, SSD_INNER), F32),
                        pltpu.VMEM((l, 128), F32),
                        pltpu.VMEM((nc, 128, q), F32),
                        pltpu.VMEM((nc, 8, 128), F32),
                        pltpu.VMEM((nc, SSD_GROUPS, SSD_STATE, GROUP_COLS), F32),
                        pltpu.VMEM((2, SSD_GROUPS, SSD_STATE, GROUP_COLS), F32),
                        pltpu.VMEM((2, 128, SSD_INNER), BF16)],
        compiler_params=_params("arbitrary", "arbitrary"),
        name="ssd_scan",
    )(proj_hi, proj_lo, proj_lo, proj_hi, dt_bias128, a128, d_exp, *([h0] if has_h0 else []))


def _merge_kernel(attn_ref, yz_ref, gm_ref, gs_ref, x_ref, g1_ref, ng_ref, womla_ref, wossd_ref, wout_ref, o_ref, w_scr):
    @pl.when(pl.program_id(0) == 0)
    def _():
        w_scr[0] = womla_ref[...].astype(BF16)
        w_scr[1] = wossd_ref[...].astype(BF16)
        w_scr[2] = wout_ref[...].astype(BF16)

    o_mla = _dot(attn_ref[...], w_scr[0])
    o_ssd = _dot(_rmsnorm(yz_ref[...].astype(F32), ng_ref[...]).astype(BF16), w_scr[1])
    merged = gm_ref[...].astype(F32) * o_mla + gs_ref[...].astype(F32) * o_ssd
    o_ref[...] = x_ref[...] + g1_ref[...] * _dot(merged.astype(BF16), w_scr[2])


def _merge(attn2d, yz2d, proj, x2d, mod48, mod_row, ssd_norm_g, w_o_mla, w_o_ssd, w_out):
    t = x2d.shape[0]
    tm = TOKEN_TILE
    row = lambda i: (i, 0)
    const = lambda i: (0, 0)
    wspec = pl.BlockSpec((D_MODEL, D_MODEL), const, pipeline_mode=pl.Buffered(1))
    return pl.pallas_call(
        _merge_kernel,
        grid=(t // tm,),
        in_specs=[pl.BlockSpec((tm, D_MODEL), row),
                  pl.BlockSpec((tm, D_MODEL), row),
                  pl.BlockSpec((tm, D_MODEL), lambda i: (i, GM_BLK)),
                  pl.BlockSpec((tm, D_MODEL), lambda i: (i, GS_BLK)),
                  pl.BlockSpec((tm, D_MODEL), row),
                  pl.BlockSpec((None, 1, D_MODEL), lambda i: (mod_row(i * tm // ROW_GROUP) * 6 + 2, 0, 0)),
                  pl.BlockSpec((1, D_MODEL), const),
                  wspec, wspec, wspec],
        out_specs=pl.BlockSpec((tm, D_MODEL), row),
        out_shape=jax.ShapeDtypeStruct((t, D_MODEL), F32),
        scratch_shapes=[pltpu.VMEM((3, D_MODEL, D_MODEL), BF16)],
        compiler_params=_params("arbitrary"),
        name="merge_out",
    )(attn2d, yz2d, proj, proj, x2d, mod48, ssd_norm_g, w_o_mla, w_o_ssd, w_out)


def _ffn_kernel(x_ref, sh_ref, sc_ref, g2_ref, ng_ref, wg_ref, wv_ref, cwg_ref, cwv_ref, cbg_ref, cbv_ref, wd_ref,
                fg_ref, o_ref, h_scr, wup_scr, wd_scr, u_scr, *, seq_len):
    j = pl.program_id(1)
    chunk = FFN_CHUNK
    n_chunks = ROW_GROUP // chunk
    rows = lambda c: slice(c * chunk, (c + 1) * chunk)

    wd_scr[...] = wd_ref[...].astype(BF16)

    def first_up(h_rows):
        n_k = 4
        acc = None
        for k in range(n_k):
            ks = slice(k * (D_MODEL // n_k), (k + 1) * (D_MODEL // n_k))
            wup_scr[ks, 0:FFN_TILE] = wg_ref[ks, :].astype(BF16)
            wup_scr[ks, FFN_TILE:2 * FFN_TILE] = wv_ref[ks, :].astype(BF16)
            part = _dot(h_scr[h_rows, ks], wup_scr[ks, :])
            acc = part if acc is None else acc + part
        return acc

    def gated(c):
        ug = _dwconv3_rows(u_scr, c, chunk, slice(0, FFN_TILE), cwg_ref, cbg_ref, seq_len)
        uv = _dwconv3_rows(u_scr, c, chunk, slice(FFN_TILE, 2 * FFN_TILE), cwv_ref, cbv_ref, seq_len)
        return (_silu(ug) * uv).astype(BF16)

    def pipeline(first, last):
        act = {}
        lag = FFN_DOWN_LAG
        for c in range(n_chunks + lag):
            if c < n_chunks:
                if first:
                    h_scr[rows(c), :] = _norm_mod(x_ref[rows(c), :], ng_ref, sc_ref, sh_ref)
                up = first_up(rows(c)) if c == 0 else _dot(h_scr[rows(c), :], wup_scr[...])
                u_scr[_stage_rows(c, chunk), :] = up
            if c >= lag:
                r = rows(c - lag)
                acc = _dot(act.pop(c - lag), wd_scr[...])
                if not first:
                    acc = o_ref[r, :] + acc
                if last:
                    acc = _rmsnorm(x_ref[r, :] + g2_ref[...] * acc, fg_ref[...])
                o_ref[r, :] = acc
            if 1 <= c <= n_chunks:
                act[c - 1] = gated(c - 1)

    last_j = pl.num_programs(1) - 1

    @pl.when(j == 0)
    def _():
        _zero_conv_pads(u_scr)
        pipeline(True, False)

    @pl.when((j > 0) & (j < last_j))
    def _():
        pipeline(False, False)

    @pl.when(j == last_j)
    def _():
        pipeline(False, True)


def _ffn(x2d, mod48, mod_row, norm_g, w_up, conv_w, conv_b, w_down, final_g, seq_len):
    t = x2d.shape[0]
    nj = D_FF // FFN_TILE
    gate = lambda r, j: (0, j)
    val = lambda r, j: (0, nj + j)
    const = lambda r, j: (0, 0)
    mod = lambda k: pl.BlockSpec((None, 1, D_MODEL), lambda r, j: (mod_row(r) * 6 + k, 0, 0))
    return pl.pallas_call(
        functools.partial(_ffn_kernel, seq_len=seq_len),
        grid=(t // ROW_GROUP, nj),
        in_specs=[pl.BlockSpec((ROW_GROUP, D_MODEL), lambda r, j: (r, 0)),
                  mod(3), mod(4), mod(5),
                  pl.BlockSpec((1, D_MODEL), const),
                  pl.BlockSpec((D_MODEL, FFN_TILE), gate),
                  pl.BlockSpec((D_MODEL, FFN_TILE), val),
                  pl.BlockSpec((3, FFN_TILE), gate),
                  pl.BlockSpec((3, FFN_TILE), val),
                  pl.BlockSpec((1, FFN_TILE), gate),
                  pl.BlockSpec((1, FFN_TILE), val),
                  pl.BlockSpec((FFN_TILE, D_MODEL), lambda r, j: (j, 0)),
                  pl.BlockSpec((1, D_MODEL), const)],
        out_specs=pl.BlockSpec((ROW_GROUP, D_MODEL), lambda r, j: (r, 0)),
        out_shape=jax.ShapeDtypeStruct((t, D_MODEL), F32),
        scratch_shapes=[pltpu.VMEM((ROW_GROUP, D_MODEL), BF16),
                        pltpu.VMEM((D_MODEL, 2 * FFN_TILE), BF16),
                        pltpu.VMEM((FFN_TILE, D_MODEL), BF16),
                        pltpu.VMEM((ROW_GROUP + 2 * CONV_PAD, 2 * FFN_TILE), F32)],
        compiler_params=_params("arbitrary", "arbitrary"),
        name="conv_ffn",
    )(x2d, mod48, mod48, mod48, norm_g, w_up, w_up, conv_w, conv_w, conv_b, conv_b, w_down, final_g)


def _rope_tables(seq_len):
    t = np.arange(seq_len)
    row = (t // GRID_W).astype(np.float32)
    col = (t % GRID_W).astype(np.float32)
    n = QK_ROPE // 4
    inv = (np.float32(ROPE_BASE) ** (-np.arange(n, dtype=np.float32) / np.float32(n))).astype(np.float32)
    ar, ac = row[:, None] * inv, col[:, None] * inv
    cos64 = np.concatenate([np.cos(ar), np.cos(ar), np.cos(ac), np.cos(ac)], axis=1)
    sin64 = np.concatenate([-np.sin(ar), np.sin(ar), -np.sin(ac), np.sin(ac)], axis=1)
    zeros = np.zeros_like(cos64)
    return (jnp.asarray(np.concatenate([cos64, zeros], axis=1), F32),
            jnp.asarray(np.concatenate([sin64, zeros], axis=1), F32))


def _swap_rope_halves(w):
    lead = w.shape[:-1]
    return w.reshape(lead + (2, 2, QK_ROPE // 4))[..., ::-1, :].reshape(lead + (QK_ROPE,))


def _trunk_pass(x, mod48, mod_row, wts, ctx, latent):
    b, l, _ = x.shape
    x2d = x.reshape(b * l, D_MODEL)
    proj_lo, proj_hi = _in_proj(x2d, mod48, mod_row, wts["norm_attn_g"], wts["w_in_r"], wts["ssd_conv_w"],
                                wts["ssd_conv_b"], l)
    rope = _rope_tables(l) if latent else None
    w_uq_r = wts["w_uq_lat"] if latent else wts["w_uq_ctx"]
    shape3 = lambda a, n: a.reshape(b, n, a.shape[-1])
    h0 = None
    mla_ctx = None
    if ctx is not None:
        cache_ckv, cache_krope, h0 = ctx
        mla_ctx = (cache_ckv, cache_krope)
    emit_cache = ctx is None
    mla_out = _mla(shape3(proj_hi, l), wts["q_norm_g"], wts["kv_norm_g"], w_uq_r, wts["w_uk_t"], wts["w_uv"],
                   rope, mla_ctx, emit_cache)
    attn, ckv_n, kr3 = mla_out if emit_cache else (mla_out[0], None, None)
    yz, h_t = _ssd(shape3(proj_lo, l), shape3(proj_hi, l), h0, wts["dt_bias128"], wts["a128"], wts["d_exp"])
    x1 = _merge(attn.reshape(b * l, -1), yz.reshape(b * l, -1), proj_lo, x2d, mod48, mod_row, wts["ssd_norm_g"],
                wts["w_o_mla"], wts["w_o_ssd"], wts["w_out"])
    y = _ffn(x1, mod48, mod_row, wts["norm_ffn_g"], wts["w_up"], wts["ffn_conv_w"], wts["ffn_conv_b"], wts["w_down"],
             wts["final_norm_g"], l)
    return y.reshape(b, l, D_MODEL), ckv_n, kr3, h_t


def kernel(x_prompt, x_sample, c, cache_ckv, cache_krope, state_ssd, c_ctx, w_ada, b_ada, norm_attn_g, w_in, q_norm_g,
           kv_norm_g, w_uq, w_ukv, w_o_mla, ssd_conv_w, ssd_conv_b, ssd_dt_bias, ssd_A_log, ssd_D, ssd_norm_g, w_o_ssd,
           w_out, norm_ffn_g, w_up, ffn_conv_w, ffn_conv_b, w_down, final_norm_g):
    depth = w_in.shape[0]
    assert depth == 1, "single trunk layer"
    dec_b = x_sample.shape[0]
    assert x_sample.shape[1] == ROW_GROUP and ROW_GROUP % x_prompt.shape[1] == 0
    lyr = 0

    cvec = jnp.zeros((8, D_MODEL), F32).at[0].set(c_ctx).at[1:1 + dec_b].set(c)
    mod48 = _ada(cvec, w_ada[lyr], b_ada[lyr]).reshape(8 * 6, 1, D_MODEL)

    w_in_r = _regroup_w_in(w_in[lyr].T)
    wq = w_uq[lyr].reshape(Q_LORA, N_HEADS, QK_NOPE + QK_ROPE)
    wq_nope = wq[:, :, :QK_NOPE].reshape(Q_LORA, -1)
    wq_rope = wq[:, :, QK_NOPE:]
    pad_rope = lambda w: jnp.pad(w, ((0, 0), (0, 0), (0, 128 - QK_ROPE))).reshape(Q_LORA, -1)
    w_uq_ctx = jnp.concatenate([wq_nope, pad_rope(wq_rope)], axis=1).astype(BF16)
    w_uq_lat = jnp.concatenate([wq_nope, pad_rope(wq_rope), pad_rope(_swap_rope_halves(wq_rope))], axis=1).astype(BF16)
    wkv = w_ukv[lyr].reshape(KV_LORA, N_HEADS, QK_NOPE + V_HEAD)
    w_uk_t = wkv[:, :, :QK_NOPE].reshape(KV_LORA, -1).T.astype(BF16)
    w_uv = wkv[:, :, QK_NOPE:].reshape(KV_LORA, -1).astype(BF16)
    pad128 = lambda a: jnp.pad(a.reshape(1, -1), ((0, 0), (0, 128 - a.size)))
    wts = {
        "norm_attn_g": norm_attn_g[lyr].reshape(1, -1), "w_in_r": w_in_r,
        "ssd_conv_w": ssd_conv_w[lyr], "ssd_conv_b": ssd_conv_b[lyr].reshape(1, -1),
        "q_norm_g": q_norm_g[lyr].reshape(1, -1), "kv_norm_g": kv_norm_g[lyr].reshape(1, -1),
        "w_uq_ctx": w_uq_ctx, "w_uq_lat": w_uq_lat, "w_uk_t": w_uk_t, "w_uv": w_uv,
        "dt_bias128": pad128(ssd_dt_bias[lyr]), "a128": pad128(-jnp.exp(ssd_A_log[lyr])),
        "d_exp": jnp.repeat(ssd_D[lyr], SSD_HEADDIM).reshape(1, -1),
        "ssd_norm_g": ssd_norm_g[lyr].reshape(1, -1),
        "w_o_mla": w_o_mla[lyr], "w_o_ssd": w_o_ssd[lyr], "w_out": w_out[lyr],
        "norm_ffn_g": norm_ffn_g[lyr].reshape(1, -1), "w_up": w_up[lyr],
        "ffn_conv_w": ffn_conv_w[lyr], "ffn_conv_b": ffn_conv_b[lyr].reshape(1, -1),
        "w_down": w_down[lyr], "final_norm_g": final_norm_g.reshape(1, -1),
    }

    y_p, ckv_p, kr_p, st_p = _trunk_pass(x_prompt, mod48, lambda r: 0, wts, None, False)
    ctx = (cache_ckv[:, lyr], cache_krope[:, lyr], state_ssd[:, lyr])
    y_s, _, _, _ = _trunk_pass(x_sample, mod48, lambda r: 1 + r, wts, ctx, True)
    return y_p, y_s, ckv_p[:, None], kr_p[:, None], st_p[:, None]
```

```python
import functools
import math

import jax
import jax.numpy as jnp
import numpy as np
from jax import lax
from jax.experimental import pallas as pl
from jax.experimental.pallas import tpu as pltpu

F32 = jnp.float32
BF16 = jnp.bfloat16

D_MODEL = 1024
GRID_W = 64
N_HEADS = 8
QK_NOPE = 128
QK_ROPE = 64
V_HEAD = 128
Q_LORA = 256
KV_LORA = 256
ROPE_BASE = 10000.0
SSD_HEADS = 16
SSD_HEADDIM = 64
SSD_INNER = SSD_HEADS * SSD_HEADDIM
SSD_GROUPS = 4
SSD_STATE = 64
SSD_CHUNK = 128
D_FF = 2816
EPS = 1e-6

ROW_GROUP = 2048
IN_TILE = 512
SSD_CHUNKS_PER_STEP = 8
SSD_SEQS_PER_STEP = 2
MLA_SEQS_PER_STEP = 2
FFN_TILE = 256
TOKEN_TILE = 512
ATTN_Q_TILE = 512
IN_CHUNK = 256
FFN_CHUNK = 512
FFN_DOWN_LAG = 2
CONV_PAD = 8
VMEM_LIMIT = 56 * 1024 * 1024
MLA_VMEM_LIMIT = 62 * 1024 * 1024
NEG_BIG = -1e30
LOG2_E = 1.4426950408889634


def _sigmoid(x):
    return 1.0 / (1.0 + jnp.exp(-x))


def _silu(x):
    return x * _sigmoid(x)


def _softplus(x):
    e = jnp.exp(-jnp.abs(x))
    u = 1.0 + e
    log1p_e = jnp.where(u == 1.0, e, e * jnp.log(u) / jnp.where(u == 1.0, 1.0, u - 1.0))
    return jnp.maximum(x, 0.0) + log1p_e


def _rmsnorm(x, g):
    return x * lax.rsqrt(jnp.mean(x * x, axis=-1, keepdims=True) + EPS) * g


def _dot(a, b):
    return jnp.dot(a, b, preferred_element_type=F32)


def _dot_nt(a, b):
    return lax.dot_general(a, b, (((1,), (1,)), ((), ())), preferred_element_type=F32)


def _params(*sem, vmem_limit=VMEM_LIMIT):
    return pltpu.CompilerParams(dimension_semantics=sem, vmem_limit_bytes=vmem_limit)


def _norm_mod(x, g_ref, sc_ref, sh_ref):
    return (_rmsnorm(x, g_ref[...]) * (1.0 + sc_ref[...]) + sh_ref[...]).astype(BF16)


def _zero_conv_pads(u_scr):
    zeros = jnp.zeros((CONV_PAD, u_scr.shape[1]), F32)
    u_scr[0:CONV_PAD, :] = zeros
    u_scr[CONV_PAD + ROW_GROUP:2 * CONV_PAD + ROW_GROUP, :] = zeros


def _stage_rows(rows):
    return slice(CONV_PAD + rows.start, CONV_PAD + rows.stop)


def _dwconv3_rows(u_scr, rows, cols, w_ref, b_ref, seq_len):
    r0, n = rows.start, rows.stop - rows.start
    base = CONV_PAD + r0
    width = cols.stop - cols.start
    prev = u_scr[base - 1:base - 1 + n, cols]
    cur = u_scr[base:base + n, cols]
    nxt = u_scr[base + 1:base + 1 + n, cols]
    pos = (lax.broadcasted_iota(jnp.int32, (n, width), 0) + r0) & (seq_len - 1)
    if (-r0) % seq_len < n:
        prev = jnp.where(pos == 0, 0.0, prev)
    if (seq_len - 1 - r0) % seq_len < n:
        nxt = jnp.where(pos == seq_len - 1, 0.0, nxt)
    return prev * w_ref[0:1, :] + cur * w_ref[1:2, :] + nxt * w_ref[2:3, :] + b_ref[...]


def _ada_kernel(c_ref, w_ref, b_ref, o_ref):
    a = _silu(c_ref[...]).astype(BF16)
    o_ref[...] = _dot(a, w_ref[...].astype(BF16)) + b_ref[...]


def _ada(cvec, w_ada, b_ada):
    tn = 1536
    return pl.pallas_call(
        _ada_kernel,
        grid=(6 * D_MODEL // tn,),
        in_specs=[pl.BlockSpec((8, D_MODEL), lambda j: (0, 0)),
                  pl.BlockSpec((D_MODEL, tn), lambda j: (0, j)),
                  pl.BlockSpec((1, tn), lambda j: (0, j))],
        out_specs=pl.BlockSpec((8, tn), lambda j: (0, j)),
        out_shape=jax.ShapeDtypeStruct((8, 6 * D_MODEL), F32),
        compiler_params=_params("arbitrary"),
        name="ada_mod",
    )(cvec, w_ada, b_ada.reshape(1, -1))


IN_SPLITS = (Q_LORA, KV_LORA, QK_ROPE, SSD_INNER, SSD_INNER, SSD_GROUPS * SSD_STATE, SSD_GROUPS * SSD_STATE,
             2 * SSD_HEADS, D_MODEL, D_MODEL)
IN_OFFSETS = tuple(int(v) for v in np.cumsum((0,) + IN_SPLITS))


def _regroup_kernel(w_ref, o_ref):
    dst = 0

    def put_block(block):
        nonlocal dst
        o_ref[:, dst:dst + 128] = block.T.astype(BF16)
        dst += 128

    def piece(i):
        for off in range(0, IN_SPLITS[i], 128):
            put_block(w_ref[IN_OFFSETS[i] + off:IN_OFFSETS[i] + off + 128, :])

    def padded(parts):
        n = sum(p.shape[0] for p in parts)
        put_block(jnp.concatenate(parts + [jnp.zeros((128 - n, w_ref.shape[1]), F32)], axis=0))

    for i in (3, 8, 9, 5, 6, 4, 0, 1):
        piece(i)
    kr0, n = IN_OFFSETS[2], QK_ROPE // 4
    padded([w_ref[kr0:kr0 + QK_ROPE, :]])
    padded([w_ref[kr0 + blk * n:kr0 + (blk + 1) * n, :] for blk in (1, 0, 3, 2)])
    padded([w_ref[IN_OFFSETS[7]:IN_OFFSETS[7] + IN_SPLITS[7], :]])
    o_ref[:, dst:] = jnp.zeros((o_ref.shape[0], o_ref.shape[1] - dst), BF16)


def _regroup_w_in(w_in_t):
    cols = 256
    n_out = (N_LOW_TILES + N_F32_TILES) * IN_TILE
    return pl.pallas_call(
        _regroup_kernel,
        grid=(D_MODEL // cols,),
        in_specs=[pl.BlockSpec((w_in_t.shape[0], cols), lambda i: (0, i))],
        out_specs=pl.BlockSpec((cols, n_out), lambda i: (i, 0)),
        out_shape=jax.ShapeDtypeStruct((D_MODEL, n_out), BF16),
        compiler_params=_params("arbitrary"),
        name="w_in_regroup",
    )(w_in_t)


N_LOW_TILES, N_F32_TILES = 7, 4
Z_BLK, GM_BLK, GS_BLK = 0, 1, 2
BC_TILE = 6
XS_BLK = 0
MLA_TILE, MISC_TILE = 2, 3
DT_BLK = (MISC_TILE * IN_TILE + 256) // 128
ATTN_HEAD_COLS = 256


def _in_kernel(x_ref, sh_ref, sc_ref, g_ref, w_ref, cw_ref, cb_ref, lo_ref, hi_ref, h_scr, u_scr, *, seq_len):
    j = pl.program_id(1)
    chunk = IN_CHUNK
    n_chunks = ROW_GROUP // chunk
    rows = lambda c: slice(c * chunk, (c + 1) * chunk)

    def pointwise(fn, o_ref, stage_h=False):
        for c in range(n_chunks):
            if stage_h:
                h_scr[rows(c), :] = _norm_mod(x_ref[rows(c), :], g_ref, sc_ref, sh_ref)
            o_ref[rows(c), :] = fn(_dot(h_scr[rows(c), :], w_ref[...])).astype(o_ref.dtype)

    def conv(o_ref):
        def conv_out(c):
            v = _silu(_dwconv3_rows(u_scr, rows(c), slice(0, IN_TILE), cw_ref, cb_ref, seq_len))
            o_ref[rows(c), :] = v.astype(o_ref.dtype)
        for c in range(n_chunks):
            u_scr[_stage_rows(rows(c)), :] = _dot(h_scr[rows(c), :], w_ref[...])
            if c >= 1:
                conv_out(c - 1)
        conv_out(n_chunks - 1)

    @pl.when(j == 0)
    def _():
        _zero_conv_pads(u_scr)
        pointwise(_silu, lo_ref, stage_h=True)

    @pl.when(j == 1)
    def _():
        pointwise(_silu, lo_ref)

    @pl.when((j >= 2) & (j <= 5))
    def _():
        pointwise(_sigmoid, lo_ref)

    @pl.when(j == BC_TILE)
    def _():
        conv(lo_ref)

    @pl.when((j == N_LOW_TILES) | (j == N_LOW_TILES + 1))
    def _():
        conv(hi_ref)

    @pl.when(j >= N_LOW_TILES + MLA_TILE)
    def _():
        pointwise(lambda u: u, hi_ref)


def _in_proj(x2d, mod48, mod_row, norm_g, w_in_r, conv_w, conv_b, seq_len):
    t = x2d.shape[0]
    n_tiles = N_LOW_TILES + N_F32_TILES
    conv_idx = lambda r, j: (0, jnp.where(j == BC_TILE, 2, jnp.clip(j - N_LOW_TILES, 0, 1)))
    return pl.pallas_call(
        functools.partial(_in_kernel, seq_len=seq_len),
        grid=(t // ROW_GROUP, n_tiles),
        in_specs=[pl.BlockSpec((ROW_GROUP, D_MODEL), lambda r, j: (r, 0)),
                  pl.BlockSpec((None, 1, D_MODEL), lambda r, j: (mod_row(r) * 6 + 0, 0, 0)),
                  pl.BlockSpec((None, 1, D_MODEL), lambda r, j: (mod_row(r) * 6 + 1, 0, 0)),
                  pl.BlockSpec((1, D_MODEL), lambda r, j: (0, 0)),
                  pl.BlockSpec((D_MODEL, IN_TILE), lambda r, j: (0, j)),
                  pl.BlockSpec((3, IN_TILE), conv_idx),
                  pl.BlockSpec((1, IN_TILE), conv_idx)],
        out_specs=[pl.BlockSpec((ROW_GROUP, IN_TILE), lambda r, j: (r, jnp.minimum(j, N_LOW_TILES - 1))),
                   pl.BlockSpec((ROW_GROUP, IN_TILE), lambda r, j: (r, jnp.maximum(j - N_LOW_TILES, 0)))],
        out_shape=[jax.ShapeDtypeStruct((t, N_LOW_TILES * IN_TILE), BF16),
                   jax.ShapeDtypeStruct((t, N_F32_TILES * IN_TILE), F32)],
        scratch_shapes=[pltpu.VMEM((ROW_GROUP, D_MODEL), BF16),
                        pltpu.VMEM((ROW_GROUP + 2 * CONV_PAD, IN_TILE), F32)],
        compiler_params=_params("arbitrary", "arbitrary"),
        name="in_proj",
    )(x2d, mod48, mod48, norm_g, w_in_r, conv_w, conv_b)


def _mla_kernel(*refs, latent, has_ctx, emit_cache, seq_len, tq, seqs):
    n_in = 7 + (2 if latent else 0) + (2 if has_ctx else 0)
    n_out = 3 if emit_cache else 1
    per_seq = {0, 1} | ({n_in - 2, n_in - 1} if has_ctx else set())
    for bi in range(seqs):
        view = lambda r: r.at[pl.ds(bi, 1)]
        ins = [view(r) if i in per_seq else r for i, r in enumerate(refs[:n_in])]
        outs = [view(r) for r in refs[n_in:n_in + n_out]]
        _mla_sequence(*ins, *outs, *refs[n_in + n_out:], latent=latent, has_ctx=has_ctx, emit_cache=emit_cache,
                      seq_len=seq_len, tq=tq)


def _mla_sequence(*refs, latent, has_ctx, emit_cache, seq_len, tq):
    refs = list(refs)
    pm_ref, px_ref, qg_ref, kvg_ref, wuq_ref, wkt_ref, wv_ref = refs[:7]
    del refs[:7]
    if latent:
        cos_ref, sin_ref = refs[:2]
        del refs[:2]
    if has_ctx:
        cckv_ref, ckr_ref = refs[:2]
        del refs[:2]
    o_ref = refs.pop(0)
    if emit_cache:
        ckv_ref, kr_ref = refs[:2]
        del refs[:2]
    k_scr, v_scr = refs[:2]
    del refs[:2]
    if has_ctx:
        kc_scr, vc_scr = refs
    t = pl.program_id(1)
    n_nope = N_HEADS * QK_NOPE

    def put_keys(k_dst, v_dst, rows, ckv_n, kr_bf):
        ckv_bf = ckv_n.astype(BF16)
        v_dst[rows, :] = _dot(ckv_bf, wv_ref[...]).astype(BF16)
        kn_t = _dot_nt(wkt_ref[...], ckv_bf).astype(BF16)
        width = kr_bf.shape[1]
        eye = jnp.where(lax.broadcasted_iota(jnp.int32, (128, width), 0)
                        == lax.broadcasted_iota(jnp.int32, (128, width), 1), 1.0, 0.0).astype(BF16)
        kr_t = _dot_nt(eye, kr_bf).astype(BF16)
        for h in range(N_HEADS):
            base = h * ATTN_HEAD_COLS
            k_dst[base:base + QK_NOPE, rows] = kn_t[h * QK_NOPE:(h + 1) * QK_NOPE, :]
            k_dst[base + QK_NOPE:base + ATTN_HEAD_COLS, rows] = kr_t

    @pl.when(t == 0)
    def _():
        step = min(seq_len, TOKEN_TILE)
        for r0 in range(0, seq_len, step):
            rows = slice(r0, r0 + step)
            ckv_n = _rmsnorm(pm_ref[0, rows, Q_LORA:], kvg_ref[...])
            kr = px_ref[0, rows, 0:128]
            if latent:
                kr = kr * cos_ref[rows, :] + px_ref[0, rows, 128:256] * sin_ref[rows, :]
            if emit_cache:
                ckv_ref[0, rows, :] = ckv_n
                kr_ref[0, rows, :] = kr[:, :QK_ROPE]
            put_keys(k_scr, v_scr, rows, ckv_n, kr.astype(BF16))
        if has_ctx:
            past = cckv_ref.shape[1]
            put_keys(kc_scr, vc_scr, slice(0, past), cckv_ref[0], ckr_ref[0].astype(BF16))

    qrows = pl.ds(pl.multiple_of(t * tq, tq), tq)
    scale = LOG2_E / math.sqrt(QK_NOPE + QK_ROPE)
    cqn = _rmsnorm(pm_ref[0, qrows, 0:Q_LORA], qg_ref[...]).astype(BF16)
    q = _dot(cqn, wuq_ref[...])
    q_rope = q[:, n_nope:2 * n_nope]
    if latent:
        q_rope = (q_rope * jnp.concatenate([cos_ref[qrows, :]] * N_HEADS, axis=1)
                  + q[:, 2 * n_nope:3 * n_nope] * jnp.concatenate([sin_ref[qrows, :]] * N_HEADS, axis=1))
    segs = ([(kc_scr, vc_scr)] if has_ctx else []) + [(k_scr, v_scr)]
    for h in range(N_HEADS):
        head = slice(h * QK_NOPE, (h + 1) * QK_NOPE)
        qk = slice(h * ATTN_HEAD_COLS, (h + 1) * ATTN_HEAD_COLS)
        qh = (jnp.concatenate([q[:, head], q_rope[:, head]], axis=1) * scale).astype(BF16)
        s = [_dot(qh, k[qk, :]) for k, _ in segs]
        m = functools.reduce(jnp.maximum, [jnp.max(si, axis=-1, keepdims=True) for si in s])
        p = [jnp.exp2(si - m) for si in s]
        l = functools.reduce(jnp.add, [jnp.sum(pi, axis=-1, keepdims=True) for pi in p])
        o = functools.reduce(jnp.add, [_dot(pi.astype(BF16), v[:, head]) for pi, (_, v) in zip(p, segs)])
        o_ref[0, :, head] = (o / l).astype(BF16)


def _mla(proj_hi, q_norm_g, kv_norm_g, w_uq_r, w_uk_t, w_uv, rope_tables, ctx, emit_cache):
    b, l, _ = proj_hi.shape
    tq = min(l, ATTN_Q_TILE)
    latent = rope_tables is not None
    has_ctx = ctx is not None
    once = dict(pipeline_mode=pl.Buffered(1))
    per_batch = once if l // tq > 1 else {}
    seqs = MLA_SEQS_PER_STEP if (l == tq and b % MLA_SEQS_PER_STEP == 0) else 1
    const2 = lambda i, t: (0, 0)
    in_specs = [pl.BlockSpec((seqs, l, IN_TILE), lambda i, t: (i, 0, MLA_TILE), **per_batch),
                pl.BlockSpec((seqs, l, 256), lambda i, t: (i, 0, MISC_TILE * IN_TILE // 256), **per_batch),
                pl.BlockSpec((1, Q_LORA), const2),
                pl.BlockSpec((1, KV_LORA), const2),
                pl.BlockSpec(w_uq_r.shape, const2, **once),
                pl.BlockSpec(w_uk_t.shape, const2, **once),
                pl.BlockSpec(w_uv.shape, const2, **once)]
    args = [proj_hi, proj_hi, q_norm_g, kv_norm_g, w_uq_r, w_uk_t, w_uv]
    scratch = [pltpu.VMEM((N_HEADS * ATTN_HEAD_COLS, l), BF16), pltpu.VMEM((l, N_HEADS * V_HEAD), BF16)]
    if latent:
        in_specs += [pl.BlockSpec((l, 128), const2, **once)] * 2
        args += list(rope_tables)
    if has_ctx:
        past = ctx[0].shape[1]
        in_specs += [pl.BlockSpec((seqs, past, KV_LORA), lambda i, t: (i, 0, 0), **per_batch),
                     pl.BlockSpec((seqs, past, QK_ROPE), lambda i, t: (i, 0, 0), **per_batch)]
        args += list(ctx)
        scratch += [pltpu.VMEM((N_HEADS * ATTN_HEAD_COLS, past), BF16), pltpu.VMEM((past, N_HEADS * V_HEAD), BF16)]
    out_specs = [pl.BlockSpec((seqs, tq, N_HEADS * V_HEAD), lambda i, t: (i, t, 0))]
    out_shape = [jax.ShapeDtypeStruct((b, l, N_HEADS * V_HEAD), BF16)]
    if emit_cache:
        out_specs += [pl.BlockSpec((seqs, l, KV_LORA), lambda i, t: (i, 0, 0)),
                      pl.BlockSpec((seqs, l, QK_ROPE), lambda i, t: (i, 0, 0))]
        out_shape += [jax.ShapeDtypeStruct((b, l, KV_LORA), F32), jax.ShapeDtypeStruct((b, l, QK_ROPE), F32)]
    return pl.pallas_call(
        functools.partial(_mla_kernel, latent=latent, has_ctx=has_ctx, emit_cache=emit_cache, seq_len=l, tq=tq,
                          seqs=seqs),
        grid=(b // seqs, l // tq),
        in_specs=in_specs,
        out_specs=out_specs,
        out_shape=out_shape,
        scratch_shapes=scratch,
        compiler_params=_params("arbitrary", "arbitrary", vmem_limit=MLA_VMEM_LIMIT),
        name="mla_attention",
    )(*args)


def _split3(x):
    hi = x.astype(BF16)
    r = x - hi.astype(F32)
    mid = r.astype(BF16)
    lo = (r - mid.astype(F32)).astype(BF16)
    return hi, mid, lo


def _exact_dot(parts, sel):
    return functools.reduce(jnp.add, [_dot(p, sel) for p in parts])


def _exact_dot_rows(sel, parts):
    return functools.reduce(jnp.add, [_dot(sel, p) for p in parts])


HEADS_PER_GROUP = SSD_HEADS // SSD_GROUPS
GROUP_COLS = HEADS_PER_GROUP * SSD_HEADDIM


def _ssd_kernel(*refs, nc, cps, has_h0, seqs):
    n_in = 8 if has_h0 else 7
    per_seq = {0, 1, 2, 3, 7} if has_h0 else {0, 1, 2, 3}
    for bi in range(seqs):
        view = lambda r: r.at[pl.ds(bi, 1)]
        ins = [view(r) if i in per_seq else r for i, r in enumerate(refs[:n_in])]
        outs = [view(r) for r in refs[n_in:n_in + 2]]
        _ssd_sequence(*ins, *outs, *refs[n_in + 2:], nc=nc, cps=cps, has_h0=has_h0)


def _ssd_sequence(*refs, nc, cps, has_h0):
    if has_h0:
        (xs_ref, zs_ref, bc_ref, dt_ref, dtb_ref, a_ref, dx_ref, h0_ref, y_ref, ht_ref,
         yl_scr, acum_scr, src_scr, tot_scr, sb_scr, h_scr, esel_scr) = refs
    else:
        (xs_ref, zs_ref, bc_ref, dt_ref, dtb_ref, a_ref, dx_ref, y_ref, ht_ref,
         yl_scr, acum_scr, src_scr, tot_scr, sb_scr, h_scr, esel_scr) = refs
    s = pl.program_id(1)
    q = SSD_CHUNK
    n_bc = SSD_GROUPS * SSD_STATE
    lane = lax.broadcasted_iota(jnp.int32, (q, 128), 1)
    low_half = lane < SSD_HEADDIM
    ii = lax.broadcasted_iota(jnp.int32, (q, q), 0)
    jj = lax.broadcasted_iota(jnp.int32, (q, q), 1)
    lower, upper = ii >= jj, ii <= jj

    def lane_bcast(parts, d):
        return _exact_dot(parts, esel_scr[d])

    def stacked_states(d):
        return [jnp.concatenate([h_scr[d, 2 * i], h_scr[d, 2 * i + 1]], axis=0).astype(BF16)
                for i in range(SSD_GROUPS // 2)]

    def group_c(bc, g):
        i, r = divmod(g, 2)
        cpair = bc[:, n_bc + i * 128:n_bc + (i + 1) * 128]
        return jnp.where(low_half if r == 0 else ~low_half, cpair, 0.0).astype(BF16)

    @pl.when(s == 0)
    def _():
        k = lax.broadcasted_iota(jnp.int32, (128, SSD_INNER), 0)
        head = lax.broadcasted_iota(jnp.int32, (128, SSD_INNER), 1) // SSD_HEADDIM
        for d in range(2):
            esel_scr[d] = jnp.where(k == d * SSD_HEADS + head, 1.0, 0.0).astype(BF16)
        if has_h0:
            for d in range(2):
                for g in range(SSD_GROUPS):
                    hpn = h0_ref[0, d, g * HEADS_PER_GROUP:(g + 1) * HEADS_PER_GROUP].reshape(GROUP_COLS, SSD_STATE)
                    h_scr[d, g] = hpn.T
        else:
            h_scr[...] = jnp.zeros(h_scr.shape, F32)
        tri_f = jnp.where(lower, 1.0, 0.0).astype(BF16)
        tri_b = jnp.where(upper, 1.0, 0.0).astype(BF16)
        fwd_col = lane < SSD_HEADS
        for c in range(nc):
            crow = slice(c * q, (c + 1) * q)
            dt = _softplus(dt_ref[0, crow, :] + dtb_ref[...])
            parts = _split3(dt * a_ref[...])
            acum = jnp.where(fwd_col, _exact_dot_rows(tri_f, parts),
                             _exact_dot_rows(tri_b, parts))
            acum = acum * LOG2_E
            acum_scr[crow, :] = acum
            tot = jnp.where(fwd_col[0:1], acum[q - 1:q, :], acum[0:1, :])
            tot_scr[c] = jnp.broadcast_to(tot, (8, 128))
            src_scr[c] = (acum - jnp.log2(dt)).T

    def first_sweep(c, blk):
        rows = pl.ds(pl.multiple_of(c * q, q), q)
        x = xs_ref[0, blk, :]
        bc = bc_ref[0, blk, :].astype(F32)
        acum = acum_scr[rows, :]
        src_t = src_scr[c]
        tot8 = tot_scr[c]
        e = jnp.exp2(acum)
        e_hi = e.astype(BF16)
        eb_f = lane_bcast([e_hi, (e - e_hi.astype(F32)).astype(BF16)], 0)
        cd_f = jnp.exp2(lane_bcast(_split3(tot8), 0))[0:1]
        b_t = [bc[:, i * 128:(i + 1) * 128].T for i in range(SSD_GROUPS // 2)]
        h_in = stacked_states(0)
        for g in range(SSD_GROUPS):
            i, r = divmod(g, 2)
            cm = group_c(bc, g)
            scores = _dot_nt(cm, bc[:, i * 128:(i + 1) * 128].astype(BF16))
            bg_t = b_t[i][r * SSD_STATE:(r + 1) * SSD_STATE, :]
            gcols = slice(g * GROUP_COLS, (g + 1) * GROUP_COLS)
            y_off = _dot(cm, h_in[i]) * eb_f[:, gcols]
            for t in range(HEADS_PER_GROUP // 2):
                pc = slice((2 * g + t) * 128, (2 * g + t + 1) * 128)
                xp = x[:, pc]
                x2 = jnp.concatenate([jnp.where(low_half, xp, 0.0), jnp.where(low_half, 0.0, xp)], axis=0).astype(BF16)
                m, sf, sb = [], [], []
                for u in range(2):
                    kf = g * HEADS_PER_GROUP + 2 * t + u
                    kb = SSD_HEADS + kf
                    af_col = jnp.broadcast_to(acum[:, kf:kf + 1], (q, q))
                    ab_col = jnp.broadcast_to(acum[:, kb:kb + 1], (q, q))
                    sf_row, sb_row = src_t[kf:kf + 1, :], src_t[kb:kb + 1, :]
                    decay = (jnp.exp2(jnp.where(lower, af_col - sf_row, NEG_BIG))
                             + jnp.exp2(jnp.where(upper, ab_col - sb_row, NEG_BIG)))
                    m.append((scores * decay).astype(BF16))
                    wf = jnp.exp2(tot8[0:1, kf:kf + 1] - sf_row)
                    wb = jnp.exp2(tot8[0:1, kb:kb + 1] - sb_row)
                    sf.append((bg_t * wf).astype(BF16))
                    sb.append((bg_t * wb).astype(BF16))
                y_pair = _dot(jnp.concatenate(m, axis=1), x2)
                tc = slice(t * 128, (t + 1) * 128)
                yl_scr[rows, pc] = y_pair + y_off[:, tc] + dx_ref[:, pc] * xp
                h_scr[0, g, :, tc] = h_scr[0, g, :, tc] * cd_f[:, pc] + _dot(jnp.concatenate(sf, axis=1), x2)
                sb_scr[c, g, :, tc] = _dot(jnp.concatenate(sb, axis=1), x2)

    def second_sweep(c, blk):
        rows = pl.ds(pl.multiple_of(c * q, q), q)
        bc = bc_ref[0, blk, :].astype(F32)
        e = jnp.exp2(acum_scr[rows, :])
        e_hi = e.astype(BF16)
        eb_b = lane_bcast([e_hi, (e - e_hi.astype(F32)).astype(BF16)], 1)
        cd_b = jnp.exp2(lane_bcast(_split3(tot_scr[c]), 1))[0:1]
        h_in = stacked_states(1)
        for g in range(SSD_GROUPS):
            gcols = slice(g * GROUP_COLS, (g + 1) * GROUP_COLS)
            y_off = _dot(group_c(bc, g), h_in[g // 2]) * eb_b[:, gcols]
            y_ref[0, blk, gcols] = ((yl_scr[rows, gcols] + y_off) * zs_ref[0, blk, gcols].astype(F32)).astype(y_ref.dtype)
            h_scr[1, g] = h_scr[1, g] * cd_b[:, gcols] + sb_scr[c, g]

    n_steps = nc // cps
    one_step = n_steps == 1

    @pl.when(s < n_steps)
    def _():
        for ci in range(cps):
            first_sweep(s * cps + ci, slice(ci * q, (ci + 1) * q))

    @pl.when(jnp.logical_or(one_step, s >= n_steps))
    def _():
        blk_id = s * 0 if one_step else 2 * n_steps - 1 - s
        for ci in reversed(range(cps)):
            second_sweep(blk_id * cps + ci, slice(ci * q, (ci + 1) * q))

    @pl.when(jnp.logical_or(one_step, s == 2 * n_steps - 1))
    def _():
        for d in range(2):
            for g in range(SSD_GROUPS):
                ht_ref[0, d, g * HEADS_PER_GROUP:(g + 1) * HEADS_PER_GROUP] = h_scr[d, g].T.reshape(
                    HEADS_PER_GROUP, SSD_HEADDIM, SSD_STATE)


def _ssd(proj_lo, proj_hi, h0, dt_bias128, a128, d_exp):
    b, l, _ = proj_hi.shape
    q = SSD_CHUNK
    nc = l // q
    cps = min(nc, SSD_CHUNKS_PER_STEP)
    n_steps = nc // cps
    n_grid = 1 if n_steps == 1 else 2 * n_steps
    rows = cps * q
    early = lambda s: jnp.minimum(s, n_steps - 1)
    both = lambda s: jnp.where(s < n_steps, s, 2 * n_steps - 1 - s)
    late = lambda s: jnp.where(s < n_steps, n_steps - 1, 2 * n_steps - 1 - s)
    seqs = SSD_SEQS_PER_STEP if (n_steps == 1 and b % SSD_SEQS_PER_STEP == 0) else 1
    st_shape = (seqs, 2, SSD_HEADS, SSD_HEADDIM, SSD_STATE)
    st_spec = pl.BlockSpec(st_shape, lambda i, s: (i, 0, 0, 0, 0))
    has_h0 = h0 is not None
    return pl.pallas_call(
        functools.partial(_ssd_kernel, nc=nc, cps=cps, has_h0=has_h0, seqs=seqs),
        grid=(b // seqs, n_grid),
        in_specs=[pl.BlockSpec((seqs, rows, SSD_INNER), lambda i, s: (i, early(s), XS_BLK)),
                  pl.BlockSpec((seqs, rows, SSD_INNER), lambda i, s: (i, late(s), Z_BLK)),
                  pl.BlockSpec((seqs, rows, IN_TILE), lambda i, s: (i, both(s), BC_TILE)),
                  pl.BlockSpec((seqs, l, 128), lambda i, s: (i, 0, DT_BLK)),
                  pl.BlockSpec((1, 128), lambda i, s: (0, 0)),
                  pl.BlockSpec((1, 128), lambda i, s: (0, 0)),
                  pl.BlockSpec((1, SSD_INNER), lambda i, s: (0, 0))] + ([st_spec] if has_h0 else []),
        out_specs=[pl.BlockSpec((seqs, rows, SSD_INNER), lambda i, s: (i, late(s), 0)),
                   pl.BlockSpec(st_shape, lambda i, s: (i, 0, 0, 0, 0))],
        out_shape=[jax.ShapeDtypeStruct((b, l, SSD_INNER), BF16),
                   jax.ShapeDtypeStruct((b,) + st_shape[1:], F32)],
        scratch_shapes=[pltpu.VMEM((l, SSD_INNER), F32),
                        pltpu.VMEM((l, 128), F32),
                        pltpu.VMEM((nc, 128, q), F32),
                        pltpu.VMEM((nc, 8, 128), F32),
                        pltpu.VMEM((nc, SSD_GROUPS, SSD_STATE, GROUP_COLS), F32),
                        pltpu.VMEM((2, SSD_GROUPS, SSD_STATE, GROUP_COLS), F32),
                        pltpu.VMEM((2, 128, SSD_INNER), BF16)],
        compiler_params=_params("arbitrary", "arbitrary"),
        name="ssd_scan",
    )(proj_hi, proj_lo, proj_lo, proj_hi, dt_bias128, a128, d_exp, *([h0] if has_h0 else []))


def _merge_kernel(attn_ref, yz_ref, gm_ref, gs_ref, x_ref, g1_ref, ng_ref, womla_ref, wossd_ref, wout_ref, o_ref, w_scr):
    @pl.when(pl.program_id(0) == 0)
    def _():
        w_scr[0] = womla_ref[...].astype(BF16)
        w_scr[1] = wossd_ref[...].astype(BF16)
        w_scr[2] = wout_ref[...].astype(BF16)

    o_mla = _dot(attn_ref[...], w_scr[0])
    o_ssd = _dot(_rmsnorm(yz_ref[...].astype(F32), ng_ref[...]).astype(BF16), w_scr[1])
    merged = gm_ref[...].astype(F32) * o_mla + gs_ref[...].astype(F32) * o_ssd
    o_ref[...] = x_ref[...] + g1_ref[...] * _dot(merged.astype(BF16), w_scr[2])


def _merge(attn2d, yz2d, proj, x2d, mod48, mod_row, ssd_norm_g, w_o_mla, w_o_ssd, w_out):
    t = x2d.shape[0]
    tm = TOKEN_TILE
    row = lambda i: (i, 0)
    const = lambda i: (0, 0)
    wspec = pl.BlockSpec((D_MODEL, D_MODEL), const, pipeline_mode=pl.Buffered(1))
    return pl.pallas_call(
        _merge_kernel,
        grid=(t // tm,),
        in_specs=[pl.BlockSpec((tm, D_MODEL), row),
                  pl.BlockSpec((tm, D_MODEL), row),
                  pl.BlockSpec((tm, D_MODEL), lambda i: (i, GM_BLK)),
                  pl.BlockSpec((tm, D_MODEL), lambda i: (i, GS_BLK)),
                  pl.BlockSpec((tm, D_MODEL), row),
                  pl.BlockSpec((None, 1, D_MODEL), lambda i: (mod_row(i * tm // ROW_GROUP) * 6 + 2, 0, 0)),
                  pl.BlockSpec((1, D_MODEL), const),
                  wspec, wspec, wspec],
        out_specs=pl.BlockSpec((tm, D_MODEL), row),
        out_shape=jax.ShapeDtypeStruct((t, D_MODEL), F32),
        scratch_shapes=[pltpu.VMEM((3, D_MODEL, D_MODEL), BF16)],
        compiler_params=_params("arbitrary"),
        name="merge_out",
    )(attn2d, yz2d, proj, proj, x2d, mod48, ssd_norm_g, w_o_mla, w_o_ssd, w_out)


def _ffn_kernel(x_ref, sh_ref, sc_ref, g2_ref, ng_ref, wg_ref, wv_ref, cwg_ref, cwv_ref, cbg_ref, cbv_ref, wd_ref,
                fg_ref, o_ref, h_scr, wup_scr, wd_scr, u_scr, *, seq_len):
    j = pl.program_id(1)
    edges = [0] + list(range(FFN_CHUNK // 2, ROW_GROUP, FFN_CHUNK)) + [ROW_GROUP]
    n_chunks = len(edges) - 1
    rows = lambda c: slice(edges[c], edges[c + 1])

    wd_scr[...] = wd_ref[...].astype(BF16)

    def first_up(h_rows):
        n_k = 4
        acc = None
        for k in range(n_k):
            ks = slice(k * (D_MODEL // n_k), (k + 1) * (D_MODEL // n_k))
            wup_scr[ks, 0:FFN_TILE] = wg_ref[ks, :].astype(BF16)
            wup_scr[ks, FFN_TILE:2 * FFN_TILE] = wv_ref[ks, :].astype(BF16)
            part = _dot(h_scr[h_rows, ks], wup_scr[ks, :])
            acc = part if acc is None else acc + part
        return acc

    def gated(c):
        ug = _dwconv3_rows(u_scr, rows(c), slice(0, FFN_TILE), cwg_ref, cbg_ref, seq_len)
        uv = _dwconv3_rows(u_scr, rows(c), slice(FFN_TILE, 2 * FFN_TILE), cwv_ref, cbv_ref, seq_len)
        return (_silu(ug) * uv).astype(BF16)

    def pipeline(first, last):
        act = {}
        lag = FFN_DOWN_LAG
        for c in range(n_chunks + lag):
            if c < n_chunks:
                if first:
                    h_scr[rows(c), :] = _norm_mod(x_ref[rows(c), :], ng_ref, sc_ref, sh_ref)
                up = first_up(rows(c)) if c == 0 else _dot(h_scr[rows(c), :], wup_scr[...])
                u_scr[_stage_rows(rows(c)), :] = up
            if c >= lag:
                r = rows(c - lag)
                acc = _dot(act.pop(c - lag), wd_scr[...])
                if not first:
                    acc = o_ref[r, :] + acc
                if last:
                    acc = _rmsnorm(x_ref[r, :] + g2_ref[...] * acc, fg_ref[...])
                o_ref[r, :] = acc
            if 1 <= c <= n_chunks:
                act[c - 1] = gated(c - 1)

    last_j = pl.num_programs(1) - 1

    @pl.when(j == 0)
    def _():
        _zero_conv_pads(u_scr)
        pipeline(True, False)

    @pl.when((j > 0) & (j < last_j))
    def _():
        pipeline(False, False)

    @pl.when(j == last_j)
    def _():
        pipeline(False, True)


def _ffn(x2d, mod48, mod_row, norm_g, w_up, conv_w, conv_b, w_down, final_g, seq_len):
    t = x2d.shape[0]
    nj = D_FF // FFN_TILE
    gate = lambda r, j: (0, j)
    val = lambda r, j: (0, nj + j)
    const = lambda r, j: (0, 0)
    mod = lambda k: pl.BlockSpec((None, 1, D_MODEL), lambda r, j: (mod_row(r) * 6 + k, 0, 0))
    return pl.pallas_call(
        functools.partial(_ffn_kernel, seq_len=seq_len),
        grid=(t // ROW_GROUP, nj),
        in_specs=[pl.BlockSpec((ROW_GROUP, D_MODEL), lambda r, j: (r, 0)),
                  mod(3), mod(4), mod(5),
                  pl.BlockSpec((1, D_MODEL), const),
                  pl.BlockSpec((D_MODEL, FFN_TILE), gate),
                  pl.BlockSpec((D_MODEL, FFN_TILE), val),
                  pl.BlockSpec((3, FFN_TILE), gate),
                  pl.BlockSpec((3, FFN_TILE), val),
                  pl.BlockSpec((1, FFN_TILE), gate),
                  pl.BlockSpec((1, FFN_TILE), val),
                  pl.BlockSpec((FFN_TILE, D_MODEL), lambda r, j: (j, 0)),
                  pl.BlockSpec((1, D_MODEL), const)],
        out_specs=pl.BlockSpec((ROW_GROUP, D_MODEL), lambda r, j: (r, 0)),
        out_shape=jax.ShapeDtypeStruct((t, D_MODEL), F32),
        scratch_shapes=[pltpu.VMEM((ROW_GROUP, D_MODEL), BF16),
                        pltpu.VMEM((D_MODEL, 2 * FFN_TILE), BF16),
                        pltpu.VMEM((FFN_TILE, D_MODEL), BF16),
                        pltpu.VMEM((ROW_GROUP + 2 * CONV_PAD, 2 * FFN_TILE), F32)],
        compiler_params=_params("arbitrary", "arbitrary"),
        name="conv_ffn",
    )(x2d, mod48, mod48, mod48, norm_g, w_up, w_up, conv_w, conv_w, conv_b, conv_b, w_down, final_g)


def _rope_tables(seq_len):
    t = np.arange(seq_len)
    row = (t // GRID_W).astype(np.float32)
    col = (t % GRID_W).astype(np.float32)
    n = QK_ROPE // 4
    inv = (np.float32(ROPE_BASE) ** (-np.arange(n, dtype=np.float32) / np.float32(n))).astype(np.float32)
    ar, ac = row[:, None] * inv, col[:, None] * inv
    cos64 = np.concatenate([np.cos(ar), np.cos(ar), np.cos(ac), np.cos(ac)], axis=1)
    sin64 = np.concatenate([-np.sin(ar), np.sin(ar), -np.sin(ac), np.sin(ac)], axis=1)
    zeros = np.zeros_like(cos64)
    return (jnp.asarray(np.concatenate([cos64, zeros], axis=1), F32),
            jnp.asarray(np.concatenate([sin64, zeros], axis=1), F32))


def _swap_rope_halves(w):
    lead = w.shape[:-1]
    return w.reshape(lead + (2, 2, QK_ROPE // 4))[..., ::-1, :].reshape(lead + (QK_ROPE,))


def _trunk_pass(x, mod48, mod_row, wts, ctx, latent):
    b, l, _ = x.shape
    x2d = x.reshape(b * l, D_MODEL)
    proj_lo, proj_hi = _in_proj(x2d, mod48, mod_row, wts["norm_attn_g"], wts["w_in_r"], wts["ssd_conv_w"],
                                wts["ssd_conv_b"], l)
    rope = _rope_tables(l) if latent else None
    w_uq_r = wts["w_uq_lat"] if latent else wts["w_uq_ctx"]
    shape3 = lambda a, n: a.reshape(b, n, a.shape[-1])
    h0 = None
    mla_ctx = None
    if ctx is not None:
        cache_ckv, cache_krope, h0 = ctx
        mla_ctx = (cache_ckv, cache_krope)
    emit_cache = ctx is None
    mla_out = _mla(shape3(proj_hi, l), wts["q_norm_g"], wts["kv_norm_g"], w_uq_r, wts["w_uk_t"], wts["w_uv"],
                   rope, mla_ctx, emit_cache)
    attn, ckv_n, kr3 = mla_out if emit_cache else (mla_out[0], None, None)
    yz, h_t = _ssd(shape3(proj_lo, l), shape3(proj_hi, l), h0, wts["dt_bias128"], wts["a128"], wts["d_exp"])
    x1 = _merge(attn.reshape(b * l, -1), yz.reshape(b * l, -1), proj_lo, x2d, mod48, mod_row, wts["ssd_norm_g"],
                wts["w_o_mla"], wts["w_o_ssd"], wts["w_out"])
    y = _ffn(x1, mod48, mod_row, wts["norm_ffn_g"], wts["w_up"], wts["ffn_conv_w"], wts["ffn_conv_b"], wts["w_down"],
             wts["final_norm_g"], l)
    return y.reshape(b, l, D_MODEL), ckv_n, kr3, h_t


def kernel(x_prompt, x_sample, c, cache_ckv, cache_krope, state_ssd, c_ctx, w_ada, b_ada, norm_attn_g, w_in, q_norm_g,
           kv_norm_g, w_uq, w_ukv, w_o_mla, ssd_conv_w, ssd_conv_b, ssd_dt_bias, ssd_A_log, ssd_D, ssd_norm_g, w_o_ssd,
           w_out, norm_ffn_g, w_up, ffn_conv_w, ffn_conv_b, w_down, final_norm_g):
    depth = w_in.shape[0]
    assert depth == 1, "single trunk layer"
    dec_b = x_sample.shape[0]
    assert x_sample.shape[1] == ROW_GROUP and ROW_GROUP % x_prompt.shape[1] == 0
    lyr = 0

    cvec = jnp.zeros((8, D_MODEL), F32).at[0].set(c_ctx).at[1:1 + dec_b].set(c)
    mod48 = _ada(cvec, w_ada[lyr], b_ada[lyr]).reshape(8 * 6, 1, D_MODEL)

    w_in_r = _regroup_w_in(w_in[lyr].T)
    wq = w_uq[lyr].reshape(Q_LORA, N_HEADS, QK_NOPE + QK_ROPE)
    wq_nope = wq[:, :, :QK_NOPE].reshape(Q_LORA, -1)
    wq_rope = wq[:, :, QK_NOPE:]
    pad_rope = lambda w: jnp.pad(w, ((0, 0), (0, 0), (0, 128 - QK_ROPE))).reshape(Q_LORA, -1)
    w_uq_ctx = jnp.concatenate([wq_nope, pad_rope(wq_rope)], axis=1).astype(BF16)
    w_uq_lat = jnp.concatenate([wq_nope, pad_rope(wq_rope), pad_rope(_swap_rope_halves(wq_rope))], axis=1).astype(BF16)
    wkv = w_ukv[lyr].reshape(KV_LORA, N_HEADS, QK_NOPE + V_HEAD)
    w_uk_t = wkv[:, :, :QK_NOPE].reshape(KV_LORA, -1).T.astype(BF16)
    w_uv = wkv[:, :, QK_NOPE:].reshape(KV_LORA, -1).astype(BF16)
    pad128 = lambda a: jnp.pad(a.reshape(1, -1), ((0, 0), (0, 128 - a.size)))
    wts = {
        "norm_attn_g": norm_attn_g[lyr].reshape(1, -1), "w_in_r": w_in_r,
        "ssd_conv_w": ssd_conv_w[lyr], "ssd_conv_b": ssd_conv_b[lyr].reshape(1, -1),
        "q_norm_g": q_norm_g[lyr].reshape(1, -1), "kv_norm_g": kv_norm_g[lyr].reshape(1, -1),
        "w_uq_ctx": w_uq_ctx, "w_uq_lat": w_uq_lat, "w_uk_t": w_uk_t, "w_uv": w_uv,
        "dt_bias128": pad128(ssd_dt_bias[lyr]), "a128": pad128(-jnp.exp(ssd_A_log[lyr])),
        "d_exp": jnp.repeat(ssd_D[lyr], SSD_HEADDIM).reshape(1, -1),
        "ssd_norm_g": ssd_norm_g[lyr].reshape(1, -1),
        "w_o_mla": w_o_mla[lyr], "w_o_ssd": w_o_ssd[lyr], "w_out": w_out[lyr],
        "norm_ffn_g": norm_ffn_g[lyr].reshape(1, -1), "w_up": w_up[lyr],
        "ffn_conv_w": ffn_conv_w[lyr], "ffn_conv_b": ffn_conv_b[lyr].reshape(1, -1),
        "w_down": w_down[lyr], "final_norm_g": final_norm_g.reshape(1, -1),
    }

    y_p, ckv_p, kr_p, st_p = _trunk_pass(x_prompt, mod48, lambda r: 0, wts, None, False)
    ctx = (cache_ckv[:, lyr], cache_krope[:, lyr], state_ssd[:, lyr])
    y_s, _, _, _ = _trunk_pass(x_sample, mod48, lambda r: 1 + r, wts, ctx, True)
    return y_p, y_s, ckv_p[:, None], kr_p[:, None], st_p[:, None]
```

```python
import functools
import math

import jax
import jax.numpy as jnp
import numpy as np
from jax import lax
from jax.experimental import pallas as pl
from jax.experimental.pallas import tpu as pltpu

F32 = jnp.float32
BF16 = jnp.bfloat16

D_MODEL = 1024
GRID_W = 64
N_HEADS = 8
QK_NOPE = 128
QK_ROPE = 64
V_HEAD = 128
Q_LORA = 256
KV_LORA = 256
ROPE_BASE = 10000.0
SSD_HEADS = 16
SSD_HEADDIM = 64
SSD_INNER = SSD_HEADS * SSD_HEADDIM
SSD_GROUPS = 4
SSD_STATE = 64
SSD_CHUNK = 128
D_FF = 2816
EPS = 1e-6

ROW_GROUP = 2048
IN_TILE = 512
SSD_CHUNKS_PER_STEP = 8
SSD_SEQS_PER_STEP = 2
MLA_SEQS_PER_STEP = 2
FFN_TILE = 256
TOKEN_TILE = 512
ATTN_Q_TILE = 512
IN_CHUNK = 256
FFN_CHUNK = 512
FFN_DOWN_LAG = 2
CONV_PAD = 8
VMEM_LIMIT = 56 * 1024 * 1024
MLA_VMEM_LIMIT = 62 * 1024 * 1024
NEG_BIG = -1e30
LOG2_E = 1.4426950408889634


def _sigmoid(x):
    return 1.0 / (1.0 + jnp.exp(-x))


def _silu(x):
    return x * _sigmoid(x)


def _softplus(x):
    e = jnp.exp(-jnp.abs(x))
    u = 1.0 + e
    log1p_e = jnp.where(u == 1.0, e, e * jnp.log(u) / jnp.where(u == 1.0, 1.0, u - 1.0))
    return jnp.maximum(x, 0.0) + log1p_e


def _rmsnorm(x, g):
    return x * lax.rsqrt(jnp.mean(x * x, axis=-1, keepdims=True) + EPS) * g


def _dot(a, b):
    return jnp.dot(a, b, preferred_element_type=F32)


def _dot_nt(a, b):
    return lax.dot_general(a, b, (((1,), (1,)), ((), ())), preferred_element_type=F32)


def _params(*sem, vmem_limit=VMEM_LIMIT):
    return pltpu.CompilerParams(dimension_semantics=sem, vmem_limit_bytes=vmem_limit)


def _norm_mod(x, g_ref, sc_ref, sh_ref):
    return (_rmsnorm(x, g_ref[...]) * (1.0 + sc_ref[...]) + sh_ref[...]).astype(BF16)


def _zero_conv_pads(u_scr):
    zeros = jnp.zeros((CONV_PAD, u_scr.shape[1]), F32)
    u_scr[0:CONV_PAD, :] = zeros
    u_scr[CONV_PAD + ROW_GROUP:2 * CONV_PAD + ROW_GROUP, :] = zeros


def _stage_rows(c, chunk):
    return slice(CONV_PAD + c * chunk, CONV_PAD + (c + 1) * chunk)


def _dwconv3_rows(u_scr, c, chunk, cols, w_ref, b_ref, seq_len):
    r0 = c * chunk
    base = CONV_PAD + r0
    width = cols.stop - cols.start
    prev = u_scr[base - 1:base - 1 + chunk, cols]
    cur = u_scr[base:base + chunk, cols]
    nxt = u_scr[base + 1:base + 1 + chunk, cols]
    pos = (lax.broadcasted_iota(jnp.int32, (chunk, width), 0) + r0) & (seq_len - 1)
    traced = not isinstance(seq_len, int)
    if traced or r0 % seq_len == 0 or chunk > seq_len:
        prev = jnp.where(pos == 0, 0.0, prev)
    if traced or (r0 + chunk) % seq_len == 0 or chunk > seq_len:
        nxt = jnp.where(pos == seq_len - 1, 0.0, nxt)
    return prev * w_ref[0:1, :] + cur * w_ref[1:2, :] + nxt * w_ref[2:3, :] + b_ref[...]


def _ada_kernel(c_ref, w_ref, b_ref, o_ref):
    a = _silu(c_ref[...]).astype(BF16)
    o_ref[...] = _dot(a, w_ref[...].astype(BF16)) + b_ref[...]


def _ada(cvec, w_ada, b_ada):
    tn = 1536
    return pl.pallas_call(
        _ada_kernel,
        grid=(6 * D_MODEL // tn,),
        in_specs=[pl.BlockSpec((8, D_MODEL), lambda j: (0, 0)),
                  pl.BlockSpec((D_MODEL, tn), lambda j: (0, j)),
                  pl.BlockSpec((1, tn), lambda j: (0, j))],
        out_specs=pl.BlockSpec((8, tn), lambda j: (0, j)),
        out_shape=jax.ShapeDtypeStruct((8, 6 * D_MODEL), F32),
        compiler_params=_params("arbitrary"),
        name="ada_mod",
    )(cvec, w_ada, b_ada.reshape(1, -1))


IN_SPLITS = (Q_LORA, KV_LORA, QK_ROPE, SSD_INNER, SSD_INNER, SSD_GROUPS * SSD_STATE, SSD_GROUPS * SSD_STATE,
             2 * SSD_HEADS, D_MODEL, D_MODEL)
IN_OFFSETS = tuple(int(v) for v in np.cumsum((0,) + IN_SPLITS))


def _regroup_kernel(w_ref, o_ref):
    dst = 0

    def put_block(block):
        nonlocal dst
        o_ref[:, dst:dst + 128] = block.T.astype(BF16)
        dst += 128

    def piece(i):
        for off in range(0, IN_SPLITS[i], 128):
            put_block(w_ref[IN_OFFSETS[i] + off:IN_OFFSETS[i] + off + 128, :])

    def padded(parts):
        n = sum(p.shape[0] for p in parts)
        put_block(jnp.concatenate(parts + [jnp.zeros((128 - n, w_ref.shape[1]), F32)], axis=0))

    for i in (3, 8, 9, 5, 6, 4, 0, 1):
        piece(i)
    kr0, n = IN_OFFSETS[2], QK_ROPE // 4
    padded([w_ref[kr0:kr0 + QK_ROPE, :]])
    padded([w_ref[kr0 + blk * n:kr0 + (blk + 1) * n, :] for blk in (1, 0, 3, 2)])
    padded([w_ref[IN_OFFSETS[7]:IN_OFFSETS[7] + IN_SPLITS[7], :]])
    o_ref[:, dst:] = jnp.zeros((o_ref.shape[0], o_ref.shape[1] - dst), BF16)


def _regroup_w_in(w_in_t):
    cols = 256
    n_out = (N_LOW_TILES + N_F32_TILES) * IN_TILE
    return pl.pallas_call(
        _regroup_kernel,
        grid=(D_MODEL // cols,),
        in_specs=[pl.BlockSpec((w_in_t.shape[0], cols), lambda i: (0, i))],
        out_specs=pl.BlockSpec((cols, n_out), lambda i: (i, 0)),
        out_shape=jax.ShapeDtypeStruct((D_MODEL, n_out), BF16),
        compiler_params=_params("arbitrary"),
        name="w_in_regroup",
    )(w_in_t)


N_LOW_TILES, N_F32_TILES = 7, 4
Z_BLK, GM_BLK, GS_BLK = 0, 1, 2
BC_TILE = 6
XS_BLK = 0
MLA_TILE, MISC_TILE = 2, 3
DT_BLK = (MISC_TILE * IN_TILE + 256) // 128
ATTN_HEAD_COLS = 256


IN_OUT_SLOTS = 2


def _in_kernel(xp_hbm, xs_hbm, sh_ref, sc_ref, g_ref, w_ref, cw_ref, cb_ref, lo_p, hi_p, lo_s, hi_s,
               x_buf, h_scr, u_scr, lo_stage, hi_stage, x_sems, o_sems, *, n_prompt_groups, n_groups, prompt_len,
               sample_len):
    r = pl.program_id(0)
    is_prompt = r < n_prompt_groups
    seq_len = jnp.where(is_prompt, prompt_len, sample_len)
    chunk = IN_CHUNK
    n_chunks = ROW_GROUP // chunk
    rows = lambda c: slice(c * chunk, (c + 1) * chunk)

    def x_copy(group, act):
        slot = group % 2
        for cond, hbm, base in ((group < n_prompt_groups, xp_hbm, group),
                                (group >= n_prompt_groups, xs_hbm, group - n_prompt_groups)):
            @pl.when(cond)
            def _():
                row0 = pl.multiple_of(base * ROW_GROUP, ROW_GROUP)
                act(pltpu.make_async_copy(hbm.at[pl.ds(row0, ROW_GROUP), :], x_buf.at[slot], x_sems.at[slot]))

    @pl.when(r == 0)
    def _():
        x_copy(r, lambda cp: cp.start())

    @pl.when(r + 1 < n_groups)
    def _():
        x_copy(r + 1, lambda cp: cp.start())

    x_copy(r, lambda cp: cp.wait())
    x_ref = x_buf.at[r % 2]

    stages = (lo_stage, hi_stage)
    outs = ((lo_p, lo_s), (hi_p, hi_s))
    in_flight = [[None] * IN_OUT_SLOTS for _ in stages]
    n_tiles_out = [0 for _ in stages]

    def out_copy(which, slot, col_tile, act):
        for cond, hbm, base in ((is_prompt, outs[which][0], r), (~is_prompt, outs[which][1], r - n_prompt_groups)):
            @pl.when(cond)
            def _():
                row0 = pl.multiple_of(base * ROW_GROUP, ROW_GROUP)
                dst = hbm.at[pl.ds(row0, ROW_GROUP), pl.ds(col_tile * IN_TILE, IN_TILE)]
                act(pltpu.make_async_copy(stages[which].at[slot], dst, o_sems.at[which, slot]))

    def put(val, which, col_tile, c):
        slot = n_tiles_out[which] % IN_OUT_SLOTS
        if c == 0 and in_flight[which][slot] is not None:
            out_copy(which, slot, in_flight[which][slot], lambda cp: cp.wait())
        stages[which][slot, rows(c), :] = val.astype(stages[which].dtype)
        if c == n_chunks - 1:
            out_copy(which, slot, col_tile, lambda cp: cp.start())
            in_flight[which][slot] = col_tile
            n_tiles_out[which] += 1

    def pointwise(fn, tile, which, col_tile, stage_h=False):
        w = w_ref[:, tile * IN_TILE:(tile + 1) * IN_TILE]
        for c in range(n_chunks):
            if stage_h:
                h_scr[rows(c), :] = _norm_mod(x_ref[rows(c), :], g_ref, sc_ref, sh_ref)
            put(fn(_dot(h_scr[rows(c), :], w)), which, col_tile, c)

    def conv(tile, conv_tile, which, col_tile):
        w = w_ref[:, tile * IN_TILE:(tile + 1) * IN_TILE]
        cols = pl.ds(conv_tile * IN_TILE, IN_TILE)
        cw, cb = cw_ref.at[:, cols], cb_ref.at[:, cols]

        def conv_out(c):
            put(_silu(_dwconv3_rows(u_scr, c, chunk, slice(0, IN_TILE), cw, cb, seq_len)), which, col_tile, c)
        for c in range(n_chunks):
            u_scr[_stage_rows(c, chunk), :] = _dot(h_scr[rows(c), :], w)
            if c >= 1:
                conv_out(c - 1)
        conv_out(n_chunks - 1)

    _zero_conv_pads(u_scr)
    pointwise(_silu, 0, 0, 0, stage_h=True)
    pointwise(_silu, 1, 0, 1)
    for tile in range(2, 6):
        pointwise(_sigmoid, tile, 0, tile)
    conv(BC_TILE, 2, 0, BC_TILE)
    conv(N_LOW_TILES, 0, 1, 0)
    conv(N_LOW_TILES + 1, 1, 1, 1)
    for tile in range(N_LOW_TILES + MLA_TILE, N_LOW_TILES + N_F32_TILES):
        pointwise(lambda u: u, tile, 1, tile - N_LOW_TILES)
    for which in range(len(stages)):
        for slot in range(IN_OUT_SLOTS):
            if in_flight[which][slot] is not None:
                out_copy(which, slot, in_flight[which][slot], lambda cp: cp.wait())


def _in_proj(xp2d, xs2d, mod48, norm_g, w_in_r, conv_w, conv_b, prompt_len, sample_len):
    npg, nsg = xp2d.shape[0] // ROW_GROUP, xs2d.shape[0] // ROW_GROUP
    mod_row = lambda r: jnp.where(r < npg, 0, 1 + r - npg)
    once = dict(pipeline_mode=pl.Buffered(1))
    hbm = pl.BlockSpec(memory_space=pl.ANY)
    out_shape = []
    for t in (xp2d.shape[0], xs2d.shape[0]):
        out_shape += [jax.ShapeDtypeStruct((t, N_LOW_TILES * IN_TILE), BF16),
                      jax.ShapeDtypeStruct((t, N_F32_TILES * IN_TILE), F32)]
    return pl.pallas_call(
        functools.partial(_in_kernel, n_prompt_groups=npg, n_groups=npg + nsg, prompt_len=prompt_len,
                          sample_len=sample_len),
        grid=(npg + nsg,),
        in_specs=[hbm, hbm,
                  pl.BlockSpec((None, 1, D_MODEL), lambda r: (mod_row(r) * 6 + 0, 0, 0)),
                  pl.BlockSpec((None, 1, D_MODEL), lambda r: (mod_row(r) * 6 + 1, 0, 0)),
                  pl.BlockSpec((1, D_MODEL), lambda r: (0, 0)),
                  pl.BlockSpec(w_in_r.shape, lambda r: (0, 0), **once),
                  pl.BlockSpec(conv_w.shape, lambda r: (0, 0)),
                  pl.BlockSpec(conv_b.shape, lambda r: (0, 0))],
        out_specs=[hbm] * 4,
        out_shape=out_shape,
        scratch_shapes=[pltpu.VMEM((2, ROW_GROUP, D_MODEL), F32),
                        pltpu.VMEM((ROW_GROUP, D_MODEL), BF16),
                        pltpu.VMEM((ROW_GROUP + 2 * CONV_PAD, IN_TILE), F32),
                        pltpu.VMEM((IN_OUT_SLOTS, ROW_GROUP, IN_TILE), BF16),
                        pltpu.VMEM((IN_OUT_SLOTS, ROW_GROUP, IN_TILE), F32),
                        pltpu.SemaphoreType.DMA((2,)),
                        pltpu.SemaphoreType.DMA((2, IN_OUT_SLOTS))],
        compiler_params=_params("arbitrary"),
        name="in_proj",
    )(xp2d, xs2d, mod48, mod48, norm_g, w_in_r, conv_w, conv_b)


def _mla_kernel(*refs, latent, has_ctx, emit_cache, seq_len, tq, seqs):
    n_in = 7 + (2 if latent else 0) + (2 if has_ctx else 0)
    n_out = 3 if emit_cache else 1
    per_seq = {0, 1} | ({n_in - 2, n_in - 1} if has_ctx else set())
    for bi in range(seqs):
        view = lambda r: r.at[pl.ds(bi, 1)]
        ins = [view(r) if i in per_seq else r for i, r in enumerate(refs[:n_in])]
        outs = [view(r) for r in refs[n_in:n_in + n_out]]
        _mla_sequence(*ins, *outs, *refs[n_in + n_out:], latent=latent, has_ctx=has_ctx, emit_cache=emit_cache,
                      seq_len=seq_len, tq=tq)


def _mla_sequence(*refs, latent, has_ctx, emit_cache, seq_len, tq):
    refs = list(refs)
    pm_ref, px_ref, qg_ref, kvg_ref, wuq_ref, wkt_ref, wv_ref = refs[:7]
    del refs[:7]
    if latent:
        cos_ref, sin_ref = refs[:2]
        del refs[:2]
    if has_ctx:
        cckv_ref, ckr_ref = refs[:2]
        del refs[:2]
    o_ref = refs.pop(0)
    if emit_cache:
        ckv_ref, kr_ref = refs[:2]
        del refs[:2]
    k_scr, v_scr = refs[:2]
    del refs[:2]
    if has_ctx:
        kc_scr, vc_scr = refs
    t = pl.program_id(1)
    n_nope = N_HEADS * QK_NOPE

    def put_keys(k_dst, v_dst, rows, ckv_n, kr_bf):
        ckv_bf = ckv_n.astype(BF16)
        v_dst[rows, :] = _dot(ckv_bf, wv_ref[...]).astype(BF16)
        kn_t = _dot_nt(wkt_ref[...], ckv_bf).astype(BF16)
        width = kr_bf.shape[1]
        eye = jnp.where(lax.broadcasted_iota(jnp.int32, (128, width), 0)
                        == lax.broadcasted_iota(jnp.int32, (128, width), 1), 1.0, 0.0).astype(BF16)
        kr_t = _dot_nt(eye, kr_bf).astype(BF16)
        for h in range(N_HEADS):
            base = h * ATTN_HEAD_COLS
            k_dst[base:base + QK_NOPE, rows] = kn_t[h * QK_NOPE:(h + 1) * QK_NOPE, :]
            k_dst[base + QK_NOPE:base + ATTN_HEAD_COLS, rows] = kr_t

    @pl.when(t == 0)
    def _():
        step = min(seq_len, TOKEN_TILE)
        for r0 in range(0, seq_len, step):
            rows = slice(r0, r0 + step)
            ckv_n = _rmsnorm(pm_ref[0, rows, Q_LORA:], kvg_ref[...])
            kr = px_ref[0, rows, 0:128]
            if latent:
                kr = kr * cos_ref[rows, :] + px_ref[0, rows, 128:256] * sin_ref[rows, :]
            if emit_cache:
                ckv_ref[0, rows, :] = ckv_n
                kr_ref[0, rows, :] = kr[:, :QK_ROPE]
            put_keys(k_scr, v_scr, rows, ckv_n, kr.astype(BF16))
        if has_ctx:
            past = cckv_ref.shape[1]
            put_keys(kc_scr, vc_scr, slice(0, past), cckv_ref[0], ckr_ref[0].astype(BF16))

    qrows = pl.ds(pl.multiple_of(t * tq, tq), tq)
    scale = LOG2_E / math.sqrt(QK_NOPE + QK_ROPE)
    cqn = _rmsnorm(pm_ref[0, qrows, 0:Q_LORA], qg_ref[...]).astype(BF16)
    q = _dot(cqn, wuq_ref[...])
    q_rope = q[:, n_nope:2 * n_nope]
    if latent:
        q_rope = (q_rope * jnp.concatenate([cos_ref[qrows, :]] * N_HEADS, axis=1)
                  + q[:, 2 * n_nope:3 * n_nope] * jnp.concatenate([sin_ref[qrows, :]] * N_HEADS, axis=1))
    segs = ([(kc_scr, vc_scr)] if has_ctx else []) + [(k_scr, v_scr)]
    for h in range(N_HEADS):
        head = slice(h * QK_NOPE, (h + 1) * QK_NOPE)
        qk = slice(h * ATTN_HEAD_COLS, (h + 1) * ATTN_HEAD_COLS)
        qh = (jnp.concatenate([q[:, head], q_rope[:, head]], axis=1) * scale).astype(BF16)
        s = [_dot(qh, k[qk, :]) for k, _ in segs]
        m = functools.reduce(jnp.maximum, [jnp.max(si, axis=-1, keepdims=True) for si in s])
        p = [jnp.exp2(si - m) for si in s]
        l = functools.reduce(jnp.add, [jnp.sum(pi, axis=-1, keepdims=True) for pi in p])
        o = functools.reduce(jnp.add, [_dot(pi.astype(BF16), v[:, head]) for pi, (_, v) in zip(p, segs)])
        o_ref[0, :, head] = (o / l).astype(BF16)


def _mla(proj_hi, q_norm_g, kv_norm_g, w_uq_r, w_uk_t, w_uv, rope_tables, ctx, emit_cache):
    b, l, _ = proj_hi.shape
    tq = min(l, ATTN_Q_TILE)
    latent = rope_tables is not None
    has_ctx = ctx is not None
    once = dict(pipeline_mode=pl.Buffered(1))
    per_batch = once if l // tq > 1 else {}
    seqs = MLA_SEQS_PER_STEP if (l == tq and b % MLA_SEQS_PER_STEP == 0) else 1
    const2 = lambda i, t: (0, 0)
    in_specs = [pl.BlockSpec((seqs, l, IN_TILE), lambda i, t: (i, 0, MLA_TILE), **per_batch),
                pl.BlockSpec((seqs, l, 256), lambda i, t: (i, 0, MISC_TILE * IN_TILE // 256), **per_batch),
                pl.BlockSpec((1, Q_LORA), const2),
                pl.BlockSpec((1, KV_LORA), const2),
                pl.BlockSpec(w_uq_r.shape, const2, **once),
                pl.BlockSpec(w_uk_t.shape, const2, **once),
                pl.BlockSpec(w_uv.shape, const2, **once)]
    args = [proj_hi, proj_hi, q_norm_g, kv_norm_g, w_uq_r, w_uk_t, w_uv]
    scratch = [pltpu.VMEM((N_HEADS * ATTN_HEAD_COLS, l), BF16), pltpu.VMEM((l, N_HEADS * V_HEAD), BF16)]
    if latent:
        in_specs += [pl.BlockSpec((l, 128), const2, **once)] * 2
        args += list(rope_tables)
    if has_ctx:
        past = ctx[0].shape[1]
        in_specs += [pl.BlockSpec((seqs, past, KV_LORA), lambda i, t: (i, 0, 0), **per_batch),
                     pl.BlockSpec((seqs, past, QK_ROPE), lambda i, t: (i, 0, 0), **per_batch)]
        args += list(ctx)
        scratch += [pltpu.VMEM((N_HEADS * ATTN_HEAD_COLS, past), BF16), pltpu.VMEM((past, N_HEADS * V_HEAD), BF16)]
    out_specs = [pl.BlockSpec((seqs, tq, N_HEADS * V_HEAD), lambda i, t: (i, t, 0))]
    out_shape = [jax.ShapeDtypeStruct((b, l, N_HEADS * V_HEAD), BF16)]
    if emit_cache:
        out_specs += [pl.BlockSpec((seqs, l, KV_LORA), lambda i, t: (i, 0, 0)),
                      pl.BlockSpec((seqs, l, QK_ROPE), lambda i, t: (i, 0, 0))]
        out_shape += [jax.ShapeDtypeStruct((b, l, KV_LORA), F32), jax.ShapeDtypeStruct((b, l, QK_ROPE), F32)]
    return pl.pallas_call(
        functools.partial(_mla_kernel, latent=latent, has_ctx=has_ctx, emit_cache=emit_cache, seq_len=l, tq=tq,
                          seqs=seqs),
        grid=(b // seqs, l // tq),
        in_specs=in_specs,
        out_specs=out_specs,
        out_shape=out_shape,
        scratch_shapes=scratch,
        compiler_params=_params("arbitrary", "arbitrary", vmem_limit=MLA_VMEM_LIMIT),
        name="mla_attention",
    )(*args)


def _split3(x):
    hi = x.astype(BF16)
    r = x - hi.astype(F32)
    mid = r.astype(BF16)
    lo = (r - mid.astype(F32)).astype(BF16)
    return hi, mid, lo


def _exact_dot(parts, sel):
    return functools.reduce(jnp.add, [_dot(p, sel) for p in parts])


def _exact_dot_rows(sel, parts):
    return functools.reduce(jnp.add, [_dot(sel, p) for p in parts])


HEADS_PER_GROUP = SSD_HEADS // SSD_GROUPS
GROUP_COLS = HEADS_PER_GROUP * SSD_HEADDIM


def _ssd_kernel(*refs, nc, cps, has_h0, seqs):
    n_in = 8 if has_h0 else 7
    per_seq = {0, 1, 2, 3, 7} if has_h0 else {0, 1, 2, 3}
    for bi in range(seqs):
        view = lambda r: r.at[pl.ds(bi, 1)]
        ins = [view(r) if i in per_seq else r for i, r in enumerate(refs[:n_in])]
        outs = [view(r) for r in refs[n_in:n_in + 2]]
        _ssd_sequence(*ins, *outs, *refs[n_in + 2:], nc=nc, cps=cps, has_h0=has_h0)


def _ssd_sequence(*refs, nc, cps, has_h0):
    if has_h0:
        (xs_ref, zs_ref, bc_ref, dt_ref, dtb_ref, a_ref, dx_ref, h0_ref, y_ref, ht_ref,
         yl_scr, acum_scr, src_scr, tot_scr, sb_scr, h_scr, esel_scr) = refs
    else:
        (xs_ref, zs_ref, bc_ref, dt_ref, dtb_ref, a_ref, dx_ref, y_ref, ht_ref,
         yl_scr, acum_scr, src_scr, tot_scr, sb_scr, h_scr, esel_scr) = refs
    s = pl.program_id(1)
    q = SSD_CHUNK
    n_bc = SSD_GROUPS * SSD_STATE
    lane = lax.broadcasted_iota(jnp.int32, (q, 128), 1)
    low_half = lane < SSD_HEADDIM
    ii = lax.broadcasted_iota(jnp.int32, (q, q), 0)
    jj = lax.broadcasted_iota(jnp.int32, (q, q), 1)
    lower, upper = ii >= jj, ii <= jj

    def lane_bcast(parts, d):
        return _exact_dot(parts, esel_scr[d])

    def stacked_states(d):
        return [jnp.concatenate([h_scr[d, 2 * i], h_scr[d, 2 * i + 1]], axis=0).astype(BF16)
                for i in range(SSD_GROUPS // 2)]

    def group_c(bc, g):
        i, r = divmod(g, 2)
        cpair = bc[:, n_bc + i * 128:n_bc + (i + 1) * 128]
        return jnp.where(low_half if r == 0 else ~low_half, cpair, 0.0).astype(BF16)

    @pl.when(s == 0)
    def _():
        k = lax.broadcasted_iota(jnp.int32, (128, SSD_INNER), 0)
        head = lax.broadcasted_iota(jnp.int32, (128, SSD_INNER), 1) // SSD_HEADDIM
        for d in range(2):
            esel_scr[d] = jnp.where(k == d * SSD_HEADS + head, 1.0, 0.0).astype(BF16)
        if has_h0:
            for d in range(2):
                for g in range(SSD_GROUPS):
                    hpn = h0_ref[0, d, g * HEADS_PER_GROUP:(g + 1) * HEADS_PER_GROUP].reshape(GROUP_COLS, SSD_STATE)
                    h_scr[d, g] = hpn.T
        else:
            h_scr[...] = jnp.zeros(h_scr.shape, F32)
        tri_f = jnp.where(lower, 1.0, 0.0).astype(BF16)
        tri_b = jnp.where(upper, 1.0, 0.0).astype(BF16)
        fwd_col = lane < SSD_HEADS
        for c in range(nc):
            crow = slice(c * q, (c + 1) * q)
            dt = _softplus(dt_ref[0, crow, :] + dtb_ref[...])
            parts = _split3(dt * a_ref[...])
            acum = jnp.where(fwd_col, _exact_dot_rows(tri_f, parts),
                             _exact_dot_rows(tri_b, parts))
            acum = acum * LOG2_E
            acum_scr[crow, :] = acum
            tot = jnp.where(fwd_col[0:1], acum[q - 1:q, :], acum[0:1, :])
            tot_scr[c] = jnp.broadcast_to(tot, (8, 128))
            src_scr[c] = (acum - jnp.log2(dt)).T

    def first_sweep(c, blk):
        rows = pl.ds(pl.multiple_of(c * q, q), q)
        x = xs_ref[0, blk, :]
        bc = bc_ref[0, blk, :].astype(F32)
        acum = acum_scr[rows, :]
        src_t = src_scr[c]
        tot8 = tot_scr[c]
        e = jnp.exp2(acum)
        e_hi = e.astype(BF16)
        eb_f = lane_bcast([e_hi, (e - e_hi.astype(F32)).astype(BF16)], 0)
        cd_f = jnp.exp2(lane_bcast(_split3(tot8), 0))[0:1]
        b_t = [bc[:, i * 128:(i + 1) * 128].T for i in range(SSD_GROUPS // 2)]
        h_in = stacked_states(0)
        for g in range(SSD_GROUPS):
            i, r = divmod(g, 2)
            cm = group_c(bc, g)
            scores = _dot_nt(cm, bc[:, i * 128:(i + 1) * 128].astype(BF16))
            bg_t = b_t[i][r * SSD_STATE:(r + 1) * SSD_STATE, :]
            gcols = slice(g * GROUP_COLS, (g + 1) * GROUP_COLS)
            y_off = _dot(cm, h_in[i]) * eb_f[:, gcols]
            for t in range(HEADS_PER_GROUP // 2):
                pc = slice((2 * g + t) * 128, (2 * g + t + 1) * 128)
                xp = x[:, pc]
                x2 = jnp.concatenate([jnp.where(low_half, xp, 0.0), jnp.where(low_half, 0.0, xp)], axis=0).astype(BF16)
                m, sf, sb = [], [], []
                for u in range(2):
                    kf = g * HEADS_PER_GROUP + 2 * t + u
                    kb = SSD_HEADS + kf
                    af_col = jnp.broadcast_to(acum[:, kf:kf + 1], (q, q))
                    ab_col = jnp.broadcast_to(acum[:, kb:kb + 1], (q, q))
                    sf_row, sb_row = src_t[kf:kf + 1, :], src_t[kb:kb + 1, :]
                    decay = (jnp.exp2(jnp.where(lower, af_col - sf_row, NEG_BIG))
                             + jnp.exp2(jnp.where(upper, ab_col - sb_row, NEG_BIG)))
                    m.append((scores * decay).astype(BF16))
                    wf = jnp.exp2(tot8[0:1, kf:kf + 1] - sf_row)
                    wb = jnp.exp2(tot8[0:1, kb:kb + 1] - sb_row)
                    sf.append((bg_t * wf).astype(BF16))
                    sb.append((bg_t * wb).astype(BF16))
                y_pair = _dot(jnp.concatenate(m, axis=1), x2)
                tc = slice(t * 128, (t + 1) * 128)
                yl_scr[rows, pc] = y_pair + y_off[:, tc] + dx_ref[:, pc] * xp
                h_scr[0, g, :, tc] = h_scr[0, g, :, tc] * cd_f[:, pc] + _dot(jnp.concatenate(sf, axis=1), x2)
                sb_scr[c, g, :, tc] = _dot(jnp.concatenate(sb, axis=1), x2)

    def second_sweep(c, blk):
        rows = pl.ds(pl.multiple_of(c * q, q), q)
        bc = bc_ref[0, blk, :].astype(F32)
        e = jnp.exp2(acum_scr[rows, :])
        e_hi = e.astype(BF16)
        eb_b = lane_bcast([e_hi, (e - e_hi.astype(F32)).astype(BF16)], 1)
        cd_b = jnp.exp2(lane_bcast(_split3(tot_scr[c]), 1))[0:1]
        h_in = stacked_states(1)
        for g in range(SSD_GROUPS):
            gcols = slice(g * GROUP_COLS, (g + 1) * GROUP_COLS)
            y_off = _dot(group_c(bc, g), h_in[g // 2]) * eb_b[:, gcols]
            y_ref[0, blk, gcols] = ((yl_scr[rows, gcols] + y_off) * zs_ref[0, blk, gcols].astype(F32)).astype(y_ref.dtype)
            h_scr[1, g] = h_scr[1, g] * cd_b[:, gcols] + sb_scr[c, g]

    n_steps = nc // cps
    one_step = n_steps == 1

    @pl.when(s < n_steps)
    def _():
        for ci in range(cps):
            first_sweep(s * cps + ci, slice(ci * q, (ci + 1) * q))

    @pl.when(jnp.logical_or(one_step, s >= n_steps))
    def _():
        blk_id = s * 0 if one_step else 2 * n_steps - 1 - s
        for ci in reversed(range(cps)):
            second_sweep(blk_id * cps + ci, slice(ci * q, (ci + 1) * q))

    @pl.when(jnp.logical_or(one_step, s == 2 * n_steps - 1))
    def _():
        for d in range(2):
            for g in range(SSD_GROUPS):
                ht_ref[0, d, g * HEADS_PER_GROUP:(g + 1) * HEADS_PER_GROUP] = h_scr[d, g].T.reshape(
                    HEADS_PER_GROUP, SSD_HEADDIM, SSD_STATE)


def _ssd(proj_lo, proj_hi, h0, dt_bias128, a128, d_exp):
    b, l, _ = proj_hi.shape
    q = SSD_CHUNK
    nc = l // q
    cps = min(nc, SSD_CHUNKS_PER_STEP)
    n_steps = nc // cps
    n_grid = 1 if n_steps == 1 else 2 * n_steps
    rows = cps * q
    early = lambda s: jnp.minimum(s, n_steps - 1)
    both = lambda s: jnp.where(s < n_steps, s, 2 * n_steps - 1 - s)
    late = lambda s: jnp.where(s < n_steps, n_steps - 1, 2 * n_steps - 1 - s)
    seqs = SSD_SEQS_PER_STEP if (n_steps == 1 and b % SSD_SEQS_PER_STEP == 0) else 1
    st_shape = (seqs, 2, SSD_HEADS, SSD_HEADDIM, SSD_STATE)
    st_spec = pl.BlockSpec(st_shape, lambda i, s: (i, 0, 0, 0, 0))
    has_h0 = h0 is not None
    return pl.pallas_call(
        functools.partial(_ssd_kernel, nc=nc, cps=cps, has_h0=has_h0, seqs=seqs),
        grid=(b // seqs, n_grid),
        in_specs=[pl.BlockSpec((seqs, rows, SSD_INNER), lambda i, s: (i, early(s), XS_BLK)),
                  pl.BlockSpec((seqs, rows, SSD_INNER), lambda i, s: (i, late(s), Z_BLK)),
                  pl.BlockSpec((seqs, rows, IN_TILE), lambda i, s: (i, both(s), BC_TILE)),
                  pl.BlockSpec((seqs, l, 128), lambda i, s: (i, 0, DT_BLK)),
                  pl.BlockSpec((1, 128), lambda i, s: (0, 0)),
                  pl.BlockSpec((1, 128), lambda i, s: (0, 0)),
                  pl.BlockSpec((1, SSD_INNER), lambda i, s: (0, 0))] + ([st_spec] if has_h0 else []),
        out_specs=[pl.BlockSpec((seqs, rows, SSD_INNER), lambda i, s: (i, late(s), 0)),
                   pl.BlockSpec(st_shape, lambda i, s: (i, 0, 0, 0, 0))],
        out_shape=[jax.ShapeDtypeStruct((b, l, SSD_INNER), BF16),
                   jax.ShapeDtypeStruct((b,) + st_shape[1:], F32)],
        scratch_shapes=[pltpu.VMEM((l, SSD_INNER), F32),
                        pltpu.VMEM((l, 128), F32),
                        pltpu.VMEM((nc, 128, q), F32),
                        pltpu.VMEM((nc, 8, 128), F32),
                        pltpu.VMEM((nc, SSD_GROUPS, SSD_STATE, GROUP_COLS), F32),
                        pltpu.VMEM((2, SSD_GROUPS, SSD_STATE, GROUP_COLS), F32),
                        pltpu.VMEM((2, 128, SSD_INNER), BF16)],
        compiler_params=_params("arbitrary", "arbitrary"),
        name="ssd_scan",
    )(proj_hi, proj_lo, proj_lo, proj_hi, dt_bias128, a128, d_exp, *([h0] if has_h0 else []))


def _merge_kernel(attn_ref, yz_ref, gm_ref, gs_ref, x_ref, g1_ref, ng_ref, womla_ref, wossd_ref, wout_ref, o_ref, w_scr):
    @pl.when(pl.program_id(0) == 0)
    def _():
        w_scr[0] = womla_ref[...].astype(BF16)
        w_scr[1] = wossd_ref[...].astype(BF16)
        w_scr[2] = wout_ref[...].astype(BF16)

    o_mla = _dot(attn_ref[...], w_scr[0])
    o_ssd = _dot(_rmsnorm(yz_ref[...].astype(F32), ng_ref[...]).astype(BF16), w_scr[1])
    merged = gm_ref[...].astype(F32) * o_mla + gs_ref[...].astype(F32) * o_ssd
    o_ref[...] = x_ref[...] + g1_ref[...] * _dot(merged.astype(BF16), w_scr[2])


def _merge(attn2d, yz2d, proj, x2d, mod48, mod_row, ssd_norm_g, w_o_mla, w_o_ssd, w_out):
    t = x2d.shape[0]
    tm = TOKEN_TILE
    row = lambda i: (i, 0)
    const = lambda i: (0, 0)
    wspec = pl.BlockSpec((D_MODEL, D_MODEL), const, pipeline_mode=pl.Buffered(1))
    return pl.pallas_call(
        _merge_kernel,
        grid=(t // tm,),
        in_specs=[pl.BlockSpec((tm, D_MODEL), row),
                  pl.BlockSpec((tm, D_MODEL), row),
                  pl.BlockSpec((tm, D_MODEL), lambda i: (i, GM_BLK)),
                  pl.BlockSpec((tm, D_MODEL), lambda i: (i, GS_BLK)),
                  pl.BlockSpec((tm, D_MODEL), row),
                  pl.BlockSpec((None, 1, D_MODEL), lambda i: (mod_row(i * tm // ROW_GROUP) * 6 + 2, 0, 0)),
                  pl.BlockSpec((1, D_MODEL), const),
                  wspec, wspec, wspec],
        out_specs=pl.BlockSpec((tm, D_MODEL), row),
        out_shape=jax.ShapeDtypeStruct((t, D_MODEL), F32),
        scratch_shapes=[pltpu.VMEM((3, D_MODEL, D_MODEL), BF16)],
        compiler_params=_params("arbitrary"),
        name="merge_out",
    )(attn2d, yz2d, proj, proj, x2d, mod48, ssd_norm_g, w_o_mla, w_o_ssd, w_out)


def _ffn_kernel(x_ref, sh_ref, sc_ref, g2_ref, ng_ref, wg_ref, wv_ref, cwg_ref, cwv_ref, cbg_ref, cbv_ref, wd_ref,
                fg_ref, o_ref, h_scr, wup_scr, wd_scr, u_scr, *, seq_len):
    j = pl.program_id(1)
    chunk = FFN_CHUNK
    n_chunks = ROW_GROUP // chunk
    rows = lambda c: slice(c * chunk, (c + 1) * chunk)

    wd_scr[...] = wd_ref[...].astype(BF16)

    def first_up(h_rows):
        n_k = 4
        acc = None
        for k in range(n_k):
            ks = slice(k * (D_MODEL // n_k), (k + 1) * (D_MODEL // n_k))
            wup_scr[ks, 0:FFN_TILE] = wg_ref[ks, :].astype(BF16)
            wup_scr[ks, FFN_TILE:2 * FFN_TILE] = wv_ref[ks, :].astype(BF16)
            part = _dot(h_scr[h_rows, ks], wup_scr[ks, :])
            acc = part if acc is None else acc + part
        return acc

    def gated(c):
        ug = _dwconv3_rows(u_scr, c, chunk, slice(0, FFN_TILE), cwg_ref, cbg_ref, seq_len)
        uv = _dwconv3_rows(u_scr, c, chunk, slice(FFN_TILE, 2 * FFN_TILE), cwv_ref, cbv_ref, seq_len)
        return (_silu(ug) * uv).astype(BF16)

    def pipeline(first, last):
        act = {}
        lag = FFN_DOWN_LAG
        for c in range(n_chunks + lag):
            if c < n_chunks:
                if first:
                    h_scr[rows(c), :] = _norm_mod(x_ref[rows(c), :], ng_ref, sc_ref, sh_ref)
                up = first_up(rows(c)) if c == 0 else _dot(h_scr[rows(c), :], wup_scr[...])
                u_scr[_stage_rows(c, chunk), :] = up
            if c >= lag:
                r = rows(c - lag)
                acc = _dot(act.pop(c - lag), wd_scr[...])
                if not first:
                    acc = o_ref[r, :] + acc
                if last:
                    acc = _rmsnorm(x_ref[r, :] + g2_ref[...] * acc, fg_ref[...])
                o_ref[r, :] = acc
            if 1 <= c <= n_chunks:
                act[c - 1] = gated(c - 1)

    last_j = pl.num_programs(1) - 1

    @pl.when(j == 0)
    def _():
        _zero_conv_pads(u_scr)
        pipeline(True, False)

    @pl.when((j > 0) & (j < last_j))
    def _():
        pipeline(False, False)

    @pl.when(j == last_j)
    def _():
        pipeline(False, True)


def _ffn(x2d, mod48, mod_row, norm_g, w_up, conv_w, conv_b, w_down, final_g, seq_len):
    t = x2d.shape[0]
    nj = D_FF // FFN_TILE
    gate = lambda r, j: (0, j)
    val = lambda r, j: (0, nj + j)
    const = lambda r, j: (0, 0)
    mod = lambda k: pl.BlockSpec((None, 1, D_MODEL), lambda r, j: (mod_row(r) * 6 + k, 0, 0))
    return pl.pallas_call(
        functools.partial(_ffn_kernel, seq_len=seq_len),
        grid=(t // ROW_GROUP, nj),
        in_specs=[pl.BlockSpec((ROW_GROUP, D_MODEL), lambda r, j: (r, 0)),
                  mod(3), mod(4), mod(5),
                  pl.BlockSpec((1, D_MODEL), const),
                  pl.BlockSpec((D_MODEL, FFN_TILE), gate),
                  pl.BlockSpec((D_MODEL, FFN_TILE), val),
                  pl.BlockSpec((3, FFN_TILE), gate),
                  pl.BlockSpec((3, FFN_TILE), val),
                  pl.BlockSpec((1, FFN_TILE), gate),
                  pl.BlockSpec((1, FFN_TILE), val),
                  pl.BlockSpec((FFN_TILE, D_MODEL), lambda r, j: (j, 0)),
                  pl.BlockSpec((1, D_MODEL), const)],
        out_specs=pl.BlockSpec((ROW_GROUP, D_MODEL), lambda r, j: (r, 0)),
        out_shape=jax.ShapeDtypeStruct((t, D_MODEL), F32),
        scratch_shapes=[pltpu.VMEM((ROW_GROUP, D_MODEL), BF16),
                        pltpu.VMEM((D_MODEL, 2 * FFN_TILE), BF16),
                        pltpu.VMEM((FFN_TILE, D_MODEL), BF16),
                        pltpu.VMEM((ROW_GROUP + 2 * CONV_PAD, 2 * FFN_TILE), F32)],
        compiler_params=_params("arbitrary", "arbitrary"),
        name="conv_ffn",
    )(x2d, mod48, mod48, mod48, norm_g, w_up, w_up, conv_w, conv_w, conv_b, conv_b, w_down, final_g)


def _rope_tables(seq_len):
    t = np.arange(seq_len)
    row = (t // GRID_W).astype(np.float32)
    col = (t % GRID_W).astype(np.float32)
    n = QK_ROPE // 4
    inv = (np.float32(ROPE_BASE) ** (-np.arange(n, dtype=np.float32) / np.float32(n))).astype(np.float32)
    ar, ac = row[:, None] * inv, col[:, None] * inv
    cos64 = np.concatenate([np.cos(ar), np.cos(ar), np.cos(ac), np.cos(ac)], axis=1)
    sin64 = np.concatenate([-np.sin(ar), np.sin(ar), -np.sin(ac), np.sin(ac)], axis=1)
    zeros = np.zeros_like(cos64)
    return (jnp.asarray(np.concatenate([cos64, zeros], axis=1), F32),
            jnp.asarray(np.concatenate([sin64, zeros], axis=1), F32))


def _swap_rope_halves(w):
    lead = w.shape[:-1]
    return w.reshape(lead + (2, 2, QK_ROPE // 4))[..., ::-1, :].reshape(lead + (QK_ROPE,))


def _trunk_pass(x, proj, mod48, mod_row, wts, ctx, latent):
    b, l, _ = x.shape
    x2d = x.reshape(b * l, D_MODEL)
    proj_lo, proj_hi = proj
    rope = _rope_tables(l) if latent else None
    w_uq_r = wts["w_uq_lat"] if latent else wts["w_uq_ctx"]
    shape3 = lambda a, n: a.reshape(b, n, a.shape[-1])
    h0 = None
    mla_ctx = None
    if ctx is not None:
        cache_ckv, cache_krope, h0 = ctx
        mla_ctx = (cache_ckv, cache_krope)
    emit_cache = ctx is None
    mla_out = _mla(shape3(proj_hi, l), wts["q_norm_g"], wts["kv_norm_g"], w_uq_r, wts["w_uk_t"], wts["w_uv"],
                   rope, mla_ctx, emit_cache)
    attn, ckv_n, kr3 = mla_out if emit_cache else (mla_out[0], None, None)
    yz, h_t = _ssd(shape3(proj_lo, l), shape3(proj_hi, l), h0, wts["dt_bias128"], wts["a128"], wts["d_exp"])
    x1 = _merge(attn.reshape(b * l, -1), yz.reshape(b * l, -1), proj_lo, x2d, mod48, mod_row, wts["ssd_norm_g"],
                wts["w_o_mla"], wts["w_o_ssd"], wts["w_out"])
    y = _ffn(x1, mod48, mod_row, wts["norm_ffn_g"], wts["w_up"], wts["ffn_conv_w"], wts["ffn_conv_b"], wts["w_down"],
             wts["final_norm_g"], l)
    return y.reshape(b, l, D_MODEL), ckv_n, kr3, h_t


def kernel(x_prompt, x_sample, c, cache_ckv, cache_krope, state_ssd, c_ctx, w_ada, b_ada, norm_attn_g, w_in, q_norm_g,
           kv_norm_g, w_uq, w_ukv, w_o_mla, ssd_conv_w, ssd_conv_b, ssd_dt_bias, ssd_A_log, ssd_D, ssd_norm_g, w_o_ssd,
           w_out, norm_ffn_g, w_up, ffn_conv_w, ffn_conv_b, w_down, final_norm_g):
    depth = w_in.shape[0]
    assert depth == 1, "single trunk layer"
    dec_b = x_sample.shape[0]
    assert x_sample.shape[1] == ROW_GROUP and ROW_GROUP % x_prompt.shape[1] == 0
    lyr = 0

    cvec = jnp.zeros((8, D_MODEL), F32).at[0].set(c_ctx).at[1:1 + dec_b].set(c)
    mod48 = _ada(cvec, w_ada[lyr], b_ada[lyr]).reshape(8 * 6, 1, D_MODEL)

    w_in_r = _regroup_w_in(w_in[lyr].T)
    wq = w_uq[lyr].reshape(Q_LORA, N_HEADS, QK_NOPE + QK_ROPE)
    wq_nope = wq[:, :, :QK_NOPE].reshape(Q_LORA, -1)
    wq_rope = wq[:, :, QK_NOPE:]
    pad_rope = lambda w: jnp.pad(w, ((0, 0), (0, 0), (0, 128 - QK_ROPE))).reshape(Q_LORA, -1)
    w_uq_ctx = jnp.concatenate([wq_nope, pad_rope(wq_rope)], axis=1).astype(BF16)
    w_uq_lat = jnp.concatenate([wq_nope, pad_rope(wq_rope), pad_rope(_swap_rope_halves(wq_rope))], axis=1).astype(BF16)
    wkv = w_ukv[lyr].reshape(KV_LORA, N_HEADS, QK_NOPE + V_HEAD)
    w_uk_t = wkv[:, :, :QK_NOPE].reshape(KV_LORA, -1).T.astype(BF16)
    w_uv = wkv[:, :, QK_NOPE:].reshape(KV_LORA, -1).astype(BF16)
    pad128 = lambda a: jnp.pad(a.reshape(1, -1), ((0, 0), (0, 128 - a.size)))
    wts = {
        "norm_attn_g": norm_attn_g[lyr].reshape(1, -1), "w_in_r": w_in_r,
        "ssd_conv_w": ssd_conv_w[lyr], "ssd_conv_b": ssd_conv_b[lyr].reshape(1, -1),
        "q_norm_g": q_norm_g[lyr].reshape(1, -1), "kv_norm_g": kv_norm_g[lyr].reshape(1, -1),
        "w_uq_ctx": w_uq_ctx, "w_uq_lat": w_uq_lat, "w_uk_t": w_uk_t, "w_uv": w_uv,
        "dt_bias128": pad128(ssd_dt_bias[lyr]), "a128": pad128(-jnp.exp(ssd_A_log[lyr])),
        "d_exp": jnp.repeat(ssd_D[lyr], SSD_HEADDIM).reshape(1, -1),
        "ssd_norm_g": ssd_norm_g[lyr].reshape(1, -1),
        "w_o_mla": w_o_mla[lyr], "w_o_ssd": w_o_ssd[lyr], "w_out": w_out[lyr],
        "norm_ffn_g": norm_ffn_g[lyr].reshape(1, -1), "w_up": w_up[lyr],
        "ffn_conv_w": ffn_conv_w[lyr], "ffn_conv_b": ffn_conv_b[lyr].reshape(1, -1),
        "w_down": w_down[lyr], "final_norm_g": final_norm_g.reshape(1, -1),
    }

    lo_p, hi_p, lo_s, hi_s = _in_proj(x_prompt.reshape(-1, D_MODEL), x_sample.reshape(-1, D_MODEL), mod48,
                                      wts["norm_attn_g"], w_in_r, wts["ssd_conv_w"], wts["ssd_conv_b"],
                                      x_prompt.shape[1], x_sample.shape[1])
    y_p, ckv_p, kr_p, st_p = _trunk_pass(x_prompt, (lo_p, hi_p), mod48, lambda r: 0, wts, None, False)
    ctx = (cache_ckv[:, lyr], cache_krope[:, lyr], state_ssd[:, lyr])
    y_s, _, _, _ = _trunk_pass(x_sample, (lo_s, hi_s), mod48, lambda r: 1 + r, wts, ctx, True)
    return y_p, y_s, ckv_p[:, None], kr_p[:, None], st_p[:, None]
```

```python
import functools
import math

import jax
import jax.numpy as jnp
import numpy as np
from jax import lax
from jax.experimental import pallas as pl
from jax.experimental.pallas import tpu as pltpu

F32 = jnp.float32
BF16 = jnp.bfloat16

D_MODEL = 1024
GRID_W = 64
N_HEADS = 8
QK_NOPE = 128
QK_ROPE = 64
V_HEAD = 128
Q_LORA = 256
KV_LORA = 256
ROPE_BASE = 10000.0
SSD_HEADS = 16
SSD_HEADDIM = 64
SSD_INNER = SSD_HEADS * SSD_HEADDIM
SSD_GROUPS = 4
SSD_STATE = 64
SSD_CHUNK = 128
D_FF = 2816
EPS = 1e-6

ROW_GROUP = 2048
IN_TILE = 512
SSD_CHUNKS_PER_STEP = 8
SSD_SEQS_PER_STEP = 2
MLA_SEQS_PER_STEP = 2
FFN_TILE = 256
TOKEN_TILE = 512
ATTN_Q_TILE = 512
IN_CHUNK = 256
FFN_CHUNK = 512
FFN_DOWN_LAG = 2
CONV_PAD = 8
VMEM_LIMIT = 56 * 1024 * 1024
MLA_VMEM_LIMIT = 62 * 1024 * 1024
NEG_BIG = -1e30
LOG2_E = 1.4426950408889634


def _sigmoid(x):
    return 1.0 / (1.0 + jnp.exp(-x))


def _silu(x):
    return x * _sigmoid(x)


def _softplus(x):
    e = jnp.exp(-jnp.abs(x))
    u = 1.0 + e
    log1p_e = jnp.where(u == 1.0, e, e * jnp.log(u) / jnp.where(u == 1.0, 1.0, u - 1.0))
    return jnp.maximum(x, 0.0) + log1p_e


def _rmsnorm(x, g):
    return x * lax.rsqrt(jnp.mean(x * x, axis=-1, keepdims=True) + EPS) * g


def _dot(a, b):
    return jnp.dot(a, b, preferred_element_type=F32)


def _dot_nt(a, b):
    return lax.dot_general(a, b, (((1,), (1,)), ((), ())), preferred_element_type=F32)


def _params(*sem, vmem_limit=VMEM_LIMIT):
    return pltpu.CompilerParams(dimension_semantics=sem, vmem_limit_bytes=vmem_limit)


def _norm_mod(x, g_ref, sc_ref, sh_ref):
    return (_rmsnorm(x, g_ref[...]) * (1.0 + sc_ref[...]) + sh_ref[...]).astype(BF16)


def _zero_conv_pads(u_scr):
    zeros = jnp.zeros((CONV_PAD, u_scr.shape[1]), F32)
    u_scr[0:CONV_PAD, :] = zeros
    u_scr[CONV_PAD + ROW_GROUP:2 * CONV_PAD + ROW_GROUP, :] = zeros


def _stage_rows(c, chunk):
    return slice(CONV_PAD + c * chunk, CONV_PAD + (c + 1) * chunk)


def _dwconv3_rows(u_scr, c, chunk, cols, w_ref, b_ref, seq_len):
    r0 = c * chunk
    base = CONV_PAD + r0
    width = cols.stop - cols.start
    prev = u_scr[base - 1:base - 1 + chunk, cols]
    cur = u_scr[base:base + chunk, cols]
    nxt = u_scr[base + 1:base + 1 + chunk, cols]
    pos = (lax.broadcasted_iota(jnp.int32, (chunk, width), 0) + r0) & (seq_len - 1)
    if r0 % seq_len == 0 or chunk > seq_len:
        prev = jnp.where(pos == 0, 0.0, prev)
    if (r0 + chunk) % seq_len == 0 or chunk > seq_len:
        nxt = jnp.where(pos == seq_len - 1, 0.0, nxt)
    return prev * w_ref[0:1, :] + cur * w_ref[1:2, :] + nxt * w_ref[2:3, :] + b_ref[...]


def _ada_kernel(c_ref, w_ref, b_ref, o_ref):
    a = _silu(c_ref[...]).astype(BF16)
    o_ref[...] = _dot(a, w_ref[...].astype(BF16)) + b_ref[...]


def _ada(cvec, w_ada, b_ada):
    tn = 1536
    return pl.pallas_call(
        _ada_kernel,
        grid=(6 * D_MODEL // tn,),
        in_specs=[pl.BlockSpec((8, D_MODEL), lambda j: (0, 0)),
                  pl.BlockSpec((D_MODEL, tn), lambda j: (0, j)),
                  pl.BlockSpec((1, tn), lambda j: (0, j))],
        out_specs=pl.BlockSpec((8, tn), lambda j: (0, j)),
        out_shape=jax.ShapeDtypeStruct((8, 6 * D_MODEL), F32),
        compiler_params=_params("arbitrary"),
        name="ada_mod",
    )(cvec, w_ada, b_ada.reshape(1, -1))


IN_SPLITS = (Q_LORA, KV_LORA, QK_ROPE, SSD_INNER, SSD_INNER, SSD_GROUPS * SSD_STATE, SSD_GROUPS * SSD_STATE,
             2 * SSD_HEADS, D_MODEL, D_MODEL)
IN_OFFSETS = tuple(int(v) for v in np.cumsum((0,) + IN_SPLITS))


def _regroup_kernel(w_ref, o_ref):
    dst = 0

    def put_block(block):
        nonlocal dst
        o_ref[:, dst:dst + 128] = block.T.astype(BF16)
        dst += 128

    def piece(i):
        for off in range(0, IN_SPLITS[i], 128):
            put_block(w_ref[IN_OFFSETS[i] + off:IN_OFFSETS[i] + off + 128, :])

    def padded(parts):
        n = sum(p.shape[0] for p in parts)
        put_block(jnp.concatenate(parts + [jnp.zeros((128 - n, w_ref.shape[1]), F32)], axis=0))

    for i in (3, 8, 9, 5, 6, 4, 0, 1):
        piece(i)
    kr0, n = IN_OFFSETS[2], QK_ROPE // 4
    padded([w_ref[kr0:kr0 + QK_ROPE, :]])
    padded([w_ref[kr0 + blk * n:kr0 + (blk + 1) * n, :] for blk in (1, 0, 3, 2)])
    padded([w_ref[IN_OFFSETS[7]:IN_OFFSETS[7] + IN_SPLITS[7], :]])
    o_ref[:, dst:] = jnp.zeros((o_ref.shape[0], o_ref.shape[1] - dst), BF16)


def _regroup_w_in(w_in_t):
    cols = 256
    n_out = (N_LOW_TILES + N_F32_TILES) * IN_TILE
    return pl.pallas_call(
        _regroup_kernel,
        grid=(D_MODEL // cols,),
        in_specs=[pl.BlockSpec((w_in_t.shape[0], cols), lambda i: (0, i))],
        out_specs=pl.BlockSpec((cols, n_out), lambda i: (i, 0)),
        out_shape=jax.ShapeDtypeStruct((D_MODEL, n_out), BF16),
        compiler_params=_params("arbitrary"),
        name="w_in_regroup",
    )(w_in_t)


N_LOW_TILES, N_F32_TILES = 7, 4
Z_BLK, GM_BLK, GS_BLK = 0, 1, 2
BC_TILE = 6
XS_BLK = 0
MLA_TILE, MISC_TILE = 2, 3
DT_BLK = (MISC_TILE * IN_TILE + 256) // 128
ATTN_HEAD_COLS = 256


IN_OUT_SLOTS = 2


def _in_kernel_resident(x_ref, sh_ref, sc_ref, g_ref, w_ref, cw_ref, cb_ref, lo_hbm, hi_hbm, h_scr, u_scr, lo_stage,
                        hi_stage, sems, *, seq_len):
    r = pl.program_id(0)
    chunk = IN_CHUNK
    n_chunks = ROW_GROUP // chunk
    rows = lambda c: slice(c * chunk, (c + 1) * chunk)
    outputs = ((lo_hbm, lo_stage), (hi_hbm, hi_stage))
    in_flight = [[None] * IN_OUT_SLOTS for _ in outputs]
    n_tiles_out = [0 for _ in outputs]

    def put(val, which, col_tile, c):
        hbm, stage = outputs[which]
        slot = n_tiles_out[which] % IN_OUT_SLOTS
        if c == 0 and in_flight[which][slot] is not None:
            in_flight[which][slot].wait()
        stage[slot, rows(c), :] = val.astype(stage.dtype)
        if c == n_chunks - 1:
            row0 = pl.multiple_of(r * ROW_GROUP, ROW_GROUP)
            dst = hbm.at[pl.ds(row0, ROW_GROUP), pl.ds(col_tile * IN_TILE, IN_TILE)]
            copy = pltpu.make_async_copy(stage.at[slot], dst, sems.at[which, slot])
            copy.start()
            in_flight[which][slot] = copy
            n_tiles_out[which] += 1

    def pointwise(fn, tile, which, col_tile, stage_h=False):
        w = w_ref[:, tile * IN_TILE:(tile + 1) * IN_TILE]
        for c in range(n_chunks):
            if stage_h:
                h_scr[rows(c), :] = _norm_mod(x_ref[rows(c), :], g_ref, sc_ref, sh_ref)
            put(fn(_dot(h_scr[rows(c), :], w)), which, col_tile, c)

    def conv(tile, conv_tile, which, col_tile):
        w = w_ref[:, tile * IN_TILE:(tile + 1) * IN_TILE]
        cols = pl.ds(conv_tile * IN_TILE, IN_TILE)
        cw, cb = cw_ref.at[:, cols], cb_ref.at[:, cols]

        def conv_out(c):
            put(_silu(_dwconv3_rows(u_scr, c, chunk, slice(0, IN_TILE), cw, cb, seq_len)), which, col_tile, c)
        for c in range(n_chunks):
            u_scr[_stage_rows(c, chunk), :] = _dot(h_scr[rows(c), :], w)
            if c >= 1:
                conv_out(c - 1)
        conv_out(n_chunks - 1)

    _zero_conv_pads(u_scr)
    pointwise(_silu, 0, 0, 0, stage_h=True)
    pointwise(_silu, 1, 0, 1)
    for tile in range(2, 6):
        pointwise(_sigmoid, tile, 0, tile)
    conv(BC_TILE, 2, 0, BC_TILE)
    conv(N_LOW_TILES, 0, 1, 0)
    conv(N_LOW_TILES + 1, 1, 1, 1)
    for tile in range(N_LOW_TILES + MLA_TILE, N_LOW_TILES + N_F32_TILES):
        pointwise(lambda u: u, tile, 1, tile - N_LOW_TILES)
    for slots in in_flight:
        for copy in slots:
            if copy is not None:
                copy.wait()


def _in_proj_resident(x2d, mod48, mod_row, norm_g, w_in_r, conv_w, conv_b, seq_len):
    t = x2d.shape[0]
    once = dict(pipeline_mode=pl.Buffered(1))
    return pl.pallas_call(
        functools.partial(_in_kernel_resident, seq_len=seq_len),
        grid=(t // ROW_GROUP,),
        in_specs=[pl.BlockSpec((ROW_GROUP, D_MODEL), lambda r: (r, 0)),
                  pl.BlockSpec((None, 1, D_MODEL), lambda r: (mod_row(r) * 6 + 0, 0, 0)),
                  pl.BlockSpec((None, 1, D_MODEL), lambda r: (mod_row(r) * 6 + 1, 0, 0)),
                  pl.BlockSpec((1, D_MODEL), lambda r: (0, 0)),
                  pl.BlockSpec(w_in_r.shape, lambda r: (0, 0), **once),
                  pl.BlockSpec(conv_w.shape, lambda r: (0, 0)),
                  pl.BlockSpec(conv_b.shape, lambda r: (0, 0))],
        out_specs=[pl.BlockSpec(memory_space=pl.ANY), pl.BlockSpec(memory_space=pl.ANY)],
        out_shape=[jax.ShapeDtypeStruct((t, N_LOW_TILES * IN_TILE), BF16),
                   jax.ShapeDtypeStruct((t, N_F32_TILES * IN_TILE), F32)],
        scratch_shapes=[pltpu.VMEM((ROW_GROUP, D_MODEL), BF16),
                        pltpu.VMEM((ROW_GROUP + 2 * CONV_PAD, IN_TILE), F32),
                        pltpu.VMEM((IN_OUT_SLOTS, ROW_GROUP, IN_TILE), BF16),
                        pltpu.VMEM((IN_OUT_SLOTS, ROW_GROUP, IN_TILE), F32),
                        pltpu.SemaphoreType.DMA((2, IN_OUT_SLOTS))],
        compiler_params=_params("arbitrary"),
        name="in_proj",
    )(x2d, mod48, mod48, norm_g, w_in_r, conv_w, conv_b)


def _in_kernel(x_ref, sh_ref, sc_ref, g_ref, w_ref, cw_ref, cb_ref, lo_ref, hi_ref, h_scr, u_scr, *, seq_len):
    j = pl.program_id(1)
    chunk = IN_CHUNK
    n_chunks = ROW_GROUP // chunk
    rows = lambda c: slice(c * chunk, (c + 1) * chunk)

    def pointwise(fn, o_ref, stage_h=False):
        for c in range(n_chunks):
            if stage_h:
                h_scr[rows(c), :] = _norm_mod(x_ref[rows(c), :], g_ref, sc_ref, sh_ref)
            o_ref[rows(c), :] = fn(_dot(h_scr[rows(c), :], w_ref[...])).astype(o_ref.dtype)

    def conv(o_ref):
        def conv_out(c):
            v = _silu(_dwconv3_rows(u_scr, c, chunk, slice(0, IN_TILE), cw_ref, cb_ref, seq_len))
            o_ref[rows(c), :] = v.astype(o_ref.dtype)
        for c in range(n_chunks):
            u_scr[_stage_rows(c, chunk), :] = _dot(h_scr[rows(c), :], w_ref[...])
            if c >= 1:
                conv_out(c - 1)
        conv_out(n_chunks - 1)

    @pl.when(j == 0)
    def _():
        _zero_conv_pads(u_scr)
        pointwise(_silu, lo_ref, stage_h=True)

    @pl.when(j == 1)
    def _():
        pointwise(_silu, lo_ref)

    @pl.when((j >= 2) & (j <= 5))
    def _():
        pointwise(_sigmoid, lo_ref)

    @pl.when(j == BC_TILE)
    def _():
        conv(lo_ref)

    @pl.when((j == N_LOW_TILES) | (j == N_LOW_TILES + 1))
    def _():
        conv(hi_ref)

    @pl.when(j >= N_LOW_TILES + MLA_TILE)
    def _():
        pointwise(lambda u: u, hi_ref)


IN_RESIDENT_MIN_GROUPS = 4


def _in_proj(x2d, mod48, mod_row, norm_g, w_in_r, conv_w, conv_b, seq_len):
    t = x2d.shape[0]
    if t // ROW_GROUP >= IN_RESIDENT_MIN_GROUPS:
        return _in_proj_resident(x2d, mod48, mod_row, norm_g, w_in_r, conv_w, conv_b, seq_len)
    n_tiles = N_LOW_TILES + N_F32_TILES
    conv_idx = lambda r, j: (0, jnp.where(j == BC_TILE, 2, jnp.clip(j - N_LOW_TILES, 0, 1)))
    return pl.pallas_call(
        functools.partial(_in_kernel, seq_len=seq_len),
        grid=(t // ROW_GROUP, n_tiles),
        in_specs=[pl.BlockSpec((ROW_GROUP, D_MODEL), lambda r, j: (r, 0)),
                  pl.BlockSpec((None, 1, D_MODEL), lambda r, j: (mod_row(r) * 6 + 0, 0, 0)),
                  pl.BlockSpec((None, 1, D_MODEL), lambda r, j: (mod_row(r) * 6 + 1, 0, 0)),
                  pl.BlockSpec((1, D_MODEL), lambda r, j: (0, 0)),
                  pl.BlockSpec((D_MODEL, IN_TILE), lambda r, j: (0, j)),
                  pl.BlockSpec((3, IN_TILE), conv_idx),
                  pl.BlockSpec((1, IN_TILE), conv_idx)],
        out_specs=[pl.BlockSpec((ROW_GROUP, IN_TILE), lambda r, j: (r, jnp.minimum(j, N_LOW_TILES - 1))),
                   pl.BlockSpec((ROW_GROUP, IN_TILE), lambda r, j: (r, jnp.maximum(j - N_LOW_TILES, 0)))],
        out_shape=[jax.ShapeDtypeStruct((t, N_LOW_TILES * IN_TILE), BF16),
                   jax.ShapeDtypeStruct((t, N_F32_TILES * IN_TILE), F32)],
        scratch_shapes=[pltpu.VMEM((ROW_GROUP, D_MODEL), BF16),
                        pltpu.VMEM((ROW_GROUP + 2 * CONV_PAD, IN_TILE), F32)],
        compiler_params=_params("arbitrary", "arbitrary"),
        name="in_proj",
    )(x2d, mod48, mod48, norm_g, w_in_r, conv_w, conv_b)


def _mla_kernel(*refs, latent, has_ctx, emit_cache, seq_len, tq, seqs):
    n_in = 7 + (2 if latent else 0) + (2 if has_ctx else 0)
    n_out = 3 if emit_cache else 1
    per_seq = {0, 1} | ({n_in - 2, n_in - 1} if has_ctx else set())
    for bi in range(seqs):
        view = lambda r: r.at[pl.ds(bi, 1)]
        ins = [view(r) if i in per_seq else r for i, r in enumerate(refs[:n_in])]
        outs = [view(r) for r in refs[n_in:n_in + n_out]]
        _mla_sequence(*ins, *outs, *refs[n_in + n_out:], latent=latent, has_ctx=has_ctx, emit_cache=emit_cache,
                      seq_len=seq_len, tq=tq)


def _mla_sequence(*refs, latent, has_ctx, emit_cache, seq_len, tq):
    refs = list(refs)
    pm_ref, px_ref, qg_ref, kvg_ref, wuq_ref, wkt_ref, wv_ref = refs[:7]
    del refs[:7]
    if latent:
        cos_ref, sin_ref = refs[:2]
        del refs[:2]
    if has_ctx:
        cckv_ref, ckr_ref = refs[:2]
        del refs[:2]
    o_ref = refs.pop(0)
    if emit_cache:
        ckv_ref, kr_ref = refs[:2]
        del refs[:2]
    k_scr, v_scr = refs[:2]
    del refs[:2]
    if has_ctx:
        kc_scr, vc_scr = refs
    t = pl.program_id(1)
    n_nope = N_HEADS * QK_NOPE

    def put_keys(k_dst, v_dst, rows, ckv_n, kr_bf):
        ckv_bf = ckv_n.astype(BF16)
        v_dst[rows, :] = _dot(ckv_bf, wv_ref[...]).astype(BF16)
        kn_t = _dot_nt(wkt_ref[...], ckv_bf).astype(BF16)
        width = kr_bf.shape[1]
        eye = jnp.where(lax.broadcasted_iota(jnp.int32, (128, width), 0)
                        == lax.broadcasted_iota(jnp.int32, (128, width), 1), 1.0, 0.0).astype(BF16)
        kr_t = _dot_nt(eye, kr_bf).astype(BF16)
        for h in range(N_HEADS):
            base = h * ATTN_HEAD_COLS
            k_dst[base:base + QK_NOPE, rows] = kn_t[h * QK_NOPE:(h + 1) * QK_NOPE, :]
            k_dst[base + QK_NOPE:base + ATTN_HEAD_COLS, rows] = kr_t

    @pl.when(t == 0)
    def _():
        step = min(seq_len, TOKEN_TILE)
        for r0 in range(0, seq_len, step):
            rows = slice(r0, r0 + step)
            ckv_n = _rmsnorm(pm_ref[0, rows, Q_LORA:], kvg_ref[...])
            kr = px_ref[0, rows, 0:128]
            if latent:
                kr = kr * cos_ref[rows, :] + px_ref[0, rows, 128:256] * sin_ref[rows, :]
            if emit_cache:
                ckv_ref[0, rows, :] = ckv_n
                kr_ref[0, rows, :] = kr[:, :QK_ROPE]
            put_keys(k_scr, v_scr, rows, ckv_n, kr.astype(BF16))
        if has_ctx:
            past = cckv_ref.shape[1]
            put_keys(kc_scr, vc_scr, slice(0, past), cckv_ref[0], ckr_ref[0].astype(BF16))

    qrows = pl.ds(pl.multiple_of(t * tq, tq), tq)
    scale = LOG2_E / math.sqrt(QK_NOPE + QK_ROPE)
    cqn = _rmsnorm(pm_ref[0, qrows, 0:Q_LORA], qg_ref[...]).astype(BF16)
    q = _dot(cqn, wuq_ref[...])
    q_rope = q[:, n_nope:2 * n_nope]
    if latent:
        q_rope = (q_rope * jnp.concatenate([cos_ref[qrows, :]] * N_HEADS, axis=1)
                  + q[:, 2 * n_nope:3 * n_nope] * jnp.concatenate([sin_ref[qrows, :]] * N_HEADS, axis=1))
    segs = ([(kc_scr, vc_scr)] if has_ctx else []) + [(k_scr, v_scr)]
    for h in range(N_HEADS):
        head = slice(h * QK_NOPE, (h + 1) * QK_NOPE)
        qk = slice(h * ATTN_HEAD_COLS, (h + 1) * ATTN_HEAD_COLS)
        qh = (jnp.concatenate([q[:, head], q_rope[:, head]], axis=1) * scale).astype(BF16)
        s = [_dot(qh, k[qk, :]) for k, _ in segs]
        m = functools.reduce(jnp.maximum, [jnp.max(si, axis=-1, keepdims=True) for si in s])
        p = [jnp.exp2(si - m) for si in s]
        l = functools.reduce(jnp.add, [jnp.sum(pi, axis=-1, keepdims=True) for pi in p])
        o = functools.reduce(jnp.add, [_dot(pi.astype(BF16), v[:, head]) for pi, (_, v) in zip(p, segs)])
        o_ref[0, :, head] = (o / l).astype(BF16)


def _mla(proj_hi, q_norm_g, kv_norm_g, w_uq_r, w_uk_t, w_uv, rope_tables, ctx, emit_cache):
    b, l, _ = proj_hi.shape
    tq = min(l, ATTN_Q_TILE)
    latent = rope_tables is not None
    has_ctx = ctx is not None
    once = dict(pipeline_mode=pl.Buffered(1))
    per_batch = once if l // tq > 1 else {}
    seqs = MLA_SEQS_PER_STEP if (l == tq and b % MLA_SEQS_PER_STEP == 0) else 1
    const2 = lambda i, t: (0, 0)
    in_specs = [pl.BlockSpec((seqs, l, IN_TILE), lambda i, t: (i, 0, MLA_TILE), **per_batch),
                pl.BlockSpec((seqs, l, 256), lambda i, t: (i, 0, MISC_TILE * IN_TILE // 256), **per_batch),
                pl.BlockSpec((1, Q_LORA), const2),
                pl.BlockSpec((1, KV_LORA), const2),
                pl.BlockSpec(w_uq_r.shape, const2, **once),
                pl.BlockSpec(w_uk_t.shape, const2, **once),
                pl.BlockSpec(w_uv.shape, const2, **once)]
    args = [proj_hi, proj_hi, q_norm_g, kv_norm_g, w_uq_r, w_uk_t, w_uv]
    scratch = [pltpu.VMEM((N_HEADS * ATTN_HEAD_COLS, l), BF16), pltpu.VMEM((l, N_HEADS * V_HEAD), BF16)]
    if latent:
        in_specs += [pl.BlockSpec((l, 128), const2, **once)] * 2
        args += list(rope_tables)
    if has_ctx:
        past = ctx[0].shape[1]
        in_specs += [pl.BlockSpec((seqs, past, KV_LORA), lambda i, t: (i, 0, 0), **per_batch),
                     pl.BlockSpec((seqs, past, QK_ROPE), lambda i, t: (i, 0, 0), **per_batch)]
        args += list(ctx)
        scratch += [pltpu.VMEM((N_HEADS * ATTN_HEAD_COLS, past), BF16), pltpu.VMEM((past, N_HEADS * V_HEAD), BF16)]
    out_specs = [pl.BlockSpec((seqs, tq, N_HEADS * V_HEAD), lambda i, t: (i, t, 0))]
    out_shape = [jax.ShapeDtypeStruct((b, l, N_HEADS * V_HEAD), BF16)]
    if emit_cache:
        out_specs += [pl.BlockSpec((seqs, l, KV_LORA), lambda i, t: (i, 0, 0)),
                      pl.BlockSpec((seqs, l, QK_ROPE), lambda i, t: (i, 0, 0))]
        out_shape += [jax.ShapeDtypeStruct((b, l, KV_LORA), F32), jax.ShapeDtypeStruct((b, l, QK_ROPE), F32)]
    return pl.pallas_call(
        functools.partial(_mla_kernel, latent=latent, has_ctx=has_ctx, emit_cache=emit_cache, seq_len=l, tq=tq,
                          seqs=seqs),
        grid=(b // seqs, l // tq),
        in_specs=in_specs,
        out_specs=out_specs,
        out_shape=out_shape,
        scratch_shapes=scratch,
        compiler_params=_params("arbitrary", "arbitrary", vmem_limit=MLA_VMEM_LIMIT),
        name="mla_attention",
    )(*args)


def _split3(x):
    hi = x.astype(BF16)
    r = x - hi.astype(F32)
    mid = r.astype(BF16)
    lo = (r - mid.astype(F32)).astype(BF16)
    return hi, mid, lo


def _exact_dot(parts, sel):
    return functools.reduce(jnp.add, [_dot(p, sel) for p in parts])


def _exact_dot_rows(sel, parts):
    return functools.reduce(jnp.add, [_dot(sel, p) for p in parts])


HEADS_PER_GROUP = SSD_HEADS // SSD_GROUPS
GROUP_COLS = HEADS_PER_GROUP * SSD_HEADDIM


def _ssd_kernel(*refs, nc, cps, has_h0, seqs):
    n_in = 8 if has_h0 else 7
    per_seq = {0, 1, 2, 3, 7} if has_h0 else {0, 1, 2, 3}
    for bi in range(seqs):
        view = lambda r: r.at[pl.ds(bi, 1)]
        ins = [view(r) if i in per_seq else r for i, r in enumerate(refs[:n_in])]
        outs = [view(r) for r in refs[n_in:n_in + 2]]
        _ssd_sequence(*ins, *outs, *refs[n_in + 2:], nc=nc, cps=cps, has_h0=has_h0)


def _ssd_sequence(*refs, nc, cps, has_h0):
    if has_h0:
        (xs_ref, zs_ref, bc_ref, dt_ref, dtb_ref, a_ref, dx_ref, h0_ref, y_ref, ht_ref,
         yl_scr, acum_scr, src_scr, tot_scr, sb_scr, h_scr, esel_scr) = refs
    else:
        (xs_ref, zs_ref, bc_ref, dt_ref, dtb_ref, a_ref, dx_ref, y_ref, ht_ref,
         yl_scr, acum_scr, src_scr, tot_scr, sb_scr, h_scr, esel_scr) = refs
    s = pl.program_id(1)
    q = SSD_CHUNK
    n_bc = SSD_GROUPS * SSD_STATE
    lane = lax.broadcasted_iota(jnp.int32, (q, 128), 1)
    low_half = lane < SSD_HEADDIM
    ii = lax.broadcasted_iota(jnp.int32, (q, q), 0)
    jj = lax.broadcasted_iota(jnp.int32, (q, q), 1)
    lower, upper = ii >= jj, ii <= jj

    def lane_bcast(parts, d):
        return _exact_dot(parts, esel_scr[d])

    def stacked_states(d):
        return [jnp.concatenate([h_scr[d, 2 * i], h_scr[d, 2 * i + 1]], axis=0).astype(BF16)
                for i in range(SSD_GROUPS // 2)]

    def group_c(bc, g):
        i, r = divmod(g, 2)
        cpair = bc[:, n_bc + i * 128:n_bc + (i + 1) * 128]
        return jnp.where(low_half if r == 0 else ~low_half, cpair, 0.0).astype(BF16)

    @pl.when(s == 0)
    def _():
        k = lax.broadcasted_iota(jnp.int32, (128, SSD_INNER), 0)
        head = lax.broadcasted_iota(jnp.int32, (128, SSD_INNER), 1) // SSD_HEADDIM
        for d in range(2):
            esel_scr[d] = jnp.where(k == d * SSD_HEADS + head, 1.0, 0.0).astype(BF16)
        if has_h0:
            for d in range(2):
                for g in range(SSD_GROUPS):
                    hpn = h0_ref[0, d, g * HEADS_PER_GROUP:(g + 1) * HEADS_PER_GROUP].reshape(GROUP_COLS, SSD_STATE)
                    h_scr[d, g] = hpn.T
        else:
            h_scr[...] = jnp.zeros(h_scr.shape, F32)
        tri_f = jnp.where(lower, 1.0, 0.0).astype(BF16)
        tri_b = jnp.where(upper, 1.0, 0.0).astype(BF16)
        fwd_col = lane < SSD_HEADS
        for c in range(nc):
            crow = slice(c * q, (c + 1) * q)
            dt = _softplus(dt_ref[0, crow, :] + dtb_ref[...])
            parts = _split3(dt * a_ref[...])
            acum = jnp.where(fwd_col, _exact_dot_rows(tri_f, parts),
                             _exact_dot_rows(tri_b, parts))
            acum = acum * LOG2_E
            acum_scr[crow, :] = acum
            tot = jnp.where(fwd_col[0:1], acum[q - 1:q, :], acum[0:1, :])
            tot_scr[c] = jnp.broadcast_to(tot, (8, 128))
            src_scr[c] = (acum - jnp.log2(dt)).T

    def first_sweep(c, blk):
        rows = pl.ds(pl.multiple_of(c * q, q), q)
        x = xs_ref[0, blk, :]
        bc = bc_ref[0, blk, :].astype(F32)
        acum = acum_scr[rows, :]
        src_t = src_scr[c]
        tot8 = tot_scr[c]
        e = jnp.exp2(acum)
        e_hi = e.astype(BF16)
        eb_f = lane_bcast([e_hi, (e - e_hi.astype(F32)).astype(BF16)], 0)
        cd_f = jnp.exp2(lane_bcast(_split3(tot8), 0))[0:1]
        b_t = [bc[:, i * 128:(i + 1) * 128].T for i in range(SSD_GROUPS // 2)]
        h_in = stacked_states(0)
        for g in range(SSD_GROUPS):
            i, r = divmod(g, 2)
            cm = group_c(bc, g)
            scores = _dot_nt(cm, bc[:, i * 128:(i + 1) * 128].astype(BF16))
            bg_t = b_t[i][r * SSD_STATE:(r + 1) * SSD_STATE, :]
            gcols = slice(g * GROUP_COLS, (g + 1) * GROUP_COLS)
            y_off = _dot(cm, h_in[i]) * eb_f[:, gcols]
            for t in range(HEADS_PER_GROUP // 2):
                pc = slice((2 * g + t) * 128, (2 * g + t + 1) * 128)
                xp = x[:, pc]
                x2 = jnp.concatenate([jnp.where(low_half, xp, 0.0), jnp.where(low_half, 0.0, xp)], axis=0).astype(BF16)
                m, sf, sb = [], [], []
                for u in range(2):
                    kf = g * HEADS_PER_GROUP + 2 * t + u
                    kb = SSD_HEADS + kf
                    af_col = jnp.broadcast_to(acum[:, kf:kf + 1], (q, q))
                    ab_col = jnp.broadcast_to(acum[:, kb:kb + 1], (q, q))
                    sf_row, sb_row = src_t[kf:kf + 1, :], src_t[kb:kb + 1, :]
                    decay = (jnp.exp2(jnp.where(lower, af_col - sf_row, NEG_BIG))
                             + jnp.exp2(jnp.where(upper, ab_col - sb_row, NEG_BIG)))
                    m.append((scores * decay).astype(BF16))
                    wf = jnp.exp2(tot8[0:1, kf:kf + 1] - sf_row)
                    wb = jnp.exp2(tot8[0:1, kb:kb + 1] - sb_row)
                    sf.append((bg_t * wf).astype(BF16))
                    sb.append((bg_t * wb).astype(BF16))
                y_pair = _dot(jnp.concatenate(m, axis=1), x2)
                tc = slice(t * 128, (t + 1) * 128)
                yl_scr[rows, pc] = y_pair + y_off[:, tc] + dx_ref[:, pc] * xp
                h_scr[0, g, :, tc] = h_scr[0, g, :, tc] * cd_f[:, pc] + _dot(jnp.concatenate(sf, axis=1), x2)
                sb_scr[c, g, :, tc] = _dot(jnp.concatenate(sb, axis=1), x2)

    def second_sweep(c, blk):
        rows = pl.ds(pl.multiple_of(c * q, q), q)
        bc = bc_ref[0, blk, :].astype(F32)
        e = jnp.exp2(acum_scr[rows, :])
        e_hi = e.astype(BF16)
        eb_b = lane_bcast([e_hi, (e - e_hi.astype(F32)).astype(BF16)], 1)
        cd_b = jnp.exp2(lane_bcast(_split3(tot_scr[c]), 1))[0:1]
        h_in = stacked_states(1)
        for g in range(SSD_GROUPS):
            gcols = slice(g * GROUP_COLS, (g + 1) * GROUP_COLS)
            y_off = _dot(group_c(bc, g), h_in[g // 2]) * eb_b[:, gcols]
            y_ref[0, blk, gcols] = ((yl_scr[rows, gcols] + y_off) * zs_ref[0, blk, gcols].astype(F32)).astype(y_ref.dtype)
            h_scr[1, g] = h_scr[1, g] * cd_b[:, gcols] + sb_scr[c, g]

    n_steps = nc // cps
    one_step = n_steps == 1

    @pl.when(s < n_steps)
    def _():
        for ci in range(cps):
            first_sweep(s * cps + ci, slice(ci * q, (ci + 1) * q))

    @pl.when(jnp.logical_or(one_step, s >= n_steps))
    def _():
        blk_id = s * 0 if one_step else 2 * n_steps - 1 - s
        for ci in reversed(range(cps)):
            second_sweep(blk_id * cps + ci, slice(ci * q, (ci + 1) * q))

    @pl.when(jnp.logical_or(one_step, s == 2 * n_steps - 1))
    def _():
        for d in range(2):
            for g in range(SSD_GROUPS):
                ht_ref[0, d, g * HEADS_PER_GROUP:(g + 1) * HEADS_PER_GROUP] = h_scr[d, g].T.reshape(
                    HEADS_PER_GROUP, SSD_HEADDIM, SSD_STATE)


def _ssd(proj_lo, proj_hi, h0, dt_bias128, a128, d_exp):
    b, l, _ = proj_hi.shape
    q = SSD_CHUNK
    nc = l // q
    cps = min(nc, SSD_CHUNKS_PER_STEP)
    n_steps = nc // cps
    n_grid = 1 if n_steps == 1 else 2 * n_steps
    rows = cps * q
    early = lambda s: jnp.minimum(s, n_steps - 1)
    both = lambda s: jnp.where(s < n_steps, s, 2 * n_steps - 1 - s)
    late = lambda s: jnp.where(s < n_steps, n_steps - 1, 2 * n_steps - 1 - s)
    seqs = SSD_SEQS_PER_STEP if (n_steps == 1 and b % SSD_SEQS_PER_STEP == 0) else 1
    st_shape = (seqs, 2, SSD_HEADS, SSD_HEADDIM, SSD_STATE)
    st_spec = pl.BlockSpec(st_shape, lambda i, s: (i, 0, 0, 0, 0))
    has_h0 = h0 is not None
    return pl.pallas_call(
        functools.partial(_ssd_kernel, nc=nc, cps=cps, has_h0=has_h0, seqs=seqs),
        grid=(b // seqs, n_grid),
        in_specs=[pl.BlockSpec((seqs, rows, SSD_INNER), lambda i, s: (i, early(s), XS_BLK)),
                  pl.BlockSpec((seqs, rows, SSD_INNER), lambda i, s: (i, late(s), Z_BLK)),
                  pl.BlockSpec((seqs, rows, IN_TILE), lambda i, s: (i, both(s), BC_TILE)),
                  pl.BlockSpec((seqs, l, 128), lambda i, s: (i, 0, DT_BLK)),
                  pl.BlockSpec((1, 128), lambda i, s: (0, 0)),
                  pl.BlockSpec((1, 128), lambda i, s: (0, 0)),
                  pl.BlockSpec((1, SSD_INNER), lambda i, s: (0, 0))] + ([st_spec] if has_h0 else []),
        out_specs=[pl.BlockSpec((seqs, rows, SSD_INNER), lambda i, s: (i, late(s), 0)),
                   pl.BlockSpec(st_shape, lambda i, s: (i, 0, 0, 0, 0))],
        out_shape=[jax.ShapeDtypeStruct((b, l, SSD_INNER), BF16),
                   jax.ShapeDtypeStruct((b,) + st_shape[1:], F32)],
        scratch_shapes=[pltpu.VMEM((l, SSD_INNER), F32),
                        pltpu.VMEM((l, 128), F32),
                        pltpu.VMEM((nc, 128, q), F32),
                        pltpu.VMEM((nc, 8, 128), F32),
                        pltpu.VMEM((nc, SSD_GROUPS, SSD_STATE, GROUP_COLS), F32),
                        pltpu.VMEM((2, SSD_GROUPS, SSD_STATE, GROUP_COLS), F32),
                        pltpu.VMEM((2, 128, SSD_INNER), BF16)],
        compiler_params=_params("arbitrary", "arbitrary"),
        name="ssd_scan",
    )(proj_hi, proj_lo, proj_lo, proj_hi, dt_bias128, a128, d_exp, *([h0] if has_h0 else []))


def _merge_kernel(attn_ref, yz_ref, gm_ref, gs_ref, x_ref, g1_ref, ng_ref, womla_ref, wossd_ref, wout_ref, o_ref, w_scr):
    @pl.when(pl.program_id(0) == 0)
    def _():
        w_scr[0] = womla_ref[...].astype(BF16)
        w_scr[1] = wossd_ref[...].astype(BF16)
        w_scr[2] = wout_ref[...].astype(BF16)

    o_mla = _dot(attn_ref[...], w_scr[0])
    o_ssd = _dot(_rmsnorm(yz_ref[...].astype(F32), ng_ref[...]).astype(BF16), w_scr[1])
    merged = gm_ref[...].astype(F32) * o_mla + gs_ref[...].astype(F32) * o_ssd
    o_ref[...] = x_ref[...] + g1_ref[...] * _dot(merged.astype(BF16), w_scr[2])


def _merge(attn2d, yz2d, proj, x2d, mod48, mod_row, ssd_norm_g, w_o_mla, w_o_ssd, w_out):
    t = x2d.shape[0]
    tm = TOKEN_TILE
    row = lambda i: (i, 0)
    const = lambda i: (0, 0)
    wspec = pl.BlockSpec((D_MODEL, D_MODEL), const, pipeline_mode=pl.Buffered(1))
    return pl.pallas_call(
        _merge_kernel,
        grid=(t // tm,),
        in_specs=[pl.BlockSpec((tm, D_MODEL), row),
                  pl.BlockSpec((tm, D_MODEL), row),
                  pl.BlockSpec((tm, D_MODEL), lambda i: (i, GM_BLK)),
                  pl.BlockSpec((tm, D_MODEL), lambda i: (i, GS_BLK)),
                  pl.BlockSpec((tm, D_MODEL), row),
                  pl.BlockSpec((None, 1, D_MODEL), lambda i: (mod_row(i * tm // ROW_GROUP) * 6 + 2, 0, 0)),
                  pl.BlockSpec((1, D_MODEL), const),
                  wspec, wspec, wspec],
        out_specs=pl.BlockSpec((tm, D_MODEL), row),
        out_shape=jax.ShapeDtypeStruct((t, D_MODEL), F32),
        scratch_shapes=[pltpu.VMEM((3, D_MODEL, D_MODEL), BF16)],
        compiler_params=_params("arbitrary"),
        name="merge_out",
    )(attn2d, yz2d, proj, proj, x2d, mod48, ssd_norm_g, w_o_mla, w_o_ssd, w_out)


def _ffn_kernel(x_ref, sh_ref, sc_ref, g2_ref, ng_ref, wg_ref, wv_ref, cwg_ref, cwv_ref, cbg_ref, cbv_ref, wd_ref,
                fg_ref, o_ref, h_scr, wup_scr, wd_scr, u_scr, *, seq_len):
    j = pl.program_id(1)
    chunk = FFN_CHUNK
    n_chunks = ROW_GROUP // chunk
    rows = lambda c: slice(c * chunk, (c + 1) * chunk)

    wd_scr[...] = wd_ref[...].astype(BF16)

    def first_up(h_rows):
        n_k = 4
        acc = None
        for k in range(n_k):
            ks = slice(k * (D_MODEL // n_k), (k + 1) * (D_MODEL // n_k))
            wup_scr[ks, 0:FFN_TILE] = wg_ref[ks, :].astype(BF16)
            wup_scr[ks, FFN_TILE:2 * FFN_TILE] = wv_ref[ks, :].astype(BF16)
            part = _dot(h_scr[h_rows, ks], wup_scr[ks, :])
            acc = part if acc is None else acc + part
        return acc

    def gated(c):
        ug = _dwconv3_rows(u_scr, c, chunk, slice(0, FFN_TILE), cwg_ref, cbg_ref, seq_len)
        uv = _dwconv3_rows(u_scr, c, chunk, slice(FFN_TILE, 2 * FFN_TILE), cwv_ref, cbv_ref, seq_len)
        return (_silu(ug) * uv).astype(BF16)

    def pipeline(first, last):
        act = {}
        lag = FFN_DOWN_LAG
        for c in range(n_chunks + lag):
            if c < n_chunks:
                if first:
                    h_scr[rows(c), :] = _norm_mod(x_ref[rows(c), :], ng_ref, sc_ref, sh_ref)
                up = first_up(rows(c)) if c == 0 else _dot(h_scr[rows(c), :], wup_scr[...])
                u_scr[_stage_rows(c, chunk), :] = up
            if c >= lag:
                r = rows(c - lag)
                acc = _dot(act.pop(c - lag), wd_scr[...])
                if not first:
                    acc = o_ref[r, :] + acc
                if last:
                    acc = _rmsnorm(x_ref[r, :] + g2_ref[...] * acc, fg_ref[...])
                o_ref[r, :] = acc
            if 1 <= c <= n_chunks:
                act[c - 1] = gated(c - 1)

    last_j = pl.num_programs(1) - 1

    @pl.when(j == 0)
    def _():
        _zero_conv_pads(u_scr)
        pipeline(True, False)

    @pl.when((j > 0) & (j < last_j))
    def _():
        pipeline(False, False)

    @pl.when(j == last_j)
    def _():
        pipeline(False, True)


def _ffn(x2d, mod48, mod_row, norm_g, w_up, conv_w, conv_b, w_down, final_g, seq_len):
    t = x2d.shape[0]
    nj = D_FF // FFN_TILE
    gate = lambda r, j: (0, j)
    val = lambda r, j: (0, nj + j)
    const = lambda r, j: (0, 0)
    mod = lambda k: pl.BlockSpec((None, 1, D_MODEL), lambda r, j: (mod_row(r) * 6 + k, 0, 0))
    return pl.pallas_call(
        functools.partial(_ffn_kernel, seq_len=seq_len),
        grid=(t // ROW_GROUP, nj),
        in_specs=[pl.BlockSpec((ROW_GROUP, D_MODEL), lambda r, j: (r, 0)),
                  mod(3), mod(4), mod(5),
                  pl.BlockSpec((1, D_MODEL), const),
                  pl.BlockSpec((D_MODEL, FFN_TILE), gate),
                  pl.BlockSpec((D_MODEL, FFN_TILE), val),
                  pl.BlockSpec((3, FFN_TILE), gate),
                  pl.BlockSpec((3, FFN_TILE), val),
                  pl.BlockSpec((1, FFN_TILE), gate),
                  pl.BlockSpec((1, FFN_TILE), val),
                  pl.BlockSpec((FFN_TILE, D_MODEL), lambda r, j: (j, 0)),
                  pl.BlockSpec((1, D_MODEL), const)],
        out_specs=pl.BlockSpec((ROW_GROUP, D_MODEL), lambda r, j: (r, 0)),
        out_shape=jax.ShapeDtypeStruct((t, D_MODEL), F32),
        scratch_shapes=[pltpu.VMEM((ROW_GROUP, D_MODEL), BF16),
                        pltpu.VMEM((D_MODEL, 2 * FFN_TILE), BF16),
                        pltpu.VMEM((FFN_TILE, D_MODEL), BF16),
                        pltpu.VMEM((ROW_GROUP + 2 * CONV_PAD, 2 * FFN_TILE), F32)],
        compiler_params=_params("arbitrary", "arbitrary"),
        name="conv_ffn",
    )(x2d, mod48, mod48, mod48, norm_g, w_up, w_up, conv_w, conv_w, conv_b, conv_b, w_down, final_g)


def _rope_tables(seq_len):
    t = np.arange(seq_len)
    row = (t // GRID_W).astype(np.float32)
    col = (t % GRID_W).astype(np.float32)
    n = QK_ROPE // 4
    inv = (np.float32(ROPE_BASE) ** (-np.arange(n, dtype=np.float32) / np.float32(n))).astype(np.float32)
    ar, ac = row[:, None] * inv, col[:, None] * inv
    cos64 = np.concatenate([np.cos(ar), np.cos(ar), np.cos(ac), np.cos(ac)], axis=1)
    sin64 = np.concatenate([-np.sin(ar), np.sin(ar), -np.sin(ac), np.sin(ac)], axis=1)
    zeros = np.zeros_like(cos64)
    return (jnp.asarray(np.concatenate([cos64, zeros], axis=1), F32),
            jnp.asarray(np.concatenate([sin64, zeros], axis=1), F32))


def _swap_rope_halves(w):
    lead = w.shape[:-1]
    return w.reshape(lead + (2, 2, QK_ROPE // 4))[..., ::-1, :].reshape(lead + (QK_ROPE,))


def _trunk_pass(x, mod48, mod_row, wts, ctx, latent):
    b, l, _ = x.shape
    x2d = x.reshape(b * l, D_MODEL)
    proj_lo, proj_hi = _in_proj(x2d, mod48, mod_row, wts["norm_attn_g"], wts["w_in_r"], wts["ssd_conv_w"],
                                wts["ssd_conv_b"], l)
    rope = _rope_tables(l) if latent else None
    w_uq_r = wts["w_uq_lat"] if latent else wts["w_uq_ctx"]
    shape3 = lambda a, n: a.reshape(b, n, a.shape[-1])
    h0 = None
    mla_ctx = None
    if ctx is not None:
        cache_ckv, cache_krope, h0 = ctx
        mla_ctx = (cache_ckv, cache_krope)
    emit_cache = ctx is None
    mla_out = _mla(shape3(proj_hi, l), wts["q_norm_g"], wts["kv_norm_g"], w_uq_r, wts["w_uk_t"], wts["w_uv"],
                   rope, mla_ctx, emit_cache)
    attn, ckv_n, kr3 = mla_out if emit_cache else (mla_out[0], None, None)
    yz, h_t = _ssd(shape3(proj_lo, l), shape3(proj_hi, l), h0, wts["dt_bias128"], wts["a128"], wts["d_exp"])
    x1 = _merge(attn.reshape(b * l, -1), yz.reshape(b * l, -1), proj_lo, x2d, mod48, mod_row, wts["ssd_norm_g"],
                wts["w_o_mla"], wts["w_o_ssd"], wts["w_out"])
    y = _ffn(x1, mod48, mod_row, wts["norm_ffn_g"], wts["w_up"], wts["ffn_conv_w"], wts["ffn_conv_b"], wts["w_down"],
             wts["final_norm_g"], l)
    return y.reshape(b, l, D_MODEL), ckv_n, kr3, h_t


def kernel(x_prompt, x_sample, c, cache_ckv, cache_krope, state_ssd, c_ctx, w_ada, b_ada, norm_attn_g, w_in, q_norm_g,
           kv_norm_g, w_uq, w_ukv, w_o_mla, ssd_conv_w, ssd_conv_b, ssd_dt_bias, ssd_A_log, ssd_D, ssd_norm_g, w_o_ssd,
           w_out, norm_ffn_g, w_up, ffn_conv_w, ffn_conv_b, w_down, final_norm_g):
    depth = w_in.shape[0]
    assert depth == 1, "single trunk layer"
    dec_b = x_sample.shape[0]
    assert x_sample.shape[1] == ROW_GROUP and ROW_GROUP % x_prompt.shape[1] == 0
    lyr = 0

    cvec = jnp.zeros((8, D_MODEL), F32).at[0].set(c_ctx).at[1:1 + dec_b].set(c)
    mod48 = _ada(cvec, w_ada[lyr], b_ada[lyr]).reshape(8 * 6, 1, D_MODEL)

    w_in_r = _regroup_w_in(w_in[lyr].T)
    wq = w_uq[lyr].reshape(Q_LORA, N_HEADS, QK_NOPE + QK_ROPE)
    wq_nope = wq[:, :, :QK_NOPE].reshape(Q_LORA, -1)
    wq_rope = wq[:, :, QK_NOPE:]
    pad_rope = lambda w: jnp.pad(w, ((0, 0), (0, 0), (0, 128 - QK_ROPE))).reshape(Q_LORA, -1)
    w_uq_ctx = jnp.concatenate([wq_nope, pad_rope(wq_rope)], axis=1).astype(BF16)
    w_uq_lat = jnp.concatenate([wq_nope, pad_rope(wq_rope), pad_rope(_swap_rope_halves(wq_rope))], axis=1).astype(BF16)
    wkv = w_ukv[lyr].reshape(KV_LORA, N_HEADS, QK_NOPE + V_HEAD)
    w_uk_t = wkv[:, :, :QK_NOPE].reshape(KV_LORA, -1).T.astype(BF16)
    w_uv = wkv[:, :, QK_NOPE:].reshape(KV_LORA, -1).astype(BF16)
    pad128 = lambda a: jnp.pad(a.reshape(1, -1), ((0, 0), (0, 128 - a.size)))
    wts = {
        "norm_attn_g": norm_attn_g[lyr].reshape(1, -1), "w_in_r": w_in_r,
        "ssd_conv_w": ssd_conv_w[lyr], "ssd_conv_b": ssd_conv_b[lyr].reshape(1, -1),
        "q_norm_g": q_norm_g[lyr].reshape(1, -1), "kv_norm_g": kv_norm_g[lyr].reshape(1, -1),
        "w_uq_ctx": w_uq_ctx, "w_uq_lat": w_uq_lat, "w_uk_t": w_uk_t, "w_uv": w_uv,
        "dt_bias128": pad128(ssd_dt_bias[lyr]), "a128": pad128(-jnp.exp(ssd_A_log[lyr])),
        "d_exp": jnp.repeat(ssd_D[lyr], SSD_HEADDIM).reshape(1, -1),
        "ssd_norm_g": ssd_norm_g[lyr].reshape(1, -1),
        "w_o_mla": w_o_mla[lyr], "w_o_ssd": w_o_ssd[lyr], "w_out": w_out[lyr],
        "norm_ffn_g": norm_ffn_g[lyr].reshape(1, -1), "w_up": w_up[lyr],
        "ffn_conv_w": ffn_conv_w[lyr], "ffn_conv_b": ffn_conv_b[lyr].reshape(1, -1),
        "w_down": w_down[lyr], "final_norm_g": final_norm_g.reshape(1, -1),
    }

    y_p, ckv_p, kr_p, st_p = _trunk_pass(x_prompt, mod48, lambda r: 0, wts, None, False)
    ctx = (cache_ckv[:, lyr], cache_krope[:, lyr], state_ssd[:, lyr])
    y_s, _, _, _ = _trunk_pass(x_sample, mod48, lambda r: 1 + r, wts, ctx, True)
    return y_p, y_s, ckv_p[:, None], kr_p[:, None], st_p[:, None]
```

```python
import functools
import math

import jax
import jax.numpy as jnp
import numpy as np
from jax import lax
from jax.experimental import pallas as pl
from jax.experimental.pallas import tpu as pltpu

F32 = jnp.float32
BF16 = jnp.bfloat16

D_MODEL = 1024
GRID_W = 64
N_HEADS = 8
QK_NOPE = 128
QK_ROPE = 64
V_HEAD = 128
Q_LORA = 256
KV_LORA = 256
ROPE_BASE = 10000.0
SSD_HEADS = 16
SSD_HEADDIM = 64
SSD_INNER = SSD_HEADS * SSD_HEADDIM
SSD_GROUPS = 4
SSD_STATE = 64
SSD_CHUNK = 128
D_FF = 2816
EPS = 1e-6

ROW_GROUP = 2048
IN_TILE = 512
SSD_CHUNKS_PER_STEP = 8
SSD_SEQS_PER_STEP = 2
MLA_SEQS_PER_STEP = 2
FFN_TILE = 256
TOKEN_TILE = 512
ATTN_Q_TILE = 512
IN_CHUNK = 256
FFN_CHUNK = 512
FFN_DOWN_LAG = 2
CONV_PAD = 8
VMEM_LIMIT = 56 * 1024 * 1024
MLA_VMEM_LIMIT = 62 * 1024 * 1024
NEG_BIG = -1e30
LOG2_E = 1.4426950408889634


def _sigmoid(x):
    return 1.0 / (1.0 + jnp.exp(-x))


def _silu(x):
    return x * _sigmoid(x)


def _softplus(x):
    e = jnp.exp(-jnp.abs(x))
    u = 1.0 + e
    log1p_e = jnp.where(u == 1.0, e, e * jnp.log(u) / jnp.where(u == 1.0, 1.0, u - 1.0))
    return jnp.maximum(x, 0.0) + log1p_e


def _rmsnorm(x, g):
    return x * lax.rsqrt(jnp.mean(x * x, axis=-1, keepdims=True) + EPS) * g


def _dot(a, b):
    return jnp.dot(a, b, preferred_element_type=F32)


def _dot_nt(a, b):
    return lax.dot_general(a, b, (((1,), (1,)), ((), ())), preferred_element_type=F32)


def _params(*sem, vmem_limit=VMEM_LIMIT):
    return pltpu.CompilerParams(dimension_semantics=sem, vmem_limit_bytes=vmem_limit)


def _norm_mod(x, g_ref, sc_ref, sh_ref):
    return (_rmsnorm(x, g_ref[...]) * (1.0 + sc_ref[...]) + sh_ref[...]).astype(BF16)


def _zero_conv_pads(u_scr):
    zeros = jnp.zeros((CONV_PAD, u_scr.shape[1]), F32)
    u_scr[0:CONV_PAD, :] = zeros
    u_scr[CONV_PAD + ROW_GROUP:2 * CONV_PAD + ROW_GROUP, :] = zeros


def _stage_rows(c, chunk):
    return slice(CONV_PAD + c * chunk, CONV_PAD + (c + 1) * chunk)


def _dwconv3_rows(u_scr, c, chunk, cols, w_ref, b_ref, seq_len):
    r0 = c * chunk
    base = CONV_PAD + r0
    width = cols.stop - cols.start
    prev = u_scr[base - 1:base - 1 + chunk, cols]
    cur = u_scr[base:base + chunk, cols]
    nxt = u_scr[base + 1:base + 1 + chunk, cols]
    pos = (lax.broadcasted_iota(jnp.int32, (chunk, width), 0) + r0) & (seq_len - 1)
    if r0 % seq_len == 0 or chunk > seq_len:
        prev = jnp.where(pos == 0, 0.0, prev)
    if (r0 + chunk) % seq_len == 0 or chunk > seq_len:
        nxt = jnp.where(pos == seq_len - 1, 0.0, nxt)
    return prev * w_ref[0:1, :] + cur * w_ref[1:2, :] + nxt * w_ref[2:3, :] + b_ref[...]


def _ada_kernel(c_ref, w_ref, b_ref, o_ref):
    a = _silu(c_ref[...]).astype(BF16)
    o_ref[...] = _dot(a, w_ref[...].astype(BF16)) + b_ref[...]


def _ada(cvec, w_ada, b_ada):
    tn = 1536
    return pl.pallas_call(
        _ada_kernel,
        grid=(6 * D_MODEL // tn,),
        in_specs=[pl.BlockSpec((8, D_MODEL), lambda j: (0, 0)),
                  pl.BlockSpec((D_MODEL, tn), lambda j: (0, j)),
                  pl.BlockSpec((1, tn), lambda j: (0, j))],
        out_specs=pl.BlockSpec((8, tn), lambda j: (0, j)),
        out_shape=jax.ShapeDtypeStruct((8, 6 * D_MODEL), F32),
        compiler_params=_params("arbitrary"),
        name="ada_mod",
    )(cvec, w_ada, b_ada.reshape(1, -1))


IN_SPLITS = (Q_LORA, KV_LORA, QK_ROPE, SSD_INNER, SSD_INNER, SSD_GROUPS * SSD_STATE, SSD_GROUPS * SSD_STATE,
             2 * SSD_HEADS, D_MODEL, D_MODEL)
IN_OFFSETS = tuple(int(v) for v in np.cumsum((0,) + IN_SPLITS))


def _regroup_kernel(w_ref, o_ref):
    dst = 0

    def put_block(block):
        nonlocal dst
        o_ref[:, dst:dst + 128] = block.T.astype(BF16)
        dst += 128

    def piece(i):
        for off in range(0, IN_SPLITS[i], 128):
            put_block(w_ref[IN_OFFSETS[i] + off:IN_OFFSETS[i] + off + 128, :])

    def padded(parts):
        n = sum(p.shape[0] for p in parts)
        put_block(jnp.concatenate(parts + [jnp.zeros((128 - n, w_ref.shape[1]), F32)], axis=0))

    for i in (3, 8, 9, 5, 6, 4, 0, 1):
        piece(i)
    kr0, n = IN_OFFSETS[2], QK_ROPE // 4
    padded([w_ref[kr0:kr0 + QK_ROPE, :]])
    padded([w_ref[kr0 + blk * n:kr0 + (blk + 1) * n, :] for blk in (1, 0, 3, 2)])
    padded([w_ref[IN_OFFSETS[7]:IN_OFFSETS[7] + IN_SPLITS[7], :]])
    o_ref[:, dst:] = jnp.zeros((o_ref.shape[0], o_ref.shape[1] - dst), BF16)


def _regroup_w_in(w_in_t):
    cols = 256
    n_out = (N_LOW_TILES + N_F32_TILES) * IN_TILE
    return pl.pallas_call(
        _regroup_kernel,
        grid=(D_MODEL // cols,),
        in_specs=[pl.BlockSpec((w_in_t.shape[0], cols), lambda i: (0, i))],
        out_specs=pl.BlockSpec((cols, n_out), lambda i: (i, 0)),
        out_shape=jax.ShapeDtypeStruct((D_MODEL, n_out), BF16),
        compiler_params=_params("arbitrary"),
        name="w_in_regroup",
    )(w_in_t)


N_LOW_TILES, N_F32_TILES = 7, 4
Z_BLK, GM_BLK, GS_BLK = 0, 1, 2
BC_TILE = 6
XS_BLK = 0
MLA_TILE, MISC_TILE = 2, 3
DT_BLK = (MISC_TILE * IN_TILE + 256) // 128
ATTN_HEAD_COLS = 256


IN_OUT_SLOTS = 2


def _in_kernel_resident(x_ref, sh_ref, sc_ref, g_ref, w_hbm, cw_ref, cb_ref, lo_hbm, hi_hbm, h_scr, u_scr, lo_stage,
                        hi_stage, w_scr, sems, w_sems, *, seq_len):
    r = pl.program_id(0)
    chunk = IN_CHUNK
    n_chunks = ROW_GROUP // chunk
    rows = lambda c: slice(c * chunk, (c + 1) * chunk)

    def w_copy(tile):
        cols = pl.ds(tile * IN_TILE, IN_TILE)
        return pltpu.make_async_copy(w_hbm.at[:, cols], w_scr.at[:, cols], w_sems.at[tile])

    @pl.when(r == 0)
    def _():
        for tile in range(N_LOW_TILES + N_F32_TILES):
            w_copy(tile).start()

    def weights(tile):
        @pl.when(r == 0)
        def _():
            w_copy(tile).wait()
        return w_scr[:, tile * IN_TILE:(tile + 1) * IN_TILE]

    outputs = ((lo_hbm, lo_stage), (hi_hbm, hi_stage))
    in_flight = [[None] * IN_OUT_SLOTS for _ in outputs]
    n_tiles_out = [0 for _ in outputs]

    def put(val, which, col_tile, c):
        hbm, stage = outputs[which]
        slot = n_tiles_out[which] % IN_OUT_SLOTS
        if c == 0 and in_flight[which][slot] is not None:
            in_flight[which][slot].wait()
        stage[slot, rows(c), :] = val.astype(stage.dtype)
        if c == n_chunks - 1:
            row0 = pl.multiple_of(r * ROW_GROUP, ROW_GROUP)
            dst = hbm.at[pl.ds(row0, ROW_GROUP), pl.ds(col_tile * IN_TILE, IN_TILE)]
            copy = pltpu.make_async_copy(stage.at[slot], dst, sems.at[which, slot])
            copy.start()
            in_flight[which][slot] = copy
            n_tiles_out[which] += 1

    def pointwise(fn, tile, which, col_tile, stage_h=False):
        w = weights(tile)
        for c in range(n_chunks):
            if stage_h:
                h_scr[rows(c), :] = _norm_mod(x_ref[rows(c), :], g_ref, sc_ref, sh_ref)
            put(fn(_dot(h_scr[rows(c), :], w)), which, col_tile, c)

    def conv(tile, conv_tile, which, col_tile):
        w = weights(tile)
        cols = pl.ds(conv_tile * IN_TILE, IN_TILE)
        cw, cb = cw_ref.at[:, cols], cb_ref.at[:, cols]

        def conv_out(c):
            put(_silu(_dwconv3_rows(u_scr, c, chunk, slice(0, IN_TILE), cw, cb, seq_len)), which, col_tile, c)
        for c in range(n_chunks):
            u_scr[_stage_rows(c, chunk), :] = _dot(h_scr[rows(c), :], w)
            if c >= 1:
                conv_out(c - 1)
        conv_out(n_chunks - 1)

    _zero_conv_pads(u_scr)
    pointwise(_silu, 0, 0, 0, stage_h=True)
    pointwise(_silu, 1, 0, 1)
    for tile in range(2, 6):
        pointwise(_sigmoid, tile, 0, tile)
    conv(BC_TILE, 2, 0, BC_TILE)
    conv(N_LOW_TILES, 0, 1, 0)
    conv(N_LOW_TILES + 1, 1, 1, 1)
    for tile in range(N_LOW_TILES + MLA_TILE, N_LOW_TILES + N_F32_TILES):
        pointwise(lambda u: u, tile, 1, tile - N_LOW_TILES)
    for slots in in_flight:
        for copy in slots:
            if copy is not None:
                copy.wait()


def _in_proj_resident(x2d, mod48, mod_row, norm_g, w_in_r, conv_w, conv_b, seq_len):
    t = x2d.shape[0]
    once = dict(pipeline_mode=pl.Buffered(1))
    return pl.pallas_call(
        functools.partial(_in_kernel_resident, seq_len=seq_len),
        grid=(t // ROW_GROUP,),
        in_specs=[pl.BlockSpec((ROW_GROUP, D_MODEL), lambda r: (r, 0)),
                  pl.BlockSpec((None, 1, D_MODEL), lambda r: (mod_row(r) * 6 + 0, 0, 0)),
                  pl.BlockSpec((None, 1, D_MODEL), lambda r: (mod_row(r) * 6 + 1, 0, 0)),
                  pl.BlockSpec((1, D_MODEL), lambda r: (0, 0)),
                  pl.BlockSpec(memory_space=pl.ANY),
                  pl.BlockSpec(conv_w.shape, lambda r: (0, 0)),
                  pl.BlockSpec(conv_b.shape, lambda r: (0, 0))],
        out_specs=[pl.BlockSpec(memory_space=pl.ANY), pl.BlockSpec(memory_space=pl.ANY)],
        out_shape=[jax.ShapeDtypeStruct((t, N_LOW_TILES * IN_TILE), BF16),
                   jax.ShapeDtypeStruct((t, N_F32_TILES * IN_TILE), F32)],
        scratch_shapes=[pltpu.VMEM((ROW_GROUP, D_MODEL), BF16),
                        pltpu.VMEM((ROW_GROUP + 2 * CONV_PAD, IN_TILE), F32),
                        pltpu.VMEM((IN_OUT_SLOTS, ROW_GROUP, IN_TILE), BF16),
                        pltpu.VMEM((IN_OUT_SLOTS, ROW_GROUP, IN_TILE), F32),
                        pltpu.VMEM(w_in_r.shape, BF16),
                        pltpu.SemaphoreType.DMA((2, IN_OUT_SLOTS)),
                        pltpu.SemaphoreType.DMA((N_LOW_TILES + N_F32_TILES,))],
        compiler_params=_params("arbitrary"),
        name="in_proj",
    )(x2d, mod48, mod48, norm_g, w_in_r, conv_w, conv_b)


def _in_kernel(x_ref, sh_ref, sc_ref, g_ref, w_ref, cw_ref, cb_ref, lo_ref, hi_ref, h_scr, u_scr, *, seq_len):
    j = pl.program_id(1)
    chunk = IN_CHUNK
    n_chunks = ROW_GROUP // chunk
    rows = lambda c: slice(c * chunk, (c + 1) * chunk)

    def pointwise(fn, o_ref, stage_h=False):
        for c in range(n_chunks):
            if stage_h:
                h_scr[rows(c), :] = _norm_mod(x_ref[rows(c), :], g_ref, sc_ref, sh_ref)
            o_ref[rows(c), :] = fn(_dot(h_scr[rows(c), :], w_ref[...])).astype(o_ref.dtype)

    def conv(o_ref):
        def conv_out(c):
            v = _silu(_dwconv3_rows(u_scr, c, chunk, slice(0, IN_TILE), cw_ref, cb_ref, seq_len))
            o_ref[rows(c), :] = v.astype(o_ref.dtype)
        for c in range(n_chunks):
            u_scr[_stage_rows(c, chunk), :] = _dot(h_scr[rows(c), :], w_ref[...])
            if c >= 1:
                conv_out(c - 1)
        conv_out(n_chunks - 1)

    @pl.when(j == 0)
    def _():
        _zero_conv_pads(u_scr)
        pointwise(_silu, lo_ref, stage_h=True)

    @pl.when(j == 1)
    def _():
        pointwise(_silu, lo_ref)

    @pl.when((j >= 2) & (j <= 5))
    def _():
        pointwise(_sigmoid, lo_ref)

    @pl.when(j == BC_TILE)
    def _():
        conv(lo_ref)

    @pl.when((j == N_LOW_TILES) | (j == N_LOW_TILES + 1))
    def _():
        conv(hi_ref)

    @pl.when(j >= N_LOW_TILES + MLA_TILE)
    def _():
        pointwise(lambda u: u, hi_ref)


IN_RESIDENT_MIN_GROUPS = 2


def _in_proj(x2d, mod48, mod_row, norm_g, w_in_r, conv_w, conv_b, seq_len):
    t = x2d.shape[0]
    if t // ROW_GROUP >= IN_RESIDENT_MIN_GROUPS:
        return _in_proj_resident(x2d, mod48, mod_row, norm_g, w_in_r, conv_w, conv_b, seq_len)
    n_tiles = N_LOW_TILES + N_F32_TILES
    conv_idx = lambda r, j: (0, jnp.where(j == BC_TILE, 2, jnp.clip(j - N_LOW_TILES, 0, 1)))
    return pl.pallas_call(
        functools.partial(_in_kernel, seq_len=seq_len),
        grid=(t // ROW_GROUP, n_tiles),
        in_specs=[pl.BlockSpec((ROW_GROUP, D_MODEL), lambda r, j: (r, 0)),
                  pl.BlockSpec((None, 1, D_MODEL), lambda r, j: (mod_row(r) * 6 + 0, 0, 0)),
                  pl.BlockSpec((None, 1, D_MODEL), lambda r, j: (mod_row(r) * 6 + 1, 0, 0)),
                  pl.BlockSpec((1, D_MODEL), lambda r, j: (0, 0)),
                  pl.BlockSpec((D_MODEL, IN_TILE), lambda r, j: (0, j)),
                  pl.BlockSpec((3, IN_TILE), conv_idx),
                  pl.BlockSpec((1, IN_TILE), conv_idx)],
        out_specs=[pl.BlockSpec((ROW_GROUP, IN_TILE), lambda r, j: (r, jnp.minimum(j, N_LOW_TILES - 1))),
                   pl.BlockSpec((ROW_GROUP, IN_TILE), lambda r, j: (r, jnp.maximum(j - N_LOW_TILES, 0)))],
        out_shape=[jax.ShapeDtypeStruct((t, N_LOW_TILES * IN_TILE), BF16),
                   jax.ShapeDtypeStruct((t, N_F32_TILES * IN_TILE), F32)],
        scratch_shapes=[pltpu.VMEM((ROW_GROUP, D_MODEL), BF16),
                        pltpu.VMEM((ROW_GROUP + 2 * CONV_PAD, IN_TILE), F32)],
        compiler_params=_params("arbitrary", "arbitrary"),
        name="in_proj",
    )(x2d, mod48, mod48, norm_g, w_in_r, conv_w, conv_b)


def _mla_kernel(*refs, latent, has_ctx, emit_cache, seq_len, tq, seqs):
    n_in = 7 + (2 if latent else 0) + (2 if has_ctx else 0)
    n_out = 3 if emit_cache else 1
    per_seq = {0, 1} | ({n_in - 2, n_in - 1} if has_ctx else set())
    for bi in range(seqs):
        view = lambda r: r.at[pl.ds(bi, 1)]
        ins = [view(r) if i in per_seq else r for i, r in enumerate(refs[:n_in])]
        outs = [view(r) for r in refs[n_in:n_in + n_out]]
        _mla_sequence(*ins, *outs, *refs[n_in + n_out:], latent=latent, has_ctx=has_ctx, emit_cache=emit_cache,
                      seq_len=seq_len, tq=tq)


def _mla_sequence(*refs, latent, has_ctx, emit_cache, seq_len, tq):
    refs = list(refs)
    pm_ref, px_ref, qg_ref, kvg_ref, wuq_ref, wkt_ref, wv_ref = refs[:7]
    del refs[:7]
    if latent:
        cos_ref, sin_ref = refs[:2]
        del refs[:2]
    if has_ctx:
        cckv_ref, ckr_ref = refs[:2]
        del refs[:2]
    o_ref = refs.pop(0)
    if emit_cache:
        ckv_ref, kr_ref = refs[:2]
        del refs[:2]
    k_scr, v_scr = refs[:2]
    del refs[:2]
    if has_ctx:
        kc_scr, vc_scr = refs
    t = pl.program_id(1)
    n_nope = N_HEADS * QK_NOPE

    def put_keys(k_dst, v_dst, rows, ckv_n, kr_bf):
        ckv_bf = ckv_n.astype(BF16)
        v_dst[rows, :] = _dot(ckv_bf, wv_ref[...]).astype(BF16)
        kn_t = _dot_nt(wkt_ref[...], ckv_bf).astype(BF16)
        width = kr_bf.shape[1]
        eye = jnp.where(lax.broadcasted_iota(jnp.int32, (128, width), 0)
                        == lax.broadcasted_iota(jnp.int32, (128, width), 1), 1.0, 0.0).astype(BF16)
        kr_t = _dot_nt(eye, kr_bf).astype(BF16)
        for h in range(N_HEADS):
            base = h * ATTN_HEAD_COLS
            k_dst[base:base + QK_NOPE, rows] = kn_t[h * QK_NOPE:(h + 1) * QK_NOPE, :]
            k_dst[base + QK_NOPE:base + ATTN_HEAD_COLS, rows] = kr_t

    @pl.when(t == 0)
    def _():
        step = min(seq_len, TOKEN_TILE)
        for r0 in range(0, seq_len, step):
            rows = slice(r0, r0 + step)
            ckv_n = _rmsnorm(pm_ref[0, rows, Q_LORA:], kvg_ref[...])
            kr = px_ref[0, rows, 0:128]
            if latent:
                kr = kr * cos_ref[rows, :] + px_ref[0, rows, 128:256] * sin_ref[rows, :]
            if emit_cache:
                ckv_ref[0, rows, :] = ckv_n
                kr_ref[0, rows, :] = kr[:, :QK_ROPE]
            put_keys(k_scr, v_scr, rows, ckv_n, kr.astype(BF16))
        if has_ctx:
            past = cckv_ref.shape[1]
            put_keys(kc_scr, vc_scr, slice(0, past), cckv_ref[0], ckr_ref[0].astype(BF16))

    qrows = pl.ds(pl.multiple_of(t * tq, tq), tq)
    scale = LOG2_E / math.sqrt(QK_NOPE + QK_ROPE)
    cqn = _rmsnorm(pm_ref[0, qrows, 0:Q_LORA], qg_ref[...]).astype(BF16)
    q = _dot(cqn, wuq_ref[...])
    q_rope = q[:, n_nope:2 * n_nope]
    if latent:
        q_rope = (q_rope * jnp.concatenate([cos_ref[qrows, :]] * N_HEADS, axis=1)
                  + q[:, 2 * n_nope:3 * n_nope] * jnp.concatenate([sin_ref[qrows, :]] * N_HEADS, axis=1))
    segs = ([(kc_scr, vc_scr)] if has_ctx else []) + [(k_scr, v_scr)]
    for h in range(N_HEADS):
        head = slice(h * QK_NOPE, (h + 1) * QK_NOPE)
        qk = slice(h * ATTN_HEAD_COLS, (h + 1) * ATTN_HEAD_COLS)
        qh = (jnp.concatenate([q[:, head], q_rope[:, head]], axis=1) * scale).astype(BF16)
        s = [_dot(qh, k[qk, :]) for k, _ in segs]
        m = functools.reduce(jnp.maximum, [jnp.max(si, axis=-1, keepdims=True) for si in s])
        p = [jnp.exp2(si - m) for si in s]
        l = functools.reduce(jnp.add, [jnp.sum(pi, axis=-1, keepdims=True) for pi in p])
        o = functools.reduce(jnp.add, [_dot(pi.astype(BF16), v[:, head]) for pi, (_, v) in zip(p, segs)])
        o_ref[0, :, head] = (o / l).astype(BF16)


def _mla(proj_hi, q_norm_g, kv_norm_g, w_uq_r, w_uk_t, w_uv, rope_tables, ctx, emit_cache):
    b, l, _ = proj_hi.shape
    tq = min(l, ATTN_Q_TILE)
    latent = rope_tables is not None
    has_ctx = ctx is not None
    once = dict(pipeline_mode=pl.Buffered(1))
    per_batch = once if l // tq > 1 else {}
    seqs = MLA_SEQS_PER_STEP if (l == tq and b % MLA_SEQS_PER_STEP == 0) else 1
    const2 = lambda i, t: (0, 0)
    in_specs = [pl.BlockSpec((seqs, l, IN_TILE), lambda i, t: (i, 0, MLA_TILE), **per_batch),
                pl.BlockSpec((seqs, l, 256), lambda i, t: (i, 0, MISC_TILE * IN_TILE // 256), **per_batch),
                pl.BlockSpec((1, Q_LORA), const2),
                pl.BlockSpec((1, KV_LORA), const2),
                pl.BlockSpec(w_uq_r.shape, const2, **once),
                pl.BlockSpec(w_uk_t.shape, const2, **once),
                pl.BlockSpec(w_uv.shape, const2, **once)]
    args = [proj_hi, proj_hi, q_norm_g, kv_norm_g, w_uq_r, w_uk_t, w_uv]
    scratch = [pltpu.VMEM((N_HEADS * ATTN_HEAD_COLS, l), BF16), pltpu.VMEM((l, N_HEADS * V_HEAD), BF16)]
    if latent:
        in_specs += [pl.BlockSpec((l, 128), const2, **once)] * 2
        args += list(rope_tables)
    if has_ctx:
        past = ctx[0].shape[1]
        in_specs += [pl.BlockSpec((seqs, past, KV_LORA), lambda i, t: (i, 0, 0), **per_batch),
                     pl.BlockSpec((seqs, past, QK_ROPE), lambda i, t: (i, 0, 0), **per_batch)]
        args += list(ctx)
        scratch += [pltpu.VMEM((N_HEADS * ATTN_HEAD_COLS, past), BF16), pltpu.VMEM((past, N_HEADS * V_HEAD), BF16)]
    out_specs = [pl.BlockSpec((seqs, tq, N_HEADS * V_HEAD), lambda i, t: (i, t, 0))]
    out_shape = [jax.ShapeDtypeStruct((b, l, N_HEADS * V_HEAD), BF16)]
    if emit_cache:
        out_specs += [pl.BlockSpec((seqs, l, KV_LORA), lambda i, t: (i, 0, 0)),
                      pl.BlockSpec((seqs, l, QK_ROPE), lambda i, t: (i, 0, 0))]
        out_shape += [jax.ShapeDtypeStruct((b, l, KV_LORA), F32), jax.ShapeDtypeStruct((b, l, QK_ROPE), F32)]
    return pl.pallas_call(
        functools.partial(_mla_kernel, latent=latent, has_ctx=has_ctx, emit_cache=emit_cache, seq_len=l, tq=tq,
                          seqs=seqs),
        grid=(b // seqs, l // tq),
        in_specs=in_specs,
        out_specs=out_specs,
        out_shape=out_shape,
        scratch_shapes=scratch,
        compiler_params=_params("arbitrary", "arbitrary", vmem_limit=MLA_VMEM_LIMIT),
        name="mla_attention",
    )(*args)


def _split3(x):
    hi = x.astype(BF16)
    r = x - hi.astype(F32)
    mid = r.astype(BF16)
    lo = (r - mid.astype(F32)).astype(BF16)
    return hi, mid, lo


def _exact_dot(parts, sel):
    return functools.reduce(jnp.add, [_dot(p, sel) for p in parts])


def _exact_dot_rows(sel, parts):
    return functools.reduce(jnp.add, [_dot(sel, p) for p in parts])


HEADS_PER_GROUP = SSD_HEADS // SSD_GROUPS
GROUP_COLS = HEADS_PER_GROUP * SSD_HEADDIM


def _ssd_kernel(*refs, nc, cps, has_h0, seqs):
    n_in = 8 if has_h0 else 7
    per_seq = {0, 1, 2, 3, 7} if has_h0 else {0, 1, 2, 3}
    for bi in range(seqs):
        view = lambda r: r.at[pl.ds(bi, 1)]
        ins = [view(r) if i in per_seq else r for i, r in enumerate(refs[:n_in])]
        outs = [view(r) for r in refs[n_in:n_in + 2]]
        _ssd_sequence(*ins, *outs, *refs[n_in + 2:], nc=nc, cps=cps, has_h0=has_h0)


def _ssd_sequence(*refs, nc, cps, has_h0):
    if has_h0:
        (xs_ref, zs_ref, bc_ref, dt_ref, dtb_ref, a_ref, dx_ref, h0_ref, y_ref, ht_ref,
         yl_scr, acum_scr, src_scr, tot_scr, sb_scr, h_scr, esel_scr) = refs
    else:
        (xs_ref, zs_ref, bc_ref, dt_ref, dtb_ref, a_ref, dx_ref, y_ref, ht_ref,
         yl_scr, acum_scr, src_scr, tot_scr, sb_scr, h_scr, esel_scr) = refs
    s = pl.program_id(1)
    q = SSD_CHUNK
    n_bc = SSD_GROUPS * SSD_STATE
    lane = lax.broadcasted_iota(jnp.int32, (q, 128), 1)
    low_half = lane < SSD_HEADDIM
    ii = lax.broadcasted_iota(jnp.int32, (q, q), 0)
    jj = lax.broadcasted_iota(jnp.int32, (q, q), 1)
    lower, upper = ii >= jj, ii <= jj

    def lane_bcast(parts, d):
        return _exact_dot(parts, esel_scr[d])

    def stacked_states(d):
        return [jnp.concatenate([h_scr[d, 2 * i], h_scr[d, 2 * i + 1]], axis=0).astype(BF16)
                for i in range(SSD_GROUPS // 2)]

    def group_c(bc, g):
        i, r = divmod(g, 2)
        cpair = bc[:, n_bc + i * 128:n_bc + (i + 1) * 128]
        return jnp.where(low_half if r == 0 else ~low_half, cpair, 0.0).astype(BF16)

    @pl.when(s == 0)
    def _():
        k = lax.broadcasted_iota(jnp.int32, (128, SSD_INNER), 0)
        head = lax.broadcasted_iota(jnp.int32, (128, SSD_INNER), 1) // SSD_HEADDIM
        for d in range(2):
            esel_scr[d] = jnp.where(k == d * SSD_HEADS + head, 1.0, 0.0).astype(BF16)
        if has_h0:
            for d in range(2):
                for g in range(SSD_GROUPS):
                    hpn = h0_ref[0, d, g * HEADS_PER_GROUP:(g + 1) * HEADS_PER_GROUP].reshape(GROUP_COLS, SSD_STATE)
                    h_scr[d, g] = hpn.T
        else:
            h_scr[...] = jnp.zeros(h_scr.shape, F32)
        tri_f = jnp.where(lower, 1.0, 0.0).astype(BF16)
        tri_b = jnp.where(upper, 1.0, 0.0).astype(BF16)
        fwd_col = lane < SSD_HEADS
        for c in range(nc):
            crow = slice(c * q, (c + 1) * q)
            dt = _softplus(dt_ref[0, crow, :] + dtb_ref[...])
            parts = _split3(dt * a_ref[...])
            acum = jnp.where(fwd_col, _exact_dot_rows(tri_f, parts),
                             _exact_dot_rows(tri_b, parts))
            acum = acum * LOG2_E
            acum_scr[crow, :] = acum
            tot = jnp.where(fwd_col[0:1], acum[q - 1:q, :], acum[0:1, :])
            tot_scr[c] = jnp.broadcast_to(tot, (8, 128))
            src_scr[c] = (acum - jnp.log2(dt)).T

    def first_sweep(c, blk):
        rows = pl.ds(pl.multiple_of(c * q, q), q)
        x = xs_ref[0, blk, :]
        bc = bc_ref[0, blk, :].astype(F32)
        acum = acum_scr[rows, :]
        src_t = src_scr[c]
        tot8 = tot_scr[c]
        e = jnp.exp2(acum)
        e_hi = e.astype(BF16)
        eb_f = lane_bcast([e_hi, (e - e_hi.astype(F32)).astype(BF16)], 0)
        cd_f = jnp.exp2(lane_bcast(_split3(tot8), 0))[0:1]
        b_t = [bc[:, i * 128:(i + 1) * 128].T for i in range(SSD_GROUPS // 2)]
        h_in = stacked_states(0)
        for g in range(SSD_GROUPS):
            i, r = divmod(g, 2)
            cm = group_c(bc, g)
            scores = _dot_nt(cm, bc[:, i * 128:(i + 1) * 128].astype(BF16))
            bg_t = b_t[i][r * SSD_STATE:(r + 1) * SSD_STATE, :]
            gcols = slice(g * GROUP_COLS, (g + 1) * GROUP_COLS)
            y_off = _dot(cm, h_in[i]) * eb_f[:, gcols]
            for t in range(HEADS_PER_GROUP // 2):
                pc = slice((2 * g + t) * 128, (2 * g + t + 1) * 128)
                xp = x[:, pc]
                x2 = jnp.concatenate([jnp.where(low_half, xp, 0.0), jnp.where(low_half, 0.0, xp)], axis=0).astype(BF16)
                m, sf, sb = [], [], []
                for u in range(2):
                    kf = g * HEADS_PER_GROUP + 2 * t + u
                    kb = SSD_HEADS + kf
                    af_col = jnp.broadcast_to(acum[:, kf:kf + 1], (q, q))
                    ab_col = jnp.broadcast_to(acum[:, kb:kb + 1], (q, q))
                    sf_row, sb_row = src_t[kf:kf + 1, :], src_t[kb:kb + 1, :]
                    decay = (jnp.exp2(jnp.where(lower, af_col - sf_row, NEG_BIG))
                             + jnp.exp2(jnp.where(upper, ab_col - sb_row, NEG_BIG)))
                    m.append((scores * decay).astype(BF16))
                    wf = jnp.exp2(tot8[0:1, kf:kf + 1] - sf_row)
                    wb = jnp.exp2(tot8[0:1, kb:kb + 1] - sb_row)
                    sf.append((bg_t * wf).astype(BF16))
                    sb.append((bg_t * wb).astype(BF16))
                y_pair = _dot(jnp.concatenate(m, axis=1), x2)
                tc = slice(t * 128, (t + 1) * 128)
                yl_scr[rows, pc] = y_pair + y_off[:, tc] + dx_ref[:, pc] * xp
                h_scr[0, g, :, tc] = h_scr[0, g, :, tc] * cd_f[:, pc] + _dot(jnp.concatenate(sf, axis=1), x2)
                sb_scr[c, g, :, tc] = _dot(jnp.concatenate(sb, axis=1), x2)

    def second_sweep(c, blk):
        rows = pl.ds(pl.multiple_of(c * q, q), q)
        bc = bc_ref[0, blk, :].astype(F32)
        e = jnp.exp2(acum_scr[rows, :])
        e_hi = e.astype(BF16)
        eb_b = lane_bcast([e_hi, (e - e_hi.astype(F32)).astype(BF16)], 1)
        cd_b = jnp.exp2(lane_bcast(_split3(tot_scr[c]), 1))[0:1]
        h_in = stacked_states(1)
        for g in range(SSD_GROUPS):
            gcols = slice(g * GROUP_COLS, (g + 1) * GROUP_COLS)
            y_off = _dot(group_c(bc, g), h_in[g // 2]) * eb_b[:, gcols]
            y_ref[0, blk, gcols] = ((yl_scr[rows, gcols] + y_off) * zs_ref[0, blk, gcols].astype(F32)).astype(y_ref.dtype)
            h_scr[1, g] = h_scr[1, g] * cd_b[:, gcols] + sb_scr[c, g]

    n_steps = nc // cps
    one_step = n_steps == 1

    @pl.when(s < n_steps)
    def _():
        for ci in range(cps):
            first_sweep(s * cps + ci, slice(ci * q, (ci + 1) * q))

    @pl.when(jnp.logical_or(one_step, s >= n_steps))
    def _():
        blk_id = s * 0 if one_step else 2 * n_steps - 1 - s
        for ci in reversed(range(cps)):
            second_sweep(blk_id * cps + ci, slice(ci * q, (ci + 1) * q))

    @pl.when(jnp.logical_or(one_step, s == 2 * n_steps - 1))
    def _():
        for d in range(2):
            for g in range(SSD_GROUPS):
                ht_ref[0, d, g * HEADS_PER_GROUP:(g + 1) * HEADS_PER_GROUP] = h_scr[d, g].T.reshape(
                    HEADS_PER_GROUP, SSD_HEADDIM, SSD_STATE)


def _ssd(proj_lo, proj_hi, h0, dt_bias128, a128, d_exp):
    b, l, _ = proj_hi.shape
    q = SSD_CHUNK
    nc = l // q
    cps = min(nc, SSD_CHUNKS_PER_STEP)
    n_steps = nc // cps
    n_grid = 1 if n_steps == 1 else 2 * n_steps
    rows = cps * q
    early = lambda s: jnp.minimum(s, n_steps - 1)
    both = lambda s: jnp.where(s < n_steps, s, 2 * n_steps - 1 - s)
    late = lambda s: jnp.where(s < n_steps, n_steps - 1, 2 * n_steps - 1 - s)
    seqs = SSD_SEQS_PER_STEP if (n_steps == 1 and b % SSD_SEQS_PER_STEP == 0) else 1
    st_shape = (seqs, 2, SSD_HEADS, SSD_HEADDIM, SSD_STATE)
    st_spec = pl.BlockSpec(st_shape, lambda i, s: (i, 0, 0, 0, 0))
    has_h0 = h0 is not None
    return pl.pallas_call(
        functools.partial(_ssd_kernel, nc=nc, cps=cps, has_h0=has_h0, seqs=seqs),
        grid=(b // seqs, n_grid),
        in_specs=[pl.BlockSpec((seqs, rows, SSD_INNER), lambda i, s: (i, early(s), XS_BLK)),
                  pl.BlockSpec((seqs, rows, SSD_INNER), lambda i, s: (i, late(s), Z_BLK)),
                  pl.BlockSpec((seqs, rows, IN_TILE), lambda i, s: (i, both(s), BC_TILE)),
                  pl.BlockSpec((seqs, l, 128), lambda i, s: (i, 0, DT_BLK)),
                  pl.BlockSpec((1, 128), lambda i, s: (0, 0)),
                  pl.BlockSpec((1, 128), lambda i, s: (0, 0)),
                  pl.BlockSpec((1, SSD_INNER), lambda i, s: (0, 0))] + ([st_spec] if has_h0 else []),
        out_specs=[pl.BlockSpec((seqs, rows, SSD_INNER), lambda i, s: (i, late(s), 0)),
                   pl.BlockSpec(st_shape, lambda i, s: (i, 0, 0, 0, 0))],
        out_shape=[jax.ShapeDtypeStruct((b, l, SSD_INNER), BF16),
                   jax.ShapeDtypeStruct((b,) + st_shape[1:], F32)],
        scratch_shapes=[pltpu.VMEM((l, SSD_INNER), F32),
                        pltpu.VMEM((l, 128), F32),
                        pltpu.VMEM((nc, 128, q), F32),
                        pltpu.VMEM((nc, 8, 128), F32),
                        pltpu.VMEM((nc, SSD_GROUPS, SSD_STATE, GROUP_COLS), F32),
                        pltpu.VMEM((2, SSD_GROUPS, SSD_STATE, GROUP_COLS), F32),
                        pltpu.VMEM((2, 128, SSD_INNER), BF16)],
        compiler_params=_params("arbitrary", "arbitrary"),
        name="ssd_scan",
    )(proj_hi, proj_lo, proj_lo, proj_hi, dt_bias128, a128, d_exp, *([h0] if has_h0 else []))


def _merge_kernel(attn_ref, yz_ref, gm_ref, gs_ref, x_ref, g1_ref, ng_ref, womla_ref, wossd_ref, wout_ref, o_ref, w_scr):
    @pl.when(pl.program_id(0) == 0)
    def _():
        w_scr[0] = womla_ref[...].astype(BF16)
        w_scr[1] = wossd_ref[...].astype(BF16)
        w_scr[2] = wout_ref[...].astype(BF16)

    o_mla = _dot(attn_ref[...], w_scr[0])
    o_ssd = _dot(_rmsnorm(yz_ref[...].astype(F32), ng_ref[...]).astype(BF16), w_scr[1])
    merged = gm_ref[...].astype(F32) * o_mla + gs_ref[...].astype(F32) * o_ssd
    o_ref[...] = x_ref[...] + g1_ref[...] * _dot(merged.astype(BF16), w_scr[2])


def _merge(attn2d, yz2d, proj, x2d, mod48, mod_row, ssd_norm_g, w_o_mla, w_o_ssd, w_out):
    t = x2d.shape[0]
    tm = TOKEN_TILE
    row = lambda i: (i, 0)
    const = lambda i: (0, 0)
    wspec = pl.BlockSpec((D_MODEL, D_MODEL), const, pipeline_mode=pl.Buffered(1))
    return pl.pallas_call(
        _merge_kernel,
        grid=(t // tm,),
        in_specs=[pl.BlockSpec((tm, D_MODEL), row),
                  pl.BlockSpec((tm, D_MODEL), row),
                  pl.BlockSpec((tm, D_MODEL), lambda i: (i, GM_BLK)),
                  pl.BlockSpec((tm, D_MODEL), lambda i: (i, GS_BLK)),
                  pl.BlockSpec((tm, D_MODEL), row),
                  pl.BlockSpec((None, 1, D_MODEL), lambda i: (mod_row(i * tm // ROW_GROUP) * 6 + 2, 0, 0)),
                  pl.BlockSpec((1, D_MODEL), const),
                  wspec, wspec, wspec],
        out_specs=pl.BlockSpec((tm, D_MODEL), row),
        out_shape=jax.ShapeDtypeStruct((t, D_MODEL), F32),
        scratch_shapes=[pltpu.VMEM((3, D_MODEL, D_MODEL), BF16)],
        compiler_params=_params("arbitrary"),
        name="merge_out",
    )(attn2d, yz2d, proj, proj, x2d, mod48, ssd_norm_g, w_o_mla, w_o_ssd, w_out)


def _ffn_kernel(x_ref, sh_ref, sc_ref, g2_ref, ng_ref, wg_ref, wv_ref, cwg_ref, cwv_ref, cbg_ref, cbv_ref, wd_ref,
                fg_ref, o_ref, h_scr, wup_scr, wd_scr, u_scr, *, seq_len):
    j = pl.program_id(1)
    chunk = FFN_CHUNK
    n_chunks = ROW_GROUP // chunk
    rows = lambda c: slice(c * chunk, (c + 1) * chunk)

    wd_scr[...] = wd_ref[...].astype(BF16)

    def first_up(h_rows):
        n_k = 4
        acc = None
        for k in range(n_k):
            ks = slice(k * (D_MODEL // n_k), (k + 1) * (D_MODEL // n_k))
            wup_scr[ks, 0:FFN_TILE] = wg_ref[ks, :].astype(BF16)
            wup_scr[ks, FFN_TILE:2 * FFN_TILE] = wv_ref[ks, :].astype(BF16)
            part = _dot(h_scr[h_rows, ks], wup_scr[ks, :])
            acc = part if acc is None else acc + part
        return acc

    def gated(c):
        ug = _dwconv3_rows(u_scr, c, chunk, slice(0, FFN_TILE), cwg_ref, cbg_ref, seq_len)
        uv = _dwconv3_rows(u_scr, c, chunk, slice(FFN_TILE, 2 * FFN_TILE), cwv_ref, cbv_ref, seq_len)
        return (_silu(ug) * uv).astype(BF16)

    def pipeline(first, last):
        act = {}
        lag = FFN_DOWN_LAG
        for c in range(n_chunks + lag):
            if c < n_chunks:
                if first:
                    h_scr[rows(c), :] = _norm_mod(x_ref[rows(c), :], ng_ref, sc_ref, sh_ref)
                up = first_up(rows(c)) if c == 0 else _dot(h_scr[rows(c), :], wup_scr[...])
                u_scr[_stage_rows(c, chunk), :] = up
            if c >= lag:
                r = rows(c - lag)
                acc = _dot(act.pop(c - lag), wd_scr[...])
                if not first:
                    acc = o_ref[r, :] + acc
                if last:
                    acc = _rmsnorm(x_ref[r, :] + g2_ref[...] * acc, fg_ref[...])
                o_ref[r, :] = acc
            if 1 <= c <= n_chunks:
                act[c - 1] = gated(c - 1)

    last_j = pl.num_programs(1) - 1

    @pl.when(j == 0)
    def _():
        _zero_conv_pads(u_scr)
        pipeline(True, False)

    @pl.when((j > 0) & (j < last_j))
    def _():
        pipeline(False, False)

    @pl.when(j == last_j)
    def _():
        pipeline(False, True)


def _ffn(x2d, mod48, mod_row, norm_g, w_up, conv_w, conv_b, w_down, final_g, seq_len):
    t = x2d.shape[0]
    nj = D_FF // FFN_TILE
    gate = lambda r, j: (0, j)
    val = lambda r, j: (0, nj + j)
    const = lambda r, j: (0, 0)
    mod = lambda k: pl.BlockSpec((None, 1, D_MODEL), lambda r, j: (mod_row(r) * 6 + k, 0, 0))
    return pl.pallas_call(
        functools.partial(_ffn_kernel, seq_len=seq_len),
        grid=(t // ROW_GROUP, nj),
        in_specs=[pl.BlockSpec((ROW_GROUP, D_MODEL), lambda r, j: (r, 0)),
                  mod(3), mod(4), mod(5),
                  pl.BlockSpec((1, D_MODEL), const),
                  pl.BlockSpec((D_MODEL, FFN_TILE), gate),
                  pl.BlockSpec((D_MODEL, FFN_TILE), val),
                  pl.BlockSpec((3, FFN_TILE), gate),
                  pl.BlockSpec((3, FFN_TILE), val),
                  pl.BlockSpec((1, FFN_TILE), gate),
                  pl.BlockSpec((1, FFN_TILE), val),
                  pl.BlockSpec((FFN_TILE, D_MODEL), lambda r, j: (j, 0)),
                  pl.BlockSpec((1, D_MODEL), const)],
        out_specs=pl.BlockSpec((ROW_GROUP, D_MODEL), lambda r, j: (r, 0)),
        out_shape=jax.ShapeDtypeStruct((t, D_MODEL), F32),
        scratch_shapes=[pltpu.VMEM((ROW_GROUP, D_MODEL), BF16),
                        pltpu.VMEM((D_MODEL, 2 * FFN_TILE), BF16),
                        pltpu.VMEM((FFN_TILE, D_MODEL), BF16),
                        pltpu.VMEM((ROW_GROUP + 2 * CONV_PAD, 2 * FFN_TILE), F32)],
        compiler_params=_params("arbitrary", "arbitrary"),
        name="conv_ffn",
    )(x2d, mod48, mod48, mod48, norm_g, w_up, w_up, conv_w, conv_w, conv_b, conv_b, w_down, final_g)


def _rope_tables(seq_len):
    t = np.arange(seq_len)
    row = (t // GRID_W).astype(np.float32)
    col = (t % GRID_W).astype(np.float32)
    n = QK_ROPE // 4
    inv = (np.float32(ROPE_BASE) ** (-np.arange(n, dtype=np.float32) / np.float32(n))).astype(np.float32)
    ar, ac = row[:, None] * inv, col[:, None] * inv
    cos64 = np.concatenate([np.cos(ar), np.cos(ar), np.cos(ac), np.cos(ac)], axis=1)
    sin64 = np.concatenate([-np.sin(ar), np.sin(ar), -np.sin(ac), np.sin(ac)], axis=1)
    zeros = np.zeros_like(cos64)
    return (jnp.asarray(np.concatenate([cos64, zeros], axis=1), F32),
            jnp.asarray(np.concatenate([sin64, zeros], axis=1), F32))


def _swap_rope_halves(w):
    lead = w.shape[:-1]
    return w.reshape(lead + (2, 2, QK_ROPE // 4))[..., ::-1, :].reshape(lead + (QK_ROPE,))


def _trunk_pass(x, mod48, mod_row, wts, ctx, latent):
    b, l, _ = x.shape
    x2d = x.reshape(b * l, D_MODEL)
    proj_lo, proj_hi = _in_proj(x2d, mod48, mod_row, wts["norm_attn_g"], wts["w_in_r"], wts["ssd_conv_w"],
                                wts["ssd_conv_b"], l)
    rope = _rope_tables(l) if latent else None
    w_uq_r = wts["w_uq_lat"] if latent else wts["w_uq_ctx"]
    shape3 = lambda a, n: a.reshape(b, n, a.shape[-1])
    h0 = None
    mla_ctx = None
    if ctx is not None:
        cache_ckv, cache_krope, h0 = ctx
        mla_ctx = (cache_ckv, cache_krope)
    emit_cache = ctx is None
    mla_out = _mla(shape3(proj_hi, l), wts["q_norm_g"], wts["kv_norm_g"], w_uq_r, wts["w_uk_t"], wts["w_uv"],
                   rope, mla_ctx, emit_cache)
    attn, ckv_n, kr3 = mla_out if emit_cache else (mla_out[0], None, None)
    yz, h_t = _ssd(shape3(proj_lo, l), shape3(proj_hi, l), h0, wts["dt_bias128"], wts["a128"], wts["d_exp"])
    x1 = _merge(attn.reshape(b * l, -1), yz.reshape(b * l, -1), proj_lo, x2d, mod48, mod_row, wts["ssd_norm_g"],
                wts["w_o_mla"], wts["w_o_ssd"], wts["w_out"])
    y = _ffn(x1, mod48, mod_row, wts["norm_ffn_g"], wts["w_up"], wts["ffn_conv_w"], wts["ffn_conv_b"], wts["w_down"],
             wts["final_norm_g"], l)
    return y.reshape(b, l, D_MODEL), ckv_n, kr3, h_t


def kernel(x_prompt, x_sample, c, cache_ckv, cache_krope, state_ssd, c_ctx, w_ada, b_ada, norm_attn_g, w_in, q_norm_g,
           kv_norm_g, w_uq, w_ukv, w_o_mla, ssd_conv_w, ssd_conv_b, ssd_dt_bias, ssd_A_log, ssd_D, ssd_norm_g, w_o_ssd,
           w_out, norm_ffn_g, w_up, ffn_conv_w, ffn_conv_b, w_down, final_norm_g):
    depth = w_in.shape[0]
    assert depth == 1, "single trunk layer"
    dec_b = x_sample.shape[0]
    assert x_sample.shape[1] == ROW_GROUP and ROW_GROUP % x_prompt.shape[1] == 0
    lyr = 0

    cvec = jnp.zeros((8, D_MODEL), F32).at[0].set(c_ctx).at[1:1 + dec_b].set(c)
    mod48 = _ada(cvec, w_ada[lyr], b_ada[lyr]).reshape(8 * 6, 1, D_MODEL)

    w_in_r = _regroup_w_in(w_in[lyr].T)
    wq = w_uq[lyr].reshape(Q_LORA, N_HEADS, QK_NOPE + QK_ROPE)
    wq_nope = wq[:, :, :QK_NOPE].reshape(Q_LORA, -1)
    wq_rope = wq[:, :, QK_NOPE:]
    pad_rope = lambda w: jnp.pad(w, ((0, 0), (0, 0), (0, 128 - QK_ROPE))).reshape(Q_LORA, -1)
    w_uq_ctx = jnp.concatenate([wq_nope, pad_rope(wq_rope)], axis=1).astype(BF16)
    w_uq_lat = jnp.concatenate([wq_nope, pad_rope(wq_rope), pad_rope(_swap_rope_halves(wq_rope))], axis=1).astype(BF16)
    wkv = w_ukv[lyr].reshape(KV_LORA, N_HEADS, QK_NOPE + V_HEAD)
    w_uk_t = wkv[:, :, :QK_NOPE].reshape(KV_LORA, -1).T.astype(BF16)
    w_uv = wkv[:, :, QK_NOPE:].reshape(KV_LORA, -1).astype(BF16)
    pad128 = lambda a: jnp.pad(a.reshape(1, -1), ((0, 0), (0, 128 - a.size)))
    wts = {
        "norm_attn_g": norm_attn_g[lyr].reshape(1, -1), "w_in_r": w_in_r,
        "ssd_conv_w": ssd_conv_w[lyr], "ssd_conv_b": ssd_conv_b[lyr].reshape(1, -1),
        "q_norm_g": q_norm_g[lyr].reshape(1, -1), "kv_norm_g": kv_norm_g[lyr].reshape(1, -1),
        "w_uq_ctx": w_uq_ctx, "w_uq_lat": w_uq_lat, "w_uk_t": w_uk_t, "w_uv": w_uv,
        "dt_bias128": pad128(ssd_dt_bias[lyr]), "a128": pad128(-jnp.exp(ssd_A_log[lyr])),
        "d_exp": jnp.repeat(ssd_D[lyr], SSD_HEADDIM).reshape(1, -1),
        "ssd_norm_g": ssd_norm_g[lyr].reshape(1, -1),
        "w_o_mla": w_o_mla[lyr], "w_o_ssd": w_o_ssd[lyr], "w_out": w_out[lyr],
        "norm_ffn_g": norm_ffn_g[lyr].reshape(1, -1), "w_up": w_up[lyr],
        "ffn_conv_w": ffn_conv_w[lyr], "ffn_conv_b": ffn_conv_b[lyr].reshape(1, -1),
        "w_down": w_down[lyr], "final_norm_g": final_norm_g.reshape(1, -1),
    }

    y_p, ckv_p, kr_p, st_p = _trunk_pass(x_prompt, mod48, lambda r: 0, wts, None, False)
    ctx = (cache_ckv[:, lyr], cache_krope[:, lyr], state_ssd[:, lyr])
    y_s, _, _, _ = _trunk_pass(x_sample, mod48, lambda r: 1 + r, wts, ctx, True)
    return y_p, y_s, ckv_p[:, None], kr_p[:, None], st_p[:, None]
```

```python
import functools
import math

import jax
import jax.numpy as jnp
import numpy as np
from jax import lax
from jax.experimental import pallas as pl
from jax.experimental.pallas import tpu as pltpu

F32 = jnp.float32
BF16 = jnp.bfloat16

D_MODEL = 1024
GRID_W = 64
N_HEADS = 8
QK_NOPE = 128
QK_ROPE = 64
V_HEAD = 128
Q_LORA = 256
KV_LORA = 256
ROPE_BASE = 10000.0
SSD_HEADS = 16
SSD_HEADDIM = 64
SSD_INNER = SSD_HEADS * SSD_HEADDIM
SSD_GROUPS = 4
SSD_STATE = 64
SSD_CHUNK = 128
D_FF = 2816
EPS = 1e-6

ROW_GROUP = 2048
IN_TILE = 512
SSD_CHUNKS_PER_STEP = 8
SSD_SEQS_PER_STEP = 2
MLA_SEQS_PER_STEP = 2
FFN_TILE = 256
TOKEN_TILE = 512
ATTN_Q_TILE = 512
IN_CHUNK = 256
FFN_CHUNK = 512
FFN_DOWN_LAG = 2
CONV_PAD = 8
VMEM_LIMIT = 56 * 1024 * 1024
MLA_VMEM_LIMIT = 62 * 1024 * 1024
NEG_BIG = -1e30
LOG2_E = 1.4426950408889634


def _sigmoid(x):
    return 1.0 / (1.0 + jnp.exp(-x))


def _silu(x):
    return x * _sigmoid(x)


def _softplus(x):
    e = jnp.exp(-jnp.abs(x))
    u = 1.0 + e
    log1p_e = jnp.where(u == 1.0, e, e * jnp.log(u) / jnp.where(u == 1.0, 1.0, u - 1.0))
    return jnp.maximum(x, 0.0) + log1p_e


def _rmsnorm(x, g):
    return x * lax.rsqrt(jnp.mean(x * x, axis=-1, keepdims=True) + EPS) * g


def _dot(a, b):
    return jnp.dot(a, b, preferred_element_type=F32)


def _dot_nt(a, b):
    return lax.dot_general(a, b, (((1,), (1,)), ((), ())), preferred_element_type=F32)


def _params(*sem, vmem_limit=VMEM_LIMIT):
    return pltpu.CompilerParams(dimension_semantics=sem, vmem_limit_bytes=vmem_limit)


def _norm_mod(x, g_ref, sc_ref, sh_ref):
    return (_rmsnorm(x, g_ref[...]) * (1.0 + sc_ref[...]) + sh_ref[...]).astype(BF16)


def _zero_conv_pads(u_scr):
    zeros = jnp.zeros((CONV_PAD, u_scr.shape[1]), F32)
    u_scr[0:CONV_PAD, :] = zeros
    u_scr[CONV_PAD + ROW_GROUP:2 * CONV_PAD + ROW_GROUP, :] = zeros


def _stage_rows(c, chunk):
    return slice(CONV_PAD + c * chunk, CONV_PAD + (c + 1) * chunk)


def _dwconv3_rows(u_scr, c, chunk, cols, w_ref, b_ref, seq_len):
    r0 = c * chunk
    base = CONV_PAD + r0
    width = cols.stop - cols.start
    prev = u_scr[base - 1:base - 1 + chunk, cols]
    cur = u_scr[base:base + chunk, cols]
    nxt = u_scr[base + 1:base + 1 + chunk, cols]
    pos = (lax.broadcasted_iota(jnp.int32, (chunk, width), 0) + r0) & (seq_len - 1)
    if r0 % seq_len == 0 or chunk > seq_len:
        prev = jnp.where(pos == 0, 0.0, prev)
    if (r0 + chunk) % seq_len == 0 or chunk > seq_len:
        nxt = jnp.where(pos == seq_len - 1, 0.0, nxt)
    return prev * w_ref[0:1, :] + cur * w_ref[1:2, :] + nxt * w_ref[2:3, :] + b_ref[...]


def _ada_kernel(c_ref, w_ref, b_ref, o_ref):
    a = _silu(c_ref[...]).astype(BF16)
    o_ref[...] = _dot(a, w_ref[...].astype(BF16)) + b_ref[...]


def _ada(cvec, w_ada, b_ada):
    tn = 1536
    return pl.pallas_call(
        _ada_kernel,
        grid=(6 * D_MODEL // tn,),
        in_specs=[pl.BlockSpec((8, D_MODEL), lambda j: (0, 0)),
                  pl.BlockSpec((D_MODEL, tn), lambda j: (0, j)),
                  pl.BlockSpec((1, tn), lambda j: (0, j))],
        out_specs=pl.BlockSpec((8, tn), lambda j: (0, j)),
        out_shape=jax.ShapeDtypeStruct((8, 6 * D_MODEL), F32),
        compiler_params=_params("arbitrary"),
        name="ada_mod",
    )(cvec, w_ada, b_ada.reshape(1, -1))


IN_SPLITS = (Q_LORA, KV_LORA, QK_ROPE, SSD_INNER, SSD_INNER, SSD_GROUPS * SSD_STATE, SSD_GROUPS * SSD_STATE,
             2 * SSD_HEADS, D_MODEL, D_MODEL)
IN_OFFSETS = tuple(int(v) for v in np.cumsum((0,) + IN_SPLITS))


def _regroup_kernel(w_ref, o_ref):
    dst = 0

    def put_block(block):
        nonlocal dst
        o_ref[:, dst:dst + 128] = block.T.astype(BF16)
        dst += 128

    def piece(i):
        for off in range(0, IN_SPLITS[i], 128):
            put_block(w_ref[IN_OFFSETS[i] + off:IN_OFFSETS[i] + off + 128, :])

    def padded(parts):
        n = sum(p.shape[0] for p in parts)
        put_block(jnp.concatenate(parts + [jnp.zeros((128 - n, w_ref.shape[1]), F32)], axis=0))

    for i in (3, 8, 9, 5, 6, 4, 0, 1):
        piece(i)
    kr0, n = IN_OFFSETS[2], QK_ROPE // 4
    padded([w_ref[kr0:kr0 + QK_ROPE, :]])
    padded([w_ref[kr0 + blk * n:kr0 + (blk + 1) * n, :] for blk in (1, 0, 3, 2)])
    padded([w_ref[IN_OFFSETS[7]:IN_OFFSETS[7] + IN_SPLITS[7], :]])
    o_ref[:, dst:] = jnp.zeros((o_ref.shape[0], o_ref.shape[1] - dst), BF16)


def _regroup_w_in(w_in_t):
    cols = 256
    n_out = (N_LOW_TILES + N_F32_TILES) * IN_TILE
    return pl.pallas_call(
        _regroup_kernel,
        grid=(D_MODEL // cols,),
        in_specs=[pl.BlockSpec((w_in_t.shape[0], cols), lambda i: (0, i))],
        out_specs=pl.BlockSpec((cols, n_out), lambda i: (i, 0)),
        out_shape=jax.ShapeDtypeStruct((D_MODEL, n_out), BF16),
        compiler_params=_params("arbitrary"),
        name="w_in_regroup",
    )(w_in_t)


N_LOW_TILES, N_F32_TILES = 7, 4
Z_BLK, GM_BLK, GS_BLK = 0, 1, 2
BC_TILE = 6
XS_BLK = 0
MLA_TILE, MISC_TILE = 2, 3
DT_BLK = (MISC_TILE * IN_TILE + 256) // 128
ATTN_HEAD_COLS = 256


IN_OUT_SLOTS = 2


def _in_kernel_resident(x_ref, sh_ref, sc_ref, g_ref, w_ref, cw_ref, cb_ref, lo_hbm, hi_hbm, h_scr, u_scr, lo_stage,
                        hi_stage, sems, *, seq_len):
    r = pl.program_id(0)
    chunk = IN_CHUNK
    n_chunks = ROW_GROUP // chunk
    rows = lambda c: slice(c * chunk, (c + 1) * chunk)
    outputs = ((lo_hbm, lo_stage), (hi_hbm, hi_stage))
    in_flight = [[None] * IN_OUT_SLOTS for _ in outputs]
    n_tiles_out = [0 for _ in outputs]

    tiles_per_output = (N_LOW_TILES, N_F32_TILES)

    def tile_copy(which, slot, col_tile, group):
        hbm, stage = outputs[which]
        row0 = pl.multiple_of(group * ROW_GROUP, ROW_GROUP)
        dst = hbm.at[pl.ds(row0, ROW_GROUP), pl.ds(col_tile * IN_TILE, IN_TILE)]
        return pltpu.make_async_copy(stage.at[slot], dst, sems.at[which, slot])

    def put(val, which, col_tile, c):
        stage = outputs[which][1]
        slot = n_tiles_out[which] % IN_OUT_SLOTS
        if c == 0:
            if in_flight[which][slot] is not None:
                in_flight[which][slot].wait()
            else:
                left = [k for k in range(tiles_per_output[which]) if k % IN_OUT_SLOTS == slot]
                if left:
                    @pl.when(r > 0)
                    def _():
                        tile_copy(which, slot, left[-1], r - 1).wait()
        stage[slot, rows(c), :] = val.astype(stage.dtype)
        if c == n_chunks - 1:
            copy = tile_copy(which, slot, col_tile, r)
            copy.start()
            in_flight[which][slot] = copy
            n_tiles_out[which] += 1

    def pointwise(fn, tile, which, col_tile, stage_h=False):
        w = w_ref[:, tile * IN_TILE:(tile + 1) * IN_TILE]
        for c in range(n_chunks):
            if stage_h:
                h_scr[rows(c), :] = _norm_mod(x_ref[rows(c), :], g_ref, sc_ref, sh_ref)
            put(fn(_dot(h_scr[rows(c), :], w)), which, col_tile, c)

    def conv(tile, conv_tile, which, col_tile):
        w = w_ref[:, tile * IN_TILE:(tile + 1) * IN_TILE]
        cols = pl.ds(conv_tile * IN_TILE, IN_TILE)
        cw, cb = cw_ref.at[:, cols], cb_ref.at[:, cols]

        def conv_out(c):
            put(_silu(_dwconv3_rows(u_scr, c, chunk, slice(0, IN_TILE), cw, cb, seq_len)), which, col_tile, c)
        for c in range(n_chunks):
            u_scr[_stage_rows(c, chunk), :] = _dot(h_scr[rows(c), :], w)
            if c >= 1:
                conv_out(c - 1)
        conv_out(n_chunks - 1)

    _zero_conv_pads(u_scr)
    pointwise(_silu, 0, 0, 0, stage_h=True)
    pointwise(_silu, 1, 0, 1)
    for tile in range(2, 6):
        pointwise(_sigmoid, tile, 0, tile)
    conv(BC_TILE, 2, 0, BC_TILE)
    conv(N_LOW_TILES, 0, 1, 0)
    conv(N_LOW_TILES + 1, 1, 1, 1)
    for tile in range(N_LOW_TILES + MLA_TILE, N_LOW_TILES + N_F32_TILES):
        pointwise(lambda u: u, tile, 1, tile - N_LOW_TILES)
    @pl.when(r == pl.num_programs(0) - 1)
    def _():
        for slots in in_flight:
            for copy in slots:
                if copy is not None:
                    copy.wait()


def _in_proj_resident(x2d, mod48, mod_row, norm_g, w_in_r, conv_w, conv_b, seq_len):
    t = x2d.shape[0]
    once = dict(pipeline_mode=pl.Buffered(1))
    return pl.pallas_call(
        functools.partial(_in_kernel_resident, seq_len=seq_len),
        grid=(t // ROW_GROUP,),
        in_specs=[pl.BlockSpec((ROW_GROUP, D_MODEL), lambda r: (r, 0)),
                  pl.BlockSpec((None, 1, D_MODEL), lambda r: (mod_row(r) * 6 + 0, 0, 0)),
                  pl.BlockSpec((None, 1, D_MODEL), lambda r: (mod_row(r) * 6 + 1, 0, 0)),
                  pl.BlockSpec((1, D_MODEL), lambda r: (0, 0)),
                  pl.BlockSpec(w_in_r.shape, lambda r: (0, 0), **once),
                  pl.BlockSpec(conv_w.shape, lambda r: (0, 0)),
                  pl.BlockSpec(conv_b.shape, lambda r: (0, 0))],
        out_specs=[pl.BlockSpec(memory_space=pl.ANY), pl.BlockSpec(memory_space=pl.ANY)],
        out_shape=[jax.ShapeDtypeStruct((t, N_LOW_TILES * IN_TILE), BF16),
                   jax.ShapeDtypeStruct((t, N_F32_TILES * IN_TILE), F32)],
        scratch_shapes=[pltpu.VMEM((ROW_GROUP, D_MODEL), BF16),
                        pltpu.VMEM((ROW_GROUP + 2 * CONV_PAD, IN_TILE), F32),
                        pltpu.VMEM((IN_OUT_SLOTS, ROW_GROUP, IN_TILE), BF16),
                        pltpu.VMEM((IN_OUT_SLOTS, ROW_GROUP, IN_TILE), F32),
                        pltpu.SemaphoreType.DMA((2, IN_OUT_SLOTS))],
        compiler_params=_params("arbitrary"),
        name="in_proj",
    )(x2d, mod48, mod48, norm_g, w_in_r, conv_w, conv_b)


def _in_kernel(x_ref, sh_ref, sc_ref, g_ref, w_ref, cw_ref, cb_ref, lo_ref, hi_ref, h_scr, u_scr, *, seq_len):
    j = pl.program_id(1)
    chunk = IN_CHUNK
    n_chunks = ROW_GROUP // chunk
    rows = lambda c: slice(c * chunk, (c + 1) * chunk)

    def pointwise(fn, o_ref, stage_h=False):
        for c in range(n_chunks):
            if stage_h:
                h_scr[rows(c), :] = _norm_mod(x_ref[rows(c), :], g_ref, sc_ref, sh_ref)
            o_ref[rows(c), :] = fn(_dot(h_scr[rows(c), :], w_ref[...])).astype(o_ref.dtype)

    def conv(o_ref):
        def conv_out(c):
            v = _silu(_dwconv3_rows(u_scr, c, chunk, slice(0, IN_TILE), cw_ref, cb_ref, seq_len))
            o_ref[rows(c), :] = v.astype(o_ref.dtype)
        for c in range(n_chunks):
            u_scr[_stage_rows(c, chunk), :] = _dot(h_scr[rows(c), :], w_ref[...])
            if c >= 1:
                conv_out(c - 1)
        conv_out(n_chunks - 1)

    @pl.when(j == 0)
    def _():
        _zero_conv_pads(u_scr)
        pointwise(_silu, lo_ref, stage_h=True)

    @pl.when(j == 1)
    def _():
        pointwise(_silu, lo_ref)

    @pl.when((j >= 2) & (j <= 5))
    def _():
        pointwise(_sigmoid, lo_ref)

    @pl.when(j == BC_TILE)
    def _():
        conv(lo_ref)

    @pl.when((j == N_LOW_TILES) | (j == N_LOW_TILES + 1))
    def _():
        conv(hi_ref)

    @pl.when(j >= N_LOW_TILES + MLA_TILE)
    def _():
        pointwise(lambda u: u, hi_ref)


IN_RESIDENT_MIN_GROUPS = 4


def _in_proj(x2d, mod48, mod_row, norm_g, w_in_r, conv_w, conv_b, seq_len):
    t = x2d.shape[0]
    if t // ROW_GROUP >= IN_RESIDENT_MIN_GROUPS:
        return _in_proj_resident(x2d, mod48, mod_row, norm_g, w_in_r, conv_w, conv_b, seq_len)
    n_tiles = N_LOW_TILES + N_F32_TILES
    conv_idx = lambda r, j: (0, jnp.where(j == BC_TILE, 2, jnp.clip(j - N_LOW_TILES, 0, 1)))
    return pl.pallas_call(
        functools.partial(_in_kernel, seq_len=seq_len),
        grid=(t // ROW_GROUP, n_tiles),
        in_specs=[pl.BlockSpec((ROW_GROUP, D_MODEL), lambda r, j: (r, 0)),
                  pl.BlockSpec((None, 1, D_MODEL), lambda r, j: (mod_row(r) * 6 + 0, 0, 0)),
                  pl.BlockSpec((None, 1, D_MODEL), lambda r, j: (mod_row(r) * 6 + 1, 0, 0)),
                  pl.BlockSpec((1, D_MODEL), lambda r, j: (0, 0)),
                  pl.BlockSpec((D_MODEL, IN_TILE), lambda r, j: (0, j)),
                  pl.BlockSpec((3, IN_TILE), conv_idx),
                  pl.BlockSpec((1, IN_TILE), conv_idx)],
        out_specs=[pl.BlockSpec((ROW_GROUP, IN_TILE), lambda r, j: (r, jnp.minimum(j, N_LOW_TILES - 1))),
                   pl.BlockSpec((ROW_GROUP, IN_TILE), lambda r, j: (r, jnp.maximum(j - N_LOW_TILES, 0)))],
        out_shape=[jax.ShapeDtypeStruct((t, N_LOW_TILES * IN_TILE), BF16),
                   jax.ShapeDtypeStruct((t, N_F32_TILES * IN_TILE), F32)],
        scratch_shapes=[pltpu.VMEM((ROW_GROUP, D_MODEL), BF16),
                        pltpu.VMEM((ROW_GROUP + 2 * CONV_PAD, IN_TILE), F32)],
        compiler_params=_params("arbitrary", "arbitrary"),
        name="in_proj",
    )(x2d, mod48, mod48, norm_g, w_in_r, conv_w, conv_b)


def _mla_kernel(*refs, latent, has_ctx, emit_cache, seq_len, tq, seqs):
    n_in = 7 + (2 if latent else 0) + (2 if has_ctx else 0)
    n_out = 3 if emit_cache else 1
    per_seq = {0, 1} | ({n_in - 2, n_in - 1} if has_ctx else set())
    for bi in range(seqs):
        view = lambda r: r.at[pl.ds(bi, 1)]
        ins = [view(r) if i in per_seq else r for i, r in enumerate(refs[:n_in])]
        outs = [view(r) for r in refs[n_in:n_in + n_out]]
        _mla_sequence(*ins, *outs, *refs[n_in + n_out:], latent=latent, has_ctx=has_ctx, emit_cache=emit_cache,
                      seq_len=seq_len, tq=tq)


def _mla_sequence(*refs, latent, has_ctx, emit_cache, seq_len, tq):
    refs = list(refs)
    pm_ref, px_ref, qg_ref, kvg_ref, wuq_ref, wkt_ref, wv_ref = refs[:7]
    del refs[:7]
    if latent:
        cos_ref, sin_ref = refs[:2]
        del refs[:2]
    if has_ctx:
        cckv_ref, ckr_ref = refs[:2]
        del refs[:2]
    o_ref = refs.pop(0)
    if emit_cache:
        ckv_ref, kr_ref = refs[:2]
        del refs[:2]
    k_scr, v_scr = refs[:2]
    del refs[:2]
    if has_ctx:
        kc_scr, vc_scr = refs
    t = pl.program_id(1)
    n_nope = N_HEADS * QK_NOPE

    def put_keys(k_dst, v_dst, rows, ckv_n, kr_bf):
        ckv_bf = ckv_n.astype(BF16)
        v_dst[rows, :] = _dot(ckv_bf, wv_ref[...]).astype(BF16)
        kn_t = _dot_nt(wkt_ref[...], ckv_bf).astype(BF16)
        width = kr_bf.shape[1]
        eye = jnp.where(lax.broadcasted_iota(jnp.int32, (128, width), 0)
                        == lax.broadcasted_iota(jnp.int32, (128, width), 1), 1.0, 0.0).astype(BF16)
        kr_t = _dot_nt(eye, kr_bf).astype(BF16)
        for h in range(N_HEADS):
            base = h * ATTN_HEAD_COLS
            k_dst[base:base + QK_NOPE, rows] = kn_t[h * QK_NOPE:(h + 1) * QK_NOPE, :]
            k_dst[base + QK_NOPE:base + ATTN_HEAD_COLS, rows] = kr_t

    @pl.when(t == 0)
    def _():
        step = min(seq_len, TOKEN_TILE)
        for r0 in range(0, seq_len, step):
            rows = slice(r0, r0 + step)
            ckv_n = _rmsnorm(pm_ref[0, rows, Q_LORA:], kvg_ref[...])
            kr = px_ref[0, rows, 0:128]
            if latent:
                kr = kr * cos_ref[rows, :] + px_ref[0, rows, 128:256] * sin_ref[rows, :]
            if emit_cache:
                ckv_ref[0, rows, :] = ckv_n
                kr_ref[0, rows, :] = kr[:, :QK_ROPE]
            put_keys(k_scr, v_scr, rows, ckv_n, kr.astype(BF16))
        if has_ctx:
            past = cckv_ref.shape[1]
            put_keys(kc_scr, vc_scr, slice(0, past), cckv_ref[0], ckr_ref[0].astype(BF16))

    qrows = pl.ds(pl.multiple_of(t * tq, tq), tq)
    scale = LOG2_E / math.sqrt(QK_NOPE + QK_ROPE)
    cqn = _rmsnorm(pm_ref[0, qrows, 0:Q_LORA], qg_ref[...]).astype(BF16)
    q = _dot(cqn, wuq_ref[...])
    q_rope = q[:, n_nope:2 * n_nope]
    if latent:
        q_rope = (q_rope * jnp.concatenate([cos_ref[qrows, :]] * N_HEADS, axis=1)
                  + q[:, 2 * n_nope:3 * n_nope] * jnp.concatenate([sin_ref[qrows, :]] * N_HEADS, axis=1))
    segs = ([(kc_scr, vc_scr)] if has_ctx else []) + [(k_scr, v_scr)]
    for h in range(N_HEADS):
        head = slice(h * QK_NOPE, (h + 1) * QK_NOPE)
        qk = slice(h * ATTN_HEAD_COLS, (h + 1) * ATTN_HEAD_COLS)
        qh = (jnp.concatenate([q[:, head], q_rope[:, head]], axis=1) * scale).astype(BF16)
        s = [_dot(qh, k[qk, :]) for k, _ in segs]
        m = functools.reduce(jnp.maximum, [jnp.max(si, axis=-1, keepdims=True) for si in s])
        p = [jnp.exp2(si - m) for si in s]
        l = functools.reduce(jnp.add, [jnp.sum(pi, axis=-1, keepdims=True) for pi in p])
        o = functools.reduce(jnp.add, [_dot(pi.astype(BF16), v[:, head]) for pi, (_, v) in zip(p, segs)])
        o_ref[0, :, head] = (o / l).astype(BF16)


def _mla(proj_hi, q_norm_g, kv_norm_g, w_uq_r, w_uk_t, w_uv, rope_tables, ctx, emit_cache):
    b, l, _ = proj_hi.shape
    tq = min(l, ATTN_Q_TILE)
    latent = rope_tables is not None
    has_ctx = ctx is not None
    once = dict(pipeline_mode=pl.Buffered(1))
    per_batch = once if l // tq > 1 else {}
    seqs = MLA_SEQS_PER_STEP if (l == tq and b % MLA_SEQS_PER_STEP == 0) else 1
    const2 = lambda i, t: (0, 0)
    in_specs = [pl.BlockSpec((seqs, l, IN_TILE), lambda i, t: (i, 0, MLA_TILE), **per_batch),
                pl.BlockSpec((seqs, l, 256), lambda i, t: (i, 0, MISC_TILE * IN_TILE // 256), **per_batch),
                pl.BlockSpec((1, Q_LORA), const2),
                pl.BlockSpec((1, KV_LORA), const2),
                pl.BlockSpec(w_uq_r.shape, const2, **once),
                pl.BlockSpec(w_uk_t.shape, const2, **once),
                pl.BlockSpec(w_uv.shape, const2, **once)]
    args = [proj_hi, proj_hi, q_norm_g, kv_norm_g, w_uq_r, w_uk_t, w_uv]
    scratch = [pltpu.VMEM((N_HEADS * ATTN_HEAD_COLS, l), BF16), pltpu.VMEM((l, N_HEADS * V_HEAD), BF16)]
    if latent:
        in_specs += [pl.BlockSpec((l, 128), const2, **once)] * 2
        args += list(rope_tables)
    if has_ctx:
        past = ctx[0].shape[1]
        in_specs += [pl.BlockSpec((seqs, past, KV_LORA), lambda i, t: (i, 0, 0), **per_batch),
                     pl.BlockSpec((seqs, past, QK_ROPE), lambda i, t: (i, 0, 0), **per_batch)]
        args += list(ctx)
        scratch += [pltpu.VMEM((N_HEADS * ATTN_HEAD_COLS, past), BF16), pltpu.VMEM((past, N_HEADS * V_HEAD), BF16)]
    out_specs = [pl.BlockSpec((seqs, tq, N_HEADS * V_HEAD), lambda i, t: (i, t, 0))]
    out_shape = [jax.ShapeDtypeStruct((b, l, N_HEADS * V_HEAD), BF16)]
    if emit_cache:
        out_specs += [pl.BlockSpec((seqs, l, KV_LORA), lambda i, t: (i, 0, 0)),
                      pl.BlockSpec((seqs, l, QK_ROPE), lambda i, t: (i, 0, 0))]
        out_shape += [jax.ShapeDtypeStruct((b, l, KV_LORA), F32), jax.ShapeDtypeStruct((b, l, QK_ROPE), F32)]
    return pl.pallas_call(
        functools.partial(_mla_kernel, latent=latent, has_ctx=has_ctx, emit_cache=emit_cache, seq_len=l, tq=tq,
                          seqs=seqs),
        grid=(b // seqs, l // tq),
        in_specs=in_specs,
        out_specs=out_specs,
        out_shape=out_shape,
        scratch_shapes=scratch,
        compiler_params=_params("arbitrary", "arbitrary", vmem_limit=MLA_VMEM_LIMIT),
        name="mla_attention",
    )(*args)


def _split3(x):
    hi = x.astype(BF16)
    r = x - hi.astype(F32)
    mid = r.astype(BF16)
    lo = (r - mid.astype(F32)).astype(BF16)
    return hi, mid, lo


def _exact_dot(parts, sel):
    return functools.reduce(jnp.add, [_dot(p, sel) for p in parts])


def _exact_dot_rows(sel, parts):
    return functools.reduce(jnp.add, [_dot(sel, p) for p in parts])


HEADS_PER_GROUP = SSD_HEADS // SSD_GROUPS
GROUP_COLS = HEADS_PER_GROUP * SSD_HEADDIM


def _ssd_kernel(*refs, nc, cps, has_h0, seqs):
    n_in = 8 if has_h0 else 7
    per_seq = {0, 1, 2, 3, 7} if has_h0 else {0, 1, 2, 3}
    for bi in range(seqs):
        view = lambda r: r.at[pl.ds(bi, 1)]
        ins = [view(r) if i in per_seq else r for i, r in enumerate(refs[:n_in])]
        outs = [view(r) for r in refs[n_in:n_in + 2]]
        _ssd_sequence(*ins, *outs, *refs[n_in + 2:], nc=nc, cps=cps, has_h0=has_h0)


def _ssd_sequence(*refs, nc, cps, has_h0):
    if has_h0:
        (xs_ref, zs_ref, bc_ref, dt_ref, dtb_ref, a_ref, dx_ref, h0_ref, y_ref, ht_ref,
         yl_scr, acum_scr, src_scr, tot_scr, sb_scr, h_scr, esel_scr) = refs
    else:
        (xs_ref, zs_ref, bc_ref, dt_ref, dtb_ref, a_ref, dx_ref, y_ref, ht_ref,
         yl_scr, acum_scr, src_scr, tot_scr, sb_scr, h_scr, esel_scr) = refs
    s = pl.program_id(1)
    q = SSD_CHUNK
    n_bc = SSD_GROUPS * SSD_STATE
    lane = lax.broadcasted_iota(jnp.int32, (q, 128), 1)
    low_half = lane < SSD_HEADDIM
    ii = lax.broadcasted_iota(jnp.int32, (q, q), 0)
    jj = lax.broadcasted_iota(jnp.int32, (q, q), 1)
    lower, upper = ii >= jj, ii <= jj

    def lane_bcast(parts, d):
        return _exact_dot(parts, esel_scr[d])

    def stacked_states(d):
        return [jnp.concatenate([h_scr[d, 2 * i], h_scr[d, 2 * i + 1]], axis=0).astype(BF16)
                for i in range(SSD_GROUPS // 2)]

    def group_c(bc, g):
        i, r = divmod(g, 2)
        cpair = bc[:, n_bc + i * 128:n_bc + (i + 1) * 128]
        return jnp.where(low_half if r == 0 else ~low_half, cpair, 0.0).astype(BF16)

    @pl.when(s == 0)
    def _():
        k = lax.broadcasted_iota(jnp.int32, (128, SSD_INNER), 0)
        head = lax.broadcasted_iota(jnp.int32, (128, SSD_INNER), 1) // SSD_HEADDIM
        for d in range(2):
            esel_scr[d] = jnp.where(k == d * SSD_HEADS + head, 1.0, 0.0).astype(BF16)
        if has_h0:
            for d in range(2):
                for g in range(SSD_GROUPS):
                    hpn = h0_ref[0, d, g * HEADS_PER_GROUP:(g + 1) * HEADS_PER_GROUP].reshape(GROUP_COLS, SSD_STATE)
                    h_scr[d, g] = hpn.T
        else:
            h_scr[...] = jnp.zeros(h_scr.shape, F32)
        tri_f = jnp.where(lower, 1.0, 0.0).astype(BF16)
        tri_b = jnp.where(upper, 1.0, 0.0).astype(BF16)
        fwd_col = lane < SSD_HEADS
        for c in range(nc):
            crow = slice(c * q, (c + 1) * q)
            dt = _softplus(dt_ref[0, crow, :] + dtb_ref[...])
            parts = _split3(dt * a_ref[...])
            acum = jnp.where(fwd_col, _exact_dot_rows(tri_f, parts),
                             _exact_dot_rows(tri_b, parts))
            acum = acum * LOG2_E
            acum_scr[crow, :] = acum
            tot = jnp.where(fwd_col[0:1], acum[q - 1:q, :], acum[0:1, :])
            tot_scr[c] = jnp.broadcast_to(tot, (8, 128))
            src_scr[c] = (acum - jnp.log2(dt)).T

    def first_sweep(c, blk):
        rows = pl.ds(pl.multiple_of(c * q, q), q)
        x = xs_ref[0, blk, :]
        bc = bc_ref[0, blk, :].astype(F32)
        acum = acum_scr[rows, :]
        src_t = src_scr[c]
        tot8 = tot_scr[c]
        e = jnp.exp2(acum)
        e_hi = e.astype(BF16)
        eb_f = lane_bcast([e_hi, (e - e_hi.astype(F32)).astype(BF16)], 0)
        cd_f = jnp.exp2(lane_bcast(_split3(tot8), 0))[0:1]
        b_t = [bc[:, i * 128:(i + 1) * 128].T for i in range(SSD_GROUPS // 2)]
        h_in = stacked_states(0)
        for g in range(SSD_GROUPS):
            i, r = divmod(g, 2)
            cm = group_c(bc, g)
            scores = _dot_nt(cm, bc[:, i * 128:(i + 1) * 128].astype(BF16))
            bg_t = b_t[i][r * SSD_STATE:(r + 1) * SSD_STATE, :]
            gcols = slice(g * GROUP_COLS, (g + 1) * GROUP_COLS)
            y_off = _dot(cm, h_in[i]) * eb_f[:, gcols]
            for t in range(HEADS_PER_GROUP // 2):
                pc = slice((2 * g + t) * 128, (2 * g + t + 1) * 128)
                xp = x[:, pc]
                x2 = jnp.concatenate([jnp.where(low_half, xp, 0.0), jnp.where(low_half, 0.0, xp)], axis=0).astype(BF16)
                m, sf, sb = [], [], []
                for u in range(2):
                    kf = g * HEADS_PER_GROUP + 2 * t + u
                    kb = SSD_HEADS + kf
                    af_col = jnp.broadcast_to(acum[:, kf:kf + 1], (q, q))
                    ab_col = jnp.broadcast_to(acum[:, kb:kb + 1], (q, q))
                    sf_row, sb_row = src_t[kf:kf + 1, :], src_t[kb:kb + 1, :]
                    decay = (jnp.exp2(jnp.where(lower, af_col - sf_row, NEG_BIG))
                             + jnp.exp2(jnp.where(upper, ab_col - sb_row, NEG_BIG)))
                    m.append((scores * decay).astype(BF16))
                    wf = jnp.exp2(tot8[0:1, kf:kf + 1] - sf_row)
                    wb = jnp.exp2(tot8[0:1, kb:kb + 1] - sb_row)
                    sf.append((bg_t * wf).astype(BF16))
                    sb.append((bg_t * wb).astype(BF16))
                y_pair = _dot(jnp.concatenate(m, axis=1), x2)
                tc = slice(t * 128, (t + 1) * 128)
                yl_scr[rows, pc] = y_pair + y_off[:, tc] + dx_ref[:, pc] * xp
                h_scr[0, g, :, tc] = h_scr[0, g, :, tc] * cd_f[:, pc] + _dot(jnp.concatenate(sf, axis=1), x2)
                sb_scr[c, g, :, tc] = _dot(jnp.concatenate(sb, axis=1), x2)

    def second_sweep(c, blk):
        rows = pl.ds(pl.multiple_of(c * q, q), q)
        bc = bc_ref[0, blk, :].astype(F32)
        e = jnp.exp2(acum_scr[rows, :])
        e_hi = e.astype(BF16)
        eb_b = lane_bcast([e_hi, (e - e_hi.astype(F32)).astype(BF16)], 1)
        cd_b = jnp.exp2(lane_bcast(_split3(tot_scr[c]), 1))[0:1]
        h_in = stacked_states(1)
        for g in range(SSD_GROUPS):
            gcols = slice(g * GROUP_COLS, (g + 1) * GROUP_COLS)
            y_off = _dot(group_c(bc, g), h_in[g // 2]) * eb_b[:, gcols]
            y_ref[0, blk, gcols] = ((yl_scr[rows, gcols] + y_off) * zs_ref[0, blk, gcols].astype(F32)).astype(y_ref.dtype)
            h_scr[1, g] = h_scr[1, g] * cd_b[:, gcols] + sb_scr[c, g]

    n_steps = nc // cps
    one_step = n_steps == 1

    @pl.when(s < n_steps)
    def _():
        for ci in range(cps):
            first_sweep(s * cps + ci, slice(ci * q, (ci + 1) * q))

    @pl.when(jnp.logical_or(one_step, s >= n_steps))
    def _():
        blk_id = s * 0 if one_step else 2 * n_steps - 1 - s
        for ci in reversed(range(cps)):
            second_sweep(blk_id * cps + ci, slice(ci * q, (ci + 1) * q))

    @pl.when(jnp.logical_or(one_step, s == 2 * n_steps - 1))
    def _():
        for d in range(2):
            for g in range(SSD_GROUPS):
                ht_ref[0, d, g * HEADS_PER_GROUP:(g + 1) * HEADS_PER_GROUP] = h_scr[d, g].T.reshape(
                    HEADS_PER_GROUP, SSD_HEADDIM, SSD_STATE)


def _ssd(proj_lo, proj_hi, h0, dt_bias128, a128, d_exp):
    b, l, _ = proj_hi.shape
    q = SSD_CHUNK
    nc = l // q
    cps = min(nc, SSD_CHUNKS_PER_STEP)
    n_steps = nc // cps
    n_grid = 1 if n_steps == 1 else 2 * n_steps
    rows = cps * q
    early = lambda s: jnp.minimum(s, n_steps - 1)
    both = lambda s: jnp.where(s < n_steps, s, 2 * n_steps - 1 - s)
    late = lambda s: jnp.where(s < n_steps, n_steps - 1, 2 * n_steps - 1 - s)
    seqs = SSD_SEQS_PER_STEP if (n_steps == 1 and b % SSD_SEQS_PER_STEP == 0) else 1
    st_shape = (seqs, 2, SSD_HEADS, SSD_HEADDIM, SSD_STATE)
    st_spec = pl.BlockSpec(st_shape, lambda i, s: (i, 0, 0, 0, 0))
    has_h0 = h0 is not None
    return pl.pallas_call(
        functools.partial(_ssd_kernel, nc=nc, cps=cps, has_h0=has_h0, seqs=seqs),
        grid=(b // seqs, n_grid),
        in_specs=[pl.BlockSpec((seqs, rows, SSD_INNER), lambda i, s: (i, early(s), XS_BLK)),
                  pl.BlockSpec((seqs, rows, SSD_INNER), lambda i, s: (i, late(s), Z_BLK)),
                  pl.BlockSpec((seqs, rows, IN_TILE), lambda i, s: (i, both(s), BC_TILE)),
                  pl.BlockSpec((seqs, l, 128), lambda i, s: (i, 0, DT_BLK)),
                  pl.BlockSpec((1, 128), lambda i, s: (0, 0)),
                  pl.BlockSpec((1, 128), lambda i, s: (0, 0)),
                  pl.BlockSpec((1, SSD_INNER), lambda i, s: (0, 0))] + ([st_spec] if has_h0 else []),
        out_specs=[pl.BlockSpec((seqs, rows, SSD_INNER), lambda i, s: (i, late(s), 0)),
                   pl.BlockSpec(st_shape, lambda i, s: (i, 0, 0, 0, 0))],
        out_shape=[jax.ShapeDtypeStruct((b, l, SSD_INNER), BF16),
                   jax.ShapeDtypeStruct((b,) + st_shape[1:], F32)],
        scratch_shapes=[pltpu.VMEM((l, SSD_INNER), F32),
                        pltpu.VMEM((l, 128), F32),
                        pltpu.VMEM((nc, 128, q), F32),
                        pltpu.VMEM((nc, 8, 128), F32),
                        pltpu.VMEM((nc, SSD_GROUPS, SSD_STATE, GROUP_COLS), F32),
                        pltpu.VMEM((2, SSD_GROUPS, SSD_STATE, GROUP_COLS), F32),
                        pltpu.VMEM((2, 128, SSD_INNER), BF16)],
        compiler_params=_params("arbitrary", "arbitrary"),
        name="ssd_scan",
    )(proj_hi, proj_lo, proj_lo, proj_hi, dt_bias128, a128, d_exp, *([h0] if has_h0 else []))


def _merge_kernel(attn_ref, yz_ref, gm_ref, gs_ref, x_ref, g1_ref, ng_ref, womla_ref, wossd_ref, wout_ref, o_ref, w_scr):
    @pl.when(pl.program_id(0) == 0)
    def _():
        w_scr[0] = womla_ref[...].astype(BF16)
        w_scr[1] = wossd_ref[...].astype(BF16)
        w_scr[2] = wout_ref[...].astype(BF16)

    o_mla = _dot(attn_ref[...], w_scr[0])
    o_ssd = _dot(_rmsnorm(yz_ref[...].astype(F32), ng_ref[...]).astype(BF16), w_scr[1])
    merged = gm_ref[...].astype(F32) * o_mla + gs_ref[...].astype(F32) * o_ssd
    o_ref[...] = x_ref[...] + g1_ref[...] * _dot(merged.astype(BF16), w_scr[2])


def _merge(attn2d, yz2d, proj, x2d, mod48, mod_row, ssd_norm_g, w_o_mla, w_o_ssd, w_out):
    t = x2d.shape[0]
    tm = TOKEN_TILE
    row = lambda i: (i, 0)
    const = lambda i: (0, 0)
    wspec = pl.BlockSpec((D_MODEL, D_MODEL), const, pipeline_mode=pl.Buffered(1))
    return pl.pallas_call(
        _merge_kernel,
        grid=(t // tm,),
        in_specs=[pl.BlockSpec((tm, D_MODEL), row),
                  pl.BlockSpec((tm, D_MODEL), row),
                  pl.BlockSpec((tm, D_MODEL), lambda i: (i, GM_BLK)),
                  pl.BlockSpec((tm, D_MODEL), lambda i: (i, GS_BLK)),
                  pl.BlockSpec((tm, D_MODEL), row),
                  pl.BlockSpec((None, 1, D_MODEL), lambda i: (mod_row(i * tm // ROW_GROUP) * 6 + 2, 0, 0)),
                  pl.BlockSpec((1, D_MODEL), const),
                  wspec, wspec, wspec],
        out_specs=pl.BlockSpec((tm, D_MODEL), row),
        out_shape=jax.ShapeDtypeStruct((t, D_MODEL), F32),
        scratch_shapes=[pltpu.VMEM((3, D_MODEL, D_MODEL), BF16)],
        compiler_params=_params("arbitrary"),
        name="merge_out",
    )(attn2d, yz2d, proj, proj, x2d, mod48, ssd_norm_g, w_o_mla, w_o_ssd, w_out)


def _ffn_kernel(x_ref, sh_ref, sc_ref, g2_ref, ng_ref, wg_ref, wv_ref, cwg_ref, cwv_ref, cbg_ref, cbv_ref, wd_ref,
                fg_ref, o_ref, h_scr, wup_scr, wd_scr, u_scr, *, seq_len):
    j = pl.program_id(1)
    chunk = FFN_CHUNK
    n_chunks = ROW_GROUP // chunk
    rows = lambda c: slice(c * chunk, (c + 1) * chunk)

    wd_scr[...] = wd_ref[...].astype(BF16)

    def first_up(h_rows):
        n_k = 4
        acc = None
        for k in range(n_k):
            ks = slice(k * (D_MODEL // n_k), (k + 1) * (D_MODEL // n_k))
            wup_scr[ks, 0:FFN_TILE] = wg_ref[ks, :].astype(BF16)
            wup_scr[ks, FFN_TILE:2 * FFN_TILE] = wv_ref[ks, :].astype(BF16)
            part = _dot(h_scr[h_rows, ks], wup_scr[ks, :])
            acc = part if acc is None else acc + part
        return acc

    def gated(c):
        ug = _dwconv3_rows(u_scr, c, chunk, slice(0, FFN_TILE), cwg_ref, cbg_ref, seq_len)
        uv = _dwconv3_rows(u_scr, c, chunk, slice(FFN_TILE, 2 * FFN_TILE), cwv_ref, cbv_ref, seq_len)
        return (_silu(ug) * uv).astype(BF16)

    def pipeline(first, last):
        act = {}
        lag = FFN_DOWN_LAG
        for c in range(n_chunks + lag):
            if c < n_chunks:
                if first:
                    h_scr[rows(c), :] = _norm_mod(x_ref[rows(c), :], ng_ref, sc_ref, sh_ref)
                up = first_up(rows(c)) if c == 0 else _dot(h_scr[rows(c), :], wup_scr[...])
                u_scr[_stage_rows(c, chunk), :] = up
            if c >= lag:
                r = rows(c - lag)
                acc = _dot(act.pop(c - lag), wd_scr[...])
                if not first:
                    acc = o_ref[r, :] + acc
                if last:
                    acc = _rmsnorm(x_ref[r, :] + g2_ref[...] * acc, fg_ref[...])
                o_ref[r, :] = acc
            if 1 <= c <= n_chunks:
                act[c - 1] = gated(c - 1)

    last_j = pl.num_programs(1) - 1

    @pl.when(j == 0)
    def _():
        _zero_conv_pads(u_scr)
        pipeline(True, False)

    @pl.when((j > 0) & (j < last_j))
    def _():
        pipeline(False, False)

    @pl.when(j == last_j)
    def _():
        pipeline(False, True)


def _ffn(x2d, mod48, mod_row, norm_g, w_up, conv_w, conv_b, w_down, final_g, seq_len):
    t = x2d.shape[0]
    nj = D_FF // FFN_TILE
    gate = lambda r, j: (0, j)
    val = lambda r, j: (0, nj + j)
    const = lambda r, j: (0, 0)
    mod = lambda k: pl.BlockSpec((None, 1, D_MODEL), lambda r, j: (mod_row(r) * 6 + k, 0, 0))
    return pl.pallas_call(
        functools.partial(_ffn_kernel, seq_len=seq_len),
        grid=(t // ROW_GROUP, nj),
        in_specs=[pl.BlockSpec((ROW_GROUP, D_MODEL), lambda r, j: (r, 0)),
                  mod(3), mod(4), mod(5),
                  pl.BlockSpec((1, D_MODEL), const),
                  pl.BlockSpec((D_MODEL, FFN_TILE), gate),
                  pl.BlockSpec((D_MODEL, FFN_TILE), val),
                  pl.BlockSpec((3, FFN_TILE), gate),
                  pl.BlockSpec((3, FFN_TILE), val),
                  pl.BlockSpec((1, FFN_TILE), gate),
                  pl.BlockSpec((1, FFN_TILE), val),
                  pl.BlockSpec((FFN_TILE, D_MODEL), lambda r, j: (j, 0)),
                  pl.BlockSpec((1, D_MODEL), const)],
        out_specs=pl.BlockSpec((ROW_GROUP, D_MODEL), lambda r, j: (r, 0)),
        out_shape=jax.ShapeDtypeStruct((t, D_MODEL), F32),
        scratch_shapes=[pltpu.VMEM((ROW_GROUP, D_MODEL), BF16),
                        pltpu.VMEM((D_MODEL, 2 * FFN_TILE), BF16),
                        pltpu.VMEM((FFN_TILE, D_MODEL), BF16),
                        pltpu.VMEM((ROW_GROUP + 2 * CONV_PAD, 2 * FFN_TILE), F32)],
        compiler_params=_params("arbitrary", "arbitrary"),
        name="conv_ffn",
    )(x2d, mod48, mod48, mod48, norm_g, w_up, w_up, conv_w, conv_w, conv_b, conv_b, w_down, final_g)


def _rope_tables(seq_len):
    t = np.arange(seq_len)
    row = (t // GRID_W).astype(np.float32)
    col = (t % GRID_W).astype(np.float32)
    n = QK_ROPE // 4
    inv = (np.float32(ROPE_BASE) ** (-np.arange(n, dtype=np.float32) / np.float32(n))).astype(np.float32)
    ar, ac = row[:, None] * inv, col[:, None] * inv
    cos64 = np.concatenate([np.cos(ar), np.cos(ar), np.cos(ac), np.cos(ac)], axis=1)
    sin64 = np.concatenate([-np.sin(ar), np.sin(ar), -np.sin(ac), np.sin(ac)], axis=1)
    zeros = np.zeros_like(cos64)
    return (jnp.asarray(np.concatenate([cos64, zeros], axis=1), F32),
            jnp.asarray(np.concatenate([sin64, zeros], axis=1), F32))


def _swap_rope_halves(w):
    lead = w.shape[:-1]
    return w.reshape(lead + (2, 2, QK_ROPE // 4))[..., ::-1, :].reshape(lead + (QK_ROPE,))


def _trunk_pass(x, mod48, mod_row, wts, ctx, latent):
    b, l, _ = x.shape
    x2d = x.reshape(b * l, D_MODEL)
    proj_lo, proj_hi = _in_proj(x2d, mod48, mod_row, wts["norm_attn_g"], wts["w_in_r"], wts["ssd_conv_w"],
                                wts["ssd_conv_b"], l)
    rope = _rope_tables(l) if latent else None
    w_uq_r = wts["w_uq_lat"] if latent else wts["w_uq_ctx"]
    shape3 = lambda a, n: a.reshape(b, n, a.shape[-1])
    h0 = None
    mla_ctx = None
    if ctx is not None:
        cache_ckv, cache_krope, h0 = ctx
        mla_ctx = (cache_ckv, cache_krope)
    emit_cache = ctx is None
    mla_out = _mla(shape3(proj_hi, l), wts["q_norm_g"], wts["kv_norm_g"], w_uq_r, wts["w_uk_t"], wts["w_uv"],
                   rope, mla_ctx, emit_cache)
    attn, ckv_n, kr3 = mla_out if emit_cache else (mla_out[0], None, None)
    yz, h_t = _ssd(shape3(proj_lo, l), shape3(proj_hi, l), h0, wts["dt_bias128"], wts["a128"], wts["d_exp"])
    x1 = _merge(attn.reshape(b * l, -1), yz.reshape(b * l, -1), proj_lo, x2d, mod48, mod_row, wts["ssd_norm_g"],
                wts["w_o_mla"], wts["w_o_ssd"], wts["w_out"])
    y = _ffn(x1, mod48, mod_row, wts["norm_ffn_g"], wts["w_up"], wts["ffn_conv_w"], wts["ffn_conv_b"], wts["w_down"],
             wts["final_norm_g"], l)
    return y.reshape(b, l, D_MODEL), ckv_n, kr3, h_t


def kernel(x_prompt, x_sample, c, cache_ckv, cache_krope, state_ssd, c_ctx, w_ada, b_ada, norm_attn_g, w_in, q_norm_g,
           kv_norm_g, w_uq, w_ukv, w_o_mla, ssd_conv_w, ssd_conv_b, ssd_dt_bias, ssd_A_log, ssd_D, ssd_norm_g, w_o_ssd,
           w_out, norm_ffn_g, w_up, ffn_conv_w, ffn_conv_b, w_down, final_norm_g):
    depth = w_in.shape[0]
    assert depth == 1, "single trunk layer"
    dec_b = x_sample.shape[0]
    assert x_sample.shape[1] == ROW_GROUP and ROW_GROUP % x_prompt.shape[1] == 0
    lyr = 0

    cvec = jnp.zeros((8, D_MODEL), F32).at[0].set(c_ctx).at[1:1 + dec_b].set(c)
    mod48 = _ada(cvec, w_ada[lyr], b_ada[lyr]).reshape(8 * 6, 1, D_MODEL)

    w_in_r = _regroup_w_in(w_in[lyr].T)
    wq = w_uq[lyr].reshape(Q_LORA, N_HEADS, QK_NOPE + QK_ROPE)
    wq_nope = wq[:, :, :QK_NOPE].reshape(Q_LORA, -1)
    wq_rope = wq[:, :, QK_NOPE:]
    pad_rope = lambda w: jnp.pad(w, ((0, 0), (0, 0), (0, 128 - QK_ROPE))).reshape(Q_LORA, -1)
    w_uq_ctx = jnp.concatenate([wq_nope, pad_rope(wq_rope)], axis=1).astype(BF16)
    w_uq_lat = jnp.concatenate([wq_nope, pad_rope(wq_rope), pad_rope(_swap_rope_halves(wq_rope))], axis=1).astype(BF16)
    wkv = w_ukv[lyr].reshape(KV_LORA, N_HEADS, QK_NOPE + V_HEAD)
    w_uk_t = wkv[:, :, :QK_NOPE].reshape(KV_LORA, -1).T.astype(BF16)
    w_uv = wkv[:, :, QK_NOPE:].reshape(KV_LORA, -1).astype(BF16)
    pad128 = lambda a: jnp.pad(a.reshape(1, -1), ((0, 0), (0, 128 - a.size)))
    wts = {
        "norm_attn_g": norm_attn_g[lyr].reshape(1, -1), "w_in_r": w_in_r,
        "ssd_conv_w": ssd_conv_w[lyr], "ssd_conv_b": ssd_conv_b[lyr].reshape(1, -1),
        "q_norm_g": q_norm_g[lyr].reshape(1, -1), "kv_norm_g": kv_norm_g[lyr].reshape(1, -1),
        "w_uq_ctx": w_uq_ctx, "w_uq_lat": w_uq_lat, "w_uk_t": w_uk_t, "w_uv": w_uv,
        "dt_bias128": pad128(ssd_dt_bias[lyr]), "a128": pad128(-jnp.exp(ssd_A_log[lyr])),
        "d_exp": jnp.repeat(ssd_D[lyr], SSD_HEADDIM).reshape(1, -1),
        "ssd_norm_g": ssd_norm_g[lyr].reshape(1, -1),
        "w_o_mla": w_o_mla[lyr], "w_o_ssd": w_o_ssd[lyr], "w_out": w_out[lyr],
        "norm_ffn_g": norm_ffn_g[lyr].reshape(1, -1), "w_up": w_up[lyr],
        "ffn_conv_w": ffn_conv_w[lyr], "ffn_conv_b": ffn_conv_b[lyr].reshape(1, -1),
        "w_down": w_down[lyr], "final_norm_g": final_norm_g.reshape(1, -1),
    }

    y_p, ckv_p, kr_p, st_p = _trunk_pass(x_prompt, mod48, lambda r: 0, wts, None, False)
    ctx = (cache_ckv[:, lyr], cache_krope[:, lyr], state_ssd[:, lyr])
    y_s, _, _, _ = _trunk_pass(x_sample, mod48, lambda r: 1 + r, wts, ctx, True)
    return y_p, y_s, ckv_p[:, None], kr_p[:, None], st_p[:, None]
```
